```python
import jax, jax.numpy as jnp
from jax import lax
import numpy as np

D_MODEL = 2048
BATCH = 8
SEQ = 4096
DEPTH = 1

CHUNK = 64
MIX_WIDTH = D_MODEL
HG_HEADS = 8
HG_HEAD_DIM = 128
HG_WIDTH = HG_HEADS * HG_HEAD_DIM
ATT_HEADS = 8
ATT_HEAD_DIM = 128
ATT_WIDTH = ATT_HEADS * ATT_HEAD_DIM
LEFT_CHUNKS = 8
BAND = (LEFT_CHUNKS + 1) * CHUNK
REL_CLIP = 128
D_FF = 5632
CONV_WIDTH = 3
PLE_DIM = 256
EPS = 1e-6
IN_SPLITS = (HG_WIDTH, HG_WIDTH, HG_WIDTH, HG_WIDTH, ATT_WIDTH, ATT_WIDTH, ATT_WIDTH)
IN_COLS = sum(IN_SPLITS)

kernel_name = "hybrid_hgrn2_chunkattn_convffn_ple"


def rmsnorm(x, g):
    xf = x.astype(jnp.float32)
    y = xf * lax.rsqrt(jnp.mean(xf * xf, axis=-1, keepdims=True) + EPS)
    return (y * g.astype(jnp.float32)).astype(x.dtype)


def hgrn2_mixer(q, f_pre, i_in, g, lb, norm_g):
    B, S, _ = q.shape
    nc = S // CHUNK
    lb = lb.astype(jnp.float32)
    sig = jax.nn.sigmoid(f_pre.astype(jnp.float32))
    f = lb + (1.0 - lb) * sig
    log_f = jnp.log(f)
    k = (1.0 - lb) * jax.nn.sigmoid(-f_pre.astype(jnp.float32))
    qf = jax.nn.silu(q.astype(jnp.float32))
    vf = i_in.astype(jnp.float32)

    def heads(t):
        return t.reshape(B, nc, CHUNK, HG_HEADS, HG_HEAD_DIM).transpose(1, 0, 3, 2, 4)

    causal = jnp.tril(jnp.ones((CHUNK, CHUNK), dtype=bool))

    def step(state, inp):
        qc, kc, vc, lfc = inp
        b = jnp.cumsum(lfc, axis=2)
        diff = b[:, :, :, None, :] - b[:, :, None, :, :]
        decay = jnp.exp(jnp.where(causal[None, None, :, :, None], diff, -jnp.inf))
        scores = jnp.einsum('bhtd,bhsd,bhtsd->bhts', qc, kc, decay)
        o_intra = jnp.einsum('bhts,bhsv->bhtv', scores, vc)
        o_inter = jnp.einsum('bhtd,bhdv->bhtv', qc * jnp.exp(b), state)
        b_last = b[:, :, -1:, :]
        new_state = (jnp.exp(b_last[:, :, 0, :])[..., None] * state
                     + jnp.einsum('bhsd,bhsv->bhdv', kc * jnp.exp(b_last - b), vc))
        return new_state, o_intra + o_inter

    s0 = jnp.zeros((B, HG_HEADS, HG_HEAD_DIM, HG_HEAD_DIM), jnp.float32)
    _, o = lax.scan(step, s0, (heads(qf), heads(k), heads(vf), heads(log_f)))
    o = o.transpose(1, 0, 3, 2, 4).reshape(B, S, HG_HEADS, HG_HEAD_DIM)
    o = rmsnorm(o, norm_g)
    gate = jax.nn.silu(g.astype(jnp.float32)).reshape(B, S, HG_HEADS, HG_HEAD_DIM)
    return (o * gate).reshape(B, S, HG_WIDTH).astype(q.dtype)


def chunk_attention(q, k, v, rel_bias):
    B, S, _ = q.shape
    nc = S // CHUNK
    pad = LEFT_CHUNKS * CHUNK

    def heads(t):
        return t.reshape(B, S, ATT_HEADS, ATT_HEAD_DIM).transpose(0, 2, 1, 3)

    qh = heads(q) * (ATT_HEAD_DIM ** -0.5)
    kp = jnp.pad(heads(k), ((0, 0), (0, 0), (pad, 0), (0, 0)))
    vp = jnp.pad(heads(v), ((0, 0), (0, 0), (pad, 0), (0, 0)))
    t_off = jnp.arange(CHUNK)[:, None]
    j_off = jnp.arange(BAND)[None, :]
    rel = t_off + pad - j_off
    bias = rel_bias.astype(jnp.float32)[:, jnp.clip(rel, -REL_CLIP, REL_CLIP) + REL_CLIP]
    band_idx = jnp.arange(BAND)

    def one_chunk(c):
        start = c * CHUNK
        qc = lax.dynamic_slice_in_dim(qh, start, CHUNK, axis=2)
        kc = lax.dynamic_slice_in_dim(kp, start, BAND, axis=2)
        vc = lax.dynamic_slice_in_dim(vp, start, BAND, axis=2)
        s = jnp.einsum('bhtd,bhsd->bhts', qc, kc).astype(jnp.float32) + bias[None]
        valid = (start - pad + band_idx) >= 0
        s = jnp.where(valid[None, None, None, :], s, jnp.finfo(jnp.float32).min)
        pr = jax.nn.softmax(s, axis=-1)
        return jnp.einsum('bhts,bhsd->bhtd', pr.astype(vc.dtype), vc)

    out = lax.map(one_chunk, jnp.arange(nc))
    return out.transpose(1, 0, 3, 2, 4).reshape(B, S, ATT_WIDTH)


def conv_ffn(h, w_up, conv_w, conv_b, w_down):
    S = h.shape[1]
    u = h @ w_up
    up = jnp.pad(u, ((0, 0), (CONV_WIDTH - 1, 0), (0, 0)))
    uc = conv_b + sum(conv_w[j] * up[:, j:j + S] for j in range(CONV_WIDTH))
    gate, val = uc[..., :D_FF], uc[..., D_FF:]
    return (jax.nn.silu(gate) * val) @ w_down


def _fwd_setup_inputs(seed: int = 0) -> dict:
    key = jax.random.key(seed)
    ks = jax.random.split(key, 20)
    f32 = jnp.float32
    nrm = lambda k, shape, s: jax.random.normal(k, shape, f32) * s
    return {
        "x": nrm(ks[0], (BATCH, SEQ, D_MODEL), 1.0),
        "p": nrm(ks[1], (DEPTH, BATCH, SEQ, PLE_DIM), 1.0),
        "norm_mix": 1.0 + nrm(ks[2], (DEPTH, D_MODEL), 0.02),
        "w_in": nrm(ks[3], (DEPTH, D_MODEL, IN_COLS), D_MODEL ** -0.5),
        "lb_logits": nrm(ks[4], (DEPTH + 1, HG_WIDTH), 0.5),
        "hg_norm": 1.0 + nrm(ks[5], (DEPTH, HG_HEAD_DIM), 0.02),
        "rel_bias": nrm(ks[6], (DEPTH, ATT_HEADS, 2 * REL_CLIP + 1), 0.5),
        "w_out": nrm(ks[7], (DEPTH, MIX_WIDTH, D_MODEL), MIX_WIDTH ** -0.5),
        "norm_ffn": 1.0 + nrm(ks[8], (DEPTH, D_MODEL), 0.02),
        "w_up": nrm(ks[9], (DEPTH, D_MODEL, 2 * D_FF), D_MODEL ** -0.5),
        "conv_w": nrm(ks[10], (DEPTH, CONV_WIDTH, 2 * D_FF), CONV_WIDTH ** -0.5),
        "conv_b": nrm(ks[11], (DEPTH, 2 * D_FF), 0.02),
        "w_down": nrm(ks[12], (DEPTH, D_FF, D_MODEL), D_FF ** -0.5),
        "norm_ple": 1.0 + nrm(ks[13], (DEPTH, D_MODEL), 0.02),
        "w_ple_gate": nrm(ks[14], (DEPTH, D_MODEL, D_MODEL), D_MODEL ** -0.5),
        "w_ple_proj": nrm(ks[15], (DEPTH, PLE_DIM, D_MODEL), PLE_DIM ** -0.5),
        "final_norm": 1.0 + nrm(ks[16], (D_MODEL,), 0.02),
    }


def _fwd_reference(x, p, norm_mix, w_in, lb_logits, hg_norm, rel_bias, w_out, norm_ffn,
              w_up, conv_w, conv_b, w_down, norm_ple, w_ple_gate, w_ple_proj, final_norm):
    lb_all = jnp.cumsum(jax.nn.softmax(lb_logits.astype(jnp.float32), axis=0), axis=0)[:DEPTH]
    split_idx = [int(v) for v in np.cumsum(IN_SPLITS)[:-1]]
    h = x
    for i in range(DEPTH):
        a = rmsnorm(h, norm_mix[i])
        proj = a @ w_in[i]
        hq, hf, hi, hg, aq, ak, av = jnp.split(proj, split_idx, axis=-1)
        y_hg = hgrn2_mixer(hq, hf, hi, hg, lb_all[i], hg_norm[i])
        y_att = chunk_attention(aq, ak, av, rel_bias[i])
        h = h + jnp.concatenate([y_hg, y_att], axis=-1) @ w_out[i]
        h = h + conv_ffn(rmsnorm(h, norm_ffn[i]), w_up[i], conv_w[i], conv_b[i], w_down[i])
        gate = jax.nn.sigmoid(rmsnorm(h, norm_ple[i]) @ w_ple_gate[i])
        h = h + gate * (p[i] @ w_ple_proj[i])
    return rmsnorm(h, final_norm)


import jax as _jax
import jax.numpy as _jnp

TWIN_FORMAT = 'train_step'
FWD_PARAMS = ['x', 'p', 'norm_mix', 'w_in', 'lb_logits', 'hg_norm', 'rel_bias', 'w_out', 'norm_ffn', 'w_up', 'conv_w', 'conv_b', 'w_down', 'norm_ple', 'w_ple_gate', 'w_ple_proj', 'final_norm']
TWIN_WEIGHTS = ['norm_mix', 'w_in', 'lb_logits', 'hg_norm', 'rel_bias', 'w_out', 'norm_ffn', 'w_up', 'conv_w', 'conv_b', 'w_down', 'norm_ple', 'w_ple_gate', 'w_ple_proj', 'final_norm']
TWIN_DIFF_INPUT = 'x'
TWIN_INPUTS = ['x', 'p', 'norm_mix', 'w_in', 'lb_logits', 'hg_norm', 'rel_bias', 'w_out', 'norm_ffn', 'w_up', 'conv_w', 'conv_b', 'w_down', 'norm_ple', 'w_ple_gate', 'w_ple_proj', 'final_norm', 'loss_target', 'm_norm_mix', 'm_w_in', 'm_lb_logits', 'm_hg_norm', 'm_rel_bias', 'm_w_out', 'm_norm_ffn', 'm_w_up', 'm_conv_w', 'm_conv_b', 'm_w_down', 'm_norm_ple', 'm_w_ple_gate', 'm_w_ple_proj', 'm_final_norm', 'v_norm_mix', 'v_w_in', 'v_lb_logits', 'v_hg_norm', 'v_rel_bias', 'v_w_out', 'v_norm_ffn', 'v_w_up', 'v_conv_w', 'v_conv_b', 'v_w_down', 'v_norm_ple', 'v_w_ple_gate', 'v_w_ple_proj', 'v_final_norm']
TWIN_OUTPUTS = ['loss', 'grad_x', 'grad_norm_mix', 'grad_w_in', 'grad_lb_logits', 'grad_hg_norm', 'grad_rel_bias', 'grad_w_out', 'grad_norm_ffn', 'grad_w_up', 'grad_conv_w', 'grad_conv_b', 'grad_w_down', 'grad_norm_ple', 'grad_w_ple_gate', 'grad_w_ple_proj', 'grad_final_norm', 'delta_norm_mix', 'delta_w_in', 'delta_lb_logits', 'delta_hg_norm', 'delta_rel_bias', 'delta_w_out', 'delta_norm_ffn', 'delta_w_up', 'delta_conv_w', 'delta_conv_b', 'delta_w_down', 'delta_norm_ple', 'delta_w_ple_gate', 'delta_w_ple_proj', 'delta_final_norm', 'new_m_norm_mix', 'new_m_w_in', 'new_m_lb_logits', 'new_m_hg_norm', 'new_m_rel_bias', 'new_m_w_out', 'new_m_norm_ffn', 'new_m_w_up', 'new_m_conv_w', 'new_m_conv_b', 'new_m_w_down', 'new_m_norm_ple', 'new_m_w_ple_gate', 'new_m_w_ple_proj', 'new_m_final_norm', 'new_v_norm_mix', 'new_v_w_in', 'new_v_lb_logits', 'new_v_hg_norm', 'new_v_rel_bias', 'new_v_w_out', 'new_v_norm_ffn', 'new_v_w_up', 'new_v_conv_w', 'new_v_conv_b', 'new_v_w_down', 'new_v_norm_ple', 'new_v_w_ple_gate', 'new_v_w_ple_proj', 'new_v_final_norm']
TWIN_LEAF_KINDS = {'loss': 'loss', 'grad_x': 'grad_x', 'grad_norm_mix': 'grad_w', 'grad_w_in': 'grad_w', 'grad_lb_logits': 'grad_w', 'grad_hg_norm': 'grad_w', 'grad_rel_bias': 'grad_w', 'grad_w_out': 'grad_w', 'grad_norm_ffn': 'grad_w', 'grad_w_up': 'grad_w', 'grad_conv_w': 'grad_w', 'grad_conv_b': 'grad_w', 'grad_w_down': 'grad_w', 'grad_norm_ple': 'grad_w', 'grad_w_ple_gate': 'grad_w', 'grad_w_ple_proj': 'grad_w', 'grad_final_norm': 'grad_w', 'delta_norm_mix': 'delta_w', 'delta_w_in': 'delta_w', 'delta_lb_logits': 'delta_w', 'delta_hg_norm': 'delta_w', 'delta_rel_bias': 'delta_w', 'delta_w_out': 'delta_w', 'delta_norm_ffn': 'delta_w', 'delta_w_up': 'delta_w', 'delta_conv_w': 'delta_w', 'delta_conv_b': 'delta_w', 'delta_w_down': 'delta_w', 'delta_norm_ple': 'delta_w', 'delta_w_ple_gate': 'delta_w', 'delta_w_ple_proj': 'delta_w', 'delta_final_norm': 'delta_w', 'new_m_norm_mix': 'new_m', 'new_m_w_in': 'new_m', 'new_m_lb_logits': 'new_m', 'new_m_hg_norm': 'new_m', 'new_m_rel_bias': 'new_m', 'new_m_w_out': 'new_m', 'new_m_norm_ffn': 'new_m', 'new_m_w_up': 'new_m', 'new_m_conv_w': 'new_m', 'new_m_conv_b': 'new_m', 'new_m_w_down': 'new_m', 'new_m_norm_ple': 'new_m', 'new_m_w_ple_gate': 'new_m', 'new_m_w_ple_proj': 'new_m', 'new_m_final_norm': 'new_m', 'new_v_norm_mix': 'new_v', 'new_v_w_in': 'new_v', 'new_v_lb_logits': 'new_v', 'new_v_hg_norm': 'new_v', 'new_v_rel_bias': 'new_v', 'new_v_w_out': 'new_v', 'new_v_norm_ffn': 'new_v', 'new_v_w_up': 'new_v', 'new_v_conv_w': 'new_v', 'new_v_conv_b': 'new_v', 'new_v_w_down': 'new_v', 'new_v_norm_ple': 'new_v', 'new_v_w_ple_gate': 'new_v', 'new_v_w_ple_proj': 'new_v', 'new_v_final_norm': 'new_v'}


def _forward(args):
    return _fwd_reference(*[args[k] for k in FWD_PARAMS])


def _output_shape():
    def fwd():
        inp = _fwd_setup_inputs(0)
        return _fwd_reference(*[inp[k] for k in FWD_PARAMS])
    out = _jax.eval_shape(fwd)
    return out.shape, out.dtype

N_MICROBATCH = 1
ADAM_LR = 0.001
ADAM_B1 = 0.9
ADAM_B2 = 0.999
ADAM_EPS = 1e-08
ADAM_WD = 0.01
ADAM_STEP = 10
PER_EXAMPLE_BATCH_AXIS = {'x': 0, 'p': 1, 'loss_target': 0}
SHARED_INPUTS = []
_WEIGHT_DTYPES = {'norm_mix': _jnp.float32, 'w_in': _jnp.float32, 'lb_logits': _jnp.float32, 'hg_norm': _jnp.float32, 'rel_bias': _jnp.float32, 'w_out': _jnp.float32, 'norm_ffn': _jnp.float32, 'w_up': _jnp.float32, 'conv_w': _jnp.float32, 'conv_b': _jnp.float32, 'w_down': _jnp.float32, 'norm_ple': _jnp.float32, 'w_ple_gate': _jnp.float32, 'w_ple_proj': _jnp.float32, 'final_norm': _jnp.float32}
MOMENT_SCALE = {'norm_mix': 5.435010e-02, 'w_in': 2.901903e-02, 'lb_logits': 4.408048e-03, 'hg_norm': 1.350337e-01, 'rel_bias': 9.055291e-03, 'w_out': 3.707365e-02, 'norm_ffn': 5.850264e-02, 'w_up': 2.499834e-02, 'conv_w': 2.506116e-02, 'conv_b': 2.470774e-02, 'w_down': 4.086885e-02, 'norm_ple': 1.356454e-02, 'w_ple_gate': 1.380965e-02, 'w_ple_proj': 3.550989e-02, 'final_norm': 1.600417e+01}


def _to_microbatches(a, axis):
    t = _jnp.moveaxis(a, axis, 0)
    t = t.reshape((N_MICROBATCH, t.shape[0] // N_MICROBATCH) + t.shape[1:])
    return _jnp.moveaxis(t, 1, axis + 1)


def setup_inputs(seed: int = 0) -> dict:
    inp = _fwd_setup_inputs(seed)
    key = _jax.random.fold_in(_jax.random.key(seed), 7919)
    shape, _ = _output_shape()
    out = dict(inp)
    out["loss_target"] = _jax.random.normal(_jax.random.fold_in(key, 0), shape, _jnp.float32)
    for i, name in enumerate(TWIN_WEIGHTS):
        w = inp[name].astype(_jnp.float32)
        if MOMENT_SCALE is None:
            s = _jnp.sqrt(_jnp.mean(_jnp.square(w)) + 1e-30)
        else:
            s = MOMENT_SCALE[name]
        km, kv = _jax.random.split(_jax.random.fold_in(key, i + 1))
        out[name] = w
        out["m_" + name] = s * _jax.random.normal(km, w.shape, _jnp.float32)
        out["v_" + name] = (s * s) * _jax.random.uniform(kv, w.shape, _jnp.float32, 0.5, 1.5)
    if N_MICROBATCH > 1:
        for name, axis in PER_EXAMPLE_BATCH_AXIS.items():
            out[name] = _to_microbatches(out[name], axis)
    return {'x': out['x'], 'p': out['p'], 'norm_mix': out['norm_mix'], 'w_in': out['w_in'], 'lb_logits': out['lb_logits'], 'hg_norm': out['hg_norm'], 'rel_bias': out['rel_bias'], 'w_out': out['w_out'], 'norm_ffn': out['norm_ffn'], 'w_up': out['w_up'], 'conv_w': out['conv_w'], 'conv_b': out['conv_b'], 'w_down': out['w_down'], 'norm_ple': out['norm_ple'], 'w_ple_gate': out['w_ple_gate'], 'w_ple_proj': out['w_ple_proj'], 'final_norm': out['final_norm'], 'loss_target': out['loss_target'], 'm_norm_mix': out['m_norm_mix'], 'm_w_in': out['m_w_in'], 'm_lb_logits': out['m_lb_logits'], 'm_hg_norm': out['m_hg_norm'], 'm_rel_bias': out['m_rel_bias'], 'm_w_out': out['m_w_out'], 'm_norm_ffn': out['m_norm_ffn'], 'm_w_up': out['m_w_up'], 'm_conv_w': out['m_conv_w'], 'm_conv_b': out['m_conv_b'], 'm_w_down': out['m_w_down'], 'm_norm_ple': out['m_norm_ple'], 'm_w_ple_gate': out['m_w_ple_gate'], 'm_w_ple_proj': out['m_w_ple_proj'], 'm_final_norm': out['m_final_norm'], 'v_norm_mix': out['v_norm_mix'], 'v_w_in': out['v_w_in'], 'v_lb_logits': out['v_lb_logits'], 'v_hg_norm': out['v_hg_norm'], 'v_rel_bias': out['v_rel_bias'], 'v_w_out': out['v_w_out'], 'v_norm_ffn': out['v_norm_ffn'], 'v_w_up': out['v_w_up'], 'v_conv_w': out['v_conv_w'], 'v_conv_b': out['v_conv_b'], 'v_w_down': out['v_w_down'], 'v_norm_ple': out['v_norm_ple'], 'v_w_ple_gate': out['v_w_ple_gate'], 'v_w_ple_proj': out['v_w_ple_proj'], 'v_final_norm': out['v_final_norm']}


def _loss(weights, diff, rest, loss_target):
    with _jax.named_scope("forward"):
        args = {**rest, TWIN_DIFF_INPUT: diff, **{k: w.astype(_WEIGHT_DTYPES[k]) for k, w in weights.items()}}
        y = _forward(args)
    with _jax.named_scope("loss_head"):
        err = _jnp.square(y.astype(_jnp.float32) - loss_target)
        return 0.5 * _jnp.sum(_jnp.mean(err, axis=-1)) if err.ndim else 0.5 * err


def _adamw(w, g, m, v):
    m = ADAM_B1 * m + (1.0 - ADAM_B1) * g
    v = ADAM_B2 * v + (1.0 - ADAM_B2) * _jnp.square(g)
    m_hat = m / (1.0 - ADAM_B1 ** ADAM_STEP)
    v_hat = v / (1.0 - ADAM_B2 ** ADAM_STEP)
    delta = -ADAM_LR * (m_hat / (_jnp.sqrt(v_hat) + ADAM_EPS) + ADAM_WD * w)
    return delta, m, v


def reference(x, p, norm_mix, w_in, lb_logits, hg_norm, rel_bias, w_out, norm_ffn, w_up, conv_w, conv_b, w_down, norm_ple, w_ple_gate, w_ple_proj, final_norm, loss_target, m_norm_mix, m_w_in, m_lb_logits, m_hg_norm, m_rel_bias, m_w_out, m_norm_ffn, m_w_up, m_conv_w, m_conv_b, m_w_down, m_norm_ple, m_w_ple_gate, m_w_ple_proj, m_final_norm, v_norm_mix, v_w_in, v_lb_logits, v_hg_norm, v_rel_bias, v_w_out, v_norm_ffn, v_w_up, v_conv_w, v_conv_b, v_w_down, v_norm_ple, v_w_ple_gate, v_w_ple_proj, v_final_norm):
    given = dict(x=x, p=p, norm_mix=norm_mix, w_in=w_in, lb_logits=lb_logits, hg_norm=hg_norm, rel_bias=rel_bias, w_out=w_out, norm_ffn=norm_ffn, w_up=w_up, conv_w=conv_w, conv_b=conv_b, w_down=w_down, norm_ple=norm_ple, w_ple_gate=w_ple_gate, w_ple_proj=w_ple_proj, final_norm=final_norm, loss_target=loss_target, m_norm_mix=m_norm_mix, m_w_in=m_w_in, m_lb_logits=m_lb_logits, m_hg_norm=m_hg_norm, m_rel_bias=m_rel_bias, m_w_out=m_w_out, m_norm_ffn=m_norm_ffn, m_w_up=m_w_up, m_conv_w=m_conv_w, m_conv_b=m_conv_b, m_w_down=m_w_down, m_norm_ple=m_norm_ple, m_w_ple_gate=m_w_ple_gate, m_w_ple_proj=m_w_ple_proj, m_final_norm=m_final_norm, v_norm_mix=v_norm_mix, v_w_in=v_w_in, v_lb_logits=v_lb_logits, v_hg_norm=v_hg_norm, v_rel_bias=v_rel_bias, v_w_out=v_w_out, v_norm_ffn=v_norm_ffn, v_w_up=v_w_up, v_conv_w=v_conv_w, v_conv_b=v_conv_b, v_w_down=v_w_down, v_norm_ple=v_norm_ple, v_w_ple_gate=v_w_ple_gate, v_w_ple_proj=v_w_ple_proj, v_final_norm=v_final_norm)
    weights = {n: given[n] for n in TWIN_WEIGHTS}
    shared = {n: given[n] for n in SHARED_INPUTS}
    per_example = {n: given[n] for n in ['x', 'p']}
    grad_fn = _jax.value_and_grad(_loss, argnums=(0, 1))

    def one_microbatch(ex, loss_target):
        ex = dict(ex)
        diff = ex.pop(TWIN_DIFF_INPUT)
        return grad_fn(weights, diff, {**shared, **ex}, loss_target)

    if N_MICROBATCH == 1:
        loss, (grad_w, grad_x) = one_microbatch(per_example, given["loss_target"])
    else:
        def body(carry, xs):
            loss_sum, grad_sum = carry
            l_k, (gw_k, gx_k) = one_microbatch(xs[0], xs[1])
            with _jax.named_scope("update"):
                return (loss_sum + l_k, _jax.tree.map(_jnp.add, grad_sum, gw_k)), gx_k

        init = (_jnp.zeros((), _jnp.float32), _jax.tree.map(_jnp.zeros_like, weights))
        (loss, grad_w), grad_x = _jax.lax.scan(body, init, (per_example, given["loss_target"]))
    with _jax.named_scope("update"):
        delta_w, new_m, new_v = {}, {}, {}
        for n in TWIN_WEIGHTS:
            delta_w[n], new_m[n], new_v[n] = _adamw(weights[n], grad_w[n], given["m_" + n], given["v_" + n])
    return (loss, grad_x, *[grad_w[n] for n in TWIN_WEIGHTS], *[delta_w[n] for n in TWIN_WEIGHTS],
            *[new_m[n] for n in TWIN_WEIGHTS], *[new_v[n] for n in TWIN_WEIGHTS])
```

```python
import functools

import jax
import jax.numpy as jnp
import numpy as np
from jax import lax
from jax.experimental import pallas as pl
from jax.experimental.pallas import tpu as pltpu

f32 = jnp.float32
bf16 = jnp.bfloat16

CHUNK = 64
HEAD_DIM = 128
LEFT_CHUNKS = 8
REL_CLIP = 128
EPS = 1e-6
HG_SUB = 16
HG_HEADS_PER_STEP = 2
ATT_Q_ROWS = 256
ADAM_LR, ADAM_B1, ADAM_B2, ADAM_EPS, ADAM_WD, ADAM_STEP = 0.001, 0.9, 0.999, 1e-08, 0.01, 10
LANES = 128
SUBLANES = 8
N_CHIPS = 4
N_DEV = 8
MESH = pl.DeviceIdType.MESH
NEG = float(np.finfo(np.float32).min)

NN = (((1,), (0,)), ((), ()))
NT = (((1,), (1,)), ((), ()))
TN = (((0,), (0,)), ((), ()))
HIGHEST = lax.Precision.HIGHEST


def _dot(a, b, dims, precision=None):
    return lax.dot_general(a, b, dims, preferred_element_type=f32, precision=precision)


def _sigmoid(v):
    return 1.0 / (1.0 + jnp.exp(-v))


def _div(n, pref, mult):
    best = None
    d = mult
    while d <= min(n, pref):
        if n % d == 0:
            best = d
        d += mult
    return best if best is not None else n


def _hbm():
    return pl.BlockSpec(memory_space=pltpu.HBM)


def _vmem():
    return pl.BlockSpec(memory_space=pltpu.VMEM)


def _mm(a, b, *, dims, tm, tn, tk, out_dtype, name, res=None, a_idx=None, b_idx=None, o_idx=None, out_shape=None):
    if dims == "nn":
        (M, K), (_, N) = a.shape, b.shape
        a_blk, b_blk, dn = (tm, tk), (tk, tn), NN
        a_def, b_def = (lambda i, j, k: (i, k)), (lambda i, j, k: (k, j))
    elif dims == "nt":
        (M, K), (N, _) = a.shape, b.shape
        a_blk, b_blk, dn = (tm, tk), (tn, tk), NT
        a_def, b_def = (lambda i, j, k: (i, k)), (lambda i, j, k: (j, k))
    else:
        (K, M), (_, N) = a.shape, b.shape
        a_blk, b_blk, dn = (tk, tm), (tk, tn), TN
        a_def, b_def = (lambda i, j, k: (k, i)), (lambda i, j, k: (k, j))
    assert M % tm == 0 and N % tn == 0 and K % tk == 0, (name, M, N, K, tm, tn, tk)
    nk = K // tk
    has_res = res is not None

    def body(*refs):
        a_ref, b_ref = refs[0], refs[1]
        res_ref = refs[2] if has_res else None
        o_ref = refs[3] if has_res else refs[2]
        part = _dot(a_ref[...].astype(bf16), b_ref[...].astype(bf16), dn)

        def finish(acc):
            if has_res:
                acc = acc + res_ref[...]
            o_ref[...] = acc.astype(out_dtype)

        if nk == 1:
            finish(part)
        else:
            acc_ref = refs[-1]
            k = pl.program_id(2)

            @pl.when(k == 0)
            def _():
                acc_ref[...] = part

            @pl.when(k > 0)
            def _():
                acc_ref[...] += part

            @pl.when(k == nk - 1)
            def _():
                finish(acc_ref[...])

    o_map = o_idx or (lambda i, j, k: (i, j))
    in_specs = [pl.BlockSpec(a_blk, a_idx or a_def), pl.BlockSpec(b_blk, b_idx or b_def)]
    args = [a, b]
    if has_res:
        in_specs.append(pl.BlockSpec((tm, tn), o_map))
        args.append(res)
    return pl.pallas_call(
        body,
        name=name,
        grid=(M // tm, N // tn, nk),
        in_specs=in_specs,
        out_specs=pl.BlockSpec((tm, tn), o_map),
        out_shape=jax.ShapeDtypeStruct(out_shape or (M, N), out_dtype),
        scratch_shapes=[pltpu.VMEM((tm, tn), f32)] if nk > 1 else [],
        compiler_params=pltpu.CompilerParams(dimension_semantics=("parallel", "parallel", "arbitrary")),
    )(*args)


def _rms_fwd(xv, g, *, name):
    S, D = xv.shape
    ts = _div(S, 512, SUBLANES)

    def body(x_ref, g_ref, o_ref):
        v = x_ref[...]
        r = lax.rsqrt(jnp.mean(v * v, axis=-1, keepdims=True) + EPS)
        o_ref[...] = (v * r * g_ref[...]).astype(bf16)

    return pl.pallas_call(
        body,
        name=name,
        grid=(S // ts,),
        in_specs=[pl.BlockSpec((ts, D), lambda i: (i, 0)), pl.BlockSpec((1, D), lambda i: (0, 0))],
        out_specs=pl.BlockSpec((ts, D), lambda i: (i, 0)),
        out_shape=jax.ShapeDtypeStruct((S, D), bf16),
        compiler_params=pltpu.CompilerParams(dimension_semantics=("parallel",)),
    )(xv, g)


def _rms_bwd(xv, g, dy, dres, *, name):
    S, D = xv.shape
    ts = _div(S, 256, SUBLANES)

    def body(x_ref, g_ref, dy_ref, dres_ref, dx_ref, dxb_ref, dg_ref):
        i = pl.program_id(0)
        v = x_ref[...]
        r = lax.rsqrt(jnp.mean(v * v, axis=-1, keepdims=True) + EPS)
        vn = v * r
        d = dy_ref[...]
        part = jnp.sum(d * vn, axis=0, keepdims=True)

        @pl.when(i == 0)
        def _():
            dg_ref[...] = part

        @pl.when(i > 0)
        def _():
            dg_ref[...] += part

        t = d * g_ref[...]
        dx = dres_ref[...] + r * (t - vn * jnp.mean(t * vn, axis=-1, keepdims=True))
        dx_ref[...] = dx
        dxb_ref[...] = dx.astype(bf16)

    row = pl.BlockSpec((ts, D), lambda i: (i, 0))
    vec = pl.BlockSpec((1, D), lambda i: (0, 0))
    return pl.pallas_call(
        body,
        name=name,
        grid=(S // ts,),
        in_specs=[row, vec, row, row],
        out_specs=[row, row, vec],
        out_shape=[jax.ShapeDtypeStruct((S, D), f32), jax.ShapeDtypeStruct((S, D), bf16), jax.ShapeDtypeStruct((1, D), f32)],
        compiler_params=pltpu.CompilerParams(dimension_semantics=("arbitrary",)),
    )(xv, g, dy, dres)


def _tail(h2, zg, pp, tgt, fn):
    S, D = h2.shape
    ts = _div(S, 256, SUBLANES)

    def body(h_ref, z_ref, p_ref, t_ref, fn_ref, dh_ref, dz_ref, dp_ref, loss_ref, dfn_ref):
        i = pl.program_id(0)
        sg = _sigmoid(z_ref[...])
        ppv = p_ref[...]
        h3 = h_ref[...] + sg * ppv
        r = lax.rsqrt(jnp.mean(h3 * h3, axis=-1, keepdims=True) + EPS)
        hn = h3 * r
        fnv = fn_ref[...]
        e = hn * fnv - t_ref[...]
        lpart = 0.5 * jnp.sum(jnp.mean(e * e, axis=-1, keepdims=True), axis=0, keepdims=True)
        dy = e * (1.0 / D)
        dpart = jnp.sum(dy * hn, axis=0, keepdims=True)

        @pl.when(i == 0)
        def _():
            loss_ref[...] = jnp.broadcast_to(lpart, loss_ref.shape)
            dfn_ref[...] = dpart

        @pl.when(i > 0)
        def _():
            loss_ref[...] += jnp.broadcast_to(lpart, loss_ref.shape)
            dfn_ref[...] += dpart

        t = dy * fnv
        dh3 = r * (t - hn * jnp.mean(t * hn, axis=-1, keepdims=True))
        dh_ref[...] = dh3
        dz_ref[...] = (dh3 * ppv * sg * (1.0 - sg)).astype(bf16)
        dp_ref[...] = (dh3 * sg).astype(bf16)

    row = pl.BlockSpec((ts, D), lambda i: (i, 0))
    vec = pl.BlockSpec((1, D), lambda i: (0, 0))
    one = pl.BlockSpec((1, LANES), lambda i: (0, 0))
    return pl.pallas_call(
        body,
        name="ple_final_loss",
        grid=(S // ts,),
        in_specs=[row, row, row, row, vec],
        out_specs=[row, row, row, one, vec],
        out_shape=[
            jax.ShapeDtypeStruct((S, D), f32),
            jax.ShapeDtypeStruct((S, D), bf16),
            jax.ShapeDtypeStruct((S, D), bf16),
            jax.ShapeDtypeStruct((1, LANES), f32),
            jax.ShapeDtypeStruct((1, D), f32),
        ],
        compiler_params=pltpu.CompilerParams(dimension_semantics=("arbitrary",)),
    )(h2, zg, pp, tgt, fn)


def _shift_down(v, halo, k):
    r = pltpu.roll(v, k, 0)
    hr = pltpu.roll(halo, k, 0)
    row = lax.broadcasted_iota(jnp.int32, hr.shape, 0)
    top = jnp.where(row < k, hr, r[0:SUBLANES])
    return jnp.concatenate([top, r[SUBLANES:]], axis=0)


def _shift_up(v, halo, k):
    n = v.shape[0]
    r = pltpu.roll(v, n - k, 0)
    hr = pltpu.roll(halo, SUBLANES - k, 0)
    row = lax.broadcasted_iota(jnp.int32, hr.shape, 0)
    bot = jnp.where(row >= SUBLANES - k, hr, r[n - SUBLANES:])
    return jnp.concatenate([r[: n - SUBLANES], bot], axis=0)


def _conv_specs(S, F, tc, ts):
    nj = F // tc
    rows8 = ts // SUBLANES
    main = pl.BlockSpec((ts, 2 * tc), lambda j, i: (i, j))
    prev = pl.BlockSpec((SUBLANES, 2 * tc), lambda j, i: (jnp.maximum(i * rows8 - 1, 0), j))
    nxt = pl.BlockSpec((SUBLANES, 2 * tc), lambda j, i: (jnp.minimum((i + 1) * rows8, S // SUBLANES - 1), j))
    wg = pl.BlockSpec((3, tc), lambda j, i: (0, j))
    wv = pl.BlockSpec((3, tc), lambda j, i: (0, nj + j))
    bg = pl.BlockSpec((1, tc), lambda j, i: (0, j))
    bv = pl.BlockSpec((1, tc), lambda j, i: (0, nj + j))
    half = pl.BlockSpec((ts, tc), lambda j, i: (i, j))
    return nj, main, prev, nxt, wg, wv, bg, bv, half


def _conv_pre(u_ref, h_ref, wg_ref, wv_ref, bg_ref, bv_ref, tc):
    i = pl.program_id(1)
    u = u_ref[...]
    halo = jnp.where(i == 0, 0.0, h_ref[...])
    u1 = _shift_down(u, halo, 1)
    u2 = _shift_down(u, halo, 2)
    w = jnp.concatenate([wg_ref[...], wv_ref[...]], axis=1)
    b = jnp.concatenate([bg_ref[...], bv_ref[...]], axis=1)
    uc = b + w[0:1] * u2 + w[1:2] * u1 + w[2:3] * u
    return u, u1, u2, uc[:, :tc], uc[:, tc:]


def _convact_fwd(u, cw, cb, tc):
    S, F2 = u.shape
    F = F2 // 2
    ts = _div(S, 512, SUBLANES)
    nj, main, prev, nxt, wg, wv, bg, bv, half = _conv_specs(S, F, tc, ts)

    def body(u_ref, h_ref, wg_ref, wv_ref, bg_ref, bv_ref, m_ref):
        _, _, _, g, v = _conv_pre(u_ref, h_ref, wg_ref, wv_ref, bg_ref, bv_ref, tc)
        m_ref[...] = (g * _sigmoid(g) * v).astype(bf16)

    return pl.pallas_call(
        body,
        name="convact_fwd",
        grid=(nj, S // ts),
        in_specs=[main, prev, wg, wv, bg, bv],
        out_specs=half,
        out_shape=jax.ShapeDtypeStruct((S, F), bf16),
        compiler_params=pltpu.CompilerParams(dimension_semantics=("parallel", "parallel")),
    )(u, u, cw, cw, cb, cb)


def _convact_bwd_a(u, dm, cw, cb, tc):
    S, F2 = u.shape
    F = F2 // 2
    ts = _div(S, 512, SUBLANES)
    nj, main, prev, nxt, wg, wv, bg, bv, half = _conv_specs(S, F, tc, ts)

    def body(u_ref, h_ref, dm_ref, wg_ref, wv_ref, bg_ref, bv_ref, duc_ref, dwg_ref, dwv_ref, dbg_ref, dbv_ref):
        i = pl.program_id(1)
        u0, u1, u2, g, v = _conv_pre(u_ref, h_ref, wg_ref, wv_ref, bg_ref, bv_ref, tc)
        d = dm_ref[...]
        sg = _sigmoid(g)
        dg = d * v * sg * (1.0 + g * (1.0 - sg))
        dv = d * g * sg
        duc = jnp.concatenate([dg, dv], axis=1)
        duc_ref[...] = duc
        db = jnp.sum(duc, axis=0, keepdims=True)
        dw = jnp.concatenate(
            [jnp.sum(duc * u2, axis=0, keepdims=True), jnp.sum(duc * u1, axis=0, keepdims=True), jnp.sum(duc * u0, axis=0, keepdims=True)],
            axis=0,
        )

        @pl.when(i == 0)
        def _():
            dwg_ref[...] = dw[:, :tc]
            dwv_ref[...] = dw[:, tc:]
            dbg_ref[...] = db[:, :tc]
            dbv_ref[...] = db[:, tc:]

        @pl.when(i > 0)
        def _():
            dwg_ref[...] += dw[:, :tc]
            dwv_ref[...] += dw[:, tc:]
            dbg_ref[...] += db[:, :tc]
            dbv_ref[...] += db[:, tc:]

    w_out = pl.BlockSpec((3, tc), lambda j, i: (0, j))
    b_out = pl.BlockSpec((1, tc), lambda j, i: (0, j))
    return pl.pallas_call(
        body,
        name="convact_bwd_a",
        grid=(nj, S // ts),
        in_specs=[main, prev, half, wg, wv, bg, bv],
        out_specs=[main, w_out, w_out, b_out, b_out],
        out_shape=[
            jax.ShapeDtypeStruct((S, F2), f32),
            jax.ShapeDtypeStruct((3, F), f32),
            jax.ShapeDtypeStruct((3, F), f32),
            jax.ShapeDtypeStruct((1, F), f32),
            jax.ShapeDtypeStruct((1, F), f32),
        ],
        compiler_params=pltpu.CompilerParams(dimension_semantics=("parallel", "arbitrary")),
    )(u, u, dm, cw, cw, cb, cb)


def _convact_bwd_b(duc, cw, tc):
    S, F2 = duc.shape
    F = F2 // 2
    ts = _div(S, 512, SUBLANES)
    nj, main, prev, nxt, wg, wv, bg, bv, half = _conv_specs(S, F, tc, ts)
    ni = S // ts

    def body(d_ref, h_ref, wg_ref, wv_ref, du_ref):
        i = pl.program_id(1)
        d = d_ref[...]
        halo = jnp.where(i == ni - 1, 0.0, h_ref[...])
        d1 = _shift_up(d, halo, 1)
        d2 = _shift_up(d, halo, 2)
        w = jnp.concatenate([wg_ref[...], wv_ref[...]], axis=1)
        du_ref[...] = (w[2:3] * d + w[1:2] * d1 + w[0:1] * d2).astype(bf16)

    return pl.pallas_call(
        body,
        name="convact_bwd_b",
        grid=(nj, ni),
        in_specs=[main, nxt, wg, wv],
        out_specs=main,
        out_shape=jax.ShapeDtypeStruct((S, F2), bf16),
        compiler_params=pltpu.CompilerParams(dimension_semantics=("parallel", "parallel")),
    )(duc, duc, cw, cw)


def _hg_gates(qp, fp, lbl):
    lb = _sigmoid(lbl[0:1] - lbl[1:2])
    sig = _sigmoid(fp)
    sigm = _sigmoid(-fp)
    f = lb + (1.0 - lb) * sig
    k = (1.0 - lb) * sigm
    sq = _sigmoid(qp)
    qf = qp * sq
    row = lax.broadcasted_iota(jnp.int32, (CHUNK, CHUNK), 0)
    col = lax.broadcasted_iota(jnp.int32, (CHUNK, CHUNK), 1)
    b = _dot((row >= col).astype(f32), jnp.log(f), NN, HIGHEST)
    return lb, sig, sigm, f, k, sq, qf, b


def _hg_block(qf, k, b, I):
    r0, n = I * HG_SUB, (I + 1) * HG_SUB
    base = b[r0 - 1:r0] if I > 0 else jnp.zeros((1, HEAD_DIM), f32)
    eq = jnp.exp(b[r0:n] - base)
    ek = jnp.exp(base - b[0:n])
    qt = qf[r0:n] * eq
    kt = k[0:n] * ek
    row = lax.broadcasted_iota(jnp.int32, (HG_SUB, n), 0) + r0
    col = lax.broadcasted_iota(jnp.int32, (HG_SUB, n), 1)
    return r0, n, eq, ek, qt, kt, col <= row


def _hgrn_specs(S, W, hp, reverse):
    nc = S // CHUNK
    ngrp = W // (hp * HEAD_DIM)
    cw = hp * HEAD_DIM
    cidx = (lambda c: nc - 1 - c) if reverse else (lambda c: c)

    def proj(off):
        return pl.BlockSpec((CHUNK, cw), lambda h, c: (cidx(c), off * ngrp + h))

    act = pl.BlockSpec((CHUNK, cw), lambda h, c: (cidx(c), h))
    lbl = pl.BlockSpec((2, cw), lambda h, c: (0, h))
    gn = pl.BlockSpec((1, HEAD_DIM), lambda h, c: (0, 0))
    st = pl.BlockSpec((hp, 1, HEAD_DIM, HEAD_DIM), lambda h, c: (h, cidx(c), 0, 0))
    return nc, ngrp, proj, act, lbl, gn, st


def _hgrn_fwd(proj, lb_logits, hg_norm, W):
    S = proj.shape[0]
    hp = HG_HEADS_PER_STEP
    nc, ngrp, pspec, act, lbl_spec, gn_spec, st_spec = _hgrn_specs(S, W, hp, False)
    nb = CHUNK // HG_SUB

    def body(q_ref, f_ref, i_ref, g_ref, lbl_ref, gn_ref, y_ref, o_ref, st_ref, state):
        c = pl.program_id(1)

        @pl.when(c == 0)
        def _():
            state[...] = jnp.zeros_like(state)

        for p in range(hp):
            sl = slice(p * HEAD_DIM, (p + 1) * HEAD_DIM)
            _, _, _, _, k, _, qf, b = _hg_gates(q_ref[:, sl], f_ref[:, sl], lbl_ref[:, sl])
            v16 = i_ref[:, sl].astype(bf16)
            st_prev = state[p]
            st_ref[p, 0] = st_prev
            o = _dot((qf * jnp.exp(b)).astype(bf16), st_prev.astype(bf16), NT)
            parts = []
            for I in range(nb):
                r0, n, _, _, qt, kt, mask = _hg_block(qf, k, b, I)
                sc = jnp.where(mask, _dot(qt.astype(bf16), kt.astype(bf16), NT), 0.0)
                parts.append(_dot(sc.astype(bf16), v16[0:n], NN))
            o = o + jnp.concatenate(parts, axis=0)
            bl = b[CHUNK - 1:CHUNK]
            kd = k * jnp.exp(bl - b)
            state[p] = st_prev * jnp.exp(bl) + _dot(v16, kd.astype(bf16), TN)
            o_ref[:, sl] = o
            r = lax.rsqrt(jnp.mean(o * o, axis=-1, keepdims=True) + EPS)
            gp = g_ref[:, sl]
            y_ref[:, sl] = (o * r * gn_ref[...] * (gp * _sigmoid(gp))).astype(bf16)

    H = W // HEAD_DIM
    return pl.pallas_call(
        body,
        name="hgrn_fwd",
        grid=(ngrp, nc),
        in_specs=[pspec(0), pspec(1), pspec(2), pspec(3), lbl_spec, gn_spec],
        out_specs=[act, act, st_spec],
        out_shape=[
            jax.ShapeDtypeStruct((S, W), bf16),
            jax.ShapeDtypeStruct((S, W), f32),
            jax.ShapeDtypeStruct((H, nc, HEAD_DIM, HEAD_DIM), f32),
        ],
        scratch_shapes=[pltpu.VMEM((hp, HEAD_DIM, HEAD_DIM), f32)],
        compiler_params=pltpu.CompilerParams(dimension_semantics=("parallel", "arbitrary")),
    )(proj, proj, proj, proj, lb_logits, hg_norm)


def _hgrn_bwd(proj, o_hg, st, dycat, lb_logits, hg_norm, W):
    S = proj.shape[0]
    hp = HG_HEADS_PER_STEP
    nc, ngrp, pspec, act, lbl_spec, gn_spec, st_spec = _hgrn_specs(S, W, hp, True)
    nb = CHUNK // HG_SUB

    def body(q_ref, f_ref, i_ref, g_ref, o_ref, st_ref, dy_ref, lbl_ref, gn_ref,
             dq_ref, df_ref, di_ref, dg_ref, dlbl_ref, dgn_ref, dstate, dk_acc, db_acc, dv_acc, dq_acc):
        h = pl.program_id(0)
        c = pl.program_id(1)

        @pl.when(c == 0)
        def _():
            dstate[...] = jnp.zeros_like(dstate)
            dlbl_ref[...] = jnp.zeros_like(dlbl_ref)

        @pl.when((c == 0) & (h == 0))
        def _():
            dgn_ref[...] = jnp.zeros_like(dgn_ref)

        row = lax.broadcasted_iota(jnp.int32, (CHUNK, CHUNK), 0)
        col = lax.broadcasted_iota(jnp.int32, (CHUNK, CHUNK), 1)
        upper = (col >= row).astype(f32)
        last_row = lax.broadcasted_iota(jnp.int32, (CHUNK, HEAD_DIM), 0) == CHUNK - 1

        for p in range(hp):
            sl = slice(p * HEAD_DIM, (p + 1) * HEAD_DIM)
            qp, fp = q_ref[:, sl], f_ref[:, sl]
            lb, sig, sigm, f, k, sq, qf, b = _hg_gates(qp, fp, lbl_ref[:, sl])
            v16 = i_ref[:, sl].astype(bf16)
            gnv = gn_ref[...]
            gp = g_ref[:, sl]
            sgg = _sigmoid(gp)
            gate = gp * sgg
            o = o_ref[:, sl]
            r = lax.rsqrt(jnp.mean(o * o, axis=-1, keepdims=True) + EPS)
            on = o * r
            dy = dy_ref[:, sl]
            dg_ref[:, sl] = (dy * on * gnv * (sgg * (1.0 + gp * (1.0 - sgg)))).astype(bf16)
            dgn_ref[...] += jnp.sum(dy * on * gate, axis=0, keepdims=True)
            don = dy * gnv * gate
            do = r * (don - on * jnp.mean(don * on, axis=-1, keepdims=True))
            do16 = do.astype(bf16)
            eb = jnp.exp(b)
            A = qf * eb
            bl = b[CHUNK - 1:CHUNK]
            ebl = jnp.exp(bl)
            ekd = jnp.exp(bl - b)
            kd = k * ekd
            st_prev = st_ref[p, 0]
            dst_new = dstate[p]
            dst16 = dst_new.astype(bf16)
            dA = _dot(do16, st_prev.astype(bf16), NN)
            dkd = _dot(v16, dst16, NN)
            dstate[p] = dst_new * ebl + _dot(do16, A.astype(bf16), TN)
            dbl = jnp.sum(dst_new * st_prev, axis=0, keepdims=True) * ebl + jnp.sum(dkd * kd, axis=0, keepdims=True)
            dq_acc[p] = dA * eb
            db_acc[p] = dA * A - dkd * kd + jnp.where(last_row, dbl, 0.0)
            dk_acc[p] = dkd * ekd
            dv_acc[p] = _dot(kd.astype(bf16), dst16, NT)
            for I in range(nb):
                r0, n, eq, ek, qt, kt, mask = _hg_block(qf, k, b, I)
                qt16, kt16 = qt.astype(bf16), kt.astype(bf16)
                sc = jnp.where(mask, _dot(qt16, kt16, NT), 0.0)
                dsc = jnp.where(mask, _dot(do16[r0:n], v16[0:n], NT), 0.0).astype(bf16)
                dv_acc[p, 0:n, :] += _dot(sc.astype(bf16), do16[r0:n], TN)
                dqt = _dot(dsc, kt16, NN)
                dkt = _dot(dsc, qt16, TN)
                dq_acc[p, r0:n, :] += dqt * eq
                dk_acc[p, 0:n, :] += dkt * ek
                db_acc[p, r0:n, :] += dqt * qt16.astype(f32)
                db_acc[p, 0:n, :] -= dkt * kt16.astype(f32)
            dlogf = _dot(upper, db_acc[p], NN, HIGHEST)
            dfg = dlogf / f
            dk = dk_acc[p]
            df_ref[:, sl] = ((1.0 - lb) * sig * sigm * (dfg - dk)).astype(bf16)
            dlb = jnp.sum(sigm * (dfg - dk), axis=0, keepdims=True)
            dl0 = dlb * lb * (1.0 - lb)
            dlbl_ref[:, sl] += jnp.concatenate([dl0, -dl0], axis=0)
            dq_ref[:, sl] = (dq_acc[p] * (sq * (1.0 + qp * (1.0 - sq)))).astype(bf16)
            di_ref[:, sl] = dv_acc[p].astype(bf16)

    cw = hp * HEAD_DIM
    dy_spec = pl.BlockSpec((CHUNK, cw), lambda h, c: (nc - 1 - c, h))
    acc = pltpu.VMEM((hp, CHUNK, HEAD_DIM), f32)
    return pl.pallas_call(
        body,
        name="hgrn_bwd",
        grid=(ngrp, nc),
        in_specs=[pspec(0), pspec(1), pspec(2), pspec(3), act, st_spec, dy_spec, lbl_spec, gn_spec],
        out_specs=[act, act, act, act, lbl_spec, gn_spec],
        out_shape=[jax.ShapeDtypeStruct((S, W), bf16)] * 4
        + [jax.ShapeDtypeStruct((2, W), f32), jax.ShapeDtypeStruct((1, HEAD_DIM), f32)],
        scratch_shapes=[pltpu.VMEM((hp, HEAD_DIM, HEAD_DIM), f32), acc, acc, acc, acc],
        compiler_params=pltpu.CompilerParams(dimension_semantics=("arbitrary", "arbitrary")),
    )(proj, proj, proj, proj, o_hg, st, dycat, lb_logits, hg_norm)


def _att_dims(S):
    qt = ATT_Q_ROWS if S % ATT_Q_ROWS == 0 else CHUNK
    pad = LEFT_CHUNKS * CHUNK
    kb = pad + qt
    return qt, pad, kb, kb + qt


def _att_scores(q16, kb16, bias, start, qt, pad, kb):
    s = _dot(q16, kb16, NT) * (HEAD_DIM ** -0.5) + bias
    row = lax.broadcasted_iota(jnp.int32, (qt, kb), 0)
    col = lax.broadcasted_iota(jnp.int32, (qt, kb), 1)
    lo = jnp.bitwise_and(row, -CHUNK)
    ok = (col >= lo) & (col < lo + pad + CHUNK) & (col + start >= pad)
    s = jnp.where(ok, s, NEG)
    e = jnp.exp(s - jnp.max(s, axis=-1, keepdims=True))
    return e / jnp.sum(e, axis=-1, keepdims=True)


def _att_bias(ext_ref, qt, kb, ne):
    e = jnp.broadcast_to(ext_ref[0], (qt, ne))
    return pltpu.roll(e, ne - qt + 1, 1, stride=1, stride_axis=0)[:, :kb]


def _attn_fwd(proj, kp, vp, ext, W):
    S = proj.shape[0]
    H = W // HEAD_DIM
    qt, pad, kb, ne = _att_dims(S)

    def body(q_ref, k_ref, v_ref, ext_ref, o_ref, bias):
        g = pl.program_id(1)

        @pl.when(g == 0)
        def _():
            bias[...] = _att_bias(ext_ref, qt, kb, ne)

        start = pl.multiple_of(g * qt, qt)
        pn = _att_scores(q_ref[...].astype(bf16), k_ref[pl.ds(start, kb), :], bias[...], start, qt, pad, kb)
        o_ref[...] = _dot(pn.astype(bf16), v_ref[pl.ds(start, kb), :], NN).astype(bf16)

    return pl.pallas_call(
        body,
        name="attn_fwd",
        grid=(H, S // qt),
        in_specs=[
            pl.BlockSpec((qt, HEAD_DIM), lambda h, g: (g, 4 * H + h)),
            pl.BlockSpec((S + pad, HEAD_DIM), lambda h, g: (0, h)),
            pl.BlockSpec((S + pad, HEAD_DIM), lambda h, g: (0, h)),
            pl.BlockSpec((1, 1, ne), lambda h, g: (h, 0, 0)),
        ],
        out_specs=pl.BlockSpec((qt, HEAD_DIM), lambda h, g: (g, h)),
        out_shape=jax.ShapeDtypeStruct((S, W), bf16),
        scratch_shapes=[pltpu.VMEM((qt, kb), f32)],
        compiler_params=pltpu.CompilerParams(dimension_semantics=("parallel", "arbitrary")),
    )(proj, kp, vp, ext)


def _attn_bwd(proj, kp, vp, ext, dycat, W):
    S = proj.shape[0]
    H = W // HEAD_DIM
    qt, pad, kb, ne = _att_dims(S)
    scale = HEAD_DIM ** -0.5

    def body(q_ref, k_ref, v_ref, ext_ref, do_ref, dq_ref, dk_ref, dv_ref, db_ref, bias):
        g = pl.program_id(1)

        @pl.when(g == 0)
        def _():
            bias[...] = _att_bias(ext_ref, qt, kb, ne)
            dk_ref[...] = jnp.zeros_like(dk_ref)
            dv_ref[...] = jnp.zeros_like(dv_ref)
            db_ref[...] = jnp.zeros_like(db_ref)

        start = pl.multiple_of(g * qt, qt)
        q16 = q_ref[...].astype(bf16)
        kb16 = k_ref[pl.ds(start, kb), :]
        vb16 = v_ref[pl.ds(start, kb), :]
        pn = _att_scores(q16, kb16, bias[...], start, qt, pad, kb)
        do16 = do_ref[...].astype(bf16)
        dpn = _dot(do16, vb16, NT)
        dv_ref[pl.ds(start, kb), :] += _dot(pn.astype(bf16), do16, TN)
        ds = pn * (dpn - jnp.sum(dpn * pn, axis=-1, keepdims=True))
        db_ref[0] += ds
        ds16 = ds.astype(bf16)
        dq_ref[...] = (_dot(ds16, kb16, NN) * scale).astype(bf16)
        dk_ref[pl.ds(start, kb), :] += _dot(ds16, q16, TN) * scale

    kv_spec = pl.BlockSpec((S + pad, HEAD_DIM), lambda h, g: (0, h))
    return pl.pallas_call(
        body,
        name="attn_bwd",
        grid=(H, S // qt),
        in_specs=[
            pl.BlockSpec((qt, HEAD_DIM), lambda h, g: (g, 4 * H + h)),
            kv_spec,
            kv_spec,
            pl.BlockSpec((1, 1, ne), lambda h, g: (h, 0, 0)),
            pl.BlockSpec((qt, HEAD_DIM), lambda h, g: (g, H + h)),
        ],
        out_specs=[
            pl.BlockSpec((qt, HEAD_DIM), lambda h, g: (g, h)),
            kv_spec,
            kv_spec,
            pl.BlockSpec((1, qt, kb), lambda h, g: (h, 0, 0)),
        ],
        out_shape=[
            jax.ShapeDtypeStruct((S, W), bf16),
            jax.ShapeDtypeStruct((S + pad, W), f32),
            jax.ShapeDtypeStruct((S + pad, W), f32),
            jax.ShapeDtypeStruct((H, qt, kb), f32),
        ],
        scratch_shapes=[pltpu.VMEM((qt, kb), f32)],
        compiler_params=pltpu.CompilerParams(dimension_semantics=("parallel", "arbitrary")),
    )(proj, kp, vp, ext, dycat)


def _bias_onehot(qt, ne, nrel_pad):
    m = lax.broadcasted_iota(jnp.int32, (nrel_pad, ne), 1)
    r = lax.broadcasted_iota(jnp.int32, (nrel_pad, ne), 0)
    rel = LEFT_CHUNKS * CHUNK + qt - 1 - m
    hot = (r == jnp.clip(rel, -REL_CLIP, REL_CLIP) + REL_CLIP) & (m < ne - 1)
    return hot.astype(f32)


def _bias_ext(rel_bias_pad, onehot):
    H, _, nr = rel_bias_pad.shape
    ne = onehot.shape[1]

    def body(rb_ref, oh_ref, o_ref):
        o_ref[0] = _dot(rb_ref[0], oh_ref[...], NN, HIGHEST)

    return pl.pallas_call(
        body,
        name="bias_ext",
        grid=(H,),
        in_specs=[pl.BlockSpec((1, 1, nr), lambda h: (h, 0, 0)), pl.BlockSpec((nr, ne), lambda h: (0, 0))],
        out_specs=pl.BlockSpec((1, 1, ne), lambda h: (h, 0, 0)),
        out_shape=jax.ShapeDtypeStruct((H, 1, ne), f32),
        compiler_params=pltpu.CompilerParams(dimension_semantics=("parallel",)),
    )(rel_bias_pad, onehot)


def _bias_bwd(dbias, onehot):
    H, qt, kb = dbias.shape
    nr, ne = onehot.shape

    def body(d_ref, oh_ref, o_ref):
        row = lax.broadcasted_iota(jnp.int32, (qt, qt), 0)
        col = lax.broadcasted_iota(jnp.int32, (qt, qt), 1)
        flip = (row + col == qt - 1).astype(f32)
        x = _dot(flip, d_ref[0], NN, HIGHEST)
        x = jnp.concatenate([x, jnp.zeros((qt, ne - kb), f32)], axis=1)
        de = jnp.sum(pltpu.roll(x, 0, 1, stride=1, stride_axis=0), axis=0, keepdims=True)
        o_ref[0] = _dot(de, oh_ref[...], NT, HIGHEST)

    return pl.pallas_call(
        body,
        name="bias_bwd",
        grid=(H,),
        in_specs=[pl.BlockSpec((1, qt, kb), lambda h: (h, 0, 0)), pl.BlockSpec((nr, ne), lambda h: (0, 0))],
        out_specs=pl.BlockSpec((1, 1, nr), lambda h: (h, 0, 0)),
        out_shape=jax.ShapeDtypeStruct((H, 1, nr), f32),
        compiler_params=pltpu.CompilerParams(dimension_semantics=("parallel",)),
    )(dbias, onehot)


def _place():
    x, y, c = lax.axis_index("x"), lax.axis_index("y"), lax.axis_index("c")
    return x, y, c


def _flip(x, y, k):
    return (1 - x if k & 2 else x), (1 - y if k & 1 else y)


def _region(ref, shard_shape, axis, j, half):
    R, C = shard_shape
    if axis == 0:
        if half is None:
            return ref.at[pl.ds(j * R, R), :]
        return ref.at[pl.ds(j * R + half * (R // 2), R // 2), :]
    if half is None:
        return ref.at[:, pl.ds(j * C, C)]
    return ref.at[pl.ds(half * (R // 2), R // 2), pl.ds(j * C, C)]


def _remote(src, dst, send_sem, recv_sem, dev):
    return pltpu.make_async_remote_copy(src_ref=src, dst_ref=dst, send_sem=send_sem, recv_sem=recv_sem,
                                        device_id=dev, device_id_type=MESH)


def _allgather(shards, axes, splits):
    n = len(shards)
    fulls = []
    for s, ax in zip(shards, axes):
        R, C = s.shape
        fulls.append((N_CHIPS * R, C) if ax == 0 else (R, N_CHIPS * C))

    def body(*refs):
        ins, outs = refs[:n], refs[n:2 * n]
        send1, recv1, send2, recv2, lsem = refs[2 * n:]
        x, y, c = _place()
        jo = 2 * x + y
        locs, sends = [], []
        for m in range(n):
            shp = shards[m].shape
            loc = pltpu.make_async_copy(ins[m], _region(outs[m], shp, axes[m], jo, None), lsem.at[m])
            loc.start()
            locs.append(loc)
        for m in range(n):
            shp = shards[m].shape
            half = c if splits[m] else None
            src = ins[m].at[pl.ds(c * (shp[0] // 2), shp[0] // 2), :] if splits[m] else ins[m]
            for k in (1, 2, 3):
                px, py = _flip(x, y, k)
                cp = _remote(src, _region(outs[m], shp, axes[m], jo, half), send1.at[3 * m + k - 1], recv1.at[3 * m + k - 1], (px, py, c))
                cp.start()
                sends.append(cp)
        for m in range(n):
            shp = shards[m].shape
            half = c if splits[m] else None
            for k in (1, 2, 3):
                px, py = _flip(x, y, k)
                land = _region(outs[m], shp, axes[m], 2 * px + py, half)
                _remote(land, land, send1.at[3 * m + k - 1], recv1.at[3 * m + k - 1], (x, y, c)).wait_recv()
                if splits[m]:
                    fw = _remote(land, land, send2.at[3 * m + k - 1], recv2.at[3 * m + k - 1], (x, y, 1 - c))
                    fw.start()
                    sends.append(fw)
        for m in range(n):
            if not splits[m]:
                continue
            shp = shards[m].shape
            for k in (1, 2, 3):
                px, py = _flip(x, y, k)
                land = _region(outs[m], shp, axes[m], 2 * px + py, 1 - c)
                _remote(land, land, send2.at[3 * m + k - 1], recv2.at[3 * m + k - 1], (x, y, c)).wait_recv()
        for cp in sends:
            cp.wait_send()
        for loc in locs:
            loc.wait()

    sems = pltpu.SemaphoreType.DMA((3 * n,))
    return pl.pallas_call(
        body,
        name="weights_allgather",
        in_specs=[_hbm()] * n,
        out_specs=[_hbm()] * n,
        out_shape=[jax.ShapeDtypeStruct(fs, s.dtype) for fs, s in zip(fulls, shards)],
        scratch_shapes=[sems, sems, sems, sems, pltpu.SemaphoreType.DMA((n,))],
    )(*shards)


def _rs_sibling(dws, shard_shapes, axes):
    n = len(dws)

    def body(*refs):
        ins, mine, got = refs[:n], refs[n:2 * n], refs[2 * n:3 * n]
        send, recv, lsem = refs[3 * n:]
        x, y, c = _place()
        locs, cps = [], []
        for m in range(n):
            for j in range(N_CHIPS):
                lc = pltpu.make_async_copy(_region(ins[m], shard_shapes[m], axes[m], j, c), mine[m].at[j], lsem.at[N_CHIPS * m + j])
                lc.start()
                locs.append(lc)
                cp = _remote(_region(ins[m], shard_shapes[m], axes[m], j, 1 - c), got[m].at[j],
                             send.at[N_CHIPS * m + j], recv.at[N_CHIPS * m + j], (x, y, 1 - c))
                cp.start()
                cps.append(cp)
        for cp in cps:
            cp.wait_recv()
        for cp in cps:
            cp.wait_send()
        for lc in locs:
            lc.wait()

    halves = [jax.ShapeDtypeStruct((N_CHIPS, s[0] // 2, s[1]), d.dtype) for s, d in zip(shard_shapes, dws)]
    sems = pltpu.SemaphoreType.DMA((N_CHIPS * n,))
    outs = pl.pallas_call(
        body,
        name="grads_to_sibling",
        in_specs=[_hbm()] * n,
        out_specs=[_hbm()] * (2 * n),
        out_shape=halves + halves,
        scratch_shapes=[sems, sems, sems],
    )(*dws)
    return outs[:n], outs[n:]


def _rs_chips(ts):
    n = len(ts)

    def body(*refs):
        ins, outs = refs[:n], refs[n:2 * n]
        send, recv, lsem = refs[2 * n:]
        x, y, c = _place()
        jo = 2 * x + y
        locs, cps = [], []
        for m in range(n):
            lc = pltpu.make_async_copy(ins[m].at[jo], outs[m].at[0], lsem.at[m])
            lc.start()
            locs.append(lc)
            for k in (1, 2, 3):
                px, py = _flip(x, y, k)
                cp = _remote(ins[m].at[2 * px + py], outs[m].at[k], send.at[3 * m + k - 1], recv.at[3 * m + k - 1], (px, py, c))
                cp.start()
                cps.append(cp)
        for cp in cps:
            cp.wait_recv()
        for cp in cps:
            cp.wait_send()
        for lc in locs:
            lc.wait()

    sems = pltpu.SemaphoreType.DMA((3 * n,))
    return pl.pallas_call(
        body,
        name="grads_to_chips",
        in_specs=[_hbm()] * n,
        out_specs=[_hbm()] * n,
        out_shape=[jax.ShapeDtypeStruct(t.shape, t.dtype) for t in ts],
        scratch_shapes=[sems, sems, pltpu.SemaphoreType.DMA((n,))],
    )(*ts)


def _rs_join(gs):
    n = len(gs)

    def body(*refs):
        ins, outs = refs[:n], refs[n:2 * n]
        send, recv, lsem = refs[2 * n:]
        x, y, c = _place()
        locs, cps = [], []
        for m in range(n):
            hr = gs[m].shape[0]
            dst = outs[m].at[pl.ds(c * hr, hr), :]
            lc = pltpu.make_async_copy(ins[m], dst, lsem.at[m])
            lc.start()
            locs.append(lc)
            cp = _remote(ins[m], dst, send.at[m], recv.at[m], (x, y, 1 - c))
            cp.start()
            cps.append(cp)
        for cp in cps:
            cp.wait_recv()
        for cp in cps:
            cp.wait_send()
        for lc in locs:
            lc.wait()

    sems = pltpu.SemaphoreType.DMA((n,))
    return pl.pallas_call(
        body,
        name="grads_join_halves",
        in_specs=[_hbm()] * n,
        out_specs=[_hbm()] * n,
        out_shape=[jax.ShapeDtypeStruct((2 * g.shape[0], g.shape[1]), g.dtype) for g in gs],
        scratch_shapes=[sems, sems, sems],
    )(*gs)


def _sum_slots(parts, out_dtype, name):
    nslot, hr, C = parts[0].shape
    tr = _div(hr, 256, SUBLANES)
    over_slots = len(parts) == 1

    if over_slots:
        def body(a_ref, o_ref):
            acc = a_ref[0].astype(f32)
            for s in range(1, nslot):
                acc = acc + a_ref[s].astype(f32)
            o_ref[...] = acc.astype(out_dtype)

        return pl.pallas_call(
            body,
            name=name,
            grid=(hr // tr,),
            in_specs=[pl.BlockSpec((nslot, tr, C), lambda r: (0, r, 0))],
            out_specs=pl.BlockSpec((tr, C), lambda r: (r, 0)),
            out_shape=jax.ShapeDtypeStruct((hr, C), out_dtype),
            compiler_params=pltpu.CompilerParams(dimension_semantics=("parallel",)),
        )(parts[0])

    def body2(a_ref, b_ref, o_ref):
        o_ref[...] = (a_ref[...].astype(f32) + b_ref[...].astype(f32)).astype(out_dtype)

    spec = pl.BlockSpec((1, tr, C), lambda s, r: (s, r, 0))
    return pl.pallas_call(
        body2,
        name=name,
        grid=(nslot, hr // tr),
        in_specs=[spec, spec],
        out_specs=spec,
        out_shape=jax.ShapeDtypeStruct((nslot, hr, C), out_dtype),
        compiler_params=pltpu.CompilerParams(dimension_semantics=("parallel", "parallel")),
    )(*parts)


def _allreduce_small(v):
    rows = v.shape[0]

    def body(v_ref, o_ref, gath, send, recv):
        x, y, c = _place()
        me = 4 * x + 2 * y + c
        gath[pl.ds(me, 1)] = v_ref[...][None]
        cps = []
        for k in range(1, N_DEV):
            peer = (1 - x if k & 4 else x, 1 - y if k & 2 else y, 1 - c if k & 1 else c)
            cp = _remote(v_ref, gath.at[me], send.at[k - 1], recv.at[k - 1], peer)
            cp.start()
            cps.append(cp)
        for cp in cps:
            cp.wait_recv()
        for cp in cps:
            cp.wait_send()
        acc = gath[0]
        for d in range(1, N_DEV):
            acc = acc + gath[d]
        o_ref[...] = acc

    return pl.pallas_call(
        body,
        name="small_allreduce",
        in_specs=[_vmem()],
        out_specs=_vmem(),
        out_shape=jax.ShapeDtypeStruct(v.shape, f32),
        scratch_shapes=[pltpu.VMEM((N_DEV, rows, LANES), f32), pltpu.SemaphoreType.DMA((N_DEV - 1,)), pltpu.SemaphoreType.DMA((N_DEV - 1,))],
    )(v)


def _adamw_math(w, g, m, v):
    m = ADAM_B1 * m + (1.0 - ADAM_B1) * g
    v = ADAM_B2 * v + (1.0 - ADAM_B2) * (g * g)
    m_hat = m / (1.0 - ADAM_B1 ** ADAM_STEP)
    v_hat = v / (1.0 - ADAM_B2 ** ADAM_STEP)
    delta = -ADAM_LR * (m_hat / (jnp.sqrt(v_hat) + ADAM_EPS) + ADAM_WD * w)
    return delta, m, v


def _adamw(w, g, m, v, *, name):
    R, C = w.shape
    tr = _div(R, 128, SUBLANES)

    def body(w_ref, g_ref, m_ref, v_ref, d_ref, mo_ref, vo_ref):
        d, mn, vn = _adamw_math(w_ref[...], g_ref[...], m_ref[...], v_ref[...])
        d_ref[...] = d
        mo_ref[...] = mn
        vo_ref[...] = vn

    spec = pl.BlockSpec((tr, C), lambda r: (r, 0))
    return pl.pallas_call(
        body,
        name=name,
        grid=(R // tr,),
        in_specs=[spec] * 4,
        out_specs=[spec] * 3,
        out_shape=[jax.ShapeDtypeStruct((R, C), f32)] * 3,
        compiler_params=pltpu.CompilerParams(dimension_semantics=("parallel",)),
    )(w, g, m, v)


def _pack(arrs):
    flat = jnp.concatenate([a.reshape(-1).astype(f32) for a in arrs])
    tile = SUBLANES * LANES
    total = -(-flat.shape[0] // tile) * tile
    return jnp.pad(flat, (0, total - flat.shape[0])).reshape(total // LANES, LANES)


def _unpack(buf, shapes):
    flat = buf.reshape(-1)
    out, off = [], 0
    for s in shapes:
        size = int(np.prod(s))
        out.append(flat[off:off + size].reshape(s))
        off += size
    return out


def kernel(x, p, norm_mix, w_in, lb_logits, hg_norm, rel_bias, w_out, norm_ffn, w_up, conv_w, conv_b, w_down, norm_ple, w_ple_gate, w_ple_proj, final_norm, loss_target, m_norm_mix, m_w_in, m_lb_logits, m_hg_norm, m_rel_bias, m_w_out, m_norm_ffn, m_w_up, m_conv_w, m_conv_b, m_w_down, m_norm_ple, m_w_ple_gate, m_w_ple_proj, m_final_norm, v_norm_mix, v_w_in, v_lb_logits, v_hg_norm, v_rel_bias, v_w_out, v_norm_ffn, v_w_up, v_conv_w, v_conv_b, v_w_down, v_norm_ple, v_w_ple_gate, v_w_ple_proj, v_final_norm):
    S, D = x.shape[1], x.shape[2]
    xv, pv, tgt = x[0], p[0, 0], loss_target[0]
    W = (w_in.shape[2] * N_CHIPS) // 7
    H = W // HEAD_DIM
    F = w_down.shape[1] * N_CHIPS
    tc = _div(F // 2, 1408, LANES)
    nj = F // tc
    jx, jy = lax.axis_index("x"), lax.axis_index("y")
    chip = 2 * jx + jy

    big = [w_in[0], w_out[0], w_up[0], w_down[0], w_ple_gate[0], w_ple_proj[0]]
    big_axes = [1, 0, 1, 0, 0, 1]
    cw_pad = jnp.pad(conv_w[0], ((0, SUBLANES - conv_w.shape[1]), (0, 0)))
    gathered = _allgather([b.astype(bf16) for b in big] + [cw_pad], big_axes + [0], [True] * 6 + [False])
    W_in, W_out, W_up, W_down, W_pg, W_pp, cw_all = gathered
    ncw = conv_w.shape[1]
    cw_full = jnp.transpose(cw_all.reshape(N_CHIPS, SUBLANES, -1)[:, :ncw], (1, 0, 2)).reshape(ncw, -1)
    cb = conv_b

    tm = _div(S, 1024, LANES)

    a1 = _rms_fwd(xv, norm_mix, name="rms_mix")
    proj = _mm(a1, W_in, dims="nn", tm=tm, tn=_div(7 * W, 512, LANES), tk=D, out_dtype=f32, name="mm_in")
    y_hg, o_hg, st = _hgrn_fwd(proj, lb_logits, hg_norm, W)

    qt, pad, kb, ne = _att_dims(S)
    nrel = rel_bias.shape[2]
    nrel_pad = -(-nrel // LANES) * LANES
    onehot = _bias_onehot(qt, ne, nrel_pad)
    rb_pad = jnp.pad(rel_bias[0], ((0, 0), (0, nrel_pad - nrel)))[:, None, :]
    ext = _bias_ext(rb_pad, onehot)
    kp = jnp.pad(proj[:, 5 * W:6 * W].astype(bf16), ((pad, 0), (0, 0)))
    vp = jnp.pad(proj[:, 6 * W:7 * W].astype(bf16), ((pad, 0), (0, 0)))
    y_att = _attn_fwd(proj, kp, vp, ext, W)
    ycat = jnp.concatenate([y_hg, y_att], axis=1)

    tn_d = _div(D, 512, LANES)
    h1 = _mm(ycat, W_out, dims="nn", tm=tm, tn=tn_d, tk=D, out_dtype=f32, res=xv, name="mm_out")
    a2 = _rms_fwd(h1, norm_ffn, name="rms_ffn")
    perm = lambda b: (b % 2) * nj + b // 2
    u = _mm(a2, W_up, dims="nn", tm=tm, tn=tc, tk=D, out_dtype=f32, name="mm_up",
            b_idx=lambda i, j, k: (k, perm(j)))
    mact = _convact_fwd(u, cw_full, cb, tc)
    h2 = _mm(mact, W_down, dims="nn", tm=tm, tn=tn_d, tk=_div(F, 1408, LANES), out_dtype=f32, res=h1, name="mm_down")
    a3 = _rms_fwd(h2, norm_ple, name="rms_ple")
    zg = _mm(a3, W_pg, dims="nn", tm=tm, tn=tn_d, tk=D, out_dtype=f32, name="mm_ple_gate")
    p16 = pv.astype(bf16)
    pp = _mm(p16, W_pp, dims="nn", tm=tm, tn=tn_d, tk=pv.shape[1], out_dtype=f32, name="mm_ple_proj")
    dh3, dzg, dpp, loss_part, d_fn = _tail(h2, zg, pp, tgt, final_norm.reshape(1, D))

    tk_s = _div(S, 2048, LANES)
    dW_pp = _mm(p16, dpp, dims="tn", tm=pv.shape[1], tn=tn_d, tk=tk_s, out_dtype=bf16, name="mm_d_ple_proj")
    dW_pg = _mm(a3, dzg, dims="tn", tm=tn_d, tn=tn_d, tk=tk_s, out_dtype=bf16, name="mm_d_ple_gate")
    da3 = _mm(dzg, W_pg, dims="nt", tm=tm, tn=_div(D, 1024, LANES), tk=D, out_dtype=f32, name="mm_da3")
    dh2, dh2b, d_nple = _rms_bwd(h2, norm_ple, da3, dh3, name="rms_ple_bwd")
    dm = _mm(dh2b, W_down, dims="nt", tm=tm, tn=_div(F, 512, LANES), tk=D, out_dtype=f32, name="mm_dm")
    dW_down = _mm(mact, dh2b, dims="tn", tm=_div(F, 512, LANES), tn=tn_d, tk=tk_s, out_dtype=bf16, name="mm_d_down")
    duc, dcw_g, dcw_v, dcb_g, dcb_v = _convact_bwd_a(u, dm, cw_full, cb, tc)
    du = _convact_bwd_b(duc, cw_full, tc)
    dW_up = _mm(a2, du, dims="tn", tm=_div(D, 1024, LANES), tn=tc, tk=tk_s, out_dtype=bf16, name="mm_d_up",
                o_idx=lambda i, j, k: (i, perm(j)))
    da2 = _mm(du, W_up, dims="nt", tm=tm, tn=_div(D, 1024, LANES), tk=tc, out_dtype=f32, name="mm_da2",
              b_idx=lambda i, j, k: (j, perm(k)))
    dh1, dh1b, d_nffn = _rms_bwd(h1, norm_ffn, da2, dh2, name="rms_ffn_bwd")
    dycat = _mm(dh1b, W_out, dims="nt", tm=tm, tn=_div(D, 1024, LANES), tk=D, out_dtype=f32, name="mm_dycat")
    dW_out = _mm(ycat, dh1b, dims="tn", tm=tn_d, tn=tn_d, tk=tk_s, out_dtype=bf16, name="mm_d_out")
    dq_att, dk_pad, dv_pad, dbias = _attn_bwd(proj, kp, vp, ext, dycat, W)
    dhq, dhf, dhi, dhg, d_lbl, d_hgn = _hgrn_bwd(proj, o_hg, st, dycat, lb_logits, hg_norm, W)
    d_rb = _bias_bwd(dbias, onehot)[:, 0, :nrel]
    dproj = jnp.concatenate([dhq, dhf, dhi, dhg, dq_att, dk_pad[pad:].astype(bf16), dv_pad[pad:].astype(bf16)], axis=1)
    dW_in = _mm(a1, dproj, dims="tn", tm=_div(D, 1024, LANES), tn=_div(7 * W, 512, LANES), tk=tk_s, out_dtype=bf16, name="mm_d_in")
    da1 = _mm(dproj, W_in, dims="nt", tm=tm, tn=_div(D, 1024, LANES), tk=_div(7 * W, 1792, LANES), out_dtype=f32, name="mm_da1")
    grad_x, _, d_nmix = _rms_bwd(xv, norm_mix, da1, dh1, name="rms_mix_bwd")

    d_cw = jnp.concatenate([dcw_g, dcw_v], axis=1)
    d_cb = jnp.concatenate([dcb_g, dcb_v], axis=1)
    small_parts = [loss_part[:, :1], d_nmix, d_lbl, d_hgn, d_rb, d_nffn, d_cw, d_cb, d_nple, d_fn]
    small_shapes = [(1, 1), (1, D), lb_logits.shape, hg_norm.shape, (H, nrel), (1, D), (ncw, 2 * F), (1, 2 * F), (1, D), (1, D)]
    red = _unpack(_allreduce_small(_pack(small_parts)), small_shapes)
    loss = red[0].reshape(())
    g_nmix, g_lbl, g_hgn, g_rb, g_nffn, g_cw_all, g_cb, g_nple, g_fn = red[1:]
    csh = conv_w.shape[2]
    g_cw = lax.dynamic_slice(g_cw_all, (0, chip * csh), (ncw, csh))
    small_g = [g_nmix, g_lbl, g_hgn, g_rb[None], g_nffn, g_cw[None], g_cb, g_nple, g_fn.reshape(D)]
    small_w = [norm_mix, lb_logits, hg_norm, rel_bias, norm_ffn, conv_w, conv_b, norm_ple, final_norm]
    small_m = [m_norm_mix, m_lb_logits, m_hg_norm, m_rel_bias, m_norm_ffn, m_conv_w, m_conv_b, m_norm_ple, m_final_norm]
    small_v = [v_norm_mix, v_lb_logits, v_hg_norm, v_rel_bias, v_norm_ffn, v_conv_w, v_conv_b, v_norm_ple, v_final_norm]
    shapes_s = [w.shape for w in small_w]
    sd, sm, sv = _adamw(_pack(small_w), _pack(small_g), _pack(small_m), _pack(small_v), name="adamw_small")
    small_g = [g.reshape(s) for g, s in zip(small_g, shapes_s)]
    small_d, small_nm, small_nv = _unpack(sd, shapes_s), _unpack(sm, shapes_s), _unpack(sv, shapes_s)

    dws = [dW_in, dW_out, dW_up, dW_down, dW_pg, dW_pp]
    shard_shapes = [b.shape for b in big]
    mine, got = _rs_sibling(dws, shard_shapes, big_axes)
    pair = [_sum_slots([a, b], bf16, name=f"grad_pair_sum_{i}") for i, (a, b) in enumerate(zip(mine, got))]
    slots = _rs_chips(pair)
    halves = [_sum_slots([s], f32, name=f"grad_chip_sum_{i}") for i, s in enumerate(slots)]
    g_big = _rs_join(halves)
    big_m = [m_w_in[0], m_w_out[0], m_w_up[0], m_w_down[0], m_w_ple_gate[0], m_w_ple_proj[0]]
    big_v = [v_w_in[0], v_w_out[0], v_w_up[0], v_w_down[0], v_w_ple_gate[0], v_w_ple_proj[0]]
    big_d, big_nm, big_nv = [], [], []
    for i in range(6):
        d_, m_, v_ = _adamw(big[i], g_big[i], big_m[i], big_v[i], name=f"adamw_{i}")
        big_d.append(d_[None])
        big_nm.append(m_[None])
        big_nv.append(v_[None])
    g_big = [g[None] for g in g_big]

    def order(sm_list, bg_list):
        s, b = sm_list, bg_list
        return [s[0], b[0], s[1], s[2], s[3], b[1], s[4], b[2], s[5], s[6], b[3], s[7], b[4], b[5], s[8]]

    return (loss, grad_x[None], *order(small_g, g_big), *order(small_d, big_d), *order(small_nm, big_nm), *order(small_nv, big_nv))
```

```python
import functools

import jax
import jax.numpy as jnp
import numpy as np
from jax import lax
from jax.experimental import pallas as pl
from jax.experimental.pallas import tpu as pltpu

f32 = jnp.float32
bf16 = jnp.bfloat16

CHUNK = 64
HEAD_DIM = 128
LEFT_CHUNKS = 8
REL_CLIP = 128
EPS = 1e-6
HG_SUB = 16
HG_HEADS_PER_STEP = 2
ATT_Q_ROWS = 256
ADAM_LR, ADAM_B1, ADAM_B2, ADAM_EPS, ADAM_WD, ADAM_STEP = 0.001, 0.9, 0.999, 1e-08, 0.01, 10
LANES = 128
SUBLANES = 8
N_CHIPS = 4
N_DEV = 8
MESH = pl.DeviceIdType.MESH
NEG = float(np.finfo(np.float32).min)

NN = (((1,), (0,)), ((), ()))
NT = (((1,), (1,)), ((), ()))
TN = (((0,), (0,)), ((), ()))
HIGHEST = lax.Precision.HIGHEST


def _dot(a, b, dims, precision=None):
    return lax.dot_general(a, b, dims, preferred_element_type=f32, precision=precision)


def _sigmoid(v):
    return 1.0 / (1.0 + jnp.exp(-v))


def _div(n, pref, mult):
    best = None
    d = mult
    while d <= min(n, pref):
        if n % d == 0:
            best = d
        d += mult
    return best if best is not None else n


def _hbm():
    return pl.BlockSpec(memory_space=pltpu.HBM)


def _vmem():
    return pl.BlockSpec(memory_space=pltpu.VMEM)


def _mm(a, b, *, dims, tm, tn, tk, out_dtype, name, res=None, a_idx=None, b_idx=None, o_idx=None, out_shape=None):
    if dims == "nn":
        (M, K), (_, N) = a.shape, b.shape
        a_blk, b_blk, dn = (tm, tk), (tk, tn), NN
        a_def, b_def = (lambda i, j, k: (i, k)), (lambda i, j, k: (k, j))
    elif dims == "nt":
        (M, K), (N, _) = a.shape, b.shape
        a_blk, b_blk, dn = (tm, tk), (tn, tk), NT
        a_def, b_def = (lambda i, j, k: (i, k)), (lambda i, j, k: (j, k))
    else:
        (K, M), (_, N) = a.shape, b.shape
        a_blk, b_blk, dn = (tk, tm), (tk, tn), TN
        a_def, b_def = (lambda i, j, k: (k, i)), (lambda i, j, k: (k, j))
    assert M % tm == 0 and N % tn == 0 and K % tk == 0, (name, M, N, K, tm, tn, tk)
    nk = K // tk
    has_res = res is not None

    def body(*refs):
        a_ref, b_ref = refs[0], refs[1]
        res_ref = refs[2] if has_res else None
        o_ref = refs[3] if has_res else refs[2]
        part = _dot(a_ref[...].astype(bf16), b_ref[...].astype(bf16), dn)

        def finish(acc):
            if has_res:
                acc = acc + res_ref[...]
            o_ref[...] = acc.astype(out_dtype)

        if nk == 1:
            finish(part)
        else:
            acc_ref = refs[-1]
            k = pl.program_id(2)

            @pl.when(k == 0)
            def _():
                acc_ref[...] = part

            @pl.when(k > 0)
            def _():
                acc_ref[...] += part

            @pl.when(k == nk - 1)
            def _():
                finish(acc_ref[...])

    o_map = o_idx or (lambda i, j, k: (i, j))
    in_specs = [pl.BlockSpec(a_blk, a_idx or a_def), pl.BlockSpec(b_blk, b_idx or b_def)]
    args = [a, b]
    if has_res:
        in_specs.append(pl.BlockSpec((tm, tn), o_map))
        args.append(res)
    return pl.pallas_call(
        body,
        name=name,
        grid=(M // tm, N // tn, nk),
        in_specs=in_specs,
        out_specs=pl.BlockSpec((tm, tn), o_map),
        out_shape=jax.ShapeDtypeStruct(out_shape or (M, N), out_dtype),
        scratch_shapes=[pltpu.VMEM((tm, tn), f32)] if nk > 1 else [],
        compiler_params=pltpu.CompilerParams(dimension_semantics=("parallel", "parallel", "arbitrary")),
    )(*args)


def _rms_fwd(xv, g, *, name):
    S, D = xv.shape
    ts = _div(S, 512, SUBLANES)

    def body(x_ref, g_ref, o_ref):
        v = x_ref[...]
        r = lax.rsqrt(jnp.mean(v * v, axis=-1, keepdims=True) + EPS)
        o_ref[...] = (v * r * g_ref[...]).astype(bf16)

    return pl.pallas_call(
        body,
        name=name,
        grid=(S // ts,),
        in_specs=[pl.BlockSpec((ts, D), lambda i: (i, 0)), pl.BlockSpec((1, D), lambda i: (0, 0))],
        out_specs=pl.BlockSpec((ts, D), lambda i: (i, 0)),
        out_shape=jax.ShapeDtypeStruct((S, D), bf16),
        compiler_params=pltpu.CompilerParams(dimension_semantics=("parallel",)),
    )(xv, g)


def _rms_bwd(xv, g, dy, dres, *, name):
    S, D = xv.shape
    ts = _div(S, 256, SUBLANES)

    def body(x_ref, g_ref, dy_ref, dres_ref, dx_ref, dxb_ref, dg_ref):
        i = pl.program_id(0)
        v = x_ref[...]
        r = lax.rsqrt(jnp.mean(v * v, axis=-1, keepdims=True) + EPS)
        vn = v * r
        d = dy_ref[...]
        part = jnp.sum(d * vn, axis=0, keepdims=True)

        @pl.when(i == 0)
        def _():
            dg_ref[...] = part

        @pl.when(i > 0)
        def _():
            dg_ref[...] += part

        t = d * g_ref[...]
        dx = dres_ref[...] + r * (t - vn * jnp.mean(t * vn, axis=-1, keepdims=True))
        dx_ref[...] = dx
        dxb_ref[...] = dx.astype(bf16)

    row = pl.BlockSpec((ts, D), lambda i: (i, 0))
    vec = pl.BlockSpec((1, D), lambda i: (0, 0))
    return pl.pallas_call(
        body,
        name=name,
        grid=(S // ts,),
        in_specs=[row, vec, row, row],
        out_specs=[row, row, vec],
        out_shape=[jax.ShapeDtypeStruct((S, D), f32), jax.ShapeDtypeStruct((S, D), bf16), jax.ShapeDtypeStruct((1, D), f32)],
        compiler_params=pltpu.CompilerParams(dimension_semantics=("arbitrary",)),
    )(xv, g, dy, dres)


def _tail(h2, zg, pp, tgt, fn):
    S, D = h2.shape
    ts = _div(S, 256, SUBLANES)

    def body(h_ref, z_ref, p_ref, t_ref, fn_ref, dh_ref, dz_ref, dp_ref, loss_ref, dfn_ref):
        i = pl.program_id(0)
        sg = _sigmoid(z_ref[...])
        ppv = p_ref[...]
        h3 = h_ref[...] + sg * ppv
        r = lax.rsqrt(jnp.mean(h3 * h3, axis=-1, keepdims=True) + EPS)
        hn = h3 * r
        fnv = fn_ref[...]
        e = hn * fnv - t_ref[...]
        lpart = 0.5 * jnp.sum(jnp.mean(e * e, axis=-1, keepdims=True), axis=0, keepdims=True)
        dy = e * (1.0 / D)
        dpart = jnp.sum(dy * hn, axis=0, keepdims=True)

        @pl.when(i == 0)
        def _():
            loss_ref[...] = jnp.broadcast_to(lpart, loss_ref.shape)
            dfn_ref[...] = dpart

        @pl.when(i > 0)
        def _():
            loss_ref[...] += jnp.broadcast_to(lpart, loss_ref.shape)
            dfn_ref[...] += dpart

        t = dy * fnv
        dh3 = r * (t - hn * jnp.mean(t * hn, axis=-1, keepdims=True))
        dh_ref[...] = dh3
        dz_ref[...] = (dh3 * ppv * sg * (1.0 - sg)).astype(bf16)
        dp_ref[...] = (dh3 * sg).astype(bf16)

    row = pl.BlockSpec((ts, D), lambda i: (i, 0))
    vec = pl.BlockSpec((1, D), lambda i: (0, 0))
    one = pl.BlockSpec((1, LANES), lambda i: (0, 0))
    return pl.pallas_call(
        body,
        name="ple_final_loss",
        grid=(S // ts,),
        in_specs=[row, row, row, row, vec],
        out_specs=[row, row, row, one, vec],
        out_shape=[
            jax.ShapeDtypeStruct((S, D), f32),
            jax.ShapeDtypeStruct((S, D), bf16),
            jax.ShapeDtypeStruct((S, D), bf16),
            jax.ShapeDtypeStruct((1, LANES), f32),
            jax.ShapeDtypeStruct((1, D), f32),
        ],
        compiler_params=pltpu.CompilerParams(dimension_semantics=("arbitrary",)),
    )(h2, zg, pp, tgt, fn)


def _shift_down(v, halo, k):
    r = pltpu.roll(v, k, 0)
    hr = pltpu.roll(halo, k, 0)
    row = lax.broadcasted_iota(jnp.int32, hr.shape, 0)
    top = jnp.where(row < k, hr, r[0:SUBLANES])
    return jnp.concatenate([top, r[SUBLANES:]], axis=0)


def _shift_up(v, halo, k):
    n = v.shape[0]
    r = pltpu.roll(v, n - k, 0)
    hr = pltpu.roll(halo, SUBLANES - k, 0)
    row = lax.broadcasted_iota(jnp.int32, hr.shape, 0)
    bot = jnp.where(row >= SUBLANES - k, hr, r[n - SUBLANES:])
    return jnp.concatenate([r[: n - SUBLANES], bot], axis=0)


def _conv_specs(S, F, tc, ts):
    nj = F // tc
    rows8 = ts // SUBLANES
    main = pl.BlockSpec((ts, 2 * tc), lambda j, i: (i, j))
    prev = pl.BlockSpec((SUBLANES, 2 * tc), lambda j, i: (jnp.maximum(i * rows8 - 1, 0), j))
    nxt = pl.BlockSpec((SUBLANES, 2 * tc), lambda j, i: (jnp.minimum((i + 1) * rows8, S // SUBLANES - 1), j))
    wg = pl.BlockSpec((3, tc), lambda j, i: (0, j))
    wv = pl.BlockSpec((3, tc), lambda j, i: (0, nj + j))
    bg = pl.BlockSpec((1, tc), lambda j, i: (0, j))
    bv = pl.BlockSpec((1, tc), lambda j, i: (0, nj + j))
    half = pl.BlockSpec((ts, tc), lambda j, i: (i, j))
    return nj, main, prev, nxt, wg, wv, bg, bv, half


def _conv_pre(u_ref, h_ref, wg_ref, wv_ref, bg_ref, bv_ref, tc):
    i = pl.program_id(1)
    u = u_ref[...]
    halo = jnp.where(i == 0, 0.0, h_ref[...])
    u1 = _shift_down(u, halo, 1)
    u2 = _shift_down(u, halo, 2)
    w = jnp.concatenate([wg_ref[...], wv_ref[...]], axis=1)
    b = jnp.concatenate([bg_ref[...], bv_ref[...]], axis=1)
    uc = b + w[0:1] * u2 + w[1:2] * u1 + w[2:3] * u
    return u, u1, u2, uc[:, :tc], uc[:, tc:]


def _convact_fwd(u, cw, cb, tc):
    S, F2 = u.shape
    F = F2 // 2
    ts = _div(S, 512, SUBLANES)
    nj, main, prev, nxt, wg, wv, bg, bv, half = _conv_specs(S, F, tc, ts)

    def body(u_ref, h_ref, wg_ref, wv_ref, bg_ref, bv_ref, m_ref):
        _, _, _, g, v = _conv_pre(u_ref, h_ref, wg_ref, wv_ref, bg_ref, bv_ref, tc)
        m_ref[...] = (g * _sigmoid(g) * v).astype(bf16)

    return pl.pallas_call(
        body,
        name="convact_fwd",
        grid=(nj, S // ts),
        in_specs=[main, prev, wg, wv, bg, bv],
        out_specs=half,
        out_shape=jax.ShapeDtypeStruct((S, F), bf16),
        compiler_params=pltpu.CompilerParams(dimension_semantics=("parallel", "parallel")),
    )(u, u, cw, cw, cb, cb)


def _convact_bwd_a(u, dm, cw, cb, tc):
    S, F2 = u.shape
    F = F2 // 2
    ts = _div(S, 512, SUBLANES)
    nj, main, prev, nxt, wg, wv, bg, bv, half = _conv_specs(S, F, tc, ts)

    def body(u_ref, h_ref, dm_ref, wg_ref, wv_ref, bg_ref, bv_ref, duc_ref, dwg_ref, dwv_ref, dbg_ref, dbv_ref):
        i = pl.program_id(1)
        u0, u1, u2, g, v = _conv_pre(u_ref, h_ref, wg_ref, wv_ref, bg_ref, bv_ref, tc)
        d = dm_ref[...]
        sg = _sigmoid(g)
        dg = d * v * sg * (1.0 + g * (1.0 - sg))
        dv = d * g * sg
        duc = jnp.concatenate([dg, dv], axis=1)
        duc_ref[...] = duc
        db = jnp.sum(duc, axis=0, keepdims=True)
        dw = jnp.concatenate(
            [jnp.sum(duc * u2, axis=0, keepdims=True), jnp.sum(duc * u1, axis=0, keepdims=True), jnp.sum(duc * u0, axis=0, keepdims=True)],
            axis=0,
        )

        @pl.when(i == 0)
        def _():
            dwg_ref[...] = dw[:, :tc]
            dwv_ref[...] = dw[:, tc:]
            dbg_ref[...] = db[:, :tc]
            dbv_ref[...] = db[:, tc:]

        @pl.when(i > 0)
        def _():
            dwg_ref[...] += dw[:, :tc]
            dwv_ref[...] += dw[:, tc:]
            dbg_ref[...] += db[:, :tc]
            dbv_ref[...] += db[:, tc:]

    w_out = pl.BlockSpec((3, tc), lambda j, i: (0, j))
    b_out = pl.BlockSpec((1, tc), lambda j, i: (0, j))
    return pl.pallas_call(
        body,
        name="convact_bwd_a",
        grid=(nj, S // ts),
        in_specs=[main, prev, half, wg, wv, bg, bv],
        out_specs=[main, w_out, w_out, b_out, b_out],
        out_shape=[
            jax.ShapeDtypeStruct((S, F2), f32),
            jax.ShapeDtypeStruct((3, F), f32),
            jax.ShapeDtypeStruct((3, F), f32),
            jax.ShapeDtypeStruct((1, F), f32),
            jax.ShapeDtypeStruct((1, F), f32),
        ],
        compiler_params=pltpu.CompilerParams(dimension_semantics=("parallel", "arbitrary")),
    )(u, u, dm, cw, cw, cb, cb)


def _convact_bwd_b(duc, cw, tc):
    S, F2 = duc.shape
    F = F2 // 2
    ts = _div(S, 512, SUBLANES)
    nj, main, prev, nxt, wg, wv, bg, bv, half = _conv_specs(S, F, tc, ts)
    ni = S // ts

    def body(d_ref, h_ref, wg_ref, wv_ref, du_ref):
        i = pl.program_id(1)
        d = d_ref[...]
        halo = jnp.where(i == ni - 1, 0.0, h_ref[...])
        d1 = _shift_up(d, halo, 1)
        d2 = _shift_up(d, halo, 2)
        w = jnp.concatenate([wg_ref[...], wv_ref[...]], axis=1)
        du_ref[...] = (w[2:3] * d + w[1:2] * d1 + w[0:1] * d2).astype(bf16)

    return pl.pallas_call(
        body,
        name="convact_bwd_b",
        grid=(nj, ni),
        in_specs=[main, nxt, wg, wv],
        out_specs=main,
        out_shape=jax.ShapeDtypeStruct((S, F2), bf16),
        compiler_params=pltpu.CompilerParams(dimension_semantics=("parallel", "parallel")),
    )(duc, duc, cw, cw)


def _hg_gates(qp, fp, lbl):
    lb = _sigmoid(lbl[0:1] - lbl[1:2])
    sig = _sigmoid(fp)
    sigm = _sigmoid(-fp)
    f = lb + (1.0 - lb) * sig
    k = (1.0 - lb) * sigm
    sq = _sigmoid(qp)
    qf = qp * sq
    row = lax.broadcasted_iota(jnp.int32, (CHUNK, CHUNK), 0)
    col = lax.broadcasted_iota(jnp.int32, (CHUNK, CHUNK), 1)
    b = _dot((row >= col).astype(f32), jnp.log(f), NN, HIGHEST)
    return lb, sig, sigm, f, k, sq, qf, b


def _hg_block(qf, k, b, I):
    r0, n = I * HG_SUB, (I + 1) * HG_SUB
    base = b[r0 - 1:r0] if I > 0 else jnp.zeros((1, HEAD_DIM), f32)
    eq = jnp.exp(b[r0:n] - base)
    ek = jnp.exp(base - b[0:n])
    qt = qf[r0:n] * eq
    kt = k[0:n] * ek
    row = lax.broadcasted_iota(jnp.int32, (HG_SUB, n), 0) + r0
    col = lax.broadcasted_iota(jnp.int32, (HG_SUB, n), 1)
    return r0, n, eq, ek, qt, kt, col <= row


def _hgrn_specs(S, W, hp, reverse):
    nc = S // CHUNK
    ngrp = W // (hp * HEAD_DIM)
    cw = hp * HEAD_DIM
    cidx = (lambda c: nc - 1 - c) if reverse else (lambda c: c)

    def proj(off):
        return pl.BlockSpec((CHUNK, cw), lambda h, c: (cidx(c), off * ngrp + h))

    act = pl.BlockSpec((CHUNK, cw), lambda h, c: (cidx(c), h))
    lbl = pl.BlockSpec((2, cw), lambda h, c: (0, h))
    gn = pl.BlockSpec((1, HEAD_DIM), lambda h, c: (0, 0))
    st = pl.BlockSpec((hp, 1, HEAD_DIM, HEAD_DIM), lambda h, c: (h, cidx(c), 0, 0))
    return nc, ngrp, proj, act, lbl, gn, st


def _hgrn_fwd(proj, lb_logits, hg_norm, W):
    S = proj.shape[0]
    hp = HG_HEADS_PER_STEP
    nc, ngrp, pspec, act, lbl_spec, gn_spec, st_spec = _hgrn_specs(S, W, hp, False)
    nb = CHUNK // HG_SUB

    def body(q_ref, f_ref, i_ref, g_ref, lbl_ref, gn_ref, y_ref, o_ref, st_ref, state):
        c = pl.program_id(1)

        @pl.when(c == 0)
        def _():
            state[...] = jnp.zeros_like(state)

        for p in range(hp):
            sl = slice(p * HEAD_DIM, (p + 1) * HEAD_DIM)
            _, _, _, _, k, _, qf, b = _hg_gates(q_ref[:, sl], f_ref[:, sl], lbl_ref[:, sl])
            v16 = i_ref[:, sl].astype(bf16)
            st_prev = state[p]
            st_ref[p, 0] = st_prev
            o = _dot((qf * jnp.exp(b)).astype(bf16), st_prev.astype(bf16), NT)
            parts = []
            for I in range(nb):
                r0, n, _, _, qt, kt, mask = _hg_block(qf, k, b, I)
                sc = jnp.where(mask, _dot(qt.astype(bf16), kt.astype(bf16), NT), 0.0)
                parts.append(_dot(sc.astype(bf16), v16[0:n], NN))
            o = o + jnp.concatenate(parts, axis=0)
            bl = b[CHUNK - 1:CHUNK]
            kd = k * jnp.exp(bl - b)
            state[p] = st_prev * jnp.exp(bl) + _dot(v16, kd.astype(bf16), TN)
            o_ref[:, sl] = o
            r = lax.rsqrt(jnp.mean(o * o, axis=-1, keepdims=True) + EPS)
            gp = g_ref[:, sl]
            y_ref[:, sl] = (o * r * gn_ref[...] * (gp * _sigmoid(gp))).astype(bf16)

    H = W // HEAD_DIM
    return pl.pallas_call(
        body,
        name="hgrn_fwd",
        grid=(ngrp, nc),
        in_specs=[pspec(0), pspec(1), pspec(2), pspec(3), lbl_spec, gn_spec],
        out_specs=[act, act, st_spec],
        out_shape=[
            jax.ShapeDtypeStruct((S, W), bf16),
            jax.ShapeDtypeStruct((S, W), f32),
            jax.ShapeDtypeStruct((H, nc, HEAD_DIM, HEAD_DIM), f32),
        ],
        scratch_shapes=[pltpu.VMEM((hp, HEAD_DIM, HEAD_DIM), f32)],
        compiler_params=pltpu.CompilerParams(dimension_semantics=("parallel", "arbitrary")),
    )(proj, proj, proj, proj, lb_logits, hg_norm)


def _hgrn_bwd(proj, o_hg, st, dycat, lb_logits, hg_norm, W):
    S = proj.shape[0]
    hp = HG_HEADS_PER_STEP
    nc, ngrp, pspec, act, lbl_spec, gn_spec, st_spec = _hgrn_specs(S, W, hp, True)
    nb = CHUNK // HG_SUB

    def body(q_ref, f_ref, i_ref, g_ref, o_ref, st_ref, dy_ref, lbl_ref, gn_ref,
             dq_ref, df_ref, di_ref, dg_ref, dlbl_ref, dgn_ref, dstate, dk_acc, db_acc, dv_acc, dq_acc):
        h = pl.program_id(0)
        c = pl.program_id(1)

        @pl.when(c == 0)
        def _():
            dstate[...] = jnp.zeros_like(dstate)
            dlbl_ref[...] = jnp.zeros_like(dlbl_ref)

        @pl.when((c == 0) & (h == 0))
        def _():
            dgn_ref[...] = jnp.zeros_like(dgn_ref)

        row = lax.broadcasted_iota(jnp.int32, (CHUNK, CHUNK), 0)
        col = lax.broadcasted_iota(jnp.int32, (CHUNK, CHUNK), 1)
        upper = (col >= row).astype(f32)
        last_row = lax.broadcasted_iota(jnp.int32, (CHUNK, HEAD_DIM), 0) == CHUNK - 1

        for p in range(hp):
            sl = slice(p * HEAD_DIM, (p + 1) * HEAD_DIM)
            qp, fp = q_ref[:, sl], f_ref[:, sl]
            lb, sig, sigm, f, k, sq, qf, b = _hg_gates(qp, fp, lbl_ref[:, sl])
            v16 = i_ref[:, sl].astype(bf16)
            gnv = gn_ref[...]
            gp = g_ref[:, sl]
            sgg = _sigmoid(gp)
            gate = gp * sgg
            o = o_ref[:, sl]
            r = lax.rsqrt(jnp.mean(o * o, axis=-1, keepdims=True) + EPS)
            on = o * r
            dy = dy_ref[:, sl]
            dg_ref[:, sl] = (dy * on * gnv * (sgg * (1.0 + gp * (1.0 - sgg)))).astype(bf16)
            dgn_ref[...] += jnp.sum(dy * on * gate, axis=0, keepdims=True)
            don = dy * gnv * gate
            do = r * (don - on * jnp.mean(don * on, axis=-1, keepdims=True))
            do16 = do.astype(bf16)
            eb = jnp.exp(b)
            A = qf * eb
            bl = b[CHUNK - 1:CHUNK]
            ebl = jnp.exp(bl)
            ekd = jnp.exp(bl - b)
            kd = k * ekd
            st_prev = st_ref[p, 0]
            dst_new = dstate[p]
            dst16 = dst_new.astype(bf16)
            dA = _dot(do16, st_prev.astype(bf16), NN)
            dkd = _dot(v16, dst16, NN)
            dstate[p] = dst_new * ebl + _dot(do16, A.astype(bf16), TN)
            dbl = jnp.sum(dst_new * st_prev, axis=0, keepdims=True) * ebl + jnp.sum(dkd * kd, axis=0, keepdims=True)
            dq_acc[p] = dA * eb
            db_acc[p] = dA * A - dkd * kd + jnp.where(last_row, dbl, 0.0)
            dk_acc[p] = dkd * ekd
            dv_acc[p] = _dot(kd.astype(bf16), dst16, NT)
            for I in range(nb):
                r0, n, eq, ek, qt, kt, mask = _hg_block(qf, k, b, I)
                qt16, kt16 = qt.astype(bf16), kt.astype(bf16)
                sc = jnp.where(mask, _dot(qt16, kt16, NT), 0.0)
                dsc = jnp.where(mask, _dot(do16[r0:n], v16[0:n], NT), 0.0).astype(bf16)
                dv_acc[p, 0:n, :] += _dot(sc.astype(bf16), do16[r0:n], TN)
                dqt = _dot(dsc, kt16, NN)
                dkt = _dot(dsc, qt16, TN)
                dq_acc[p, r0:n, :] += dqt * eq
                dk_acc[p, 0:n, :] += dkt * ek
                db_acc[p, r0:n, :] += dqt * qt16.astype(f32)
                db_acc[p, 0:n, :] -= dkt * kt16.astype(f32)
            dlogf = _dot(upper, db_acc[p], NN, HIGHEST)
            dfg = dlogf / f
            dk = dk_acc[p]
            df_ref[:, sl] = ((1.0 - lb) * sig * sigm * (dfg - dk)).astype(bf16)
            dlb = jnp.sum(sigm * (dfg - dk), axis=0, keepdims=True)
            dl0 = dlb * lb * (1.0 - lb)
            dlbl_ref[:, sl] += jnp.concatenate([dl0, -dl0], axis=0)
            dq_ref[:, sl] = (dq_acc[p] * (sq * (1.0 + qp * (1.0 - sq)))).astype(bf16)
            di_ref[:, sl] = dv_acc[p].astype(bf16)

    cw = hp * HEAD_DIM
    dy_spec = pl.BlockSpec((CHUNK, cw), lambda h, c: (nc - 1 - c, h))
    acc = pltpu.VMEM((hp, CHUNK, HEAD_DIM), f32)
    return pl.pallas_call(
        body,
        name="hgrn_bwd",
        grid=(ngrp, nc),
        in_specs=[pspec(0), pspec(1), pspec(2), pspec(3), act, st_spec, dy_spec, lbl_spec, gn_spec],
        out_specs=[act, act, act, act, lbl_spec, gn_spec],
        out_shape=[jax.ShapeDtypeStruct((S, W), bf16)] * 4
        + [jax.ShapeDtypeStruct((2, W), f32), jax.ShapeDtypeStruct((1, HEAD_DIM), f32)],
        scratch_shapes=[pltpu.VMEM((hp, HEAD_DIM, HEAD_DIM), f32), acc, acc, acc, acc],
        compiler_params=pltpu.CompilerParams(dimension_semantics=("arbitrary", "arbitrary")),
    )(proj, proj, proj, proj, o_hg, st, dycat, lb_logits, hg_norm)


def _att_dims(S):
    qt = ATT_Q_ROWS if S % ATT_Q_ROWS == 0 else CHUNK
    pad = LEFT_CHUNKS * CHUNK
    kb = pad + qt
    return qt, pad, kb, kb + qt


def _att_scores(q16, kb16, bias, start, qt, pad, kb):
    s = _dot(q16, kb16, NT) * (HEAD_DIM ** -0.5) + bias
    row = lax.broadcasted_iota(jnp.int32, (qt, kb), 0)
    col = lax.broadcasted_iota(jnp.int32, (qt, kb), 1)
    lo = jnp.bitwise_and(row, -CHUNK)
    ok = (col >= lo) & (col < lo + pad + CHUNK) & (col + start >= pad)
    s = jnp.where(ok, s, NEG)
    e = jnp.exp(s - jnp.max(s, axis=-1, keepdims=True))
    return e / jnp.sum(e, axis=-1, keepdims=True)


def _att_bias(ext_ref, qt, kb, ne):
    e = jnp.broadcast_to(ext_ref[0], (qt, ne))
    return pltpu.roll(e, ne - qt + 1, 1, stride=1, stride_axis=0)[:, :kb]


def _attn_fwd(proj, kp, vp, ext, W):
    S = proj.shape[0]
    H = W // HEAD_DIM
    qt, pad, kb, ne = _att_dims(S)

    def body(q_ref, k_ref, v_ref, ext_ref, o_ref, bias):
        g = pl.program_id(1)

        @pl.when(g == 0)
        def _():
            bias[...] = _att_bias(ext_ref, qt, kb, ne)

        start = pl.multiple_of(g * qt, qt)
        pn = _att_scores(q_ref[...].astype(bf16), k_ref[pl.ds(start, kb), :], bias[...], start, qt, pad, kb)
        o_ref[...] = _dot(pn.astype(bf16), v_ref[pl.ds(start, kb), :], NN).astype(bf16)

    return pl.pallas_call(
        body,
        name="attn_fwd",
        grid=(H, S // qt),
        in_specs=[
            pl.BlockSpec((qt, HEAD_DIM), lambda h, g: (g, 4 * H + h)),
            pl.BlockSpec((S + pad, HEAD_DIM), lambda h, g: (0, h)),
            pl.BlockSpec((S + pad, HEAD_DIM), lambda h, g: (0, h)),
            pl.BlockSpec((1, 1, ne), lambda h, g: (h, 0, 0)),
        ],
        out_specs=pl.BlockSpec((qt, HEAD_DIM), lambda h, g: (g, h)),
        out_shape=jax.ShapeDtypeStruct((S, W), bf16),
        scratch_shapes=[pltpu.VMEM((qt, kb), f32)],
        compiler_params=pltpu.CompilerParams(dimension_semantics=("parallel", "arbitrary")),
    )(proj, kp, vp, ext)


def _attn_bwd(proj, kp, vp, ext, dycat, W):
    S = proj.shape[0]
    H = W // HEAD_DIM
    qt, pad, kb, ne = _att_dims(S)
    scale = HEAD_DIM ** -0.5

    def body(q_ref, k_ref, v_ref, ext_ref, do_ref, dq_ref, dk_ref, dv_ref, db_ref, bias):
        g = pl.program_id(1)

        @pl.when(g == 0)
        def _():
            bias[...] = _att_bias(ext_ref, qt, kb, ne)
            dk_ref[...] = jnp.zeros_like(dk_ref)
            dv_ref[...] = jnp.zeros_like(dv_ref)
            db_ref[...] = jnp.zeros_like(db_ref)

        start = pl.multiple_of(g * qt, qt)
        q16 = q_ref[...].astype(bf16)
        kb16 = k_ref[pl.ds(start, kb), :]
        vb16 = v_ref[pl.ds(start, kb), :]
        pn = _att_scores(q16, kb16, bias[...], start, qt, pad, kb)
        do16 = do_ref[...].astype(bf16)
        dpn = _dot(do16, vb16, NT)
        dv_ref[pl.ds(start, kb), :] += _dot(pn.astype(bf16), do16, TN)
        ds = pn * (dpn - jnp.sum(dpn * pn, axis=-1, keepdims=True))
        db_ref[0] += ds
        ds16 = ds.astype(bf16)
        dq_ref[...] = (_dot(ds16, kb16, NN) * scale).astype(bf16)
        dk_ref[pl.ds(start, kb), :] += _dot(ds16, q16, TN) * scale

    kv_spec = pl.BlockSpec((S + pad, HEAD_DIM), lambda h, g: (0, h))
    return pl.pallas_call(
        body,
        name="attn_bwd",
        grid=(H, S // qt),
        in_specs=[
            pl.BlockSpec((qt, HEAD_DIM), lambda h, g: (g, 4 * H + h)),
            kv_spec,
            kv_spec,
            pl.BlockSpec((1, 1, ne), lambda h, g: (h, 0, 0)),
            pl.BlockSpec((qt, HEAD_DIM), lambda h, g: (g, H + h)),
        ],
        out_specs=[
            pl.BlockSpec((qt, HEAD_DIM), lambda h, g: (g, h)),
            kv_spec,
            kv_spec,
            pl.BlockSpec((1, qt, kb), lambda h, g: (h, 0, 0)),
        ],
        out_shape=[
            jax.ShapeDtypeStruct((S, W), bf16),
            jax.ShapeDtypeStruct((S + pad, W), f32),
            jax.ShapeDtypeStruct((S + pad, W), f32),
            jax.ShapeDtypeStruct((H, qt, kb), f32),
        ],
        scratch_shapes=[pltpu.VMEM((qt, kb), f32)],
        compiler_params=pltpu.CompilerParams(dimension_semantics=("parallel", "arbitrary")),
    )(proj, kp, vp, ext, dycat)


def _bias_onehot(qt, ne, nrel_pad):
    m = lax.broadcasted_iota(jnp.int32, (nrel_pad, ne), 1)
    r = lax.broadcasted_iota(jnp.int32, (nrel_pad, ne), 0)
    rel = LEFT_CHUNKS * CHUNK + qt - 1 - m
    hot = (r == jnp.clip(rel, -REL_CLIP, REL_CLIP) + REL_CLIP) & (m < ne - 1)
    return hot.astype(f32)


def _bias_ext(rel_bias_pad, onehot):
    H, _, nr = rel_bias_pad.shape
    ne = onehot.shape[1]

    def body(rb_ref, oh_ref, o_ref):
        o_ref[0] = _dot(rb_ref[0], oh_ref[...], NN, HIGHEST)

    return pl.pallas_call(
        body,
        name="bias_ext",
        grid=(H,),
        in_specs=[pl.BlockSpec((1, 1, nr), lambda h: (h, 0, 0)), pl.BlockSpec((nr, ne), lambda h: (0, 0))],
        out_specs=pl.BlockSpec((1, 1, ne), lambda h: (h, 0, 0)),
        out_shape=jax.ShapeDtypeStruct((H, 1, ne), f32),
        compiler_params=pltpu.CompilerParams(dimension_semantics=("parallel",)),
    )(rel_bias_pad, onehot)


def _bias_bwd(dbias, onehot):
    H, qt, kb = dbias.shape
    nr, ne = onehot.shape

    def body(d_ref, oh_ref, o_ref):
        row = lax.broadcasted_iota(jnp.int32, (qt, qt), 0)
        col = lax.broadcasted_iota(jnp.int32, (qt, qt), 1)
        flip = (row + col == qt - 1).astype(f32)
        x = _dot(flip, d_ref[0], NN, HIGHEST)
        x = jnp.concatenate([x, jnp.zeros((qt, ne - kb), f32)], axis=1)
        de = jnp.sum(pltpu.roll(x, 0, 1, stride=1, stride_axis=0), axis=0, keepdims=True)
        o_ref[0] = _dot(de, oh_ref[...], NT, HIGHEST)

    return pl.pallas_call(
        body,
        name="bias_bwd",
        grid=(H,),
        in_specs=[pl.BlockSpec((1, qt, kb), lambda h: (h, 0, 0)), pl.BlockSpec((nr, ne), lambda h: (0, 0))],
        out_specs=pl.BlockSpec((1, 1, nr), lambda h: (h, 0, 0)),
        out_shape=jax.ShapeDtypeStruct((H, 1, nr), f32),
        compiler_params=pltpu.CompilerParams(dimension_semantics=("parallel",)),
    )(dbias, onehot)


def _place():
    x, y, c = lax.axis_index("x"), lax.axis_index("y"), lax.axis_index("c")
    return x, y, c


def _flip(x, y, k):
    return (1 - x if k & 2 else x), (1 - y if k & 1 else y)


def _region(ref, shard_shape, axis, j, half):
    R, C = shard_shape
    if axis == 0:
        if half is None:
            return ref.at[pl.ds(j * R, R), :]
        return ref.at[pl.ds(j * R + half * (R // 2), R // 2), :]
    if half is None:
        return ref.at[:, pl.ds(j * C, C)]
    return ref.at[pl.ds(half * (R // 2), R // 2), pl.ds(j * C, C)]


def _remote(src, dst, send_sem, recv_sem, dev):
    return pltpu.make_async_remote_copy(src_ref=src, dst_ref=dst, send_sem=send_sem, recv_sem=recv_sem,
                                        device_id=dev, device_id_type=MESH)


def _cast_place(shard, axis, where, *, name):
    R, C = shard.shape
    tr = _div(R, 256, 2 * SUBLANES)
    nr = R // tr
    full = (N_CHIPS * R, C) if axis == 0 else (R, N_CHIPS * C)
    omap = (lambda r, s: (s[0] * nr + r, 0)) if axis == 0 else (lambda r, s: (r, s[0]))

    def body(s_ref, x_ref, o_ref):
        o_ref[...] = x_ref[...].astype(bf16)

    return pl.pallas_call(
        body,
        name=name,
        grid_spec=pltpu.PrefetchScalarGridSpec(
            num_scalar_prefetch=1,
            grid=(nr,),
            in_specs=[pl.BlockSpec((tr, C), lambda r, s: (r, 0))],
            out_specs=pl.BlockSpec((tr, C), omap),
        ),
        out_shape=jax.ShapeDtypeStruct(full, bf16),
        compiler_params=pltpu.CompilerParams(dimension_semantics=("parallel",)),
    )(where, shard)


def _allgather(fulls, shard_shapes, axes, splits):
    n = len(fulls)

    def body(*refs):
        ins, outs = refs[:n], refs[n:2 * n]
        send1, recv1, send2, recv2 = refs[2 * n:]
        x, y, c = _place()
        jo = 2 * x + y
        sends = []
        for m in range(n):
            half = c if splits[m] else None
            for k in (1, 2, 3):
                px, py = _flip(x, y, k)
                cp = _remote(_region(ins[m], shard_shapes[m], axes[m], jo, half), _region(outs[m], shard_shapes[m], axes[m], jo, half),
                             send1.at[3 * m + k - 1], recv1.at[3 * m + k - 1], (px, py, c))
                cp.start()
                sends.append(cp)
        for m in range(n):
            half = c if splits[m] else None
            for k in (1, 2, 3):
                px, py = _flip(x, y, k)
                land = _region(outs[m], shard_shapes[m], axes[m], 2 * px + py, half)
                _remote(land, land, send1.at[3 * m + k - 1], recv1.at[3 * m + k - 1], (x, y, c)).wait_recv()
                if splits[m]:
                    fw = _remote(land, land, send2.at[3 * m + k - 1], recv2.at[3 * m + k - 1], (x, y, 1 - c))
                    fw.start()
                    sends.append(fw)
        for m in range(n):
            if not splits[m]:
                continue
            for k in (1, 2, 3):
                px, py = _flip(x, y, k)
                land = _region(outs[m], shard_shapes[m], axes[m], 2 * px + py, 1 - c)
                _remote(land, land, send2.at[3 * m + k - 1], recv2.at[3 * m + k - 1], (x, y, c)).wait_recv()
        for cp in sends:
            cp.wait_send()

    sems = pltpu.SemaphoreType.DMA((3 * n,))
    return pl.pallas_call(
        body,
        name="weights_allgather",
        in_specs=[_hbm()] * n,
        out_specs=[_hbm()] * n,
        out_shape=[jax.ShapeDtypeStruct(f.shape, f.dtype) for f in fulls],
        input_output_aliases={m: m for m in range(n)},
        scratch_shapes=[sems, sems, sems, sems],
    )(*fulls)


def _rs_sibling(dws, shard_shapes, axes):
    n = len(dws)

    def body(*refs):
        ins, got = refs[:n], refs[n:2 * n]
        send, recv = refs[2 * n:]
        x, y, c = _place()
        cps = []
        for m in range(n):
            for j in range(N_CHIPS):
                cp = _remote(_region(ins[m], shard_shapes[m], axes[m], j, 1 - c), got[m].at[j],
                             send.at[N_CHIPS * m + j], recv.at[N_CHIPS * m + j], (x, y, 1 - c))
                cp.start()
                cps.append(cp)
        for cp in cps:
            cp.wait_recv()
        for cp in cps:
            cp.wait_send()

    sems = pltpu.SemaphoreType.DMA((N_CHIPS * n,))
    return pl.pallas_call(
        body,
        name="grads_to_sibling",
        in_specs=[_hbm()] * n,
        out_specs=[_hbm()] * n,
        out_shape=[jax.ShapeDtypeStruct((N_CHIPS, s[0] // 2, s[1]), d.dtype) for s, d in zip(shard_shapes, dws)],
        scratch_shapes=[sems, sems],
    )(*dws)


def _pair_sum(dw, got, shard_shape, axis, where, *, name):
    R, C = shard_shape
    hr = R // 2
    tr = _div(hr, 256, 2 * SUBLANES)
    nr = hr // tr
    dmap = (lambda j, r, s: ((2 * j + s[1]) * nr + r, 0)) if axis == 0 else (lambda j, r, s: (s[1] * nr + r, j))
    slot = pl.BlockSpec((1, tr, C), lambda j, r, s: (j, r, 0))

    def body(s_ref, d_ref, g_ref, o_ref):
        o_ref[0] = (d_ref[...].astype(f32) + g_ref[0].astype(f32)).astype(bf16)

    return pl.pallas_call(
        body,
        name=name,
        grid_spec=pltpu.PrefetchScalarGridSpec(
            num_scalar_prefetch=1,
            grid=(N_CHIPS, nr),
            in_specs=[pl.BlockSpec((tr, C), dmap), slot],
            out_specs=slot,
        ),
        out_shape=jax.ShapeDtypeStruct((N_CHIPS, hr, C), bf16),
        compiler_params=pltpu.CompilerParams(dimension_semantics=("parallel", "parallel")),
    )(where, dw, got)


def _rs_chips(ts):
    n = len(ts)

    def body(*refs):
        ins, outs = refs[:n], refs[n:2 * n]
        send, recv = refs[2 * n:]
        x, y, c = _place()
        cps = []
        for m in range(n):
            for k in (1, 2, 3):
                px, py = _flip(x, y, k)
                cp = _remote(ins[m].at[2 * px + py], outs[m].at[k - 1], send.at[3 * m + k - 1], recv.at[3 * m + k - 1], (px, py, c))
                cp.start()
                cps.append(cp)
        for cp in cps:
            cp.wait_recv()
        for cp in cps:
            cp.wait_send()

    sems = pltpu.SemaphoreType.DMA((3 * n,))
    return pl.pallas_call(
        body,
        name="grads_to_chips",
        in_specs=[_hbm()] * n,
        out_specs=[_hbm()] * n,
        out_shape=[jax.ShapeDtypeStruct((3,) + t.shape[1:], t.dtype) for t in ts],
        scratch_shapes=[sems, sems],
    )(*ts)


def _chip_sum(pair, others, where, *, name):
    _, hr, C = pair.shape
    tr = _div(hr, 256, 2 * SUBLANES)
    nr = hr // tr

    def body(s_ref, p_ref, o_ref, g_ref):
        g_ref[...] = p_ref[0].astype(f32) + o_ref[0].astype(f32) + o_ref[1].astype(f32) + o_ref[2].astype(f32)

    return pl.pallas_call(
        body,
        name=name,
        grid_spec=pltpu.PrefetchScalarGridSpec(
            num_scalar_prefetch=1,
            grid=(nr,),
            in_specs=[pl.BlockSpec((1, tr, C), lambda r, s: (s[0], r, 0)), pl.BlockSpec((3, tr, C), lambda r, s: (0, r, 0))],
            out_specs=pl.BlockSpec((tr, C), lambda r, s: (s[1] * nr + r, 0)),
        ),
        out_shape=jax.ShapeDtypeStruct((2 * hr, C), f32),
        compiler_params=pltpu.CompilerParams(dimension_semantics=("parallel",)),
    )(where, pair, others)


def _rs_join(gs):
    n = len(gs)

    def body(*refs):
        ins, outs = refs[:n], refs[n:2 * n]
        send, recv = refs[2 * n:]
        x, y, c = _place()
        cps = []
        for m in range(n):
            hr = gs[m].shape[0] // 2
            cp = _remote(ins[m].at[pl.ds(c * hr, hr), :], outs[m].at[pl.ds(c * hr, hr), :], send.at[m], recv.at[m], (x, y, 1 - c))
            cp.start()
            cps.append(cp)
        for cp in cps:
            cp.wait_recv()
        for cp in cps:
            cp.wait_send()

    sems = pltpu.SemaphoreType.DMA((n,))
    return pl.pallas_call(
        body,
        name="grads_join_halves",
        in_specs=[_hbm()] * n,
        out_specs=[_hbm()] * n,
        out_shape=[jax.ShapeDtypeStruct(g.shape, g.dtype) for g in gs],
        input_output_aliases={m: m for m in range(n)},
        scratch_shapes=[sems, sems],
    )(*gs)


def _allreduce_small(v):
    rows = v.shape[0]

    def body(v_ref, o_ref, gath, send, recv):
        x, y, c = _place()
        me = 4 * x + 2 * y + c
        gath[pl.ds(me, 1)] = v_ref[...][None]
        cps = []
        for k in range(1, N_DEV):
            peer = (1 - x if k & 4 else x, 1 - y if k & 2 else y, 1 - c if k & 1 else c)
            cp = _remote(v_ref, gath.at[me], send.at[k - 1], recv.at[k - 1], peer)
            cp.start()
            cps.append(cp)
        for cp in cps:
            cp.wait_recv()
        for cp in cps:
            cp.wait_send()
        acc = gath[0]
        for d in range(1, N_DEV):
            acc = acc + gath[d]
        o_ref[...] = acc

    return pl.pallas_call(
        body,
        name="small_allreduce",
        in_specs=[_vmem()],
        out_specs=_vmem(),
        out_shape=jax.ShapeDtypeStruct(v.shape, f32),
        scratch_shapes=[pltpu.VMEM((N_DEV, rows, LANES), f32), pltpu.SemaphoreType.DMA((N_DEV - 1,)), pltpu.SemaphoreType.DMA((N_DEV - 1,))],
    )(v)


def _adamw_math(w, g, m, v):
    m = ADAM_B1 * m + (1.0 - ADAM_B1) * g
    v = ADAM_B2 * v + (1.0 - ADAM_B2) * (g * g)
    m_hat = m / (1.0 - ADAM_B1 ** ADAM_STEP)
    v_hat = v / (1.0 - ADAM_B2 ** ADAM_STEP)
    delta = -ADAM_LR * (m_hat / (jnp.sqrt(v_hat) + ADAM_EPS) + ADAM_WD * w)
    return delta, m, v


def _adamw(w, g, m, v, *, name):
    R, C = w.shape
    tr = _div(R, 128, SUBLANES)

    def body(w_ref, g_ref, m_ref, v_ref, d_ref, mo_ref, vo_ref):
        d, mn, vn = _adamw_math(w_ref[...], g_ref[...], m_ref[...], v_ref[...])
        d_ref[...] = d
        mo_ref[...] = mn
        vo_ref[...] = vn

    spec = pl.BlockSpec((tr, C), lambda r: (r, 0))
    return pl.pallas_call(
        body,
        name=name,
        grid=(R // tr,),
        in_specs=[spec] * 4,
        out_specs=[spec] * 3,
        out_shape=[jax.ShapeDtypeStruct((R, C), f32)] * 3,
        compiler_params=pltpu.CompilerParams(dimension_semantics=("parallel",)),
    )(w, g, m, v)


def _pack(arrs):
    flat = jnp.concatenate([a.reshape(-1).astype(f32) for a in arrs])
    tile = SUBLANES * LANES
    total = -(-flat.shape[0] // tile) * tile
    return jnp.pad(flat, (0, total - flat.shape[0])).reshape(total // LANES, LANES)


def _unpack(buf, shapes):
    flat = buf.reshape(-1)
    out, off = [], 0
    for s in shapes:
        size = int(np.prod(s))
        out.append(flat[off:off + size].reshape(s))
        off += size
    return out


def kernel(x, p, norm_mix, w_in, lb_logits, hg_norm, rel_bias, w_out, norm_ffn, w_up, conv_w, conv_b, w_down, norm_ple, w_ple_gate, w_ple_proj, final_norm, loss_target, m_norm_mix, m_w_in, m_lb_logits, m_hg_norm, m_rel_bias, m_w_out, m_norm_ffn, m_w_up, m_conv_w, m_conv_b, m_w_down, m_norm_ple, m_w_ple_gate, m_w_ple_proj, m_final_norm, v_norm_mix, v_w_in, v_lb_logits, v_hg_norm, v_rel_bias, v_w_out, v_norm_ffn, v_w_up, v_conv_w, v_conv_b, v_w_down, v_norm_ple, v_w_ple_gate, v_w_ple_proj, v_final_norm):
    S, D = x.shape[1], x.shape[2]
    xv, pv, tgt = x[0], p[0, 0], loss_target[0]
    W = (w_in.shape[2] * N_CHIPS) // 7
    H = W // HEAD_DIM
    F = w_down.shape[1] * N_CHIPS
    tc = _div(F // 2, 1408, LANES)
    nj = F // tc
    jx, jy = lax.axis_index("x"), lax.axis_index("y")
    chip = 2 * jx + jy

    big = [w_in[0], w_out[0], w_up[0], w_down[0], w_ple_gate[0], w_ple_proj[0]]
    big_axes = [1, 0, 1, 0, 0, 1]
    where = jnp.stack([chip, lax.axis_index("c")]).astype(jnp.int32)
    cw_pad = jnp.pad(conv_w[0], ((0, SUBLANES - conv_w.shape[1]), (0, 0)))
    cw_mine = lax.dynamic_update_slice(jnp.zeros((N_CHIPS * SUBLANES, cw_pad.shape[1]), f32), cw_pad, (chip * SUBLANES, 0))
    placed = [_cast_place(b, ax, where, name=f"cast_place_{i}") for i, (b, ax) in enumerate(zip(big, big_axes))]
    gathered = _allgather(placed + [cw_mine], [b.shape for b in big] + [cw_pad.shape], big_axes + [0], [True] * 6 + [False])
    W_in, W_out, W_up, W_down, W_pg, W_pp, cw_all = gathered
    ncw = conv_w.shape[1]
    cw_full = jnp.transpose(cw_all.reshape(N_CHIPS, SUBLANES, -1)[:, :ncw], (1, 0, 2)).reshape(ncw, -1)
    cb = conv_b

    tm = _div(S, 1024, LANES)

    a1 = _rms_fwd(xv, norm_mix, name="rms_mix")
    proj = _mm(a1, W_in, dims="nn", tm=tm, tn=_div(7 * W, 512, LANES), tk=D, out_dtype=f32, name="mm_in")
    y_hg, o_hg, st = _hgrn_fwd(proj, lb_logits, hg_norm, W)

    qt, pad, kb, ne = _att_dims(S)
    nrel = rel_bias.shape[2]
    nrel_pad = -(-nrel // LANES) * LANES
    onehot = _bias_onehot(qt, ne, nrel_pad)
    rb_pad = jnp.pad(rel_bias[0], ((0, 0), (0, nrel_pad - nrel)))[:, None, :]
    ext = _bias_ext(rb_pad, onehot)
    kp = jnp.pad(proj[:, 5 * W:6 * W].astype(bf16), ((pad, 0), (0, 0)))
    vp = jnp.pad(proj[:, 6 * W:7 * W].astype(bf16), ((pad, 0), (0, 0)))
    y_att = _attn_fwd(proj, kp, vp, ext, W)
    ycat = jnp.concatenate([y_hg, y_att], axis=1)

    tn_d = _div(D, 512, LANES)
    h1 = _mm(ycat, W_out, dims="nn", tm=tm, tn=tn_d, tk=D, out_dtype=f32, res=xv, name="mm_out")
    a2 = _rms_fwd(h1, norm_ffn, name="rms_ffn")
    perm = lambda b: (b % 2) * nj + b // 2
    u = _mm(a2, W_up, dims="nn", tm=tm, tn=tc, tk=D, out_dtype=f32, name="mm_up",
            b_idx=lambda i, j, k: (k, perm(j)))
    mact = _convact_fwd(u, cw_full, cb, tc)
    h2 = _mm(mact, W_down, dims="nn", tm=tm, tn=tn_d, tk=_div(F, 1408, LANES), out_dtype=f32, res=h1, name="mm_down")
    a3 = _rms_fwd(h2, norm_ple, name="rms_ple")
    zg = _mm(a3, W_pg, dims="nn", tm=tm, tn=tn_d, tk=D, out_dtype=f32, name="mm_ple_gate")
    p16 = pv.astype(bf16)
    pp = _mm(p16, W_pp, dims="nn", tm=tm, tn=tn_d, tk=pv.shape[1], out_dtype=f32, name="mm_ple_proj")
    dh3, dzg, dpp, loss_part, d_fn = _tail(h2, zg, pp, tgt, final_norm.reshape(1, D))

    tk_s = _div(S, 2048, LANES)
    dW_pp = _mm(p16, dpp, dims="tn", tm=pv.shape[1], tn=tn_d, tk=tk_s, out_dtype=bf16, name="mm_d_ple_proj")
    dW_pg = _mm(a3, dzg, dims="tn", tm=tn_d, tn=tn_d, tk=tk_s, out_dtype=bf16, name="mm_d_ple_gate")
    da3 = _mm(dzg, W_pg, dims="nt", tm=tm, tn=_div(D, 1024, LANES), tk=D, out_dtype=f32, name="mm_da3")
    dh2, dh2b, d_nple = _rms_bwd(h2, norm_ple, da3, dh3, name="rms_ple_bwd")
    dm = _mm(dh2b, W_down, dims="nt", tm=tm, tn=_div(F, 512, LANES), tk=D, out_dtype=f32, name="mm_dm")
    dW_down = _mm(mact, dh2b, dims="tn", tm=_div(F, 512, LANES), tn=tn_d, tk=tk_s, out_dtype=bf16, name="mm_d_down")
    duc, dcw_g, dcw_v, dcb_g, dcb_v = _convact_bwd_a(u, dm, cw_full, cb, tc)
    du = _convact_bwd_b(duc, cw_full, tc)
    dW_up = _mm(a2, du, dims="tn", tm=_div(D, 1024, LANES), tn=tc, tk=tk_s, out_dtype=bf16, name="mm_d_up",
                o_idx=lambda i, j, k: (i, perm(j)))
    da2 = _mm(du, W_up, dims="nt", tm=tm, tn=_div(D, 1024, LANES), tk=tc, out_dtype=f32, name="mm_da2",
              b_idx=lambda i, j, k: (j, perm(k)))
    dh1, dh1b, d_nffn = _rms_bwd(h1, norm_ffn, da2, dh2, name="rms_ffn_bwd")
    dycat = _mm(dh1b, W_out, dims="nt", tm=tm, tn=_div(D, 1024, LANES), tk=D, out_dtype=f32, name="mm_dycat")
    dW_out = _mm(ycat, dh1b, dims="tn", tm=tn_d, tn=tn_d, tk=tk_s, out_dtype=bf16, name="mm_d_out")
    dq_att, dk_pad, dv_pad, dbias = _attn_bwd(proj, kp, vp, ext, dycat, W)
    dhq, dhf, dhi, dhg, d_lbl, d_hgn = _hgrn_bwd(proj, o_hg, st, dycat, lb_logits, hg_norm, W)
    d_rb = _bias_bwd(dbias, onehot)[:, 0, :nrel]
    dproj = jnp.concatenate([dhq, dhf, dhi, dhg, dq_att, dk_pad[pad:].astype(bf16), dv_pad[pad:].astype(bf16)], axis=1)
    dW_in = _mm(a1, dproj, dims="tn", tm=_div(D, 1024, LANES), tn=_div(7 * W, 512, LANES), tk=tk_s, out_dtype=bf16, name="mm_d_in")
    da1 = _mm(dproj, W_in, dims="nt", tm=tm, tn=_div(D, 1024, LANES), tk=_div(7 * W, 1792, LANES), out_dtype=f32, name="mm_da1")
    grad_x, _, d_nmix = _rms_bwd(xv, norm_mix, da1, dh1, name="rms_mix_bwd")

    d_cw = jnp.concatenate([dcw_g, dcw_v], axis=1)
    d_cb = jnp.concatenate([dcb_g, dcb_v], axis=1)
    small_parts = [loss_part[:, :1], d_nmix, d_lbl, d_hgn, d_rb, d_nffn, d_cw, d_cb, d_nple, d_fn]
    small_shapes = [(1, 1), (1, D), lb_logits.shape, hg_norm.shape, (H, nrel), (1, D), (ncw, 2 * F), (1, 2 * F), (1, D), (1, D)]
    red = _unpack(_allreduce_small(_pack(small_parts)), small_shapes)
    loss = red[0].reshape(())
    g_nmix, g_lbl, g_hgn, g_rb, g_nffn, g_cw_all, g_cb, g_nple, g_fn = red[1:]
    csh = conv_w.shape[2]
    g_cw = lax.dynamic_slice(g_cw_all, (0, chip * csh), (ncw, csh))
    small_g = [g_nmix, g_lbl, g_hgn, g_rb[None], g_nffn, g_cw[None], g_cb, g_nple, g_fn.reshape(D)]
    small_w = [norm_mix, lb_logits, hg_norm, rel_bias, norm_ffn, conv_w, conv_b, norm_ple, final_norm]
    small_m = [m_norm_mix, m_lb_logits, m_hg_norm, m_rel_bias, m_norm_ffn, m_conv_w, m_conv_b, m_norm_ple, m_final_norm]
    small_v = [v_norm_mix, v_lb_logits, v_hg_norm, v_rel_bias, v_norm_ffn, v_conv_w, v_conv_b, v_norm_ple, v_final_norm]
    shapes_s = [w.shape for w in small_w]
    sd, sm, sv = _adamw(_pack(small_w), _pack(small_g), _pack(small_m), _pack(small_v), name="adamw_small")
    small_g = [g.reshape(s) for g, s in zip(small_g, shapes_s)]
    small_d, small_nm, small_nv = _unpack(sd, shapes_s), _unpack(sm, shapes_s), _unpack(sv, shapes_s)

    dws = [dW_in, dW_out, dW_up, dW_down, dW_pg, dW_pp]
    shard_shapes = [b.shape for b in big]
    got = _rs_sibling(dws, shard_shapes, big_axes)
    pair = [_pair_sum(dws[i], got[i], shard_shapes[i], big_axes[i], where, name=f"grad_pair_sum_{i}") for i in range(6)]
    others = _rs_chips(pair)
    halves = [_chip_sum(pair[i], others[i], where, name=f"grad_chip_sum_{i}") for i in range(6)]
    g_big = _rs_join(halves)
    big_m = [m_w_in[0], m_w_out[0], m_w_up[0], m_w_down[0], m_w_ple_gate[0], m_w_ple_proj[0]]
    big_v = [v_w_in[0], v_w_out[0], v_w_up[0], v_w_down[0], v_w_ple_gate[0], v_w_ple_proj[0]]
    big_d, big_nm, big_nv = [], [], []
    for i in range(6):
        d_, m_, v_ = _adamw(big[i], g_big[i], big_m[i], big_v[i], name=f"adamw_{i}")
        big_d.append(d_[None])
        big_nm.append(m_[None])
        big_nv.append(v_[None])
    g_big = [g[None] for g in g_big]

    def order(sm_list, bg_list):
        s, b = sm_list, bg_list
        return [s[0], b[0], s[1], s[2], s[3], b[1], s[4], b[2], s[5], s[6], b[3], s[7], b[4], b[5], s[8]]

    return (loss, grad_x[None], *order(small_g, g_big), *order(small_d, big_d), *order(small_nm, big_nm), *order(small_nv, big_nv))
```

```python
import functools
from typing import Callable, NamedTuple

import jax
import jax.numpy as jnp
import numpy as np
from jax import lax
from jax.experimental import pallas as pl
from jax.experimental.pallas import tpu as pltpu

f32 = jnp.float32
bf16 = jnp.bfloat16

CHUNK = 64
HEAD_DIM = 128
LEFT_CHUNKS = 8
REL_CLIP = 128
EPS = 1e-6
HG_SUB = 16
HG_HEADS_PER_STEP = 2
ATT_Q_ROWS = 256
ADAM_LR, ADAM_B1, ADAM_B2, ADAM_EPS, ADAM_WD, ADAM_STEP = 0.001, 0.9, 0.999, 1e-08, 0.01, 10
LANES = 128
SUBLANES = 8
N_CHIPS = 4
N_DEV = 8
MESH = pl.DeviceIdType.MESH
NEG = float(np.finfo(np.float32).min)

NN = (((1,), (0,)), ((), ()))
NT = (((1,), (1,)), ((), ()))
TN = (((0,), (0,)), ((), ()))
HIGHEST = lax.Precision.HIGHEST


def _dot(a, b, dims, precision=None):
    return lax.dot_general(a, b, dims, preferred_element_type=f32, precision=precision)


def _sigmoid(v):
    return 1.0 / (1.0 + jnp.exp(-v))


def _div(n, pref, mult):
    best = None
    d = mult
    while d <= min(n, pref):
        if n % d == 0:
            best = d
        d += mult
    return best if best is not None else n


def _hbm():
    return pl.BlockSpec(memory_space=pltpu.HBM)


def _vmem():
    return pl.BlockSpec(memory_space=pltpu.VMEM)


def _mm(a, b, *, dims, tm, tn, tk, out_dtype, name, res=None, a_idx=None, b_idx=None, o_idx=None, out_shape=None):
    if dims == "nn":
        (M, K), (_, N) = a.shape, b.shape
        a_blk, b_blk, dn = (tm, tk), (tk, tn), NN
        a_def, b_def = (lambda i, j, k: (i, k)), (lambda i, j, k: (k, j))
    elif dims == "nt":
        (M, K), (N, _) = a.shape, b.shape
        a_blk, b_blk, dn = (tm, tk), (tn, tk), NT
        a_def, b_def = (lambda i, j, k: (i, k)), (lambda i, j, k: (j, k))
    else:
        (K, M), (_, N) = a.shape, b.shape
        a_blk, b_blk, dn = (tk, tm), (tk, tn), TN
        a_def, b_def = (lambda i, j, k: (k, i)), (lambda i, j, k: (k, j))
    assert M % tm == 0 and N % tn == 0 and K % tk == 0, (name, M, N, K, tm, tn, tk)
    nk = K // tk
    has_res = res is not None

    def body(*refs):
        a_ref, b_ref = refs[0], refs[1]
        res_ref = refs[2] if has_res else None
        o_ref = refs[3] if has_res else refs[2]
        part = _dot(a_ref[...].astype(bf16), b_ref[...].astype(bf16), dn)

        def finish(acc):
            if has_res:
                acc = acc + res_ref[...]
            o_ref[...] = acc.astype(out_dtype)

        if nk == 1:
            finish(part)
        else:
            acc_ref = refs[-1]
            k = pl.program_id(2)

            @pl.when(k == 0)
            def _():
                acc_ref[...] = part

            @pl.when(k > 0)
            def _():
                acc_ref[...] += part

            @pl.when(k == nk - 1)
            def _():
                finish(acc_ref[...])

    o_map = o_idx or (lambda i, j, k: (i, j))
    in_specs = [pl.BlockSpec(a_blk, a_idx or a_def), pl.BlockSpec(b_blk, b_idx or b_def)]
    args = [a, b]
    if has_res:
        in_specs.append(pl.BlockSpec((tm, tn), o_map))
        args.append(res)
    return pl.pallas_call(
        body,
        name=name,
        grid=(M // tm, N // tn, nk),
        in_specs=in_specs,
        out_specs=pl.BlockSpec((tm, tn), o_map),
        out_shape=jax.ShapeDtypeStruct(out_shape or (M, N), out_dtype),
        scratch_shapes=[pltpu.VMEM((tm, tn), f32)] if nk > 1 else [],
        compiler_params=pltpu.CompilerParams(dimension_semantics=("parallel", "parallel", "arbitrary")),
    )(*args)


def _rms_fwd(xv, g, *, name):
    S, D = xv.shape
    ts = _div(S, 512, SUBLANES)

    def body(x_ref, g_ref, o_ref):
        v = x_ref[...]
        r = lax.rsqrt(jnp.mean(v * v, axis=-1, keepdims=True) + EPS)
        o_ref[...] = (v * r * g_ref[...]).astype(bf16)

    return pl.pallas_call(
        body,
        name=name,
        grid=(S // ts,),
        in_specs=[pl.BlockSpec((ts, D), lambda i: (i, 0)), pl.BlockSpec((1, D), lambda i: (0, 0))],
        out_specs=pl.BlockSpec((ts, D), lambda i: (i, 0)),
        out_shape=jax.ShapeDtypeStruct((S, D), bf16),
        compiler_params=pltpu.CompilerParams(dimension_semantics=("parallel",)),
    )(xv, g)


def _rms_bwd(xv, g, dy, dres, *, name):
    S, D = xv.shape
    ts = _div(S, 256, SUBLANES)

    def body(x_ref, g_ref, dy_ref, dres_ref, dx_ref, dxb_ref, dg_ref):
        i = pl.program_id(0)
        v = x_ref[...]
        r = lax.rsqrt(jnp.mean(v * v, axis=-1, keepdims=True) + EPS)
        vn = v * r
        d = dy_ref[...]
        part = jnp.sum(d * vn, axis=0, keepdims=True)

        @pl.when(i == 0)
        def _():
            dg_ref[...] = part

        @pl.when(i > 0)
        def _():
            dg_ref[...] += part

        t = d * g_ref[...]
        dx = dres_ref[...] + r * (t - vn * jnp.mean(t * vn, axis=-1, keepdims=True))
        dx_ref[...] = dx
        dxb_ref[...] = dx.astype(bf16)

    row = pl.BlockSpec((ts, D), lambda i: (i, 0))
    vec = pl.BlockSpec((1, D), lambda i: (0, 0))
    return pl.pallas_call(
        body,
        name=name,
        grid=(S // ts,),
        in_specs=[row, vec, row, row],
        out_specs=[row, row, vec],
        out_shape=[jax.ShapeDtypeStruct((S, D), f32), jax.ShapeDtypeStruct((S, D), bf16), jax.ShapeDtypeStruct((1, D), f32)],
        compiler_params=pltpu.CompilerParams(dimension_semantics=("arbitrary",)),
    )(xv, g, dy, dres)


def _tail(h2, zg, pp, tgt, fn):
    S, D = h2.shape
    ts = _div(S, 256, SUBLANES)

    def body(h_ref, z_ref, p_ref, t_ref, fn_ref, dh_ref, dz_ref, dp_ref, loss_ref, dfn_ref):
        i = pl.program_id(0)
        sg = _sigmoid(z_ref[...])
        ppv = p_ref[...]
        h3 = h_ref[...] + sg * ppv
        r = lax.rsqrt(jnp.mean(h3 * h3, axis=-1, keepdims=True) + EPS)
        hn = h3 * r
        fnv = fn_ref[...]
        e = hn * fnv - t_ref[...]
        lpart = 0.5 * jnp.sum(jnp.mean(e * e, axis=-1, keepdims=True), axis=0, keepdims=True)
        dy = e * (1.0 / D)
        dpart = jnp.sum(dy * hn, axis=0, keepdims=True)

        @pl.when(i == 0)
        def _():
            loss_ref[...] = jnp.broadcast_to(lpart, loss_ref.shape)
            dfn_ref[...] = dpart

        @pl.when(i > 0)
        def _():
            loss_ref[...] += jnp.broadcast_to(lpart, loss_ref.shape)
            dfn_ref[...] += dpart

        t = dy * fnv
        dh3 = r * (t - hn * jnp.mean(t * hn, axis=-1, keepdims=True))
        dh_ref[...] = dh3
        dz_ref[...] = (dh3 * ppv * sg * (1.0 - sg)).astype(bf16)
        dp_ref[...] = (dh3 * sg).astype(bf16)

    row = pl.BlockSpec((ts, D), lambda i: (i, 0))
    vec = pl.BlockSpec((1, D), lambda i: (0, 0))
    one = pl.BlockSpec((1, LANES), lambda i: (0, 0))
    return pl.pallas_call(
        body,
        name="ple_final_loss",
        grid=(S // ts,),
        in_specs=[row, row, row, row, vec],
        out_specs=[row, row, row, one, vec],
        out_shape=[
            jax.ShapeDtypeStruct((S, D), f32),
            jax.ShapeDtypeStruct((S, D), bf16),
            jax.ShapeDtypeStruct((S, D), bf16),
            jax.ShapeDtypeStruct((1, LANES), f32),
            jax.ShapeDtypeStruct((1, D), f32),
        ],
        compiler_params=pltpu.CompilerParams(dimension_semantics=("arbitrary",)),
    )(h2, zg, pp, tgt, fn)


def _shift_down(v, halo, k):
    r = pltpu.roll(v, k, 0)
    hr = pltpu.roll(halo, k, 0)
    row = lax.broadcasted_iota(jnp.int32, hr.shape, 0)
    top = jnp.where(row < k, hr, r[0:SUBLANES])
    return jnp.concatenate([top, r[SUBLANES:]], axis=0)


def _shift_up(v, halo, k):
    n = v.shape[0]
    r = pltpu.roll(v, n - k, 0)
    hr = pltpu.roll(halo, SUBLANES - k, 0)
    row = lax.broadcasted_iota(jnp.int32, hr.shape, 0)
    bot = jnp.where(row >= SUBLANES - k, hr, r[n - SUBLANES:])
    return jnp.concatenate([r[: n - SUBLANES], bot], axis=0)


def _conv_specs(S, F, tc, ts):
    nj = F // tc
    rows8 = ts // SUBLANES
    main = pl.BlockSpec((ts, 2 * tc), lambda j, i: (i, j))
    prev = pl.BlockSpec((SUBLANES, 2 * tc), lambda j, i: (jnp.maximum(i * rows8 - 1, 0), j))
    nxt = pl.BlockSpec((SUBLANES, 2 * tc), lambda j, i: (jnp.minimum((i + 1) * rows8, S // SUBLANES - 1), j))
    wg = pl.BlockSpec((3, tc), lambda j, i: (0, j))
    wv = pl.BlockSpec((3, tc), lambda j, i: (0, nj + j))
    bg = pl.BlockSpec((1, tc), lambda j, i: (0, j))
    bv = pl.BlockSpec((1, tc), lambda j, i: (0, nj + j))
    half = pl.BlockSpec((ts, tc), lambda j, i: (i, j))
    return nj, main, prev, nxt, wg, wv, bg, bv, half


def _conv_pre(u_ref, h_ref, wg_ref, wv_ref, bg_ref, bv_ref, tc):
    i = pl.program_id(1)
    u = u_ref[...]
    halo = jnp.where(i == 0, 0.0, h_ref[...])
    u1 = _shift_down(u, halo, 1)
    u2 = _shift_down(u, halo, 2)
    w = jnp.concatenate([wg_ref[...], wv_ref[...]], axis=1)
    b = jnp.concatenate([bg_ref[...], bv_ref[...]], axis=1)
    uc = b + w[0:1] * u2 + w[1:2] * u1 + w[2:3] * u
    return u, u1, u2, uc[:, :tc], uc[:, tc:]


def _convact_fwd(u, cw, cb, tc):
    S, F2 = u.shape
    F = F2 // 2
    ts = _div(S, 512, SUBLANES)
    nj, main, prev, nxt, wg, wv, bg, bv, half = _conv_specs(S, F, tc, ts)

    def body(u_ref, h_ref, wg_ref, wv_ref, bg_ref, bv_ref, m_ref):
        _, _, _, g, v = _conv_pre(u_ref, h_ref, wg_ref, wv_ref, bg_ref, bv_ref, tc)
        m_ref[...] = (g * _sigmoid(g) * v).astype(bf16)

    return pl.pallas_call(
        body,
        name="convact_fwd",
        grid=(nj, S // ts),
        in_specs=[main, prev, wg, wv, bg, bv],
        out_specs=half,
        out_shape=jax.ShapeDtypeStruct((S, F), bf16),
        compiler_params=pltpu.CompilerParams(dimension_semantics=("parallel", "parallel")),
    )(u, u, cw, cw, cb, cb)


def _convact_bwd_a(u, dm, cw, cb, tc):
    S, F2 = u.shape
    F = F2 // 2
    ts = _div(S, 512, SUBLANES)
    nj, main, prev, nxt, wg, wv, bg, bv, half = _conv_specs(S, F, tc, ts)

    def body(u_ref, h_ref, dm_ref, wg_ref, wv_ref, bg_ref, bv_ref, duc_ref, dwg_ref, dwv_ref, dbg_ref, dbv_ref):
        i = pl.program_id(1)
        u0, u1, u2, g, v = _conv_pre(u_ref, h_ref, wg_ref, wv_ref, bg_ref, bv_ref, tc)
        d = dm_ref[...]
        sg = _sigmoid(g)
        dg = d * v * sg * (1.0 + g * (1.0 - sg))
        dv = d * g * sg
        duc = jnp.concatenate([dg, dv], axis=1)
        duc_ref[...] = duc
        db = jnp.sum(duc, axis=0, keepdims=True)
        dw = jnp.concatenate(
            [jnp.sum(duc * u2, axis=0, keepdims=True), jnp.sum(duc * u1, axis=0, keepdims=True), jnp.sum(duc * u0, axis=0, keepdims=True)],
            axis=0,
        )

        @pl.when(i == 0)
        def _():
            dwg_ref[...] = dw[:, :tc]
            dwv_ref[...] = dw[:, tc:]
            dbg_ref[...] = db[:, :tc]
            dbv_ref[...] = db[:, tc:]

        @pl.when(i > 0)
        def _():
            dwg_ref[...] += dw[:, :tc]
            dwv_ref[...] += dw[:, tc:]
            dbg_ref[...] += db[:, :tc]
            dbv_ref[...] += db[:, tc:]

    w_out = pl.BlockSpec((3, tc), lambda j, i: (0, j))
    b_out = pl.BlockSpec((1, tc), lambda j, i: (0, j))
    return pl.pallas_call(
        body,
        name="convact_bwd_a",
        grid=(nj, S // ts),
        in_specs=[main, prev, half, wg, wv, bg, bv],
        out_specs=[main, w_out, w_out, b_out, b_out],
        out_shape=[
            jax.ShapeDtypeStruct((S, F2), f32),
            jax.ShapeDtypeStruct((3, F), f32),
            jax.ShapeDtypeStruct((3, F), f32),
            jax.ShapeDtypeStruct((1, F), f32),
            jax.ShapeDtypeStruct((1, F), f32),
        ],
        compiler_params=pltpu.CompilerParams(dimension_semantics=("parallel", "arbitrary")),
    )(u, u, dm, cw, cw, cb, cb)


def _convact_bwd_b(duc, cw, tc):
    S, F2 = duc.shape
    F = F2 // 2
    ts = _div(S, 512, SUBLANES)
    nj, main, prev, nxt, wg, wv, bg, bv, half = _conv_specs(S, F, tc, ts)
    ni = S // ts

    def body(d_ref, h_ref, wg_ref, wv_ref, du_ref):
        i = pl.program_id(1)
        d = d_ref[...]
        halo = jnp.where(i == ni - 1, 0.0, h_ref[...])
        d1 = _shift_up(d, halo, 1)
        d2 = _shift_up(d, halo, 2)
        w = jnp.concatenate([wg_ref[...], wv_ref[...]], axis=1)
        du_ref[...] = (w[2:3] * d + w[1:2] * d1 + w[0:1] * d2).astype(bf16)

    return pl.pallas_call(
        body,
        name="convact_bwd_b",
        grid=(nj, ni),
        in_specs=[main, nxt, wg, wv],
        out_specs=main,
        out_shape=jax.ShapeDtypeStruct((S, F2), bf16),
        compiler_params=pltpu.CompilerParams(dimension_semantics=("parallel", "parallel")),
    )(duc, duc, cw, cw)


def _hg_gates(qp, fp, lbl):
    lb = _sigmoid(lbl[0:1] - lbl[1:2])
    sig = _sigmoid(fp)
    sigm = _sigmoid(-fp)
    f = lb + (1.0 - lb) * sig
    k = (1.0 - lb) * sigm
    sq = _sigmoid(qp)
    qf = qp * sq
    row = lax.broadcasted_iota(jnp.int32, (CHUNK, CHUNK), 0)
    col = lax.broadcasted_iota(jnp.int32, (CHUNK, CHUNK), 1)
    b = _dot((row >= col).astype(f32), jnp.log(f), NN, HIGHEST)
    return lb, sig, sigm, f, k, sq, qf, b


def _hg_block(qf, k, b, I):
    r0, n = I * HG_SUB, (I + 1) * HG_SUB
    base = b[r0 - 1:r0] if I > 0 else jnp.zeros((1, HEAD_DIM), f32)
    eq = jnp.exp(b[r0:n] - base)
    ek = jnp.exp(base - b[0:n])
    qt = qf[r0:n] * eq
    kt = k[0:n] * ek
    row = lax.broadcasted_iota(jnp.int32, (HG_SUB, n), 0) + r0
    col = lax.broadcasted_iota(jnp.int32, (HG_SUB, n), 1)
    return r0, n, eq, ek, qt, kt, col <= row


def _hgrn_specs(S, W, hp, reverse):
    nc = S // CHUNK
    ngrp = W // (hp * HEAD_DIM)
    cw = hp * HEAD_DIM
    cidx = (lambda c: nc - 1 - c) if reverse else (lambda c: c)

    def proj(off):
        return pl.BlockSpec((CHUNK, cw), lambda h, c: (cidx(c), off * ngrp + h))

    act = pl.BlockSpec((CHUNK, cw), lambda h, c: (cidx(c), h))
    lbl = pl.BlockSpec((2, cw), lambda h, c: (0, h))
    gn = pl.BlockSpec((1, HEAD_DIM), lambda h, c: (0, 0))
    st = pl.BlockSpec((hp, 1, HEAD_DIM, HEAD_DIM), lambda h, c: (h, cidx(c), 0, 0))
    return nc, ngrp, proj, act, lbl, gn, st


def _grid_call(body, rider, *, name, grid, in_specs, out_specs, out_shape, scratch_shapes, args):
    params = pltpu.CompilerParams(dimension_semantics=("arbitrary",) * len(grid))
    if rider is None:
        return pl.pallas_call(body, name=name, grid=grid, in_specs=in_specs, out_specs=out_specs, out_shape=out_shape,
                              scratch_shapes=scratch_shapes, compiler_params=params)(*args)
    n_in, n_out, n_scr = len(in_specs), len(out_specs), len(scratch_shapes)
    ri, ro = len(rider.ins), len(rider.out_shapes)

    def wrapped(*refs):
        a, b = n_in, n_in + ri
        c, d = b + n_out, b + n_out + ro
        e = d + n_scr
        ids = [pl.program_id(t) for t in range(len(grid))]
        is_first = functools.reduce(jnp.logical_and, [i == 0 for i in ids])
        is_last = functools.reduce(jnp.logical_and, [i == g - 1 for i, g in zip(ids, grid)])

        @pl.when(is_first)
        def _():
            rider.first(refs[a:b], refs[c:d], refs[e:])

        body(*refs[:a], *refs[b:c], *refs[d:e])

        @pl.when(is_last)
        def _():
            rider.last(refs[a:b], refs[c:d], refs[e:])

    return pl.pallas_call(
        wrapped,
        name=name,
        grid=grid,
        in_specs=list(in_specs) + [_hbm()] * ri,
        out_specs=list(out_specs) + [_hbm()] * ro,
        out_shape=list(out_shape) + list(rider.out_shapes),
        input_output_aliases={n_in + i: n_out + o for i, o in rider.aliases.items()},
        scratch_shapes=list(scratch_shapes) + list(rider.sems),
        compiler_params=params,
    )(*args, *rider.ins)


def _hgrn_fwd(proj, lb_logits, hg_norm, W, rider=None):
    S = proj.shape[0]
    hp = HG_HEADS_PER_STEP
    nc, ngrp, pspec, act, lbl_spec, gn_spec, st_spec = _hgrn_specs(S, W, hp, False)
    nb = CHUNK // HG_SUB

    def body(q_ref, f_ref, i_ref, g_ref, lbl_ref, gn_ref, y_ref, o_ref, st_ref, state):
        c = pl.program_id(1)

        @pl.when(c == 0)
        def _():
            state[...] = jnp.zeros_like(state)

        for p in range(hp):
            sl = slice(p * HEAD_DIM, (p + 1) * HEAD_DIM)
            _, _, _, _, k, _, qf, b = _hg_gates(q_ref[:, sl], f_ref[:, sl], lbl_ref[:, sl])
            v16 = i_ref[:, sl].astype(bf16)
            st_prev = state[p]
            st_ref[p, 0] = st_prev
            o = _dot((qf * jnp.exp(b)).astype(bf16), st_prev.astype(bf16), NT)
            parts = []
            for I in range(nb):
                r0, n, _, _, qt, kt, mask = _hg_block(qf, k, b, I)
                sc = jnp.where(mask, _dot(qt.astype(bf16), kt.astype(bf16), NT), 0.0)
                parts.append(_dot(sc.astype(bf16), v16[0:n], NN))
            o = o + jnp.concatenate(parts, axis=0)
            bl = b[CHUNK - 1:CHUNK]
            kd = k * jnp.exp(bl - b)
            state[p] = st_prev * jnp.exp(bl) + _dot(v16, kd.astype(bf16), TN)
            o_ref[:, sl] = o
            r = lax.rsqrt(jnp.mean(o * o, axis=-1, keepdims=True) + EPS)
            gp = g_ref[:, sl]
            y_ref[:, sl] = (o * r * gn_ref[...] * (gp * _sigmoid(gp))).astype(bf16)

    H = W // HEAD_DIM
    return _grid_call(
        body,
        rider,
        name="hgrn_fwd",
        grid=(ngrp, nc),
        in_specs=[pspec(0), pspec(1), pspec(2), pspec(3), lbl_spec, gn_spec],
        out_specs=[act, act, st_spec],
        out_shape=[
            jax.ShapeDtypeStruct((S, W), bf16),
            jax.ShapeDtypeStruct((S, W), f32),
            jax.ShapeDtypeStruct((H, nc, HEAD_DIM, HEAD_DIM), f32),
        ],
        scratch_shapes=[pltpu.VMEM((hp, HEAD_DIM, HEAD_DIM), f32)],
        args=(proj, proj, proj, proj, lb_logits, hg_norm),
    )


def _hgrn_bwd(proj, o_hg, st, dycat, lb_logits, hg_norm, W, rider=None):
    S = proj.shape[0]
    hp = HG_HEADS_PER_STEP
    nc, ngrp, pspec, act, lbl_spec, gn_spec, st_spec = _hgrn_specs(S, W, hp, True)
    nb = CHUNK // HG_SUB

    def body(q_ref, f_ref, i_ref, g_ref, o_ref, st_ref, dy_ref, lbl_ref, gn_ref,
             dq_ref, df_ref, di_ref, dg_ref, dlbl_ref, dgn_ref, dstate, dk_acc, db_acc, dv_acc, dq_acc):
        h = pl.program_id(0)
        c = pl.program_id(1)

        @pl.when(c == 0)
        def _():
            dstate[...] = jnp.zeros_like(dstate)
            dlbl_ref[...] = jnp.zeros_like(dlbl_ref)

        @pl.when((c == 0) & (h == 0))
        def _():
            dgn_ref[...] = jnp.zeros_like(dgn_ref)

        row = lax.broadcasted_iota(jnp.int32, (CHUNK, CHUNK), 0)
        col = lax.broadcasted_iota(jnp.int32, (CHUNK, CHUNK), 1)
        upper = (col >= row).astype(f32)
        last_row = lax.broadcasted_iota(jnp.int32, (CHUNK, HEAD_DIM), 0) == CHUNK - 1

        for p in range(hp):
            sl = slice(p * HEAD_DIM, (p + 1) * HEAD_DIM)
            qp, fp = q_ref[:, sl], f_ref[:, sl]
            lb, sig, sigm, f, k, sq, qf, b = _hg_gates(qp, fp, lbl_ref[:, sl])
            v16 = i_ref[:, sl].astype(bf16)
            gnv = gn_ref[...]
            gp = g_ref[:, sl]
            sgg = _sigmoid(gp)
            gate = gp * sgg
            o = o_ref[:, sl]
            r = lax.rsqrt(jnp.mean(o * o, axis=-1, keepdims=True) + EPS)
            on = o * r
            dy = dy_ref[:, sl]
            dg_ref[:, sl] = (dy * on * gnv * (sgg * (1.0 + gp * (1.0 - sgg)))).astype(bf16)
            dgn_ref[...] += jnp.sum(dy * on * gate, axis=0, keepdims=True)
            don = dy * gnv * gate
            do = r * (don - on * jnp.mean(don * on, axis=-1, keepdims=True))
            do16 = do.astype(bf16)
            eb = jnp.exp(b)
            A = qf * eb
            bl = b[CHUNK - 1:CHUNK]
            ebl = jnp.exp(bl)
            ekd = jnp.exp(bl - b)
            kd = k * ekd
            st_prev = st_ref[p, 0]
            dst_new = dstate[p]
            dst16 = dst_new.astype(bf16)
            dA = _dot(do16, st_prev.astype(bf16), NN)
            dkd = _dot(v16, dst16, NN)
            dstate[p] = dst_new * ebl + _dot(do16, A.astype(bf16), TN)
            dbl = jnp.sum(dst_new * st_prev, axis=0, keepdims=True) * ebl + jnp.sum(dkd * kd, axis=0, keepdims=True)
            dq_acc[p] = dA * eb
            db_acc[p] = dA * A - dkd * kd + jnp.where(last_row, dbl, 0.0)
            dk_acc[p] = dkd * ekd
            dv_acc[p] = _dot(kd.astype(bf16), dst16, NT)
            for I in range(nb):
                r0, n, eq, ek, qt, kt, mask = _hg_block(qf, k, b, I)
                qt16, kt16 = qt.astype(bf16), kt.astype(bf16)
                sc = jnp.where(mask, _dot(qt16, kt16, NT), 0.0)
                dsc = jnp.where(mask, _dot(do16[r0:n], v16[0:n], NT), 0.0).astype(bf16)
                dv_acc[p, 0:n, :] += _dot(sc.astype(bf16), do16[r0:n], TN)
                dqt = _dot(dsc, kt16, NN)
                dkt = _dot(dsc, qt16, TN)
                dq_acc[p, r0:n, :] += dqt * eq
                dk_acc[p, 0:n, :] += dkt * ek
                db_acc[p, r0:n, :] += dqt * qt16.astype(f32)
                db_acc[p, 0:n, :] -= dkt * kt16.astype(f32)
            dlogf = _dot(upper, db_acc[p], NN, HIGHEST)
            dfg = dlogf / f
            dk = dk_acc[p]
            df_ref[:, sl] = ((1.0 - lb) * sig * sigm * (dfg - dk)).astype(bf16)
            dlb = jnp.sum(sigm * (dfg - dk), axis=0, keepdims=True)
            dl0 = dlb * lb * (1.0 - lb)
            dlbl_ref[:, sl] += jnp.concatenate([dl0, -dl0], axis=0)
            dq_ref[:, sl] = (dq_acc[p] * (sq * (1.0 + qp * (1.0 - sq)))).astype(bf16)
            di_ref[:, sl] = dv_acc[p].astype(bf16)

    cw = hp * HEAD_DIM
    dy_spec = pl.BlockSpec((CHUNK, cw), lambda h, c: (nc - 1 - c, h))
    acc = pltpu.VMEM((hp, CHUNK, HEAD_DIM), f32)
    return _grid_call(
        body,
        rider,
        name="hgrn_bwd",
        grid=(ngrp, nc),
        in_specs=[pspec(0), pspec(1), pspec(2), pspec(3), act, st_spec, dy_spec, lbl_spec, gn_spec],
        out_specs=[act, act, act, act, lbl_spec, gn_spec],
        out_shape=[jax.ShapeDtypeStruct((S, W), bf16)] * 4
        + [jax.ShapeDtypeStruct((2, W), f32), jax.ShapeDtypeStruct((1, HEAD_DIM), f32)],
        scratch_shapes=[pltpu.VMEM((hp, HEAD_DIM, HEAD_DIM), f32), acc, acc, acc, acc],
        args=(proj, proj, proj, proj, o_hg, st, dycat, lb_logits, hg_norm),
    )


def _att_dims(S):
    qt = ATT_Q_ROWS if S % ATT_Q_ROWS == 0 else CHUNK
    pad = LEFT_CHUNKS * CHUNK
    kb = pad + qt
    return qt, pad, kb, kb + qt


def _att_scores(q16, kb16, bias, start, qt, pad, kb):
    s = _dot(q16, kb16, NT) * (HEAD_DIM ** -0.5) + bias
    row = lax.broadcasted_iota(jnp.int32, (qt, kb), 0)
    col = lax.broadcasted_iota(jnp.int32, (qt, kb), 1)
    lo = jnp.bitwise_and(row, -CHUNK)
    ok = (col >= lo) & (col < lo + pad + CHUNK) & (col + start >= pad)
    s = jnp.where(ok, s, NEG)
    e = jnp.exp(s - jnp.max(s, axis=-1, keepdims=True))
    return e / jnp.sum(e, axis=-1, keepdims=True)


def _att_bias(ext_ref, qt, kb, ne):
    e = jnp.broadcast_to(ext_ref[0], (qt, ne))
    return pltpu.roll(e, ne - qt + 1, 1, stride=1, stride_axis=0)[:, :kb]


def _attn_fwd(proj, kp, vp, ext, W):
    S = proj.shape[0]
    H = W // HEAD_DIM
    qt, pad, kb, ne = _att_dims(S)

    def body(q_ref, k_ref, v_ref, ext_ref, o_ref, bias):
        g = pl.program_id(1)

        @pl.when(g == 0)
        def _():
            bias[...] = _att_bias(ext_ref, qt, kb, ne)

        start = pl.multiple_of(g * qt, qt)
        pn = _att_scores(q_ref[...].astype(bf16), k_ref[pl.ds(start, kb), :], bias[...], start, qt, pad, kb)
        o_ref[...] = _dot(pn.astype(bf16), v_ref[pl.ds(start, kb), :], NN).astype(bf16)

    return pl.pallas_call(
        body,
        name="attn_fwd",
        grid=(H, S // qt),
        in_specs=[
            pl.BlockSpec((qt, HEAD_DIM), lambda h, g: (g, 4 * H + h)),
            pl.BlockSpec((S + pad, HEAD_DIM), lambda h, g: (0, h)),
            pl.BlockSpec((S + pad, HEAD_DIM), lambda h, g: (0, h)),
            pl.BlockSpec((1, 1, ne), lambda h, g: (h, 0, 0)),
        ],
        out_specs=pl.BlockSpec((qt, HEAD_DIM), lambda h, g: (g, h)),
        out_shape=jax.ShapeDtypeStruct((S, W), bf16),
        scratch_shapes=[pltpu.VMEM((qt, kb), f32)],
        compiler_params=pltpu.CompilerParams(dimension_semantics=("parallel", "arbitrary")),
    )(proj, kp, vp, ext)


def _attn_bwd(proj, kp, vp, ext, dycat, W):
    S = proj.shape[0]
    H = W // HEAD_DIM
    qt, pad, kb, ne = _att_dims(S)
    scale = HEAD_DIM ** -0.5

    def body(q_ref, k_ref, v_ref, ext_ref, do_ref, dq_ref, dk_ref, dv_ref, db_ref, bias):
        g = pl.program_id(1)

        @pl.when(g == 0)
        def _():
            bias[...] = _att_bias(ext_ref, qt, kb, ne)
            dk_ref[...] = jnp.zeros_like(dk_ref)
            dv_ref[...] = jnp.zeros_like(dv_ref)
            db_ref[...] = jnp.zeros_like(db_ref)

        start = pl.multiple_of(g * qt, qt)
        q16 = q_ref[...].astype(bf16)
        kb16 = k_ref[pl.ds(start, kb), :]
        vb16 = v_ref[pl.ds(start, kb), :]
        pn = _att_scores(q16, kb16, bias[...], start, qt, pad, kb)
        do16 = do_ref[...].astype(bf16)
        dpn = _dot(do16, vb16, NT)
        dv_ref[pl.ds(start, kb), :] += _dot(pn.astype(bf16), do16, TN)
        ds = pn * (dpn - jnp.sum(dpn * pn, axis=-1, keepdims=True))
        db_ref[0] += ds
        ds16 = ds.astype(bf16)
        dq_ref[...] = (_dot(ds16, kb16, NN) * scale).astype(bf16)
        dk_ref[pl.ds(start, kb), :] += _dot(ds16, q16, TN) * scale

    kv_spec = pl.BlockSpec((S + pad, HEAD_DIM), lambda h, g: (0, h))
    return pl.pallas_call(
        body,
        name="attn_bwd",
        grid=(H, S // qt),
        in_specs=[
            pl.BlockSpec((qt, HEAD_DIM), lambda h, g: (g, 4 * H + h)),
            kv_spec,
            kv_spec,
            pl.BlockSpec((1, 1, ne), lambda h, g: (h, 0, 0)),
            pl.BlockSpec((qt, HEAD_DIM), lambda h, g: (g, H + h)),
        ],
        out_specs=[
            pl.BlockSpec((qt, HEAD_DIM), lambda h, g: (g, h)),
            kv_spec,
            kv_spec,
            pl.BlockSpec((1, qt, kb), lambda h, g: (h, 0, 0)),
        ],
        out_shape=[
            jax.ShapeDtypeStruct((S, W), bf16),
            jax.ShapeDtypeStruct((S + pad, W), f32),
            jax.ShapeDtypeStruct((S + pad, W), f32),
            jax.ShapeDtypeStruct((H, qt, kb), f32),
        ],
        scratch_shapes=[pltpu.VMEM((qt, kb), f32)],
        compiler_params=pltpu.CompilerParams(dimension_semantics=("parallel", "arbitrary")),
    )(proj, kp, vp, ext, dycat)


def _bias_onehot(qt, ne, nrel_pad):
    m = lax.broadcasted_iota(jnp.int32, (nrel_pad, ne), 1)
    r = lax.broadcasted_iota(jnp.int32, (nrel_pad, ne), 0)
    rel = LEFT_CHUNKS * CHUNK + qt - 1 - m
    hot = (r == jnp.clip(rel, -REL_CLIP, REL_CLIP) + REL_CLIP) & (m < ne - 1)
    return hot.astype(f32)


def _bias_ext(rel_bias_pad, onehot):
    H, _, nr = rel_bias_pad.shape
    ne = onehot.shape[1]

    def body(rb_ref, oh_ref, o_ref):
        o_ref[0] = _dot(rb_ref[0], oh_ref[...], NN, HIGHEST)

    return pl.pallas_call(
        body,
        name="bias_ext",
        grid=(H,),
        in_specs=[pl.BlockSpec((1, 1, nr), lambda h: (h, 0, 0)), pl.BlockSpec((nr, ne), lambda h: (0, 0))],
        out_specs=pl.BlockSpec((1, 1, ne), lambda h: (h, 0, 0)),
        out_shape=jax.ShapeDtypeStruct((H, 1, ne), f32),
        compiler_params=pltpu.CompilerParams(dimension_semantics=("parallel",)),
    )(rel_bias_pad, onehot)


def _bias_bwd(dbias, onehot):
    H, qt, kb = dbias.shape
    nr, ne = onehot.shape

    def body(d_ref, oh_ref, o_ref):
        row = lax.broadcasted_iota(jnp.int32, (qt, qt), 0)
        col = lax.broadcasted_iota(jnp.int32, (qt, qt), 1)
        flip = (row + col == qt - 1).astype(f32)
        x = _dot(flip, d_ref[0], NN, HIGHEST)
        x = jnp.concatenate([x, jnp.zeros((qt, ne - kb), f32)], axis=1)
        de = jnp.sum(pltpu.roll(x, 0, 1, stride=1, stride_axis=0), axis=0, keepdims=True)
        o_ref[0] = _dot(de, oh_ref[...], NT, HIGHEST)

    return pl.pallas_call(
        body,
        name="bias_bwd",
        grid=(H,),
        in_specs=[pl.BlockSpec((1, qt, kb), lambda h: (h, 0, 0)), pl.BlockSpec((nr, ne), lambda h: (0, 0))],
        out_specs=pl.BlockSpec((1, 1, nr), lambda h: (h, 0, 0)),
        out_shape=jax.ShapeDtypeStruct((H, 1, nr), f32),
        compiler_params=pltpu.CompilerParams(dimension_semantics=("parallel",)),
    )(dbias, onehot)


def _place():
    x, y, c = lax.axis_index("x"), lax.axis_index("y"), lax.axis_index("c")
    return x, y, c


def _flip(x, y, k):
    return (1 - x if k & 2 else x), (1 - y if k & 1 else y)


def _region(ref, shard_shape, axis, j, half):
    R, C = shard_shape
    if axis == 0:
        if half is None:
            return ref.at[pl.ds(j * R, R), :]
        return ref.at[pl.ds(j * R + half * (R // 2), R // 2), :]
    if half is None:
        return ref.at[:, pl.ds(j * C, C)]
    return ref.at[pl.ds(half * (R // 2), R // 2), pl.ds(j * C, C)]


def _remote(src, dst, send_sem, recv_sem, dev):
    return pltpu.make_async_remote_copy(src_ref=src, dst_ref=dst, send_sem=send_sem, recv_sem=recv_sem,
                                        device_id=dev, device_id_type=MESH)


def _cast_place(shard, axis, where, *, name):
    R, C = shard.shape
    tr = _div(R, 256, 2 * SUBLANES)
    nr = R // tr
    full = (N_CHIPS * R, C) if axis == 0 else (R, N_CHIPS * C)
    omap = (lambda r, s: (s[0] * nr + r, 0)) if axis == 0 else (lambda r, s: (r, s[0]))

    def body(s_ref, x_ref, o_ref):
        o_ref[...] = x_ref[...].astype(bf16)

    return pl.pallas_call(
        body,
        name=name,
        grid_spec=pltpu.PrefetchScalarGridSpec(
            num_scalar_prefetch=1,
            grid=(nr,),
            in_specs=[pl.BlockSpec((tr, C), lambda r, s: (r, 0))],
            out_specs=pl.BlockSpec((tr, C), omap),
        ),
        out_shape=jax.ShapeDtypeStruct(full, bf16),
        compiler_params=pltpu.CompilerParams(dimension_semantics=("parallel",)),
    )(where, shard)


class _Rider(NamedTuple):
    ins: list
    out_shapes: list
    aliases: dict
    sems: list
    first: Callable
    last: Callable


def _run_rider(rider, *, name):
    ni, no = len(rider.ins), len(rider.out_shapes)

    def body(*refs):
        ins, outs, sems = refs[:ni], refs[ni:ni + no], refs[ni + no:]
        rider.first(ins, outs, sems)
        rider.last(ins, outs, sems)

    return pl.pallas_call(
        body,
        name=name,
        in_specs=[_hbm()] * ni,
        out_specs=[_hbm()] * no,
        out_shape=rider.out_shapes,
        input_output_aliases=rider.aliases,
        scratch_shapes=rider.sems,
    )(*rider.ins)


def _gather_rider(fulls, shard_shapes, axes, splits):
    n = len(fulls)

    def stage1(ins, outs, sems):
        x, y, c = _place()
        jo = 2 * x + y
        cps = []
        for m in range(n):
            half = c if splits[m] else None
            for k in (1, 2, 3):
                px, py = _flip(x, y, k)
                cps.append(_remote(_region(ins[m], shard_shapes[m], axes[m], jo, half), _region(outs[m], shard_shapes[m], axes[m], jo, half),
                                   sems[0].at[3 * m + k - 1], sems[1].at[3 * m + k - 1], (px, py, c)))
        return cps

    def first(ins, outs, sems):
        for cp in stage1(ins, outs, sems):
            cp.start()

    def last(ins, outs, sems):
        x, y, c = _place()
        send1, recv1, send2, recv2 = sems
        forwards = []
        for m in range(n):
            half = c if splits[m] else None
            for k in (1, 2, 3):
                px, py = _flip(x, y, k)
                land = _region(outs[m], shard_shapes[m], axes[m], 2 * px + py, half)
                _remote(land, land, send1.at[3 * m + k - 1], recv1.at[3 * m + k - 1], (x, y, c)).wait_recv()
                if splits[m]:
                    fw = _remote(land, land, send2.at[3 * m + k - 1], recv2.at[3 * m + k - 1], (x, y, 1 - c))
                    fw.start()
                    forwards.append(fw)
        for m in range(n):
            if not splits[m]:
                continue
            for k in (1, 2, 3):
                px, py = _flip(x, y, k)
                land = _region(outs[m], shard_shapes[m], axes[m], 2 * px + py, 1 - c)
                _remote(land, land, send2.at[3 * m + k - 1], recv2.at[3 * m + k - 1], (x, y, c)).wait_recv()
        for cp in stage1(ins, outs, sems) + forwards:
            cp.wait_send()

    sems = pltpu.SemaphoreType.DMA((3 * n,))
    return _Rider(list(fulls), [jax.ShapeDtypeStruct(f.shape, f.dtype) for f in fulls], {m: m for m in range(n)},
                  [sems, sems, sems, sems], first, last)


def _chips_rider(ts):
    n = len(ts)

    def copies(ins, outs, sems):
        x, y, c = _place()
        cps = []
        for m in range(n):
            for k in (1, 2, 3):
                px, py = _flip(x, y, k)
                cps.append(_remote(ins[m].at[2 * px + py], outs[m].at[k - 1], sems[0].at[3 * m + k - 1], sems[1].at[3 * m + k - 1], (px, py, c)))
        return cps

    def first(ins, outs, sems):
        for cp in copies(ins, outs, sems):
            cp.start()

    def last(ins, outs, sems):
        cps = copies(ins, outs, sems)
        for cp in cps:
            cp.wait_recv()
        for cp in cps:
            cp.wait_send()

    sems = pltpu.SemaphoreType.DMA((3 * n,))
    return _Rider(list(ts), [jax.ShapeDtypeStruct((3,) + t.shape[1:], t.dtype) for t in ts], {}, [sems, sems], first, last)


def _rs_sibling(dws, shard_shapes, axes, *, name):
    n = len(dws)

    def body(*refs):
        ins, got = refs[:n], refs[n:2 * n]
        send, recv = refs[2 * n:]
        x, y, c = _place()
        cps = []
        for m in range(n):
            for j in range(N_CHIPS):
                cp = _remote(_region(ins[m], shard_shapes[m], axes[m], j, 1 - c), got[m].at[j],
                             send.at[N_CHIPS * m + j], recv.at[N_CHIPS * m + j], (x, y, 1 - c))
                cp.start()
                cps.append(cp)
        for cp in cps:
            cp.wait_recv()
        for cp in cps:
            cp.wait_send()

    sems = pltpu.SemaphoreType.DMA((N_CHIPS * n,))
    return pl.pallas_call(
        body,
        name=name,
        in_specs=[_hbm()] * n,
        out_specs=[_hbm()] * n,
        out_shape=[jax.ShapeDtypeStruct((N_CHIPS, s[0] // 2, s[1]), d.dtype) for s, d in zip(shard_shapes, dws)],
        scratch_shapes=[sems, sems],
    )(*dws)


def _pair_sum(dw, got, shard_shape, axis, where, *, name):
    R, C = shard_shape
    hr = R // 2
    tr = _div(hr, 256, 2 * SUBLANES)
    nr = hr // tr
    dmap = (lambda j, r, s: ((2 * j + s[1]) * nr + r, 0)) if axis == 0 else (lambda j, r, s: (s[1] * nr + r, j))
    slot = pl.BlockSpec((1, tr, C), lambda j, r, s: (j, r, 0))

    def body(s_ref, d_ref, g_ref, o_ref):
        o_ref[0] = (d_ref[...].astype(f32) + g_ref[0].astype(f32)).astype(bf16)

    return pl.pallas_call(
        body,
        name=name,
        grid_spec=pltpu.PrefetchScalarGridSpec(
            num_scalar_prefetch=1,
            grid=(N_CHIPS, nr),
            in_specs=[pl.BlockSpec((tr, C), dmap), slot],
            out_specs=slot,
        ),
        out_shape=jax.ShapeDtypeStruct((N_CHIPS, hr, C), bf16),
        compiler_params=pltpu.CompilerParams(dimension_semantics=("parallel", "parallel")),
    )(where, dw, got)


def _chip_sum(pair, others, where, *, name):
    _, hr, C = pair.shape
    tr = _div(hr, 256, 2 * SUBLANES)
    nr = hr // tr

    def body(s_ref, p_ref, o_ref, g_ref):
        g_ref[...] = p_ref[0].astype(f32) + o_ref[0].astype(f32) + o_ref[1].astype(f32) + o_ref[2].astype(f32)

    return pl.pallas_call(
        body,
        name=name,
        grid_spec=pltpu.PrefetchScalarGridSpec(
            num_scalar_prefetch=1,
            grid=(nr,),
            in_specs=[pl.BlockSpec((1, tr, C), lambda r, s: (s[0], r, 0)), pl.BlockSpec((3, tr, C), lambda r, s: (0, r, 0))],
            out_specs=pl.BlockSpec((tr, C), lambda r, s: (s[1] * nr + r, 0)),
        ),
        out_shape=jax.ShapeDtypeStruct((2 * hr, C), f32),
        compiler_params=pltpu.CompilerParams(dimension_semantics=("parallel",)),
    )(where, pair, others)


def _rs_join(gs):
    n = len(gs)

    def body(*refs):
        ins, outs = refs[:n], refs[n:2 * n]
        send, recv = refs[2 * n:]
        x, y, c = _place()
        cps = []
        for m in range(n):
            hr = gs[m].shape[0] // 2
            cp = _remote(ins[m].at[pl.ds(c * hr, hr), :], outs[m].at[pl.ds(c * hr, hr), :], send.at[m], recv.at[m], (x, y, 1 - c))
            cp.start()
            cps.append(cp)
        for cp in cps:
            cp.wait_recv()
        for cp in cps:
            cp.wait_send()

    sems = pltpu.SemaphoreType.DMA((n,))
    return pl.pallas_call(
        body,
        name="grads_join_halves",
        in_specs=[_hbm()] * n,
        out_specs=[_hbm()] * n,
        out_shape=[jax.ShapeDtypeStruct(g.shape, g.dtype) for g in gs],
        input_output_aliases={m: m for m in range(n)},
        scratch_shapes=[sems, sems],
    )(*gs)


def _allreduce_small(v):
    rows = v.shape[0]

    def body(v_ref, o_ref, gath, send, recv):
        x, y, c = _place()
        me = 4 * x + 2 * y + c
        gath[pl.ds(me, 1)] = v_ref[...][None]
        cps = []
        for k in range(1, N_DEV):
            peer = (1 - x if k & 4 else x, 1 - y if k & 2 else y, 1 - c if k & 1 else c)
            cp = _remote(v_ref, gath.at[me], send.at[k - 1], recv.at[k - 1], peer)
            cp.start()
            cps.append(cp)
        for cp in cps:
            cp.wait_recv()
        for cp in cps:
            cp.wait_send()
        acc = gath[0]
        for d in range(1, N_DEV):
            acc = acc + gath[d]
        o_ref[...] = acc

    return pl.pallas_call(
        body,
        name="small_allreduce",
        in_specs=[_vmem()],
        out_specs=_vmem(),
        out_shape=jax.ShapeDtypeStruct(v.shape, f32),
        scratch_shapes=[pltpu.VMEM((N_DEV, rows, LANES), f32), pltpu.SemaphoreType.DMA((N_DEV - 1,)), pltpu.SemaphoreType.DMA((N_DEV - 1,))],
    )(v)


def _adamw_math(w, g, m, v):
    m = ADAM_B1 * m + (1.0 - ADAM_B1) * g
    v = ADAM_B2 * v + (1.0 - ADAM_B2) * (g * g)
    m_hat = m / (1.0 - ADAM_B1 ** ADAM_STEP)
    v_hat = v / (1.0 - ADAM_B2 ** ADAM_STEP)
    delta = -ADAM_LR * (m_hat / (jnp.sqrt(v_hat) + ADAM_EPS) + ADAM_WD * w)
    return delta, m, v


def _adamw(w, g, m, v, *, name):
    R, C = w.shape
    tr = _div(R, 128, SUBLANES)

    def body(w_ref, g_ref, m_ref, v_ref, d_ref, mo_ref, vo_ref):
        d, mn, vn = _adamw_math(w_ref[...], g_ref[...], m_ref[...], v_ref[...])
        d_ref[...] = d
        mo_ref[...] = mn
        vo_ref[...] = vn

    spec = pl.BlockSpec((tr, C), lambda r: (r, 0))
    return pl.pallas_call(
        body,
        name=name,
        grid=(R // tr,),
        in_specs=[spec] * 4,
        out_specs=[spec] * 3,
        out_shape=[jax.ShapeDtypeStruct((R, C), f32)] * 3,
        compiler_params=pltpu.CompilerParams(dimension_semantics=("parallel",)),
    )(w, g, m, v)


def _pack(arrs):
    flat = jnp.concatenate([a.reshape(-1).astype(f32) for a in arrs])
    tile = SUBLANES * LANES
    total = -(-flat.shape[0] // tile) * tile
    return jnp.pad(flat, (0, total - flat.shape[0])).reshape(total // LANES, LANES)


def _unpack(buf, shapes):
    flat = buf.reshape(-1)
    out, off = [], 0
    for s in shapes:
        size = int(np.prod(s))
        out.append(flat[off:off + size].reshape(s))
        off += size
    return out


def kernel(x, p, norm_mix, w_in, lb_logits, hg_norm, rel_bias, w_out, norm_ffn, w_up, conv_w, conv_b, w_down, norm_ple, w_ple_gate, w_ple_proj, final_norm, loss_target, m_norm_mix, m_w_in, m_lb_logits, m_hg_norm, m_rel_bias, m_w_out, m_norm_ffn, m_w_up, m_conv_w, m_conv_b, m_w_down, m_norm_ple, m_w_ple_gate, m_w_ple_proj, m_final_norm, v_norm_mix, v_w_in, v_lb_logits, v_hg_norm, v_rel_bias, v_w_out, v_norm_ffn, v_w_up, v_conv_w, v_conv_b, v_w_down, v_norm_ple, v_w_ple_gate, v_w_ple_proj, v_final_norm):
    S, D = x.shape[1], x.shape[2]
    xv, pv, tgt = x[0], p[0, 0], loss_target[0]
    W = (w_in.shape[2] * N_CHIPS) // 7
    H = W // HEAD_DIM
    F = w_down.shape[1] * N_CHIPS
    tc = _div(F // 2, 1408, LANES)
    nj = F // tc
    jx, jy = lax.axis_index("x"), lax.axis_index("y")
    chip = 2 * jx + jy

    big = [w_in[0], w_out[0], w_up[0], w_down[0], w_ple_gate[0], w_ple_proj[0]]
    big_axes = [1, 0, 1, 0, 0, 1]
    where = jnp.stack([chip, lax.axis_index("c")]).astype(jnp.int32)
    cw_pad = jnp.pad(conv_w[0], ((0, SUBLANES - conv_w.shape[1]), (0, 0)))
    cw_mine = lax.dynamic_update_slice(jnp.zeros((N_CHIPS * SUBLANES, cw_pad.shape[1]), f32), cw_pad, (chip * SUBLANES, 0))
    placed = [_cast_place(b, ax, where, name=f"cast_place_{i}") for i, (b, ax) in enumerate(zip(big, big_axes))]
    shard_shapes = [b.shape for b in big]
    (W_in,) = _run_rider(_gather_rider(placed[:1], shard_shapes[:1], big_axes[:1], [True]), name="gather_w_in")
    later = _gather_rider(placed[1:] + [cw_mine], shard_shapes[1:] + [cw_pad.shape], big_axes[1:] + [0], [True] * 5 + [False])

    tm = _div(S, 1024, LANES)

    a1 = _rms_fwd(xv, norm_mix, name="rms_mix")
    proj = _mm(a1, W_in, dims="nn", tm=tm, tn=_div(7 * W, 512, LANES), tk=D, out_dtype=f32, name="mm_in")
    y_hg, o_hg, st, W_out, W_up, W_down, W_pg, W_pp, cw_all = _hgrn_fwd(proj, lb_logits, hg_norm, W, rider=later)
    ncw = conv_w.shape[1]
    cw_full = jnp.transpose(cw_all.reshape(N_CHIPS, SUBLANES, -1)[:, :ncw], (1, 0, 2)).reshape(ncw, -1)
    cb = conv_b

    qt, pad, kb, ne = _att_dims(S)
    nrel = rel_bias.shape[2]
    nrel_pad = -(-nrel // LANES) * LANES
    onehot = _bias_onehot(qt, ne, nrel_pad)
    rb_pad = jnp.pad(rel_bias[0], ((0, 0), (0, nrel_pad - nrel)))[:, None, :]
    ext = _bias_ext(rb_pad, onehot)
    kp = jnp.pad(proj[:, 5 * W:6 * W].astype(bf16), ((pad, 0), (0, 0)))
    vp = jnp.pad(proj[:, 6 * W:7 * W].astype(bf16), ((pad, 0), (0, 0)))
    y_att = _attn_fwd(proj, kp, vp, ext, W)
    ycat = jnp.concatenate([y_hg, y_att], axis=1)

    tn_d = _div(D, 512, LANES)
    h1 = _mm(ycat, W_out, dims="nn", tm=tm, tn=tn_d, tk=D, out_dtype=f32, res=xv, name="mm_out")
    a2 = _rms_fwd(h1, norm_ffn, name="rms_ffn")
    perm = lambda b: (b % 2) * nj + b // 2
    u = _mm(a2, W_up, dims="nn", tm=tm, tn=tc, tk=D, out_dtype=f32, name="mm_up",
            b_idx=lambda i, j, k: (k, perm(j)))
    mact = _convact_fwd(u, cw_full, cb, tc)
    h2 = _mm(mact, W_down, dims="nn", tm=tm, tn=tn_d, tk=_div(F, 1408, LANES), out_dtype=f32, res=h1, name="mm_down")
    a3 = _rms_fwd(h2, norm_ple, name="rms_ple")
    zg = _mm(a3, W_pg, dims="nn", tm=tm, tn=tn_d, tk=D, out_dtype=f32, name="mm_ple_gate")
    p16 = pv.astype(bf16)
    pp = _mm(p16, W_pp, dims="nn", tm=tm, tn=tn_d, tk=pv.shape[1], out_dtype=f32, name="mm_ple_proj")
    dh3, dzg, dpp, loss_part, d_fn = _tail(h2, zg, pp, tgt, final_norm.reshape(1, D))

    tk_s = _div(S, 2048, LANES)
    dW_pp = _mm(p16, dpp, dims="tn", tm=pv.shape[1], tn=tn_d, tk=tk_s, out_dtype=bf16, name="mm_d_ple_proj")
    dW_pg = _mm(a3, dzg, dims="tn", tm=tn_d, tn=tn_d, tk=tk_s, out_dtype=bf16, name="mm_d_ple_gate")
    da3 = _mm(dzg, W_pg, dims="nt", tm=tm, tn=_div(D, 1024, LANES), tk=D, out_dtype=f32, name="mm_da3")
    dh2, dh2b, d_nple = _rms_bwd(h2, norm_ple, da3, dh3, name="rms_ple_bwd")
    dm = _mm(dh2b, W_down, dims="nt", tm=tm, tn=_div(F, 512, LANES), tk=D, out_dtype=f32, name="mm_dm")
    dW_down = _mm(mact, dh2b, dims="tn", tm=_div(F, 512, LANES), tn=tn_d, tk=tk_s, out_dtype=bf16, name="mm_d_down")
    duc, dcw_g, dcw_v, dcb_g, dcb_v = _convact_bwd_a(u, dm, cw_full, cb, tc)
    du = _convact_bwd_b(duc, cw_full, tc)
    dW_up = _mm(a2, du, dims="tn", tm=_div(D, 1024, LANES), tn=tc, tk=tk_s, out_dtype=bf16, name="mm_d_up",
                o_idx=lambda i, j, k: (i, perm(j)))
    da2 = _mm(du, W_up, dims="nt", tm=tm, tn=_div(D, 1024, LANES), tk=tc, out_dtype=f32, name="mm_da2",
              b_idx=lambda i, j, k: (j, perm(k)))
    dh1, dh1b, d_nffn = _rms_bwd(h1, norm_ffn, da2, dh2, name="rms_ffn_bwd")
    dycat = _mm(dh1b, W_out, dims="nt", tm=tm, tn=_div(D, 1024, LANES), tk=D, out_dtype=f32, name="mm_dycat")
    dW_out = _mm(ycat, dh1b, dims="tn", tm=tn_d, tn=tn_d, tk=tk_s, out_dtype=bf16, name="mm_d_out")
    dq_att, dk_pad, dv_pad, dbias = _attn_bwd(proj, kp, vp, ext, dycat, W)
    early = [dW_out, dW_up, dW_down, dW_pg, dW_pp]
    got_e = _rs_sibling(early, shard_shapes[1:], big_axes[1:], name="grads_to_sibling_early")
    pair_e = [_pair_sum(early[i], got_e[i], shard_shapes[1 + i], big_axes[1 + i], where, name=f"grad_pair_sum_{1 + i}") for i in range(5)]
    hg_out = _hgrn_bwd(proj, o_hg, st, dycat, lb_logits, hg_norm, W, rider=_chips_rider(pair_e))
    dhq, dhf, dhi, dhg, d_lbl, d_hgn = hg_out[:6]
    others_e = list(hg_out[6:])
    d_rb = _bias_bwd(dbias, onehot)[:, 0, :nrel]
    dproj = jnp.concatenate([dhq, dhf, dhi, dhg, dq_att, dk_pad[pad:].astype(bf16), dv_pad[pad:].astype(bf16)], axis=1)
    dW_in = _mm(a1, dproj, dims="tn", tm=_div(D, 1024, LANES), tn=_div(7 * W, 512, LANES), tk=tk_s, out_dtype=bf16, name="mm_d_in")
    da1 = _mm(dproj, W_in, dims="nt", tm=tm, tn=_div(D, 1024, LANES), tk=_div(7 * W, 1792, LANES), out_dtype=f32, name="mm_da1")
    grad_x, _, d_nmix = _rms_bwd(xv, norm_mix, da1, dh1, name="rms_mix_bwd")

    d_cw = jnp.concatenate([dcw_g, dcw_v], axis=1)
    d_cb = jnp.concatenate([dcb_g, dcb_v], axis=1)
    small_parts = [loss_part[:, :1], d_nmix, d_lbl, d_hgn, d_rb, d_nffn, d_cw, d_cb, d_nple, d_fn]
    small_shapes = [(1, 1), (1, D), lb_logits.shape, hg_norm.shape, (H, nrel), (1, D), (ncw, 2 * F), (1, 2 * F), (1, D), (1, D)]
    red = _unpack(_allreduce_small(_pack(small_parts)), small_shapes)
    loss = red[0].reshape(())
    g_nmix, g_lbl, g_hgn, g_rb, g_nffn, g_cw_all, g_cb, g_nple, g_fn = red[1:]
    csh = conv_w.shape[2]
    g_cw = lax.dynamic_slice(g_cw_all, (0, chip * csh), (ncw, csh))
    small_g = [g_nmix, g_lbl, g_hgn, g_rb[None], g_nffn, g_cw[None], g_cb, g_nple, g_fn.reshape(D)]
    small_w = [norm_mix, lb_logits, hg_norm, rel_bias, norm_ffn, conv_w, conv_b, norm_ple, final_norm]
    small_m = [m_norm_mix, m_lb_logits, m_hg_norm, m_rel_bias, m_norm_ffn, m_conv_w, m_conv_b, m_norm_ple, m_final_norm]
    small_v = [v_norm_mix, v_lb_logits, v_hg_norm, v_rel_bias, v_norm_ffn, v_conv_w, v_conv_b, v_norm_ple, v_final_norm]
    shapes_s = [w.shape for w in small_w]
    sd, sm, sv = _adamw(_pack(small_w), _pack(small_g), _pack(small_m), _pack(small_v), name="adamw_small")
    small_g = [g.reshape(s) for g, s in zip(small_g, shapes_s)]
    small_d, small_nm, small_nv = _unpack(sd, shapes_s), _unpack(sm, shapes_s), _unpack(sv, shapes_s)

    got_in = _rs_sibling([dW_in], shard_shapes[:1], big_axes[:1], name="grads_to_sibling_w_in")
    pair_in = _pair_sum(dW_in, got_in[0], shard_shapes[0], big_axes[0], where, name="grad_pair_sum_0")
    others_in = _run_rider(_chips_rider([pair_in]), name="grads_to_chips_w_in")
    pair, others = [pair_in] + pair_e, list(others_in) + others_e
    halves = [_chip_sum(pair[i], others[i], where, name=f"grad_chip_sum_{i}") for i in range(6)]
    g_big = _rs_join(halves)
    big_m = [m_w_in[0], m_w_out[0], m_w_up[0], m_w_down[0], m_w_ple_gate[0], m_w_ple_proj[0]]
    big_v = [v_w_in[0], v_w_out[0], v_w_up[0], v_w_down[0], v_w_ple_gate[0], v_w_ple_proj[0]]
    big_d, big_nm, big_nv = [], [], []
    for i in range(6):
        d_, m_, v_ = _adamw(big[i], g_big[i], big_m[i], big_v[i], name=f"adamw_{i}")
        big_d.append(d_[None])
        big_nm.append(m_[None])
        big_nv.append(v_[None])
    g_big = [g[None] for g in g_big]

    def order(sm_list, bg_list):
        s, b = sm_list, bg_list
        return [s[0], b[0], s[1], s[2], s[3], b[1], s[4], b[2], s[5], s[6], b[3], s[7], b[4], b[5], s[8]]

    return (loss, grad_x[None], *order(small_g, g_big), *order(small_d, big_d), *order(small_nm, big_nm), *order(small_nv, big_nv))
```

```python
import functools
from typing import Callable, NamedTuple

import jax
import jax.numpy as jnp
import numpy as np
from jax import lax
from jax.experimental import pallas as pl
from jax.experimental.pallas import tpu as pltpu

f32 = jnp.float32
bf16 = jnp.bfloat16

CHUNK = 64
HEAD_DIM = 128
LEFT_CHUNKS = 8
REL_CLIP = 128
EPS = 1e-6
HG_SUB = 16
HG_HEADS_PER_STEP = 8
ATT_Q_ROWS = 256
ADAM_LR, ADAM_B1, ADAM_B2, ADAM_EPS, ADAM_WD, ADAM_STEP = 0.001, 0.9, 0.999, 1e-08, 0.01, 10
LANES = 128
SUBLANES = 8
N_CHIPS = 4
N_DEV = 8
MESH = pl.DeviceIdType.MESH
NEG = float(np.finfo(np.float32).min)

NN = (((1,), (0,)), ((), ()))
NT = (((1,), (1,)), ((), ()))
TN = (((0,), (0,)), ((), ()))
HIGHEST = lax.Precision.HIGHEST


def _dot(a, b, dims, precision=None):
    return lax.dot_general(a, b, dims, preferred_element_type=f32, precision=precision)


def _sigmoid(v):
    return 1.0 / (1.0 + jnp.exp(-v))


def _div(n, pref, mult):
    best = None
    d = mult
    while d <= min(n, pref):
        if n % d == 0:
            best = d
        d += mult
    return best if best is not None else n


def _hbm():
    return pl.BlockSpec(memory_space=pltpu.HBM)


def _vmem():
    return pl.BlockSpec(memory_space=pltpu.VMEM)


def _mm(a, b, *, dims, tm, tn, tk, out_dtype, name, res=None, a_idx=None, b_idx=None, o_idx=None, out_shape=None, rider=None):
    if dims == "nn":
        (M, K), (_, N) = a.shape, b.shape
        a_blk, b_blk, dn = (tm, tk), (tk, tn), NN
        a_def, b_def = (lambda i, j, k: (i, k)), (lambda i, j, k: (k, j))
    elif dims == "nt":
        (M, K), (N, _) = a.shape, b.shape
        a_blk, b_blk, dn = (tm, tk), (tn, tk), NT
        a_def, b_def = (lambda i, j, k: (i, k)), (lambda i, j, k: (j, k))
    else:
        (K, M), (_, N) = a.shape, b.shape
        a_blk, b_blk, dn = (tk, tm), (tk, tn), TN
        a_def, b_def = (lambda i, j, k: (k, i)), (lambda i, j, k: (k, j))
    assert M % tm == 0 and N % tn == 0 and K % tk == 0, (name, M, N, K, tm, tn, tk)
    nk = K // tk
    has_res = res is not None

    def body(*refs):
        a_ref, b_ref = refs[0], refs[1]
        res_ref = refs[2] if has_res else None
        o_ref = refs[3] if has_res else refs[2]
        part = _dot(a_ref[...].astype(bf16), b_ref[...].astype(bf16), dn)

        def finish(acc):
            if has_res:
                acc = acc + res_ref[...]
            o_ref[...] = acc.astype(out_dtype)

        if nk == 1:
            finish(part)
        else:
            acc_ref = refs[-1]
            k = pl.program_id(2)

            @pl.when(k == 0)
            def _():
                acc_ref[...] = part

            @pl.when(k > 0)
            def _():
                acc_ref[...] += part

            @pl.when(k == nk - 1)
            def _():
                finish(acc_ref[...])

    o_map = o_idx or (lambda i, j, k: (i, j))
    in_specs = [pl.BlockSpec(a_blk, a_idx or a_def), pl.BlockSpec(b_blk, b_idx or b_def)]
    args = [a, b]
    if has_res:
        in_specs.append(pl.BlockSpec((tm, tn), o_map))
        args.append(res)
    outs = _grid_call(
        body,
        rider,
        name=name,
        grid=(M // tm, N // tn, nk),
        in_specs=in_specs,
        out_specs=[pl.BlockSpec((tm, tn), o_map)],
        out_shape=[jax.ShapeDtypeStruct(out_shape or (M, N), out_dtype)],
        scratch_shapes=[pltpu.VMEM((tm, tn), f32)] if nk > 1 else [],
        args=args,
    )
    return outs[0] if rider is None else outs


def _rms_fwd(xv, g, *, name):
    S, D = xv.shape
    ts = _div(S, 512, SUBLANES)

    def body(x_ref, g_ref, o_ref):
        v = x_ref[...]
        r = lax.rsqrt(jnp.mean(v * v, axis=-1, keepdims=True) + EPS)
        o_ref[...] = (v * r * g_ref[...]).astype(bf16)

    return pl.pallas_call(
        body,
        name=name,
        grid=(S // ts,),
        in_specs=[pl.BlockSpec((ts, D), lambda i: (i, 0)), pl.BlockSpec((1, D), lambda i: (0, 0))],
        out_specs=pl.BlockSpec((ts, D), lambda i: (i, 0)),
        out_shape=jax.ShapeDtypeStruct((S, D), bf16),
        compiler_params=pltpu.CompilerParams(dimension_semantics=("parallel",)),
    )(xv, g)


def _rms_bwd(xv, g, dy, dres, *, name):
    S, D = xv.shape
    ts = _div(S, 256, SUBLANES)

    def body(x_ref, g_ref, dy_ref, dres_ref, dx_ref, dxb_ref, dg_ref):
        i = pl.program_id(0)
        v = x_ref[...]
        r = lax.rsqrt(jnp.mean(v * v, axis=-1, keepdims=True) + EPS)
        vn = v * r
        d = dy_ref[...]
        part = jnp.sum(d * vn, axis=0, keepdims=True)

        @pl.when(i == 0)
        def _():
            dg_ref[...] = part

        @pl.when(i > 0)
        def _():
            dg_ref[...] += part

        t = d * g_ref[...]
        dx = dres_ref[...] + r * (t - vn * jnp.mean(t * vn, axis=-1, keepdims=True))
        dx_ref[...] = dx
        dxb_ref[...] = dx.astype(bf16)

    row = pl.BlockSpec((ts, D), lambda i: (i, 0))
    vec = pl.BlockSpec((1, D), lambda i: (0, 0))
    return pl.pallas_call(
        body,
        name=name,
        grid=(S // ts,),
        in_specs=[row, vec, row, row],
        out_specs=[row, row, vec],
        out_shape=[jax.ShapeDtypeStruct((S, D), f32), jax.ShapeDtypeStruct((S, D), bf16), jax.ShapeDtypeStruct((1, D), f32)],
        compiler_params=pltpu.CompilerParams(dimension_semantics=("arbitrary",)),
    )(xv, g, dy, dres)


def _tail(h2, zg, pp, tgt, fn):
    S, D = h2.shape
    ts = _div(S, 256, SUBLANES)

    def body(h_ref, z_ref, p_ref, t_ref, fn_ref, dh_ref, dz_ref, dp_ref, loss_ref, dfn_ref):
        i = pl.program_id(0)
        sg = _sigmoid(z_ref[...])
        ppv = p_ref[...]
        h3 = h_ref[...] + sg * ppv
        r = lax.rsqrt(jnp.mean(h3 * h3, axis=-1, keepdims=True) + EPS)
        hn = h3 * r
        fnv = fn_ref[...]
        e = hn * fnv - t_ref[...]
        lpart = 0.5 * jnp.sum(jnp.mean(e * e, axis=-1, keepdims=True), axis=0, keepdims=True)
        dy = e * (1.0 / D)
        dpart = jnp.sum(dy * hn, axis=0, keepdims=True)

        @pl.when(i == 0)
        def _():
            loss_ref[...] = jnp.broadcast_to(lpart, loss_ref.shape)
            dfn_ref[...] = dpart

        @pl.when(i > 0)
        def _():
            loss_ref[...] += jnp.broadcast_to(lpart, loss_ref.shape)
            dfn_ref[...] += dpart

        t = dy * fnv
        dh3 = r * (t - hn * jnp.mean(t * hn, axis=-1, keepdims=True))
        dh_ref[...] = dh3
        dz_ref[...] = (dh3 * ppv * sg * (1.0 - sg)).astype(bf16)
        dp_ref[...] = (dh3 * sg).astype(bf16)

    row = pl.BlockSpec((ts, D), lambda i: (i, 0))
    vec = pl.BlockSpec((1, D), lambda i: (0, 0))
    one = pl.BlockSpec((1, LANES), lambda i: (0, 0))
    return pl.pallas_call(
        body,
        name="ple_final_loss",
        grid=(S // ts,),
        in_specs=[row, row, row, row, vec],
        out_specs=[row, row, row, one, vec],
        out_shape=[
            jax.ShapeDtypeStruct((S, D), f32),
            jax.ShapeDtypeStruct((S, D), bf16),
            jax.ShapeDtypeStruct((S, D), bf16),
            jax.ShapeDtypeStruct((1, LANES), f32),
            jax.ShapeDtypeStruct((1, D), f32),
        ],
        compiler_params=pltpu.CompilerParams(dimension_semantics=("arbitrary",)),
    )(h2, zg, pp, tgt, fn)


def _shift_down(v, halo, k):
    r = pltpu.roll(v, k, 0)
    hr = pltpu.roll(halo, k, 0)
    row = lax.broadcasted_iota(jnp.int32, hr.shape, 0)
    top = jnp.where(row < k, hr, r[0:SUBLANES])
    return jnp.concatenate([top, r[SUBLANES:]], axis=0)


def _shift_up(v, halo, k):
    n = v.shape[0]
    r = pltpu.roll(v, n - k, 0)
    hr = pltpu.roll(halo, SUBLANES - k, 0)
    row = lax.broadcasted_iota(jnp.int32, hr.shape, 0)
    bot = jnp.where(row >= SUBLANES - k, hr, r[n - SUBLANES:])
    return jnp.concatenate([r[: n - SUBLANES], bot], axis=0)


def _conv_specs(S, F, tc, ts):
    nj = F // tc
    rows8 = ts // SUBLANES
    main = pl.BlockSpec((ts, 2 * tc), lambda j, i: (i, j))
    prev = pl.BlockSpec((SUBLANES, 2 * tc), lambda j, i: (jnp.maximum(i * rows8 - 1, 0), j))
    nxt = pl.BlockSpec((SUBLANES, 2 * tc), lambda j, i: (jnp.minimum((i + 1) * rows8, S // SUBLANES - 1), j))
    wg = pl.BlockSpec((3, tc), lambda j, i: (0, j))
    wv = pl.BlockSpec((3, tc), lambda j, i: (0, nj + j))
    bg = pl.BlockSpec((1, tc), lambda j, i: (0, j))
    bv = pl.BlockSpec((1, tc), lambda j, i: (0, nj + j))
    half = pl.BlockSpec((ts, tc), lambda j, i: (i, j))
    return nj, main, prev, nxt, wg, wv, bg, bv, half


def _conv_pre(u_ref, h_ref, wg_ref, wv_ref, bg_ref, bv_ref, tc):
    i = pl.program_id(1)
    u = u_ref[...]
    halo = jnp.where(i == 0, 0.0, h_ref[...])
    u1 = _shift_down(u, halo, 1)
    u2 = _shift_down(u, halo, 2)
    w = jnp.concatenate([wg_ref[...], wv_ref[...]], axis=1)
    b = jnp.concatenate([bg_ref[...], bv_ref[...]], axis=1)
    uc = b + w[0:1] * u2 + w[1:2] * u1 + w[2:3] * u
    return u, u1, u2, uc[:, :tc], uc[:, tc:]


def _convact_fwd(u, cw, cb, tc):
    S, F2 = u.shape
    F = F2 // 2
    ts = _div(S, 512, SUBLANES)
    nj, main, prev, nxt, wg, wv, bg, bv, half = _conv_specs(S, F, tc, ts)

    def body(u_ref, h_ref, wg_ref, wv_ref, bg_ref, bv_ref, m_ref):
        _, _, _, g, v = _conv_pre(u_ref, h_ref, wg_ref, wv_ref, bg_ref, bv_ref, tc)
        m_ref[...] = (g * _sigmoid(g) * v).astype(bf16)

    return pl.pallas_call(
        body,
        name="convact_fwd",
        grid=(nj, S // ts),
        in_specs=[main, prev, wg, wv, bg, bv],
        out_specs=half,
        out_shape=jax.ShapeDtypeStruct((S, F), bf16),
        compiler_params=pltpu.CompilerParams(dimension_semantics=("parallel", "parallel")),
    )(u, u, cw, cw, cb, cb)


def _convact_bwd_a(u, dm, cw, cb, tc, rider=None):
    S, F2 = u.shape
    F = F2 // 2
    ts = _div(S, 512, SUBLANES)
    nj, main, prev, nxt, wg, wv, bg, bv, half = _conv_specs(S, F, tc, ts)

    def body(u_ref, h_ref, dm_ref, wg_ref, wv_ref, bg_ref, bv_ref, duc_ref, dwg_ref, dwv_ref, dbg_ref, dbv_ref):
        i = pl.program_id(1)
        u0, u1, u2, g, v = _conv_pre(u_ref, h_ref, wg_ref, wv_ref, bg_ref, bv_ref, tc)
        d = dm_ref[...]
        sg = _sigmoid(g)
        dg = d * v * sg * (1.0 + g * (1.0 - sg))
        dv = d * g * sg
        duc = jnp.concatenate([dg, dv], axis=1)
        duc_ref[...] = duc
        db = jnp.sum(duc, axis=0, keepdims=True)
        dw = jnp.concatenate(
            [jnp.sum(duc * u2, axis=0, keepdims=True), jnp.sum(duc * u1, axis=0, keepdims=True), jnp.sum(duc * u0, axis=0, keepdims=True)],
            axis=0,
        )

        @pl.when(i == 0)
        def _():
            dwg_ref[...] = dw[:, :tc]
            dwv_ref[...] = dw[:, tc:]
            dbg_ref[...] = db[:, :tc]
            dbv_ref[...] = db[:, tc:]

        @pl.when(i > 0)
        def _():
            dwg_ref[...] += dw[:, :tc]
            dwv_ref[...] += dw[:, tc:]
            dbg_ref[...] += db[:, :tc]
            dbv_ref[...] += db[:, tc:]

    w_out = pl.BlockSpec((3, tc), lambda j, i: (0, j))
    b_out = pl.BlockSpec((1, tc), lambda j, i: (0, j))
    return _grid_call(
        body,
        rider,
        name="convact_bwd_a",
        grid=(nj, S // ts),
        in_specs=[main, prev, half, wg, wv, bg, bv],
        out_specs=[main, w_out, w_out, b_out, b_out],
        out_shape=[
            jax.ShapeDtypeStruct((S, F2), f32),
            jax.ShapeDtypeStruct((3, F), f32),
            jax.ShapeDtypeStruct((3, F), f32),
            jax.ShapeDtypeStruct((1, F), f32),
            jax.ShapeDtypeStruct((1, F), f32),
        ],
        scratch_shapes=[],
        args=(u, u, dm, cw, cw, cb, cb),
    )


def _convact_bwd_b(duc, cw, tc):
    S, F2 = duc.shape
    F = F2 // 2
    ts = _div(S, 512, SUBLANES)
    nj, main, prev, nxt, wg, wv, bg, bv, half = _conv_specs(S, F, tc, ts)
    ni = S // ts

    def body(d_ref, h_ref, wg_ref, wv_ref, du_ref):
        i = pl.program_id(1)
        d = d_ref[...]
        halo = jnp.where(i == ni - 1, 0.0, h_ref[...])
        d1 = _shift_up(d, halo, 1)
        d2 = _shift_up(d, halo, 2)
        w = jnp.concatenate([wg_ref[...], wv_ref[...]], axis=1)
        du_ref[...] = (w[2:3] * d + w[1:2] * d1 + w[0:1] * d2).astype(bf16)

    return pl.pallas_call(
        body,
        name="convact_bwd_b",
        grid=(nj, ni),
        in_specs=[main, nxt, wg, wv],
        out_specs=main,
        out_shape=jax.ShapeDtypeStruct((S, F2), bf16),
        compiler_params=pltpu.CompilerParams(dimension_semantics=("parallel", "parallel")),
    )(duc, duc, cw, cw)


def _hg_gates(qp, fp, lbl):
    lb = _sigmoid(lbl[0:1] - lbl[1:2])
    sig = _sigmoid(fp)
    sigm = _sigmoid(-fp)
    f = lb + (1.0 - lb) * sig
    k = (1.0 - lb) * sigm
    sq = _sigmoid(qp)
    qf = qp * sq
    row = lax.broadcasted_iota(jnp.int32, (CHUNK, CHUNK), 0)
    col = lax.broadcasted_iota(jnp.int32, (CHUNK, CHUNK), 1)
    b = _dot((row >= col).astype(f32), jnp.log(f), NN, HIGHEST)
    return lb, sig, sigm, f, k, sq, qf, b


def _hg_block(qf, k, b, I):
    r0, n = I * HG_SUB, (I + 1) * HG_SUB
    base = b[r0 - 1:r0] if I > 0 else jnp.zeros((1, HEAD_DIM), f32)
    eq = jnp.exp(b[r0:n] - base)
    ek = jnp.exp(base - b[0:n])
    qt = qf[r0:n] * eq
    kt = k[0:n] * ek
    row = lax.broadcasted_iota(jnp.int32, (HG_SUB, n), 0) + r0
    col = lax.broadcasted_iota(jnp.int32, (HG_SUB, n), 1)
    return r0, n, eq, ek, qt, kt, col <= row


def _hgrn_specs(S, W, hp, reverse):
    nc = S // CHUNK
    ngrp = W // (hp * HEAD_DIM)
    cw = hp * HEAD_DIM
    cidx = (lambda c: nc - 1 - c) if reverse else (lambda c: c)

    def proj(off):
        return pl.BlockSpec((CHUNK, cw), lambda h, c: (cidx(c), off * ngrp + h))

    act = pl.BlockSpec((CHUNK, cw), lambda h, c: (cidx(c), h))
    lbl = pl.BlockSpec((2, cw), lambda h, c: (0, h))
    gn = pl.BlockSpec((1, HEAD_DIM), lambda h, c: (0, 0))
    st = pl.BlockSpec((hp, 1, HEAD_DIM, HEAD_DIM), lambda h, c: (h, cidx(c), 0, 0))
    return nc, ngrp, proj, act, lbl, gn, st


def _grid_call(body, rider, *, name, grid, in_specs, out_specs, out_shape, scratch_shapes, args):
    params = pltpu.CompilerParams(dimension_semantics=("arbitrary",) * len(grid))
    if rider is None:
        return pl.pallas_call(body, name=name, grid=grid, in_specs=in_specs, out_specs=out_specs, out_shape=out_shape,
                              scratch_shapes=scratch_shapes, compiler_params=params)(*args)
    n_in, n_out, n_scr = len(in_specs), len(out_specs), len(scratch_shapes)
    ri, ro = len(rider.ins), len(rider.out_shapes)

    def wrapped(*refs):
        a, b = n_in, n_in + ri
        c, d = b + n_out, b + n_out + ro
        e = d + n_scr
        ids = [pl.program_id(t) for t in range(len(grid))]
        is_first = functools.reduce(jnp.logical_and, [i == 0 for i in ids])
        is_last = functools.reduce(jnp.logical_and, [i == g - 1 for i, g in zip(ids, grid)])

        @pl.when(is_first)
        def _():
            rider.first(refs[a:b], refs[c:d], refs[e:])

        body(*refs[:a], *refs[b:c], *refs[d:e])

        @pl.when(is_last)
        def _():
            rider.last(refs[a:b], refs[c:d], refs[e:])

    return pl.pallas_call(
        wrapped,
        name=name,
        grid=grid,
        in_specs=list(in_specs) + [_hbm()] * ri,
        out_specs=list(out_specs) + [_hbm()] * ro,
        out_shape=list(out_shape) + list(rider.out_shapes),
        input_output_aliases={n_in + i: n_out + o for i, o in rider.aliases.items()},
        scratch_shapes=list(scratch_shapes) + list(rider.sems),
        compiler_params=params,
    )(*args, *rider.ins)


def _hgrn_fwd(proj, lb_logits, hg_norm, W, rider=None):
    S = proj.shape[0]
    hp = min(HG_HEADS_PER_STEP, W // HEAD_DIM)
    nc, ngrp, pspec, act, lbl_spec, gn_spec, st_spec = _hgrn_specs(S, W, hp, False)
    nb = CHUNK // HG_SUB

    def body(q_ref, f_ref, i_ref, g_ref, lbl_ref, gn_ref, y_ref, o_ref, st_ref, state):
        c = pl.program_id(1)

        @pl.when(c == 0)
        def _():
            state[...] = jnp.zeros_like(state)

        for p in range(hp):
            sl = slice(p * HEAD_DIM, (p + 1) * HEAD_DIM)
            _, _, _, _, k, _, qf, b = _hg_gates(q_ref[:, sl], f_ref[:, sl], lbl_ref[:, sl])
            v16 = i_ref[:, sl].astype(bf16)
            st_prev = state[p]
            st_ref[p, 0] = st_prev
            o = _dot((qf * jnp.exp(b)).astype(bf16), st_prev.astype(bf16), NT)
            parts = []
            for I in range(nb):
                r0, n, _, _, qt, kt, mask = _hg_block(qf, k, b, I)
                sc = jnp.where(mask, _dot(qt.astype(bf16), kt.astype(bf16), NT), 0.0)
                parts.append(_dot(sc.astype(bf16), v16[0:n], NN))
            o = o + jnp.concatenate(parts, axis=0)
            bl = b[CHUNK - 1:CHUNK]
            kd = k * jnp.exp(bl - b)
            state[p] = st_prev * jnp.exp(bl) + _dot(v16, kd.astype(bf16), TN)
            o_ref[:, sl] = o
            r = lax.rsqrt(jnp.mean(o * o, axis=-1, keepdims=True) + EPS)
            gp = g_ref[:, sl]
            y_ref[:, sl] = (o * r * gn_ref[...] * (gp * _sigmoid(gp))).astype(bf16)

    H = W // HEAD_DIM
    return _grid_call(
        body,
        rider,
        name="hgrn_fwd",
        grid=(ngrp, nc),
        in_specs=[pspec(0), pspec(1), pspec(2), pspec(3), lbl_spec, gn_spec],
        out_specs=[act, act, st_spec],
        out_shape=[
            jax.ShapeDtypeStruct((S, W), bf16),
            jax.ShapeDtypeStruct((S, W), f32),
            jax.ShapeDtypeStruct((H, nc, HEAD_DIM, HEAD_DIM), f32),
        ],
        scratch_shapes=[pltpu.VMEM((hp, HEAD_DIM, HEAD_DIM), f32)],
        args=(proj, proj, proj, proj, lb_logits, hg_norm),
    )


def _hgrn_bwd(proj, o_hg, st, dycat, lb_logits, hg_norm, W, rider=None):
    S = proj.shape[0]
    hp = min(HG_HEADS_PER_STEP, W // HEAD_DIM)
    nc, ngrp, pspec, act, lbl_spec, gn_spec, st_spec = _hgrn_specs(S, W, hp, True)
    nb = CHUNK // HG_SUB

    def body(q_ref, f_ref, i_ref, g_ref, o_ref, st_ref, dy_ref, lbl_ref, gn_ref,
             dq_ref, df_ref, di_ref, dg_ref, dlbl_ref, dgn_ref, dstate, dk_acc, db_acc, dv_acc, dq_acc):
        h = pl.program_id(0)
        c = pl.program_id(1)

        @pl.when(c == 0)
        def _():
            dstate[...] = jnp.zeros_like(dstate)
            dlbl_ref[...] = jnp.zeros_like(dlbl_ref)

        @pl.when((c == 0) & (h == 0))
        def _():
            dgn_ref[...] = jnp.zeros_like(dgn_ref)

        row = lax.broadcasted_iota(jnp.int32, (CHUNK, CHUNK), 0)
        col = lax.broadcasted_iota(jnp.int32, (CHUNK, CHUNK), 1)
        upper = (col >= row).astype(f32)
        last_row = lax.broadcasted_iota(jnp.int32, (CHUNK, HEAD_DIM), 0) == CHUNK - 1

        for p in range(hp):
            sl = slice(p * HEAD_DIM, (p + 1) * HEAD_DIM)
            qp, fp = q_ref[:, sl], f_ref[:, sl]
            lb, sig, sigm, f, k, sq, qf, b = _hg_gates(qp, fp, lbl_ref[:, sl])
            v16 = i_ref[:, sl].astype(bf16)
            gnv = gn_ref[...]
            gp = g_ref[:, sl]
            sgg = _sigmoid(gp)
            gate = gp * sgg
            o = o_ref[:, sl]
            r = lax.rsqrt(jnp.mean(o * o, axis=-1, keepdims=True) + EPS)
            on = o * r
            dy = dy_ref[:, sl]
            dg_ref[:, sl] = (dy * on * gnv * (sgg * (1.0 + gp * (1.0 - sgg)))).astype(bf16)
            dgn_ref[...] += jnp.sum(dy * on * gate, axis=0, keepdims=True)
            don = dy * gnv * gate
            do = r * (don - on * jnp.mean(don * on, axis=-1, keepdims=True))
            do16 = do.astype(bf16)
            eb = jnp.exp(b)
            A = qf * eb
            bl = b[CHUNK - 1:CHUNK]
            ebl = jnp.exp(bl)
            ekd = jnp.exp(bl - b)
            kd = k * ekd
            st_prev = st_ref[p, 0]
            dst_new = dstate[p]
            dst16 = dst_new.astype(bf16)
            dA = _dot(do16, st_prev.astype(bf16), NN)
            dkd = _dot(v16, dst16, NN)
            dstate[p] = dst_new * ebl + _dot(do16, A.astype(bf16), TN)
            dbl = jnp.sum(dst_new * st_prev, axis=0, keepdims=True) * ebl + jnp.sum(dkd * kd, axis=0, keepdims=True)
            dq_acc[p] = dA * eb
            db_acc[p] = dA * A - dkd * kd + jnp.where(last_row, dbl, 0.0)
            dk_acc[p] = dkd * ekd
            dv_acc[p] = _dot(kd.astype(bf16), dst16, NT)
            for I in range(nb):
                r0, n, eq, ek, qt, kt, mask = _hg_block(qf, k, b, I)
                qt16, kt16 = qt.astype(bf16), kt.astype(bf16)
                sc = jnp.where(mask, _dot(qt16, kt16, NT), 0.0)
                dsc = jnp.where(mask, _dot(do16[r0:n], v16[0:n], NT), 0.0).astype(bf16)
                dv_acc[p, 0:n, :] += _dot(sc.astype(bf16), do16[r0:n], TN)
                dqt = _dot(dsc, kt16, NN)
                dkt = _dot(dsc, qt16, TN)
                dq_acc[p, r0:n, :] += dqt * eq
                dk_acc[p, 0:n, :] += dkt * ek
                db_acc[p, r0:n, :] += dqt * qt16.astype(f32)
                db_acc[p, 0:n, :] -= dkt * kt16.astype(f32)
            dlogf = _dot(upper, db_acc[p], NN, HIGHEST)
            dfg = dlogf / f
            dk = dk_acc[p]
            df_ref[:, sl] = ((1.0 - lb) * sig * sigm * (dfg - dk)).astype(bf16)
            dlb = jnp.sum(sigm * (dfg - dk), axis=0, keepdims=True)
            dl0 = dlb * lb * (1.0 - lb)
            dlbl_ref[:, sl] += jnp.concatenate([dl0, -dl0], axis=0)
            dq_ref[:, sl] = (dq_acc[p] * (sq * (1.0 + qp * (1.0 - sq)))).astype(bf16)
            di_ref[:, sl] = dv_acc[p].astype(bf16)

    cw = hp * HEAD_DIM
    dy_spec = pl.BlockSpec((CHUNK, cw), lambda h, c: (nc - 1 - c, h))
    acc = pltpu.VMEM((hp, CHUNK, HEAD_DIM), f32)
    return _grid_call(
        body,
        rider,
        name="hgrn_bwd",
        grid=(ngrp, nc),
        in_specs=[pspec(0), pspec(1), pspec(2), pspec(3), act, st_spec, dy_spec, lbl_spec, gn_spec],
        out_specs=[act, act, act, act, lbl_spec, gn_spec],
        out_shape=[jax.ShapeDtypeStruct((S, W), bf16)] * 4
        + [jax.ShapeDtypeStruct((2, W), f32), jax.ShapeDtypeStruct((1, HEAD_DIM), f32)],
        scratch_shapes=[pltpu.VMEM((hp, HEAD_DIM, HEAD_DIM), f32), acc, acc, acc, acc],
        args=(proj, proj, proj, proj, o_hg, st, dycat, lb_logits, hg_norm),
    )


def _att_dims(S):
    qt = ATT_Q_ROWS if S % ATT_Q_ROWS == 0 else CHUNK
    pad = LEFT_CHUNKS * CHUNK
    kb = pad + qt
    return qt, pad, kb, kb + qt


def _att_scores(q16, kb16, bias, start, qt, pad, kb):
    s = _dot(q16, kb16, NT) * (HEAD_DIM ** -0.5) + bias
    row = lax.broadcasted_iota(jnp.int32, (qt, kb), 0)
    col = lax.broadcasted_iota(jnp.int32, (qt, kb), 1)
    lo = jnp.bitwise_and(row, -CHUNK)
    ok = (col >= lo) & (col < lo + pad + CHUNK) & (col + start >= pad)
    s = jnp.where(ok, s, NEG)
    e = jnp.exp(s - jnp.max(s, axis=-1, keepdims=True))
    return e / jnp.sum(e, axis=-1, keepdims=True)


def _att_bias(ext_ref, qt, kb, ne):
    e = jnp.broadcast_to(ext_ref[0], (qt, ne))
    return pltpu.roll(e, ne - qt + 1, 1, stride=1, stride_axis=0)[:, :kb]


def _attn_fwd(proj, kp, vp, ext, W, rider=None):
    S = proj.shape[0]
    H = W // HEAD_DIM
    qt, pad, kb, ne = _att_dims(S)

    def body(q_ref, k_ref, v_ref, ext_ref, o_ref, bias):
        g = pl.program_id(1)

        @pl.when(g == 0)
        def _():
            bias[...] = _att_bias(ext_ref, qt, kb, ne)

        start = pl.multiple_of(g * qt, qt)
        pn = _att_scores(q_ref[...].astype(bf16), k_ref[pl.ds(start, kb), :], bias[...], start, qt, pad, kb)
        o_ref[...] = _dot(pn.astype(bf16), v_ref[pl.ds(start, kb), :], NN).astype(bf16)

    outs = _grid_call(
        body,
        rider,
        name="attn_fwd",
        grid=(H, S // qt),
        in_specs=[
            pl.BlockSpec((qt, HEAD_DIM), lambda h, g: (g, 4 * H + h)),
            pl.BlockSpec((S + pad, HEAD_DIM), lambda h, g: (0, h)),
            pl.BlockSpec((S + pad, HEAD_DIM), lambda h, g: (0, h)),
            pl.BlockSpec((1, 1, ne), lambda h, g: (h, 0, 0)),
        ],
        out_specs=[pl.BlockSpec((qt, HEAD_DIM), lambda h, g: (g, h))],
        out_shape=[jax.ShapeDtypeStruct((S, W), bf16)],
        scratch_shapes=[pltpu.VMEM((qt, kb), f32)],
        args=(proj, kp, vp, ext),
    )
    return outs[0] if rider is None else outs


def _attn_bwd(proj, kp, vp, ext, dycat, W, rider=None):
    S = proj.shape[0]
    H = W // HEAD_DIM
    qt, pad, kb, ne = _att_dims(S)
    scale = HEAD_DIM ** -0.5

    def body(q_ref, k_ref, v_ref, ext_ref, do_ref, dq_ref, dk_ref, dv_ref, db_ref, bias):
        g = pl.program_id(1)

        @pl.when(g == 0)
        def _():
            bias[...] = _att_bias(ext_ref, qt, kb, ne)
            dk_ref[...] = jnp.zeros_like(dk_ref)
            dv_ref[...] = jnp.zeros_like(dv_ref)
            db_ref[...] = jnp.zeros_like(db_ref)

        start = pl.multiple_of(g * qt, qt)
        q16 = q_ref[...].astype(bf16)
        kb16 = k_ref[pl.ds(start, kb), :]
        vb16 = v_ref[pl.ds(start, kb), :]
        pn = _att_scores(q16, kb16, bias[...], start, qt, pad, kb)
        do16 = do_ref[...].astype(bf16)
        dpn = _dot(do16, vb16, NT)
        dv_ref[pl.ds(start, kb), :] += _dot(pn.astype(bf16), do16, TN)
        ds = pn * (dpn - jnp.sum(dpn * pn, axis=-1, keepdims=True))
        db_ref[0] += ds
        ds16 = ds.astype(bf16)
        dq_ref[...] = (_dot(ds16, kb16, NN) * scale).astype(bf16)
        dk_ref[pl.ds(start, kb), :] += _dot(ds16, q16, TN) * scale

    kv_spec = pl.BlockSpec((S + pad, HEAD_DIM), lambda h, g: (0, h))
    return _grid_call(
        body,
        rider,
        name="attn_bwd",
        grid=(H, S // qt),
        in_specs=[
            pl.BlockSpec((qt, HEAD_DIM), lambda h, g: (g, 4 * H + h)),
            kv_spec,
            kv_spec,
            pl.BlockSpec((1, 1, ne), lambda h, g: (h, 0, 0)),
            pl.BlockSpec((qt, HEAD_DIM), lambda h, g: (g, H + h)),
        ],
        out_specs=[
            pl.BlockSpec((qt, HEAD_DIM), lambda h, g: (g, h)),
            kv_spec,
            kv_spec,
            pl.BlockSpec((1, qt, kb), lambda h, g: (h, 0, 0)),
        ],
        out_shape=[
            jax.ShapeDtypeStruct((S, W), bf16),
            jax.ShapeDtypeStruct((S + pad, W), f32),
            jax.ShapeDtypeStruct((S + pad, W), f32),
            jax.ShapeDtypeStruct((H, qt, kb), f32),
        ],
        scratch_shapes=[pltpu.VMEM((qt, kb), f32)],
        args=(proj, kp, vp, ext, dycat),
    )


def _bias_onehot(qt, ne, nrel_pad):
    m = lax.broadcasted_iota(jnp.int32, (nrel_pad, ne), 1)
    r = lax.broadcasted_iota(jnp.int32, (nrel_pad, ne), 0)
    rel = LEFT_CHUNKS * CHUNK + qt - 1 - m
    hot = (r == jnp.clip(rel, -REL_CLIP, REL_CLIP) + REL_CLIP) & (m < ne - 1)
    return hot.astype(f32)


def _bias_ext(rel_bias_pad, onehot):
    H, _, nr = rel_bias_pad.shape
    ne = onehot.shape[1]

    def body(rb_ref, oh_ref, o_ref):
        o_ref[0] = _dot(rb_ref[0], oh_ref[...], NN, HIGHEST)

    return pl.pallas_call(
        body,
        name="bias_ext",
        grid=(H,),
        in_specs=[pl.BlockSpec((1, 1, nr), lambda h: (h, 0, 0)), pl.BlockSpec((nr, ne), lambda h: (0, 0))],
        out_specs=pl.BlockSpec((1, 1, ne), lambda h: (h, 0, 0)),
        out_shape=jax.ShapeDtypeStruct((H, 1, ne), f32),
        compiler_params=pltpu.CompilerParams(dimension_semantics=("parallel",)),
    )(rel_bias_pad, onehot)


def _bias_bwd(dbias, onehot):
    H, qt, kb = dbias.shape
    nr, ne = onehot.shape

    def body(d_ref, oh_ref, o_ref):
        row = lax.broadcasted_iota(jnp.int32, (qt, qt), 0)
        col = lax.broadcasted_iota(jnp.int32, (qt, qt), 1)
        flip = (row + col == qt - 1).astype(f32)
        x = _dot(flip, d_ref[0], NN, HIGHEST)
        x = jnp.concatenate([x, jnp.zeros((qt, ne - kb), f32)], axis=1)
        de = jnp.sum(pltpu.roll(x, 0, 1, stride=1, stride_axis=0), axis=0, keepdims=True)
        o_ref[0] = _dot(de, oh_ref[...], NT, HIGHEST)

    return pl.pallas_call(
        body,
        name="bias_bwd",
        grid=(H,),
        in_specs=[pl.BlockSpec((1, qt, kb), lambda h: (h, 0, 0)), pl.BlockSpec((nr, ne), lambda h: (0, 0))],
        out_specs=pl.BlockSpec((1, 1, nr), lambda h: (h, 0, 0)),
        out_shape=jax.ShapeDtypeStruct((H, 1, nr), f32),
        compiler_params=pltpu.CompilerParams(dimension_semantics=("parallel",)),
    )(dbias, onehot)


def _place():
    x, y, c = lax.axis_index("x"), lax.axis_index("y"), lax.axis_index("c")
    return x, y, c


def _flip(x, y, k):
    return (1 - x if k & 2 else x), (1 - y if k & 1 else y)


def _region(ref, shard_shape, axis, j, half):
    R, C = shard_shape
    if axis == 0:
        if half is None:
            return ref.at[pl.ds(j * R, R), :]
        return ref.at[pl.ds(j * R + half * (R // 2), R // 2), :]
    if half is None:
        return ref.at[:, pl.ds(j * C, C)]
    return ref.at[pl.ds(half * (R // 2), R // 2), pl.ds(j * C, C)]


def _remote(src, dst, send_sem, recv_sem, dev):
    return pltpu.make_async_remote_copy(src_ref=src, dst_ref=dst, send_sem=send_sem, recv_sem=recv_sem,
                                        device_id=dev, device_id_type=MESH)


def _cast_place(shard, axis, where, *, name):
    R, C = shard.shape
    tr = _div(R, 256, 2 * SUBLANES)
    nr = R // tr
    full = (N_CHIPS * R, C) if axis == 0 else (R, N_CHIPS * C)
    omap = (lambda r, s: (s[0] * nr + r, 0)) if axis == 0 else (lambda r, s: (r, s[0]))

    def body(s_ref, x_ref, o_ref):
        o_ref[...] = x_ref[...].astype(bf16)

    return pl.pallas_call(
        body,
        name=name,
        grid_spec=pltpu.PrefetchScalarGridSpec(
            num_scalar_prefetch=1,
            grid=(nr,),
            in_specs=[pl.BlockSpec((tr, C), lambda r, s: (r, 0))],
            out_specs=pl.BlockSpec((tr, C), omap),
        ),
        out_shape=jax.ShapeDtypeStruct(full, bf16),
        compiler_params=pltpu.CompilerParams(dimension_semantics=("parallel",)),
    )(where, shard)


class _Rider(NamedTuple):
    ins: list
    out_shapes: list
    aliases: dict
    sems: list
    first: Callable
    last: Callable


def _run_rider(rider, *, name):
    ni, no = len(rider.ins), len(rider.out_shapes)

    def body(*refs):
        ins, outs, sems = refs[:ni], refs[ni:ni + no], refs[ni + no:]
        rider.first(ins, outs, sems)
        rider.last(ins, outs, sems)

    return pl.pallas_call(
        body,
        name=name,
        in_specs=[_hbm()] * ni,
        out_specs=[_hbm()] * no,
        out_shape=rider.out_shapes,
        input_output_aliases=rider.aliases,
        scratch_shapes=rider.sems,
    )(*rider.ins)


def _gather_rider(fulls, shard_shapes, axes, splits):
    n = len(fulls)

    def stage1(ins, outs, sems):
        x, y, c = _place()
        jo = 2 * x + y
        cps = []
        for m in range(n):
            half = c if splits[m] else None
            for k in (1, 2, 3):
                px, py = _flip(x, y, k)
                cps.append(_remote(_region(ins[m], shard_shapes[m], axes[m], jo, half), _region(outs[m], shard_shapes[m], axes[m], jo, half),
                                   sems[0].at[3 * m + k - 1], sems[1].at[3 * m + k - 1], (px, py, c)))
        return cps

    def first(ins, outs, sems):
        for cp in stage1(ins, outs, sems):
            cp.start()

    def last(ins, outs, sems):
        x, y, c = _place()
        send1, recv1, send2, recv2 = sems
        forwards = []
        for m in range(n):
            half = c if splits[m] else None
            for k in (1, 2, 3):
                px, py = _flip(x, y, k)
                land = _region(outs[m], shard_shapes[m], axes[m], 2 * px + py, half)
                _remote(land, land, send1.at[3 * m + k - 1], recv1.at[3 * m + k - 1], (x, y, c)).wait_recv()
                if splits[m]:
                    fw = _remote(land, land, send2.at[3 * m + k - 1], recv2.at[3 * m + k - 1], (x, y, 1 - c))
                    fw.start()
                    forwards.append(fw)
        for m in range(n):
            if not splits[m]:
                continue
            for k in (1, 2, 3):
                px, py = _flip(x, y, k)
                land = _region(outs[m], shard_shapes[m], axes[m], 2 * px + py, 1 - c)
                _remote(land, land, send2.at[3 * m + k - 1], recv2.at[3 * m + k - 1], (x, y, c)).wait_recv()
        for cp in stage1(ins, outs, sems) + forwards:
            cp.wait_send()

    sems = pltpu.SemaphoreType.DMA((3 * n,))
    return _Rider(list(fulls), [jax.ShapeDtypeStruct(f.shape, f.dtype) for f in fulls], {m: m for m in range(n)},
                  [sems, sems, sems, sems], first, last)


def _chips_rider(ts):
    n = len(ts)

    def copies(ins, outs, sems):
        x, y, c = _place()
        cps = []
        for m in range(n):
            for k in (1, 2, 3):
                px, py = _flip(x, y, k)
                cps.append(_remote(ins[m].at[2 * px + py], outs[m].at[k - 1], sems[0].at[3 * m + k - 1], sems[1].at[3 * m + k - 1], (px, py, c)))
        return cps

    def first(ins, outs, sems):
        for cp in copies(ins, outs, sems):
            cp.start()

    def last(ins, outs, sems):
        cps = copies(ins, outs, sems)
        for cp in cps:
            cp.wait_recv()
        for cp in cps:
            cp.wait_send()

    sems = pltpu.SemaphoreType.DMA((3 * n,))
    return _Rider(list(ts), [jax.ShapeDtypeStruct((3,) + t.shape[1:], t.dtype) for t in ts], {}, [sems, sems], first, last)


def _rs_sibling(dws, shard_shapes, axes, *, name):
    n = len(dws)

    def body(*refs):
        ins, got = refs[:n], refs[n:2 * n]
        send, recv = refs[2 * n:]
        x, y, c = _place()
        cps = []
        for m in range(n):
            for j in range(N_CHIPS):
                cp = _remote(_region(ins[m], shard_shapes[m], axes[m], j, 1 - c), got[m].at[j],
                             send.at[N_CHIPS * m + j], recv.at[N_CHIPS * m + j], (x, y, 1 - c))
                cp.start()
                cps.append(cp)
        for cp in cps:
            cp.wait_recv()
        for cp in cps:
            cp.wait_send()

    sems = pltpu.SemaphoreType.DMA((N_CHIPS * n,))
    return pl.pallas_call(
        body,
        name=name,
        in_specs=[_hbm()] * n,
        out_specs=[_hbm()] * n,
        out_shape=[jax.ShapeDtypeStruct((N_CHIPS, s[0] // 2, s[1]), d.dtype) for s, d in zip(shard_shapes, dws)],
        scratch_shapes=[sems, sems],
    )(*dws)


def _pair_sum(dw, got, shard_shape, axis, where, *, name):
    R, C = shard_shape
    hr = R // 2
    tr = _div(hr, 256, 2 * SUBLANES)
    nr = hr // tr
    dmap = (lambda j, r, s: ((2 * j + s[1]) * nr + r, 0)) if axis == 0 else (lambda j, r, s: (s[1] * nr + r, j))
    slot = pl.BlockSpec((1, tr, C), lambda j, r, s: (j, r, 0))

    def body(s_ref, d_ref, g_ref, o_ref):
        o_ref[0] = (d_ref[...].astype(f32) + g_ref[0].astype(f32)).astype(bf16)

    return pl.pallas_call(
        body,
        name=name,
        grid_spec=pltpu.PrefetchScalarGridSpec(
            num_scalar_prefetch=1,
            grid=(N_CHIPS, nr),
            in_specs=[pl.BlockSpec((tr, C), dmap), slot],
            out_specs=slot,
        ),
        out_shape=jax.ShapeDtypeStruct((N_CHIPS, hr, C), bf16),
        compiler_params=pltpu.CompilerParams(dimension_semantics=("parallel", "parallel")),
    )(where, dw, got)


def _chip_sum(pair, others, where, *, name):
    _, hr, C = pair.shape
    tr = _div(hr, 256, 2 * SUBLANES)
    nr = hr // tr

    def body(s_ref, p_ref, o_ref, g_ref):
        g_ref[...] = p_ref[0].astype(f32) + o_ref[0].astype(f32) + o_ref[1].astype(f32) + o_ref[2].astype(f32)

    return pl.pallas_call(
        body,
        name=name,
        grid_spec=pltpu.PrefetchScalarGridSpec(
            num_scalar_prefetch=1,
            grid=(nr,),
            in_specs=[pl.BlockSpec((1, tr, C), lambda r, s: (s[0], r, 0)), pl.BlockSpec((3, tr, C), lambda r, s: (0, r, 0))],
            out_specs=pl.BlockSpec((tr, C), lambda r, s: (s[1] * nr + r, 0)),
        ),
        out_shape=jax.ShapeDtypeStruct((2 * hr, C), f32),
        compiler_params=pltpu.CompilerParams(dimension_semantics=("parallel",)),
    )(where, pair, others)


def _rs_join(gs):
    n = len(gs)

    def body(*refs):
        ins, outs = refs[:n], refs[n:2 * n]
        send, recv = refs[2 * n:]
        x, y, c = _place()
        cps = []
        for m in range(n):
            hr = gs[m].shape[0] // 2
            cp = _remote(ins[m].at[pl.ds(c * hr, hr), :], outs[m].at[pl.ds(c * hr, hr), :], send.at[m], recv.at[m], (x, y, 1 - c))
            cp.start()
            cps.append(cp)
        for cp in cps:
            cp.wait_recv()
        for cp in cps:
            cp.wait_send()

    sems = pltpu.SemaphoreType.DMA((n,))
    return pl.pallas_call(
        body,
        name="grads_join_halves",
        in_specs=[_hbm()] * n,
        out_specs=[_hbm()] * n,
        out_shape=[jax.ShapeDtypeStruct(g.shape, g.dtype) for g in gs],
        input_output_aliases={m: m for m in range(n)},
        scratch_shapes=[sems, sems],
    )(*gs)


def _allreduce_small(v):
    rows = v.shape[0]

    def body(v_ref, o_ref, gath, send, recv):
        x, y, c = _place()
        me = 4 * x + 2 * y + c
        gath[pl.ds(me, 1)] = v_ref[...][None]
        cps = []
        for k in range(1, N_DEV):
            peer = (1 - x if k & 4 else x, 1 - y if k & 2 else y, 1 - c if k & 1 else c)
            cp = _remote(v_ref, gath.at[me], send.at[k - 1], recv.at[k - 1], peer)
            cp.start()
            cps.append(cp)
        for cp in cps:
            cp.wait_recv()
        for cp in cps:
            cp.wait_send()
        acc = gath[0]
        for d in range(1, N_DEV):
            acc = acc + gath[d]
        o_ref[...] = acc

    return pl.pallas_call(
        body,
        name="small_allreduce",
        in_specs=[_vmem()],
        out_specs=_vmem(),
        out_shape=jax.ShapeDtypeStruct(v.shape, f32),
        scratch_shapes=[pltpu.VMEM((N_DEV, rows, LANES), f32), pltpu.SemaphoreType.DMA((N_DEV - 1,)), pltpu.SemaphoreType.DMA((N_DEV - 1,))],
    )(v)


def _adamw_math(w, g, m, v):
    m = ADAM_B1 * m + (1.0 - ADAM_B1) * g
    v = ADAM_B2 * v + (1.0 - ADAM_B2) * (g * g)
    m_hat = m / (1.0 - ADAM_B1 ** ADAM_STEP)
    v_hat = v / (1.0 - ADAM_B2 ** ADAM_STEP)
    delta = -ADAM_LR * (m_hat / (jnp.sqrt(v_hat) + ADAM_EPS) + ADAM_WD * w)
    return delta, m, v


def _adamw(w, g, m, v, *, name):
    R, C = w.shape
    tr = _div(R, 128, SUBLANES)

    def body(w_ref, g_ref, m_ref, v_ref, d_ref, mo_ref, vo_ref):
        d, mn, vn = _adamw_math(w_ref[...], g_ref[...], m_ref[...], v_ref[...])
        d_ref[...] = d
        mo_ref[...] = mn
        vo_ref[...] = vn

    spec = pl.BlockSpec((tr, C), lambda r: (r, 0))
    return pl.pallas_call(
        body,
        name=name,
        grid=(R // tr,),
        in_specs=[spec] * 4,
        out_specs=[spec] * 3,
        out_shape=[jax.ShapeDtypeStruct((R, C), f32)] * 3,
        compiler_params=pltpu.CompilerParams(dimension_semantics=("parallel",)),
    )(w, g, m, v)


def _pack(arrs):
    flat = jnp.concatenate([a.reshape(-1).astype(f32) for a in arrs])
    tile = SUBLANES * LANES
    total = -(-flat.shape[0] // tile) * tile
    return jnp.pad(flat, (0, total - flat.shape[0])).reshape(total // LANES, LANES)


def _unpack(buf, shapes):
    flat = buf.reshape(-1)
    out, off = [], 0
    for s in shapes:
        size = int(np.prod(s))
        out.append(flat[off:off + size].reshape(s))
        off += size
    return out


def kernel(x, p, norm_mix, w_in, lb_logits, hg_norm, rel_bias, w_out, norm_ffn, w_up, conv_w, conv_b, w_down, norm_ple, w_ple_gate, w_ple_proj, final_norm, loss_target, m_norm_mix, m_w_in, m_lb_logits, m_hg_norm, m_rel_bias, m_w_out, m_norm_ffn, m_w_up, m_conv_w, m_conv_b, m_w_down, m_norm_ple, m_w_ple_gate, m_w_ple_proj, m_final_norm, v_norm_mix, v_w_in, v_lb_logits, v_hg_norm, v_rel_bias, v_w_out, v_norm_ffn, v_w_up, v_conv_w, v_conv_b, v_w_down, v_norm_ple, v_w_ple_gate, v_w_ple_proj, v_final_norm):
    S, D = x.shape[1], x.shape[2]
    xv, pv, tgt = x[0], p[0, 0], loss_target[0]
    W = (w_in.shape[2] * N_CHIPS) // 7
    H = W // HEAD_DIM
    F = w_down.shape[1] * N_CHIPS
    tc = _div(F // 2, 1408, LANES)
    nj = F // tc
    jx, jy = lax.axis_index("x"), lax.axis_index("y")
    chip = 2 * jx + jy

    big = [w_in[0], w_out[0], w_up[0], w_down[0], w_ple_gate[0], w_ple_proj[0]]
    big_axes = [1, 0, 1, 0, 0, 1]
    where = jnp.stack([chip, lax.axis_index("c")]).astype(jnp.int32)
    cw_pad = jnp.pad(conv_w[0], ((0, SUBLANES - conv_w.shape[1]), (0, 0)))
    cw_mine = lax.dynamic_update_slice(jnp.zeros((N_CHIPS * SUBLANES, cw_pad.shape[1]), f32), cw_pad, (chip * SUBLANES, 0))
    placed = [_cast_place(b, ax, where, name=f"cast_place_{i}") for i, (b, ax) in enumerate(zip(big, big_axes))]
    shard_shapes = [b.shape for b in big]

    def gather(idx, extra=()):
        return _gather_rider([placed[i] for i in idx] + list(extra), [shard_shapes[i] for i in idx] + [cw_pad.shape] * len(extra),
                             [big_axes[i] for i in idx] + [0] * len(extra), [True] * len(idx) + [False] * len(extra))

    (W_in,) = _run_rider(gather([0]), name="gather_w_in")

    tm = _div(S, 1024, LANES)

    a1 = _rms_fwd(xv, norm_mix, name="rms_mix")
    proj = _mm(a1, W_in, dims="nn", tm=tm, tn=_div(7 * W, 512, LANES), tk=D, out_dtype=f32, name="mm_in")
    y_hg, o_hg, st, W_out, W_up = _hgrn_fwd(proj, lb_logits, hg_norm, W, rider=gather([1, 2]))

    qt, pad, kb, ne = _att_dims(S)
    nrel = rel_bias.shape[2]
    nrel_pad = -(-nrel // LANES) * LANES
    onehot = _bias_onehot(qt, ne, nrel_pad)
    rb_pad = jnp.pad(rel_bias[0], ((0, 0), (0, nrel_pad - nrel)))[:, None, :]
    ext = _bias_ext(rb_pad, onehot)
    kp = jnp.pad(proj[:, 5 * W:6 * W].astype(bf16), ((pad, 0), (0, 0)))
    vp = jnp.pad(proj[:, 6 * W:7 * W].astype(bf16), ((pad, 0), (0, 0)))
    y_att, W_down = _attn_fwd(proj, kp, vp, ext, W, rider=gather([3]))
    ycat = jnp.concatenate([y_hg, y_att], axis=1)

    tn_d = _div(D, 512, LANES)
    h1 = _mm(ycat, W_out, dims="nn", tm=tm, tn=tn_d, tk=D, out_dtype=f32, res=xv, name="mm_out")
    a2 = _rms_fwd(h1, norm_ffn, name="rms_ffn")
    perm = lambda b: (b % 2) * nj + b // 2
    u, W_pg, W_pp, cw_all = _mm(a2, W_up, dims="nn", tm=tm, tn=tc, tk=D, out_dtype=f32, name="mm_up",
                                b_idx=lambda i, j, k: (k, perm(j)), rider=gather([4, 5], extra=[cw_mine]))
    ncw = conv_w.shape[1]
    cw_full = jnp.transpose(cw_all.reshape(N_CHIPS, SUBLANES, -1)[:, :ncw], (1, 0, 2)).reshape(ncw, -1)
    cb = conv_b
    mact = _convact_fwd(u, cw_full, cb, tc)
    h2 = _mm(mact, W_down, dims="nn", tm=tm, tn=tn_d, tk=_div(F, 1408, LANES), out_dtype=f32, res=h1, name="mm_down")
    a3 = _rms_fwd(h2, norm_ple, name="rms_ple")
    zg = _mm(a3, W_pg, dims="nn", tm=tm, tn=tn_d, tk=D, out_dtype=f32, name="mm_ple_gate")
    p16 = pv.astype(bf16)
    pp = _mm(p16, W_pp, dims="nn", tm=tm, tn=tn_d, tk=pv.shape[1], out_dtype=f32, name="mm_ple_proj")
    dh3, dzg, dpp, loss_part, d_fn = _tail(h2, zg, pp, tgt, final_norm.reshape(1, D))

    pair, others = {}, {}

    def reduce_start(idx, grads):
        got = _rs_sibling(grads, [shard_shapes[i] for i in idx], [big_axes[i] for i in idx], name="grads_to_sibling_" + "_".join(map(str, idx)))
        for i, d, g in zip(idx, grads, got):
            pair[i] = _pair_sum(d, g, shard_shapes[i], big_axes[i], where, name=f"grad_pair_sum_{i}")
        return _chips_rider([pair[i] for i in idx])

    def reduce_landed(idx, landed):
        for i, o in zip(idx, landed):
            others[i] = o

    tk_s = _div(S, 2048, LANES)
    dW_pp = _mm(p16, dpp, dims="tn", tm=pv.shape[1], tn=tn_d, tk=tk_s, out_dtype=bf16, name="mm_d_ple_proj")
    dW_pg = _mm(a3, dzg, dims="tn", tm=tn_d, tn=tn_d, tk=tk_s, out_dtype=bf16, name="mm_d_ple_gate")
    da3 = _mm(dzg, W_pg, dims="nt", tm=tm, tn=_div(D, 1024, LANES), tk=D, out_dtype=f32, name="mm_da3")
    dh2, dh2b, d_nple = _rms_bwd(h2, norm_ple, da3, dh3, name="rms_ple_bwd")
    dm, *landed = _mm(dh2b, W_down, dims="nt", tm=tm, tn=_div(F, 512, LANES), tk=D, out_dtype=f32, name="mm_dm",
                      rider=reduce_start([4, 5], [dW_pg, dW_pp]))
    reduce_landed([4, 5], landed)
    dW_down = _mm(mact, dh2b, dims="tn", tm=_div(F, 512, LANES), tn=tn_d, tk=tk_s, out_dtype=bf16, name="mm_d_down")
    duc, dcw_g, dcw_v, dcb_g, dcb_v, *landed = _convact_bwd_a(u, dm, cw_full, cb, tc, rider=reduce_start([3], [dW_down]))
    reduce_landed([3], landed)
    du = _convact_bwd_b(duc, cw_full, tc)
    dW_up = _mm(a2, du, dims="tn", tm=_div(D, 1024, LANES), tn=tc, tk=tk_s, out_dtype=bf16, name="mm_d_up",
                o_idx=lambda i, j, k: (i, perm(j)))
    da2, *landed = _mm(du, W_up, dims="nt", tm=tm, tn=_div(D, 1024, LANES), tk=tc, out_dtype=f32, name="mm_da2",
                       b_idx=lambda i, j, k: (j, perm(k)), rider=reduce_start([2], [dW_up]))
    reduce_landed([2], landed)
    dh1, dh1b, d_nffn = _rms_bwd(h1, norm_ffn, da2, dh2, name="rms_ffn_bwd")
    dycat = _mm(dh1b, W_out, dims="nt", tm=tm, tn=_div(D, 1024, LANES), tk=D, out_dtype=f32, name="mm_dycat")
    dW_out = _mm(ycat, dh1b, dims="tn", tm=tn_d, tn=tn_d, tk=tk_s, out_dtype=bf16, name="mm_d_out")
    dq_att, dk_pad, dv_pad, dbias, *landed = _attn_bwd(proj, kp, vp, ext, dycat, W, rider=reduce_start([1], [dW_out]))
    reduce_landed([1], landed)
    dhq, dhf, dhi, dhg, d_lbl, d_hgn = _hgrn_bwd(proj, o_hg, st, dycat, lb_logits, hg_norm, W)
    d_rb = _bias_bwd(dbias, onehot)[:, 0, :nrel]
    dproj = jnp.concatenate([dhq, dhf, dhi, dhg, dq_att, dk_pad[pad:].astype(bf16), dv_pad[pad:].astype(bf16)], axis=1)
    dW_in = _mm(a1, dproj, dims="tn", tm=_div(D, 1024, LANES), tn=_div(7 * W, 512, LANES), tk=tk_s, out_dtype=bf16, name="mm_d_in")
    da1, *landed = _mm(dproj, W_in, dims="nt", tm=tm, tn=_div(D, 1024, LANES), tk=_div(7 * W, 1792, LANES), out_dtype=f32, name="mm_da1",
                       rider=reduce_start([0], [dW_in]))
    reduce_landed([0], landed)
    grad_x, _, d_nmix = _rms_bwd(xv, norm_mix, da1, dh1, name="rms_mix_bwd")

    d_cw = jnp.concatenate([dcw_g, dcw_v], axis=1)
    d_cb = jnp.concatenate([dcb_g, dcb_v], axis=1)
    small_parts = [loss_part[:, :1], d_nmix, d_lbl, d_hgn, d_rb, d_nffn, d_cw, d_cb, d_nple, d_fn]
    small_shapes = [(1, 1), (1, D), lb_logits.shape, hg_norm.shape, (H, nrel), (1, D), (ncw, 2 * F), (1, 2 * F), (1, D), (1, D)]
    red = _unpack(_allreduce_small(_pack(small_parts)), small_shapes)
    loss = red[0].reshape(())
    g_nmix, g_lbl, g_hgn, g_rb, g_nffn, g_cw_all, g_cb, g_nple, g_fn = red[1:]
    csh = conv_w.shape[2]
    g_cw = lax.dynamic_slice(g_cw_all, (0, chip * csh), (ncw, csh))
    small_g = [g_nmix, g_lbl, g_hgn, g_rb[None], g_nffn, g_cw[None], g_cb, g_nple, g_fn.reshape(D)]
    small_w = [norm_mix, lb_logits, hg_norm, rel_bias, norm_ffn, conv_w, conv_b, norm_ple, final_norm]
    small_m = [m_norm_mix, m_lb_logits, m_hg_norm, m_rel_bias, m_norm_ffn, m_conv_w, m_conv_b, m_norm_ple, m_final_norm]
    small_v = [v_norm_mix, v_lb_logits, v_hg_norm, v_rel_bias, v_norm_ffn, v_conv_w, v_conv_b, v_norm_ple, v_final_norm]
    shapes_s = [w.shape for w in small_w]
    sd, sm, sv = _adamw(_pack(small_w), _pack(small_g), _pack(small_m), _pack(small_v), name="adamw_small")
    small_g = [g.reshape(s) for g, s in zip(small_g, shapes_s)]
    small_d, small_nm, small_nv = _unpack(sd, shapes_s), _unpack(sm, shapes_s), _unpack(sv, shapes_s)

    halves = [_chip_sum(pair[i], others[i], where, name=f"grad_chip_sum_{i}") for i in range(6)]
    g_big = _rs_join(halves)
    big_m = [m_w_in[0], m_w_out[0], m_w_up[0], m_w_down[0], m_w_ple_gate[0], m_w_ple_proj[0]]
    big_v = [v_w_in[0], v_w_out[0], v_w_up[0], v_w_down[0], v_w_ple_gate[0], v_w_ple_proj[0]]
    big_d, big_nm, big_nv = [], [], []
    for i in range(6):
        d_, m_, v_ = _adamw(big[i], g_big[i], big_m[i], big_v[i], name=f"adamw_{i}")
        big_d.append(d_[None])
        big_nm.append(m_[None])
        big_nv.append(v_[None])
    g_big = [g[None] for g in g_big]

    def order(sm_list, bg_list):
        s, b = sm_list, bg_list
        return [s[0], b[0], s[1], s[2], s[3], b[1], s[4], b[2], s[5], s[6], b[3], s[7], b[4], b[5], s[8]]

    return (loss, grad_x[None], *order(small_g, g_big), *order(small_d, big_d), *order(small_nm, big_nm), *order(small_nv, big_nv))
```

```python
import functools
from typing import Callable, NamedTuple

import jax
import jax.numpy as jnp
import numpy as np
from jax import lax
from jax.experimental import pallas as pl
from jax.experimental.pallas import tpu as pltpu

f32 = jnp.float32
bf16 = jnp.bfloat16

CHUNK = 64
HEAD_DIM = 128
LEFT_CHUNKS = 8
REL_CLIP = 128
EPS = 1e-6
HG_SUB = 16
HG_HEADS_PER_STEP = 8
ATT_Q_ROWS = 256
ADAM_LR, ADAM_B1, ADAM_B2, ADAM_EPS, ADAM_WD, ADAM_STEP = 0.001, 0.9, 0.999, 1e-08, 0.01, 10
LANES = 128
SUBLANES = 8
N_CHIPS = 4
N_DEV = 8
MESH = pl.DeviceIdType.MESH
NEG = float(np.finfo(np.float32).min)

NN = (((1,), (0,)), ((), ()))
NT = (((1,), (1,)), ((), ()))
TN = (((0,), (0,)), ((), ()))
HIGHEST = lax.Precision.HIGHEST


def _dot(a, b, dims, precision=None):
    return lax.dot_general(a, b, dims, preferred_element_type=f32, precision=precision)


def _sigmoid(v):
    return 1.0 / (1.0 + jnp.exp(-v))


def _div(n, pref, mult):
    best = None
    d = mult
    while d <= min(n, pref):
        if n % d == 0:
            best = d
        d += mult
    return best if best is not None else n


def _hbm():
    return pl.BlockSpec(memory_space=pltpu.HBM)


def _vmem():
    return pl.BlockSpec(memory_space=pltpu.VMEM)


def _mm(a, b, *, dims, tm, tn, tk, out_dtype, name, res=None, a_idx=None, b_idx=None, o_idx=None, out_shape=None, rider=None):
    if dims == "nn":
        (M, K), (_, N) = a.shape, b.shape
        a_blk, b_blk, dn = (tm, tk), (tk, tn), NN
        a_def, b_def = (lambda i, j, k: (i, k)), (lambda i, j, k: (k, j))
    elif dims == "nt":
        (M, K), (N, _) = a.shape, b.shape
        a_blk, b_blk, dn = (tm, tk), (tn, tk), NT
        a_def, b_def = (lambda i, j, k: (i, k)), (lambda i, j, k: (j, k))
    else:
        (K, M), (_, N) = a.shape, b.shape
        a_blk, b_blk, dn = (tk, tm), (tk, tn), TN
        a_def, b_def = (lambda i, j, k: (k, i)), (lambda i, j, k: (k, j))
    assert M % tm == 0 and N % tn == 0 and K % tk == 0, (name, M, N, K, tm, tn, tk)
    nk = K // tk
    has_res = res is not None

    def body(*refs):
        a_ref, b_ref = refs[0], refs[1]
        res_ref = refs[2] if has_res else None
        o_ref = refs[3] if has_res else refs[2]
        part = _dot(a_ref[...].astype(bf16), b_ref[...].astype(bf16), dn)

        def finish(acc):
            if has_res:
                acc = acc + res_ref[...]
            o_ref[...] = acc.astype(out_dtype)

        if nk == 1:
            finish(part)
        else:
            acc_ref = refs[-1]
            k = pl.program_id(2)

            @pl.when(k == 0)
            def _():
                acc_ref[...] = part

            @pl.when(k > 0)
            def _():
                acc_ref[...] += part

            @pl.when(k == nk - 1)
            def _():
                finish(acc_ref[...])

    o_map = o_idx or (lambda i, j, k: (i, j))
    in_specs = [pl.BlockSpec(a_blk, a_idx or a_def), pl.BlockSpec(b_blk, b_idx or b_def)]
    args = [a, b]
    if has_res:
        in_specs.append(pl.BlockSpec((tm, tn), o_map))
        args.append(res)
    outs = _grid_call(
        body,
        rider,
        name=name,
        grid=(M // tm, N // tn, nk),
        in_specs=in_specs,
        out_specs=[pl.BlockSpec((tm, tn), o_map)],
        out_shape=[jax.ShapeDtypeStruct(out_shape or (M, N), out_dtype)],
        scratch_shapes=[pltpu.VMEM((tm, tn), f32)] if nk > 1 else [],
        args=args,
    )
    return outs[0] if rider is None else outs


def _rms_fwd(xv, g, *, name):
    S, D = xv.shape
    ts = _div(S, 512, SUBLANES)

    def body(x_ref, g_ref, o_ref):
        v = x_ref[...]
        r = lax.rsqrt(jnp.mean(v * v, axis=-1, keepdims=True) + EPS)
        o_ref[...] = (v * r * g_ref[...]).astype(bf16)

    return pl.pallas_call(
        body,
        name=name,
        grid=(S // ts,),
        in_specs=[pl.BlockSpec((ts, D), lambda i: (i, 0)), pl.BlockSpec((1, D), lambda i: (0, 0))],
        out_specs=pl.BlockSpec((ts, D), lambda i: (i, 0)),
        out_shape=jax.ShapeDtypeStruct((S, D), bf16),
        compiler_params=pltpu.CompilerParams(dimension_semantics=("parallel",)),
    )(xv, g)


def _rms_bwd(xv, g, dy, dres, *, name):
    S, D = xv.shape
    ts = _div(S, 256, SUBLANES)

    def body(x_ref, g_ref, dy_ref, dres_ref, dx_ref, dxb_ref, dg_ref):
        i = pl.program_id(0)
        v = x_ref[...]
        r = lax.rsqrt(jnp.mean(v * v, axis=-1, keepdims=True) + EPS)
        vn = v * r
        d = dy_ref[...]
        part = jnp.sum(d * vn, axis=0, keepdims=True)

        @pl.when(i == 0)
        def _():
            dg_ref[...] = part

        @pl.when(i > 0)
        def _():
            dg_ref[...] += part

        t = d * g_ref[...]
        dx = dres_ref[...] + r * (t - vn * jnp.mean(t * vn, axis=-1, keepdims=True))
        dx_ref[...] = dx
        dxb_ref[...] = dx.astype(bf16)

    row = pl.BlockSpec((ts, D), lambda i: (i, 0))
    vec = pl.BlockSpec((1, D), lambda i: (0, 0))
    return pl.pallas_call(
        body,
        name=name,
        grid=(S // ts,),
        in_specs=[row, vec, row, row],
        out_specs=[row, row, vec],
        out_shape=[jax.ShapeDtypeStruct((S, D), f32), jax.ShapeDtypeStruct((S, D), bf16), jax.ShapeDtypeStruct((1, D), f32)],
        compiler_params=pltpu.CompilerParams(dimension_semantics=("arbitrary",)),
    )(xv, g, dy, dres)


def _tail(h2, zg, pp, tgt, fn):
    S, D = h2.shape
    ts = _div(S, 256, SUBLANES)

    def body(h_ref, z_ref, p_ref, t_ref, fn_ref, dh_ref, dz_ref, dp_ref, loss_ref, dfn_ref):
        i = pl.program_id(0)
        sg = _sigmoid(z_ref[...])
        ppv = p_ref[...]
        h3 = h_ref[...] + sg * ppv
        r = lax.rsqrt(jnp.mean(h3 * h3, axis=-1, keepdims=True) + EPS)
        hn = h3 * r
        fnv = fn_ref[...]
        e = hn * fnv - t_ref[...]
        lpart = 0.5 * jnp.sum(jnp.mean(e * e, axis=-1, keepdims=True), axis=0, keepdims=True)
        dy = e * (1.0 / D)
        dpart = jnp.sum(dy * hn, axis=0, keepdims=True)

        @pl.when(i == 0)
        def _():
            loss_ref[...] = jnp.broadcast_to(lpart, loss_ref.shape)
            dfn_ref[...] = dpart

        @pl.when(i > 0)
        def _():
            loss_ref[...] += jnp.broadcast_to(lpart, loss_ref.shape)
            dfn_ref[...] += dpart

        t = dy * fnv
        dh3 = r * (t - hn * jnp.mean(t * hn, axis=-1, keepdims=True))
        dh_ref[...] = dh3
        dz_ref[...] = (dh3 * ppv * sg * (1.0 - sg)).astype(bf16)
        dp_ref[...] = (dh3 * sg).astype(bf16)

    row = pl.BlockSpec((ts, D), lambda i: (i, 0))
    vec = pl.BlockSpec((1, D), lambda i: (0, 0))
    one = pl.BlockSpec((1, LANES), lambda i: (0, 0))
    return pl.pallas_call(
        body,
        name="ple_final_loss",
        grid=(S // ts,),
        in_specs=[row, row, row, row, vec],
        out_specs=[row, row, row, one, vec],
        out_shape=[
            jax.ShapeDtypeStruct((S, D), f32),
            jax.ShapeDtypeStruct((S, D), bf16),
            jax.ShapeDtypeStruct((S, D), bf16),
            jax.ShapeDtypeStruct((1, LANES), f32),
            jax.ShapeDtypeStruct((1, D), f32),
        ],
        compiler_params=pltpu.CompilerParams(dimension_semantics=("arbitrary",)),
    )(h2, zg, pp, tgt, fn)


def _shift_down(v, halo, k):
    r = pltpu.roll(v, k, 0)
    hr = pltpu.roll(halo, k, 0)
    row = lax.broadcasted_iota(jnp.int32, hr.shape, 0)
    top = jnp.where(row < k, hr, r[0:SUBLANES])
    return jnp.concatenate([top, r[SUBLANES:]], axis=0)


def _shift_up(v, halo, k):
    n = v.shape[0]
    r = pltpu.roll(v, n - k, 0)
    hr = pltpu.roll(halo, SUBLANES - k, 0)
    row = lax.broadcasted_iota(jnp.int32, hr.shape, 0)
    bot = jnp.where(row >= SUBLANES - k, hr, r[n - SUBLANES:])
    return jnp.concatenate([r[: n - SUBLANES], bot], axis=0)


def _conv_specs(S, F, tc, ts):
    nj = F // tc
    rows8 = ts // SUBLANES
    main = pl.BlockSpec((ts, 2 * tc), lambda j, i: (i, j))
    prev = pl.BlockSpec((SUBLANES, 2 * tc), lambda j, i: (jnp.maximum(i * rows8 - 1, 0), j))
    nxt = pl.BlockSpec((SUBLANES, 2 * tc), lambda j, i: (jnp.minimum((i + 1) * rows8, S // SUBLANES - 1), j))
    wg = pl.BlockSpec((3, tc), lambda j, i: (0, j))
    wv = pl.BlockSpec((3, tc), lambda j, i: (0, nj + j))
    bg = pl.BlockSpec((1, tc), lambda j, i: (0, j))
    bv = pl.BlockSpec((1, tc), lambda j, i: (0, nj + j))
    half = pl.BlockSpec((ts, tc), lambda j, i: (i, j))
    return nj, main, prev, nxt, wg, wv, bg, bv, half


def _conv_pre(u_ref, h_ref, wg_ref, wv_ref, bg_ref, bv_ref, tc):
    i = pl.program_id(1)
    u = u_ref[...]
    halo = jnp.where(i == 0, 0.0, h_ref[...])
    u1 = _shift_down(u, halo, 1)
    u2 = _shift_down(u, halo, 2)
    w = jnp.concatenate([wg_ref[...], wv_ref[...]], axis=1)
    b = jnp.concatenate([bg_ref[...], bv_ref[...]], axis=1)
    uc = b + w[0:1] * u2 + w[1:2] * u1 + w[2:3] * u
    return u, u1, u2, uc[:, :tc], uc[:, tc:]


def _convact_fwd(u, cw, cb, tc):
    S, F2 = u.shape
    F = F2 // 2
    ts = _div(S, 512, SUBLANES)
    nj, main, prev, nxt, wg, wv, bg, bv, half = _conv_specs(S, F, tc, ts)

    def body(u_ref, h_ref, wg_ref, wv_ref, bg_ref, bv_ref, m_ref):
        _, _, _, g, v = _conv_pre(u_ref, h_ref, wg_ref, wv_ref, bg_ref, bv_ref, tc)
        m_ref[...] = (g * _sigmoid(g) * v).astype(bf16)

    return pl.pallas_call(
        body,
        name="convact_fwd",
        grid=(nj, S // ts),
        in_specs=[main, prev, wg, wv, bg, bv],
        out_specs=half,
        out_shape=jax.ShapeDtypeStruct((S, F), bf16),
        compiler_params=pltpu.CompilerParams(dimension_semantics=("parallel", "parallel")),
    )(u, u, cw, cw, cb, cb)


def _convact_bwd_a(u, dm, cw, cb, tc, rider=None):
    S, F2 = u.shape
    F = F2 // 2
    ts = _div(S, 512, SUBLANES)
    nj, main, prev, nxt, wg, wv, bg, bv, half = _conv_specs(S, F, tc, ts)

    def body(u_ref, h_ref, dm_ref, wg_ref, wv_ref, bg_ref, bv_ref, duc_ref, dwg_ref, dwv_ref, dbg_ref, dbv_ref):
        i = pl.program_id(1)
        u0, u1, u2, g, v = _conv_pre(u_ref, h_ref, wg_ref, wv_ref, bg_ref, bv_ref, tc)
        d = dm_ref[...]
        sg = _sigmoid(g)
        dg = d * v * sg * (1.0 + g * (1.0 - sg))
        dv = d * g * sg
        duc = jnp.concatenate([dg, dv], axis=1)
        duc_ref[...] = duc
        db = jnp.sum(duc, axis=0, keepdims=True)
        dw = jnp.concatenate(
            [jnp.sum(duc * u2, axis=0, keepdims=True), jnp.sum(duc * u1, axis=0, keepdims=True), jnp.sum(duc * u0, axis=0, keepdims=True)],
            axis=0,
        )

        @pl.when(i == 0)
        def _():
            dwg_ref[...] = dw[:, :tc]
            dwv_ref[...] = dw[:, tc:]
            dbg_ref[...] = db[:, :tc]
            dbv_ref[...] = db[:, tc:]

        @pl.when(i > 0)
        def _():
            dwg_ref[...] += dw[:, :tc]
            dwv_ref[...] += dw[:, tc:]
            dbg_ref[...] += db[:, :tc]
            dbv_ref[...] += db[:, tc:]

    w_out = pl.BlockSpec((3, tc), lambda j, i: (0, j))
    b_out = pl.BlockSpec((1, tc), lambda j, i: (0, j))
    return _grid_call(
        body,
        rider,
        name="convact_bwd_a",
        grid=(nj, S // ts),
        in_specs=[main, prev, half, wg, wv, bg, bv],
        out_specs=[main, w_out, w_out, b_out, b_out],
        out_shape=[
            jax.ShapeDtypeStruct((S, F2), f32),
            jax.ShapeDtypeStruct((3, F), f32),
            jax.ShapeDtypeStruct((3, F), f32),
            jax.ShapeDtypeStruct((1, F), f32),
            jax.ShapeDtypeStruct((1, F), f32),
        ],
        scratch_shapes=[],
        args=(u, u, dm, cw, cw, cb, cb),
    )


def _convact_bwd_b(duc, cw, tc):
    S, F2 = duc.shape
    F = F2 // 2
    ts = _div(S, 512, SUBLANES)
    nj, main, prev, nxt, wg, wv, bg, bv, half = _conv_specs(S, F, tc, ts)
    ni = S // ts

    def body(d_ref, h_ref, wg_ref, wv_ref, du_ref):
        i = pl.program_id(1)
        d = d_ref[...]
        halo = jnp.where(i == ni - 1, 0.0, h_ref[...])
        d1 = _shift_up(d, halo, 1)
        d2 = _shift_up(d, halo, 2)
        w = jnp.concatenate([wg_ref[...], wv_ref[...]], axis=1)
        du_ref[...] = (w[2:3] * d + w[1:2] * d1 + w[0:1] * d2).astype(bf16)

    return pl.pallas_call(
        body,
        name="convact_bwd_b",
        grid=(nj, ni),
        in_specs=[main, nxt, wg, wv],
        out_specs=main,
        out_shape=jax.ShapeDtypeStruct((S, F2), bf16),
        compiler_params=pltpu.CompilerParams(dimension_semantics=("parallel", "parallel")),
    )(duc, duc, cw, cw)


def _hg_gates(qp, fp, lbl):
    lb = _sigmoid(lbl[0:1] - lbl[1:2])
    sig = _sigmoid(fp)
    sigm = _sigmoid(-fp)
    f = lb + (1.0 - lb) * sig
    k = (1.0 - lb) * sigm
    sq = _sigmoid(qp)
    qf = qp * sq
    row = lax.broadcasted_iota(jnp.int32, (CHUNK, CHUNK), 0)
    col = lax.broadcasted_iota(jnp.int32, (CHUNK, CHUNK), 1)
    b = _dot((row >= col).astype(f32), jnp.log(f), NN, HIGHEST)
    return lb, sig, sigm, f, k, sq, qf, b


def _hg_block(qf, k, b, I):
    r0, n = I * HG_SUB, (I + 1) * HG_SUB
    base = b[r0 - 1:r0] if I > 0 else jnp.zeros_like(b[0:1])
    eq = jnp.exp(b[r0:n] - base)
    ek = jnp.exp(base - b[0:n])
    qt16 = (qf[r0:n] * eq).astype(bf16)
    kt16 = (k[0:n] * ek).astype(bf16)
    row = lax.broadcasted_iota(jnp.int32, (HG_SUB, n), 0) + r0
    col = lax.broadcasted_iota(jnp.int32, (HG_SUB, n), 1)
    return r0, n, eq, ek, qt16, kt16, col <= row


def _per_head(v, hp, fn):
    return jnp.concatenate([fn(v[:, p * HEAD_DIM:(p + 1) * HEAD_DIM]) for p in range(hp)], axis=1)


def _head_mean(v, hp):
    return _per_head(v, hp, lambda t: jnp.broadcast_to(jnp.mean(t, axis=-1, keepdims=True), t.shape))


def _pad_rows(v, rows):
    return v if v.shape[0] == rows else jnp.concatenate([v, jnp.zeros((rows - v.shape[0], v.shape[1]), v.dtype)], axis=0)


def _hgrn_specs(S, W, hp, reverse):
    nc = S // CHUNK
    ngrp = W // (hp * HEAD_DIM)
    cw = hp * HEAD_DIM
    cidx = (lambda c: nc - 1 - c) if reverse else (lambda c: c)

    def proj(off):
        return pl.BlockSpec((CHUNK, cw), lambda h, c: (cidx(c), off * ngrp + h))

    act = pl.BlockSpec((CHUNK, cw), lambda h, c: (cidx(c), h))
    lbl = pl.BlockSpec((2, cw), lambda h, c: (0, h))
    gn = pl.BlockSpec((1, HEAD_DIM), lambda h, c: (0, 0))
    st = pl.BlockSpec((hp, 1, HEAD_DIM, HEAD_DIM), lambda h, c: (h, cidx(c), 0, 0))
    return nc, ngrp, proj, act, lbl, gn, st


def _grid_call(body, rider, *, name, grid, in_specs, out_specs, out_shape, scratch_shapes, args):
    params = pltpu.CompilerParams(dimension_semantics=("arbitrary",) * len(grid))
    if rider is None:
        return pl.pallas_call(body, name=name, grid=grid, in_specs=in_specs, out_specs=out_specs, out_shape=out_shape,
                              scratch_shapes=scratch_shapes, compiler_params=params)(*args)
    n_in, n_out, n_scr = len(in_specs), len(out_specs), len(scratch_shapes)
    ri, ro = len(rider.ins), len(rider.out_shapes)

    def wrapped(*refs):
        a, b = n_in, n_in + ri
        c, d = b + n_out, b + n_out + ro
        e = d + n_scr
        ids = [pl.program_id(t) for t in range(len(grid))]
        is_first = functools.reduce(jnp.logical_and, [i == 0 for i in ids])
        is_last = functools.reduce(jnp.logical_and, [i == g - 1 for i, g in zip(ids, grid)])

        @pl.when(is_first)
        def _():
            rider.first(refs[a:b], refs[c:d], refs[e:])

        body(*refs[:a], *refs[b:c], *refs[d:e])

        @pl.when(is_last)
        def _():
            rider.last(refs[a:b], refs[c:d], refs[e:])

    return pl.pallas_call(
        wrapped,
        name=name,
        grid=grid,
        in_specs=list(in_specs) + [_hbm()] * ri,
        out_specs=list(out_specs) + [_hbm()] * ro,
        out_shape=list(out_shape) + list(rider.out_shapes),
        input_output_aliases={n_in + i: n_out + o for i, o in rider.aliases.items()},
        scratch_shapes=list(scratch_shapes) + list(rider.sems),
        compiler_params=params,
    )(*args, *rider.ins)


def _hgrn_fwd(proj, lb_logits, hg_norm, W, rider=None):
    S = proj.shape[0]
    hp = min(HG_HEADS_PER_STEP, W // HEAD_DIM)
    nc, ngrp, pspec, act, lbl_spec, gn_spec, st_spec = _hgrn_specs(S, W, hp, False)
    nb = CHUNK // HG_SUB

    def body(q_ref, f_ref, i_ref, g_ref, lbl_ref, gn_ref, y_ref, o_ref, st_ref, state):
        c = pl.program_id(1)

        @pl.when(c == 0)
        def _():
            state[...] = jnp.zeros_like(state)

        _, _, _, _, k, _, qf, b = _hg_gates(q_ref[...], f_ref[...], lbl_ref[...])
        v16 = i_ref[...].astype(bf16)
        a16 = (qf * jnp.exp(b)).astype(bf16)
        bl = b[CHUNK - 1:CHUNK]
        kd16 = (k * jnp.exp(bl - b)).astype(bf16)
        ebl = jnp.exp(bl)
        blocks = [_hg_block(qf, k, b, I) for I in range(nb)]
        heads = [slice(p * HEAD_DIM, (p + 1) * HEAD_DIM) for p in range(hp)]
        st_prev = state[...]
        st_ref[:, 0] = st_prev
        inter = [_dot(a16[:, sl], st_prev[p].astype(bf16), NT) for p, sl in enumerate(heads)]
        scores = [[_dot(qt16[:, sl], kt16[:, sl], NT) for sl in heads] for _, _, _, _, qt16, kt16, _ in blocks]
        intra = [[_dot(jnp.where(blk[6], sc, 0.0).astype(bf16), v16[0:blk[1], sl], NN) for sc, sl in zip(scs, heads)]
                 for blk, scs in zip(blocks, scores)]
        st_new = [st_prev[p] * ebl[:, sl] + _dot(v16[:, sl], kd16[:, sl], TN) for p, sl in enumerate(heads)]
        state[...] = jnp.stack(st_new)
        o = jnp.concatenate([inter[p] + jnp.concatenate([rows[p] for rows in intra], axis=0) for p in range(hp)], axis=1)
        o_ref[...] = o
        r = lax.rsqrt(_head_mean(o * o, hp) + EPS)
        gp = g_ref[...]
        y_ref[...] = (o * r * jnp.tile(gn_ref[...], (1, hp)) * (gp * _sigmoid(gp))).astype(bf16)

    H = W // HEAD_DIM
    return _grid_call(
        body,
        rider,
        name="hgrn_fwd",
        grid=(ngrp, nc),
        in_specs=[pspec(0), pspec(1), pspec(2), pspec(3), lbl_spec, gn_spec],
        out_specs=[act, act, st_spec],
        out_shape=[
            jax.ShapeDtypeStruct((S, W), bf16),
            jax.ShapeDtypeStruct((S, W), f32),
            jax.ShapeDtypeStruct((H, nc, HEAD_DIM, HEAD_DIM), f32),
        ],
        scratch_shapes=[pltpu.VMEM((hp, HEAD_DIM, HEAD_DIM), f32)],
        args=(proj, proj, proj, proj, lb_logits, hg_norm),
    )


def _hgrn_bwd(proj, o_hg, st, dycat, lb_logits, hg_norm, W, rider=None):
    S = proj.shape[0]
    hp = min(HG_HEADS_PER_STEP, W // HEAD_DIM)
    nc, ngrp, pspec, act, lbl_spec, gn_spec, st_spec = _hgrn_specs(S, W, hp, True)
    nb = CHUNK // HG_SUB

    def body(q_ref, f_ref, i_ref, g_ref, o_ref, st_ref, dy_ref, lbl_ref, gn_ref,
             dq_ref, df_ref, di_ref, dg_ref, dlbl_ref, dgn_ref, dstate):
        h = pl.program_id(0)
        c = pl.program_id(1)

        @pl.when(c == 0)
        def _():
            dstate[...] = jnp.zeros_like(dstate)
            dlbl_ref[...] = jnp.zeros_like(dlbl_ref)

        @pl.when((c == 0) & (h == 0))
        def _():
            dgn_ref[...] = jnp.zeros_like(dgn_ref)

        row = lax.broadcasted_iota(jnp.int32, (CHUNK, CHUNK), 0)
        col = lax.broadcasted_iota(jnp.int32, (CHUNK, CHUNK), 1)
        upper = (col >= row).astype(f32)
        last_row = lax.broadcasted_iota(jnp.int32, (CHUNK, hp * HEAD_DIM), 0) == CHUNK - 1

        qp, fp = q_ref[...], f_ref[...]
        lb, sig, sigm, f, k, sq, qf, b = _hg_gates(qp, fp, lbl_ref[...])
        v16 = i_ref[...].astype(bf16)
        gnv = jnp.tile(gn_ref[...], (1, hp))
        gp = g_ref[...]
        sgg = _sigmoid(gp)
        gate = gp * sgg
        o = o_ref[...]
        r = lax.rsqrt(_head_mean(o * o, hp) + EPS)
        on = o * r
        dy = dy_ref[...]
        dg_ref[...] = (dy * on * gnv * (sgg * (1.0 + gp * (1.0 - sgg)))).astype(bf16)
        dgn_wide = jnp.sum(dy * on * gate, axis=0, keepdims=True)
        dgn_ref[...] += functools.reduce(jnp.add, [dgn_wide[:, p * HEAD_DIM:(p + 1) * HEAD_DIM] for p in range(hp)])
        don = dy * gnv * gate
        do16 = (r * (don - on * _head_mean(don * on, hp))).astype(bf16)
        eb = jnp.exp(b)
        A = qf * eb
        a16 = A.astype(bf16)
        bl = b[CHUNK - 1:CHUNK]
        ebl = jnp.exp(bl)
        ekd = jnp.exp(bl - b)
        kd = k * ekd
        kd16 = kd.astype(bf16)
        blocks = [_hg_block(qf, k, b, I) for I in range(nb)]

        heads = [slice(p * HEAD_DIM, (p + 1) * HEAD_DIM) for p in range(hp)]
        st_prev = st_ref[:, 0]
        dst_new = dstate[...]
        st16, dst16 = st_prev.astype(bf16), dst_new.astype(bf16)
        dA_h = [_dot(do16[:, sl], st16[p], NN) for p, sl in enumerate(heads)]
        dkd_h = [_dot(v16[:, sl], dst16[p], NN) for p, sl in enumerate(heads)]
        dv_h = [_dot(kd16[:, sl], dst16[p], NT) for p, sl in enumerate(heads)]
        dstate[...] = jnp.stack([dst_new[p] * ebl[:, sl] + _dot(do16[:, sl], a16[:, sl], TN) for p, sl in enumerate(heads)])
        dbl_h = [jnp.sum(dst_new[p] * st_prev[p], axis=0, keepdims=True) for p in range(hp)]
        sc_h = [[jnp.where(mask, _dot(qt16[:, sl], kt16[:, sl], NT), 0.0).astype(bf16) for sl in heads]
                for _, _, _, _, qt16, kt16, mask in blocks]
        dsc_h = [[jnp.where(mask, _dot(do16[r0:n, sl], v16[0:n, sl], NT), 0.0).astype(bf16) for sl in heads]
                 for r0, n, _, _, _, _, mask in blocks]
        dvi_h = [[_dot(sc, do16[blk[0]:blk[1], sl], TN) for sc, sl in zip(scs, heads)] for blk, scs in zip(blocks, sc_h)]
        dqt_h = [[_dot(dsc, blk[5][:, sl], NN) for dsc, sl in zip(dscs, heads)] for blk, dscs in zip(blocks, dsc_h)]
        dkt_h = [[_dot(dsc, blk[4][:, sl], TN) for dsc, sl in zip(dscs, heads)] for blk, dscs in zip(blocks, dsc_h)]
        dv_h = [functools.reduce(jnp.add, [dv_h[p]] + [_pad_rows(rows[p], CHUNK) for rows in dvi_h]) for p in range(hp)]

        dA, dkd = jnp.concatenate(dA_h, axis=1), jnp.concatenate(dkd_h, axis=1)
        dbl = jnp.concatenate(dbl_h, axis=1) * ebl + jnp.sum(dkd * kd, axis=0, keepdims=True)
        db = dA * A - dkd * kd + jnp.where(last_row, dbl, 0.0)
        dk = dkd * ekd
        dq_rows, db_rows = [], []
        for I, (r0, n, eq, ek, qt16, kt16, _) in enumerate(blocks):
            dqt, dkt = jnp.concatenate(dqt_h[I], axis=1), jnp.concatenate(dkt_h[I], axis=1)
            dq_rows.append(dqt * eq)
            dk = dk + _pad_rows(dkt * ek, CHUNK)
            db_rows.append(dqt * qt16.astype(f32))
            db = db - _pad_rows(dkt * kt16.astype(f32), CHUNK)
        dqf = dA * eb + jnp.concatenate(dq_rows, axis=0)
        db = db + jnp.concatenate(db_rows, axis=0)
        dlogf = _dot(upper, db, NN, HIGHEST)
        dfg = dlogf / f
        df_ref[...] = ((1.0 - lb) * sig * sigm * (dfg - dk)).astype(bf16)
        dlb = jnp.sum(sigm * (dfg - dk), axis=0, keepdims=True)
        dl0 = dlb * lb * (1.0 - lb)
        dlbl_ref[...] += jnp.concatenate([dl0, -dl0], axis=0)
        dq_ref[...] = (dqf * (sq * (1.0 + qp * (1.0 - sq)))).astype(bf16)
        di_ref[...] = jnp.concatenate(dv_h, axis=1).astype(bf16)

    cw = hp * HEAD_DIM
    dy_spec = pl.BlockSpec((CHUNK, cw), lambda h, c: (nc - 1 - c, h))
    return _grid_call(
        body,
        rider,
        name="hgrn_bwd",
        grid=(ngrp, nc),
        in_specs=[pspec(0), pspec(1), pspec(2), pspec(3), act, st_spec, dy_spec, lbl_spec, gn_spec],
        out_specs=[act, act, act, act, lbl_spec, gn_spec],
        out_shape=[jax.ShapeDtypeStruct((S, W), bf16)] * 4
        + [jax.ShapeDtypeStruct((2, W), f32), jax.ShapeDtypeStruct((1, HEAD_DIM), f32)],
        scratch_shapes=[pltpu.VMEM((hp, HEAD_DIM, HEAD_DIM), f32)],
        args=(proj, proj, proj, proj, o_hg, st, dycat, lb_logits, hg_norm),
    )


def _att_dims(S):
    qt = ATT_Q_ROWS if S % ATT_Q_ROWS == 0 else CHUNK
    pad = LEFT_CHUNKS * CHUNK
    kb = pad + qt
    return qt, pad, kb, kb + qt


def _att_scores(q16, kb16, bias, start, qt, pad, kb):
    s = _dot(q16, kb16, NT) * (HEAD_DIM ** -0.5) + bias
    row = lax.broadcasted_iota(jnp.int32, (qt, kb), 0)
    col = lax.broadcasted_iota(jnp.int32, (qt, kb), 1)
    lo = jnp.bitwise_and(row, -CHUNK)
    ok = (col >= lo) & (col < lo + pad + CHUNK) & (col + start >= pad)
    s = jnp.where(ok, s, NEG)
    e = jnp.exp(s - jnp.max(s, axis=-1, keepdims=True))
    return e / jnp.sum(e, axis=-1, keepdims=True)


def _att_bias(ext_ref, qt, kb, ne):
    e = jnp.broadcast_to(ext_ref[0], (qt, ne))
    return pltpu.roll(e, ne - qt + 1, 1, stride=1, stride_axis=0)[:, :kb]


def _attn_fwd(proj, kp, vp, ext, W, rider=None):
    S = proj.shape[0]
    H = W // HEAD_DIM
    qt, pad, kb, ne = _att_dims(S)

    def body(q_ref, k_ref, v_ref, ext_ref, o_ref, bias):
        g = pl.program_id(1)

        @pl.when(g == 0)
        def _():
            bias[...] = _att_bias(ext_ref, qt, kb, ne)

        start = pl.multiple_of(g * qt, qt)
        pn = _att_scores(q_ref[...].astype(bf16), k_ref[pl.ds(start, kb), :], bias[...], start, qt, pad, kb)
        o_ref[...] = _dot(pn.astype(bf16), v_ref[pl.ds(start, kb), :], NN).astype(bf16)

    outs = _grid_call(
        body,
        rider,
        name="attn_fwd",
        grid=(H, S // qt),
        in_specs=[
            pl.BlockSpec((qt, HEAD_DIM), lambda h, g: (g, 4 * H + h)),
            pl.BlockSpec((S + pad, HEAD_DIM), lambda h, g: (0, h)),
            pl.BlockSpec((S + pad, HEAD_DIM), lambda h, g: (0, h)),
            pl.BlockSpec((1, 1, ne), lambda h, g: (h, 0, 0)),
        ],
        out_specs=[pl.BlockSpec((qt, HEAD_DIM), lambda h, g: (g, h))],
        out_shape=[jax.ShapeDtypeStruct((S, W), bf16)],
        scratch_shapes=[pltpu.VMEM((qt, kb), f32)],
        args=(proj, kp, vp, ext),
    )
    return outs[0] if rider is None else outs


def _attn_bwd(proj, kp, vp, ext, dycat, W, rider=None):
    S = proj.shape[0]
    H = W // HEAD_DIM
    qt, pad, kb, ne = _att_dims(S)
    scale = HEAD_DIM ** -0.5

    def body(q_ref, k_ref, v_ref, ext_ref, do_ref, dq_ref, dk_ref, dv_ref, db_ref, bias):
        g = pl.program_id(1)

        @pl.when(g == 0)
        def _():
            bias[...] = _att_bias(ext_ref, qt, kb, ne)
            dk_ref[...] = jnp.zeros_like(dk_ref)
            dv_ref[...] = jnp.zeros_like(dv_ref)
            db_ref[...] = jnp.zeros_like(db_ref)

        start = pl.multiple_of(g * qt, qt)
        q16 = q_ref[...].astype(bf16)
        kb16 = k_ref[pl.ds(start, kb), :]
        vb16 = v_ref[pl.ds(start, kb), :]
        pn = _att_scores(q16, kb16, bias[...], start, qt, pad, kb)
        do16 = do_ref[...].astype(bf16)
        dpn = _dot(do16, vb16, NT)
        dv_ref[pl.ds(start, kb), :] += _dot(pn.astype(bf16), do16, TN)
        ds = pn * (dpn - jnp.sum(dpn * pn, axis=-1, keepdims=True))
        db_ref[0] += ds
        ds16 = ds.astype(bf16)
        dq_ref[...] = (_dot(ds16, kb16, NN) * scale).astype(bf16)
        dk_ref[pl.ds(start, kb), :] += _dot(ds16, q16, TN) * scale

    kv_spec = pl.BlockSpec((S + pad, HEAD_DIM), lambda h, g: (0, h))
    return _grid_call(
        body,
        rider,
        name="attn_bwd",
        grid=(H, S // qt),
        in_specs=[
            pl.BlockSpec((qt, HEAD_DIM), lambda h, g: (g, 4 * H + h)),
            kv_spec,
            kv_spec,
            pl.BlockSpec((1, 1, ne), lambda h, g: (h, 0, 0)),
            pl.BlockSpec((qt, HEAD_DIM), lambda h, g: (g, H + h)),
        ],
        out_specs=[
            pl.BlockSpec((qt, HEAD_DIM), lambda h, g: (g, h)),
            kv_spec,
            kv_spec,
            pl.BlockSpec((1, qt, kb), lambda h, g: (h, 0, 0)),
        ],
        out_shape=[
            jax.ShapeDtypeStruct((S, W), bf16),
            jax.ShapeDtypeStruct((S + pad, W), f32),
            jax.ShapeDtypeStruct((S + pad, W), f32),
            jax.ShapeDtypeStruct((H, qt, kb), f32),
        ],
        scratch_shapes=[pltpu.VMEM((qt, kb), f32)],
        args=(proj, kp, vp, ext, dycat),
    )


def _bias_onehot(qt, ne, nrel_pad):
    m = lax.broadcasted_iota(jnp.int32, (nrel_pad, ne), 1)
    r = lax.broadcasted_iota(jnp.int32, (nrel_pad, ne), 0)
    rel = LEFT_CHUNKS * CHUNK + qt - 1 - m
    hot = (r == jnp.clip(rel, -REL_CLIP, REL_CLIP) + REL_CLIP) & (m < ne - 1)
    return hot.astype(f32)


def _bias_ext(rel_bias_pad, onehot):
    H, _, nr = rel_bias_pad.shape
    ne = onehot.shape[1]

    def body(rb_ref, oh_ref, o_ref):
        o_ref[0] = _dot(rb_ref[0], oh_ref[...], NN, HIGHEST)

    return pl.pallas_call(
        body,
        name="bias_ext",
        grid=(H,),
        in_specs=[pl.BlockSpec((1, 1, nr), lambda h: (h, 0, 0)), pl.BlockSpec((nr, ne), lambda h: (0, 0))],
        out_specs=pl.BlockSpec((1, 1, ne), lambda h: (h, 0, 0)),
        out_shape=jax.ShapeDtypeStruct((H, 1, ne), f32),
        compiler_params=pltpu.CompilerParams(dimension_semantics=("parallel",)),
    )(rel_bias_pad, onehot)


def _bias_bwd(dbias, onehot):
    H, qt, kb = dbias.shape
    nr, ne = onehot.shape

    def body(d_ref, oh_ref, o_ref):
        row = lax.broadcasted_iota(jnp.int32, (qt, qt), 0)
        col = lax.broadcasted_iota(jnp.int32, (qt, qt), 1)
        flip = (row + col == qt - 1).astype(f32)
        x = _dot(flip, d_ref[0], NN, HIGHEST)
        x = jnp.concatenate([x, jnp.zeros((qt, ne - kb), f32)], axis=1)
        de = jnp.sum(pltpu.roll(x, 0, 1, stride=1, stride_axis=0), axis=0, keepdims=True)
        o_ref[0] = _dot(de, oh_ref[...], NT, HIGHEST)

    return pl.pallas_call(
        body,
        name="bias_bwd",
        grid=(H,),
        in_specs=[pl.BlockSpec((1, qt, kb), lambda h: (h, 0, 0)), pl.BlockSpec((nr, ne), lambda h: (0, 0))],
        out_specs=pl.BlockSpec((1, 1, nr), lambda h: (h, 0, 0)),
        out_shape=jax.ShapeDtypeStruct((H, 1, nr), f32),
        compiler_params=pltpu.CompilerParams(dimension_semantics=("parallel",)),
    )(dbias, onehot)


def _place():
    x, y, c = lax.axis_index("x"), lax.axis_index("y"), lax.axis_index("c")
    return x, y, c


def _flip(x, y, k):
    return (1 - x if k & 2 else x), (1 - y if k & 1 else y)


def _region(ref, shard_shape, axis, j, half):
    R, C = shard_shape
    if axis == 0:
        if half is None:
            return ref.at[pl.ds(j * R, R), :]
        return ref.at[pl.ds(j * R + half * (R // 2), R // 2), :]
    if half is None:
        return ref.at[:, pl.ds(j * C, C)]
    return ref.at[pl.ds(half * (R // 2), R // 2), pl.ds(j * C, C)]


def _remote(src, dst, send_sem, recv_sem, dev):
    return pltpu.make_async_remote_copy(src_ref=src, dst_ref=dst, send_sem=send_sem, recv_sem=recv_sem,
                                        device_id=dev, device_id_type=MESH)


def _cast_place(shard, axis, where, *, name):
    R, C = shard.shape
    tr = _div(R, 256, 2 * SUBLANES)
    nr = R // tr
    full = (N_CHIPS * R, C) if axis == 0 else (R, N_CHIPS * C)
    omap = (lambda r, s: (s[0] * nr + r, 0)) if axis == 0 else (lambda r, s: (r, s[0]))

    def body(s_ref, x_ref, o_ref):
        o_ref[...] = x_ref[...].astype(bf16)

    return pl.pallas_call(
        body,
        name=name,
        grid_spec=pltpu.PrefetchScalarGridSpec(
            num_scalar_prefetch=1,
            grid=(nr,),
            in_specs=[pl.BlockSpec((tr, C), lambda r, s: (r, 0))],
            out_specs=pl.BlockSpec((tr, C), omap),
        ),
        out_shape=jax.ShapeDtypeStruct(full, bf16),
        compiler_params=pltpu.CompilerParams(dimension_semantics=("parallel",)),
    )(where, shard)


class _Rider(NamedTuple):
    ins: list
    out_shapes: list
    aliases: dict
    sems: list
    first: Callable
    last: Callable


def _run_rider(rider, *, name):
    ni, no = len(rider.ins), len(rider.out_shapes)

    def body(*refs):
        ins, outs, sems = refs[:ni], refs[ni:ni + no], refs[ni + no:]
        rider.first(ins, outs, sems)
        rider.last(ins, outs, sems)

    return pl.pallas_call(
        body,
        name=name,
        in_specs=[_hbm()] * ni,
        out_specs=[_hbm()] * no,
        out_shape=rider.out_shapes,
        input_output_aliases=rider.aliases,
        scratch_shapes=rider.sems,
    )(*rider.ins)


def _gather_rider(fulls, shard_shapes, axes, splits, rels):
    n = len(fulls)

    def stage1(ins, outs, sems):
        x, y, c = _place()
        jo = 2 * x + y
        cps = []
        for m in range(n):
            half = c if splits[m] else None
            for k in rels[m]:
                px, py = _flip(x, y, k)
                cps.append(_remote(_region(ins[m], shard_shapes[m], axes[m], jo, half), _region(outs[m], shard_shapes[m], axes[m], jo, half),
                                   sems[0].at[3 * m + k - 1], sems[1].at[3 * m + k - 1], (px, py, c)))
        return cps

    def first(ins, outs, sems):
        for cp in stage1(ins, outs, sems):
            cp.start()

    def last(ins, outs, sems):
        x, y, c = _place()
        send1, recv1, send2, recv2 = sems
        forwards = []
        for m in range(n):
            half = c if splits[m] else None
            for k in rels[m]:
                px, py = _flip(x, y, k)
                land = _region(outs[m], shard_shapes[m], axes[m], 2 * px + py, half)
                _remote(land, land, send1.at[3 * m + k - 1], recv1.at[3 * m + k - 1], (x, y, c)).wait_recv()
                if splits[m]:
                    fw = _remote(land, land, send2.at[3 * m + k - 1], recv2.at[3 * m + k - 1], (x, y, 1 - c))
                    fw.start()
                    forwards.append(fw)
        for m in range(n):
            if not splits[m]:
                continue
            for k in rels[m]:
                px, py = _flip(x, y, k)
                land = _region(outs[m], shard_shapes[m], axes[m], 2 * px + py, 1 - c)
                _remote(land, land, send2.at[3 * m + k - 1], recv2.at[3 * m + k - 1], (x, y, c)).wait_recv()
        for cp in stage1(ins, outs, sems) + forwards:
            cp.wait_send()

    sems = pltpu.SemaphoreType.DMA((3 * n,))
    return _Rider(list(fulls), [jax.ShapeDtypeStruct(f.shape, f.dtype) for f in fulls], {m: m for m in range(n)},
                  [sems, sems, sems, sems], first, last)


def _chips_rider(ts):
    n = len(ts)

    def copies(ins, outs, sems):
        x, y, c = _place()
        cps = []
        for m in range(n):
            for k in (1, 2, 3):
                px, py = _flip(x, y, k)
                cps.append(_remote(ins[m].at[2 * px + py], outs[m].at[k - 1], sems[0].at[3 * m + k - 1], sems[1].at[3 * m + k - 1], (px, py, c)))
        return cps

    def first(ins, outs, sems):
        for cp in copies(ins, outs, sems):
            cp.start()

    def last(ins, outs, sems):
        cps = copies(ins, outs, sems)
        for cp in cps:
            cp.wait_recv()
        for cp in cps:
            cp.wait_send()

    sems = pltpu.SemaphoreType.DMA((3 * n,))
    return _Rider(list(ts), [jax.ShapeDtypeStruct((3,) + t.shape[1:], t.dtype) for t in ts], {}, [sems, sems], first, last)


def _rs_sibling(dws, shard_shapes, axes, *, name):
    n = len(dws)

    def body(*refs):
        ins, got = refs[:n], refs[n:2 * n]
        send, recv = refs[2 * n:]
        x, y, c = _place()
        cps = []
        for m in range(n):
            for j in range(N_CHIPS):
                cp = _remote(_region(ins[m], shard_shapes[m], axes[m], j, 1 - c), got[m].at[j],
                             send.at[N_CHIPS * m + j], recv.at[N_CHIPS * m + j], (x, y, 1 - c))
                cp.start()
                cps.append(cp)
        for cp in cps:
            cp.wait_recv()
        for cp in cps:
            cp.wait_send()

    sems = pltpu.SemaphoreType.DMA((N_CHIPS * n,))
    return pl.pallas_call(
        body,
        name=name,
        in_specs=[_hbm()] * n,
        out_specs=[_hbm()] * n,
        out_shape=[jax.ShapeDtypeStruct((N_CHIPS, s[0] // 2, s[1]), d.dtype) for s, d in zip(shard_shapes, dws)],
        scratch_shapes=[sems, sems],
    )(*dws)


def _pair_sum(dw, got, shard_shape, axis, where, *, name):
    R, C = shard_shape
    hr = R // 2
    tr = _div(hr, 256, 2 * SUBLANES)
    nr = hr // tr
    dmap = (lambda j, r, s: ((2 * j + s[1]) * nr + r, 0)) if axis == 0 else (lambda j, r, s: (s[1] * nr + r, j))
    slot = pl.BlockSpec((1, tr, C), lambda j, r, s: (j, r, 0))

    def body(s_ref, d_ref, g_ref, o_ref):
        o_ref[0] = (d_ref[...].astype(f32) + g_ref[0].astype(f32)).astype(bf16)

    return pl.pallas_call(
        body,
        name=name,
        grid_spec=pltpu.PrefetchScalarGridSpec(
            num_scalar_prefetch=1,
            grid=(N_CHIPS, nr),
            in_specs=[pl.BlockSpec((tr, C), dmap), slot],
            out_specs=slot,
        ),
        out_shape=jax.ShapeDtypeStruct((N_CHIPS, hr, C), bf16),
        compiler_params=pltpu.CompilerParams(dimension_semantics=("parallel", "parallel")),
    )(where, dw, got)


def _chip_sum(pair, others, where, *, name):
    _, hr, C = pair.shape
    tr = _div(hr, 256, 2 * SUBLANES)
    nr = hr // tr

    def body(s_ref, p_ref, o_ref, g_ref):
        g_ref[...] = p_ref[0].astype(f32) + o_ref[0].astype(f32) + o_ref[1].astype(f32) + o_ref[2].astype(f32)

    return pl.pallas_call(
        body,
        name=name,
        grid_spec=pltpu.PrefetchScalarGridSpec(
            num_scalar_prefetch=1,
            grid=(nr,),
            in_specs=[pl.BlockSpec((1, tr, C), lambda r, s: (s[0], r, 0)), pl.BlockSpec((3, tr, C), lambda r, s: (0, r, 0))],
            out_specs=pl.BlockSpec((tr, C), lambda r, s: (s[1] * nr + r, 0)),
        ),
        out_shape=jax.ShapeDtypeStruct((2 * hr, C), f32),
        compiler_params=pltpu.CompilerParams(dimension_semantics=("parallel",)),
    )(where, pair, others)


def _rs_join(gs):
    n = len(gs)

    def body(*refs):
        ins, outs = refs[:n], refs[n:2 * n]
        send, recv = refs[2 * n:]
        x, y, c = _place()
        cps = []
        for m in range(n):
            hr = gs[m].shape[0] // 2
            cp = _remote(ins[m].at[pl.ds(c * hr, hr), :], outs[m].at[pl.ds(c * hr, hr), :], send.at[m], recv.at[m], (x, y, 1 - c))
            cp.start()
            cps.append(cp)
        for cp in cps:
            cp.wait_recv()
        for cp in cps:
            cp.wait_send()

    sems = pltpu.SemaphoreType.DMA((n,))
    return pl.pallas_call(
        body,
        name="grads_join_halves",
        in_specs=[_hbm()] * n,
        out_specs=[_hbm()] * n,
        out_shape=[jax.ShapeDtypeStruct(g.shape, g.dtype) for g in gs],
        input_output_aliases={m: m for m in range(n)},
        scratch_shapes=[sems, sems],
    )(*gs)


def _allreduce_small(v):
    rows = v.shape[0]

    def body(v_ref, o_ref, gath, send, recv):
        x, y, c = _place()
        me = 4 * x + 2 * y + c
        gath[pl.ds(me, 1)] = v_ref[...][None]
        cps = []
        for k in range(1, N_DEV):
            peer = (1 - x if k & 4 else x, 1 - y if k & 2 else y, 1 - c if k & 1 else c)
            cp = _remote(v_ref, gath.at[me], send.at[k - 1], recv.at[k - 1], peer)
            cp.start()
            cps.append(cp)
        for cp in cps:
            cp.wait_recv()
        for cp in cps:
            cp.wait_send()
        acc = gath[0]
        for d in range(1, N_DEV):
            acc = acc + gath[d]
        o_ref[...] = acc

    return pl.pallas_call(
        body,
        name="small_allreduce",
        in_specs=[_vmem()],
        out_specs=_vmem(),
        out_shape=jax.ShapeDtypeStruct(v.shape, f32),
        scratch_shapes=[pltpu.VMEM((N_DEV, rows, LANES), f32), pltpu.SemaphoreType.DMA((N_DEV - 1,)), pltpu.SemaphoreType.DMA((N_DEV - 1,))],
    )(v)


def _adamw_math(w, g, m, v):
    m = ADAM_B1 * m + (1.0 - ADAM_B1) * g
    v = ADAM_B2 * v + (1.0 - ADAM_B2) * (g * g)
    m_hat = m / (1.0 - ADAM_B1 ** ADAM_STEP)
    v_hat = v / (1.0 - ADAM_B2 ** ADAM_STEP)
    delta = -ADAM_LR * (m_hat / (jnp.sqrt(v_hat) + ADAM_EPS) + ADAM_WD * w)
    return delta, m, v


def _adamw(w, g, m, v, *, name):
    R, C = w.shape
    tr = _div(R, 128, SUBLANES)

    def body(w_ref, g_ref, m_ref, v_ref, d_ref, mo_ref, vo_ref):
        d, mn, vn = _adamw_math(w_ref[...], g_ref[...], m_ref[...], v_ref[...])
        d_ref[...] = d
        mo_ref[...] = mn
        vo_ref[...] = vn

    spec = pl.BlockSpec((tr, C), lambda r: (r, 0))
    return pl.pallas_call(
        body,
        name=name,
        grid=(R // tr,),
        in_specs=[spec] * 4,
        out_specs=[spec] * 3,
        out_shape=[jax.ShapeDtypeStruct((R, C), f32)] * 3,
        compiler_params=pltpu.CompilerParams(dimension_semantics=("parallel",)),
    )(w, g, m, v)


def _pack(arrs):
    flat = jnp.concatenate([a.reshape(-1).astype(f32) for a in arrs])
    tile = SUBLANES * LANES
    total = -(-flat.shape[0] // tile) * tile
    return jnp.pad(flat, (0, total - flat.shape[0])).reshape(total // LANES, LANES)


def _unpack(buf, shapes):
    flat = buf.reshape(-1)
    out, off = [], 0
    for s in shapes:
        size = int(np.prod(s))
        out.append(flat[off:off + size].reshape(s))
        off += size
    return out


def kernel(x, p, norm_mix, w_in, lb_logits, hg_norm, rel_bias, w_out, norm_ffn, w_up, conv_w, conv_b, w_down, norm_ple, w_ple_gate, w_ple_proj, final_norm, loss_target, m_norm_mix, m_w_in, m_lb_logits, m_hg_norm, m_rel_bias, m_w_out, m_norm_ffn, m_w_up, m_conv_w, m_conv_b, m_w_down, m_norm_ple, m_w_ple_gate, m_w_ple_proj, m_final_norm, v_norm_mix, v_w_in, v_lb_logits, v_hg_norm, v_rel_bias, v_w_out, v_norm_ffn, v_w_up, v_conv_w, v_conv_b, v_w_down, v_norm_ple, v_w_ple_gate, v_w_ple_proj, v_final_norm):
    S, D = x.shape[1], x.shape[2]
    xv, pv, tgt = x[0], p[0, 0], loss_target[0]
    W = (w_in.shape[2] * N_CHIPS) // 7
    H = W // HEAD_DIM
    F = w_down.shape[1] * N_CHIPS
    tc = _div(F // 2, 1408, LANES)
    nj = F // tc
    jx, jy = lax.axis_index("x"), lax.axis_index("y")
    chip = 2 * jx + jy

    big = [w_in[0], w_out[0], w_up[0], w_down[0], w_ple_gate[0], w_ple_proj[0]]
    big_axes = [1, 0, 1, 0, 0, 1]
    where = jnp.stack([chip, lax.axis_index("c")]).astype(jnp.int32)
    cw_pad = jnp.pad(conv_w[0], ((0, SUBLANES - conv_w.shape[1]), (0, 0)))
    cw_mine = lax.dynamic_update_slice(jnp.zeros((N_CHIPS * SUBLANES, cw_pad.shape[1]), f32), cw_pad, (chip * SUBLANES, 0))
    placed = [_cast_place(b, ax, where, name=f"cast_place_{i}") for i, (b, ax) in enumerate(zip(big, big_axes))]
    shard_shapes = [b.shape for b in big]

    every = (1, 2, 3)

    def gather(items):
        return _gather_rider([b for b, _, _ in items],
                             [cw_pad.shape if i is None else shard_shapes[i] for _, i, _ in items],
                             [0 if i is None else big_axes[i] for _, i, _ in items],
                             [i is not None for _, i, _ in items], [r for _, _, r in items])

    (W_in,) = _run_rider(gather([(placed[0], 0, every)]), name="gather_w_in")

    tm = _div(S, 1024, LANES)

    a1 = _rms_fwd(xv, norm_mix, name="rms_mix")
    proj, W_out, W_up_part = _mm(a1, W_in, dims="nn", tm=tm, tn=_div(7 * W, 512, LANES), tk=D, out_dtype=f32, name="mm_in",
                                 rider=gather([(placed[1], 1, every), (placed[2], 2, (1, 2))]))
    y_hg, o_hg, st, W_down = _hgrn_fwd(proj, lb_logits, hg_norm, W, rider=gather([(placed[3], 3, every)]))

    qt, pad, kb, ne = _att_dims(S)
    nrel = rel_bias.shape[2]
    nrel_pad = -(-nrel // LANES) * LANES
    onehot = _bias_onehot(qt, ne, nrel_pad)
    rb_pad = jnp.pad(rel_bias[0], ((0, 0), (0, nrel_pad - nrel)))[:, None, :]
    ext = _bias_ext(rb_pad, onehot)
    kp = jnp.pad(proj[:, 5 * W:6 * W].astype(bf16), ((pad, 0), (0, 0)))
    vp = jnp.pad(proj[:, 6 * W:7 * W].astype(bf16), ((pad, 0), (0, 0)))
    y_att, W_up = _attn_fwd(proj, kp, vp, ext, W, rider=gather([(W_up_part, 2, (3,))]))
    ycat = jnp.concatenate([y_hg, y_att], axis=1)

    tn_d = _div(D, 512, LANES)
    h1, W_pg, W_pp, cw_all = _mm(ycat, W_out, dims="nn", tm=tm, tn=tn_d, tk=D, out_dtype=f32, res=xv, name="mm_out",
                                 rider=gather([(placed[4], 4, every), (placed[5], 5, every), (cw_mine, None, every)]))
    a2 = _rms_fwd(h1, norm_ffn, name="rms_ffn")
    perm = lambda b: (b % 2) * nj + b // 2
    u = _mm(a2, W_up, dims="nn", tm=tm, tn=tc, tk=D, out_dtype=f32, name="mm_up", b_idx=lambda i, j, k: (k, perm(j)))
    ncw = conv_w.shape[1]
    cw_full = jnp.transpose(cw_all.reshape(N_CHIPS, SUBLANES, -1)[:, :ncw], (1, 0, 2)).reshape(ncw, -1)
    cb = conv_b
    mact = _convact_fwd(u, cw_full, cb, tc)
    h2 = _mm(mact, W_down, dims="nn", tm=tm, tn=tn_d, tk=_div(F, 1408, LANES), out_dtype=f32, res=h1, name="mm_down")
    a3 = _rms_fwd(h2, norm_ple, name="rms_ple")
    zg = _mm(a3, W_pg, dims="nn", tm=tm, tn=tn_d, tk=D, out_dtype=f32, name="mm_ple_gate")
    p16 = pv.astype(bf16)
    pp = _mm(p16, W_pp, dims="nn", tm=tm, tn=tn_d, tk=pv.shape[1], out_dtype=f32, name="mm_ple_proj")
    dh3, dzg, dpp, loss_part, d_fn = _tail(h2, zg, pp, tgt, final_norm.reshape(1, D))

    pair, others = {}, {}

    def reduce_start(idx, grads):
        got = _rs_sibling(grads, [shard_shapes[i] for i in idx], [big_axes[i] for i in idx], name="grads_to_sibling_" + "_".join(map(str, idx)))
        for i, d, g in zip(idx, grads, got):
            pair[i] = _pair_sum(d, g, shard_shapes[i], big_axes[i], where, name=f"grad_pair_sum_{i}")
        return _chips_rider([pair[i] for i in idx])

    def reduce_landed(idx, landed):
        for i, o in zip(idx, landed):
            others[i] = o

    tk_s = _div(S, 2048, LANES)
    dW_pp = _mm(p16, dpp, dims="tn", tm=pv.shape[1], tn=tn_d, tk=tk_s, out_dtype=bf16, name="mm_d_ple_proj")
    dW_pg = _mm(a3, dzg, dims="tn", tm=tn_d, tn=tn_d, tk=tk_s, out_dtype=bf16, name="mm_d_ple_gate")
    da3 = _mm(dzg, W_pg, dims="nt", tm=tm, tn=_div(D, 1024, LANES), tk=D, out_dtype=f32, name="mm_da3")
    dh2, dh2b, d_nple = _rms_bwd(h2, norm_ple, da3, dh3, name="rms_ple_bwd")
    dm, *landed = _mm(dh2b, W_down, dims="nt", tm=tm, tn=_div(F, 512, LANES), tk=D, out_dtype=f32, name="mm_dm",
                      rider=reduce_start([4, 5], [dW_pg, dW_pp]))
    reduce_landed([4, 5], landed)
    dW_down = _mm(mact, dh2b, dims="tn", tm=_div(F, 512, LANES), tn=tn_d, tk=tk_s, out_dtype=bf16, name="mm_d_down")
    duc, dcw_g, dcw_v, dcb_g, dcb_v, *landed = _convact_bwd_a(u, dm, cw_full, cb, tc, rider=reduce_start([3], [dW_down]))
    reduce_landed([3], landed)
    du = _convact_bwd_b(duc, cw_full, tc)
    dW_up = _mm(a2, du, dims="tn", tm=_div(D, 1024, LANES), tn=tc, tk=tk_s, out_dtype=bf16, name="mm_d_up",
                o_idx=lambda i, j, k: (i, perm(j)))
    da2, *landed = _mm(du, W_up, dims="nt", tm=tm, tn=_div(D, 1024, LANES), tk=tc, out_dtype=f32, name="mm_da2",
                       b_idx=lambda i, j, k: (j, perm(k)), rider=reduce_start([2], [dW_up]))
    reduce_landed([2], landed)
    dh1, dh1b, d_nffn = _rms_bwd(h1, norm_ffn, da2, dh2, name="rms_ffn_bwd")
    dycat = _mm(dh1b, W_out, dims="nt", tm=tm, tn=_div(D, 1024, LANES), tk=D, out_dtype=f32, name="mm_dycat")
    dW_out = _mm(ycat, dh1b, dims="tn", tm=tn_d, tn=tn_d, tk=tk_s, out_dtype=bf16, name="mm_d_out")
    dq_att, dk_pad, dv_pad, dbias, *landed = _attn_bwd(proj, kp, vp, ext, dycat, W, rider=reduce_start([1], [dW_out]))
    reduce_landed([1], landed)
    dhq, dhf, dhi, dhg, d_lbl, d_hgn = _hgrn_bwd(proj, o_hg, st, dycat, lb_logits, hg_norm, W)
    d_rb = _bias_bwd(dbias, onehot)[:, 0, :nrel]
    dproj = jnp.concatenate([dhq, dhf, dhi, dhg, dq_att, dk_pad[pad:].astype(bf16), dv_pad[pad:].astype(bf16)], axis=1)
    dW_in = _mm(a1, dproj, dims="tn", tm=_div(D, 1024, LANES), tn=_div(7 * W, 512, LANES), tk=tk_s, out_dtype=bf16, name="mm_d_in")
    da1, *landed = _mm(dproj, W_in, dims="nt", tm=tm, tn=_div(D, 1024, LANES), tk=_div(7 * W, 1792, LANES), out_dtype=f32, name="mm_da1",
                       rider=reduce_start([0], [dW_in]))
    reduce_landed([0], landed)
    grad_x, _, d_nmix = _rms_bwd(xv, norm_mix, da1, dh1, name="rms_mix_bwd")

    d_cw = jnp.concatenate([dcw_g, dcw_v], axis=1)
    d_cb = jnp.concatenate([dcb_g, dcb_v], axis=1)
    small_parts = [loss_part[:, :1], d_nmix, d_lbl, d_hgn, d_rb, d_nffn, d_cw, d_cb, d_nple, d_fn]
    small_shapes = [(1, 1), (1, D), lb_logits.shape, hg_norm.shape, (H, nrel), (1, D), (ncw, 2 * F), (1, 2 * F), (1, D), (1, D)]
    red = _unpack(_allreduce_small(_pack(small_parts)), small_shapes)
    loss = red[0].reshape(())
    g_nmix, g_lbl, g_hgn, g_rb, g_nffn, g_cw_all, g_cb, g_nple, g_fn = red[1:]
    csh = conv_w.shape[2]
    g_cw = lax.dynamic_slice(g_cw_all, (0, chip * csh), (ncw, csh))
    small_g = [g_nmix, g_lbl, g_hgn, g_rb[None], g_nffn, g_cw[None], g_cb, g_nple, g_fn.reshape(D)]
    small_w = [norm_mix, lb_logits, hg_norm, rel_bias, norm_ffn, conv_w, conv_b, norm_ple, final_norm]
    small_m = [m_norm_mix, m_lb_logits, m_hg_norm, m_rel_bias, m_norm_ffn, m_conv_w, m_conv_b, m_norm_ple, m_final_norm]
    small_v = [v_norm_mix, v_lb_logits, v_hg_norm, v_rel_bias, v_norm_ffn, v_conv_w, v_conv_b, v_norm_ple, v_final_norm]
    shapes_s = [w.shape for w in small_w]
    sd, sm, sv = _adamw(_pack(small_w), _pack(small_g), _pack(small_m), _pack(small_v), name="adamw_small")
    small_g = [g.reshape(s) for g, s in zip(small_g, shapes_s)]
    small_d, small_nm, small_nv = _unpack(sd, shapes_s), _unpack(sm, shapes_s), _unpack(sv, shapes_s)

    halves = [_chip_sum(pair[i], others[i], where, name=f"grad_chip_sum_{i}") for i in range(6)]
    g_big = _rs_join(halves)
    big_m = [m_w_in[0], m_w_out[0], m_w_up[0], m_w_down[0], m_w_ple_gate[0], m_w_ple_proj[0]]
    big_v = [v_w_in[0], v_w_out[0], v_w_up[0], v_w_down[0], v_w_ple_gate[0], v_w_ple_proj[0]]
    big_d, big_nm, big_nv = [], [], []
    for i in range(6):
        d_, m_, v_ = _adamw(big[i], g_big[i], big_m[i], big_v[i], name=f"adamw_{i}")
        big_d.append(d_[None])
        big_nm.append(m_[None])
        big_nv.append(v_[None])
    g_big = [g[None] for g in g_big]

    def order(sm_list, bg_list):
        s, b = sm_list, bg_list
        return [s[0], b[0], s[1], s[2], s[3], b[1], s[4], b[2], s[5], s[6], b[3], s[7], b[4], b[5], s[8]]

    return (loss, grad_x[None], *order(small_g, g_big), *order(small_d, big_d), *order(small_nm, big_nm), *order(small_nv, big_nv))
```

```python
import functools
from typing import Callable, NamedTuple

import jax
import jax.numpy as jnp
import numpy as np
from jax import lax
from jax.experimental import pallas as pl
from jax.experimental.pallas import tpu as pltpu

f32 = jnp.float32
bf16 = jnp.bfloat16

CHUNK = 64
HEAD_DIM = 128
LEFT_CHUNKS = 8
REL_CLIP = 128
EPS = 1e-6
HG_SUB = 16
HG_HEADS_PER_STEP = 8
ATT_Q_ROWS = 256
ADAM_LR, ADAM_B1, ADAM_B2, ADAM_EPS, ADAM_WD, ADAM_STEP = 0.001, 0.9, 0.999, 1e-08, 0.01, 10
LANES = 128
SUBLANES = 8
N_CHIPS = 4
N_DEV = 8
MESH = pl.DeviceIdType.MESH
NEG = float(np.finfo(np.float32).min)

NN = (((1,), (0,)), ((), ()))
NT = (((1,), (1,)), ((), ()))
TN = (((0,), (0,)), ((), ()))
HIGHEST = lax.Precision.HIGHEST


def _dot(a, b, dims, precision=None):
    return lax.dot_general(a, b, dims, preferred_element_type=f32, precision=precision)


def _sigmoid(v):
    return 1.0 / (1.0 + jnp.exp(-v))


def _div(n, pref, mult):
    best = None
    d = mult
    while d <= min(n, pref):
        if n % d == 0:
            best = d
        d += mult
    return best if best is not None else n


def _hbm():
    return pl.BlockSpec(memory_space=pltpu.HBM)


def _vmem():
    return pl.BlockSpec(memory_space=pltpu.VMEM)


def _mm(a, b, *, dims, tm, tn, tk, out_dtype, name, res=None, a_idx=None, b_idx=None, o_idx=None, out_shape=None, rider=None,
        cols_outer=False):
    if dims == "nn":
        (M, K), (_, N) = a.shape, b.shape
        a_blk, b_blk, dn = (tm, tk), (tk, tn), NN
        a_def, b_def = (lambda i, j, k: (i, k)), (lambda i, j, k: (k, j))
    elif dims == "nt":
        (M, K), (N, _) = a.shape, b.shape
        a_blk, b_blk, dn = (tm, tk), (tn, tk), NT
        a_def, b_def = (lambda i, j, k: (i, k)), (lambda i, j, k: (j, k))
    else:
        (K, M), (_, N) = a.shape, b.shape
        a_blk, b_blk, dn = (tk, tm), (tk, tn), TN
        a_def, b_def = (lambda i, j, k: (k, i)), (lambda i, j, k: (k, j))
    assert M % tm == 0 and N % tn == 0 and K % tk == 0, (name, M, N, K, tm, tn, tk)
    nk = K // tk
    has_res = res is not None

    def body(*refs):
        a_ref, b_ref = refs[0], refs[1]
        res_ref = refs[2] if has_res else None
        o_ref = refs[3] if has_res else refs[2]
        part = _dot(a_ref[...].astype(bf16), b_ref[...].astype(bf16), dn)

        def finish(acc):
            if has_res:
                acc = acc + res_ref[...]
            o_ref[...] = acc.astype(out_dtype)

        if nk == 1:
            finish(part)
        else:
            acc_ref = refs[-1]
            k = pl.program_id(2)

            @pl.when(k == 0)
            def _():
                acc_ref[...] = part

            @pl.when(k > 0)
            def _():
                acc_ref[...] += part

            @pl.when(k == nk - 1)
            def _():
                finish(acc_ref[...])

    def ordered(f):
        return (lambda j, i, k: f(i, j, k)) if cols_outer else f

    o_map = ordered(o_idx or (lambda i, j, k: (i, j)))
    in_specs = [pl.BlockSpec(a_blk, ordered(a_idx or a_def)), pl.BlockSpec(b_blk, ordered(b_idx or b_def))]
    args = [a, b]
    if has_res:
        in_specs.append(pl.BlockSpec((tm, tn), o_map))
        args.append(res)
    outs = _grid_call(
        body,
        rider,
        name=name,
        grid=(N // tn, M // tm, nk) if cols_outer else (M // tm, N // tn, nk),
        in_specs=in_specs,
        out_specs=[pl.BlockSpec((tm, tn), o_map)],
        out_shape=[jax.ShapeDtypeStruct(out_shape or (M, N), out_dtype)],
        scratch_shapes=[pltpu.VMEM((tm, tn), f32)] if nk > 1 else [],
        args=args,
    )
    return outs[0] if rider is None else outs


def _rms_fwd(xv, g, *, name):
    S, D = xv.shape
    ts = _div(S, 512, SUBLANES)

    def body(x_ref, g_ref, o_ref):
        v = x_ref[...]
        r = lax.rsqrt(jnp.mean(v * v, axis=-1, keepdims=True) + EPS)
        o_ref[...] = (v * r * g_ref[...]).astype(bf16)

    return pl.pallas_call(
        body,
        name=name,
        grid=(S // ts,),
        in_specs=[pl.BlockSpec((ts, D), lambda i: (i, 0)), pl.BlockSpec((1, D), lambda i: (0, 0))],
        out_specs=pl.BlockSpec((ts, D), lambda i: (i, 0)),
        out_shape=jax.ShapeDtypeStruct((S, D), bf16),
        compiler_params=pltpu.CompilerParams(dimension_semantics=("parallel",)),
    )(xv, g)


def _rms_bwd(xv, g, dy, dres, *, name):
    S, D = xv.shape
    ts = _div(S, 256, SUBLANES)

    def body(x_ref, g_ref, dy_ref, dres_ref, dx_ref, dxb_ref, dg_ref):
        i = pl.program_id(0)
        v = x_ref[...]
        r = lax.rsqrt(jnp.mean(v * v, axis=-1, keepdims=True) + EPS)
        vn = v * r
        d = dy_ref[...]
        part = jnp.sum(d * vn, axis=0, keepdims=True)

        @pl.when(i == 0)
        def _():
            dg_ref[...] = part

        @pl.when(i > 0)
        def _():
            dg_ref[...] += part

        t = d * g_ref[...]
        dx = dres_ref[...] + r * (t - vn * jnp.mean(t * vn, axis=-1, keepdims=True))
        dx_ref[...] = dx
        dxb_ref[...] = dx.astype(bf16)

    row = pl.BlockSpec((ts, D), lambda i: (i, 0))
    vec = pl.BlockSpec((1, D), lambda i: (0, 0))
    return pl.pallas_call(
        body,
        name=name,
        grid=(S // ts,),
        in_specs=[row, vec, row, row],
        out_specs=[row, row, vec],
        out_shape=[jax.ShapeDtypeStruct((S, D), f32), jax.ShapeDtypeStruct((S, D), bf16), jax.ShapeDtypeStruct((1, D), f32)],
        compiler_params=pltpu.CompilerParams(dimension_semantics=("arbitrary",)),
    )(xv, g, dy, dres)


def _tail(h2, zg, pp, tgt, fn):
    S, D = h2.shape
    ts = _div(S, 256, SUBLANES)

    def body(h_ref, z_ref, p_ref, t_ref, fn_ref, dh_ref, dz_ref, dp_ref, loss_ref, dfn_ref):
        i = pl.program_id(0)
        sg = _sigmoid(z_ref[...])
        ppv = p_ref[...]
        h3 = h_ref[...] + sg * ppv
        r = lax.rsqrt(jnp.mean(h3 * h3, axis=-1, keepdims=True) + EPS)
        hn = h3 * r
        fnv = fn_ref[...]
        e = hn * fnv - t_ref[...]
        lpart = 0.5 * jnp.sum(jnp.mean(e * e, axis=-1, keepdims=True), axis=0, keepdims=True)
        dy = e * (1.0 / D)
        dpart = jnp.sum(dy * hn, axis=0, keepdims=True)

        @pl.when(i == 0)
        def _():
            loss_ref[...] = jnp.broadcast_to(lpart, loss_ref.shape)
            dfn_ref[...] = dpart

        @pl.when(i > 0)
        def _():
            loss_ref[...] += jnp.broadcast_to(lpart, loss_ref.shape)
            dfn_ref[...] += dpart

        t = dy * fnv
        dh3 = r * (t - hn * jnp.mean(t * hn, axis=-1, keepdims=True))
        dh_ref[...] = dh3
        dz_ref[...] = (dh3 * ppv * sg * (1.0 - sg)).astype(bf16)
        dp_ref[...] = (dh3 * sg).astype(bf16)

    row = pl.BlockSpec((ts, D), lambda i: (i, 0))
    vec = pl.BlockSpec((1, D), lambda i: (0, 0))
    one = pl.BlockSpec((1, LANES), lambda i: (0, 0))
    return pl.pallas_call(
        body,
        name="ple_final_loss",
        grid=(S // ts,),
        in_specs=[row, row, row, row, vec],
        out_specs=[row, row, row, one, vec],
        out_shape=[
            jax.ShapeDtypeStruct((S, D), f32),
            jax.ShapeDtypeStruct((S, D), bf16),
            jax.ShapeDtypeStruct((S, D), bf16),
            jax.ShapeDtypeStruct((1, LANES), f32),
            jax.ShapeDtypeStruct((1, D), f32),
        ],
        compiler_params=pltpu.CompilerParams(dimension_semantics=("arbitrary",)),
    )(h2, zg, pp, tgt, fn)


def _shift_down(v, halo, k):
    r = pltpu.roll(v, k, 0)
    hr = pltpu.roll(halo, k, 0)
    row = lax.broadcasted_iota(jnp.int32, hr.shape, 0)
    top = jnp.where(row < k, hr, r[0:SUBLANES])
    return jnp.concatenate([top, r[SUBLANES:]], axis=0)


def _shift_up(v, halo, k):
    n = v.shape[0]
    r = pltpu.roll(v, n - k, 0)
    hr = pltpu.roll(halo, SUBLANES - k, 0)
    row = lax.broadcasted_iota(jnp.int32, hr.shape, 0)
    bot = jnp.where(row >= SUBLANES - k, hr, r[n - SUBLANES:])
    return jnp.concatenate([r[: n - SUBLANES], bot], axis=0)


def _conv_specs(S, F, tc, ts):
    nj = F // tc
    rows8 = ts // SUBLANES
    main = pl.BlockSpec((ts, 2 * tc), lambda j, i: (i, j))
    prev = pl.BlockSpec((SUBLANES, 2 * tc), lambda j, i: (jnp.maximum(i * rows8 - 1, 0), j))
    nxt = pl.BlockSpec((SUBLANES, 2 * tc), lambda j, i: (jnp.minimum((i + 1) * rows8, S // SUBLANES - 1), j))
    wg = pl.BlockSpec((3, tc), lambda j, i: (0, j))
    wv = pl.BlockSpec((3, tc), lambda j, i: (0, nj + j))
    bg = pl.BlockSpec((1, tc), lambda j, i: (0, j))
    bv = pl.BlockSpec((1, tc), lambda j, i: (0, nj + j))
    half = pl.BlockSpec((ts, tc), lambda j, i: (i, j))
    return nj, main, prev, nxt, wg, wv, bg, bv, half


def _conv_pre(u_ref, h_ref, wg_ref, wv_ref, bg_ref, bv_ref, tc):
    i = pl.program_id(1)
    u = u_ref[...]
    halo = jnp.where(i == 0, 0.0, h_ref[...])
    u1 = _shift_down(u, halo, 1)
    u2 = _shift_down(u, halo, 2)
    w = jnp.concatenate([wg_ref[...], wv_ref[...]], axis=1)
    b = jnp.concatenate([bg_ref[...], bv_ref[...]], axis=1)
    uc = b + w[0:1] * u2 + w[1:2] * u1 + w[2:3] * u
    return u, u1, u2, uc[:, :tc], uc[:, tc:]


def _convact_fwd(u, cw, cb, tc):
    S, F2 = u.shape
    F = F2 // 2
    ts = _div(S, 512, SUBLANES)
    nj, main, prev, nxt, wg, wv, bg, bv, half = _conv_specs(S, F, tc, ts)

    def body(u_ref, h_ref, wg_ref, wv_ref, bg_ref, bv_ref, m_ref):
        _, _, _, g, v = _conv_pre(u_ref, h_ref, wg_ref, wv_ref, bg_ref, bv_ref, tc)
        m_ref[...] = (g * _sigmoid(g) * v).astype(bf16)

    return pl.pallas_call(
        body,
        name="convact_fwd",
        grid=(nj, S // ts),
        in_specs=[main, prev, wg, wv, bg, bv],
        out_specs=half,
        out_shape=jax.ShapeDtypeStruct((S, F), bf16),
        compiler_params=pltpu.CompilerParams(dimension_semantics=("parallel", "parallel")),
    )(u, u, cw, cw, cb, cb)


def _convact_bwd_a(u, dm, cw, cb, tc, rider=None):
    S, F2 = u.shape
    F = F2 // 2
    ts = _div(S, 512, SUBLANES)
    nj, main, prev, nxt, wg, wv, bg, bv, half = _conv_specs(S, F, tc, ts)

    def body(u_ref, h_ref, dm_ref, wg_ref, wv_ref, bg_ref, bv_ref, duc_ref, dwg_ref, dwv_ref, dbg_ref, dbv_ref):
        i = pl.program_id(1)
        u0, u1, u2, g, v = _conv_pre(u_ref, h_ref, wg_ref, wv_ref, bg_ref, bv_ref, tc)
        d = dm_ref[...]
        sg = _sigmoid(g)
        dg = d * v * sg * (1.0 + g * (1.0 - sg))
        dv = d * g * sg
        duc = jnp.concatenate([dg, dv], axis=1)
        duc_ref[...] = duc
        db = jnp.sum(duc, axis=0, keepdims=True)
        dw = jnp.concatenate(
            [jnp.sum(duc * u2, axis=0, keepdims=True), jnp.sum(duc * u1, axis=0, keepdims=True), jnp.sum(duc * u0, axis=0, keepdims=True)],
            axis=0,
        )

        @pl.when(i == 0)
        def _():
            dwg_ref[...] = dw[:, :tc]
            dwv_ref[...] = dw[:, tc:]
            dbg_ref[...] = db[:, :tc]
            dbv_ref[...] = db[:, tc:]

        @pl.when(i > 0)
        def _():
            dwg_ref[...] += dw[:, :tc]
            dwv_ref[...] += dw[:, tc:]
            dbg_ref[...] += db[:, :tc]
            dbv_ref[...] += db[:, tc:]

    w_out = pl.BlockSpec((3, tc), lambda j, i: (0, j))
    b_out = pl.BlockSpec((1, tc), lambda j, i: (0, j))
    return _grid_call(
        body,
        rider,
        name="convact_bwd_a",
        grid=(nj, S // ts),
        in_specs=[main, prev, half, wg, wv, bg, bv],
        out_specs=[main, w_out, w_out, b_out, b_out],
        out_shape=[
            jax.ShapeDtypeStruct((S, F2), f32),
            jax.ShapeDtypeStruct((3, F), f32),
            jax.ShapeDtypeStruct((3, F), f32),
            jax.ShapeDtypeStruct((1, F), f32),
            jax.ShapeDtypeStruct((1, F), f32),
        ],
        scratch_shapes=[],
        args=(u, u, dm, cw, cw, cb, cb),
    )


def _convact_bwd_b(duc, cw, tc):
    S, F2 = duc.shape
    F = F2 // 2
    ts = _div(S, 512, SUBLANES)
    nj, main, prev, nxt, wg, wv, bg, bv, half = _conv_specs(S, F, tc, ts)
    ni = S // ts

    def body(d_ref, h_ref, wg_ref, wv_ref, du_ref):
        i = pl.program_id(1)
        d = d_ref[...]
        halo = jnp.where(i == ni - 1, 0.0, h_ref[...])
        d1 = _shift_up(d, halo, 1)
        d2 = _shift_up(d, halo, 2)
        w = jnp.concatenate([wg_ref[...], wv_ref[...]], axis=1)
        du_ref[...] = (w[2:3] * d + w[1:2] * d1 + w[0:1] * d2).astype(bf16)

    return pl.pallas_call(
        body,
        name="convact_bwd_b",
        grid=(nj, ni),
        in_specs=[main, nxt, wg, wv],
        out_specs=main,
        out_shape=jax.ShapeDtypeStruct((S, F2), bf16),
        compiler_params=pltpu.CompilerParams(dimension_semantics=("parallel", "parallel")),
    )(duc, duc, cw, cw)


def _hg_gates(qp, fp, lbl):
    lb = _sigmoid(lbl[0:1] - lbl[1:2])
    sig = _sigmoid(fp)
    sigm = _sigmoid(-fp)
    f = lb + (1.0 - lb) * sig
    k = (1.0 - lb) * sigm
    sq = _sigmoid(qp)
    qf = qp * sq
    row = lax.broadcasted_iota(jnp.int32, (CHUNK, CHUNK), 0)
    col = lax.broadcasted_iota(jnp.int32, (CHUNK, CHUNK), 1)
    b = _dot((row >= col).astype(f32), jnp.log(f), NN, HIGHEST)
    return lb, sig, sigm, f, k, sq, qf, b


def _hg_block(qf, k, b, I):
    r0, n = I * HG_SUB, (I + 1) * HG_SUB
    base = b[r0 - 1:r0] if I > 0 else jnp.zeros_like(b[0:1])
    eq = jnp.exp(b[r0:n] - base)
    ek = jnp.exp(base - b[0:n])
    qt16 = (qf[r0:n] * eq).astype(bf16)
    kt16 = (k[0:n] * ek).astype(bf16)
    row = lax.broadcasted_iota(jnp.int32, (HG_SUB, n), 0) + r0
    col = lax.broadcasted_iota(jnp.int32, (HG_SUB, n), 1)
    return r0, n, eq, ek, qt16, kt16, col <= row


def _per_head(v, hp, fn):
    return jnp.concatenate([fn(v[:, p * HEAD_DIM:(p + 1) * HEAD_DIM]) for p in range(hp)], axis=1)


def _head_mean(v, hp):
    return _per_head(v, hp, lambda t: jnp.broadcast_to(jnp.mean(t, axis=-1, keepdims=True), t.shape))


def _pad_rows(v, rows):
    return v if v.shape[0] == rows else jnp.concatenate([v, jnp.zeros((rows - v.shape[0], v.shape[1]), v.dtype)], axis=0)


def _hgrn_specs(S, W, hp, reverse):
    nc = S // CHUNK
    ngrp = W // (hp * HEAD_DIM)
    cw = hp * HEAD_DIM
    cidx = (lambda c: nc - 1 - c) if reverse else (lambda c: c)

    def proj(off):
        return pl.BlockSpec((CHUNK, cw), lambda h, c: (cidx(c), off * ngrp + h))

    act = pl.BlockSpec((CHUNK, cw), lambda h, c: (cidx(c), h))
    lbl = pl.BlockSpec((2, cw), lambda h, c: (0, h))
    gn = pl.BlockSpec((1, HEAD_DIM), lambda h, c: (0, 0))
    st = pl.BlockSpec((hp, 1, HEAD_DIM, HEAD_DIM), lambda h, c: (h, cidx(c), 0, 0))
    return nc, ngrp, proj, act, lbl, gn, st


def _grid_call(body, rider, *, name, grid, in_specs, out_specs, out_shape, scratch_shapes, args):
    params = pltpu.CompilerParams(dimension_semantics=("arbitrary",) * len(grid))
    if rider is None:
        return pl.pallas_call(body, name=name, grid=grid, in_specs=in_specs, out_specs=out_specs, out_shape=out_shape,
                              scratch_shapes=scratch_shapes, compiler_params=params)(*args)
    n_in, n_out, n_scr = len(in_specs), len(out_specs), len(scratch_shapes)
    ri, ro = len(rider.ins), len(rider.out_shapes)

    def wrapped(*refs):
        a, b = n_in, n_in + ri
        c, d = b + n_out, b + n_out + ro
        e = d + n_scr
        ids = [pl.program_id(t) for t in range(len(grid))]
        is_first = functools.reduce(jnp.logical_and, [i == 0 for i in ids])
        is_last = functools.reduce(jnp.logical_and, [i == g - 1 for i, g in zip(ids, grid)])

        @pl.when(is_first)
        def _():
            rider.first(refs[a:b], refs[c:d], refs[e:])

        body(*refs[:a], *refs[b:c], *refs[d:e])

        @pl.when(is_last)
        def _():
            rider.last(refs[a:b], refs[c:d], refs[e:])

    return pl.pallas_call(
        wrapped,
        name=name,
        grid=grid,
        in_specs=list(in_specs) + [_hbm()] * ri,
        out_specs=list(out_specs) + [_hbm()] * ro,
        out_shape=list(out_shape) + list(rider.out_shapes),
        input_output_aliases={n_in + i: n_out + o for i, o in rider.aliases.items()},
        scratch_shapes=list(scratch_shapes) + list(rider.sems),
        compiler_params=params,
    )(*args, *rider.ins)


def _hgrn_fwd(proj, lb_logits, hg_norm, W, rider=None):
    S = proj.shape[0]
    hp = min(HG_HEADS_PER_STEP, W // HEAD_DIM)
    nc, ngrp, pspec, act, lbl_spec, gn_spec, st_spec = _hgrn_specs(S, W, hp, False)
    nb = CHUNK // HG_SUB

    def body(q_ref, f_ref, i_ref, g_ref, lbl_ref, gn_ref, y_ref, o_ref, st_ref, state):
        c = pl.program_id(1)

        @pl.when(c == 0)
        def _():
            state[...] = jnp.zeros_like(state)

        _, _, _, _, k, _, qf, b = _hg_gates(q_ref[...], f_ref[...], lbl_ref[...])
        v16 = i_ref[...].astype(bf16)
        a16 = (qf * jnp.exp(b)).astype(bf16)
        bl = b[CHUNK - 1:CHUNK]
        kd16 = (k * jnp.exp(bl - b)).astype(bf16)
        ebl = jnp.exp(bl)
        blocks = [_hg_block(qf, k, b, I) for I in range(nb)]
        heads = [slice(p * HEAD_DIM, (p + 1) * HEAD_DIM) for p in range(hp)]
        st_prev = state[...]
        st_ref[:, 0] = st_prev
        inter = [_dot(a16[:, sl], st_prev[p].astype(bf16), NT) for p, sl in enumerate(heads)]
        scores = [[_dot(qt16[:, sl], kt16[:, sl], NT) for sl in heads] for _, _, _, _, qt16, kt16, _ in blocks]
        intra = [[_dot(jnp.where(blk[6], sc, 0.0).astype(bf16), v16[0:blk[1], sl], NN) for sc, sl in zip(scs, heads)]
                 for blk, scs in zip(blocks, scores)]
        st_new = [st_prev[p] * ebl[:, sl] + _dot(v16[:, sl], kd16[:, sl], TN) for p, sl in enumerate(heads)]
        state[...] = jnp.stack(st_new)
        o = jnp.concatenate([inter[p] + jnp.concatenate([rows[p] for rows in intra], axis=0) for p in range(hp)], axis=1)
        o_ref[...] = o
        r = lax.rsqrt(_head_mean(o * o, hp) + EPS)
        gp = g_ref[...]
        y_ref[...] = (o * r * jnp.tile(gn_ref[...], (1, hp)) * (gp * _sigmoid(gp))).astype(bf16)

    H = W // HEAD_DIM
    return _grid_call(
        body,
        rider,
        name="hgrn_fwd",
        grid=(ngrp, nc),
        in_specs=[pspec(0), pspec(1), pspec(2), pspec(3), lbl_spec, gn_spec],
        out_specs=[act, act, st_spec],
        out_shape=[
            jax.ShapeDtypeStruct((S, W), bf16),
            jax.ShapeDtypeStruct((S, W), f32),
            jax.ShapeDtypeStruct((H, nc, HEAD_DIM, HEAD_DIM), f32),
        ],
        scratch_shapes=[pltpu.VMEM((hp, HEAD_DIM, HEAD_DIM), f32)],
        args=(proj, proj, proj, proj, lb_logits, hg_norm),
    )


def _hgrn_bwd(proj, o_hg, st, dycat, lb_logits, hg_norm, W, rider=None):
    S = proj.shape[0]
    hp = min(HG_HEADS_PER_STEP, W // HEAD_DIM)
    nc, ngrp, pspec, act, lbl_spec, gn_spec, st_spec = _hgrn_specs(S, W, hp, True)
    nb = CHUNK // HG_SUB

    def body(q_ref, f_ref, i_ref, g_ref, o_ref, st_ref, dy_ref, lbl_ref, gn_ref,
             dq_ref, df_ref, di_ref, dg_ref, dlbl_ref, dgn_ref, dstate):
        h = pl.program_id(0)
        c = pl.program_id(1)

        @pl.when(c == 0)
        def _():
            dstate[...] = jnp.zeros_like(dstate)
            dlbl_ref[...] = jnp.zeros_like(dlbl_ref)

        @pl.when((c == 0) & (h == 0))
        def _():
            dgn_ref[...] = jnp.zeros_like(dgn_ref)

        row = lax.broadcasted_iota(jnp.int32, (CHUNK, CHUNK), 0)
        col = lax.broadcasted_iota(jnp.int32, (CHUNK, CHUNK), 1)
        upper = (col >= row).astype(f32)
        last_row = lax.broadcasted_iota(jnp.int32, (CHUNK, hp * HEAD_DIM), 0) == CHUNK - 1

        qp, fp = q_ref[...], f_ref[...]
        lb, sig, sigm, f, k, sq, qf, b = _hg_gates(qp, fp, lbl_ref[...])
        v16 = i_ref[...].astype(bf16)
        gnv = jnp.tile(gn_ref[...], (1, hp))
        gp = g_ref[...]
        sgg = _sigmoid(gp)
        gate = gp * sgg
        o = o_ref[...]
        r = lax.rsqrt(_head_mean(o * o, hp) + EPS)
        on = o * r
        dy = dy_ref[...]
        dg_ref[...] = (dy * on * gnv * (sgg * (1.0 + gp * (1.0 - sgg)))).astype(bf16)
        dgn_wide = jnp.sum(dy * on * gate, axis=0, keepdims=True)
        dgn_ref[...] += functools.reduce(jnp.add, [dgn_wide[:, p * HEAD_DIM:(p + 1) * HEAD_DIM] for p in range(hp)])
        don = dy * gnv * gate
        do16 = (r * (don - on * _head_mean(don * on, hp))).astype(bf16)
        eb = jnp.exp(b)
        A = qf * eb
        a16 = A.astype(bf16)
        bl = b[CHUNK - 1:CHUNK]
        ebl = jnp.exp(bl)
        ekd = jnp.exp(bl - b)
        kd = k * ekd
        kd16 = kd.astype(bf16)
        blocks = [_hg_block(qf, k, b, I) for I in range(nb)]

        heads = [slice(p * HEAD_DIM, (p + 1) * HEAD_DIM) for p in range(hp)]
        st_prev = st_ref[:, 0]
        dst_new = dstate[...]
        st16, dst16 = st_prev.astype(bf16), dst_new.astype(bf16)
        dA_h = [_dot(do16[:, sl], st16[p], NN) for p, sl in enumerate(heads)]
        dkd_h = [_dot(v16[:, sl], dst16[p], NN) for p, sl in enumerate(heads)]
        dv_h = [_dot(kd16[:, sl], dst16[p], NT) for p, sl in enumerate(heads)]
        dstate[...] = jnp.stack([dst_new[p] * ebl[:, sl] + _dot(do16[:, sl], a16[:, sl], TN) for p, sl in enumerate(heads)])
        dbl_h = [jnp.sum(dst_new[p] * st_prev[p], axis=0, keepdims=True) for p in range(hp)]
        sc_h = [[jnp.where(mask, _dot(qt16[:, sl], kt16[:, sl], NT), 0.0).astype(bf16) for sl in heads]
                for _, _, _, _, qt16, kt16, mask in blocks]
        dsc_h = [[jnp.where(mask, _dot(do16[r0:n, sl], v16[0:n, sl], NT), 0.0).astype(bf16) for sl in heads]
                 for r0, n, _, _, _, _, mask in blocks]
        dvi_h = [[_dot(sc, do16[blk[0]:blk[1], sl], TN) for sc, sl in zip(scs, heads)] for blk, scs in zip(blocks, sc_h)]
        dqt_h = [[_dot(dsc, blk[5][:, sl], NN) for dsc, sl in zip(dscs, heads)] for blk, dscs in zip(blocks, dsc_h)]
        dkt_h = [[_dot(dsc, blk[4][:, sl], TN) for dsc, sl in zip(dscs, heads)] for blk, dscs in zip(blocks, dsc_h)]
        dv_h = [functools.reduce(jnp.add, [dv_h[p]] + [_pad_rows(rows[p], CHUNK) for rows in dvi_h]) for p in range(hp)]

        dA, dkd = jnp.concatenate(dA_h, axis=1), jnp.concatenate(dkd_h, axis=1)
        dbl = jnp.concatenate(dbl_h, axis=1) * ebl + jnp.sum(dkd * kd, axis=0, keepdims=True)
        db = dA * A - dkd * kd + jnp.where(last_row, dbl, 0.0)
        dk = dkd * ekd
        dq_rows, db_rows = [], []
        for I, (r0, n, eq, ek, qt16, kt16, _) in enumerate(blocks):
            dqt, dkt = jnp.concatenate(dqt_h[I], axis=1), jnp.concatenate(dkt_h[I], axis=1)
            dq_rows.append(dqt * eq)
            dk = dk + _pad_rows(dkt * ek, CHUNK)
            db_rows.append(dqt * qt16.astype(f32))
            db = db - _pad_rows(dkt * kt16.astype(f32), CHUNK)
        dqf = dA * eb + jnp.concatenate(dq_rows, axis=0)
        db = db + jnp.concatenate(db_rows, axis=0)
        dlogf = _dot(upper, db, NN, HIGHEST)
        dfg = dlogf / f
        df_ref[...] = ((1.0 - lb) * sig * sigm * (dfg - dk)).astype(bf16)
        dlb = jnp.sum(sigm * (dfg - dk), axis=0, keepdims=True)
        dl0 = dlb * lb * (1.0 - lb)
        dlbl_ref[...] += jnp.concatenate([dl0, -dl0], axis=0)
        dq_ref[...] = (dqf * (sq * (1.0 + qp * (1.0 - sq)))).astype(bf16)
        di_ref[...] = jnp.concatenate(dv_h, axis=1).astype(bf16)

    cw = hp * HEAD_DIM
    dy_spec = pl.BlockSpec((CHUNK, cw), lambda h, c: (nc - 1 - c, h))
    return _grid_call(
        body,
        rider,
        name="hgrn_bwd",
        grid=(ngrp, nc),
        in_specs=[pspec(0), pspec(1), pspec(2), pspec(3), act, st_spec, dy_spec, lbl_spec, gn_spec],
        out_specs=[act, act, act, act, lbl_spec, gn_spec],
        out_shape=[jax.ShapeDtypeStruct((S, W), bf16)] * 4
        + [jax.ShapeDtypeStruct((2, W), f32), jax.ShapeDtypeStruct((1, HEAD_DIM), f32)],
        scratch_shapes=[pltpu.VMEM((hp, HEAD_DIM, HEAD_DIM), f32)],
        args=(proj, proj, proj, proj, o_hg, st, dycat, lb_logits, hg_norm),
    )


def _att_dims(S):
    qt = ATT_Q_ROWS if S % ATT_Q_ROWS == 0 else CHUNK
    pad = LEFT_CHUNKS * CHUNK
    kb = pad + qt
    return qt, pad, kb, kb + qt


def _att_scores(q16, kb16, bias, start, qt, pad, kb):
    s = _dot(q16, kb16, NT) * (HEAD_DIM ** -0.5) + bias
    row = lax.broadcasted_iota(jnp.int32, (qt, kb), 0)
    col = lax.broadcasted_iota(jnp.int32, (qt, kb), 1)
    lo = jnp.bitwise_and(row, -CHUNK)
    ok = (col >= lo) & (col < lo + pad + CHUNK) & (col + start >= pad)
    s = jnp.where(ok, s, NEG)
    e = jnp.exp(s - jnp.max(s, axis=-1, keepdims=True))
    return e / jnp.sum(e, axis=-1, keepdims=True)


def _att_bias(ext_ref, qt, kb, ne):
    e = jnp.broadcast_to(ext_ref[0], (qt, ne))
    return pltpu.roll(e, ne - qt + 1, 1, stride=1, stride_axis=0)[:, :kb]


def _attn_fwd(proj, kp, vp, ext, W, rider=None):
    S = proj.shape[0]
    H = W // HEAD_DIM
    qt, pad, kb, ne = _att_dims(S)

    def body(q_ref, k_ref, v_ref, ext_ref, o_ref, bias):
        g = pl.program_id(1)

        @pl.when(g == 0)
        def _():
            bias[...] = _att_bias(ext_ref, qt, kb, ne)

        start = pl.multiple_of(g * qt, qt)
        pn = _att_scores(q_ref[...].astype(bf16), k_ref[pl.ds(start, kb), :], bias[...], start, qt, pad, kb)
        o_ref[...] = _dot(pn.astype(bf16), v_ref[pl.ds(start, kb), :], NN).astype(bf16)

    outs = _grid_call(
        body,
        rider,
        name="attn_fwd",
        grid=(H, S // qt),
        in_specs=[
            pl.BlockSpec((qt, HEAD_DIM), lambda h, g: (g, 4 * H + h)),
            pl.BlockSpec((S + pad, HEAD_DIM), lambda h, g: (0, h)),
            pl.BlockSpec((S + pad, HEAD_DIM), lambda h, g: (0, h)),
            pl.BlockSpec((1, 1, ne), lambda h, g: (h, 0, 0)),
        ],
        out_specs=[pl.BlockSpec((qt, HEAD_DIM), lambda h, g: (g, h))],
        out_shape=[jax.ShapeDtypeStruct((S, W), bf16)],
        scratch_shapes=[pltpu.VMEM((qt, kb), f32)],
        args=(proj, kp, vp, ext),
    )
    return outs[0] if rider is None else outs


def _attn_bwd(proj, kp, vp, ext, dycat, W, rider=None):
    S = proj.shape[0]
    H = W // HEAD_DIM
    qt, pad, kb, ne = _att_dims(S)
    scale = HEAD_DIM ** -0.5

    def body(q_ref, k_ref, v_ref, ext_ref, do_ref, dq_ref, dk_ref, dv_ref, db_ref, bias):
        g = pl.program_id(1)

        @pl.when(g == 0)
        def _():
            bias[...] = _att_bias(ext_ref, qt, kb, ne)
            dk_ref[...] = jnp.zeros_like(dk_ref)
            dv_ref[...] = jnp.zeros_like(dv_ref)
            db_ref[...] = jnp.zeros_like(db_ref)

        start = pl.multiple_of(g * qt, qt)
        q16 = q_ref[...].astype(bf16)
        kb16 = k_ref[pl.ds(start, kb), :]
        vb16 = v_ref[pl.ds(start, kb), :]
        pn = _att_scores(q16, kb16, bias[...], start, qt, pad, kb)
        do16 = do_ref[...].astype(bf16)
        dpn = _dot(do16, vb16, NT)
        dv_ref[pl.ds(start, kb), :] += _dot(pn.astype(bf16), do16, TN)
        ds = pn * (dpn - jnp.sum(dpn * pn, axis=-1, keepdims=True))
        db_ref[0] += ds
        ds16 = ds.astype(bf16)
        dq_ref[...] = (_dot(ds16, kb16, NN) * scale).astype(bf16)
        dk_ref[pl.ds(start, kb), :] += _dot(ds16, q16, TN) * scale

    kv_spec = pl.BlockSpec((S + pad, HEAD_DIM), lambda h, g: (0, h))
    return _grid_call(
        body,
        rider,
        name="attn_bwd",
        grid=(H, S // qt),
        in_specs=[
            pl.BlockSpec((qt, HEAD_DIM), lambda h, g: (g, 4 * H + h)),
            kv_spec,
            kv_spec,
            pl.BlockSpec((1, 1, ne), lambda h, g: (h, 0, 0)),
            pl.BlockSpec((qt, HEAD_DIM), lambda h, g: (g, H + h)),
        ],
        out_specs=[
            pl.BlockSpec((qt, HEAD_DIM), lambda h, g: (g, h)),
            kv_spec,
            kv_spec,
            pl.BlockSpec((1, qt, kb), lambda h, g: (h, 0, 0)),
        ],
        out_shape=[
            jax.ShapeDtypeStruct((S, W), bf16),
            jax.ShapeDtypeStruct((S + pad, W), f32),
            jax.ShapeDtypeStruct((S + pad, W), f32),
            jax.ShapeDtypeStruct((H, qt, kb), f32),
        ],
        scratch_shapes=[pltpu.VMEM((qt, kb), f32)],
        args=(proj, kp, vp, ext, dycat),
    )


def _bias_onehot(qt, ne, nrel_pad):
    m = lax.broadcasted_iota(jnp.int32, (nrel_pad, ne), 1)
    r = lax.broadcasted_iota(jnp.int32, (nrel_pad, ne), 0)
    rel = LEFT_CHUNKS * CHUNK + qt - 1 - m
    hot = (r == jnp.clip(rel, -REL_CLIP, REL_CLIP) + REL_CLIP) & (m < ne - 1)
    return hot.astype(f32)


def _bias_ext(rel_bias_pad, onehot):
    H, _, nr = rel_bias_pad.shape
    ne = onehot.shape[1]

    def body(rb_ref, oh_ref, o_ref):
        o_ref[0] = _dot(rb_ref[0], oh_ref[...], NN, HIGHEST)

    return pl.pallas_call(
        body,
        name="bias_ext",
        grid=(H,),
        in_specs=[pl.BlockSpec((1, 1, nr), lambda h: (h, 0, 0)), pl.BlockSpec((nr, ne), lambda h: (0, 0))],
        out_specs=pl.BlockSpec((1, 1, ne), lambda h: (h, 0, 0)),
        out_shape=jax.ShapeDtypeStruct((H, 1, ne), f32),
        compiler_params=pltpu.CompilerParams(dimension_semantics=("parallel",)),
    )(rel_bias_pad, onehot)


def _bias_bwd(dbias, onehot):
    H, qt, kb = dbias.shape
    nr, ne = onehot.shape

    def body(d_ref, oh_ref, o_ref):
        row = lax.broadcasted_iota(jnp.int32, (qt, qt), 0)
        col = lax.broadcasted_iota(jnp.int32, (qt, qt), 1)
        flip = (row + col == qt - 1).astype(f32)
        x = _dot(flip, d_ref[0], NN, HIGHEST)
        x = jnp.concatenate([x, jnp.zeros((qt, ne - kb), f32)], axis=1)
        de = jnp.sum(pltpu.roll(x, 0, 1, stride=1, stride_axis=0), axis=0, keepdims=True)
        o_ref[0] = _dot(de, oh_ref[...], NT, HIGHEST)

    return pl.pallas_call(
        body,
        name="bias_bwd",
        grid=(H,),
        in_specs=[pl.BlockSpec((1, qt, kb), lambda h: (h, 0, 0)), pl.BlockSpec((nr, ne), lambda h: (0, 0))],
        out_specs=pl.BlockSpec((1, 1, nr), lambda h: (h, 0, 0)),
        out_shape=jax.ShapeDtypeStruct((H, 1, nr), f32),
        compiler_params=pltpu.CompilerParams(dimension_semantics=("parallel",)),
    )(dbias, onehot)


def _place():
    x, y, c = lax.axis_index("x"), lax.axis_index("y"), lax.axis_index("c")
    return x, y, c


def _flip(x, y, k):
    return (1 - x if k & 2 else x), (1 - y if k & 1 else y)


def _region(ref, shard_shape, axis, j, half):
    R, C = shard_shape
    if axis == 0:
        if half is None:
            return ref.at[pl.ds(j * R, R), :]
        return ref.at[pl.ds(j * R + half * (R // 2), R // 2), :]
    if half is None:
        return ref.at[:, pl.ds(j * C, C)]
    return ref.at[pl.ds(half * (R // 2), R // 2), pl.ds(j * C, C)]


def _remote(src, dst, send_sem, recv_sem, dev):
    return pltpu.make_async_remote_copy(src_ref=src, dst_ref=dst, send_sem=send_sem, recv_sem=recv_sem,
                                        device_id=dev, device_id_type=MESH)


def _cast_place(shard, axis, where, *, name):
    R, C = shard.shape
    tr = _div(R, 256, 2 * SUBLANES)
    nr = R // tr
    full = (N_CHIPS * R, C) if axis == 0 else (R, N_CHIPS * C)
    omap = (lambda r, s: (s[0] * nr + r, 0)) if axis == 0 else (lambda r, s: (r, s[0]))

    def body(s_ref, x_ref, o_ref):
        o_ref[...] = x_ref[...].astype(bf16)

    return pl.pallas_call(
        body,
        name=name,
        grid_spec=pltpu.PrefetchScalarGridSpec(
            num_scalar_prefetch=1,
            grid=(nr,),
            in_specs=[pl.BlockSpec((tr, C), lambda r, s: (r, 0))],
            out_specs=pl.BlockSpec((tr, C), omap),
        ),
        out_shape=jax.ShapeDtypeStruct(full, bf16),
        compiler_params=pltpu.CompilerParams(dimension_semantics=("parallel",)),
    )(where, shard)


class _Rider(NamedTuple):
    ins: list
    out_shapes: list
    aliases: dict
    sems: list
    first: Callable
    last: Callable


def _run_rider(rider, *, name):
    ni, no = len(rider.ins), len(rider.out_shapes)

    def body(*refs):
        ins, outs, sems = refs[:ni], refs[ni:ni + no], refs[ni + no:]
        rider.first(ins, outs, sems)
        rider.last(ins, outs, sems)

    return pl.pallas_call(
        body,
        name=name,
        in_specs=[_hbm()] * ni,
        out_specs=[_hbm()] * no,
        out_shape=rider.out_shapes,
        input_output_aliases=rider.aliases,
        scratch_shapes=rider.sems,
    )(*rider.ins)


def _gather_rider(fulls, shard_shapes, axes, splits, rels):
    n = len(fulls)

    def stage1(ins, outs, sems):
        x, y, c = _place()
        jo = 2 * x + y
        cps = []
        for m in range(n):
            half = c if splits[m] else None
            for k in rels[m]:
                px, py = _flip(x, y, k)
                cps.append(_remote(_region(ins[m], shard_shapes[m], axes[m], jo, half), _region(outs[m], shard_shapes[m], axes[m], jo, half),
                                   sems[0].at[3 * m + k - 1], sems[1].at[3 * m + k - 1], (px, py, c)))
        return cps

    def first(ins, outs, sems):
        for cp in stage1(ins, outs, sems):
            cp.start()

    def last(ins, outs, sems):
        x, y, c = _place()
        send1, recv1, send2, recv2 = sems
        forwards = []
        for m in range(n):
            half = c if splits[m] else None
            for k in rels[m]:
                px, py = _flip(x, y, k)
                land = _region(outs[m], shard_shapes[m], axes[m], 2 * px + py, half)
                _remote(land, land, send1.at[3 * m + k - 1], recv1.at[3 * m + k - 1], (x, y, c)).wait_recv()
                if splits[m]:
                    fw = _remote(land, land, send2.at[3 * m + k - 1], recv2.at[3 * m + k - 1], (x, y, 1 - c))
                    fw.start()
                    forwards.append(fw)
        for m in range(n):
            if not splits[m]:
                continue
            for k in rels[m]:
                px, py = _flip(x, y, k)
                land = _region(outs[m], shard_shapes[m], axes[m], 2 * px + py, 1 - c)
                _remote(land, land, send2.at[3 * m + k - 1], recv2.at[3 * m + k - 1], (x, y, c)).wait_recv()
        for cp in stage1(ins, outs, sems) + forwards:
            cp.wait_send()

    sems = pltpu.SemaphoreType.DMA((3 * n,))
    return _Rider(list(fulls), [jax.ShapeDtypeStruct(f.shape, f.dtype) for f in fulls], {m: m for m in range(n)},
                  [sems, sems, sems, sems], first, last)


def _chips_rider(ts):
    n = len(ts)

    def copies(ins, outs, sems):
        x, y, c = _place()
        cps = []
        for m in range(n):
            for k in (1, 2, 3):
                px, py = _flip(x, y, k)
                cps.append(_remote(ins[m].at[2 * px + py], outs[m].at[k - 1], sems[0].at[3 * m + k - 1], sems[1].at[3 * m + k - 1], (px, py, c)))
        return cps

    def first(ins, outs, sems):
        for cp in copies(ins, outs, sems):
            cp.start()

    def last(ins, outs, sems):
        cps = copies(ins, outs, sems)
        for cp in cps:
            cp.wait_recv()
        for cp in cps:
            cp.wait_send()

    sems = pltpu.SemaphoreType.DMA((3 * n,))
    return _Rider(list(ts), [jax.ShapeDtypeStruct((3,) + t.shape[1:], t.dtype) for t in ts], {}, [sems, sems], first, last)


def _rs_sibling(dws, shard_shapes, axes, *, name):
    n = len(dws)

    def body(*refs):
        ins, got = refs[:n], refs[n:2 * n]
        send, recv = refs[2 * n:]
        x, y, c = _place()
        cps = []
        for m in range(n):
            for j in range(N_CHIPS):
                cp = _remote(_region(ins[m], shard_shapes[m], axes[m], j, 1 - c), got[m].at[j],
                             send.at[N_CHIPS * m + j], recv.at[N_CHIPS * m + j], (x, y, 1 - c))
                cp.start()
                cps.append(cp)
        for cp in cps:
            cp.wait_recv()
        for cp in cps:
            cp.wait_send()

    sems = pltpu.SemaphoreType.DMA((N_CHIPS * n,))
    return pl.pallas_call(
        body,
        name=name,
        in_specs=[_hbm()] * n,
        out_specs=[_hbm()] * n,
        out_shape=[jax.ShapeDtypeStruct((N_CHIPS, s[0] // 2, s[1]), d.dtype) for s, d in zip(shard_shapes, dws)],
        scratch_shapes=[sems, sems],
    )(*dws)


def _pair_sum(dw, got, shard_shape, axis, where, *, name):
    R, C = shard_shape
    hr = R // 2
    tr = _div(hr, 256, 2 * SUBLANES)
    nr = hr // tr
    dmap = (lambda j, r, s: ((2 * j + s[1]) * nr + r, 0)) if axis == 0 else (lambda j, r, s: (s[1] * nr + r, j))
    slot = pl.BlockSpec((1, tr, C), lambda j, r, s: (j, r, 0))

    def body(s_ref, d_ref, g_ref, o_ref):
        o_ref[0] = (d_ref[...].astype(f32) + g_ref[0].astype(f32)).astype(bf16)

    return pl.pallas_call(
        body,
        name=name,
        grid_spec=pltpu.PrefetchScalarGridSpec(
            num_scalar_prefetch=1,
            grid=(N_CHIPS, nr),
            in_specs=[pl.BlockSpec((tr, C), dmap), slot],
            out_specs=slot,
        ),
        out_shape=jax.ShapeDtypeStruct((N_CHIPS, hr, C), bf16),
        compiler_params=pltpu.CompilerParams(dimension_semantics=("parallel", "parallel")),
    )(where, dw, got)


def _chip_sum(pair, others, where, *, name):
    _, hr, C = pair.shape
    tr = _div(hr, 256, 2 * SUBLANES)
    nr = hr // tr

    def body(s_ref, p_ref, o_ref, g_ref):
        g_ref[...] = p_ref[0].astype(f32) + o_ref[0].astype(f32) + o_ref[1].astype(f32) + o_ref[2].astype(f32)

    return pl.pallas_call(
        body,
        name=name,
        grid_spec=pltpu.PrefetchScalarGridSpec(
            num_scalar_prefetch=1,
            grid=(nr,),
            in_specs=[pl.BlockSpec((1, tr, C), lambda r, s: (s[0], r, 0)), pl.BlockSpec((3, tr, C), lambda r, s: (0, r, 0))],
            out_specs=pl.BlockSpec((tr, C), lambda r, s: (s[1] * nr + r, 0)),
        ),
        out_shape=jax.ShapeDtypeStruct((2 * hr, C), f32),
        compiler_params=pltpu.CompilerParams(dimension_semantics=("parallel",)),
    )(where, pair, others)


def _rs_join(gs):
    n = len(gs)

    def body(*refs):
        ins, outs = refs[:n], refs[n:2 * n]
        send, recv = refs[2 * n:]
        x, y, c = _place()
        cps = []
        for m in range(n):
            hr = gs[m].shape[0] // 2
            cp = _remote(ins[m].at[pl.ds(c * hr, hr), :], outs[m].at[pl.ds(c * hr, hr), :], send.at[m], recv.at[m], (x, y, 1 - c))
            cp.start()
            cps.append(cp)
        for cp in cps:
            cp.wait_recv()
        for cp in cps:
            cp.wait_send()

    sems = pltpu.SemaphoreType.DMA((n,))
    return pl.pallas_call(
        body,
        name="grads_join_halves",
        in_specs=[_hbm()] * n,
        out_specs=[_hbm()] * n,
        out_shape=[jax.ShapeDtypeStruct(g.shape, g.dtype) for g in gs],
        input_output_aliases={m: m for m in range(n)},
        scratch_shapes=[sems, sems],
    )(*gs)


def _allreduce_small(v):
    rows = v.shape[0]

    def body(v_ref, o_ref, gath, send, recv):
        x, y, c = _place()
        me = 4 * x + 2 * y + c
        gath[pl.ds(me, 1)] = v_ref[...][None]
        cps = []
        for k in range(1, N_DEV):
            peer = (1 - x if k & 4 else x, 1 - y if k & 2 else y, 1 - c if k & 1 else c)
            cp = _remote(v_ref, gath.at[me], send.at[k - 1], recv.at[k - 1], peer)
            cp.start()
            cps.append(cp)
        for cp in cps:
            cp.wait_recv()
        for cp in cps:
            cp.wait_send()
        acc = gath[0]
        for d in range(1, N_DEV):
            acc = acc + gath[d]
        o_ref[...] = acc

    return pl.pallas_call(
        body,
        name="small_allreduce",
        in_specs=[_vmem()],
        out_specs=_vmem(),
        out_shape=jax.ShapeDtypeStruct(v.shape, f32),
        scratch_shapes=[pltpu.VMEM((N_DEV, rows, LANES), f32), pltpu.SemaphoreType.DMA((N_DEV - 1,)), pltpu.SemaphoreType.DMA((N_DEV - 1,))],
    )(v)


def _adamw_math(w, g, m, v):
    m = ADAM_B1 * m + (1.0 - ADAM_B1) * g
    v = ADAM_B2 * v + (1.0 - ADAM_B2) * (g * g)
    m_hat = m / (1.0 - ADAM_B1 ** ADAM_STEP)
    v_hat = v / (1.0 - ADAM_B2 ** ADAM_STEP)
    delta = -ADAM_LR * (m_hat / (jnp.sqrt(v_hat) + ADAM_EPS) + ADAM_WD * w)
    return delta, m, v


def _adamw(w, g, m, v, *, name):
    R, C = w.shape
    tr = _div(R, 128, SUBLANES)

    def body(w_ref, g_ref, m_ref, v_ref, d_ref, mo_ref, vo_ref):
        d, mn, vn = _adamw_math(w_ref[...], g_ref[...], m_ref[...], v_ref[...])
        d_ref[...] = d
        mo_ref[...] = mn
        vo_ref[...] = vn

    spec = pl.BlockSpec((tr, C), lambda r: (r, 0))
    return pl.pallas_call(
        body,
        name=name,
        grid=(R // tr,),
        in_specs=[spec] * 4,
        out_specs=[spec] * 3,
        out_shape=[jax.ShapeDtypeStruct((R, C), f32)] * 3,
        compiler_params=pltpu.CompilerParams(dimension_semantics=("parallel",)),
    )(w, g, m, v)


def _pack(arrs):
    flat = jnp.concatenate([a.reshape(-1).astype(f32) for a in arrs])
    tile = SUBLANES * LANES
    total = -(-flat.shape[0] // tile) * tile
    return jnp.pad(flat, (0, total - flat.shape[0])).reshape(total // LANES, LANES)


def _unpack(buf, shapes):
    flat = buf.reshape(-1)
    out, off = [], 0
    for s in shapes:
        size = int(np.prod(s))
        out.append(flat[off:off + size].reshape(s))
        off += size
    return out


def kernel(x, p, norm_mix, w_in, lb_logits, hg_norm, rel_bias, w_out, norm_ffn, w_up, conv_w, conv_b, w_down, norm_ple, w_ple_gate, w_ple_proj, final_norm, loss_target, m_norm_mix, m_w_in, m_lb_logits, m_hg_norm, m_rel_bias, m_w_out, m_norm_ffn, m_w_up, m_conv_w, m_conv_b, m_w_down, m_norm_ple, m_w_ple_gate, m_w_ple_proj, m_final_norm, v_norm_mix, v_w_in, v_lb_logits, v_hg_norm, v_rel_bias, v_w_out, v_norm_ffn, v_w_up, v_conv_w, v_conv_b, v_w_down, v_norm_ple, v_w_ple_gate, v_w_ple_proj, v_final_norm):
    S, D = x.shape[1], x.shape[2]
    xv, pv, tgt = x[0], p[0, 0], loss_target[0]
    W = (w_in.shape[2] * N_CHIPS) // 7
    H = W // HEAD_DIM
    F = w_down.shape[1] * N_CHIPS
    tc = _div(F // 2, 1408, LANES)
    nj = F // tc
    jx, jy = lax.axis_index("x"), lax.axis_index("y")
    chip = 2 * jx + jy

    big = [w_in[0], w_out[0], w_up[0], w_down[0], w_ple_gate[0], w_ple_proj[0]]
    big_axes = [1, 0, 1, 0, 0, 1]
    where = jnp.stack([chip, lax.axis_index("c")]).astype(jnp.int32)
    cw_pad = jnp.pad(conv_w[0], ((0, SUBLANES - conv_w.shape[1]), (0, 0)))
    cw_mine = lax.dynamic_update_slice(jnp.zeros((N_CHIPS * SUBLANES, cw_pad.shape[1]), f32), cw_pad, (chip * SUBLANES, 0))
    placed = [_cast_place(b, ax, where, name=f"cast_place_{i}") for i, (b, ax) in enumerate(zip(big, big_axes))]
    shard_shapes = [b.shape for b in big]

    every = (1, 2, 3)

    def gather(items):
        return _gather_rider([b for b, _, _ in items],
                             [cw_pad.shape if i is None else shard_shapes[i] for _, i, _ in items],
                             [0 if i is None else big_axes[i] for _, i, _ in items],
                             [i is not None for _, i, _ in items], [r for _, _, r in items])

    (W_in,) = _run_rider(gather([(placed[0], 0, every)]), name="gather_w_in")

    tm = _div(S, 1024, LANES)

    a1 = _rms_fwd(xv, norm_mix, name="rms_mix")
    proj, W_out, W_up_part = _mm(a1, W_in, dims="nn", tm=tm, tn=_div(7 * W, 512, LANES), tk=D, out_dtype=f32, name="mm_in",
                                 rider=gather([(placed[1], 1, every), (placed[2], 2, (1, 2))]))
    y_hg, o_hg, st, W_down = _hgrn_fwd(proj, lb_logits, hg_norm, W, rider=gather([(placed[3], 3, every)]))

    qt, pad, kb, ne = _att_dims(S)
    nrel = rel_bias.shape[2]
    nrel_pad = -(-nrel // LANES) * LANES
    onehot = _bias_onehot(qt, ne, nrel_pad)
    rb_pad = jnp.pad(rel_bias[0], ((0, 0), (0, nrel_pad - nrel)))[:, None, :]
    ext = _bias_ext(rb_pad, onehot)
    kp = jnp.pad(proj[:, 5 * W:6 * W].astype(bf16), ((pad, 0), (0, 0)))
    vp = jnp.pad(proj[:, 6 * W:7 * W].astype(bf16), ((pad, 0), (0, 0)))
    y_att, W_up = _attn_fwd(proj, kp, vp, ext, W, rider=gather([(W_up_part, 2, (3,))]))
    ycat = jnp.concatenate([y_hg, y_att], axis=1)

    tn_d = _div(D, 512, LANES)
    h1, W_pg, W_pp, cw_all = _mm(ycat, W_out, dims="nn", tm=tm, tn=tn_d, tk=D, out_dtype=f32, res=xv, name="mm_out",
                                 rider=gather([(placed[4], 4, every), (placed[5], 5, every), (cw_mine, None, every)]))
    a2 = _rms_fwd(h1, norm_ffn, name="rms_ffn")
    perm = lambda b: (b % 2) * nj + b // 2
    u = _mm(a2, W_up, dims="nn", tm=tm, tn=tc, tk=D, out_dtype=f32, name="mm_up", b_idx=lambda i, j, k: (k, perm(j)))
    ncw = conv_w.shape[1]
    cw_full = jnp.transpose(cw_all.reshape(N_CHIPS, SUBLANES, -1)[:, :ncw], (1, 0, 2)).reshape(ncw, -1)
    cb = conv_b
    mact = _convact_fwd(u, cw_full, cb, tc)
    tn_w = _div(D, 1024, LANES)
    h2 = _mm(mact, W_down, dims="nn", tm=_div(S, 512, LANES), tn=tn_w, tk=F, out_dtype=f32, res=h1, name="mm_down", cols_outer=True)
    a3 = _rms_fwd(h2, norm_ple, name="rms_ple")
    zg = _mm(a3, W_pg, dims="nn", tm=tm, tn=tn_d, tk=D, out_dtype=f32, name="mm_ple_gate")
    p16 = pv.astype(bf16)
    pp = _mm(p16, W_pp, dims="nn", tm=tm, tn=tn_d, tk=pv.shape[1], out_dtype=f32, name="mm_ple_proj")
    dh3, dzg, dpp, loss_part, d_fn = _tail(h2, zg, pp, tgt, final_norm.reshape(1, D))

    pair, others = {}, {}

    def reduce_start(idx, grads):
        got = _rs_sibling(grads, [shard_shapes[i] for i in idx], [big_axes[i] for i in idx], name="grads_to_sibling_" + "_".join(map(str, idx)))
        for i, d, g in zip(idx, grads, got):
            pair[i] = _pair_sum(d, g, shard_shapes[i], big_axes[i], where, name=f"grad_pair_sum_{i}")
        return _chips_rider([pair[i] for i in idx])

    def reduce_landed(idx, landed):
        for i, o in zip(idx, landed):
            others[i] = o

    tk_s = S
    dW_pp = _mm(p16, dpp, dims="tn", tm=pv.shape[1], tn=tn_w, tk=tk_s, out_dtype=bf16, name="mm_d_ple_proj")
    dW_pg = _mm(a3, dzg, dims="tn", tm=tn_w, tn=tn_w, tk=tk_s, out_dtype=bf16, name="mm_d_ple_gate")
    da3 = _mm(dzg, W_pg, dims="nt", tm=tm, tn=_div(D, 1024, LANES), tk=D, out_dtype=f32, name="mm_da3")
    dh2, dh2b, d_nple = _rms_bwd(h2, norm_ple, da3, dh3, name="rms_ple_bwd")
    dm, *landed = _mm(dh2b, W_down, dims="nt", tm=tm, tn=_div(F, 512, LANES), tk=D, out_dtype=f32, name="mm_dm",
                      rider=reduce_start([4, 5], [dW_pg, dW_pp]))
    reduce_landed([4, 5], landed)
    dW_down = _mm(mact, dh2b, dims="tn", tm=_div(F, 1408, LANES), tn=tn_d, tk=tk_s, out_dtype=bf16, name="mm_d_down")
    duc, dcw_g, dcw_v, dcb_g, dcb_v, *landed = _convact_bwd_a(u, dm, cw_full, cb, tc, rider=reduce_start([3], [dW_down]))
    reduce_landed([3], landed)
    du = _convact_bwd_b(duc, cw_full, tc)
    dW_up = _mm(a2, du, dims="tn", tm=_div(D, 1024, LANES), tn=tc, tk=tk_s, out_dtype=bf16, name="mm_d_up",
                o_idx=lambda i, j, k: (i, perm(j)))
    da2, *landed = _mm(du, W_up, dims="nt", tm=tm, tn=D, tk=tc, out_dtype=f32, name="mm_da2",
                       b_idx=lambda i, j, k: (j, perm(k)), rider=reduce_start([2], [dW_up]))
    reduce_landed([2], landed)
    dh1, dh1b, d_nffn = _rms_bwd(h1, norm_ffn, da2, dh2, name="rms_ffn_bwd")
    dycat = _mm(dh1b, W_out, dims="nt", tm=tm, tn=_div(D, 1024, LANES), tk=D, out_dtype=f32, name="mm_dycat")
    dW_out = _mm(ycat, dh1b, dims="tn", tm=tn_w, tn=tn_w, tk=tk_s, out_dtype=bf16, name="mm_d_out")
    dq_att, dk_pad, dv_pad, dbias, *landed = _attn_bwd(proj, kp, vp, ext, dycat, W, rider=reduce_start([1], [dW_out]))
    reduce_landed([1], landed)
    dhq, dhf, dhi, dhg, d_lbl, d_hgn = _hgrn_bwd(proj, o_hg, st, dycat, lb_logits, hg_norm, W)
    d_rb = _bias_bwd(dbias, onehot)[:, 0, :nrel]
    dproj = jnp.concatenate([dhq, dhf, dhi, dhg, dq_att, dk_pad[pad:].astype(bf16), dv_pad[pad:].astype(bf16)], axis=1)
    dW_in = _mm(a1, dproj, dims="tn", tm=tn_w, tn=_div(7 * W, 1024, LANES), tk=tk_s, out_dtype=bf16, name="mm_d_in")
    da1, *landed = _mm(dproj, W_in, dims="nt", tm=tm, tn=D, tk=_div(7 * W, 1792, LANES), out_dtype=f32, name="mm_da1",
                       rider=reduce_start([0], [dW_in]))
    reduce_landed([0], landed)
    grad_x, _, d_nmix = _rms_bwd(xv, norm_mix, da1, dh1, name="rms_mix_bwd")

    d_cw = jnp.concatenate([dcw_g, dcw_v], axis=1)
    d_cb = jnp.concatenate([dcb_g, dcb_v], axis=1)
    small_parts = [loss_part[:, :1], d_nmix, d_lbl, d_hgn, d_rb, d_nffn, d_cw, d_cb, d_nple, d_fn]
    small_shapes = [(1, 1), (1, D), lb_logits.shape, hg_norm.shape, (H, nrel), (1, D), (ncw, 2 * F), (1, 2 * F), (1, D), (1, D)]
    red = _unpack(_allreduce_small(_pack(small_parts)), small_shapes)
    loss = red[0].reshape(())
    g_nmix, g_lbl, g_hgn, g_rb, g_nffn, g_cw_all, g_cb, g_nple, g_fn = red[1:]
    csh = conv_w.shape[2]
    g_cw = lax.dynamic_slice(g_cw_all, (0, chip * csh), (ncw, csh))
    small_g = [g_nmix, g_lbl, g_hgn, g_rb[None], g_nffn, g_cw[None], g_cb, g_nple, g_fn.reshape(D)]
    small_w = [norm_mix, lb_logits, hg_norm, rel_bias, norm_ffn, conv_w, conv_b, norm_ple, final_norm]
    small_m = [m_norm_mix, m_lb_logits, m_hg_norm, m_rel_bias, m_norm_ffn, m_conv_w, m_conv_b, m_norm_ple, m_final_norm]
    small_v = [v_norm_mix, v_lb_logits, v_hg_norm, v_rel_bias, v_norm_ffn, v_conv_w, v_conv_b, v_norm_ple, v_final_norm]
    shapes_s = [w.shape for w in small_w]
    sd, sm, sv = _adamw(_pack(small_w), _pack(small_g), _pack(small_m), _pack(small_v), name="adamw_small")
    small_g = [g.reshape(s) for g, s in zip(small_g, shapes_s)]
    small_d, small_nm, small_nv = _unpack(sd, shapes_s), _unpack(sm, shapes_s), _unpack(sv, shapes_s)

    halves = [_chip_sum(pair[i], others[i], where, name=f"grad_chip_sum_{i}") for i in range(6)]
    g_big = _rs_join(halves)
    big_m = [m_w_in[0], m_w_out[0], m_w_up[0], m_w_down[0], m_w_ple_gate[0], m_w_ple_proj[0]]
    big_v = [v_w_in[0], v_w_out[0], v_w_up[0], v_w_down[0], v_w_ple_gate[0], v_w_ple_proj[0]]
    big_d, big_nm, big_nv = [], [], []
    for i in range(6):
        d_, m_, v_ = _adamw(big[i], g_big[i], big_m[i], big_v[i], name=f"adamw_{i}")
        big_d.append(d_[None])
        big_nm.append(m_[None])
        big_nv.append(v_[None])
    g_big = [g[None] for g in g_big]

    def order(sm_list, bg_list):
        s, b = sm_list, bg_list
        return [s[0], b[0], s[1], s[2], s[3], b[1], s[4], b[2], s[5], s[6], b[3], s[7], b[4], b[5], s[8]]

    return (loss, grad_x[None], *order(small_g, g_big), *order(small_d, big_d), *order(small_nm, big_nm), *order(small_nv, big_nv))
```

```python
import functools
from typing import Callable, NamedTuple

import jax
import jax.numpy as jnp
import numpy as np
from jax import lax
from jax.experimental import pallas as pl
from jax.experimental.pallas import tpu as pltpu

f32 = jnp.float32
bf16 = jnp.bfloat16

CHUNK = 64
HEAD_DIM = 128
LEFT_CHUNKS = 8
REL_CLIP = 128
EPS = 1e-6
HG_SUB = 16
HG_HEADS_PER_STEP = 8
ATT_Q_ROWS = 256
ADAM_LR, ADAM_B1, ADAM_B2, ADAM_EPS, ADAM_WD, ADAM_STEP = 0.001, 0.9, 0.999, 1e-08, 0.01, 10
LANES = 128
SUBLANES = 8
N_CHIPS = 4
N_DEV = 8
MESH = pl.DeviceIdType.MESH
NEG = float(np.finfo(np.float32).min)

NN = (((1,), (0,)), ((), ()))
NT = (((1,), (1,)), ((), ()))
TN = (((0,), (0,)), ((), ()))
HIGHEST = lax.Precision.HIGHEST


def _dot(a, b, dims, precision=None):
    return lax.dot_general(a, b, dims, preferred_element_type=f32, precision=precision)


def _sigmoid(v):
    return 1.0 / (1.0 + jnp.exp(-v))


def _div(n, pref, mult):
    best = None
    d = mult
    while d <= min(n, pref):
        if n % d == 0:
            best = d
        d += mult
    return best if best is not None else n


def _hbm():
    return pl.BlockSpec(memory_space=pltpu.HBM)


def _vmem():
    return pl.BlockSpec(memory_space=pltpu.VMEM)


def _mm(a, b, *, dims, tm, tn, tk, out_dtype, name, res=None, a_idx=None, b_idx=None, o_idx=None, out_shape=None, rider=None,
        cols_outer=False):
    if dims == "nn":
        (M, K), (_, N) = a.shape, b.shape
        a_blk, b_blk, dn = (tm, tk), (tk, tn), NN
        a_def, b_def = (lambda i, j, k: (i, k)), (lambda i, j, k: (k, j))
    elif dims == "nt":
        (M, K), (N, _) = a.shape, b.shape
        a_blk, b_blk, dn = (tm, tk), (tn, tk), NT
        a_def, b_def = (lambda i, j, k: (i, k)), (lambda i, j, k: (j, k))
    else:
        (K, M), (_, N) = a.shape, b.shape
        a_blk, b_blk, dn = (tk, tm), (tk, tn), TN
        a_def, b_def = (lambda i, j, k: (k, i)), (lambda i, j, k: (k, j))
    assert M % tm == 0 and N % tn == 0 and K % tk == 0, (name, M, N, K, tm, tn, tk)
    nk = K // tk
    has_res = res is not None

    def body(*refs):
        a_ref, b_ref = refs[0], refs[1]
        res_ref = refs[2] if has_res else None
        o_ref = refs[3] if has_res else refs[2]
        part = _dot(a_ref[...].astype(bf16), b_ref[...].astype(bf16), dn)

        def finish(acc):
            if has_res:
                acc = acc + res_ref[...]
            o_ref[...] = acc.astype(out_dtype)

        if nk == 1:
            finish(part)
        else:
            acc_ref = refs[-1]
            k = pl.program_id(2)

            @pl.when(k == 0)
            def _():
                acc_ref[...] = part

            @pl.when(k > 0)
            def _():
                acc_ref[...] += part

            @pl.when(k == nk - 1)
            def _():
                finish(acc_ref[...])

    def ordered(f):
        return (lambda j, i, k: f(i, j, k)) if cols_outer else f

    o_map = ordered(o_idx or (lambda i, j, k: (i, j)))
    in_specs = [pl.BlockSpec(a_blk, ordered(a_idx or a_def)), pl.BlockSpec(b_blk, ordered(b_idx or b_def))]
    args = [a, b]
    if has_res:
        in_specs.append(pl.BlockSpec((tm, tn), o_map))
        args.append(res)
    outs = _grid_call(
        body,
        rider,
        name=name,
        grid=(N // tn, M // tm, nk) if cols_outer else (M // tm, N // tn, nk),
        in_specs=in_specs,
        out_specs=[pl.BlockSpec((tm, tn), o_map)],
        out_shape=[jax.ShapeDtypeStruct(out_shape or (M, N), out_dtype)],
        scratch_shapes=[pltpu.VMEM((tm, tn), f32)] if nk > 1 else [],
        args=args,
    )
    return outs[0] if rider is None else outs


def _rms_fwd(xv, g, *, name):
    S, D = xv.shape
    ts = _div(S, 512, SUBLANES)

    def body(x_ref, g_ref, o_ref):
        v = x_ref[...]
        r = lax.rsqrt(jnp.mean(v * v, axis=-1, keepdims=True) + EPS)
        o_ref[...] = (v * r * g_ref[...]).astype(bf16)

    return pl.pallas_call(
        body,
        name=name,
        grid=(S // ts,),
        in_specs=[pl.BlockSpec((ts, D), lambda i: (i, 0)), pl.BlockSpec((1, D), lambda i: (0, 0))],
        out_specs=pl.BlockSpec((ts, D), lambda i: (i, 0)),
        out_shape=jax.ShapeDtypeStruct((S, D), bf16),
        compiler_params=pltpu.CompilerParams(dimension_semantics=("parallel",)),
    )(xv, g)


def _rms_bwd(xv, g, dy, dres, *, name):
    S, D = xv.shape
    ts = _div(S, 256, SUBLANES)

    def body(x_ref, g_ref, dy_ref, dres_ref, dx_ref, dxb_ref, dg_ref):
        i = pl.program_id(0)
        v = x_ref[...]
        r = lax.rsqrt(jnp.mean(v * v, axis=-1, keepdims=True) + EPS)
        vn = v * r
        d = dy_ref[...]
        part = jnp.sum(d * vn, axis=0, keepdims=True)

        @pl.when(i == 0)
        def _():
            dg_ref[...] = part

        @pl.when(i > 0)
        def _():
            dg_ref[...] += part

        t = d * g_ref[...]
        dx = dres_ref[...] + r * (t - vn * jnp.mean(t * vn, axis=-1, keepdims=True))
        dx_ref[...] = dx
        dxb_ref[...] = dx.astype(bf16)

    row = pl.BlockSpec((ts, D), lambda i: (i, 0))
    vec = pl.BlockSpec((1, D), lambda i: (0, 0))
    return pl.pallas_call(
        body,
        name=name,
        grid=(S // ts,),
        in_specs=[row, vec, row, row],
        out_specs=[row, row, vec],
        out_shape=[jax.ShapeDtypeStruct((S, D), f32), jax.ShapeDtypeStruct((S, D), bf16), jax.ShapeDtypeStruct((1, D), f32)],
        compiler_params=pltpu.CompilerParams(dimension_semantics=("arbitrary",)),
    )(xv, g, dy, dres)


def _tail(h2, zg, pp, tgt, fn):
    S, D = h2.shape
    ts = _div(S, 256, SUBLANES)

    def body(h_ref, z_ref, p_ref, t_ref, fn_ref, dh_ref, dz_ref, dp_ref, loss_ref, dfn_ref):
        i = pl.program_id(0)
        sg = _sigmoid(z_ref[...])
        ppv = p_ref[...]
        h3 = h_ref[...] + sg * ppv
        r = lax.rsqrt(jnp.mean(h3 * h3, axis=-1, keepdims=True) + EPS)
        hn = h3 * r
        fnv = fn_ref[...]
        e = hn * fnv - t_ref[...]
        lpart = 0.5 * jnp.sum(jnp.mean(e * e, axis=-1, keepdims=True), axis=0, keepdims=True)
        dy = e * (1.0 / D)
        dpart = jnp.sum(dy * hn, axis=0, keepdims=True)

        @pl.when(i == 0)
        def _():
            loss_ref[...] = jnp.broadcast_to(lpart, loss_ref.shape)
            dfn_ref[...] = dpart

        @pl.when(i > 0)
        def _():
            loss_ref[...] += jnp.broadcast_to(lpart, loss_ref.shape)
            dfn_ref[...] += dpart

        t = dy * fnv
        dh3 = r * (t - hn * jnp.mean(t * hn, axis=-1, keepdims=True))
        dh_ref[...] = dh3
        dz_ref[...] = (dh3 * ppv * sg * (1.0 - sg)).astype(bf16)
        dp_ref[...] = (dh3 * sg).astype(bf16)

    row = pl.BlockSpec((ts, D), lambda i: (i, 0))
    vec = pl.BlockSpec((1, D), lambda i: (0, 0))
    one = pl.BlockSpec((1, LANES), lambda i: (0, 0))
    return pl.pallas_call(
        body,
        name="ple_final_loss",
        grid=(S // ts,),
        in_specs=[row, row, row, row, vec],
        out_specs=[row, row, row, one, vec],
        out_shape=[
            jax.ShapeDtypeStruct((S, D), f32),
            jax.ShapeDtypeStruct((S, D), bf16),
            jax.ShapeDtypeStruct((S, D), bf16),
            jax.ShapeDtypeStruct((1, LANES), f32),
            jax.ShapeDtypeStruct((1, D), f32),
        ],
        compiler_params=pltpu.CompilerParams(dimension_semantics=("arbitrary",)),
    )(h2, zg, pp, tgt, fn)


def _shift_down(v, halo, k):
    r = pltpu.roll(v, k, 0)
    hr = pltpu.roll(halo, k, 0)
    row = lax.broadcasted_iota(jnp.int32, hr.shape, 0)
    top = jnp.where(row < k, hr, r[0:SUBLANES])
    return top if v.shape[0] == SUBLANES else jnp.concatenate([top, r[SUBLANES:]], axis=0)


def _shift_up(v, halo, k):
    n = v.shape[0]
    r = pltpu.roll(v, n - k, 0)
    hr = pltpu.roll(halo, SUBLANES - k, 0)
    row = lax.broadcasted_iota(jnp.int32, hr.shape, 0)
    bot = jnp.where(row >= SUBLANES - k, hr, r[n - SUBLANES:])
    return jnp.concatenate([r[: n - SUBLANES], bot], axis=0)


def _conv_specs(S, F, tc, ts):
    nj = F // tc
    rows8 = ts // SUBLANES
    main = pl.BlockSpec((ts, 2 * tc), lambda j, i: (i, j))
    prev = pl.BlockSpec((SUBLANES, 2 * tc), lambda j, i: (jnp.maximum(i * rows8 - 1, 0), j))
    nxt = pl.BlockSpec((SUBLANES, 2 * tc), lambda j, i: (jnp.minimum((i + 1) * rows8, S // SUBLANES - 1), j))
    wg = pl.BlockSpec((3, tc), lambda j, i: (0, j))
    wv = pl.BlockSpec((3, tc), lambda j, i: (0, nj + j))
    bg = pl.BlockSpec((1, tc), lambda j, i: (0, j))
    bv = pl.BlockSpec((1, tc), lambda j, i: (0, nj + j))
    half = pl.BlockSpec((ts, tc), lambda j, i: (i, j))
    return nj, main, prev, nxt, wg, wv, bg, bv, half


def _conv_pre(u_ref, h_ref, wg_ref, wv_ref, bg_ref, bv_ref, tc):
    i = pl.program_id(1)
    u = u_ref[...]
    halo = jnp.where(i == 0, 0.0, h_ref[...])
    u1 = _shift_down(u, halo, 1)
    u2 = _shift_down(u, halo, 2)
    w = jnp.concatenate([wg_ref[...], wv_ref[...]], axis=1)
    b = jnp.concatenate([bg_ref[...], bv_ref[...]], axis=1)
    uc = b + w[0:1] * u2 + w[1:2] * u1 + w[2:3] * u
    return u, u1, u2, uc[:, :tc], uc[:, tc:]


def _convact_fwd(u, cw, cb, tc):
    S, F2 = u.shape
    F = F2 // 2
    ts = _div(S, 512, SUBLANES)
    nj, main, prev, nxt, wg, wv, bg, bv, half = _conv_specs(S, F, tc, ts)

    def body(u_ref, h_ref, wg_ref, wv_ref, bg_ref, bv_ref, m_ref):
        _, _, _, g, v = _conv_pre(u_ref, h_ref, wg_ref, wv_ref, bg_ref, bv_ref, tc)
        m_ref[...] = (g * _sigmoid(g) * v).astype(bf16)

    return pl.pallas_call(
        body,
        name="convact_fwd",
        grid=(nj, S // ts),
        in_specs=[main, prev, wg, wv, bg, bv],
        out_specs=half,
        out_shape=jax.ShapeDtypeStruct((S, F), bf16),
        compiler_params=pltpu.CompilerParams(dimension_semantics=("parallel", "parallel")),
    )(u, u, cw, cw, cb, cb)


def _convact_bwd(u, dm, cw, cb, tc, rider=None):
    S, F2 = u.shape
    F = F2 // 2
    ts = _div(S, 512, SUBLANES)
    nj, main, prev, nxt, wg, wv, bg, bv, half = _conv_specs(S, F, tc, ts)
    ni = S // ts
    half_nxt = pl.BlockSpec((SUBLANES, tc), lambda j, i: (jnp.minimum((i + 1) * (ts // SUBLANES), S // SUBLANES - 1), j))

    def act_bwd(g, v, d):
        sg = _sigmoid(g)
        return jnp.concatenate([d * v * sg * (1.0 + g * (1.0 - sg)), d * g * sg], axis=1)

    def body(u_ref, h_ref, n_ref, dm_ref, dmn_ref, wg_ref, wv_ref, bg_ref, bv_ref, du_ref, dwg_ref, dwv_ref, dbg_ref, dbv_ref):
        i = pl.program_id(1)
        u0, u1, u2, g, v = _conv_pre(u_ref, h_ref, wg_ref, wv_ref, bg_ref, bv_ref, tc)
        duc = act_bwd(g, v, dm_ref[...])
        w = jnp.concatenate([wg_ref[...], wv_ref[...]], axis=1)
        b = jnp.concatenate([bg_ref[...], bv_ref[...]], axis=1)
        un = n_ref[...]
        tail = u0[ts - SUBLANES:]
        ucn = b + w[0:1] * _shift_down(un, tail, 2) + w[1:2] * _shift_down(un, tail, 1) + w[2:3] * un
        ducn = jnp.where(i == ni - 1, 0.0, act_bwd(ucn[:, :tc], ucn[:, tc:], dmn_ref[...]))
        du_ref[...] = (w[2:3] * duc + w[1:2] * _shift_up(duc, ducn, 1) + w[0:1] * _shift_up(duc, ducn, 2)).astype(bf16)
        db = jnp.sum(duc, axis=0, keepdims=True)
        dw = jnp.concatenate(
            [jnp.sum(duc * u2, axis=0, keepdims=True), jnp.sum(duc * u1, axis=0, keepdims=True), jnp.sum(duc * u0, axis=0, keepdims=True)],
            axis=0,
        )

        @pl.when(i == 0)
        def _():
            dwg_ref[...] = dw[:, :tc]
            dwv_ref[...] = dw[:, tc:]
            dbg_ref[...] = db[:, :tc]
            dbv_ref[...] = db[:, tc:]

        @pl.when(i > 0)
        def _():
            dwg_ref[...] += dw[:, :tc]
            dwv_ref[...] += dw[:, tc:]
            dbg_ref[...] += db[:, :tc]
            dbv_ref[...] += db[:, tc:]

    w_out = pl.BlockSpec((3, tc), lambda j, i: (0, j))
    b_out = pl.BlockSpec((1, tc), lambda j, i: (0, j))
    return _grid_call(
        body,
        rider,
        name="convact_bwd",
        grid=(nj, ni),
        in_specs=[main, prev, nxt, half, half_nxt, wg, wv, bg, bv],
        out_specs=[main, w_out, w_out, b_out, b_out],
        out_shape=[
            jax.ShapeDtypeStruct((S, F2), bf16),
            jax.ShapeDtypeStruct((3, F), f32),
            jax.ShapeDtypeStruct((3, F), f32),
            jax.ShapeDtypeStruct((1, F), f32),
            jax.ShapeDtypeStruct((1, F), f32),
        ],
        scratch_shapes=[],
        args=(u, u, u, dm, dm, cw, cw, cb, cb),
    )


def _hg_gates(qp, fp, lbl):
    lb = _sigmoid(lbl[0:1] - lbl[1:2])
    sig = _sigmoid(fp)
    sigm = _sigmoid(-fp)
    f = lb + (1.0 - lb) * sig
    k = (1.0 - lb) * sigm
    sq = _sigmoid(qp)
    qf = qp * sq
    row = lax.broadcasted_iota(jnp.int32, (CHUNK, CHUNK), 0)
    col = lax.broadcasted_iota(jnp.int32, (CHUNK, CHUNK), 1)
    b = _dot((row >= col).astype(f32), jnp.log(f), NN, HIGHEST)
    return lb, sig, sigm, f, k, sq, qf, b


def _hg_block(qf, k, b, I):
    r0, n = I * HG_SUB, (I + 1) * HG_SUB
    base = b[r0 - 1:r0] if I > 0 else jnp.zeros_like(b[0:1])
    eq = jnp.exp(b[r0:n] - base)
    ek = jnp.exp(base - b[0:n])
    qt16 = (qf[r0:n] * eq).astype(bf16)
    kt16 = (k[0:n] * ek).astype(bf16)
    row = lax.broadcasted_iota(jnp.int32, (HG_SUB, n), 0) + r0
    col = lax.broadcasted_iota(jnp.int32, (HG_SUB, n), 1)
    return r0, n, eq, ek, qt16, kt16, col <= row


def _per_head(v, hp, fn):
    return jnp.concatenate([fn(v[:, p * HEAD_DIM:(p + 1) * HEAD_DIM]) for p in range(hp)], axis=1)


def _head_mean(v, hp):
    return _per_head(v, hp, lambda t: jnp.broadcast_to(jnp.mean(t, axis=-1, keepdims=True), t.shape))


def _pad_rows(v, rows):
    return v if v.shape[0] == rows else jnp.concatenate([v, jnp.zeros((rows - v.shape[0], v.shape[1]), v.dtype)], axis=0)


def _hgrn_specs(S, W, hp, reverse):
    nc = S // CHUNK
    ngrp = W // (hp * HEAD_DIM)
    cw = hp * HEAD_DIM
    cidx = (lambda c: nc - 1 - c) if reverse else (lambda c: c)

    def proj(off):
        return pl.BlockSpec((CHUNK, cw), lambda h, c: (cidx(c), off * ngrp + h))

    act = pl.BlockSpec((CHUNK, cw), lambda h, c: (cidx(c), h))
    lbl = pl.BlockSpec((2, cw), lambda h, c: (0, h))
    gn = pl.BlockSpec((1, HEAD_DIM), lambda h, c: (0, 0))
    st = pl.BlockSpec((hp, 1, HEAD_DIM, HEAD_DIM), lambda h, c: (h, cidx(c), 0, 0))
    return nc, ngrp, proj, act, lbl, gn, st


def _grid_call(body, rider, *, name, grid, in_specs, out_specs, out_shape, scratch_shapes, args):
    params = pltpu.CompilerParams(dimension_semantics=("arbitrary",) * len(grid))
    if rider is None:
        return pl.pallas_call(body, name=name, grid=grid, in_specs=in_specs, out_specs=out_specs, out_shape=out_shape,
                              scratch_shapes=scratch_shapes, compiler_params=params)(*args)
    n_in, n_out, n_scr = len(in_specs), len(out_specs), len(scratch_shapes)
    ri, ro = len(rider.ins), len(rider.out_shapes)

    def wrapped(*refs):
        a, b = n_in, n_in + ri
        c, d = b + n_out, b + n_out + ro
        e = d + n_scr
        ids = [pl.program_id(t) for t in range(len(grid))]
        is_first = functools.reduce(jnp.logical_and, [i == 0 for i in ids])
        is_last = functools.reduce(jnp.logical_and, [i == g - 1 for i, g in zip(ids, grid)])

        @pl.when(is_first)
        def _():
            rider.first(refs[a:b], refs[c:d], refs[e:])

        body(*refs[:a], *refs[b:c], *refs[d:e])

        @pl.when(is_last)
        def _():
            rider.last(refs[a:b], refs[c:d], refs[e:])

    return pl.pallas_call(
        wrapped,
        name=name,
        grid=grid,
        in_specs=list(in_specs) + [_hbm()] * ri,
        out_specs=list(out_specs) + [_hbm()] * ro,
        out_shape=list(out_shape) + list(rider.out_shapes),
        input_output_aliases={n_in + i: n_out + o for i, o in rider.aliases.items()},
        scratch_shapes=list(scratch_shapes) + list(rider.sems),
        compiler_params=params,
    )(*args, *rider.ins)


def _hgrn_fwd(proj, lb_logits, hg_norm, W, rider=None):
    S = proj.shape[0]
    hp = min(HG_HEADS_PER_STEP, W // HEAD_DIM)
    nc, ngrp, pspec, act, lbl_spec, gn_spec, st_spec = _hgrn_specs(S, W, hp, False)
    nb = CHUNK // HG_SUB

    def body(q_ref, f_ref, i_ref, g_ref, lbl_ref, gn_ref, y_ref, o_ref, st_ref, state):
        c = pl.program_id(1)

        @pl.when(c == 0)
        def _():
            state[...] = jnp.zeros_like(state)

        _, _, _, _, k, _, qf, b = _hg_gates(q_ref[...], f_ref[...], lbl_ref[...])
        v16 = i_ref[...].astype(bf16)
        a16 = (qf * jnp.exp(b)).astype(bf16)
        bl = b[CHUNK - 1:CHUNK]
        kd16 = (k * jnp.exp(bl - b)).astype(bf16)
        ebl = jnp.exp(bl)
        blocks = [_hg_block(qf, k, b, I) for I in range(nb)]
        heads = [slice(p * HEAD_DIM, (p + 1) * HEAD_DIM) for p in range(hp)]
        st_prev = state[...]
        st_ref[:, 0] = st_prev
        inter = [_dot(a16[:, sl], st_prev[p].astype(bf16), NT) for p, sl in enumerate(heads)]
        scores = [[_dot(qt16[:, sl], kt16[:, sl], NT) for sl in heads] for _, _, _, _, qt16, kt16, _ in blocks]
        intra = [[_dot(jnp.where(blk[6], sc, 0.0).astype(bf16), v16[0:blk[1], sl], NN) for sc, sl in zip(scs, heads)]
                 for blk, scs in zip(blocks, scores)]
        st_new = [st_prev[p] * ebl[:, sl] + _dot(v16[:, sl], kd16[:, sl], TN) for p, sl in enumerate(heads)]
        state[...] = jnp.stack(st_new)
        o = jnp.concatenate([inter[p] + jnp.concatenate([rows[p] for rows in intra], axis=0) for p in range(hp)], axis=1)
        o_ref[...] = o
        r = lax.rsqrt(_head_mean(o * o, hp) + EPS)
        gp = g_ref[...]
        y_ref[...] = (o * r * jnp.tile(gn_ref[...], (1, hp)) * (gp * _sigmoid(gp))).astype(bf16)

    H = W // HEAD_DIM
    return _grid_call(
        body,
        rider,
        name="hgrn_fwd",
        grid=(ngrp, nc),
        in_specs=[pspec(0), pspec(1), pspec(2), pspec(3), lbl_spec, gn_spec],
        out_specs=[act, act, st_spec],
        out_shape=[
            jax.ShapeDtypeStruct((S, W), bf16),
            jax.ShapeDtypeStruct((S, W), f32),
            jax.ShapeDtypeStruct((H, nc, HEAD_DIM, HEAD_DIM), f32),
        ],
        scratch_shapes=[pltpu.VMEM((hp, HEAD_DIM, HEAD_DIM), f32)],
        args=(proj, proj, proj, proj, lb_logits, hg_norm),
    )


def _hgrn_bwd(proj, o_hg, st, dycat, lb_logits, hg_norm, W, rider=None):
    S = proj.shape[0]
    hp = min(HG_HEADS_PER_STEP, W // HEAD_DIM)
    assert hp * HEAD_DIM == W, "the four gradients share one output block: all heads in one grid step"
    nc, ngrp, pspec, act, lbl_spec, gn_spec, st_spec = _hgrn_specs(S, W, hp, True)
    nb = CHUNK // HG_SUB

    def body(q_ref, f_ref, i_ref, g_ref, o_ref, st_ref, dy_ref, lbl_ref, gn_ref,
             d4_ref, dlbl_ref, dgn_ref, dstate):
        cw_ = hp * HEAD_DIM
        dq_ref, df_ref, di_ref, dg_ref = (d4_ref.at[:, pl.ds(t * cw_, cw_)] for t in range(4))
        h = pl.program_id(0)
        c = pl.program_id(1)

        @pl.when(c == 0)
        def _():
            dstate[...] = jnp.zeros_like(dstate)
            dlbl_ref[...] = jnp.zeros_like(dlbl_ref)

        @pl.when((c == 0) & (h == 0))
        def _():
            dgn_ref[...] = jnp.zeros_like(dgn_ref)

        row = lax.broadcasted_iota(jnp.int32, (CHUNK, CHUNK), 0)
        col = lax.broadcasted_iota(jnp.int32, (CHUNK, CHUNK), 1)
        upper = (col >= row).astype(f32)
        last_row = lax.broadcasted_iota(jnp.int32, (CHUNK, hp * HEAD_DIM), 0) == CHUNK - 1

        qp, fp = q_ref[...], f_ref[...]
        lb, sig, sigm, f, k, sq, qf, b = _hg_gates(qp, fp, lbl_ref[...])
        v16 = i_ref[...].astype(bf16)
        gnv = jnp.tile(gn_ref[...], (1, hp))
        gp = g_ref[...]
        sgg = _sigmoid(gp)
        gate = gp * sgg
        o = o_ref[...]
        r = lax.rsqrt(_head_mean(o * o, hp) + EPS)
        on = o * r
        dy = dy_ref[...]
        dg_ref[...] = (dy * on * gnv * (sgg * (1.0 + gp * (1.0 - sgg)))).astype(bf16)
        dgn_wide = jnp.sum(dy * on * gate, axis=0, keepdims=True)
        dgn_ref[...] += functools.reduce(jnp.add, [dgn_wide[:, p * HEAD_DIM:(p + 1) * HEAD_DIM] for p in range(hp)])
        don = dy * gnv * gate
        do16 = (r * (don - on * _head_mean(don * on, hp))).astype(bf16)
        eb = jnp.exp(b)
        A = qf * eb
        a16 = A.astype(bf16)
        bl = b[CHUNK - 1:CHUNK]
        ebl = jnp.exp(bl)
        ekd = jnp.exp(bl - b)
        kd = k * ekd
        kd16 = kd.astype(bf16)
        blocks = [_hg_block(qf, k, b, I) for I in range(nb)]

        heads = [slice(p * HEAD_DIM, (p + 1) * HEAD_DIM) for p in range(hp)]
        st_prev = st_ref[:, 0]
        dst_new = dstate[...]
        st16, dst16 = st_prev.astype(bf16), dst_new.astype(bf16)
        dA_h = [_dot(do16[:, sl], st16[p], NN) for p, sl in enumerate(heads)]
        dkd_h = [_dot(v16[:, sl], dst16[p], NN) for p, sl in enumerate(heads)]
        dv_h = [_dot(kd16[:, sl], dst16[p], NT) for p, sl in enumerate(heads)]
        dstate[...] = jnp.stack([dst_new[p] * ebl[:, sl] + _dot(do16[:, sl], a16[:, sl], TN) for p, sl in enumerate(heads)])
        dbl_h = [jnp.sum(dst_new[p] * st_prev[p], axis=0, keepdims=True) for p in range(hp)]
        sc_h = [[jnp.where(mask, _dot(qt16[:, sl], kt16[:, sl], NT), 0.0).astype(bf16) for sl in heads]
                for _, _, _, _, qt16, kt16, mask in blocks]
        dsc_h = [[jnp.where(mask, _dot(do16[r0:n, sl], v16[0:n, sl], NT), 0.0).astype(bf16) for sl in heads]
                 for r0, n, _, _, _, _, mask in blocks]
        dvi_h = [[_dot(sc, do16[blk[0]:blk[1], sl], TN) for sc, sl in zip(scs, heads)] for blk, scs in zip(blocks, sc_h)]
        dqt_h = [[_dot(dsc, blk[5][:, sl], NN) for dsc, sl in zip(dscs, heads)] for blk, dscs in zip(blocks, dsc_h)]
        dkt_h = [[_dot(dsc, blk[4][:, sl], TN) for dsc, sl in zip(dscs, heads)] for blk, dscs in zip(blocks, dsc_h)]
        dv_h = [functools.reduce(jnp.add, [dv_h[p]] + [_pad_rows(rows[p], CHUNK) for rows in dvi_h]) for p in range(hp)]

        dA, dkd = jnp.concatenate(dA_h, axis=1), jnp.concatenate(dkd_h, axis=1)
        dbl = jnp.concatenate(dbl_h, axis=1) * ebl + jnp.sum(dkd * kd, axis=0, keepdims=True)
        db = dA * A - dkd * kd + jnp.where(last_row, dbl, 0.0)
        dk = dkd * ekd
        dq_rows, db_rows = [], []
        for I, (r0, n, eq, ek, qt16, kt16, _) in enumerate(blocks):
            dqt, dkt = jnp.concatenate(dqt_h[I], axis=1), jnp.concatenate(dkt_h[I], axis=1)
            dq_rows.append(dqt * eq)
            dk = dk + _pad_rows(dkt * ek, CHUNK)
            db_rows.append(dqt * qt16.astype(f32))
            db = db - _pad_rows(dkt * kt16.astype(f32), CHUNK)
        dqf = dA * eb + jnp.concatenate(dq_rows, axis=0)
        db = db + jnp.concatenate(db_rows, axis=0)
        dlogf = _dot(upper, db, NN, HIGHEST)
        dfg = dlogf / f
        df_ref[...] = ((1.0 - lb) * sig * sigm * (dfg - dk)).astype(bf16)
        dlb = jnp.sum(sigm * (dfg - dk), axis=0, keepdims=True)
        dl0 = dlb * lb * (1.0 - lb)
        dlbl_ref[...] += jnp.concatenate([dl0, -dl0], axis=0)
        dq_ref[...] = (dqf * (sq * (1.0 + qp * (1.0 - sq)))).astype(bf16)
        di_ref[...] = jnp.concatenate(dv_h, axis=1).astype(bf16)

    cw = hp * HEAD_DIM
    dy_spec = pl.BlockSpec((CHUNK, cw), lambda h, c: (nc - 1 - c, h))
    return _grid_call(
        body,
        rider,
        name="hgrn_bwd",
        grid=(ngrp, nc),
        in_specs=[pspec(0), pspec(1), pspec(2), pspec(3), act, st_spec, dy_spec, lbl_spec, gn_spec],
        out_specs=[pl.BlockSpec((CHUNK, 4 * W), lambda h, c: (nc - 1 - c, 0)), lbl_spec, gn_spec],
        out_shape=[jax.ShapeDtypeStruct((S, 4 * W), bf16), jax.ShapeDtypeStruct((2, W), f32), jax.ShapeDtypeStruct((1, HEAD_DIM), f32)],
        scratch_shapes=[pltpu.VMEM((hp, HEAD_DIM, HEAD_DIM), f32)],
        args=(proj, proj, proj, proj, o_hg, st, dycat, lb_logits, hg_norm),
    )


def _att_dims(S):
    qt = ATT_Q_ROWS if S % ATT_Q_ROWS == 0 else CHUNK
    pad = LEFT_CHUNKS * CHUNK
    kb = pad + qt
    return qt, pad, kb, kb + qt


def _att_scores(q16, kb16, bias, start, qt, pad, kb):
    s = _dot(q16, kb16, NT) * (HEAD_DIM ** -0.5) + bias
    row = lax.broadcasted_iota(jnp.int32, (qt, kb), 0)
    col = lax.broadcasted_iota(jnp.int32, (qt, kb), 1)
    lo = jnp.bitwise_and(row, -CHUNK)
    ok = (col >= lo) & (col < lo + pad + CHUNK) & (col + start >= pad)
    s = jnp.where(ok, s, NEG)
    e = jnp.exp(s - jnp.max(s, axis=-1, keepdims=True))
    return e / jnp.sum(e, axis=-1, keepdims=True)


def _att_bias(ext_ref, qt, kb, ne):
    e = jnp.broadcast_to(ext_ref[0], (qt, ne))
    return pltpu.roll(e, ne - qt + 1, 1, stride=1, stride_axis=0)[:, :kb]


def _attn_fwd(proj, kp, vp, ext, W, rider=None):
    S = proj.shape[0]
    H = W // HEAD_DIM
    qt, pad, kb, ne = _att_dims(S)

    def body(q_ref, k_ref, v_ref, ext_ref, o_ref, bias):
        g = pl.program_id(1)

        @pl.when(g == 0)
        def _():
            bias[...] = _att_bias(ext_ref, qt, kb, ne)

        start = pl.multiple_of(g * qt, qt)
        pn = _att_scores(q_ref[...].astype(bf16), k_ref[pl.ds(start, kb), :], bias[...], start, qt, pad, kb)
        o_ref[...] = _dot(pn.astype(bf16), v_ref[pl.ds(start, kb), :], NN).astype(bf16)

    outs = _grid_call(
        body,
        rider,
        name="attn_fwd",
        grid=(H, S // qt),
        in_specs=[
            pl.BlockSpec((qt, HEAD_DIM), lambda h, g: (g, 4 * H + h)),
            pl.BlockSpec((S + pad, HEAD_DIM), lambda h, g: (0, h)),
            pl.BlockSpec((S + pad, HEAD_DIM), lambda h, g: (0, h)),
            pl.BlockSpec((1, 1, ne), lambda h, g: (h, 0, 0)),
        ],
        out_specs=[pl.BlockSpec((qt, HEAD_DIM), lambda h, g: (g, h))],
        out_shape=[jax.ShapeDtypeStruct((S, W), bf16)],
        scratch_shapes=[pltpu.VMEM((qt, kb), f32)],
        args=(proj, kp, vp, ext),
    )
    return outs[0] if rider is None else outs


def _attn_bwd(proj, kp, vp, ext, dycat, W, rider=None):
    S = proj.shape[0]
    H = W // HEAD_DIM
    qt, pad, kb, ne = _att_dims(S)
    scale = HEAD_DIM ** -0.5
    ng = S // qt

    def body(q_ref, k_ref, v_ref, ext_ref, do_ref, dq_ref, dko_ref, dvo_ref, db_ref, bias, dk_ref, dv_ref):
        g = pl.program_id(1)

        @pl.when(g == 0)
        def _():
            bias[...] = _att_bias(ext_ref, qt, kb, ne)
            dk_ref[...] = jnp.zeros_like(dk_ref)
            dv_ref[...] = jnp.zeros_like(dv_ref)
            db_ref[...] = jnp.zeros_like(db_ref)

        start = pl.multiple_of(g * qt, qt)
        q16 = q_ref[...].astype(bf16)
        kb16 = k_ref[pl.ds(start, kb), :]
        vb16 = v_ref[pl.ds(start, kb), :]
        pn = _att_scores(q16, kb16, bias[...], start, qt, pad, kb)
        do16 = do_ref[...].astype(bf16)
        dpn = _dot(do16, vb16, NT)
        dv_ref[pl.ds(start, kb), :] += _dot(pn.astype(bf16), do16, TN)
        ds = pn * (dpn - jnp.sum(dpn * pn, axis=-1, keepdims=True))
        db_ref[0] += ds
        ds16 = ds.astype(bf16)
        dq_ref[...] = (_dot(ds16, kb16, NN) * scale).astype(bf16)
        dk_ref[pl.ds(start, kb), :] += _dot(ds16, q16, TN) * scale

        @pl.when(g == ng - 1)
        def _():
            dko_ref[...] = dk_ref[pl.ds(pad, S), :].astype(bf16)
            dvo_ref[...] = dv_ref[pl.ds(pad, S), :].astype(bf16)

    kv_spec = pl.BlockSpec((S + pad, HEAD_DIM), lambda h, g: (0, h))
    kv_out = pl.BlockSpec((S, HEAD_DIM), lambda h, g: (0, h))
    return _grid_call(
        body,
        rider,
        name="attn_bwd",
        grid=(H, S // qt),
        in_specs=[
            pl.BlockSpec((qt, HEAD_DIM), lambda h, g: (g, 4 * H + h)),
            kv_spec,
            kv_spec,
            pl.BlockSpec((1, 1, ne), lambda h, g: (h, 0, 0)),
            pl.BlockSpec((qt, HEAD_DIM), lambda h, g: (g, H + h)),
        ],
        out_specs=[
            pl.BlockSpec((qt, HEAD_DIM), lambda h, g: (g, h)),
            kv_out,
            kv_out,
            pl.BlockSpec((1, qt, kb), lambda h, g: (h, 0, 0)),
        ],
        out_shape=[
            jax.ShapeDtypeStruct((S, W), bf16),
            jax.ShapeDtypeStruct((S, W), bf16),
            jax.ShapeDtypeStruct((S, W), bf16),
            jax.ShapeDtypeStruct((H, qt, kb), f32),
        ],
        scratch_shapes=[pltpu.VMEM((qt, kb), f32), pltpu.VMEM((S + pad, HEAD_DIM), f32), pltpu.VMEM((S + pad, HEAD_DIM), f32)],
        args=(proj, kp, vp, ext, dycat),
    )


def _bias_onehot(qt, ne, nrel_pad):
    m = lax.broadcasted_iota(jnp.int32, (nrel_pad, ne), 1)
    r = lax.broadcasted_iota(jnp.int32, (nrel_pad, ne), 0)
    rel = LEFT_CHUNKS * CHUNK + qt - 1 - m
    hot = (r == jnp.clip(rel, -REL_CLIP, REL_CLIP) + REL_CLIP) & (m < ne - 1)
    return hot.astype(f32)


def _bias_ext(rel_bias_pad, onehot):
    H, _, nr = rel_bias_pad.shape
    ne = onehot.shape[1]

    def body(rb_ref, oh_ref, o_ref):
        o_ref[0] = _dot(rb_ref[0], oh_ref[...], NN, HIGHEST)

    return pl.pallas_call(
        body,
        name="bias_ext",
        grid=(H,),
        in_specs=[pl.BlockSpec((1, 1, nr), lambda h: (h, 0, 0)), pl.BlockSpec((nr, ne), lambda h: (0, 0))],
        out_specs=pl.BlockSpec((1, 1, ne), lambda h: (h, 0, 0)),
        out_shape=jax.ShapeDtypeStruct((H, 1, ne), f32),
        compiler_params=pltpu.CompilerParams(dimension_semantics=("parallel",)),
    )(rel_bias_pad, onehot)


def _bias_bwd(dbias, onehot):
    H, qt, kb = dbias.shape
    nr, ne = onehot.shape

    def body(d_ref, oh_ref, o_ref):
        row = lax.broadcasted_iota(jnp.int32, (qt, qt), 0)
        col = lax.broadcasted_iota(jnp.int32, (qt, qt), 1)
        flip = (row + col == qt - 1).astype(f32)
        x = _dot(flip, d_ref[0], NN, HIGHEST)
        x = jnp.concatenate([x, jnp.zeros((qt, ne - kb), f32)], axis=1)
        de = jnp.sum(pltpu.roll(x, 0, 1, stride=1, stride_axis=0), axis=0, keepdims=True)
        o_ref[0] = _dot(de, oh_ref[...], NT, HIGHEST)

    return pl.pallas_call(
        body,
        name="bias_bwd",
        grid=(H,),
        in_specs=[pl.BlockSpec((1, qt, kb), lambda h: (h, 0, 0)), pl.BlockSpec((nr, ne), lambda h: (0, 0))],
        out_specs=pl.BlockSpec((1, 1, nr), lambda h: (h, 0, 0)),
        out_shape=jax.ShapeDtypeStruct((H, 1, nr), f32),
        compiler_params=pltpu.CompilerParams(dimension_semantics=("parallel",)),
    )(dbias, onehot)


def _place():
    x, y, c = lax.axis_index("x"), lax.axis_index("y"), lax.axis_index("c")
    return x, y, c


def _flip(x, y, k):
    return (1 - x if k & 2 else x), (1 - y if k & 1 else y)


def _region(ref, shard_shape, axis, j, half):
    R, C = shard_shape
    if axis == 0:
        if half is None:
            return ref.at[pl.ds(j * R, R), :]
        return ref.at[pl.ds(j * R + half * (R // 2), R // 2), :]
    if half is None:
        return ref.at[:, pl.ds(j * C, C)]
    return ref.at[pl.ds(half * (R // 2), R // 2), pl.ds(j * C, C)]


def _remote(src, dst, send_sem, recv_sem, dev):
    return pltpu.make_async_remote_copy(src_ref=src, dst_ref=dst, send_sem=send_sem, recv_sem=recv_sem,
                                        device_id=dev, device_id_type=MESH)


def _cast_place(shard, axis, where, *, name):
    R, C = shard.shape
    tr = _div(R, 256, 2 * SUBLANES)
    nr = R // tr
    full = (N_CHIPS * R, C) if axis == 0 else (R, N_CHIPS * C)
    omap = (lambda r, s: (s[0] * nr + r, 0)) if axis == 0 else (lambda r, s: (r, s[0]))

    def body(s_ref, x_ref, o_ref):
        o_ref[...] = x_ref[...].astype(bf16)

    return pl.pallas_call(
        body,
        name=name,
        grid_spec=pltpu.PrefetchScalarGridSpec(
            num_scalar_prefetch=1,
            grid=(nr,),
            in_specs=[pl.BlockSpec((tr, C), lambda r, s: (r, 0))],
            out_specs=pl.BlockSpec((tr, C), omap),
        ),
        out_shape=jax.ShapeDtypeStruct(full, bf16),
        compiler_params=pltpu.CompilerParams(dimension_semantics=("parallel",)),
    )(where, shard)


class _Rider(NamedTuple):
    ins: list
    out_shapes: list
    aliases: dict
    sems: list
    first: Callable
    last: Callable


def _run_rider(rider, *, name):
    ni, no = len(rider.ins), len(rider.out_shapes)

    def body(*refs):
        ins, outs, sems = refs[:ni], refs[ni:ni + no], refs[ni + no:]
        rider.first(ins, outs, sems)
        rider.last(ins, outs, sems)

    return pl.pallas_call(
        body,
        name=name,
        in_specs=[_hbm()] * ni,
        out_specs=[_hbm()] * no,
        out_shape=rider.out_shapes,
        input_output_aliases=rider.aliases,
        scratch_shapes=rider.sems,
    )(*rider.ins)


def _gather_rider(fulls, shard_shapes, axes, splits, rels):
    n = len(fulls)

    def stage1(ins, outs, sems):
        x, y, c = _place()
        jo = 2 * x + y
        cps = []
        for m in range(n):
            half = c if splits[m] else None
            for k in rels[m]:
                px, py = _flip(x, y, k)
                cps.append(_remote(_region(ins[m], shard_shapes[m], axes[m], jo, half), _region(outs[m], shard_shapes[m], axes[m], jo, half),
                                   sems[0].at[3 * m + k - 1], sems[1].at[3 * m + k - 1], (px, py, c)))
        return cps

    def first(ins, outs, sems):
        for cp in stage1(ins, outs, sems):
            cp.start()

    def last(ins, outs, sems):
        x, y, c = _place()
        send1, recv1, send2, recv2 = sems
        forwards = []
        for m in range(n):
            half = c if splits[m] else None
            for k in rels[m]:
                px, py = _flip(x, y, k)
                land = _region(outs[m], shard_shapes[m], axes[m], 2 * px + py, half)
                _remote(land, land, send1.at[3 * m + k - 1], recv1.at[3 * m + k - 1], (x, y, c)).wait_recv()
                if splits[m]:
                    fw = _remote(land, land, send2.at[3 * m + k - 1], recv2.at[3 * m + k - 1], (x, y, 1 - c))
                    fw.start()
                    forwards.append(fw)
        for m in range(n):
            if not splits[m]:
                continue
            for k in rels[m]:
                px, py = _flip(x, y, k)
                land = _region(outs[m], shard_shapes[m], axes[m], 2 * px + py, 1 - c)
                _remote(land, land, send2.at[3 * m + k - 1], recv2.at[3 * m + k - 1], (x, y, c)).wait_recv()
        for cp in stage1(ins, outs, sems) + forwards:
            cp.wait_send()

    sems = pltpu.SemaphoreType.DMA((3 * n,))
    return _Rider(list(fulls), [jax.ShapeDtypeStruct(f.shape, f.dtype) for f in fulls], {m: m for m in range(n)},
                  [sems, sems, sems, sems], first, last)


def _chips_rider(ts):
    n = len(ts)

    def copies(ins, outs, sems):
        x, y, c = _place()
        cps = []
        for m in range(n):
            for k in (1, 2, 3):
                px, py = _flip(x, y, k)
                cps.append(_remote(ins[m].at[2 * px + py], outs[m].at[k - 1], sems[0].at[3 * m + k - 1], sems[1].at[3 * m + k - 1], (px, py, c)))
        return cps

    def first(ins, outs, sems):
        for cp in copies(ins, outs, sems):
            cp.start()

    def last(ins, outs, sems):
        cps = copies(ins, outs, sems)
        for cp in cps:
            cp.wait_recv()
        for cp in cps:
            cp.wait_send()

    sems = pltpu.SemaphoreType.DMA((3 * n,))
    return _Rider(list(ts), [jax.ShapeDtypeStruct((3,) + t.shape[1:], t.dtype) for t in ts], {}, [sems, sems], first, last)


def _rs_sibling(dws, shard_shapes, axes, *, name):
    n = len(dws)

    def body(*refs):
        ins, got = refs[:n], refs[n:2 * n]
        send, recv = refs[2 * n:]
        x, y, c = _place()
        cps = []
        for m in range(n):
            for j in range(N_CHIPS):
                cp = _remote(_region(ins[m], shard_shapes[m], axes[m], j, 1 - c), got[m].at[j],
                             send.at[N_CHIPS * m + j], recv.at[N_CHIPS * m + j], (x, y, 1 - c))
                cp.start()
                cps.append(cp)
        for cp in cps:
            cp.wait_recv()
        for cp in cps:
            cp.wait_send()

    sems = pltpu.SemaphoreType.DMA((N_CHIPS * n,))
    return pl.pallas_call(
        body,
        name=name,
        in_specs=[_hbm()] * n,
        out_specs=[_hbm()] * n,
        out_shape=[jax.ShapeDtypeStruct((N_CHIPS, s[0] // 2, s[1]), d.dtype) for s, d in zip(shard_shapes, dws)],
        scratch_shapes=[sems, sems],
    )(*dws)


def _pair_sum(dw, got, shard_shape, axis, where, *, name):
    R, C = shard_shape
    hr = R // 2
    tr = _div(hr, 256, 2 * SUBLANES)
    nr = hr // tr
    dmap = (lambda j, r, s: ((2 * j + s[1]) * nr + r, 0)) if axis == 0 else (lambda j, r, s: (s[1] * nr + r, j))
    slot = pl.BlockSpec((1, tr, C), lambda j, r, s: (j, r, 0))

    def body(s_ref, d_ref, g_ref, o_ref):
        o_ref[0] = (d_ref[...].astype(f32) + g_ref[0].astype(f32)).astype(bf16)

    return pl.pallas_call(
        body,
        name=name,
        grid_spec=pltpu.PrefetchScalarGridSpec(
            num_scalar_prefetch=1,
            grid=(N_CHIPS, nr),
            in_specs=[pl.BlockSpec((tr, C), dmap), slot],
            out_specs=slot,
        ),
        out_shape=jax.ShapeDtypeStruct((N_CHIPS, hr, C), bf16),
        compiler_params=pltpu.CompilerParams(dimension_semantics=("parallel", "parallel")),
    )(where, dw, got)


def _chip_sum(pair, others, where, *, name):
    _, hr, C = pair.shape
    tr = _div(hr, 256, 2 * SUBLANES)
    nr = hr // tr

    def body(s_ref, p_ref, o_ref, g_ref):
        g_ref[...] = p_ref[0].astype(f32) + o_ref[0].astype(f32) + o_ref[1].astype(f32) + o_ref[2].astype(f32)

    return pl.pallas_call(
        body,
        name=name,
        grid_spec=pltpu.PrefetchScalarGridSpec(
            num_scalar_prefetch=1,
            grid=(nr,),
            in_specs=[pl.BlockSpec((1, tr, C), lambda r, s: (s[0], r, 0)), pl.BlockSpec((3, tr, C), lambda r, s: (0, r, 0))],
            out_specs=pl.BlockSpec((tr, C), lambda r, s: (s[1] * nr + r, 0)),
        ),
        out_shape=jax.ShapeDtypeStruct((2 * hr, C), f32),
        compiler_params=pltpu.CompilerParams(dimension_semantics=("parallel",)),
    )(where, pair, others)


def _rs_join(gs):
    n = len(gs)

    def body(*refs):
        ins, outs = refs[:n], refs[n:2 * n]
        send, recv = refs[2 * n:]
        x, y, c = _place()
        cps = []
        for m in range(n):
            hr = gs[m].shape[0] // 2
            cp = _remote(ins[m].at[pl.ds(c * hr, hr), :], outs[m].at[pl.ds(c * hr, hr), :], send.at[m], recv.at[m], (x, y, 1 - c))
            cp.start()
            cps.append(cp)
        for cp in cps:
            cp.wait_recv()
        for cp in cps:
            cp.wait_send()

    sems = pltpu.SemaphoreType.DMA((n,))
    return pl.pallas_call(
        body,
        name="grads_join_halves",
        in_specs=[_hbm()] * n,
        out_specs=[_hbm()] * n,
        out_shape=[jax.ShapeDtypeStruct(g.shape, g.dtype) for g in gs],
        input_output_aliases={m: m for m in range(n)},
        scratch_shapes=[sems, sems],
    )(*gs)


def _allreduce_small(v):
    rows = v.shape[0]

    def body(v_ref, o_ref, gath, send, recv):
        x, y, c = _place()
        me = 4 * x + 2 * y + c
        gath[pl.ds(me, 1)] = v_ref[...][None]
        cps = []
        for k in range(1, N_DEV):
            peer = (1 - x if k & 4 else x, 1 - y if k & 2 else y, 1 - c if k & 1 else c)
            cp = _remote(v_ref, gath.at[me], send.at[k - 1], recv.at[k - 1], peer)
            cp.start()
            cps.append(cp)
        for cp in cps:
            cp.wait_recv()
        for cp in cps:
            cp.wait_send()
        acc = gath[0]
        for d in range(1, N_DEV):
            acc = acc + gath[d]
        o_ref[...] = acc

    return pl.pallas_call(
        body,
        name="small_allreduce",
        in_specs=[_vmem()],
        out_specs=_vmem(),
        out_shape=jax.ShapeDtypeStruct(v.shape, f32),
        scratch_shapes=[pltpu.VMEM((N_DEV, rows, LANES), f32), pltpu.SemaphoreType.DMA((N_DEV - 1,)), pltpu.SemaphoreType.DMA((N_DEV - 1,))],
    )(v)


def _adamw_math(w, g, m, v):
    m = ADAM_B1 * m + (1.0 - ADAM_B1) * g
    v = ADAM_B2 * v + (1.0 - ADAM_B2) * (g * g)
    m_hat = m / (1.0 - ADAM_B1 ** ADAM_STEP)
    v_hat = v / (1.0 - ADAM_B2 ** ADAM_STEP)
    delta = -ADAM_LR * (m_hat / (jnp.sqrt(v_hat) + ADAM_EPS) + ADAM_WD * w)
    return delta, m, v


def _adamw(w, g, m, v, *, name):
    R, C = w.shape
    tr = _div(R, 128, SUBLANES)

    def body(w_ref, g_ref, m_ref, v_ref, d_ref, mo_ref, vo_ref):
        d, mn, vn = _adamw_math(w_ref[...], g_ref[...], m_ref[...], v_ref[...])
        d_ref[...] = d
        mo_ref[...] = mn
        vo_ref[...] = vn

    spec = pl.BlockSpec((tr, C), lambda r: (r, 0))
    return pl.pallas_call(
        body,
        name=name,
        grid=(R // tr,),
        in_specs=[spec] * 4,
        out_specs=[spec] * 3,
        out_shape=[jax.ShapeDtypeStruct((R, C), f32)] * 3,
        compiler_params=pltpu.CompilerParams(dimension_semantics=("parallel",)),
    )(w, g, m, v)


def _pack(arrs):
    flat = jnp.concatenate([a.reshape(-1).astype(f32) for a in arrs])
    tile = SUBLANES * LANES
    total = -(-flat.shape[0] // tile) * tile
    return jnp.pad(flat, (0, total - flat.shape[0])).reshape(total // LANES, LANES)


def _unpack(buf, shapes):
    flat = buf.reshape(-1)
    out, off = [], 0
    for s in shapes:
        size = int(np.prod(s))
        out.append(flat[off:off + size].reshape(s))
        off += size
    return out


def kernel(x, p, norm_mix, w_in, lb_logits, hg_norm, rel_bias, w_out, norm_ffn, w_up, conv_w, conv_b, w_down, norm_ple, w_ple_gate, w_ple_proj, final_norm, loss_target, m_norm_mix, m_w_in, m_lb_logits, m_hg_norm, m_rel_bias, m_w_out, m_norm_ffn, m_w_up, m_conv_w, m_conv_b, m_w_down, m_norm_ple, m_w_ple_gate, m_w_ple_proj, m_final_norm, v_norm_mix, v_w_in, v_lb_logits, v_hg_norm, v_rel_bias, v_w_out, v_norm_ffn, v_w_up, v_conv_w, v_conv_b, v_w_down, v_norm_ple, v_w_ple_gate, v_w_ple_proj, v_final_norm):
    S, D = x.shape[1], x.shape[2]
    xv, pv, tgt = x[0], p[0, 0], loss_target[0]
    W = (w_in.shape[2] * N_CHIPS) // 7
    H = W // HEAD_DIM
    F = w_down.shape[1] * N_CHIPS
    tc = _div(F // 2, 1408, LANES)
    nj = F // tc
    jx, jy = lax.axis_index("x"), lax.axis_index("y")
    chip = 2 * jx + jy

    big = [w_in[0], w_out[0], w_up[0], w_down[0], w_ple_gate[0], w_ple_proj[0]]
    big_axes = [1, 0, 1, 0, 0, 1]
    where = jnp.stack([chip, lax.axis_index("c")]).astype(jnp.int32)
    cw_pad = jnp.pad(conv_w[0], ((0, SUBLANES - conv_w.shape[1]), (0, 0)))
    cw_mine = lax.dynamic_update_slice(jnp.zeros((N_CHIPS * SUBLANES, cw_pad.shape[1]), f32), cw_pad, (chip * SUBLANES, 0))
    placed = [_cast_place(b, ax, where, name=f"cast_place_{i}") for i, (b, ax) in enumerate(zip(big, big_axes))]
    shard_shapes = [b.shape for b in big]

    every = (1, 2, 3)

    def gather(items):
        return _gather_rider([b for b, _, _ in items],
                             [cw_pad.shape if i is None else shard_shapes[i] for _, i, _ in items],
                             [0 if i is None else big_axes[i] for _, i, _ in items],
                             [i is not None for _, i, _ in items], [r for _, _, r in items])

    (W_in,) = _run_rider(gather([(placed[0], 0, every)]), name="gather_w_in")

    tm = _div(S, 1024, LANES)

    a1 = _rms_fwd(xv, norm_mix, name="rms_mix")
    proj, W_up_part = _mm(a1, W_in, dims="nn", tm=tm, tn=_div(7 * W, 512, LANES), tk=D, out_dtype=f32, name="mm_in",
                          rider=gather([(placed[2], 2, (1, 2))]))
    y_hg, o_hg, st, W_out = _hgrn_fwd(proj, lb_logits, hg_norm, W, rider=gather([(placed[1], 1, every)]))

    qt, pad, kb, ne = _att_dims(S)
    nrel = rel_bias.shape[2]
    nrel_pad = -(-nrel // LANES) * LANES
    onehot = _bias_onehot(qt, ne, nrel_pad)
    rb_pad = jnp.pad(rel_bias[0], ((0, 0), (0, nrel_pad - nrel)))[:, None, :]
    ext = _bias_ext(rb_pad, onehot)
    kp = jnp.pad(proj[:, 5 * W:6 * W].astype(bf16), ((pad, 0), (0, 0)))
    vp = jnp.pad(proj[:, 6 * W:7 * W].astype(bf16), ((pad, 0), (0, 0)))
    y_att, W_up = _attn_fwd(proj, kp, vp, ext, W, rider=gather([(W_up_part, 2, (3,))]))
    ycat = jnp.concatenate([y_hg, y_att], axis=1)

    tn_d = _div(D, 512, LANES)
    h1, W_pg, W_pp, cw_all = _mm(ycat, W_out, dims="nn", tm=tm, tn=tn_d, tk=D, out_dtype=f32, res=xv, name="mm_out",
                                 rider=gather([(placed[4], 4, every), (placed[5], 5, every), (cw_mine, None, every)]))
    a2 = _rms_fwd(h1, norm_ffn, name="rms_ffn")
    perm = lambda b: (b % 2) * nj + b // 2
    u, W_down = _mm(a2, W_up, dims="nn", tm=tm, tn=tc, tk=D, out_dtype=f32, name="mm_up", b_idx=lambda i, j, k: (k, perm(j)),
                    rider=gather([(placed[3], 3, every)]))
    ncw = conv_w.shape[1]
    cw_full = jnp.transpose(cw_all.reshape(N_CHIPS, SUBLANES, -1)[:, :ncw], (1, 0, 2)).reshape(ncw, -1)
    cb = conv_b
    mact = _convact_fwd(u, cw_full, cb, tc)
    tn_w = _div(D, 1024, LANES)
    h2 = _mm(mact, W_down, dims="nn", tm=_div(S, 512, LANES), tn=tn_w, tk=F, out_dtype=f32, res=h1, name="mm_down", cols_outer=True)
    a3 = _rms_fwd(h2, norm_ple, name="rms_ple")
    zg = _mm(a3, W_pg, dims="nn", tm=tm, tn=tn_d, tk=D, out_dtype=f32, name="mm_ple_gate")
    p16 = pv.astype(bf16)
    pp = _mm(p16, W_pp, dims="nn", tm=tm, tn=tn_d, tk=pv.shape[1], out_dtype=f32, name="mm_ple_proj")
    dh3, dzg, dpp, loss_part, d_fn = _tail(h2, zg, pp, tgt, final_norm.reshape(1, D))

    pair, others = {}, {}

    def reduce_start(idx, grads):
        got = _rs_sibling(grads, [shard_shapes[i] for i in idx], [big_axes[i] for i in idx], name="grads_to_sibling_" + "_".join(map(str, idx)))
        for i, d, g in zip(idx, grads, got):
            pair[i] = _pair_sum(d, g, shard_shapes[i], big_axes[i], where, name=f"grad_pair_sum_{i}")
        return _chips_rider([pair[i] for i in idx])

    def reduce_landed(idx, landed):
        for i, o in zip(idx, landed):
            others[i] = o

    tk_s = S
    dW_pp = _mm(p16, dpp, dims="tn", tm=pv.shape[1], tn=tn_w, tk=tk_s, out_dtype=bf16, name="mm_d_ple_proj")
    dW_pg = _mm(a3, dzg, dims="tn", tm=tn_w, tn=tn_w, tk=tk_s, out_dtype=bf16, name="mm_d_ple_gate")
    da3 = _mm(dzg, W_pg, dims="nt", tm=tm, tn=_div(D, 1024, LANES), tk=D, out_dtype=f32, name="mm_da3")
    dh2, dh2b, d_nple = _rms_bwd(h2, norm_ple, da3, dh3, name="rms_ple_bwd")
    dm, *landed = _mm(dh2b, W_down, dims="nt", tm=tm, tn=_div(F, 512, LANES), tk=D, out_dtype=f32, name="mm_dm",
                      rider=reduce_start([4, 5], [dW_pg, dW_pp]))
    reduce_landed([4, 5], landed)
    dW_down = _mm(mact, dh2b, dims="tn", tm=_div(F, 1408, LANES), tn=tn_d, tk=tk_s, out_dtype=bf16, name="mm_d_down")
    du, dcw_g, dcw_v, dcb_g, dcb_v, *landed = _convact_bwd(u, dm, cw_full, cb, tc, rider=reduce_start([3], [dW_down]))
    reduce_landed([3], landed)
    dW_up = _mm(a2, du, dims="tn", tm=_div(D, 1024, LANES), tn=tc, tk=tk_s, out_dtype=bf16, name="mm_d_up",
                o_idx=lambda i, j, k: (i, perm(j)))
    da2, *landed = _mm(du, W_up, dims="nt", tm=tm, tn=D, tk=tc, out_dtype=f32, name="mm_da2",
                       b_idx=lambda i, j, k: (j, perm(k)), rider=reduce_start([2], [dW_up]))
    reduce_landed([2], landed)
    dh1, dh1b, d_nffn = _rms_bwd(h1, norm_ffn, da2, dh2, name="rms_ffn_bwd")
    dycat = _mm(dh1b, W_out, dims="nt", tm=tm, tn=_div(D, 1024, LANES), tk=D, out_dtype=f32, name="mm_dycat")
    dW_out = _mm(ycat, dh1b, dims="tn", tm=tn_w, tn=tn_w, tk=tk_s, out_dtype=bf16, name="mm_d_out")
    dq_att, dk_att, dv_att, dbias, *landed = _attn_bwd(proj, kp, vp, ext, dycat, W, rider=reduce_start([1], [dW_out]))
    reduce_landed([1], landed)
    d_hg4, d_lbl, d_hgn = _hgrn_bwd(proj, o_hg, st, dycat, lb_logits, hg_norm, W)
    d_rb = _bias_bwd(dbias, onehot)[:, 0, :nrel]
    dproj = jnp.concatenate([d_hg4, dq_att, dk_att, dv_att], axis=1)
    dW_in = _mm(a1, dproj, dims="tn", tm=tn_w, tn=_div(7 * W, 1024, LANES), tk=tk_s, out_dtype=bf16, name="mm_d_in")
    da1, *landed = _mm(dproj, W_in, dims="nt", tm=tm, tn=D, tk=_div(7 * W, 1792, LANES), out_dtype=f32, name="mm_da1",
                       rider=reduce_start([0], [dW_in]))
    reduce_landed([0], landed)
    grad_x, _, d_nmix = _rms_bwd(xv, norm_mix, da1, dh1, name="rms_mix_bwd")

    d_cw = jnp.concatenate([dcw_g, dcw_v], axis=1)
    d_cb = jnp.concatenate([dcb_g, dcb_v], axis=1)
    small_parts = [loss_part[:, :1], d_nmix, d_lbl, d_hgn, d_rb, d_nffn, d_cw, d_cb, d_nple, d_fn]
    small_shapes = [(1, 1), (1, D), lb_logits.shape, hg_norm.shape, (H, nrel), (1, D), (ncw, 2 * F), (1, 2 * F), (1, D), (1, D)]
    red = _unpack(_allreduce_small(_pack(small_parts)), small_shapes)
    loss = red[0].reshape(())
    g_nmix, g_lbl, g_hgn, g_rb, g_nffn, g_cw_all, g_cb, g_nple, g_fn = red[1:]
    csh = conv_w.shape[2]
    g_cw = lax.dynamic_slice(g_cw_all, (0, chip * csh), (ncw, csh))
    small_g = [g_nmix, g_lbl, g_hgn, g_rb[None], g_nffn, g_cw[None], g_cb, g_nple, g_fn.reshape(D)]
    small_w = [norm_mix, lb_logits, hg_norm, rel_bias, norm_ffn, conv_w, conv_b, norm_ple, final_norm]
    small_m = [m_norm_mix, m_lb_logits, m_hg_norm, m_rel_bias, m_norm_ffn, m_conv_w, m_conv_b, m_norm_ple, m_final_norm]
    small_v = [v_norm_mix, v_lb_logits, v_hg_norm, v_rel_bias, v_norm_ffn, v_conv_w, v_conv_b, v_norm_ple, v_final_norm]
    shapes_s = [w.shape for w in small_w]
    sd, sm, sv = _adamw(_pack(small_w), _pack(small_g), _pack(small_m), _pack(small_v), name="adamw_small")
    small_g = [g.reshape(s) for g, s in zip(small_g, shapes_s)]
    small_d, small_nm, small_nv = _unpack(sd, shapes_s), _unpack(sm, shapes_s), _unpack(sv, shapes_s)

    halves = [_chip_sum(pair[i], others[i], where, name=f"grad_chip_sum_{i}") for i in range(6)]
    g_big = _rs_join(halves)
    big_m = [m_w_in[0], m_w_out[0], m_w_up[0], m_w_down[0], m_w_ple_gate[0], m_w_ple_proj[0]]
    big_v = [v_w_in[0], v_w_out[0], v_w_up[0], v_w_down[0], v_w_ple_gate[0], v_w_ple_proj[0]]
    big_d, big_nm, big_nv = [], [], []
    for i in range(6):
        d_, m_, v_ = _adamw(big[i], g_big[i], big_m[i], big_v[i], name=f"adamw_{i}")
        big_d.append(d_[None])
        big_nm.append(m_[None])
        big_nv.append(v_[None])
    g_big = [g[None] for g in g_big]

    def order(sm_list, bg_list):
        s, b = sm_list, bg_list
        return [s[0], b[0], s[1], s[2], s[3], b[1], s[4], b[2], s[5], s[6], b[3], s[7], b[4], b[5], s[8]]

    return (loss, grad_x[None], *order(small_g, g_big), *order(small_d, big_d), *order(small_nm, big_nm), *order(small_nv, big_nv))
```

```python
import functools
from typing import Callable, NamedTuple

import jax
import jax.numpy as jnp
import numpy as np
from jax import lax
from jax.experimental import pallas as pl
from jax.experimental.pallas import tpu as pltpu

f32 = jnp.float32
bf16 = jnp.bfloat16

CHUNK = 64
HEAD_DIM = 128
LEFT_CHUNKS = 8
REL_CLIP = 128
EPS = 1e-6
HG_SUB = 16
HG_HEADS_PER_STEP = 8
ATT_Q_ROWS = 256
ATT_HEADS_PER_STEP = 2
ADAM_LR, ADAM_B1, ADAM_B2, ADAM_EPS, ADAM_WD, ADAM_STEP = 0.001, 0.9, 0.999, 1e-08, 0.01, 10
LANES = 128
SUBLANES = 8
N_CHIPS = 4
N_DEV = 8
MESH = pl.DeviceIdType.MESH
NEG = float(np.finfo(np.float32).min)

NN = (((1,), (0,)), ((), ()))
NT = (((1,), (1,)), ((), ()))
TN = (((0,), (0,)), ((), ()))
HIGHEST = lax.Precision.HIGHEST


def _dot(a, b, dims, precision=None):
    return lax.dot_general(a, b, dims, preferred_element_type=f32, precision=precision)


def _sigmoid(v):
    return 1.0 / (1.0 + jnp.exp(-v))


def _div(n, pref, mult):
    best = None
    d = mult
    while d <= min(n, pref):
        if n % d == 0:
            best = d
        d += mult
    return best if best is not None else n


def _hbm():
    return pl.BlockSpec(memory_space=pltpu.HBM)


def _vmem():
    return pl.BlockSpec(memory_space=pltpu.VMEM)


def _mm(a, b, *, dims, tm, tn, tk, out_dtype, name, res=None, a_idx=None, b_idx=None, o_idx=None, out_shape=None, rider=None,
        cols_outer=False):
    if dims == "nn":
        (M, K), (_, N) = a.shape, b.shape
        a_blk, b_blk, dn = (tm, tk), (tk, tn), NN
        a_def, b_def = (lambda i, j, k: (i, k)), (lambda i, j, k: (k, j))
    elif dims == "nt":
        (M, K), (N, _) = a.shape, b.shape
        a_blk, b_blk, dn = (tm, tk), (tn, tk), NT
        a_def, b_def = (lambda i, j, k: (i, k)), (lambda i, j, k: (j, k))
    else:
        (K, M), (_, N) = a.shape, b.shape
        a_blk, b_blk, dn = (tk, tm), (tk, tn), TN
        a_def, b_def = (lambda i, j, k: (k, i)), (lambda i, j, k: (k, j))
    assert M % tm == 0 and N % tn == 0 and K % tk == 0, (name, M, N, K, tm, tn, tk)
    nk = K // tk
    has_res = res is not None

    def body(*refs):
        a_ref, b_ref = refs[0], refs[1]
        res_ref = refs[2] if has_res else None
        o_ref = refs[3] if has_res else refs[2]
        part = _dot(a_ref[...].astype(bf16), b_ref[...].astype(bf16), dn)

        def finish(acc):
            if has_res:
                acc = acc + res_ref[...]
            o_ref[...] = acc.astype(out_dtype)

        if nk == 1:
            finish(part)
        else:
            acc_ref = refs[-1]
            k = pl.program_id(2)

            @pl.when(k == 0)
            def _():
                acc_ref[...] = part

            @pl.when(k > 0)
            def _():
                acc_ref[...] += part

            @pl.when(k == nk - 1)
            def _():
                finish(acc_ref[...])

    def ordered(f):
        return (lambda j, i, k: f(i, j, k)) if cols_outer else f

    o_map = ordered(o_idx or (lambda i, j, k: (i, j)))
    in_specs = [pl.BlockSpec(a_blk, ordered(a_idx or a_def)), pl.BlockSpec(b_blk, ordered(b_idx or b_def))]
    args = [a, b]
    if has_res:
        in_specs.append(pl.BlockSpec((tm, tn), o_map))
        args.append(res)
    outs = _grid_call(
        body,
        rider,
        name=name,
        grid=(N // tn, M // tm, nk) if cols_outer else (M // tm, N // tn, nk),
        in_specs=in_specs,
        out_specs=[pl.BlockSpec((tm, tn), o_map)],
        out_shape=[jax.ShapeDtypeStruct(out_shape or (M, N), out_dtype)],
        scratch_shapes=[pltpu.VMEM((tm, tn), f32)] if nk > 1 else [],
        args=args,
    )
    return outs[0] if rider is None else outs


def _rms_fwd(xv, g, *, name):
    S, D = xv.shape
    ts = _div(S, 512, SUBLANES)

    def body(x_ref, g_ref, o_ref):
        v = x_ref[...]
        r = lax.rsqrt(jnp.mean(v * v, axis=-1, keepdims=True) + EPS)
        o_ref[...] = (v * r * g_ref[...]).astype(bf16)

    return pl.pallas_call(
        body,
        name=name,
        grid=(S // ts,),
        in_specs=[pl.BlockSpec((ts, D), lambda i: (i, 0)), pl.BlockSpec((1, D), lambda i: (0, 0))],
        out_specs=pl.BlockSpec((ts, D), lambda i: (i, 0)),
        out_shape=jax.ShapeDtypeStruct((S, D), bf16),
        compiler_params=pltpu.CompilerParams(dimension_semantics=("parallel",)),
    )(xv, g)


def _rms_bwd(xv, g, dy, dres, *, name):
    S, D = xv.shape
    ts = _div(S, 256, SUBLANES)

    def body(x_ref, g_ref, dy_ref, dres_ref, dx_ref, dxb_ref, dg_ref):
        i = pl.program_id(0)
        v = x_ref[...]
        r = lax.rsqrt(jnp.mean(v * v, axis=-1, keepdims=True) + EPS)
        vn = v * r
        d = dy_ref[...]
        part = jnp.sum(d * vn, axis=0, keepdims=True)

        @pl.when(i == 0)
        def _():
            dg_ref[...] = part

        @pl.when(i > 0)
        def _():
            dg_ref[...] += part

        t = d * g_ref[...]
        dx = dres_ref[...] + r * (t - vn * jnp.mean(t * vn, axis=-1, keepdims=True))
        dx_ref[...] = dx
        dxb_ref[...] = dx.astype(bf16)

    row = pl.BlockSpec((ts, D), lambda i: (i, 0))
    vec = pl.BlockSpec((1, D), lambda i: (0, 0))
    return pl.pallas_call(
        body,
        name=name,
        grid=(S // ts,),
        in_specs=[row, vec, row, row],
        out_specs=[row, row, vec],
        out_shape=[jax.ShapeDtypeStruct((S, D), f32), jax.ShapeDtypeStruct((S, D), bf16), jax.ShapeDtypeStruct((1, D), f32)],
        compiler_params=pltpu.CompilerParams(dimension_semantics=("arbitrary",)),
    )(xv, g, dy, dres)


def _tail(h2, zg, pp, tgt, fn):
    S, D = h2.shape
    ts = _div(S, 256, SUBLANES)

    def body(h_ref, z_ref, p_ref, t_ref, fn_ref, dh_ref, dz_ref, dp_ref, loss_ref, dfn_ref):
        i = pl.program_id(0)
        sg = _sigmoid(z_ref[...])
        ppv = p_ref[...]
        h3 = h_ref[...] + sg * ppv
        r = lax.rsqrt(jnp.mean(h3 * h3, axis=-1, keepdims=True) + EPS)
        hn = h3 * r
        fnv = fn_ref[...]
        e = hn * fnv - t_ref[...]
        lpart = 0.5 * jnp.sum(jnp.mean(e * e, axis=-1, keepdims=True), axis=0, keepdims=True)
        dy = e * (1.0 / D)
        dpart = jnp.sum(dy * hn, axis=0, keepdims=True)

        @pl.when(i == 0)
        def _():
            loss_ref[...] = jnp.broadcast_to(lpart, loss_ref.shape)
            dfn_ref[...] = dpart

        @pl.when(i > 0)
        def _():
            loss_ref[...] += jnp.broadcast_to(lpart, loss_ref.shape)
            dfn_ref[...] += dpart

        t = dy * fnv
        dh3 = r * (t - hn * jnp.mean(t * hn, axis=-1, keepdims=True))
        dh_ref[...] = dh3
        dz_ref[...] = (dh3 * ppv * sg * (1.0 - sg)).astype(bf16)
        dp_ref[...] = (dh3 * sg).astype(bf16)

    row = pl.BlockSpec((ts, D), lambda i: (i, 0))
    vec = pl.BlockSpec((1, D), lambda i: (0, 0))
    one = pl.BlockSpec((1, LANES), lambda i: (0, 0))
    return pl.pallas_call(
        body,
        name="ple_final_loss",
        grid=(S // ts,),
        in_specs=[row, row, row, row, vec],
        out_specs=[row, row, row, one, vec],
        out_shape=[
            jax.ShapeDtypeStruct((S, D), f32),
            jax.ShapeDtypeStruct((S, D), bf16),
            jax.ShapeDtypeStruct((S, D), bf16),
            jax.ShapeDtypeStruct((1, LANES), f32),
            jax.ShapeDtypeStruct((1, D), f32),
        ],
        compiler_params=pltpu.CompilerParams(dimension_semantics=("arbitrary",)),
    )(h2, zg, pp, tgt, fn)


def _shift_down(v, halo, k):
    r = pltpu.roll(v, k, 0)
    hr = pltpu.roll(halo, k, 0)
    row = lax.broadcasted_iota(jnp.int32, hr.shape, 0)
    top = jnp.where(row < k, hr, r[0:SUBLANES])
    return top if v.shape[0] == SUBLANES else jnp.concatenate([top, r[SUBLANES:]], axis=0)


def _shift_up(v, halo, k):
    n = v.shape[0]
    r = pltpu.roll(v, n - k, 0)
    hr = pltpu.roll(halo, SUBLANES - k, 0)
    row = lax.broadcasted_iota(jnp.int32, hr.shape, 0)
    bot = jnp.where(row >= SUBLANES - k, hr, r[n - SUBLANES:])
    return jnp.concatenate([r[: n - SUBLANES], bot], axis=0)


def _conv_specs(S, F, tc, ts):
    nj = F // tc
    rows8 = ts // SUBLANES
    main = pl.BlockSpec((ts, 2 * tc), lambda j, i: (i, j))
    prev = pl.BlockSpec((SUBLANES, 2 * tc), lambda j, i: (jnp.maximum(i * rows8 - 1, 0), j))
    nxt = pl.BlockSpec((SUBLANES, 2 * tc), lambda j, i: (jnp.minimum((i + 1) * rows8, S // SUBLANES - 1), j))
    wg = pl.BlockSpec((3, tc), lambda j, i: (0, j))
    wv = pl.BlockSpec((3, tc), lambda j, i: (0, nj + j))
    bg = pl.BlockSpec((1, tc), lambda j, i: (0, j))
    bv = pl.BlockSpec((1, tc), lambda j, i: (0, nj + j))
    half = pl.BlockSpec((ts, tc), lambda j, i: (i, j))
    return nj, main, prev, nxt, wg, wv, bg, bv, half


def _conv_pre(u_ref, h_ref, wg_ref, wv_ref, bg_ref, bv_ref, tc):
    i = pl.program_id(1)
    u = u_ref[...]
    halo = jnp.where(i == 0, 0.0, h_ref[...])
    u1 = _shift_down(u, halo, 1)
    u2 = _shift_down(u, halo, 2)
    w = jnp.concatenate([wg_ref[...], wv_ref[...]], axis=1)
    b = jnp.concatenate([bg_ref[...], bv_ref[...]], axis=1)
    uc = b + w[0:1] * u2 + w[1:2] * u1 + w[2:3] * u
    return u, u1, u2, uc[:, :tc], uc[:, tc:]


def _convact_fwd(u, cw, cb, tc):
    S, F2 = u.shape
    F = F2 // 2
    ts = _div(S, 512, SUBLANES)
    nj, main, prev, nxt, wg, wv, bg, bv, half = _conv_specs(S, F, tc, ts)

    def body(u_ref, h_ref, wg_ref, wv_ref, bg_ref, bv_ref, m_ref):
        _, _, _, g, v = _conv_pre(u_ref, h_ref, wg_ref, wv_ref, bg_ref, bv_ref, tc)
        m_ref[...] = (g * _sigmoid(g) * v).astype(bf16)

    return pl.pallas_call(
        body,
        name="convact_fwd",
        grid=(nj, S // ts),
        in_specs=[main, prev, wg, wv, bg, bv],
        out_specs=half,
        out_shape=jax.ShapeDtypeStruct((S, F), bf16),
        compiler_params=pltpu.CompilerParams(dimension_semantics=("parallel", "parallel")),
    )(u, u, cw, cw, cb, cb)


def _convact_bwd(u, dm, cw, cb, tc, rider=None):
    S, F2 = u.shape
    F = F2 // 2
    ts = _div(S, 512, SUBLANES)
    nj, main, prev, nxt, wg, wv, bg, bv, half = _conv_specs(S, F, tc, ts)
    ni = S // ts
    half_nxt = pl.BlockSpec((SUBLANES, tc), lambda j, i: (jnp.minimum((i + 1) * (ts // SUBLANES), S // SUBLANES - 1), j))

    def act_bwd(g, v, d):
        sg = _sigmoid(g)
        return jnp.concatenate([d * v * sg * (1.0 + g * (1.0 - sg)), d * g * sg], axis=1)

    def body(u_ref, h_ref, n_ref, dm_ref, dmn_ref, wg_ref, wv_ref, bg_ref, bv_ref, du_ref, dwg_ref, dwv_ref, dbg_ref, dbv_ref):
        i = pl.program_id(1)
        u0, u1, u2, g, v = _conv_pre(u_ref, h_ref, wg_ref, wv_ref, bg_ref, bv_ref, tc)
        duc = act_bwd(g, v, dm_ref[...])
        w = jnp.concatenate([wg_ref[...], wv_ref[...]], axis=1)
        b = jnp.concatenate([bg_ref[...], bv_ref[...]], axis=1)
        un = n_ref[...]
        tail = u0[ts - SUBLANES:]
        ucn = b + w[0:1] * _shift_down(un, tail, 2) + w[1:2] * _shift_down(un, tail, 1) + w[2:3] * un
        ducn = jnp.where(i == ni - 1, 0.0, act_bwd(ucn[:, :tc], ucn[:, tc:], dmn_ref[...]))
        du_ref[...] = (w[2:3] * duc + w[1:2] * _shift_up(duc, ducn, 1) + w[0:1] * _shift_up(duc, ducn, 2)).astype(bf16)
        db = jnp.sum(duc, axis=0, keepdims=True)
        dw = jnp.concatenate(
            [jnp.sum(duc * u2, axis=0, keepdims=True), jnp.sum(duc * u1, axis=0, keepdims=True), jnp.sum(duc * u0, axis=0, keepdims=True)],
            axis=0,
        )

        @pl.when(i == 0)
        def _():
            dwg_ref[...] = dw[:, :tc]
            dwv_ref[...] = dw[:, tc:]
            dbg_ref[...] = db[:, :tc]
            dbv_ref[...] = db[:, tc:]

        @pl.when(i > 0)
        def _():
            dwg_ref[...] += dw[:, :tc]
            dwv_ref[...] += dw[:, tc:]
            dbg_ref[...] += db[:, :tc]
            dbv_ref[...] += db[:, tc:]

    w_out = pl.BlockSpec((3, tc), lambda j, i: (0, j))
    b_out = pl.BlockSpec((1, tc), lambda j, i: (0, j))
    return _grid_call(
        body,
        rider,
        name="convact_bwd",
        grid=(nj, ni),
        in_specs=[main, prev, nxt, half, half_nxt, wg, wv, bg, bv],
        out_specs=[main, w_out, w_out, b_out, b_out],
        out_shape=[
            jax.ShapeDtypeStruct((S, F2), bf16),
            jax.ShapeDtypeStruct((3, F), f32),
            jax.ShapeDtypeStruct((3, F), f32),
            jax.ShapeDtypeStruct((1, F), f32),
            jax.ShapeDtypeStruct((1, F), f32),
        ],
        scratch_shapes=[],
        args=(u, u, u, dm, dm, cw, cw, cb, cb),
    )


def _hg_gates(qp, fp, lbl):
    lb = _sigmoid(lbl[0:1] - lbl[1:2])
    sig = _sigmoid(fp)
    sigm = _sigmoid(-fp)
    f = lb + (1.0 - lb) * sig
    k = (1.0 - lb) * sigm
    sq = _sigmoid(qp)
    qf = qp * sq
    row = lax.broadcasted_iota(jnp.int32, (CHUNK, CHUNK), 0)
    col = lax.broadcasted_iota(jnp.int32, (CHUNK, CHUNK), 1)
    b = _dot((row >= col).astype(f32), jnp.log(f), NN, HIGHEST)
    return lb, sig, sigm, f, k, sq, qf, b


def _hg_block(qf, k, b, I):
    r0, n = I * HG_SUB, (I + 1) * HG_SUB
    base = b[r0 - 1:r0] if I > 0 else jnp.zeros_like(b[0:1])
    eq = jnp.exp(b[r0:n] - base)
    ek = jnp.exp(base - b[0:n])
    qt16 = (qf[r0:n] * eq).astype(bf16)
    kt16 = (k[0:n] * ek).astype(bf16)
    row = lax.broadcasted_iota(jnp.int32, (HG_SUB, n), 0) + r0
    col = lax.broadcasted_iota(jnp.int32, (HG_SUB, n), 1)
    return r0, n, eq, ek, qt16, kt16, col <= row


def _per_head(v, hp, fn):
    return jnp.concatenate([fn(v[:, p * HEAD_DIM:(p + 1) * HEAD_DIM]) for p in range(hp)], axis=1)


def _head_mean(v, hp):
    return _per_head(v, hp, lambda t: jnp.broadcast_to(jnp.mean(t, axis=-1, keepdims=True), t.shape))


def _pad_rows(v, rows):
    return v if v.shape[0] == rows else jnp.concatenate([v, jnp.zeros((rows - v.shape[0], v.shape[1]), v.dtype)], axis=0)


def _hgrn_specs(S, W, hp, reverse):
    nc = S // CHUNK
    ngrp = W // (hp * HEAD_DIM)
    cw = hp * HEAD_DIM
    cidx = (lambda c: nc - 1 - c) if reverse else (lambda c: c)

    def proj(off):
        return pl.BlockSpec((CHUNK, cw), lambda h, c: (cidx(c), off * ngrp + h))

    act = pl.BlockSpec((CHUNK, cw), lambda h, c: (cidx(c), h))
    lbl = pl.BlockSpec((2, cw), lambda h, c: (0, h))
    gn = pl.BlockSpec((1, HEAD_DIM), lambda h, c: (0, 0))
    st = pl.BlockSpec((hp, 1, HEAD_DIM, HEAD_DIM), lambda h, c: (h, cidx(c), 0, 0))
    return nc, ngrp, proj, act, lbl, gn, st


def _grid_call(body, rider, *, name, grid, in_specs, out_specs, out_shape, scratch_shapes, args):
    params = pltpu.CompilerParams(dimension_semantics=("arbitrary",) * len(grid))
    if rider is None:
        return pl.pallas_call(body, name=name, grid=grid, in_specs=in_specs, out_specs=out_specs, out_shape=out_shape,
                              scratch_shapes=scratch_shapes, compiler_params=params)(*args)
    n_in, n_out, n_scr = len(in_specs), len(out_specs), len(scratch_shapes)
    ri, ro = len(rider.ins), len(rider.out_shapes)

    def wrapped(*refs):
        a, b = n_in, n_in + ri
        c, d = b + n_out, b + n_out + ro
        e = d + n_scr
        ids = [pl.program_id(t) for t in range(len(grid))]
        is_first = functools.reduce(jnp.logical_and, [i == 0 for i in ids])
        is_last = functools.reduce(jnp.logical_and, [i == g - 1 for i, g in zip(ids, grid)])

        @pl.when(is_first)
        def _():
            rider.first(refs[a:b], refs[c:d], refs[e:])

        body(*refs[:a], *refs[b:c], *refs[d:e])

        @pl.when(is_last)
        def _():
            rider.last(refs[a:b], refs[c:d], refs[e:])

    return pl.pallas_call(
        wrapped,
        name=name,
        grid=grid,
        in_specs=list(in_specs) + [_hbm()] * ri,
        out_specs=list(out_specs) + [_hbm()] * ro,
        out_shape=list(out_shape) + list(rider.out_shapes),
        input_output_aliases={n_in + i: n_out + o for i, o in rider.aliases.items()},
        scratch_shapes=list(scratch_shapes) + list(rider.sems),
        compiler_params=params,
    )(*args, *rider.ins)


def _hgrn_fwd(proj, lb_logits, hg_norm, W, rider=None):
    S = proj.shape[0]
    hp = min(HG_HEADS_PER_STEP, W // HEAD_DIM)
    nc, ngrp, pspec, act, lbl_spec, gn_spec, st_spec = _hgrn_specs(S, W, hp, False)
    nb = CHUNK // HG_SUB

    def body(q_ref, f_ref, i_ref, g_ref, lbl_ref, gn_ref, y_ref, o_ref, st_ref, state):
        c = pl.program_id(1)

        @pl.when(c == 0)
        def _():
            state[...] = jnp.zeros_like(state)

        _, _, _, _, k, _, qf, b = _hg_gates(q_ref[...], f_ref[...], lbl_ref[...])
        v16 = i_ref[...].astype(bf16)
        a16 = (qf * jnp.exp(b)).astype(bf16)
        bl = b[CHUNK - 1:CHUNK]
        kd16 = (k * jnp.exp(bl - b)).astype(bf16)
        ebl = jnp.exp(bl)
        blocks = [_hg_block(qf, k, b, I) for I in range(nb)]
        heads = [slice(p * HEAD_DIM, (p + 1) * HEAD_DIM) for p in range(hp)]
        st_prev = state[...]
        st_ref[:, 0] = st_prev
        inter = [_dot(a16[:, sl], st_prev[p].astype(bf16), NT) for p, sl in enumerate(heads)]
        scores = [[_dot(qt16[:, sl], kt16[:, sl], NT) for sl in heads] for _, _, _, _, qt16, kt16, _ in blocks]
        intra = [[_dot(jnp.where(blk[6], sc, 0.0).astype(bf16), v16[0:blk[1], sl], NN) for sc, sl in zip(scs, heads)]
                 for blk, scs in zip(blocks, scores)]
        st_new = [st_prev[p] * ebl[:, sl] + _dot(v16[:, sl], kd16[:, sl], TN) for p, sl in enumerate(heads)]
        state[...] = jnp.stack(st_new)
        o = jnp.concatenate([inter[p] + jnp.concatenate([rows[p] for rows in intra], axis=0) for p in range(hp)], axis=1)
        o_ref[...] = o
        r = lax.rsqrt(_head_mean(o * o, hp) + EPS)
        gp = g_ref[...]
        y_ref[...] = (o * r * jnp.tile(gn_ref[...], (1, hp)) * (gp * _sigmoid(gp))).astype(bf16)

    H = W // HEAD_DIM
    return _grid_call(
        body,
        rider,
        name="hgrn_fwd",
        grid=(ngrp, nc),
        in_specs=[pspec(0), pspec(1), pspec(2), pspec(3), lbl_spec, gn_spec],
        out_specs=[act, act, st_spec],
        out_shape=[
            jax.ShapeDtypeStruct((S, W), bf16),
            jax.ShapeDtypeStruct((S, W), f32),
            jax.ShapeDtypeStruct((H, nc, HEAD_DIM, HEAD_DIM), f32),
        ],
        scratch_shapes=[pltpu.VMEM((hp, HEAD_DIM, HEAD_DIM), f32)],
        args=(proj, proj, proj, proj, lb_logits, hg_norm),
    )


def _hgrn_bwd(proj, o_hg, st, dycat, lb_logits, hg_norm, W, rider=None):
    S = proj.shape[0]
    hp = min(HG_HEADS_PER_STEP, W // HEAD_DIM)
    assert hp * HEAD_DIM == W, "the four gradients share one output block: all heads in one grid step"
    nc, ngrp, pspec, act, lbl_spec, gn_spec, st_spec = _hgrn_specs(S, W, hp, True)
    nb = CHUNK // HG_SUB

    def body(q_ref, f_ref, i_ref, g_ref, o_ref, st_ref, dy_ref, lbl_ref, gn_ref,
             d4_ref, dlbl_ref, dgn_ref, dstate):
        cw_ = hp * HEAD_DIM
        dq_ref, df_ref, di_ref, dg_ref = (d4_ref.at[:, pl.ds(t * cw_, cw_)] for t in range(4))
        h = pl.program_id(0)
        c = pl.program_id(1)

        @pl.when(c == 0)
        def _():
            dstate[...] = jnp.zeros_like(dstate)
            dlbl_ref[...] = jnp.zeros_like(dlbl_ref)

        @pl.when((c == 0) & (h == 0))
        def _():
            dgn_ref[...] = jnp.zeros_like(dgn_ref)

        row = lax.broadcasted_iota(jnp.int32, (CHUNK, CHUNK), 0)
        col = lax.broadcasted_iota(jnp.int32, (CHUNK, CHUNK), 1)
        upper = (col >= row).astype(f32)
        last_row = lax.broadcasted_iota(jnp.int32, (CHUNK, hp * HEAD_DIM), 0) == CHUNK - 1

        qp, fp = q_ref[...], f_ref[...]
        lb, sig, sigm, f, k, sq, qf, b = _hg_gates(qp, fp, lbl_ref[...])
        v16 = i_ref[...].astype(bf16)
        gnv = jnp.tile(gn_ref[...], (1, hp))
        gp = g_ref[...]
        sgg = _sigmoid(gp)
        gate = gp * sgg
        o = o_ref[...]
        r = lax.rsqrt(_head_mean(o * o, hp) + EPS)
        on = o * r
        dy = dy_ref[...]
        dg_ref[...] = (dy * on * gnv * (sgg * (1.0 + gp * (1.0 - sgg)))).astype(bf16)
        dgn_wide = jnp.sum(dy * on * gate, axis=0, keepdims=True)
        dgn_ref[...] += functools.reduce(jnp.add, [dgn_wide[:, p * HEAD_DIM:(p + 1) * HEAD_DIM] for p in range(hp)])
        don = dy * gnv * gate
        do16 = (r * (don - on * _head_mean(don * on, hp))).astype(bf16)
        eb = jnp.exp(b)
        A = qf * eb
        a16 = A.astype(bf16)
        bl = b[CHUNK - 1:CHUNK]
        ebl = jnp.exp(bl)
        ekd = jnp.exp(bl - b)
        kd = k * ekd
        kd16 = kd.astype(bf16)
        blocks = [_hg_block(qf, k, b, I) for I in range(nb)]

        heads = [slice(p * HEAD_DIM, (p + 1) * HEAD_DIM) for p in range(hp)]
        st_prev = st_ref[:, 0]
        dst_new = dstate[...]
        st16, dst16 = st_prev.astype(bf16), dst_new.astype(bf16)
        dA_h = [_dot(do16[:, sl], st16[p], NN) for p, sl in enumerate(heads)]
        dkd_h = [_dot(v16[:, sl], dst16[p], NN) for p, sl in enumerate(heads)]
        dv_h = [_dot(kd16[:, sl], dst16[p], NT) for p, sl in enumerate(heads)]
        dstate[...] = jnp.stack([dst_new[p] * ebl[:, sl] + _dot(do16[:, sl], a16[:, sl], TN) for p, sl in enumerate(heads)])
        dbl_h = [jnp.sum(dst_new[p] * st_prev[p], axis=0, keepdims=True) for p in range(hp)]
        sc_h = [[jnp.where(mask, _dot(qt16[:, sl], kt16[:, sl], NT), 0.0).astype(bf16) for sl in heads]
                for _, _, _, _, qt16, kt16, mask in blocks]
        dsc_h = [[jnp.where(mask, _dot(do16[r0:n, sl], v16[0:n, sl], NT), 0.0).astype(bf16) for sl in heads]
                 for r0, n, _, _, _, _, mask in blocks]
        dvi_h = [[_dot(sc, do16[blk[0]:blk[1], sl], TN) for sc, sl in zip(scs, heads)] for blk, scs in zip(blocks, sc_h)]
        dqt_h = [[_dot(dsc, blk[5][:, sl], NN) for dsc, sl in zip(dscs, heads)] for blk, dscs in zip(blocks, dsc_h)]
        dkt_h = [[_dot(dsc, blk[4][:, sl], TN) for dsc, sl in zip(dscs, heads)] for blk, dscs in zip(blocks, dsc_h)]
        dv_h = [functools.reduce(jnp.add, [dv_h[p]] + [_pad_rows(rows[p], CHUNK) for rows in dvi_h]) for p in range(hp)]

        dA, dkd = jnp.concatenate(dA_h, axis=1), jnp.concatenate(dkd_h, axis=1)
        dbl = jnp.concatenate(dbl_h, axis=1) * ebl + jnp.sum(dkd * kd, axis=0, keepdims=True)
        db = dA * A - dkd * kd + jnp.where(last_row, dbl, 0.0)
        dk = dkd * ekd
        dq_rows, db_rows = [], []
        for I, (r0, n, eq, ek, qt16, kt16, _) in enumerate(blocks):
            dqt, dkt = jnp.concatenate(dqt_h[I], axis=1), jnp.concatenate(dkt_h[I], axis=1)
            dq_rows.append(dqt * eq)
            dk = dk + _pad_rows(dkt * ek, CHUNK)
            db_rows.append(dqt * qt16.astype(f32))
            db = db - _pad_rows(dkt * kt16.astype(f32), CHUNK)
        dqf = dA * eb + jnp.concatenate(dq_rows, axis=0)
        db = db + jnp.concatenate(db_rows, axis=0)
        dlogf = _dot(upper, db, NN, HIGHEST)
        dfg = dlogf / f
        df_ref[...] = ((1.0 - lb) * sig * sigm * (dfg - dk)).astype(bf16)
        dlb = jnp.sum(sigm * (dfg - dk), axis=0, keepdims=True)
        dl0 = dlb * lb * (1.0 - lb)
        dlbl_ref[...] += jnp.concatenate([dl0, -dl0], axis=0)
        dq_ref[...] = (dqf * (sq * (1.0 + qp * (1.0 - sq)))).astype(bf16)
        di_ref[...] = jnp.concatenate(dv_h, axis=1).astype(bf16)

    cw = hp * HEAD_DIM
    dy_spec = pl.BlockSpec((CHUNK, cw), lambda h, c: (nc - 1 - c, h))
    return _grid_call(
        body,
        rider,
        name="hgrn_bwd",
        grid=(ngrp, nc),
        in_specs=[pspec(0), pspec(1), pspec(2), pspec(3), act, st_spec, dy_spec, lbl_spec, gn_spec],
        out_specs=[pl.BlockSpec((CHUNK, 4 * W), lambda h, c: (nc - 1 - c, 0)), lbl_spec, gn_spec],
        out_shape=[jax.ShapeDtypeStruct((S, 4 * W), bf16), jax.ShapeDtypeStruct((2, W), f32), jax.ShapeDtypeStruct((1, HEAD_DIM), f32)],
        scratch_shapes=[pltpu.VMEM((hp, HEAD_DIM, HEAD_DIM), f32)],
        args=(proj, proj, proj, proj, o_hg, st, dycat, lb_logits, hg_norm),
    )


def _att_dims(S):
    qt = ATT_Q_ROWS if S % ATT_Q_ROWS == 0 else CHUNK
    pad = LEFT_CHUNKS * CHUNK
    kb = pad + qt
    return qt, pad, kb, kb + qt


def _kv_padded(proj, W, pad):
    S = proj.shape[0]
    ts = _div(pad, 512, 2 * SUBLANES)
    assert S % ts == 0 and pad % ts == 0
    npad = pad // ts

    def body(k_ref, v_ref, ko_ref, vo_ref):
        i = pl.program_id(0)

        @pl.when(i < npad)
        def _():
            ko_ref[...] = jnp.zeros_like(ko_ref)
            vo_ref[...] = jnp.zeros_like(vo_ref)

        @pl.when(i >= npad)
        def _():
            ko_ref[...] = k_ref[...].astype(bf16)
            vo_ref[...] = v_ref[...].astype(bf16)

    out = pl.BlockSpec((ts, W), lambda i: (i, 0))
    return pl.pallas_call(
        body,
        name="kv_padded",
        grid=((S + pad) // ts,),
        in_specs=[pl.BlockSpec((ts, W), lambda i: (jnp.maximum(i - npad, 0), 5)), pl.BlockSpec((ts, W), lambda i: (jnp.maximum(i - npad, 0), 6))],
        out_specs=[out, out],
        out_shape=[jax.ShapeDtypeStruct((S + pad, W), bf16)] * 2,
        compiler_params=pltpu.CompilerParams(dimension_semantics=("parallel",)),
    )(proj, proj)


def _att_probs(qk, bias, start, qt, pad, kb):
    s = qk * (HEAD_DIM ** -0.5) + bias
    row = lax.broadcasted_iota(jnp.int32, (qt, kb), 0)
    col = lax.broadcasted_iota(jnp.int32, (qt, kb), 1)
    lo = jnp.bitwise_and(row, -CHUNK)
    ok = (col >= lo) & (col < lo + pad + CHUNK) & (col + start >= pad)
    s = jnp.where(ok, s, NEG)
    e = jnp.exp(s - jnp.max(s, axis=-1, keepdims=True))
    return e / jnp.sum(e, axis=-1, keepdims=True)


def _att_bias(ext_row, qt, kb, ne):
    e = jnp.broadcast_to(ext_row, (qt, ne))
    return pltpu.roll(e, ne - qt + 1, 1, stride=1, stride_axis=0)[:, :kb]


def _att_specs(S, H, ah, qt, pad, ne):
    cw = ah * HEAD_DIM
    q_spec = pl.BlockSpec((qt, cw), lambda h, g: (g, 4 * H // ah + h))
    kv_spec = pl.BlockSpec((S + pad, cw), lambda h, g: (0, h))
    ext_spec = pl.BlockSpec((ah, 1, ne), lambda h, g: (h, 0, 0))
    row_spec = pl.BlockSpec((qt, cw), lambda h, g: (g, h))
    return cw, q_spec, kv_spec, ext_spec, row_spec


def _attn_fwd(proj, kp, vp, ext, W, rider=None):
    S = proj.shape[0]
    H = W // HEAD_DIM
    ah = min(ATT_HEADS_PER_STEP, H)
    qt, pad, kb, ne = _att_dims(S)
    cw, q_spec, kv_spec, ext_spec, row_spec = _att_specs(S, H, ah, qt, pad, ne)
    heads = [slice(a * HEAD_DIM, (a + 1) * HEAD_DIM) for a in range(ah)]

    def body(q_ref, k_ref, v_ref, ext_ref, o_ref, bias):
        g = pl.program_id(1)

        @pl.when(g == 0)
        def _():
            for a in range(ah):
                bias[a] = _att_bias(ext_ref[a], qt, kb, ne)

        start = pl.multiple_of(g * qt, qt)
        q16 = q_ref[...].astype(bf16)
        kb16 = k_ref[pl.ds(start, kb), :]
        vb16 = v_ref[pl.ds(start, kb), :]
        qk = [_dot(q16[:, sl], kb16[:, sl], NT) for sl in heads]
        pn = [_att_probs(qk[a], bias[a], start, qt, pad, kb).astype(bf16) for a in range(ah)]
        o_ref[...] = jnp.concatenate([_dot(pn[a], vb16[:, sl], NN) for a, sl in enumerate(heads)], axis=1).astype(bf16)

    outs = _grid_call(
        body,
        rider,
        name="attn_fwd",
        grid=(H // ah, S // qt),
        in_specs=[q_spec, kv_spec, kv_spec, ext_spec],
        out_specs=[row_spec],
        out_shape=[jax.ShapeDtypeStruct((S, W), bf16)],
        scratch_shapes=[pltpu.VMEM((ah, qt, kb), f32)],
        args=(proj, kp, vp, ext),
    )
    return outs[0] if rider is None else outs


def _attn_bwd(proj, kp, vp, ext, dycat, W, rider=None):
    S = proj.shape[0]
    H = W // HEAD_DIM
    ah = min(ATT_HEADS_PER_STEP, H)
    qt, pad, kb, ne = _att_dims(S)
    cw, q_spec, kv_spec, ext_spec, row_spec = _att_specs(S, H, ah, qt, pad, ne)
    heads = [slice(a * HEAD_DIM, (a + 1) * HEAD_DIM) for a in range(ah)]
    scale = HEAD_DIM ** -0.5
    ng = S // qt

    def body(q_ref, k_ref, v_ref, ext_ref, do_ref, dq_ref, dko_ref, dvo_ref, db_ref, bias, dk_ref, dv_ref):
        g = pl.program_id(1)

        @pl.when(g == 0)
        def _():
            for a in range(ah):
                bias[a] = _att_bias(ext_ref[a], qt, kb, ne)
            dk_ref[...] = jnp.zeros_like(dk_ref)
            dv_ref[...] = jnp.zeros_like(dv_ref)
            db_ref[...] = jnp.zeros_like(db_ref)

        start = pl.multiple_of(g * qt, qt)
        q16 = q_ref[...].astype(bf16)
        kb16 = k_ref[pl.ds(start, kb), :]
        vb16 = v_ref[pl.ds(start, kb), :]
        do16 = do_ref[...].astype(bf16)
        qk = [_dot(q16[:, sl], kb16[:, sl], NT) for sl in heads]
        dpn = [_dot(do16[:, sl], vb16[:, sl], NT) for sl in heads]
        pn = [_att_probs(qk[a], bias[a], start, qt, pad, kb) for a in range(ah)]
        ds = [pn[a] * (dpn[a] - jnp.sum(dpn[a] * pn[a], axis=-1, keepdims=True)) for a in range(ah)]
        ds16 = [d.astype(bf16) for d in ds]
        dv_ref[pl.ds(start, kb), :] += jnp.concatenate([_dot(pn[a].astype(bf16), do16[:, sl], TN) for a, sl in enumerate(heads)], axis=1)
        dq_ref[...] = (jnp.concatenate([_dot(ds16[a], kb16[:, sl], NN) for a, sl in enumerate(heads)], axis=1) * scale).astype(bf16)
        dk_ref[pl.ds(start, kb), :] += jnp.concatenate([_dot(ds16[a], q16[:, sl], TN) for a, sl in enumerate(heads)], axis=1) * scale
        for a in range(ah):
            db_ref[a] += ds[a]

        @pl.when(g == ng - 1)
        def _():
            dko_ref[...] = dk_ref[pl.ds(pad, S), :].astype(bf16)
            dvo_ref[...] = dv_ref[pl.ds(pad, S), :].astype(bf16)

    kv_out = pl.BlockSpec((S, cw), lambda h, g: (0, h))
    return _grid_call(
        body,
        rider,
        name="attn_bwd",
        grid=(H // ah, S // qt),
        in_specs=[q_spec, kv_spec, kv_spec, ext_spec,
                  pl.BlockSpec((qt, cw), lambda h, g: (g, H // ah + h))],
        out_specs=[row_spec, kv_out, kv_out, pl.BlockSpec((ah, qt, kb), lambda h, g: (h, 0, 0))],
        out_shape=[
            jax.ShapeDtypeStruct((S, W), bf16),
            jax.ShapeDtypeStruct((S, W), bf16),
            jax.ShapeDtypeStruct((S, W), bf16),
            jax.ShapeDtypeStruct((H, qt, kb), f32),
        ],
        scratch_shapes=[pltpu.VMEM((ah, qt, kb), f32), pltpu.VMEM((S + pad, cw), f32), pltpu.VMEM((S + pad, cw), f32)],
        args=(proj, kp, vp, ext, dycat),
    )


def _bias_onehot(qt, ne, nrel_pad):
    m = lax.broadcasted_iota(jnp.int32, (nrel_pad, ne), 1)
    r = lax.broadcasted_iota(jnp.int32, (nrel_pad, ne), 0)
    rel = LEFT_CHUNKS * CHUNK + qt - 1 - m
    hot = (r == jnp.clip(rel, -REL_CLIP, REL_CLIP) + REL_CLIP) & (m < ne - 1)
    return hot.astype(f32)


def _bias_ext(rel_bias_pad, onehot):
    H, _, nr = rel_bias_pad.shape
    ne = onehot.shape[1]

    def body(rb_ref, oh_ref, o_ref):
        o_ref[0] = _dot(rb_ref[0], oh_ref[...], NN, HIGHEST)

    return pl.pallas_call(
        body,
        name="bias_ext",
        grid=(H,),
        in_specs=[pl.BlockSpec((1, 1, nr), lambda h: (h, 0, 0)), pl.BlockSpec((nr, ne), lambda h: (0, 0))],
        out_specs=pl.BlockSpec((1, 1, ne), lambda h: (h, 0, 0)),
        out_shape=jax.ShapeDtypeStruct((H, 1, ne), f32),
        compiler_params=pltpu.CompilerParams(dimension_semantics=("parallel",)),
    )(rel_bias_pad, onehot)


def _bias_bwd(dbias, onehot):
    H, qt, kb = dbias.shape
    nr, ne = onehot.shape

    def body(d_ref, oh_ref, o_ref):
        row = lax.broadcasted_iota(jnp.int32, (qt, qt), 0)
        col = lax.broadcasted_iota(jnp.int32, (qt, qt), 1)
        flip = (row + col == qt - 1).astype(f32)
        x = _dot(flip, d_ref[0], NN, HIGHEST)
        x = jnp.concatenate([x, jnp.zeros((qt, ne - kb), f32)], axis=1)
        de = jnp.sum(pltpu.roll(x, 0, 1, stride=1, stride_axis=0), axis=0, keepdims=True)
        o_ref[0] = _dot(de, oh_ref[...], NT, HIGHEST)

    return pl.pallas_call(
        body,
        name="bias_bwd",
        grid=(H,),
        in_specs=[pl.BlockSpec((1, qt, kb), lambda h: (h, 0, 0)), pl.BlockSpec((nr, ne), lambda h: (0, 0))],
        out_specs=pl.BlockSpec((1, 1, nr), lambda h: (h, 0, 0)),
        out_shape=jax.ShapeDtypeStruct((H, 1, nr), f32),
        compiler_params=pltpu.CompilerParams(dimension_semantics=("parallel",)),
    )(dbias, onehot)


def _place():
    x, y, c = lax.axis_index("x"), lax.axis_index("y"), lax.axis_index("c")
    return x, y, c


def _flip(x, y, k):
    return (1 - x if k & 2 else x), (1 - y if k & 1 else y)


def _region(ref, shard_shape, axis, j, half):
    R, C = shard_shape
    if axis == 0:
        if half is None:
            return ref.at[pl.ds(j * R, R), :]
        return ref.at[pl.ds(j * R + half * (R // 2), R // 2), :]
    if half is None:
        return ref.at[:, pl.ds(j * C, C)]
    return ref.at[pl.ds(half * (R // 2), R // 2), pl.ds(j * C, C)]


def _remote(src, dst, send_sem, recv_sem, dev):
    return pltpu.make_async_remote_copy(src_ref=src, dst_ref=dst, send_sem=send_sem, recv_sem=recv_sem,
                                        device_id=dev, device_id_type=MESH)


def _cast_place(shard, axis, where, *, name):
    R, C = shard.shape
    tr = _div(R, 256, 2 * SUBLANES)
    nr = R // tr
    full = (N_CHIPS * R, C) if axis == 0 else (R, N_CHIPS * C)
    omap = (lambda r, s: (s[0] * nr + r, 0)) if axis == 0 else (lambda r, s: (r, s[0]))

    def body(s_ref, x_ref, o_ref):
        o_ref[...] = x_ref[...].astype(bf16)

    return pl.pallas_call(
        body,
        name=name,
        grid_spec=pltpu.PrefetchScalarGridSpec(
            num_scalar_prefetch=1,
            grid=(nr,),
            in_specs=[pl.BlockSpec((tr, C), lambda r, s: (r, 0))],
            out_specs=pl.BlockSpec((tr, C), omap),
        ),
        out_shape=jax.ShapeDtypeStruct(full, bf16),
        compiler_params=pltpu.CompilerParams(dimension_semantics=("parallel",)),
    )(where, shard)


class _Rider(NamedTuple):
    ins: list
    out_shapes: list
    aliases: dict
    sems: list
    first: Callable
    last: Callable


def _run_rider(rider, *, name):
    ni, no = len(rider.ins), len(rider.out_shapes)

    def body(*refs):
        ins, outs, sems = refs[:ni], refs[ni:ni + no], refs[ni + no:]
        rider.first(ins, outs, sems)
        rider.last(ins, outs, sems)

    return pl.pallas_call(
        body,
        name=name,
        in_specs=[_hbm()] * ni,
        out_specs=[_hbm()] * no,
        out_shape=rider.out_shapes,
        input_output_aliases=rider.aliases,
        scratch_shapes=rider.sems,
    )(*rider.ins)


def _gather_rider(fulls, shard_shapes, axes, splits, rels):
    n = len(fulls)

    def stage1(ins, outs, sems):
        x, y, c = _place()
        jo = 2 * x + y
        cps = []
        for m in range(n):
            half = c if splits[m] else None
            for k in rels[m]:
                px, py = _flip(x, y, k)
                cps.append(_remote(_region(ins[m], shard_shapes[m], axes[m], jo, half), _region(outs[m], shard_shapes[m], axes[m], jo, half),
                                   sems[0].at[3 * m + k - 1], sems[1].at[3 * m + k - 1], (px, py, c)))
        return cps

    def first(ins, outs, sems):
        for cp in stage1(ins, outs, sems):
            cp.start()

    def last(ins, outs, sems):
        x, y, c = _place()
        send1, recv1, send2, recv2 = sems
        forwards = []
        for m in range(n):
            half = c if splits[m] else None
            for k in rels[m]:
                px, py = _flip(x, y, k)
                land = _region(outs[m], shard_shapes[m], axes[m], 2 * px + py, half)
                _remote(land, land, send1.at[3 * m + k - 1], recv1.at[3 * m + k - 1], (x, y, c)).wait_recv()
                if splits[m]:
                    fw = _remote(land, land, send2.at[3 * m + k - 1], recv2.at[3 * m + k - 1], (x, y, 1 - c))
                    fw.start()
                    forwards.append(fw)
        for m in range(n):
            if not splits[m]:
                continue
            for k in rels[m]:
                px, py = _flip(x, y, k)
                land = _region(outs[m], shard_shapes[m], axes[m], 2 * px + py, 1 - c)
                _remote(land, land, send2.at[3 * m + k - 1], recv2.at[3 * m + k - 1], (x, y, c)).wait_recv()
        for cp in stage1(ins, outs, sems) + forwards:
            cp.wait_send()

    sems = pltpu.SemaphoreType.DMA((3 * n,))
    return _Rider(list(fulls), [jax.ShapeDtypeStruct(f.shape, f.dtype) for f in fulls], {m: m for m in range(n)},
                  [sems, sems, sems, sems], first, last)


def _chips_rider(ts):
    n = len(ts)

    def copies(ins, outs, sems):
        x, y, c = _place()
        cps = []
        for m in range(n):
            for k in (1, 2, 3):
                px, py = _flip(x, y, k)
                cps.append(_remote(ins[m].at[2 * px + py], outs[m].at[k - 1], sems[0].at[3 * m + k - 1], sems[1].at[3 * m + k - 1], (px, py, c)))
        return cps

    def first(ins, outs, sems):
        for cp in copies(ins, outs, sems):
            cp.start()

    def last(ins, outs, sems):
        cps = copies(ins, outs, sems)
        for cp in cps:
            cp.wait_recv()
        for cp in cps:
            cp.wait_send()

    sems = pltpu.SemaphoreType.DMA((3 * n,))
    return _Rider(list(ts), [jax.ShapeDtypeStruct((3,) + t.shape[1:], t.dtype) for t in ts], {}, [sems, sems], first, last)


def _rs_sibling(dws, shard_shapes, axes, *, name):
    n = len(dws)

    def body(*refs):
        ins, got = refs[:n], refs[n:2 * n]
        send, recv = refs[2 * n:]
        x, y, c = _place()
        cps = []
        for m in range(n):
            for j in range(N_CHIPS):
                cp = _remote(_region(ins[m], shard_shapes[m], axes[m], j, 1 - c), got[m].at[j],
                             send.at[N_CHIPS * m + j], recv.at[N_CHIPS * m + j], (x, y, 1 - c))
                cp.start()
                cps.append(cp)
        for cp in cps:
            cp.wait_recv()
        for cp in cps:
            cp.wait_send()

    sems = pltpu.SemaphoreType.DMA((N_CHIPS * n,))
    return pl.pallas_call(
        body,
        name=name,
        in_specs=[_hbm()] * n,
        out_specs=[_hbm()] * n,
        out_shape=[jax.ShapeDtypeStruct((N_CHIPS, s[0] // 2, s[1]), d.dtype) for s, d in zip(shard_shapes, dws)],
        scratch_shapes=[sems, sems],
    )(*dws)


def _pair_sum(dw, got, shard_shape, axis, where, *, name):
    R, C = shard_shape
    hr = R // 2
    tr = _div(hr, 256, 2 * SUBLANES)
    nr = hr // tr
    dmap = (lambda j, r, s: ((2 * j + s[1]) * nr + r, 0)) if axis == 0 else (lambda j, r, s: (s[1] * nr + r, j))
    slot = pl.BlockSpec((1, tr, C), lambda j, r, s: (j, r, 0))

    def body(s_ref, d_ref, g_ref, o_ref):
        o_ref[0] = (d_ref[...].astype(f32) + g_ref[0].astype(f32)).astype(bf16)

    return pl.pallas_call(
        body,
        name=name,
        grid_spec=pltpu.PrefetchScalarGridSpec(
            num_scalar_prefetch=1,
            grid=(N_CHIPS, nr),
            in_specs=[pl.BlockSpec((tr, C), dmap), slot],
            out_specs=slot,
        ),
        out_shape=jax.ShapeDtypeStruct((N_CHIPS, hr, C), bf16),
        compiler_params=pltpu.CompilerParams(dimension_semantics=("parallel", "parallel")),
    )(where, dw, got)


def _chip_sum(pair, others, where, *, name):
    _, hr, C = pair.shape
    tr = _div(hr, 256, 2 * SUBLANES)
    nr = hr // tr

    def body(s_ref, p_ref, o_ref, g_ref):
        g_ref[...] = p_ref[0].astype(f32) + o_ref[0].astype(f32) + o_ref[1].astype(f32) + o_ref[2].astype(f32)

    return pl.pallas_call(
        body,
        name=name,
        grid_spec=pltpu.PrefetchScalarGridSpec(
            num_scalar_prefetch=1,
            grid=(nr,),
            in_specs=[pl.BlockSpec((1, tr, C), lambda r, s: (s[0], r, 0)), pl.BlockSpec((3, tr, C), lambda r, s: (0, r, 0))],
            out_specs=pl.BlockSpec((tr, C), lambda r, s: (s[1] * nr + r, 0)),
        ),
        out_shape=jax.ShapeDtypeStruct((2 * hr, C), f32),
        compiler_params=pltpu.CompilerParams(dimension_semantics=("parallel",)),
    )(where, pair, others)


def _rs_join(gs):
    n = len(gs)

    def body(*refs):
        ins, outs = refs[:n], refs[n:2 * n]
        send, recv = refs[2 * n:]
        x, y, c = _place()
        cps = []
        for m in range(n):
            hr = gs[m].shape[0] // 2
            cp = _remote(ins[m].at[pl.ds(c * hr, hr), :], outs[m].at[pl.ds(c * hr, hr), :], send.at[m], recv.at[m], (x, y, 1 - c))
            cp.start()
            cps.append(cp)
        for cp in cps:
            cp.wait_recv()
        for cp in cps:
            cp.wait_send()

    sems = pltpu.SemaphoreType.DMA((n,))
    return pl.pallas_call(
        body,
        name="grads_join_halves",
        in_specs=[_hbm()] * n,
        out_specs=[_hbm()] * n,
        out_shape=[jax.ShapeDtypeStruct(g.shape, g.dtype) for g in gs],
        input_output_aliases={m: m for m in range(n)},
        scratch_shapes=[sems, sems],
    )(*gs)


def _allreduce_small(v):
    rows = v.shape[0]

    def body(v_ref, o_ref, gath, send, recv):
        x, y, c = _place()
        me = 4 * x + 2 * y + c
        gath[pl.ds(me, 1)] = v_ref[...][None]
        cps = []
        for k in range(1, N_DEV):
            peer = (1 - x if k & 4 else x, 1 - y if k & 2 else y, 1 - c if k & 1 else c)
            cp = _remote(v_ref, gath.at[me], send.at[k - 1], recv.at[k - 1], peer)
            cp.start()
            cps.append(cp)
        for cp in cps:
            cp.wait_recv()
        for cp in cps:
            cp.wait_send()
        acc = gath[0]
        for d in range(1, N_DEV):
            acc = acc + gath[d]
        o_ref[...] = acc

    return pl.pallas_call(
        body,
        name="small_allreduce",
        in_specs=[_vmem()],
        out_specs=_vmem(),
        out_shape=jax.ShapeDtypeStruct(v.shape, f32),
        scratch_shapes=[pltpu.VMEM((N_DEV, rows, LANES), f32), pltpu.SemaphoreType.DMA((N_DEV - 1,)), pltpu.SemaphoreType.DMA((N_DEV - 1,))],
    )(v)


def _adamw_math(w, g, m, v):
    m = ADAM_B1 * m + (1.0 - ADAM_B1) * g
    v = ADAM_B2 * v + (1.0 - ADAM_B2) * (g * g)
    m_hat = m / (1.0 - ADAM_B1 ** ADAM_STEP)
    v_hat = v / (1.0 - ADAM_B2 ** ADAM_STEP)
    delta = -ADAM_LR * (m_hat / (jnp.sqrt(v_hat) + ADAM_EPS) + ADAM_WD * w)
    return delta, m, v


def _adamw(w, g, m, v, *, name):
    R, C = w.shape
    tr = _div(R, 128, SUBLANES)

    def body(w_ref, g_ref, m_ref, v_ref, d_ref, mo_ref, vo_ref):
        d, mn, vn = _adamw_math(w_ref[...], g_ref[...], m_ref[...], v_ref[...])
        d_ref[...] = d
        mo_ref[...] = mn
        vo_ref[...] = vn

    spec = pl.BlockSpec((tr, C), lambda r: (r, 0))
    return pl.pallas_call(
        body,
        name=name,
        grid=(R // tr,),
        in_specs=[spec] * 4,
        out_specs=[spec] * 3,
        out_shape=[jax.ShapeDtypeStruct((R, C), f32)] * 3,
        compiler_params=pltpu.CompilerParams(dimension_semantics=("parallel",)),
    )(w, g, m, v)


def _pack(arrs):
    flat = jnp.concatenate([a.reshape(-1).astype(f32) for a in arrs])
    tile = SUBLANES * LANES
    total = -(-flat.shape[0] // tile) * tile
    return jnp.pad(flat, (0, total - flat.shape[0])).reshape(total // LANES, LANES)


def _unpack(buf, shapes):
    flat = buf.reshape(-1)
    out, off = [], 0
    for s in shapes:
        size = int(np.prod(s))
        out.append(flat[off:off + size].reshape(s))
        off += size
    return out


def kernel(x, p, norm_mix, w_in, lb_logits, hg_norm, rel_bias, w_out, norm_ffn, w_up, conv_w, conv_b, w_down, norm_ple, w_ple_gate, w_ple_proj, final_norm, loss_target, m_norm_mix, m_w_in, m_lb_logits, m_hg_norm, m_rel_bias, m_w_out, m_norm_ffn, m_w_up, m_conv_w, m_conv_b, m_w_down, m_norm_ple, m_w_ple_gate, m_w_ple_proj, m_final_norm, v_norm_mix, v_w_in, v_lb_logits, v_hg_norm, v_rel_bias, v_w_out, v_norm_ffn, v_w_up, v_conv_w, v_conv_b, v_w_down, v_norm_ple, v_w_ple_gate, v_w_ple_proj, v_final_norm):
    S, D = x.shape[1], x.shape[2]
    xv, pv, tgt = x[0], p[0, 0], loss_target[0]
    W = (w_in.shape[2] * N_CHIPS) // 7
    H = W // HEAD_DIM
    F = w_down.shape[1] * N_CHIPS
    tc = _div(F // 2, 1408, LANES)
    nj = F // tc
    jx, jy = lax.axis_index("x"), lax.axis_index("y")
    chip = 2 * jx + jy

    big = [w_in[0], w_out[0], w_up[0], w_down[0], w_ple_gate[0], w_ple_proj[0]]
    big_axes = [1, 0, 1, 0, 0, 1]
    where = jnp.stack([chip, lax.axis_index("c")]).astype(jnp.int32)
    cw_pad = jnp.pad(conv_w[0], ((0, SUBLANES - conv_w.shape[1]), (0, 0)))
    cw_mine = lax.dynamic_update_slice(jnp.zeros((N_CHIPS * SUBLANES, cw_pad.shape[1]), f32), cw_pad, (chip * SUBLANES, 0))
    placed = [_cast_place(b, ax, where, name=f"cast_place_{i}") for i, (b, ax) in enumerate(zip(big, big_axes))]
    shard_shapes = [b.shape for b in big]

    every = (1, 2, 3)

    def gather(items):
        return _gather_rider([b for b, _, _ in items],
                             [cw_pad.shape if i is None else shard_shapes[i] for _, i, _ in items],
                             [0 if i is None else big_axes[i] for _, i, _ in items],
                             [i is not None for _, i, _ in items], [r for _, _, r in items])

    (W_in,) = _run_rider(gather([(placed[0], 0, every)]), name="gather_w_in")

    tm = _div(S, 1024, LANES)

    a1 = _rms_fwd(xv, norm_mix, name="rms_mix")
    proj, W_up_part = _mm(a1, W_in, dims="nn", tm=tm, tn=_div(7 * W, 512, LANES), tk=D, out_dtype=f32, name="mm_in",
                          rider=gather([(placed[2], 2, (1, 2))]))
    y_hg, o_hg, st, W_out = _hgrn_fwd(proj, lb_logits, hg_norm, W, rider=gather([(placed[1], 1, every)]))

    qt, pad, kb, ne = _att_dims(S)
    nrel = rel_bias.shape[2]
    nrel_pad = -(-nrel // LANES) * LANES
    onehot = _bias_onehot(qt, ne, nrel_pad)
    rb_pad = jnp.pad(rel_bias[0], ((0, 0), (0, nrel_pad - nrel)))[:, None, :]
    ext = _bias_ext(rb_pad, onehot)
    kp, vp = _kv_padded(proj, W, pad)
    y_att, W_up = _attn_fwd(proj, kp, vp, ext, W, rider=gather([(W_up_part, 2, (3,))]))
    ycat = jnp.concatenate([y_hg, y_att], axis=1)

    tn_d = _div(D, 512, LANES)
    h1, W_pg, W_pp, cw_all = _mm(ycat, W_out, dims="nn", tm=tm, tn=tn_d, tk=D, out_dtype=f32, res=xv, name="mm_out",
                                 rider=gather([(placed[4], 4, every), (placed[5], 5, every), (cw_mine, None, every)]))
    a2 = _rms_fwd(h1, norm_ffn, name="rms_ffn")
    perm = lambda b: (b % 2) * nj + b // 2
    u, W_down = _mm(a2, W_up, dims="nn", tm=tm, tn=tc, tk=D, out_dtype=f32, name="mm_up", b_idx=lambda i, j, k: (k, perm(j)),
                    rider=gather([(placed[3], 3, every)]))
    ncw = conv_w.shape[1]
    cw_full = jnp.transpose(cw_all.reshape(N_CHIPS, SUBLANES, -1)[:, :ncw], (1, 0, 2)).reshape(ncw, -1)
    cb = conv_b
    mact = _convact_fwd(u, cw_full, cb, tc)
    tn_w = _div(D, 1024, LANES)
    h2 = _mm(mact, W_down, dims="nn", tm=_div(S, 512, LANES), tn=tn_w, tk=F, out_dtype=f32, res=h1, name="mm_down", cols_outer=True)
    a3 = _rms_fwd(h2, norm_ple, name="rms_ple")
    zg = _mm(a3, W_pg, dims="nn", tm=tm, tn=tn_d, tk=D, out_dtype=f32, name="mm_ple_gate")
    p16 = pv.astype(bf16)
    pp = _mm(p16, W_pp, dims="nn", tm=tm, tn=tn_d, tk=pv.shape[1], out_dtype=f32, name="mm_ple_proj")
    dh3, dzg, dpp, loss_part, d_fn = _tail(h2, zg, pp, tgt, final_norm.reshape(1, D))

    pair, others = {}, {}

    def reduce_start(idx, grads):
        got = _rs_sibling(grads, [shard_shapes[i] for i in idx], [big_axes[i] for i in idx], name="grads_to_sibling_" + "_".join(map(str, idx)))
        for i, d, g in zip(idx, grads, got):
            pair[i] = _pair_sum(d, g, shard_shapes[i], big_axes[i], where, name=f"grad_pair_sum_{i}")
        return _chips_rider([pair[i] for i in idx])

    def reduce_landed(idx, landed):
        for i, o in zip(idx, landed):
            others[i] = o

    tk_s = S
    dW_pp = _mm(p16, dpp, dims="tn", tm=pv.shape[1], tn=tn_w, tk=tk_s, out_dtype=bf16, name="mm_d_ple_proj")
    dW_pg = _mm(a3, dzg, dims="tn", tm=tn_w, tn=tn_w, tk=tk_s, out_dtype=bf16, name="mm_d_ple_gate")
    da3 = _mm(dzg, W_pg, dims="nt", tm=tm, tn=_div(D, 1024, LANES), tk=D, out_dtype=f32, name="mm_da3")
    dh2, dh2b, d_nple = _rms_bwd(h2, norm_ple, da3, dh3, name="rms_ple_bwd")
    dm, *landed = _mm(dh2b, W_down, dims="nt", tm=tm, tn=_div(F, 512, LANES), tk=D, out_dtype=f32, name="mm_dm",
                      rider=reduce_start([4, 5], [dW_pg, dW_pp]))
    reduce_landed([4, 5], landed)
    dW_down = _mm(mact, dh2b, dims="tn", tm=_div(F, 1408, LANES), tn=tn_d, tk=tk_s, out_dtype=bf16, name="mm_d_down")
    du, dcw_g, dcw_v, dcb_g, dcb_v, *landed = _convact_bwd(u, dm, cw_full, cb, tc, rider=reduce_start([3], [dW_down]))
    reduce_landed([3], landed)
    dW_up = _mm(a2, du, dims="tn", tm=_div(D, 1024, LANES), tn=tc, tk=tk_s, out_dtype=bf16, name="mm_d_up",
                o_idx=lambda i, j, k: (i, perm(j)))
    da2, *landed = _mm(du, W_up, dims="nt", tm=tm, tn=D, tk=tc, out_dtype=f32, name="mm_da2",
                       b_idx=lambda i, j, k: (j, perm(k)), rider=reduce_start([2], [dW_up]))
    reduce_landed([2], landed)
    dh1, dh1b, d_nffn = _rms_bwd(h1, norm_ffn, da2, dh2, name="rms_ffn_bwd")
    dycat = _mm(dh1b, W_out, dims="nt", tm=tm, tn=_div(D, 1024, LANES), tk=D, out_dtype=f32, name="mm_dycat")
    dW_out = _mm(ycat, dh1b, dims="tn", tm=tn_w, tn=tn_w, tk=tk_s, out_dtype=bf16, name="mm_d_out")
    dq_att, dk_att, dv_att, dbias, *landed = _attn_bwd(proj, kp, vp, ext, dycat, W, rider=reduce_start([1], [dW_out]))
    reduce_landed([1], landed)
    d_hg4, d_lbl, d_hgn = _hgrn_bwd(proj, o_hg, st, dycat, lb_logits, hg_norm, W)
    d_rb = _bias_bwd(dbias, onehot)[:, 0, :nrel]
    dproj = jnp.concatenate([d_hg4, dq_att, dk_att, dv_att], axis=1)
    dW_in = _mm(a1, dproj, dims="tn", tm=tn_w, tn=_div(7 * W, 1024, LANES), tk=tk_s, out_dtype=bf16, name="mm_d_in")
    da1, *landed = _mm(dproj, W_in, dims="nt", tm=tm, tn=D, tk=_div(7 * W, 1792, LANES), out_dtype=f32, name="mm_da1",
                       rider=reduce_start([0], [dW_in]))
    reduce_landed([0], landed)
    grad_x, _, d_nmix = _rms_bwd(xv, norm_mix, da1, dh1, name="rms_mix_bwd")

    d_cw = jnp.concatenate([dcw_g, dcw_v], axis=1)
    d_cb = jnp.concatenate([dcb_g, dcb_v], axis=1)
    small_parts = [loss_part[:, :1], d_nmix, d_lbl, d_hgn, d_rb, d_nffn, d_cw, d_cb, d_nple, d_fn]
    small_shapes = [(1, 1), (1, D), lb_logits.shape, hg_norm.shape, (H, nrel), (1, D), (ncw, 2 * F), (1, 2 * F), (1, D), (1, D)]
    red = _unpack(_allreduce_small(_pack(small_parts)), small_shapes)
    loss = red[0].reshape(())
    g_nmix, g_lbl, g_hgn, g_rb, g_nffn, g_cw_all, g_cb, g_nple, g_fn = red[1:]
    csh = conv_w.shape[2]
    g_cw = lax.dynamic_slice(g_cw_all, (0, chip * csh), (ncw, csh))
    small_g = [g_nmix, g_lbl, g_hgn, g_rb[None], g_nffn, g_cw[None], g_cb, g_nple, g_fn.reshape(D)]
    small_w = [norm_mix, lb_logits, hg_norm, rel_bias, norm_ffn, conv_w, conv_b, norm_ple, final_norm]
    small_m = [m_norm_mix, m_lb_logits, m_hg_norm, m_rel_bias, m_norm_ffn, m_conv_w, m_conv_b, m_norm_ple, m_final_norm]
    small_v = [v_norm_mix, v_lb_logits, v_hg_norm, v_rel_bias, v_norm_ffn, v_conv_w, v_conv_b, v_norm_ple, v_final_norm]
    shapes_s = [w.shape for w in small_w]
    sd, sm, sv = _adamw(_pack(small_w), _pack(small_g), _pack(small_m), _pack(small_v), name="adamw_small")
    small_g = [g.reshape(s) for g, s in zip(small_g, shapes_s)]
    small_d, small_nm, small_nv = _unpack(sd, shapes_s), _unpack(sm, shapes_s), _unpack(sv, shapes_s)

    halves = [_chip_sum(pair[i], others[i], where, name=f"grad_chip_sum_{i}") for i in range(6)]
    g_big = _rs_join(halves)
    big_m = [m_w_in[0], m_w_out[0], m_w_up[0], m_w_down[0], m_w_ple_gate[0], m_w_ple_proj[0]]
    big_v = [v_w_in[0], v_w_out[0], v_w_up[0], v_w_down[0], v_w_ple_gate[0], v_w_ple_proj[0]]
    big_d, big_nm, big_nv = [], [], []
    for i in range(6):
        d_, m_, v_ = _adamw(big[i], g_big[i], big_m[i], big_v[i], name=f"adamw_{i}")
        big_d.append(d_[None])
        big_nm.append(m_[None])
        big_nv.append(v_[None])
    g_big = [g[None] for g in g_big]

    def order(sm_list, bg_list):
        s, b = sm_list, bg_list
        return [s[0], b[0], s[1], s[2], s[3], b[1], s[4], b[2], s[5], s[6], b[3], s[7], b[4], b[5], s[8]]

    return (loss, grad_x[None], *order(small_g, g_big), *order(small_d, big_d), *order(small_nm, big_nm), *order(small_nv, big_nv))
```

```python
import functools
from typing import Callable, NamedTuple

import jax
import jax.numpy as jnp
import numpy as np
from jax import lax
from jax.experimental import pallas as pl
from jax.experimental.pallas import tpu as pltpu

f32 = jnp.float32
bf16 = jnp.bfloat16

CHUNK = 64
HEAD_DIM = 128
LEFT_CHUNKS = 8
REL_CLIP = 128
EPS = 1e-6
HG_SUB = 16
HG_HEADS_PER_STEP = 8
ATT_Q_ROWS = 256
ATT_HEADS_PER_STEP = 2
ADAM_LR, ADAM_B1, ADAM_B2, ADAM_EPS, ADAM_WD, ADAM_STEP = 0.001, 0.9, 0.999, 1e-08, 0.01, 10
LANES = 128
SUBLANES = 8
N_CHIPS = 4
N_DEV = 8
MESH = pl.DeviceIdType.MESH
NEG = float(np.finfo(np.float32).min)

NN = (((1,), (0,)), ((), ()))
NT = (((1,), (1,)), ((), ()))
TN = (((0,), (0,)), ((), ()))
HIGHEST = lax.Precision.HIGHEST


def _dot(a, b, dims, precision=None):
    return lax.dot_general(a, b, dims, preferred_element_type=f32, precision=precision)


def _sigmoid(v):
    return 1.0 / (1.0 + jnp.exp(-v))


def _div(n, pref, mult):
    best = None
    d = mult
    while d <= min(n, pref):
        if n % d == 0:
            best = d
        d += mult
    return best if best is not None else n


def _hbm():
    return pl.BlockSpec(memory_space=pltpu.HBM)


def _vmem():
    return pl.BlockSpec(memory_space=pltpu.VMEM)


def _mm(a, b, *, dims, tm, tn, tk, out_dtype, name, res=None, a_idx=None, b_idx=None, o_idx=None, out_shape=None, rider=None,
        cols_outer=False):
    if dims == "nn":
        (M, K), (_, N) = a.shape, b.shape
        a_blk, b_blk, dn = (tm, tk), (tk, tn), NN
        a_def, b_def = (lambda i, j, k: (i, k)), (lambda i, j, k: (k, j))
    elif dims == "nt":
        (M, K), (N, _) = a.shape, b.shape
        a_blk, b_blk, dn = (tm, tk), (tn, tk), NT
        a_def, b_def = (lambda i, j, k: (i, k)), (lambda i, j, k: (j, k))
    else:
        (K, M), (_, N) = a.shape, b.shape
        a_blk, b_blk, dn = (tk, tm), (tk, tn), TN
        a_def, b_def = (lambda i, j, k: (k, i)), (lambda i, j, k: (k, j))
    assert M % tm == 0 and N % tn == 0 and K % tk == 0, (name, M, N, K, tm, tn, tk)
    nk = K // tk
    has_res = res is not None

    def body(*refs):
        a_ref, b_ref = refs[0], refs[1]
        res_ref = refs[2] if has_res else None
        o_ref = refs[3] if has_res else refs[2]
        part = _dot(a_ref[...].astype(bf16), b_ref[...].astype(bf16), dn)

        def finish(acc):
            if has_res:
                acc = acc + res_ref[...]
            o_ref[...] = acc.astype(out_dtype)

        if nk == 1:
            finish(part)
        else:
            acc_ref = refs[-1]
            k = pl.program_id(2)

            @pl.when(k == 0)
            def _():
                acc_ref[...] = part

            @pl.when(k > 0)
            def _():
                acc_ref[...] += part

            @pl.when(k == nk - 1)
            def _():
                finish(acc_ref[...])

    def ordered(f):
        return (lambda j, i, k: f(i, j, k)) if cols_outer else f

    o_map = ordered(o_idx or (lambda i, j, k: (i, j)))
    in_specs = [pl.BlockSpec(a_blk, ordered(a_idx or a_def)), pl.BlockSpec(b_blk, ordered(b_idx or b_def))]
    args = [a, b]
    if has_res:
        in_specs.append(pl.BlockSpec((tm, tn), o_map))
        args.append(res)
    outs = _grid_call(
        body,
        rider,
        name=name,
        grid=(N // tn, M // tm, nk) if cols_outer else (M // tm, N // tn, nk),
        in_specs=in_specs,
        out_specs=[pl.BlockSpec((tm, tn), o_map)],
        out_shape=[jax.ShapeDtypeStruct(out_shape or (M, N), out_dtype)],
        scratch_shapes=[pltpu.VMEM((tm, tn), f32)] if nk > 1 else [],
        args=args,
    )
    return outs[0] if rider is None else outs


def _rms_fwd(xv, g, *, name):
    S, D = xv.shape
    ts = _div(S, 512, SUBLANES)

    def body(x_ref, g_ref, o_ref):
        v = x_ref[...]
        r = lax.rsqrt(jnp.mean(v * v, axis=-1, keepdims=True) + EPS)
        o_ref[...] = (v * r * g_ref[...]).astype(bf16)

    return pl.pallas_call(
        body,
        name=name,
        grid=(S // ts,),
        in_specs=[pl.BlockSpec((ts, D), lambda i: (i, 0)), pl.BlockSpec((1, D), lambda i: (0, 0))],
        out_specs=pl.BlockSpec((ts, D), lambda i: (i, 0)),
        out_shape=jax.ShapeDtypeStruct((S, D), bf16),
        compiler_params=pltpu.CompilerParams(dimension_semantics=("parallel",)),
    )(xv, g)


def _rms_bwd(xv, g, dy, dres, *, name):
    S, D = xv.shape
    ts = _div(S, 256, SUBLANES)

    def body(x_ref, g_ref, dy_ref, dres_ref, dx_ref, dxb_ref, dg_ref):
        i = pl.program_id(0)
        v = x_ref[...]
        r = lax.rsqrt(jnp.mean(v * v, axis=-1, keepdims=True) + EPS)
        vn = v * r
        d = dy_ref[...]
        part = jnp.sum(d * vn, axis=0, keepdims=True)

        @pl.when(i == 0)
        def _():
            dg_ref[...] = part

        @pl.when(i > 0)
        def _():
            dg_ref[...] += part

        t = d * g_ref[...]
        dx = dres_ref[...] + r * (t - vn * jnp.mean(t * vn, axis=-1, keepdims=True))
        dx_ref[...] = dx
        dxb_ref[...] = dx.astype(bf16)

    row = pl.BlockSpec((ts, D), lambda i: (i, 0))
    vec = pl.BlockSpec((1, D), lambda i: (0, 0))
    return pl.pallas_call(
        body,
        name=name,
        grid=(S // ts,),
        in_specs=[row, vec, row, row],
        out_specs=[row, row, vec],
        out_shape=[jax.ShapeDtypeStruct((S, D), f32), jax.ShapeDtypeStruct((S, D), bf16), jax.ShapeDtypeStruct((1, D), f32)],
        compiler_params=pltpu.CompilerParams(dimension_semantics=("arbitrary",)),
    )(xv, g, dy, dres)


def _tail(h2, zg, pp, tgt, fn):
    S, D = h2.shape
    ts = _div(S, 256, SUBLANES)

    def body(h_ref, z_ref, p_ref, t_ref, fn_ref, dh_ref, dz_ref, dp_ref, loss_ref, dfn_ref):
        i = pl.program_id(0)
        sg = _sigmoid(z_ref[...])
        ppv = p_ref[...]
        h3 = h_ref[...] + sg * ppv
        r = lax.rsqrt(jnp.mean(h3 * h3, axis=-1, keepdims=True) + EPS)
        hn = h3 * r
        fnv = fn_ref[...]
        e = hn * fnv - t_ref[...]
        lpart = 0.5 * jnp.sum(jnp.mean(e * e, axis=-1, keepdims=True), axis=0, keepdims=True)
        dy = e * (1.0 / D)
        dpart = jnp.sum(dy * hn, axis=0, keepdims=True)

        @pl.when(i == 0)
        def _():
            loss_ref[...] = jnp.broadcast_to(lpart, loss_ref.shape)
            dfn_ref[...] = dpart

        @pl.when(i > 0)
        def _():
            loss_ref[...] += jnp.broadcast_to(lpart, loss_ref.shape)
            dfn_ref[...] += dpart

        t = dy * fnv
        dh3 = r * (t - hn * jnp.mean(t * hn, axis=-1, keepdims=True))
        dh_ref[...] = dh3
        dz_ref[...] = (dh3 * ppv * sg * (1.0 - sg)).astype(bf16)
        dp_ref[...] = (dh3 * sg).astype(bf16)

    row = pl.BlockSpec((ts, D), lambda i: (i, 0))
    vec = pl.BlockSpec((1, D), lambda i: (0, 0))
    one = pl.BlockSpec((1, LANES), lambda i: (0, 0))
    return pl.pallas_call(
        body,
        name="ple_final_loss",
        grid=(S // ts,),
        in_specs=[row, row, row, row, vec],
        out_specs=[row, row, row, one, vec],
        out_shape=[
            jax.ShapeDtypeStruct((S, D), f32),
            jax.ShapeDtypeStruct((S, D), bf16),
            jax.ShapeDtypeStruct((S, D), bf16),
            jax.ShapeDtypeStruct((1, LANES), f32),
            jax.ShapeDtypeStruct((1, D), f32),
        ],
        compiler_params=pltpu.CompilerParams(dimension_semantics=("arbitrary",)),
    )(h2, zg, pp, tgt, fn)


def _shift_down(v, halo, k):
    r = pltpu.roll(v, k, 0)
    hr = pltpu.roll(halo, k, 0)
    row = lax.broadcasted_iota(jnp.int32, hr.shape, 0)
    top = jnp.where(row < k, hr, r[0:SUBLANES])
    return top if v.shape[0] == SUBLANES else jnp.concatenate([top, r[SUBLANES:]], axis=0)


def _shift_up(v, halo, k):
    n = v.shape[0]
    r = pltpu.roll(v, n - k, 0)
    hr = pltpu.roll(halo, SUBLANES - k, 0)
    row = lax.broadcasted_iota(jnp.int32, hr.shape, 0)
    bot = jnp.where(row >= SUBLANES - k, hr, r[n - SUBLANES:])
    return jnp.concatenate([r[: n - SUBLANES], bot], axis=0)


def _conv_specs(S, F, tc, ts):
    nj = F // tc
    rows8 = ts // SUBLANES
    main = pl.BlockSpec((ts, 2 * tc), lambda j, i: (i, j))
    prev = pl.BlockSpec((SUBLANES, 2 * tc), lambda j, i: (jnp.maximum(i * rows8 - 1, 0), j))
    nxt = pl.BlockSpec((SUBLANES, 2 * tc), lambda j, i: (jnp.minimum((i + 1) * rows8, S // SUBLANES - 1), j))
    wg = pl.BlockSpec((3, tc), lambda j, i: (0, j))
    wv = pl.BlockSpec((3, tc), lambda j, i: (0, nj + j))
    bg = pl.BlockSpec((1, tc), lambda j, i: (0, j))
    bv = pl.BlockSpec((1, tc), lambda j, i: (0, nj + j))
    half = pl.BlockSpec((ts, tc), lambda j, i: (i, j))
    return nj, main, prev, nxt, wg, wv, bg, bv, half


def _conv_pre(u_ref, h_ref, wg_ref, wv_ref, bg_ref, bv_ref, tc):
    i = pl.program_id(1)
    u = u_ref[...]
    halo = jnp.where(i == 0, 0.0, h_ref[...])
    u1 = _shift_down(u, halo, 1)
    u2 = _shift_down(u, halo, 2)
    w = jnp.concatenate([wg_ref[...], wv_ref[...]], axis=1)
    b = jnp.concatenate([bg_ref[...], bv_ref[...]], axis=1)
    uc = b + w[0:1] * u2 + w[1:2] * u1 + w[2:3] * u
    return u, u1, u2, uc[:, :tc], uc[:, tc:]


def _convact_fwd(u, cw, cb, tc):
    S, F2 = u.shape
    F = F2 // 2
    ts = _div(S, 512, SUBLANES)
    nj, main, prev, nxt, wg, wv, bg, bv, half = _conv_specs(S, F, tc, ts)

    def body(u_ref, h_ref, wg_ref, wv_ref, bg_ref, bv_ref, m_ref):
        _, _, _, g, v = _conv_pre(u_ref, h_ref, wg_ref, wv_ref, bg_ref, bv_ref, tc)
        m_ref[...] = (g * _sigmoid(g) * v).astype(bf16)

    return pl.pallas_call(
        body,
        name="convact_fwd",
        grid=(nj, S // ts),
        in_specs=[main, prev, wg, wv, bg, bv],
        out_specs=half,
        out_shape=jax.ShapeDtypeStruct((S, F), bf16),
        compiler_params=pltpu.CompilerParams(dimension_semantics=("parallel", "parallel")),
    )(u, u, cw, cw, cb, cb)


def _convact_bwd(u, dm, cw, cb, tc, rider=None):
    S, F2 = u.shape
    F = F2 // 2
    ts = _div(S, 512, SUBLANES)
    nj, main, prev, nxt, wg, wv, bg, bv, half = _conv_specs(S, F, tc, ts)
    ni = S // ts
    half_nxt = pl.BlockSpec((SUBLANES, tc), lambda j, i: (jnp.minimum((i + 1) * (ts // SUBLANES), S // SUBLANES - 1), j))

    def act_bwd(g, v, d):
        sg = _sigmoid(g)
        return jnp.concatenate([d * v * sg * (1.0 + g * (1.0 - sg)), d * g * sg], axis=1)

    def body(u_ref, h_ref, n_ref, dm_ref, dmn_ref, wg_ref, wv_ref, bg_ref, bv_ref, du_ref, dwg_ref, dwv_ref, dbg_ref, dbv_ref):
        i = pl.program_id(1)
        u0, u1, u2, g, v = _conv_pre(u_ref, h_ref, wg_ref, wv_ref, bg_ref, bv_ref, tc)
        duc = act_bwd(g, v, dm_ref[...])
        w = jnp.concatenate([wg_ref[...], wv_ref[...]], axis=1)
        b = jnp.concatenate([bg_ref[...], bv_ref[...]], axis=1)
        un = n_ref[...]
        tail = u0[ts - SUBLANES:]
        ucn = b + w[0:1] * _shift_down(un, tail, 2) + w[1:2] * _shift_down(un, tail, 1) + w[2:3] * un
        ducn = jnp.where(i == ni - 1, 0.0, act_bwd(ucn[:, :tc], ucn[:, tc:], dmn_ref[...]))
        du_ref[...] = (w[2:3] * duc + w[1:2] * _shift_up(duc, ducn, 1) + w[0:1] * _shift_up(duc, ducn, 2)).astype(bf16)
        db = jnp.sum(duc, axis=0, keepdims=True)
        dw = jnp.concatenate(
            [jnp.sum(duc * u2, axis=0, keepdims=True), jnp.sum(duc * u1, axis=0, keepdims=True), jnp.sum(duc * u0, axis=0, keepdims=True)],
            axis=0,
        )

        @pl.when(i == 0)
        def _():
            dwg_ref[...] = dw[:, :tc]
            dwv_ref[...] = dw[:, tc:]
            dbg_ref[...] = db[:, :tc]
            dbv_ref[...] = db[:, tc:]

        @pl.when(i > 0)
        def _():
            dwg_ref[...] += dw[:, :tc]
            dwv_ref[...] += dw[:, tc:]
            dbg_ref[...] += db[:, :tc]
            dbv_ref[...] += db[:, tc:]

    w_out = pl.BlockSpec((3, tc), lambda j, i: (0, j))
    b_out = pl.BlockSpec((1, tc), lambda j, i: (0, j))
    return _grid_call(
        body,
        rider,
        name="convact_bwd",
        grid=(nj, ni),
        in_specs=[main, prev, nxt, half, half_nxt, wg, wv, bg, bv],
        out_specs=[main, w_out, w_out, b_out, b_out],
        out_shape=[
            jax.ShapeDtypeStruct((S, F2), bf16),
            jax.ShapeDtypeStruct((3, F), f32),
            jax.ShapeDtypeStruct((3, F), f32),
            jax.ShapeDtypeStruct((1, F), f32),
            jax.ShapeDtypeStruct((1, F), f32),
        ],
        scratch_shapes=[],
        args=(u, u, u, dm, dm, cw, cw, cb, cb),
    )


def _hg_gates(qp, fp, lbl):
    lb = _sigmoid(lbl[0:1] - lbl[1:2])
    sig = _sigmoid(fp)
    sigm = _sigmoid(-fp)
    f = lb + (1.0 - lb) * sig
    k = (1.0 - lb) * sigm
    sq = _sigmoid(qp)
    qf = qp * sq
    row = lax.broadcasted_iota(jnp.int32, (CHUNK, CHUNK), 0)
    col = lax.broadcasted_iota(jnp.int32, (CHUNK, CHUNK), 1)
    b = _dot((row >= col).astype(f32), jnp.log(f), NN, HIGHEST)
    return lb, sig, sigm, f, k, sq, qf, b


def _hg_block(qf, k, b, I):
    r0, n = I * HG_SUB, (I + 1) * HG_SUB
    base = b[r0 - 1:r0] if I > 0 else jnp.zeros_like(b[0:1])
    eq = jnp.exp(b[r0:n] - base)
    ek = jnp.exp(base - b[0:n])
    qt16 = (qf[r0:n] * eq).astype(bf16)
    kt16 = (k[0:n] * ek).astype(bf16)
    row = lax.broadcasted_iota(jnp.int32, (HG_SUB, n), 0) + r0
    col = lax.broadcasted_iota(jnp.int32, (HG_SUB, n), 1)
    return r0, n, eq, ek, qt16, kt16, col <= row


def _per_head(v, hp, fn):
    return jnp.concatenate([fn(v[:, p * HEAD_DIM:(p + 1) * HEAD_DIM]) for p in range(hp)], axis=1)


def _head_mean(v, hp):
    return _per_head(v, hp, lambda t: jnp.broadcast_to(jnp.mean(t, axis=-1, keepdims=True), t.shape))


def _pad_rows(v, rows):
    return v if v.shape[0] == rows else jnp.concatenate([v, jnp.zeros((rows - v.shape[0], v.shape[1]), v.dtype)], axis=0)


def _hgrn_specs(S, W, hp, reverse):
    nc = S // CHUNK
    ngrp = W // (hp * HEAD_DIM)
    cw = hp * HEAD_DIM
    cidx = (lambda c: nc - 1 - c) if reverse else (lambda c: c)

    def proj(off):
        return pl.BlockSpec((CHUNK, cw), lambda h, c: (cidx(c), off * ngrp + h))

    act = pl.BlockSpec((CHUNK, cw), lambda h, c: (cidx(c), h))
    lbl = pl.BlockSpec((2, cw), lambda h, c: (0, h))
    gn = pl.BlockSpec((1, HEAD_DIM), lambda h, c: (0, 0))
    st = pl.BlockSpec((hp, 1, HEAD_DIM, HEAD_DIM), lambda h, c: (h, cidx(c), 0, 0))
    return nc, ngrp, proj, act, lbl, gn, st


def _grid_call(body, rider, *, name, grid, in_specs, out_specs, out_shape, scratch_shapes, args):
    params = pltpu.CompilerParams(dimension_semantics=("arbitrary",) * len(grid))
    if rider is None:
        return pl.pallas_call(body, name=name, grid=grid, in_specs=in_specs, out_specs=out_specs, out_shape=out_shape,
                              scratch_shapes=scratch_shapes, compiler_params=params)(*args)
    n_in, n_out, n_scr = len(in_specs), len(out_specs), len(scratch_shapes)
    ri, ro = len(rider.ins), len(rider.out_shapes)

    def wrapped(*refs):
        a, b = n_in, n_in + ri
        c, d = b + n_out, b + n_out + ro
        e = d + n_scr
        ids = [pl.program_id(t) for t in range(len(grid))]
        is_first = functools.reduce(jnp.logical_and, [i == 0 for i in ids])
        is_last = functools.reduce(jnp.logical_and, [i == g - 1 for i, g in zip(ids, grid)])

        @pl.when(is_first)
        def _():
            rider.first(refs[a:b], refs[c:d], refs[e:])

        body(*refs[:a], *refs[b:c], *refs[d:e])

        @pl.when(is_last)
        def _():
            rider.last(refs[a:b], refs[c:d], refs[e:])

    return pl.pallas_call(
        wrapped,
        name=name,
        grid=grid,
        in_specs=list(in_specs) + [_hbm()] * ri,
        out_specs=list(out_specs) + [_hbm()] * ro,
        out_shape=list(out_shape) + list(rider.out_shapes),
        input_output_aliases={n_in + i: n_out + o for i, o in rider.aliases.items()},
        scratch_shapes=list(scratch_shapes) + list(rider.sems),
        compiler_params=params,
    )(*args, *rider.ins)


def _hgrn_fwd(proj, lb_logits, hg_norm, W, rider=None):
    S = proj.shape[0]
    hp = min(HG_HEADS_PER_STEP, W // HEAD_DIM)
    nc, ngrp, pspec, act, lbl_spec, gn_spec, st_spec = _hgrn_specs(S, W, hp, False)
    nb = CHUNK // HG_SUB

    def body(q_ref, f_ref, i_ref, g_ref, lbl_ref, gn_ref, y_ref, o_ref, st_ref, state):
        c = pl.program_id(1)

        @pl.when(c == 0)
        def _():
            state[...] = jnp.zeros_like(state)

        _, _, _, _, k, _, qf, b = _hg_gates(q_ref[...], f_ref[...], lbl_ref[...])
        v16 = i_ref[...].astype(bf16)
        a16 = (qf * jnp.exp(b)).astype(bf16)
        bl = b[CHUNK - 1:CHUNK]
        kd16 = (k * jnp.exp(bl - b)).astype(bf16)
        ebl = jnp.exp(bl)
        blocks = [_hg_block(qf, k, b, I) for I in range(nb)]
        heads = [slice(p * HEAD_DIM, (p + 1) * HEAD_DIM) for p in range(hp)]
        st_prev = state[...]
        st_ref[:, 0] = st_prev
        inter = [_dot(a16[:, sl], st_prev[p].astype(bf16), NT) for p, sl in enumerate(heads)]
        scores = [[_dot(qt16[:, sl], kt16[:, sl], NT) for sl in heads] for _, _, _, _, qt16, kt16, _ in blocks]
        intra = [[_dot(jnp.where(blk[6], sc, 0.0).astype(bf16), v16[0:blk[1], sl], NN) for sc, sl in zip(scs, heads)]
                 for blk, scs in zip(blocks, scores)]
        st_new = [st_prev[p] * ebl[:, sl] + _dot(v16[:, sl], kd16[:, sl], TN) for p, sl in enumerate(heads)]
        state[...] = jnp.stack(st_new)
        o = jnp.concatenate([inter[p] + jnp.concatenate([rows[p] for rows in intra], axis=0) for p in range(hp)], axis=1)
        o_ref[...] = o
        r = lax.rsqrt(_head_mean(o * o, hp) + EPS)
        gp = g_ref[...]
        y_ref[...] = (o * r * jnp.tile(gn_ref[...], (1, hp)) * (gp * _sigmoid(gp))).astype(bf16)

    H = W // HEAD_DIM
    return _grid_call(
        body,
        rider,
        name="hgrn_fwd",
        grid=(ngrp, nc),
        in_specs=[pspec(0), pspec(1), pspec(2), pspec(3), lbl_spec, gn_spec],
        out_specs=[act, act, st_spec],
        out_shape=[
            jax.ShapeDtypeStruct((S, W), bf16),
            jax.ShapeDtypeStruct((S, W), f32),
            jax.ShapeDtypeStruct((H, nc, HEAD_DIM, HEAD_DIM), f32),
        ],
        scratch_shapes=[pltpu.VMEM((hp, HEAD_DIM, HEAD_DIM), f32)],
        args=(proj, proj, proj, proj, lb_logits, hg_norm),
    )


def _hgrn_bwd(proj, o_hg, st, dycat, lb_logits, hg_norm, W, rider=None):
    S = proj.shape[0]
    hp = min(HG_HEADS_PER_STEP, W // HEAD_DIM)
    assert hp * HEAD_DIM == W, "the four gradients share one output block: all heads in one grid step"
    nc, ngrp, pspec, act, lbl_spec, gn_spec, st_spec = _hgrn_specs(S, W, hp, True)
    nb = CHUNK // HG_SUB

    def body(q_ref, f_ref, i_ref, g_ref, o_ref, st_ref, dy_ref, lbl_ref, gn_ref,
             d4_ref, dlbl_ref, dgn_ref, dstate):
        cw_ = hp * HEAD_DIM
        dq_ref, df_ref, di_ref, dg_ref = (d4_ref.at[:, pl.ds(t * cw_, cw_)] for t in range(4))
        h = pl.program_id(0)
        c = pl.program_id(1)

        @pl.when(c == 0)
        def _():
            dstate[...] = jnp.zeros_like(dstate)
            dlbl_ref[...] = jnp.zeros_like(dlbl_ref)

        @pl.when((c == 0) & (h == 0))
        def _():
            dgn_ref[...] = jnp.zeros_like(dgn_ref)

        row = lax.broadcasted_iota(jnp.int32, (CHUNK, CHUNK), 0)
        col = lax.broadcasted_iota(jnp.int32, (CHUNK, CHUNK), 1)
        upper = (col >= row).astype(f32)
        last_row = lax.broadcasted_iota(jnp.int32, (CHUNK, hp * HEAD_DIM), 0) == CHUNK - 1

        qp, fp = q_ref[...], f_ref[...]
        lb, sig, sigm, f, k, sq, qf, b = _hg_gates(qp, fp, lbl_ref[...])
        v16 = i_ref[...].astype(bf16)
        gnv = jnp.tile(gn_ref[...], (1, hp))
        gp = g_ref[...]
        sgg = _sigmoid(gp)
        gate = gp * sgg
        o = o_ref[...]
        r = lax.rsqrt(_head_mean(o * o, hp) + EPS)
        on = o * r
        dy = dy_ref[...]
        dg_ref[...] = (dy * on * gnv * (sgg * (1.0 + gp * (1.0 - sgg)))).astype(bf16)
        dgn_wide = jnp.sum(dy * on * gate, axis=0, keepdims=True)
        dgn_ref[...] += functools.reduce(jnp.add, [dgn_wide[:, p * HEAD_DIM:(p + 1) * HEAD_DIM] for p in range(hp)])
        don = dy * gnv * gate
        do16 = (r * (don - on * _head_mean(don * on, hp))).astype(bf16)
        eb = jnp.exp(b)
        A = qf * eb
        a16 = A.astype(bf16)
        bl = b[CHUNK - 1:CHUNK]
        ebl = jnp.exp(bl)
        ekd = jnp.exp(bl - b)
        kd = k * ekd
        kd16 = kd.astype(bf16)
        blocks = [_hg_block(qf, k, b, I) for I in range(nb)]

        heads = [slice(p * HEAD_DIM, (p + 1) * HEAD_DIM) for p in range(hp)]
        st_prev = st_ref[:, 0]
        dst_new = dstate[...]
        st16, dst16 = st_prev.astype(bf16), dst_new.astype(bf16)
        dA_h = [_dot(do16[:, sl], st16[p], NN) for p, sl in enumerate(heads)]
        dkd_h = [_dot(v16[:, sl], dst16[p], NN) for p, sl in enumerate(heads)]
        dv_h = [_dot(kd16[:, sl], dst16[p], NT) for p, sl in enumerate(heads)]
        dstate[...] = jnp.stack([dst_new[p] * ebl[:, sl] + _dot(do16[:, sl], a16[:, sl], TN) for p, sl in enumerate(heads)])
        dbl_h = [jnp.sum(dst_new[p] * st_prev[p], axis=0, keepdims=True) for p in range(hp)]
        sc_h = [[jnp.where(mask, _dot(qt16[:, sl], kt16[:, sl], NT), 0.0).astype(bf16) for sl in heads]
                for _, _, _, _, qt16, kt16, mask in blocks]
        dsc_h = [[jnp.where(mask, _dot(do16[r0:n, sl], v16[0:n, sl], NT), 0.0).astype(bf16) for sl in heads]
                 for r0, n, _, _, _, _, mask in blocks]
        dvi_h = [[_dot(sc, do16[blk[0]:blk[1], sl], TN) for sc, sl in zip(scs, heads)] for blk, scs in zip(blocks, sc_h)]
        dqt_h = [[_dot(dsc, blk[5][:, sl], NN) for dsc, sl in zip(dscs, heads)] for blk, dscs in zip(blocks, dsc_h)]
        dkt_h = [[_dot(dsc, blk[4][:, sl], TN) for dsc, sl in zip(dscs, heads)] for blk, dscs in zip(blocks, dsc_h)]
        dv_h = [functools.reduce(jnp.add, [dv_h[p]] + [_pad_rows(rows[p], CHUNK) for rows in dvi_h]) for p in range(hp)]

        dA, dkd = jnp.concatenate(dA_h, axis=1), jnp.concatenate(dkd_h, axis=1)
        dbl = jnp.concatenate(dbl_h, axis=1) * ebl + jnp.sum(dkd * kd, axis=0, keepdims=True)
        db = dA * A - dkd * kd + jnp.where(last_row, dbl, 0.0)
        dk = dkd * ekd
        dq_rows, db_rows = [], []
        for I, (r0, n, eq, ek, qt16, kt16, _) in enumerate(blocks):
            dqt, dkt = jnp.concatenate(dqt_h[I], axis=1), jnp.concatenate(dkt_h[I], axis=1)
            dq_rows.append(dqt * eq)
            dk = dk + _pad_rows(dkt * ek, CHUNK)
            db_rows.append(dqt * qt16.astype(f32))
            db = db - _pad_rows(dkt * kt16.astype(f32), CHUNK)
        dqf = dA * eb + jnp.concatenate(dq_rows, axis=0)
        db = db + jnp.concatenate(db_rows, axis=0)
        dlogf = _dot(upper, db, NN, HIGHEST)
        dfg = dlogf / f
        df_ref[...] = ((1.0 - lb) * sig * sigm * (dfg - dk)).astype(bf16)
        dlb = jnp.sum(sigm * (dfg - dk), axis=0, keepdims=True)
        dl0 = dlb * lb * (1.0 - lb)
        dlbl_ref[...] += jnp.concatenate([dl0, -dl0], axis=0)
        dq_ref[...] = (dqf * (sq * (1.0 + qp * (1.0 - sq)))).astype(bf16)
        di_ref[...] = jnp.concatenate(dv_h, axis=1).astype(bf16)

    cw = hp * HEAD_DIM
    dy_spec = pl.BlockSpec((CHUNK, cw), lambda h, c: (nc - 1 - c, h))
    return _grid_call(
        body,
        rider,
        name="hgrn_bwd",
        grid=(ngrp, nc),
        in_specs=[pspec(0), pspec(1), pspec(2), pspec(3), act, st_spec, dy_spec, lbl_spec, gn_spec],
        out_specs=[pl.BlockSpec((CHUNK, 4 * W), lambda h, c: (nc - 1 - c, 0)), lbl_spec, gn_spec],
        out_shape=[jax.ShapeDtypeStruct((S, 4 * W), bf16), jax.ShapeDtypeStruct((2, W), f32), jax.ShapeDtypeStruct((1, HEAD_DIM), f32)],
        scratch_shapes=[pltpu.VMEM((hp, HEAD_DIM, HEAD_DIM), f32)],
        args=(proj, proj, proj, proj, o_hg, st, dycat, lb_logits, hg_norm),
    )


def _att_dims(S):
    qt = ATT_Q_ROWS if S % ATT_Q_ROWS == 0 else CHUNK
    pad = LEFT_CHUNKS * CHUNK
    kb = pad + qt
    return qt, pad, kb, kb + qt


def _kv_padded(proj, W, pad):
    S = proj.shape[0]
    ts = _div(pad, 512, 2 * SUBLANES)
    assert S % ts == 0 and pad % ts == 0
    npad = pad // ts

    def body(k_ref, v_ref, ko_ref, vo_ref):
        i = pl.program_id(0)

        @pl.when(i < npad)
        def _():
            ko_ref[...] = jnp.zeros_like(ko_ref)
            vo_ref[...] = jnp.zeros_like(vo_ref)

        @pl.when(i >= npad)
        def _():
            ko_ref[...] = k_ref[...].astype(bf16)
            vo_ref[...] = v_ref[...].astype(bf16)

    out = pl.BlockSpec((ts, W), lambda i: (i, 0))
    return pl.pallas_call(
        body,
        name="kv_padded",
        grid=((S + pad) // ts,),
        in_specs=[pl.BlockSpec((ts, W), lambda i: (jnp.maximum(i - npad, 0), 5)), pl.BlockSpec((ts, W), lambda i: (jnp.maximum(i - npad, 0), 6))],
        out_specs=[out, out],
        out_shape=[jax.ShapeDtypeStruct((S + pad, W), bf16)] * 2,
        compiler_params=pltpu.CompilerParams(dimension_semantics=("parallel",)),
    )(proj, proj)


def _att_probs(qk, bias, start, qt, pad, kb):
    s = qk * (HEAD_DIM ** -0.5) + bias
    row = lax.broadcasted_iota(jnp.int32, (qt, kb), 0)
    col = lax.broadcasted_iota(jnp.int32, (qt, kb), 1)
    lo = jnp.bitwise_and(row, -CHUNK)
    ok = (col >= lo) & (col < lo + pad + CHUNK) & (col + start >= pad)
    s = jnp.where(ok, s, NEG)
    e = jnp.exp(s - jnp.max(s, axis=-1, keepdims=True))
    return e / jnp.sum(e, axis=-1, keepdims=True)


def _att_bias(ext_row, qt, kb, ne):
    e = jnp.broadcast_to(ext_row, (qt, ne))
    return pltpu.roll(e, ne - qt + 1, 1, stride=1, stride_axis=0)[:, :kb]


def _att_specs(S, H, ah, qt, pad, ne):
    cw = ah * HEAD_DIM
    q_spec = pl.BlockSpec((qt, cw), lambda h, g: (g, 4 * H // ah + h))
    kv_spec = pl.BlockSpec((S + pad, cw), lambda h, g: (0, h))
    ext_spec = pl.BlockSpec((ah, 1, ne), lambda h, g: (h, 0, 0))
    row_spec = pl.BlockSpec((qt, cw), lambda h, g: (g, h))
    return cw, q_spec, kv_spec, ext_spec, row_spec


def _attn_fwd(proj, kp, vp, ext, W, rider=None):
    S = proj.shape[0]
    H = W // HEAD_DIM
    ah = min(ATT_HEADS_PER_STEP, H)
    qt, pad, kb, ne = _att_dims(S)
    cw, q_spec, kv_spec, ext_spec, row_spec = _att_specs(S, H, ah, qt, pad, ne)
    heads = [slice(a * HEAD_DIM, (a + 1) * HEAD_DIM) for a in range(ah)]

    def body(q_ref, k_ref, v_ref, ext_ref, o_ref, bias):
        g = pl.program_id(1)

        @pl.when(g == 0)
        def _():
            for a in range(ah):
                bias[a] = _att_bias(ext_ref[a], qt, kb, ne)

        start = pl.multiple_of(g * qt, qt)
        q16 = q_ref[...].astype(bf16)
        kb16 = k_ref[pl.ds(start, kb), :]
        vb16 = v_ref[pl.ds(start, kb), :]
        qk = [_dot(q16[:, sl], kb16[:, sl], NT) for sl in heads]
        pn = [_att_probs(qk[a], bias[a], start, qt, pad, kb).astype(bf16) for a in range(ah)]
        o_ref[...] = jnp.concatenate([_dot(pn[a], vb16[:, sl], NN) for a, sl in enumerate(heads)], axis=1).astype(bf16)

    outs = _grid_call(
        body,
        rider,
        name="attn_fwd",
        grid=(H // ah, S // qt),
        in_specs=[q_spec, kv_spec, kv_spec, ext_spec],
        out_specs=[row_spec],
        out_shape=[jax.ShapeDtypeStruct((S, W), bf16)],
        scratch_shapes=[pltpu.VMEM((ah, qt, kb), f32)],
        args=(proj, kp, vp, ext),
    )
    return outs[0] if rider is None else outs


def _attn_bwd(proj, kp, vp, ext, dycat, W, rider=None):
    S = proj.shape[0]
    H = W // HEAD_DIM
    ah = min(ATT_HEADS_PER_STEP, H)
    qt, pad, kb, ne = _att_dims(S)
    cw, q_spec, kv_spec, ext_spec, row_spec = _att_specs(S, H, ah, qt, pad, ne)
    heads = [slice(a * HEAD_DIM, (a + 1) * HEAD_DIM) for a in range(ah)]
    scale = HEAD_DIM ** -0.5
    ng = S // qt

    def body(q_ref, k_ref, v_ref, ext_ref, do_ref, dq_ref, dko_ref, dvo_ref, db_ref, bias, dk_ref, dv_ref):
        g = pl.program_id(1)

        @pl.when(g == 0)
        def _():
            for a in range(ah):
                bias[a] = _att_bias(ext_ref[a], qt, kb, ne)
            dk_ref[...] = jnp.zeros_like(dk_ref)
            dv_ref[...] = jnp.zeros_like(dv_ref)
            db_ref[...] = jnp.zeros_like(db_ref)

        start = pl.multiple_of(g * qt, qt)
        q16 = q_ref[...].astype(bf16)
        kb16 = k_ref[pl.ds(start, kb), :]
        vb16 = v_ref[pl.ds(start, kb), :]
        do16 = do_ref[...].astype(bf16)
        qk = [_dot(q16[:, sl], kb16[:, sl], NT) for sl in heads]
        dpn = [_dot(do16[:, sl], vb16[:, sl], NT) for sl in heads]
        pn = [_att_probs(qk[a], bias[a], start, qt, pad, kb) for a in range(ah)]
        ds = [pn[a] * (dpn[a] - jnp.sum(dpn[a] * pn[a], axis=-1, keepdims=True)) for a in range(ah)]
        ds16 = [d.astype(bf16) for d in ds]
        dv_ref[pl.ds(start, kb), :] += jnp.concatenate([_dot(pn[a].astype(bf16), do16[:, sl], TN) for a, sl in enumerate(heads)], axis=1)
        dq_ref[...] = (jnp.concatenate([_dot(ds16[a], kb16[:, sl], NN) for a, sl in enumerate(heads)], axis=1) * scale).astype(bf16)
        dk_ref[pl.ds(start, kb), :] += jnp.concatenate([_dot(ds16[a], q16[:, sl], TN) for a, sl in enumerate(heads)], axis=1) * scale
        for a in range(ah):
            db_ref[a] += ds[a]

        @pl.when(g == ng - 1)
        def _():
            dko_ref[...] = dk_ref[pl.ds(pad, S), :].astype(bf16)
            dvo_ref[...] = dv_ref[pl.ds(pad, S), :].astype(bf16)

    kv_out = pl.BlockSpec((S, cw), lambda h, g: (0, h))
    return _grid_call(
        body,
        rider,
        name="attn_bwd",
        grid=(H // ah, S // qt),
        in_specs=[q_spec, kv_spec, kv_spec, ext_spec,
                  pl.BlockSpec((qt, cw), lambda h, g: (g, H // ah + h))],
        out_specs=[row_spec, kv_out, kv_out, pl.BlockSpec((ah, qt, kb), lambda h, g: (h, 0, 0))],
        out_shape=[
            jax.ShapeDtypeStruct((S, W), bf16),
            jax.ShapeDtypeStruct((S, W), bf16),
            jax.ShapeDtypeStruct((S, W), bf16),
            jax.ShapeDtypeStruct((H, qt, kb), f32),
        ],
        scratch_shapes=[pltpu.VMEM((ah, qt, kb), f32), pltpu.VMEM((S + pad, cw), f32), pltpu.VMEM((S + pad, cw), f32)],
        args=(proj, kp, vp, ext, dycat),
    )


def _bias_onehot(qt, ne, nrel_pad):
    m = lax.broadcasted_iota(jnp.int32, (nrel_pad, ne), 1)
    r = lax.broadcasted_iota(jnp.int32, (nrel_pad, ne), 0)
    rel = LEFT_CHUNKS * CHUNK + qt - 1 - m
    hot = (r == jnp.clip(rel, -REL_CLIP, REL_CLIP) + REL_CLIP) & (m < ne - 1)
    return hot.astype(f32)


def _bias_ext(rel_bias_pad, onehot):
    H, _, nr = rel_bias_pad.shape
    ne = onehot.shape[1]

    def body(rb_ref, oh_ref, o_ref):
        o_ref[0] = _dot(rb_ref[0], oh_ref[...], NN, HIGHEST)

    return pl.pallas_call(
        body,
        name="bias_ext",
        grid=(H,),
        in_specs=[pl.BlockSpec((1, 1, nr), lambda h: (h, 0, 0)), pl.BlockSpec((nr, ne), lambda h: (0, 0))],
        out_specs=pl.BlockSpec((1, 1, ne), lambda h: (h, 0, 0)),
        out_shape=jax.ShapeDtypeStruct((H, 1, ne), f32),
        compiler_params=pltpu.CompilerParams(dimension_semantics=("parallel",)),
    )(rel_bias_pad, onehot)


def _bias_bwd(dbias, onehot):
    H, qt, kb = dbias.shape
    nr, ne = onehot.shape

    def body(d_ref, oh_ref, o_ref):
        row = lax.broadcasted_iota(jnp.int32, (qt, qt), 0)
        col = lax.broadcasted_iota(jnp.int32, (qt, qt), 1)
        flip = (row + col == qt - 1).astype(f32)
        x = _dot(flip, d_ref[0], NN, HIGHEST)
        x = jnp.concatenate([x, jnp.zeros((qt, ne - kb), f32)], axis=1)
        de = jnp.sum(pltpu.roll(x, 0, 1, stride=1, stride_axis=0), axis=0, keepdims=True)
        o_ref[0] = _dot(de, oh_ref[...], NT, HIGHEST)

    return pl.pallas_call(
        body,
        name="bias_bwd",
        grid=(H,),
        in_specs=[pl.BlockSpec((1, qt, kb), lambda h: (h, 0, 0)), pl.BlockSpec((nr, ne), lambda h: (0, 0))],
        out_specs=pl.BlockSpec((1, 1, nr), lambda h: (h, 0, 0)),
        out_shape=jax.ShapeDtypeStruct((H, 1, nr), f32),
        compiler_params=pltpu.CompilerParams(dimension_semantics=("parallel",)),
    )(dbias, onehot)


def _place():
    x, y, c = lax.axis_index("x"), lax.axis_index("y"), lax.axis_index("c")
    return x, y, c


def _flip(x, y, k):
    return (1 - x if k & 2 else x), (1 - y if k & 1 else y)


def _region(ref, shard_shape, axis, j, half):
    R, C = shard_shape
    if axis == 0:
        if half is None:
            return ref.at[pl.ds(j * R, R), :]
        return ref.at[pl.ds(j * R + half * (R // 2), R // 2), :]
    if half is None:
        return ref.at[:, pl.ds(j * C, C)]
    return ref.at[pl.ds(half * (R // 2), R // 2), pl.ds(j * C, C)]


def _remote(src, dst, send_sem, recv_sem, dev):
    return pltpu.make_async_remote_copy(src_ref=src, dst_ref=dst, send_sem=send_sem, recv_sem=recv_sem,
                                        device_id=dev, device_id_type=MESH)


def _cast_place(shard, axis, where, *, name):
    R, C = shard.shape
    tr = _div(R, 256, 2 * SUBLANES)
    nr = R // tr
    full = (N_CHIPS * R, C) if axis == 0 else (R, N_CHIPS * C)
    omap = (lambda r, s: (s[0] * nr + r, 0)) if axis == 0 else (lambda r, s: (r, s[0]))

    def body(s_ref, x_ref, o_ref):
        o_ref[...] = x_ref[...].astype(bf16)

    return pl.pallas_call(
        body,
        name=name,
        grid_spec=pltpu.PrefetchScalarGridSpec(
            num_scalar_prefetch=1,
            grid=(nr,),
            in_specs=[pl.BlockSpec((tr, C), lambda r, s: (r, 0))],
            out_specs=pl.BlockSpec((tr, C), omap),
        ),
        out_shape=jax.ShapeDtypeStruct(full, bf16),
        compiler_params=pltpu.CompilerParams(dimension_semantics=("parallel",)),
    )(where, shard)


class _Rider(NamedTuple):
    ins: list
    out_shapes: list
    aliases: dict
    sems: list
    first: Callable
    last: Callable


def _run_rider(rider, *, name):
    ni, no = len(rider.ins), len(rider.out_shapes)

    def body(*refs):
        ins, outs, sems = refs[:ni], refs[ni:ni + no], refs[ni + no:]
        rider.first(ins, outs, sems)
        rider.last(ins, outs, sems)

    return pl.pallas_call(
        body,
        name=name,
        in_specs=[_hbm()] * ni,
        out_specs=[_hbm()] * no,
        out_shape=rider.out_shapes,
        input_output_aliases=rider.aliases,
        scratch_shapes=rider.sems,
    )(*rider.ins)


def _gather_rider(fulls, shard_shapes, axes, splits, rels):
    n = len(fulls)

    def stage1(ins, outs, sems):
        x, y, c = _place()
        jo = 2 * x + y
        cps = []
        for m in range(n):
            half = c if splits[m] else None
            for k in rels[m]:
                px, py = _flip(x, y, k)
                cps.append(_remote(_region(ins[m], shard_shapes[m], axes[m], jo, half), _region(outs[m], shard_shapes[m], axes[m], jo, half),
                                   sems[0].at[3 * m + k - 1], sems[1].at[3 * m + k - 1], (px, py, c)))
        return cps

    def first(ins, outs, sems):
        for cp in stage1(ins, outs, sems):
            cp.start()

    def last(ins, outs, sems):
        x, y, c = _place()
        send1, recv1, send2, recv2 = sems
        forwards = []
        for m in range(n):
            half = c if splits[m] else None
            for k in rels[m]:
                px, py = _flip(x, y, k)
                land = _region(outs[m], shard_shapes[m], axes[m], 2 * px + py, half)
                _remote(land, land, send1.at[3 * m + k - 1], recv1.at[3 * m + k - 1], (x, y, c)).wait_recv()
                if splits[m]:
                    fw = _remote(land, land, send2.at[3 * m + k - 1], recv2.at[3 * m + k - 1], (x, y, 1 - c))
                    fw.start()
                    forwards.append(fw)
        for m in range(n):
            if not splits[m]:
                continue
            for k in rels[m]:
                px, py = _flip(x, y, k)
                land = _region(outs[m], shard_shapes[m], axes[m], 2 * px + py, 1 - c)
                _remote(land, land, send2.at[3 * m + k - 1], recv2.at[3 * m + k - 1], (x, y, c)).wait_recv()
        for cp in stage1(ins, outs, sems) + forwards:
            cp.wait_send()

    sems = pltpu.SemaphoreType.DMA((3 * n,))
    return _Rider(list(fulls), [jax.ShapeDtypeStruct(f.shape, f.dtype) for f in fulls], {m: m for m in range(n)},
                  [sems, sems, sems, sems], first, last)


def _start_all(copies):
    def first(ins, outs, sems):
        for cp in copies(ins, outs, sems):
            cp.start()

    def last(ins, outs, sems):
        cps = copies(ins, outs, sems)
        for cp in cps:
            cp.wait_recv()
        for cp in cps:
            cp.wait_send()

    return first, last


def _chips_rider(ts, rels=(1, 2, 3), landing=None):
    n = len(ts)

    def copies(ins, outs, sems):
        x, y, c = _place()
        cps = []
        for m in range(n):
            for k in rels:
                px, py = _flip(x, y, k)
                cps.append(_remote(ins[m].at[2 * px + py], outs[m].at[k - 1], sems[0].at[3 * m + k - 1], sems[1].at[3 * m + k - 1], (px, py, c)))
        return cps

    sems = pltpu.SemaphoreType.DMA((3 * n,))
    return _Rider(list(ts) + list(landing or []), [jax.ShapeDtypeStruct((3,) + t.shape[1:], t.dtype) for t in ts],
                  {n + m: m for m in range(n)} if landing else {}, [sems, sems], *_start_all(copies))


def _sibling_rider(dws, shard_shapes, axes):
    n = len(dws)

    def copies(ins, outs, sems):
        x, y, c = _place()
        return [_remote(_region(ins[m], shard_shapes[m], axes[m], j, 1 - c), outs[m].at[j],
                        sems[0].at[N_CHIPS * m + j], sems[1].at[N_CHIPS * m + j], (x, y, 1 - c))
                for m in range(n) for j in range(N_CHIPS)]

    sems = pltpu.SemaphoreType.DMA((N_CHIPS * n,))
    return _Rider(list(dws), [jax.ShapeDtypeStruct((N_CHIPS, s[0] // 2, s[1]), d.dtype) for s, d in zip(shard_shapes, dws)], {},
                  [sems, sems], *_start_all(copies))


def _join_rider(gs):
    n = len(gs)

    def copies(ins, outs, sems):
        x, y, c = _place()
        cps = []
        for m in range(n):
            hr = gs[m].shape[0] // 2
            cps.append(_remote(ins[m].at[pl.ds(c * hr, hr), :], outs[m].at[pl.ds(c * hr, hr), :], sems[0].at[m], sems[1].at[m], (x, y, 1 - c)))
        return cps

    sems = pltpu.SemaphoreType.DMA((n,))
    return _Rider(list(gs), [jax.ShapeDtypeStruct(g.shape, g.dtype) for g in gs], {m: m for m in range(n)}, [sems, sems], *_start_all(copies))


def _both(r1, r2):
    i1, o1, s1 = len(r1.ins), len(r1.out_shapes), len(r1.sems)
    aliases = dict(r1.aliases)
    aliases.update({i1 + i: o1 + o for i, o in r2.aliases.items()})

    def first(ins, outs, sems):
        r1.first(ins[:i1], outs[:o1], sems[:s1])
        r2.first(ins[i1:], outs[o1:], sems[s1:])

    def last(ins, outs, sems):
        r1.last(ins[:i1], outs[:o1], sems[:s1])
        r2.last(ins[i1:], outs[o1:], sems[s1:])

    return _Rider(r1.ins + r2.ins, r1.out_shapes + r2.out_shapes, aliases, r1.sems + r2.sems, first, last)


def _pair_sum(dw, got, shard_shape, axis, where, *, name):
    R, C = shard_shape
    hr = R // 2
    tr = _div(hr, 256, 2 * SUBLANES)
    nr = hr // tr
    dmap = (lambda j, r, s: ((2 * j + s[1]) * nr + r, 0)) if axis == 0 else (lambda j, r, s: (s[1] * nr + r, j))
    slot = pl.BlockSpec((1, tr, C), lambda j, r, s: (j, r, 0))

    def body(s_ref, d_ref, g_ref, o_ref):
        o_ref[0] = (d_ref[...].astype(f32) + g_ref[0].astype(f32)).astype(bf16)

    return pl.pallas_call(
        body,
        name=name,
        grid_spec=pltpu.PrefetchScalarGridSpec(
            num_scalar_prefetch=1,
            grid=(N_CHIPS, nr),
            in_specs=[pl.BlockSpec((tr, C), dmap), slot],
            out_specs=slot,
        ),
        out_shape=jax.ShapeDtypeStruct((N_CHIPS, hr, C), bf16),
        compiler_params=pltpu.CompilerParams(dimension_semantics=("parallel", "parallel")),
    )(where, dw, got)


def _chip_sum(pair, others, where, *, name):
    _, hr, C = pair.shape
    tr = _div(hr, 256, 2 * SUBLANES)
    nr = hr // tr

    def body(s_ref, p_ref, o_ref, g_ref):
        g_ref[...] = p_ref[0].astype(f32) + o_ref[0].astype(f32) + o_ref[1].astype(f32) + o_ref[2].astype(f32)

    return pl.pallas_call(
        body,
        name=name,
        grid_spec=pltpu.PrefetchScalarGridSpec(
            num_scalar_prefetch=1,
            grid=(nr,),
            in_specs=[pl.BlockSpec((1, tr, C), lambda r, s: (s[0], r, 0)), pl.BlockSpec((3, tr, C), lambda r, s: (0, r, 0))],
            out_specs=pl.BlockSpec((tr, C), lambda r, s: (s[1] * nr + r, 0)),
        ),
        out_shape=jax.ShapeDtypeStruct((2 * hr, C), f32),
        compiler_params=pltpu.CompilerParams(dimension_semantics=("parallel",)),
    )(where, pair, others)


def _allreduce_small(v):
    rows = v.shape[0]

    def body(v_ref, o_ref, gath, send, recv):
        x, y, c = _place()
        me = 4 * x + 2 * y + c
        gath[pl.ds(me, 1)] = v_ref[...][None]
        cps = []
        for k in range(1, N_DEV):
            peer = (1 - x if k & 4 else x, 1 - y if k & 2 else y, 1 - c if k & 1 else c)
            cp = _remote(v_ref, gath.at[me], send.at[k - 1], recv.at[k - 1], peer)
            cp.start()
            cps.append(cp)
        for cp in cps:
            cp.wait_recv()
        for cp in cps:
            cp.wait_send()
        acc = gath[0]
        for d in range(1, N_DEV):
            acc = acc + gath[d]
        o_ref[...] = acc

    return pl.pallas_call(
        body,
        name="small_allreduce",
        in_specs=[_vmem()],
        out_specs=_vmem(),
        out_shape=jax.ShapeDtypeStruct(v.shape, f32),
        scratch_shapes=[pltpu.VMEM((N_DEV, rows, LANES), f32), pltpu.SemaphoreType.DMA((N_DEV - 1,)), pltpu.SemaphoreType.DMA((N_DEV - 1,))],
    )(v)


def _adamw_math(w, g, m, v):
    m = ADAM_B1 * m + (1.0 - ADAM_B1) * g
    v = ADAM_B2 * v + (1.0 - ADAM_B2) * (g * g)
    m_hat = m / (1.0 - ADAM_B1 ** ADAM_STEP)
    v_hat = v / (1.0 - ADAM_B2 ** ADAM_STEP)
    delta = -ADAM_LR * (m_hat / (jnp.sqrt(v_hat) + ADAM_EPS) + ADAM_WD * w)
    return delta, m, v


def _adamw(w, g, m, v, *, name):
    R, C = w.shape
    tr = _div(R, 128, SUBLANES)

    def body(w_ref, g_ref, m_ref, v_ref, d_ref, mo_ref, vo_ref):
        d, mn, vn = _adamw_math(w_ref[...], g_ref[...], m_ref[...], v_ref[...])
        d_ref[...] = d
        mo_ref[...] = mn
        vo_ref[...] = vn

    spec = pl.BlockSpec((tr, C), lambda r: (r, 0))
    return pl.pallas_call(
        body,
        name=name,
        grid=(R // tr,),
        in_specs=[spec] * 4,
        out_specs=[spec] * 3,
        out_shape=[jax.ShapeDtypeStruct((R, C), f32)] * 3,
        compiler_params=pltpu.CompilerParams(dimension_semantics=("parallel",)),
    )(w, g, m, v)


def _pack(arrs):
    flat = jnp.concatenate([a.reshape(-1).astype(f32) for a in arrs])
    tile = SUBLANES * LANES
    total = -(-flat.shape[0] // tile) * tile
    return jnp.pad(flat, (0, total - flat.shape[0])).reshape(total // LANES, LANES)


def _unpack(buf, shapes):
    flat = buf.reshape(-1)
    out, off = [], 0
    for s in shapes:
        size = int(np.prod(s))
        out.append(flat[off:off + size].reshape(s))
        off += size
    return out


def kernel(x, p, norm_mix, w_in, lb_logits, hg_norm, rel_bias, w_out, norm_ffn, w_up, conv_w, conv_b, w_down, norm_ple, w_ple_gate, w_ple_proj, final_norm, loss_target, m_norm_mix, m_w_in, m_lb_logits, m_hg_norm, m_rel_bias, m_w_out, m_norm_ffn, m_w_up, m_conv_w, m_conv_b, m_w_down, m_norm_ple, m_w_ple_gate, m_w_ple_proj, m_final_norm, v_norm_mix, v_w_in, v_lb_logits, v_hg_norm, v_rel_bias, v_w_out, v_norm_ffn, v_w_up, v_conv_w, v_conv_b, v_w_down, v_norm_ple, v_w_ple_gate, v_w_ple_proj, v_final_norm):
    S, D = x.shape[1], x.shape[2]
    xv, pv, tgt = x[0], p[0, 0], loss_target[0]
    W = (w_in.shape[2] * N_CHIPS) // 7
    H = W // HEAD_DIM
    F = w_down.shape[1] * N_CHIPS
    tc = _div(F // 2, 1408, LANES)
    nj = F // tc
    jx, jy = lax.axis_index("x"), lax.axis_index("y")
    chip = 2 * jx + jy

    big = [w_in[0], w_out[0], w_up[0], w_down[0], w_ple_gate[0], w_ple_proj[0]]
    big_axes = [1, 0, 1, 0, 0, 1]
    where = jnp.stack([chip, lax.axis_index("c")]).astype(jnp.int32)
    cw_pad = jnp.pad(conv_w[0], ((0, SUBLANES - conv_w.shape[1]), (0, 0)))
    cw_mine = lax.dynamic_update_slice(jnp.zeros((N_CHIPS * SUBLANES, cw_pad.shape[1]), f32), cw_pad, (chip * SUBLANES, 0))
    placed = [_cast_place(b, ax, where, name=f"cast_place_{i}") for i, (b, ax) in enumerate(zip(big, big_axes))]
    shard_shapes = [b.shape for b in big]

    every = (1, 2, 3)

    def gather(items):
        return _gather_rider([b for b, _, _ in items],
                             [cw_pad.shape if i is None else shard_shapes[i] for _, i, _ in items],
                             [0 if i is None else big_axes[i] for _, i, _ in items],
                             [i is not None for _, i, _ in items], [r for _, _, r in items])

    (W_in,) = _run_rider(gather([(placed[0], 0, every)]), name="gather_w_in")

    tm = _div(S, 1024, LANES)

    a1 = _rms_fwd(xv, norm_mix, name="rms_mix")
    proj, W_up_part = _mm(a1, W_in, dims="nn", tm=tm, tn=_div(7 * W, 512, LANES), tk=D, out_dtype=f32, name="mm_in",
                          rider=gather([(placed[2], 2, (1, 2))]))
    y_hg, o_hg, st, W_out = _hgrn_fwd(proj, lb_logits, hg_norm, W, rider=gather([(placed[1], 1, every)]))

    qt, pad, kb, ne = _att_dims(S)
    nrel = rel_bias.shape[2]
    nrel_pad = -(-nrel // LANES) * LANES
    onehot = _bias_onehot(qt, ne, nrel_pad)
    rb_pad = jnp.pad(rel_bias[0], ((0, 0), (0, nrel_pad - nrel)))[:, None, :]
    ext = _bias_ext(rb_pad, onehot)
    kp, vp = _kv_padded(proj, W, pad)
    y_att, W_up = _attn_fwd(proj, kp, vp, ext, W, rider=gather([(W_up_part, 2, (3,))]))
    ycat = jnp.concatenate([y_hg, y_att], axis=1)

    tn_d = _div(D, 512, LANES)
    h1, W_pg, W_pp, cw_all = _mm(ycat, W_out, dims="nn", tm=tm, tn=tn_d, tk=D, out_dtype=f32, res=xv, name="mm_out",
                                 rider=gather([(placed[4], 4, every), (placed[5], 5, every), (cw_mine, None, every)]))
    a2 = _rms_fwd(h1, norm_ffn, name="rms_ffn")
    perm = lambda b: (b % 2) * nj + b // 2
    u, W_down = _mm(a2, W_up, dims="nn", tm=tm, tn=tc, tk=D, out_dtype=f32, name="mm_up", b_idx=lambda i, j, k: (k, perm(j)),
                    rider=gather([(placed[3], 3, every)]))
    ncw = conv_w.shape[1]
    cw_full = jnp.transpose(cw_all.reshape(N_CHIPS, SUBLANES, -1)[:, :ncw], (1, 0, 2)).reshape(ncw, -1)
    cb = conv_b
    mact = _convact_fwd(u, cw_full, cb, tc)
    tn_w = _div(D, 1024, LANES)
    h2 = _mm(mact, W_down, dims="nn", tm=_div(S, 512, LANES), tn=tn_w, tk=F, out_dtype=f32, res=h1, name="mm_down", cols_outer=True)
    a3 = _rms_fwd(h2, norm_ple, name="rms_ple")
    zg = _mm(a3, W_pg, dims="nn", tm=tm, tn=tn_d, tk=D, out_dtype=f32, name="mm_ple_gate")
    p16 = pv.astype(bf16)
    pp = _mm(p16, W_pp, dims="nn", tm=tm, tn=tn_d, tk=pv.shape[1], out_dtype=f32, name="mm_ple_proj")
    dh3, dzg, dpp, loss_part, d_fn = _tail(h2, zg, pp, tgt, final_norm.reshape(1, D))

    pair = {}

    def siblings(idx, grads):
        return _sibling_rider(grads, [shard_shapes[i] for i in idx], [big_axes[i] for i in idx])

    def pair_up(idx, grads, got):
        for i, d, g in zip(idx, grads, got):
            pair[i] = _pair_sum(d, g, shard_shapes[i], big_axes[i], where, name=f"grad_pair_sum_{i}")

    def reduced_half(i, landed):
        return _chip_sum(pair[i], landed, where, name=f"grad_chip_sum_{i}")

    tk_s = S
    dW_pp = _mm(p16, dpp, dims="tn", tm=pv.shape[1], tn=tn_w, tk=tk_s, out_dtype=bf16, name="mm_d_ple_proj")
    dW_pg = _mm(a3, dzg, dims="tn", tm=tn_w, tn=tn_w, tk=tk_s, out_dtype=bf16, name="mm_d_ple_gate")
    pair_up([4, 5], [dW_pg, dW_pp], _run_rider(siblings([4, 5], [dW_pg, dW_pp]), name="grads_to_sibling_4_5"))
    da3 = _mm(dzg, W_pg, dims="nt", tm=tm, tn=_div(D, 1024, LANES), tk=D, out_dtype=f32, name="mm_da3")
    dh2, dh2b, d_nple = _rms_bwd(h2, norm_ple, da3, dh3, name="rms_ple_bwd")
    dm, land4, land5 = _mm(dh2b, W_down, dims="nt", tm=tm, tn=_div(F, 512, LANES), tk=D, out_dtype=f32, name="mm_dm",
                           rider=_chips_rider([pair[4], pair[5]]))
    dW_down = _mm(mact, dh2b, dims="tn", tm=_div(F, 1408, LANES), tn=tn_d, tk=tk_s, out_dtype=bf16, name="mm_d_down")
    du, dcw_g, dcw_v, dcb_g, dcb_v, got3 = _convact_bwd(u, dm, cw_full, cb, tc, rider=siblings([3], [dW_down]))
    pair_up([3], [dW_down], [got3])
    dW_up, land3 = _mm(a2, du, dims="tn", tm=_div(D, 1024, LANES), tn=tc, tk=tk_s, out_dtype=bf16, name="mm_d_up",
                       o_idx=lambda i, j, k: (i, perm(j)), rider=_chips_rider([pair[3]]))
    da2, got2 = _mm(du, W_up, dims="nt", tm=tm, tn=D, tk=tc, out_dtype=f32, name="mm_da2",
                    b_idx=lambda i, j, k: (j, perm(k)), rider=siblings([2], [dW_up]))
    pair_up([2], [dW_up], [got2])
    dh1, dh1b, d_nffn = _rms_bwd(h1, norm_ffn, da2, dh2, name="rms_ffn_bwd")
    dycat = _mm(dh1b, W_out, dims="nt", tm=tm, tn=_div(D, 1024, LANES), tk=D, out_dtype=f32, name="mm_dycat")
    dW_out = _mm(ycat, dh1b, dims="tn", tm=tn_w, tn=tn_w, tk=tk_s, out_dtype=bf16, name="mm_d_out")
    pair_up([1], [dW_out], _run_rider(siblings([1], [dW_out]), name="grads_to_sibling_1"))
    dq_att, dk_att, dv_att, dbias, land2 = _attn_bwd(proj, kp, vp, ext, dycat, W, rider=_chips_rider([pair[2]], rels=(1, 2)))
    d_hg4, d_lbl, d_hgn, land2 = _hgrn_bwd(proj, o_hg, st, dycat, lb_logits, hg_norm, W,
                                           rider=_chips_rider([pair[2]], rels=(3,), landing=[land2]))
    d_rb = _bias_bwd(dbias, onehot)[:, 0, :nrel]
    dproj = jnp.concatenate([d_hg4, dq_att, dk_att, dv_att], axis=1)
    dW_in, land1 = _mm(a1, dproj, dims="tn", tm=tn_w, tn=_div(7 * W, 1024, LANES), tk=tk_s, out_dtype=bf16, name="mm_d_in",
                       rider=_chips_rider([pair[1]]))
    pair_up([0], [dW_in], _run_rider(siblings([0], [dW_in]), name="grads_to_sibling_0"))
    halves = [reduced_half(i, land) for i, land in ((1, land1), (2, land2), (3, land3), (4, land4), (5, land5))]
    da1, land0, *g_rest = _mm(dproj, W_in, dims="nt", tm=tm, tn=D, tk=_div(7 * W, 1792, LANES), out_dtype=f32, name="mm_da1",
                              rider=_both(_chips_rider([pair[0]]), _join_rider(halves)))
    grad_x, _, d_nmix = _rms_bwd(xv, norm_mix, da1, dh1, name="rms_mix_bwd")
    g_big = list(_run_rider(_join_rider([reduced_half(0, land0)]), name="grads_join_w_in")) + list(g_rest)

    d_cw = jnp.concatenate([dcw_g, dcw_v], axis=1)
    d_cb = jnp.concatenate([dcb_g, dcb_v], axis=1)
    small_parts = [loss_part[:, :1], d_nmix, d_lbl, d_hgn, d_rb, d_nffn, d_cw, d_cb, d_nple, d_fn]
    small_shapes = [(1, 1), (1, D), lb_logits.shape, hg_norm.shape, (H, nrel), (1, D), (ncw, 2 * F), (1, 2 * F), (1, D), (1, D)]
    red = _unpack(_allreduce_small(_pack(small_parts)), small_shapes)
    loss = red[0].reshape(())
    g_nmix, g_lbl, g_hgn, g_rb, g_nffn, g_cw_all, g_cb, g_nple, g_fn = red[1:]
    csh = conv_w.shape[2]
    g_cw = lax.dynamic_slice(g_cw_all, (0, chip * csh), (ncw, csh))
    small_g = [g_nmix, g_lbl, g_hgn, g_rb[None], g_nffn, g_cw[None], g_cb, g_nple, g_fn.reshape(D)]
    small_w = [norm_mix, lb_logits, hg_norm, rel_bias, norm_ffn, conv_w, conv_b, norm_ple, final_norm]
    small_m = [m_norm_mix, m_lb_logits, m_hg_norm, m_rel_bias, m_norm_ffn, m_conv_w, m_conv_b, m_norm_ple, m_final_norm]
    small_v = [v_norm_mix, v_lb_logits, v_hg_norm, v_rel_bias, v_norm_ffn, v_conv_w, v_conv_b, v_norm_ple, v_final_norm]
    shapes_s = [w.shape for w in small_w]
    sd, sm, sv = _adamw(_pack(small_w), _pack(small_g), _pack(small_m), _pack(small_v), name="adamw_small")
    small_g = [g.reshape(s) for g, s in zip(small_g, shapes_s)]
    small_d, small_nm, small_nv = _unpack(sd, shapes_s), _unpack(sm, shapes_s), _unpack(sv, shapes_s)

    big_m = [m_w_in[0], m_w_out[0], m_w_up[0], m_w_down[0], m_w_ple_gate[0], m_w_ple_proj[0]]
    big_v = [v_w_in[0], v_w_out[0], v_w_up[0], v_w_down[0], v_w_ple_gate[0], v_w_ple_proj[0]]
    big_d, big_nm, big_nv = [], [], []
    for i in range(6):
        d_, m_, v_ = _adamw(big[i], g_big[i], big_m[i], big_v[i], name=f"adamw_{i}")
        big_d.append(d_[None])
        big_nm.append(m_[None])
        big_nv.append(v_[None])
    g_big = [g[None] for g in g_big]

    def order(sm_list, bg_list):
        s, b = sm_list, bg_list
        return [s[0], b[0], s[1], s[2], s[3], b[1], s[4], b[2], s[5], s[6], b[3], s[7], b[4], b[5], s[8]]

    return (loss, grad_x[None], *order(small_g, g_big), *order(small_d, big_d), *order(small_nm, big_nm), *order(small_nv, big_nv))
```

```python
import functools
from typing import Callable, NamedTuple

import jax
import jax.numpy as jnp
import numpy as np
from jax import lax
from jax.experimental import pallas as pl
from jax.experimental.pallas import tpu as pltpu

f32 = jnp.float32
bf16 = jnp.bfloat16

CHUNK = 64
HEAD_DIM = 128
LEFT_CHUNKS = 8
REL_CLIP = 128
EPS = 1e-6
HG_SUB = 16
HG_HEADS_PER_STEP = 8
ATT_Q_ROWS = 256
ATT_HEADS_PER_STEP = 2
ADAM_LR, ADAM_B1, ADAM_B2, ADAM_EPS, ADAM_WD, ADAM_STEP = 0.001, 0.9, 0.999, 1e-08, 0.01, 10
LANES = 128
SUBLANES = 8
N_CHIPS = 4
N_DEV = 8
MESH = pl.DeviceIdType.MESH
NEG = float(np.finfo(np.float32).min)

NN = (((1,), (0,)), ((), ()))
NT = (((1,), (1,)), ((), ()))
TN = (((0,), (0,)), ((), ()))
HIGHEST = lax.Precision.HIGHEST


def _dot(a, b, dims, precision=None):
    return lax.dot_general(a, b, dims, preferred_element_type=f32, precision=precision)


def _sigmoid(v):
    return 1.0 / (1.0 + jnp.exp(-v))


def _div(n, pref, mult):
    best = None
    d = mult
    while d <= min(n, pref):
        if n % d == 0:
            best = d
        d += mult
    return best if best is not None else n


def _hbm():
    return pl.BlockSpec(memory_space=pltpu.HBM)


def _vmem():
    return pl.BlockSpec(memory_space=pltpu.VMEM)


def _mm(a, b, *, dims, tm, tn, tk, out_dtype, name, res=None, a_idx=None, b_idx=None, o_idx=None, out_shape=None, rider=None,
        cols_outer=False):
    if dims == "nn":
        (M, K), (_, N) = a.shape, b.shape
        a_blk, b_blk, dn = (tm, tk), (tk, tn), NN
        a_def, b_def = (lambda i, j, k: (i, k)), (lambda i, j, k: (k, j))
    elif dims == "nt":
        (M, K), (N, _) = a.shape, b.shape
        a_blk, b_blk, dn = (tm, tk), (tn, tk), NT
        a_def, b_def = (lambda i, j, k: (i, k)), (lambda i, j, k: (j, k))
    else:
        (K, M), (_, N) = a.shape, b.shape
        a_blk, b_blk, dn = (tk, tm), (tk, tn), TN
        a_def, b_def = (lambda i, j, k: (k, i)), (lambda i, j, k: (k, j))
    assert M % tm == 0 and N % tn == 0 and K % tk == 0, (name, M, N, K, tm, tn, tk)
    nk = K // tk
    has_res = res is not None

    def body(*refs):
        a_ref, b_ref = refs[0], refs[1]
        res_ref = refs[2] if has_res else None
        o_ref = refs[3] if has_res else refs[2]
        part = _dot(a_ref[...].astype(bf16), b_ref[...].astype(bf16), dn)

        def finish(acc):
            if has_res:
                acc = acc + res_ref[...]
            o_ref[...] = acc.astype(out_dtype)

        if nk == 1:
            finish(part)
        else:
            acc_ref = refs[-1]
            k = pl.program_id(2)

            @pl.when(k == 0)
            def _():
                acc_ref[...] = part

            @pl.when(k > 0)
            def _():
                acc_ref[...] += part

            @pl.when(k == nk - 1)
            def _():
                finish(acc_ref[...])

    def ordered(custom, default):
        f = custom or default
        return (lambda j, i, k: f(i, j, k)) if cols_outer else f

    o_map = ordered(o_idx, lambda i, j, k: (i, j))
    in_specs = [pl.BlockSpec(a_blk, ordered(a_idx, a_def)), pl.BlockSpec(b_blk, ordered(b_idx, b_def))]
    args = [a, b]
    if has_res:
        in_specs.append(pl.BlockSpec((tm, tn), o_map))
        args.append(res)
    outs = _grid_call(
        body,
        rider,
        name=name,
        grid=(N // tn, M // tm, nk) if cols_outer else (M // tm, N // tn, nk),
        in_specs=in_specs,
        out_specs=[pl.BlockSpec((tm, tn), o_map)],
        out_shape=[jax.ShapeDtypeStruct(out_shape or (M, N), out_dtype)],
        scratch_shapes=[pltpu.VMEM((tm, tn), f32)] if nk > 1 else [],
        args=args,
    )
    return outs[0] if rider is None else outs


def _rms_fwd(xv, g, *, name):
    S, D = xv.shape
    ts = _div(S, 512, SUBLANES)

    def body(x_ref, g_ref, o_ref):
        v = x_ref[...]
        r = lax.rsqrt(jnp.mean(v * v, axis=-1, keepdims=True) + EPS)
        o_ref[...] = (v * r * g_ref[...]).astype(bf16)

    return pl.pallas_call(
        body,
        name=name,
        grid=(S // ts,),
        in_specs=[pl.BlockSpec((ts, D), lambda i: (i, 0)), pl.BlockSpec((1, D), lambda i: (0, 0))],
        out_specs=pl.BlockSpec((ts, D), lambda i: (i, 0)),
        out_shape=jax.ShapeDtypeStruct((S, D), bf16),
        compiler_params=pltpu.CompilerParams(dimension_semantics=("parallel",)),
    )(xv, g)


def _rms_bwd(xv, g, dy, dres, *, name):
    S, D = xv.shape
    ts = _div(S, 256, SUBLANES)

    def body(x_ref, g_ref, dy_ref, dres_ref, dx_ref, dxb_ref, dg_ref):
        i = pl.program_id(0)
        v = x_ref[...]
        r = lax.rsqrt(jnp.mean(v * v, axis=-1, keepdims=True) + EPS)
        vn = v * r
        d = dy_ref[...]
        part = jnp.sum(d * vn, axis=0, keepdims=True)

        @pl.when(i == 0)
        def _():
            dg_ref[...] = part

        @pl.when(i > 0)
        def _():
            dg_ref[...] += part

        t = d * g_ref[...]
        dx = dres_ref[...] + r * (t - vn * jnp.mean(t * vn, axis=-1, keepdims=True))
        dx_ref[...] = dx
        dxb_ref[...] = dx.astype(bf16)

    row = pl.BlockSpec((ts, D), lambda i: (i, 0))
    vec = pl.BlockSpec((1, D), lambda i: (0, 0))
    return pl.pallas_call(
        body,
        name=name,
        grid=(S // ts,),
        in_specs=[row, vec, row, row],
        out_specs=[row, row, vec],
        out_shape=[jax.ShapeDtypeStruct((S, D), f32), jax.ShapeDtypeStruct((S, D), bf16), jax.ShapeDtypeStruct((1, D), f32)],
        compiler_params=pltpu.CompilerParams(dimension_semantics=("arbitrary",)),
    )(xv, g, dy, dres)


def _tail(h2, zg, pp, tgt, fn):
    S, D = h2.shape
    ts = _div(S, 256, SUBLANES)

    def body(h_ref, z_ref, p_ref, t_ref, fn_ref, dh_ref, dz_ref, dp_ref, loss_ref, dfn_ref):
        i = pl.program_id(0)
        sg = _sigmoid(z_ref[...])
        ppv = p_ref[...]
        h3 = h_ref[...] + sg * ppv
        r = lax.rsqrt(jnp.mean(h3 * h3, axis=-1, keepdims=True) + EPS)
        hn = h3 * r
        fnv = fn_ref[...]
        e = hn * fnv - t_ref[...]
        lpart = 0.5 * jnp.sum(jnp.mean(e * e, axis=-1, keepdims=True), axis=0, keepdims=True)
        dy = e * (1.0 / D)
        dpart = jnp.sum(dy * hn, axis=0, keepdims=True)

        @pl.when(i == 0)
        def _():
            loss_ref[...] = jnp.broadcast_to(lpart, loss_ref.shape)
            dfn_ref[...] = dpart

        @pl.when(i > 0)
        def _():
            loss_ref[...] += jnp.broadcast_to(lpart, loss_ref.shape)
            dfn_ref[...] += dpart

        t = dy * fnv
        dh3 = r * (t - hn * jnp.mean(t * hn, axis=-1, keepdims=True))
        dh_ref[...] = dh3
        dz_ref[...] = (dh3 * ppv * sg * (1.0 - sg)).astype(bf16)
        dp_ref[...] = (dh3 * sg).astype(bf16)

    row = pl.BlockSpec((ts, D), lambda i: (i, 0))
    vec = pl.BlockSpec((1, D), lambda i: (0, 0))
    one = pl.BlockSpec((1, LANES), lambda i: (0, 0))
    return pl.pallas_call(
        body,
        name="ple_final_loss",
        grid=(S // ts,),
        in_specs=[row, row, row, row, vec],
        out_specs=[row, row, row, one, vec],
        out_shape=[
            jax.ShapeDtypeStruct((S, D), f32),
            jax.ShapeDtypeStruct((S, D), bf16),
            jax.ShapeDtypeStruct((S, D), bf16),
            jax.ShapeDtypeStruct((1, LANES), f32),
            jax.ShapeDtypeStruct((1, D), f32),
        ],
        compiler_params=pltpu.CompilerParams(dimension_semantics=("arbitrary",)),
    )(h2, zg, pp, tgt, fn)


def _shift_down(v, halo, k):
    r = pltpu.roll(v, k, 0)
    hr = pltpu.roll(halo, k, 0)
    row = lax.broadcasted_iota(jnp.int32, hr.shape, 0)
    top = jnp.where(row < k, hr, r[0:SUBLANES])
    return top if v.shape[0] == SUBLANES else jnp.concatenate([top, r[SUBLANES:]], axis=0)


def _shift_up(v, halo, k):
    n = v.shape[0]
    r = pltpu.roll(v, n - k, 0)
    hr = pltpu.roll(halo, SUBLANES - k, 0)
    row = lax.broadcasted_iota(jnp.int32, hr.shape, 0)
    bot = jnp.where(row >= SUBLANES - k, hr, r[n - SUBLANES:])
    return jnp.concatenate([r[: n - SUBLANES], bot], axis=0)


def _conv_specs(S, F, tc, ts):
    nj = F // tc
    rows8 = ts // SUBLANES
    main = pl.BlockSpec((ts, 2 * tc), lambda j, i: (i, j))
    prev = pl.BlockSpec((SUBLANES, 2 * tc), lambda j, i: (jnp.maximum(i * rows8 - 1, 0), j))
    nxt = pl.BlockSpec((SUBLANES, 2 * tc), lambda j, i: (jnp.minimum((i + 1) * rows8, S // SUBLANES - 1), j))
    wg = pl.BlockSpec((3, tc), lambda j, i: (0, j))
    wv = pl.BlockSpec((3, tc), lambda j, i: (0, nj + j))
    bg = pl.BlockSpec((1, tc), lambda j, i: (0, j))
    bv = pl.BlockSpec((1, tc), lambda j, i: (0, nj + j))
    half = pl.BlockSpec((ts, tc), lambda j, i: (i, j))
    return nj, main, prev, nxt, wg, wv, bg, bv, half


def _conv_pre(u_ref, h_ref, wg_ref, wv_ref, bg_ref, bv_ref, tc):
    i = pl.program_id(1)
    u = u_ref[...]
    halo = jnp.where(i == 0, 0.0, h_ref[...])
    u1 = _shift_down(u, halo, 1)
    u2 = _shift_down(u, halo, 2)
    w = jnp.concatenate([wg_ref[...], wv_ref[...]], axis=1)
    b = jnp.concatenate([bg_ref[...], bv_ref[...]], axis=1)
    uc = b + w[0:1] * u2 + w[1:2] * u1 + w[2:3] * u
    return u, u1, u2, uc[:, :tc], uc[:, tc:]


def _convact_fwd(u, cw, cb, tc):
    S, F2 = u.shape
    F = F2 // 2
    ts = _div(S, 512, SUBLANES)
    nj, main, prev, nxt, wg, wv, bg, bv, half = _conv_specs(S, F, tc, ts)

    def body(u_ref, h_ref, wg_ref, wv_ref, bg_ref, bv_ref, m_ref):
        _, _, _, g, v = _conv_pre(u_ref, h_ref, wg_ref, wv_ref, bg_ref, bv_ref, tc)
        m_ref[...] = (g * _sigmoid(g) * v).astype(bf16)

    return pl.pallas_call(
        body,
        name="convact_fwd",
        grid=(nj, S // ts),
        in_specs=[main, prev, wg, wv, bg, bv],
        out_specs=half,
        out_shape=jax.ShapeDtypeStruct((S, F), bf16),
        compiler_params=pltpu.CompilerParams(dimension_semantics=("parallel", "parallel")),
    )(u, u, cw, cw, cb, cb)


def _convact_bwd(u, dm, cw, cb, tc, rider=None):
    S, F2 = u.shape
    F = F2 // 2
    ts = _div(S, 512, SUBLANES)
    nj, main, prev, nxt, wg, wv, bg, bv, half = _conv_specs(S, F, tc, ts)
    ni = S // ts
    half_nxt = pl.BlockSpec((SUBLANES, tc), lambda j, i: (jnp.minimum((i + 1) * (ts // SUBLANES), S // SUBLANES - 1), j))

    def act_bwd(g, v, d):
        sg = _sigmoid(g)
        return jnp.concatenate([d * v * sg * (1.0 + g * (1.0 - sg)), d * g * sg], axis=1)

    def body(u_ref, h_ref, n_ref, dm_ref, dmn_ref, wg_ref, wv_ref, bg_ref, bv_ref, du_ref, dwg_ref, dwv_ref, dbg_ref, dbv_ref):
        i = pl.program_id(1)
        u0, u1, u2, g, v = _conv_pre(u_ref, h_ref, wg_ref, wv_ref, bg_ref, bv_ref, tc)
        duc = act_bwd(g, v, dm_ref[...])
        w = jnp.concatenate([wg_ref[...], wv_ref[...]], axis=1)
        b = jnp.concatenate([bg_ref[...], bv_ref[...]], axis=1)
        un = n_ref[...]
        tail = u0[ts - SUBLANES:]
        ucn = b + w[0:1] * _shift_down(un, tail, 2) + w[1:2] * _shift_down(un, tail, 1) + w[2:3] * un
        ducn = jnp.where(i == ni - 1, 0.0, act_bwd(ucn[:, :tc], ucn[:, tc:], dmn_ref[...]))
        du_ref[...] = (w[2:3] * duc + w[1:2] * _shift_up(duc, ducn, 1) + w[0:1] * _shift_up(duc, ducn, 2)).astype(bf16)
        db = jnp.sum(duc, axis=0, keepdims=True)
        dw = jnp.concatenate(
            [jnp.sum(duc * u2, axis=0, keepdims=True), jnp.sum(duc * u1, axis=0, keepdims=True), jnp.sum(duc * u0, axis=0, keepdims=True)],
            axis=0,
        )

        @pl.when(i == 0)
        def _():
            dwg_ref[...] = dw[:, :tc]
            dwv_ref[...] = dw[:, tc:]
            dbg_ref[...] = db[:, :tc]
            dbv_ref[...] = db[:, tc:]

        @pl.when(i > 0)
        def _():
            dwg_ref[...] += dw[:, :tc]
            dwv_ref[...] += dw[:, tc:]
            dbg_ref[...] += db[:, :tc]
            dbv_ref[...] += db[:, tc:]

    w_out = pl.BlockSpec((3, tc), lambda j, i: (0, j))
    b_out = pl.BlockSpec((1, tc), lambda j, i: (0, j))
    return _grid_call(
        body,
        rider,
        name="convact_bwd",
        grid=(nj, ni),
        in_specs=[main, prev, nxt, half, half_nxt, wg, wv, bg, bv],
        out_specs=[main, w_out, w_out, b_out, b_out],
        out_shape=[
            jax.ShapeDtypeStruct((S, F2), bf16),
            jax.ShapeDtypeStruct((3, F), f32),
            jax.ShapeDtypeStruct((3, F), f32),
            jax.ShapeDtypeStruct((1, F), f32),
            jax.ShapeDtypeStruct((1, F), f32),
        ],
        scratch_shapes=[],
        args=(u, u, u, dm, dm, cw, cw, cb, cb),
    )


def _hg_gates(qp, fp, lbl):
    lb = _sigmoid(lbl[0:1] - lbl[1:2])
    sig = _sigmoid(fp)
    sigm = _sigmoid(-fp)
    f = lb + (1.0 - lb) * sig
    k = (1.0 - lb) * sigm
    sq = _sigmoid(qp)
    qf = qp * sq
    row = lax.broadcasted_iota(jnp.int32, (CHUNK, CHUNK), 0)
    col = lax.broadcasted_iota(jnp.int32, (CHUNK, CHUNK), 1)
    b = _dot((row >= col).astype(f32), jnp.log(f), NN, HIGHEST)
    return lb, sig, sigm, f, k, sq, qf, b


def _hg_block(qf, k, b, I):
    r0, n = I * HG_SUB, (I + 1) * HG_SUB
    base = b[r0 - 1:r0] if I > 0 else jnp.zeros_like(b[0:1])
    eq = jnp.exp(b[r0:n] - base)
    ek = jnp.exp(base - b[0:n])
    qt16 = (qf[r0:n] * eq).astype(bf16)
    kt16 = (k[0:n] * ek).astype(bf16)
    row = lax.broadcasted_iota(jnp.int32, (HG_SUB, n), 0) + r0
    col = lax.broadcasted_iota(jnp.int32, (HG_SUB, n), 1)
    return r0, n, eq, ek, qt16, kt16, col <= row


def _per_head(v, hp, fn):
    return jnp.concatenate([fn(v[:, p * HEAD_DIM:(p + 1) * HEAD_DIM]) for p in range(hp)], axis=1)


def _head_mean(v, hp):
    return _per_head(v, hp, lambda t: jnp.broadcast_to(jnp.mean(t, axis=-1, keepdims=True), t.shape))


def _pad_rows(v, rows):
    return v if v.shape[0] == rows else jnp.concatenate([v, jnp.zeros((rows - v.shape[0], v.shape[1]), v.dtype)], axis=0)


def _hgrn_specs(S, W, hp, reverse):
    nc = S // CHUNK
    ngrp = W // (hp * HEAD_DIM)
    cw = hp * HEAD_DIM
    cidx = (lambda c: nc - 1 - c) if reverse else (lambda c: c)

    def proj(off):
        return pl.BlockSpec((CHUNK, cw), lambda h, c: (cidx(c), off * ngrp + h))

    act = pl.BlockSpec((CHUNK, cw), lambda h, c: (cidx(c), h))
    lbl = pl.BlockSpec((2, cw), lambda h, c: (0, h))
    gn = pl.BlockSpec((1, HEAD_DIM), lambda h, c: (0, 0))
    st = pl.BlockSpec((hp, 1, HEAD_DIM, HEAD_DIM), lambda h, c: (h, cidx(c), 0, 0))
    return nc, ngrp, proj, act, lbl, gn, st


def _grid_call(body, rider, *, name, grid, in_specs, out_specs, out_shape, scratch_shapes, args):
    params = pltpu.CompilerParams(dimension_semantics=("arbitrary",) * len(grid))
    if rider is None:
        return pl.pallas_call(body, name=name, grid=grid, in_specs=in_specs, out_specs=out_specs, out_shape=out_shape,
                              scratch_shapes=scratch_shapes, compiler_params=params)(*args)
    n_in, n_out, n_scr = len(in_specs), len(out_specs), len(scratch_shapes)
    ri, ro = len(rider.ins), len(rider.out_shapes)

    def wrapped(*refs):
        a, b = n_in, n_in + ri
        c, d = b + n_out, b + n_out + ro
        e = d + n_scr
        ids = [pl.program_id(t) for t in range(len(grid))]
        is_first = functools.reduce(jnp.logical_and, [i == 0 for i in ids])
        is_last = functools.reduce(jnp.logical_and, [i == g - 1 for i, g in zip(ids, grid)])

        @pl.when(is_first)
        def _():
            rider.first(refs[a:b], refs[c:d], refs[e:])

        body(*refs[:a], *refs[b:c], *refs[d:e])

        @pl.when(is_last)
        def _():
            rider.last(refs[a:b], refs[c:d], refs[e:])

    return pl.pallas_call(
        wrapped,
        name=name,
        grid=grid,
        in_specs=list(in_specs) + [_hbm()] * ri,
        out_specs=list(out_specs) + [_hbm()] * ro,
        out_shape=list(out_shape) + list(rider.out_shapes),
        input_output_aliases={n_in + i: n_out + o for i, o in rider.aliases.items()},
        scratch_shapes=list(scratch_shapes) + list(rider.sems),
        compiler_params=params,
    )(*args, *rider.ins)


def _hgrn_fwd(proj, lb_logits, hg_norm, W, rider=None):
    S = proj.shape[0]
    hp = min(HG_HEADS_PER_STEP, W // HEAD_DIM)
    nc, ngrp, pspec, act, lbl_spec, gn_spec, st_spec = _hgrn_specs(S, W, hp, False)
    nb = CHUNK // HG_SUB

    def body(q_ref, f_ref, i_ref, g_ref, lbl_ref, gn_ref, y_ref, o_ref, st_ref, state):
        c = pl.program_id(1)

        @pl.when(c == 0)
        def _():
            state[...] = jnp.zeros_like(state)

        _, _, _, _, k, _, qf, b = _hg_gates(q_ref[...], f_ref[...], lbl_ref[...])
        v16 = i_ref[...].astype(bf16)
        a16 = (qf * jnp.exp(b)).astype(bf16)
        bl = b[CHUNK - 1:CHUNK]
        kd16 = (k * jnp.exp(bl - b)).astype(bf16)
        ebl = jnp.exp(bl)
        blocks = [_hg_block(qf, k, b, I) for I in range(nb)]
        heads = [slice(p * HEAD_DIM, (p + 1) * HEAD_DIM) for p in range(hp)]
        st_prev = state[...]
        st_ref[:, 0] = st_prev
        inter = [_dot(a16[:, sl], st_prev[p].astype(bf16), NT) for p, sl in enumerate(heads)]
        scores = [[_dot(qt16[:, sl], kt16[:, sl], NT) for sl in heads] for _, _, _, _, qt16, kt16, _ in blocks]
        intra = [[_dot(jnp.where(blk[6], sc, 0.0).astype(bf16), v16[0:blk[1], sl], NN) for sc, sl in zip(scs, heads)]
                 for blk, scs in zip(blocks, scores)]
        st_new = [st_prev[p] * ebl[:, sl] + _dot(v16[:, sl], kd16[:, sl], TN) for p, sl in enumerate(heads)]
        state[...] = jnp.stack(st_new)
        o = jnp.concatenate([inter[p] + jnp.concatenate([rows[p] for rows in intra], axis=0) for p in range(hp)], axis=1)
        o_ref[...] = o
        r = lax.rsqrt(_head_mean(o * o, hp) + EPS)
        gp = g_ref[...]
        y_ref[...] = (o * r * jnp.tile(gn_ref[...], (1, hp)) * (gp * _sigmoid(gp))).astype(bf16)

    H = W // HEAD_DIM
    return _grid_call(
        body,
        rider,
        name="hgrn_fwd",
        grid=(ngrp, nc),
        in_specs=[pspec(0), pspec(1), pspec(2), pspec(3), lbl_spec, gn_spec],
        out_specs=[act, act, st_spec],
        out_shape=[
            jax.ShapeDtypeStruct((S, W), bf16),
            jax.ShapeDtypeStruct((S, W), f32),
            jax.ShapeDtypeStruct((H, nc, HEAD_DIM, HEAD_DIM), f32),
        ],
        scratch_shapes=[pltpu.VMEM((hp, HEAD_DIM, HEAD_DIM), f32)],
        args=(proj, proj, proj, proj, lb_logits, hg_norm),
    )


def _hgrn_bwd(proj, o_hg, st, dycat, lb_logits, hg_norm, W, rider=None):
    S = proj.shape[0]
    hp = min(HG_HEADS_PER_STEP, W // HEAD_DIM)
    assert hp * HEAD_DIM == W, "the four gradients share one output block: all heads in one grid step"
    nc, ngrp, pspec, act, lbl_spec, gn_spec, st_spec = _hgrn_specs(S, W, hp, True)
    nb = CHUNK // HG_SUB

    def body(q_ref, f_ref, i_ref, g_ref, o_ref, st_ref, dy_ref, lbl_ref, gn_ref,
             d4_ref, dlbl_ref, dgn_ref, dstate):
        cw_ = hp * HEAD_DIM
        dq_ref, df_ref, di_ref, dg_ref = (d4_ref.at[:, pl.ds(t * cw_, cw_)] for t in range(4))
        h = pl.program_id(0)
        c = pl.program_id(1)

        @pl.when(c == 0)
        def _():
            dstate[...] = jnp.zeros_like(dstate)
            dlbl_ref[...] = jnp.zeros_like(dlbl_ref)

        @pl.when((c == 0) & (h == 0))
        def _():
            dgn_ref[...] = jnp.zeros_like(dgn_ref)

        row = lax.broadcasted_iota(jnp.int32, (CHUNK, CHUNK), 0)
        col = lax.broadcasted_iota(jnp.int32, (CHUNK, CHUNK), 1)
        upper = (col >= row).astype(f32)
        last_row = lax.broadcasted_iota(jnp.int32, (CHUNK, hp * HEAD_DIM), 0) == CHUNK - 1

        qp, fp = q_ref[...], f_ref[...]
        lb, sig, sigm, f, k, sq, qf, b = _hg_gates(qp, fp, lbl_ref[...])
        v16 = i_ref[...].astype(bf16)
        gnv = jnp.tile(gn_ref[...], (1, hp))
        gp = g_ref[...]
        sgg = _sigmoid(gp)
        gate = gp * sgg
        o = o_ref[...]
        r = lax.rsqrt(_head_mean(o * o, hp) + EPS)
        on = o * r
        dy = dy_ref[...]
        dg_ref[...] = (dy * on * gnv * (sgg * (1.0 + gp * (1.0 - sgg)))).astype(bf16)
        dgn_wide = jnp.sum(dy * on * gate, axis=0, keepdims=True)
        dgn_ref[...] += functools.reduce(jnp.add, [dgn_wide[:, p * HEAD_DIM:(p + 1) * HEAD_DIM] for p in range(hp)])
        don = dy * gnv * gate
        do16 = (r * (don - on * _head_mean(don * on, hp))).astype(bf16)
        eb = jnp.exp(b)
        A = qf * eb
        a16 = A.astype(bf16)
        bl = b[CHUNK - 1:CHUNK]
        ebl = jnp.exp(bl)
        ekd = jnp.exp(bl - b)
        kd = k * ekd
        kd16 = kd.astype(bf16)
        blocks = [_hg_block(qf, k, b, I) for I in range(nb)]

        heads = [slice(p * HEAD_DIM, (p + 1) * HEAD_DIM) for p in range(hp)]
        st_prev = st_ref[:, 0]
        dst_new = dstate[...]
        st16, dst16 = st_prev.astype(bf16), dst_new.astype(bf16)
        dA_h = [_dot(do16[:, sl], st16[p], NN) for p, sl in enumerate(heads)]
        dkd_h = [_dot(v16[:, sl], dst16[p], NN) for p, sl in enumerate(heads)]
        dv_h = [_dot(kd16[:, sl], dst16[p], NT) for p, sl in enumerate(heads)]
        dstate[...] = jnp.stack([dst_new[p] * ebl[:, sl] + _dot(do16[:, sl], a16[:, sl], TN) for p, sl in enumerate(heads)])
        dbl_h = [jnp.sum(dst_new[p] * st_prev[p], axis=0, keepdims=True) for p in range(hp)]
        sc_h = [[jnp.where(mask, _dot(qt16[:, sl], kt16[:, sl], NT), 0.0).astype(bf16) for sl in heads]
                for _, _, _, _, qt16, kt16, mask in blocks]
        dsc_h = [[jnp.where(mask, _dot(do16[r0:n, sl], v16[0:n, sl], NT), 0.0).astype(bf16) for sl in heads]
                 for r0, n, _, _, _, _, mask in blocks]
        dvi_h = [[_dot(sc, do16[blk[0]:blk[1], sl], TN) for sc, sl in zip(scs, heads)] for blk, scs in zip(blocks, sc_h)]
        dqt_h = [[_dot(dsc, blk[5][:, sl], NN) for dsc, sl in zip(dscs, heads)] for blk, dscs in zip(blocks, dsc_h)]
        dkt_h = [[_dot(dsc, blk[4][:, sl], TN) for dsc, sl in zip(dscs, heads)] for blk, dscs in zip(blocks, dsc_h)]
        dv_h = [functools.reduce(jnp.add, [dv_h[p]] + [_pad_rows(rows[p], CHUNK) for rows in dvi_h]) for p in range(hp)]

        dA, dkd = jnp.concatenate(dA_h, axis=1), jnp.concatenate(dkd_h, axis=1)
        dbl = jnp.concatenate(dbl_h, axis=1) * ebl + jnp.sum(dkd * kd, axis=0, keepdims=True)
        db = dA * A - dkd * kd + jnp.where(last_row, dbl, 0.0)
        dk = dkd * ekd
        dq_rows, db_rows = [], []
        for I, (r0, n, eq, ek, qt16, kt16, _) in enumerate(blocks):
            dqt, dkt = jnp.concatenate(dqt_h[I], axis=1), jnp.concatenate(dkt_h[I], axis=1)
            dq_rows.append(dqt * eq)
            dk = dk + _pad_rows(dkt * ek, CHUNK)
            db_rows.append(dqt * qt16.astype(f32))
            db = db - _pad_rows(dkt * kt16.astype(f32), CHUNK)
        dqf = dA * eb + jnp.concatenate(dq_rows, axis=0)
        db = db + jnp.concatenate(db_rows, axis=0)
        dlogf = _dot(upper, db, NN, HIGHEST)
        dfg = dlogf / f
        df_ref[...] = ((1.0 - lb) * sig * sigm * (dfg - dk)).astype(bf16)
        dlb = jnp.sum(sigm * (dfg - dk), axis=0, keepdims=True)
        dl0 = dlb * lb * (1.0 - lb)
        dlbl_ref[...] += jnp.concatenate([dl0, -dl0], axis=0)
        dq_ref[...] = (dqf * (sq * (1.0 + qp * (1.0 - sq)))).astype(bf16)
        di_ref[...] = jnp.concatenate(dv_h, axis=1).astype(bf16)

    cw = hp * HEAD_DIM
    dy_spec = pl.BlockSpec((CHUNK, cw), lambda h, c: (nc - 1 - c, h))
    return _grid_call(
        body,
        rider,
        name="hgrn_bwd",
        grid=(ngrp, nc),
        in_specs=[pspec(0), pspec(1), pspec(2), pspec(3), act, st_spec, dy_spec, lbl_spec, gn_spec],
        out_specs=[pl.BlockSpec((CHUNK, 4 * W), lambda h, c: (nc - 1 - c, 0)), lbl_spec, gn_spec],
        out_shape=[jax.ShapeDtypeStruct((S, 4 * W), bf16), jax.ShapeDtypeStruct((2, W), f32), jax.ShapeDtypeStruct((1, HEAD_DIM), f32)],
        scratch_shapes=[pltpu.VMEM((hp, HEAD_DIM, HEAD_DIM), f32)],
        args=(proj, proj, proj, proj, o_hg, st, dycat, lb_logits, hg_norm),
    )


def _att_dims(S):
    qt = ATT_Q_ROWS if S % ATT_Q_ROWS == 0 else CHUNK
    pad = LEFT_CHUNKS * CHUNK
    kb = pad + qt
    return qt, pad, kb, kb + qt


def _kv_padded(proj, W, pad):
    S = proj.shape[0]
    ts = _div(pad, 512, 2 * SUBLANES)
    assert S % ts == 0 and pad % ts == 0
    npad = pad // ts

    def body(k_ref, v_ref, ko_ref, vo_ref):
        i = pl.program_id(0)

        @pl.when(i < npad)
        def _():
            ko_ref[...] = jnp.zeros_like(ko_ref)
            vo_ref[...] = jnp.zeros_like(vo_ref)

        @pl.when(i >= npad)
        def _():
            ko_ref[...] = k_ref[...].astype(bf16)
            vo_ref[...] = v_ref[...].astype(bf16)

    out = pl.BlockSpec((ts, W), lambda i: (i, 0))
    return pl.pallas_call(
        body,
        name="kv_padded",
        grid=((S + pad) // ts,),
        in_specs=[pl.BlockSpec((ts, W), lambda i: (jnp.maximum(i - npad, 0), 5)), pl.BlockSpec((ts, W), lambda i: (jnp.maximum(i - npad, 0), 6))],
        out_specs=[out, out],
        out_shape=[jax.ShapeDtypeStruct((S + pad, W), bf16)] * 2,
        compiler_params=pltpu.CompilerParams(dimension_semantics=("parallel",)),
    )(proj, proj)


def _att_probs(qk, bias, start, qt, pad, kb):
    s = qk * (HEAD_DIM ** -0.5) + bias
    row = lax.broadcasted_iota(jnp.int32, (qt, kb), 0)
    col = lax.broadcasted_iota(jnp.int32, (qt, kb), 1)
    lo = jnp.bitwise_and(row, -CHUNK)
    ok = (col >= lo) & (col < lo + pad + CHUNK) & (col + start >= pad)
    s = jnp.where(ok, s, NEG)
    e = jnp.exp(s - jnp.max(s, axis=-1, keepdims=True))
    return e / jnp.sum(e, axis=-1, keepdims=True)


def _att_bias(ext_row, qt, kb, ne):
    e = jnp.broadcast_to(ext_row, (qt, ne))
    return pltpu.roll(e, ne - qt + 1, 1, stride=1, stride_axis=0)[:, :kb]


def _att_specs(S, H, ah, qt, pad, ne):
    cw = ah * HEAD_DIM
    q_spec = pl.BlockSpec((qt, cw), lambda h, g: (g, 4 * H // ah + h))
    kv_spec = pl.BlockSpec((S + pad, cw), lambda h, g: (0, h))
    ext_spec = pl.BlockSpec((ah, 1, ne), lambda h, g: (h, 0, 0))
    row_spec = pl.BlockSpec((qt, cw), lambda h, g: (g, h))
    return cw, q_spec, kv_spec, ext_spec, row_spec


def _attn_fwd(proj, kp, vp, ext, W, rider=None):
    S = proj.shape[0]
    H = W // HEAD_DIM
    ah = min(ATT_HEADS_PER_STEP, H)
    qt, pad, kb, ne = _att_dims(S)
    cw, q_spec, kv_spec, ext_spec, row_spec = _att_specs(S, H, ah, qt, pad, ne)
    heads = [slice(a * HEAD_DIM, (a + 1) * HEAD_DIM) for a in range(ah)]

    def body(q_ref, k_ref, v_ref, ext_ref, o_ref, bias):
        g = pl.program_id(1)

        @pl.when(g == 0)
        def _():
            for a in range(ah):
                bias[a] = _att_bias(ext_ref[a], qt, kb, ne)

        start = pl.multiple_of(g * qt, qt)
        q16 = q_ref[...].astype(bf16)
        kb16 = k_ref[pl.ds(start, kb), :]
        vb16 = v_ref[pl.ds(start, kb), :]
        qk = [_dot(q16[:, sl], kb16[:, sl], NT) for sl in heads]
        pn = [_att_probs(qk[a], bias[a], start, qt, pad, kb).astype(bf16) for a in range(ah)]
        o_ref[...] = jnp.concatenate([_dot(pn[a], vb16[:, sl], NN) for a, sl in enumerate(heads)], axis=1).astype(bf16)

    outs = _grid_call(
        body,
        rider,
        name="attn_fwd",
        grid=(H // ah, S // qt),
        in_specs=[q_spec, kv_spec, kv_spec, ext_spec],
        out_specs=[row_spec],
        out_shape=[jax.ShapeDtypeStruct((S, W), bf16)],
        scratch_shapes=[pltpu.VMEM((ah, qt, kb), f32)],
        args=(proj, kp, vp, ext),
    )
    return outs[0] if rider is None else outs


def _attn_bwd(proj, kp, vp, ext, dycat, W, rider=None):
    S = proj.shape[0]
    H = W // HEAD_DIM
    ah = min(ATT_HEADS_PER_STEP, H)
    qt, pad, kb, ne = _att_dims(S)
    cw, q_spec, kv_spec, ext_spec, row_spec = _att_specs(S, H, ah, qt, pad, ne)
    heads = [slice(a * HEAD_DIM, (a + 1) * HEAD_DIM) for a in range(ah)]
    scale = HEAD_DIM ** -0.5
    ng = S // qt

    def body(q_ref, k_ref, v_ref, ext_ref, do_ref, dq_ref, dko_ref, dvo_ref, db_ref, bias, dk_ref, dv_ref):
        g = pl.program_id(1)

        @pl.when(g == 0)
        def _():
            for a in range(ah):
                bias[a] = _att_bias(ext_ref[a], qt, kb, ne)
            dk_ref[...] = jnp.zeros_like(dk_ref)
            dv_ref[...] = jnp.zeros_like(dv_ref)
            db_ref[...] = jnp.zeros_like(db_ref)

        start = pl.multiple_of(g * qt, qt)
        q16 = q_ref[...].astype(bf16)
        kb16 = k_ref[pl.ds(start, kb), :]
        vb16 = v_ref[pl.ds(start, kb), :]
        do16 = do_ref[...].astype(bf16)
        qk = [_dot(q16[:, sl], kb16[:, sl], NT) for sl in heads]
        dpn = [_dot(do16[:, sl], vb16[:, sl], NT) for sl in heads]
        pn = [_att_probs(qk[a], bias[a], start, qt, pad, kb) for a in range(ah)]
        ds = [pn[a] * (dpn[a] - jnp.sum(dpn[a] * pn[a], axis=-1, keepdims=True)) for a in range(ah)]
        ds16 = [d.astype(bf16) for d in ds]
        dv_ref[pl.ds(start, kb), :] += jnp.concatenate([_dot(pn[a].astype(bf16), do16[:, sl], TN) for a, sl in enumerate(heads)], axis=1)
        dq_ref[...] = (jnp.concatenate([_dot(ds16[a], kb16[:, sl], NN) for a, sl in enumerate(heads)], axis=1) * scale).astype(bf16)
        dk_ref[pl.ds(start, kb), :] += jnp.concatenate([_dot(ds16[a], q16[:, sl], TN) for a, sl in enumerate(heads)], axis=1) * scale
        for a in range(ah):
            db_ref[a] += ds[a]

        @pl.when(g == ng - 1)
        def _():
            dko_ref[...] = dk_ref[pl.ds(pad, S), :].astype(bf16)
            dvo_ref[...] = dv_ref[pl.ds(pad, S), :].astype(bf16)

    kv_out = pl.BlockSpec((S, cw), lambda h, g: (0, h))
    return _grid_call(
        body,
        rider,
        name="attn_bwd",
        grid=(H // ah, S // qt),
        in_specs=[q_spec, kv_spec, kv_spec, ext_spec,
                  pl.BlockSpec((qt, cw), lambda h, g: (g, H // ah + h))],
        out_specs=[row_spec, kv_out, kv_out, pl.BlockSpec((ah, qt, kb), lambda h, g: (h, 0, 0))],
        out_shape=[
            jax.ShapeDtypeStruct((S, W), bf16),
            jax.ShapeDtypeStruct((S, W), bf16),
            jax.ShapeDtypeStruct((S, W), bf16),
            jax.ShapeDtypeStruct((H, qt, kb), f32),
        ],
        scratch_shapes=[pltpu.VMEM((ah, qt, kb), f32), pltpu.VMEM((S + pad, cw), f32), pltpu.VMEM((S + pad, cw), f32)],
        args=(proj, kp, vp, ext, dycat),
    )


def _bias_onehot(qt, ne, nrel_pad):
    m = lax.broadcasted_iota(jnp.int32, (nrel_pad, ne), 1)
    r = lax.broadcasted_iota(jnp.int32, (nrel_pad, ne), 0)
    rel = LEFT_CHUNKS * CHUNK + qt - 1 - m
    hot = (r == jnp.clip(rel, -REL_CLIP, REL_CLIP) + REL_CLIP) & (m < ne - 1)
    return hot.astype(f32)


def _bias_ext(rel_bias_pad, onehot):
    H, _, nr = rel_bias_pad.shape
    ne = onehot.shape[1]

    def body(rb_ref, oh_ref, o_ref):
        o_ref[0] = _dot(rb_ref[0], oh_ref[...], NN, HIGHEST)

    return pl.pallas_call(
        body,
        name="bias_ext",
        grid=(H,),
        in_specs=[pl.BlockSpec((1, 1, nr), lambda h: (h, 0, 0)), pl.BlockSpec((nr, ne), lambda h: (0, 0))],
        out_specs=pl.BlockSpec((1, 1, ne), lambda h: (h, 0, 0)),
        out_shape=jax.ShapeDtypeStruct((H, 1, ne), f32),
        compiler_params=pltpu.CompilerParams(dimension_semantics=("parallel",)),
    )(rel_bias_pad, onehot)


def _bias_bwd(dbias, onehot):
    H, qt, kb = dbias.shape
    nr, ne = onehot.shape

    def body(d_ref, oh_ref, o_ref):
        row = lax.broadcasted_iota(jnp.int32, (qt, qt), 0)
        col = lax.broadcasted_iota(jnp.int32, (qt, qt), 1)
        flip = (row + col == qt - 1).astype(f32)
        x = _dot(flip, d_ref[0], NN, HIGHEST)
        x = jnp.concatenate([x, jnp.zeros((qt, ne - kb), f32)], axis=1)
        de = jnp.sum(pltpu.roll(x, 0, 1, stride=1, stride_axis=0), axis=0, keepdims=True)
        o_ref[0] = _dot(de, oh_ref[...], NT, HIGHEST)

    return pl.pallas_call(
        body,
        name="bias_bwd",
        grid=(H,),
        in_specs=[pl.BlockSpec((1, qt, kb), lambda h: (h, 0, 0)), pl.BlockSpec((nr, ne), lambda h: (0, 0))],
        out_specs=pl.BlockSpec((1, 1, nr), lambda h: (h, 0, 0)),
        out_shape=jax.ShapeDtypeStruct((H, 1, nr), f32),
        compiler_params=pltpu.CompilerParams(dimension_semantics=("parallel",)),
    )(dbias, onehot)


def _place():
    x, y, c = lax.axis_index("x"), lax.axis_index("y"), lax.axis_index("c")
    return x, y, c


def _flip(x, y, k):
    return (1 - x if k & 2 else x), (1 - y if k & 1 else y)


def _region(ref, shard_shape, axis, j, half):
    R, C = shard_shape
    if axis == 0:
        if half is None:
            return ref.at[pl.ds(j * R, R), :]
        return ref.at[pl.ds(j * R + half * (R // 2), R // 2), :]
    if half is None:
        return ref.at[:, pl.ds(j * C, C)]
    return ref.at[pl.ds(half * (R // 2), R // 2), pl.ds(j * C, C)]


def _remote(src, dst, send_sem, recv_sem, dev):
    return pltpu.make_async_remote_copy(src_ref=src, dst_ref=dst, send_sem=send_sem, recv_sem=recv_sem,
                                        device_id=dev, device_id_type=MESH)


def _cast_place(shard, axis, where, *, name):
    R, C = shard.shape
    tr = _div(R, 256, 2 * SUBLANES)
    nr = R // tr
    full = (N_CHIPS * R, C) if axis == 0 else (R, N_CHIPS * C)
    omap = (lambda r, s: (s[0] * nr + r, 0)) if axis == 0 else (lambda r, s: (r, s[0]))

    def body(s_ref, x_ref, o_ref):
        o_ref[...] = x_ref[...].astype(bf16)

    return pl.pallas_call(
        body,
        name=name,
        grid_spec=pltpu.PrefetchScalarGridSpec(
            num_scalar_prefetch=1,
            grid=(nr,),
            in_specs=[pl.BlockSpec((tr, C), lambda r, s: (r, 0))],
            out_specs=pl.BlockSpec((tr, C), omap),
        ),
        out_shape=jax.ShapeDtypeStruct(full, bf16),
        compiler_params=pltpu.CompilerParams(dimension_semantics=("parallel",)),
    )(where, shard)


class _Rider(NamedTuple):
    ins: list
    out_shapes: list
    aliases: dict
    sems: list
    first: Callable
    last: Callable


def _run_rider(rider, *, name):
    ni, no = len(rider.ins), len(rider.out_shapes)

    def body(*refs):
        ins, outs, sems = refs[:ni], refs[ni:ni + no], refs[ni + no:]
        rider.first(ins, outs, sems)
        rider.last(ins, outs, sems)

    return pl.pallas_call(
        body,
        name=name,
        in_specs=[_hbm()] * ni,
        out_specs=[_hbm()] * no,
        out_shape=rider.out_shapes,
        input_output_aliases=rider.aliases,
        scratch_shapes=rider.sems,
    )(*rider.ins)


def _gather_rider(fulls, shard_shapes, axes, splits, rels):
    n = len(fulls)

    def stage1(ins, outs, sems):
        x, y, c = _place()
        jo = 2 * x + y
        cps = []
        for m in range(n):
            half = c if splits[m] else None
            for k in rels[m]:
                px, py = _flip(x, y, k)
                cps.append(_remote(_region(ins[m], shard_shapes[m], axes[m], jo, half), _region(outs[m], shard_shapes[m], axes[m], jo, half),
                                   sems[0].at[3 * m + k - 1], sems[1].at[3 * m + k - 1], (px, py, c)))
        return cps

    def first(ins, outs, sems):
        for cp in stage1(ins, outs, sems):
            cp.start()

    def last(ins, outs, sems):
        x, y, c = _place()
        send1, recv1, send2, recv2 = sems
        forwards = []
        for m in range(n):
            half = c if splits[m] else None
            for k in rels[m]:
                px, py = _flip(x, y, k)
                land = _region(outs[m], shard_shapes[m], axes[m], 2 * px + py, half)
                _remote(land, land, send1.at[3 * m + k - 1], recv1.at[3 * m + k - 1], (x, y, c)).wait_recv()
                if splits[m]:
                    fw = _remote(land, land, send2.at[3 * m + k - 1], recv2.at[3 * m + k - 1], (x, y, 1 - c))
                    fw.start()
                    forwards.append(fw)
        for m in range(n):
            if not splits[m]:
                continue
            for k in rels[m]:
                px, py = _flip(x, y, k)
                land = _region(outs[m], shard_shapes[m], axes[m], 2 * px + py, 1 - c)
                _remote(land, land, send2.at[3 * m + k - 1], recv2.at[3 * m + k - 1], (x, y, c)).wait_recv()
        for cp in stage1(ins, outs, sems) + forwards:
            cp.wait_send()

    sems = pltpu.SemaphoreType.DMA((3 * n,))
    return _Rider(list(fulls), [jax.ShapeDtypeStruct(f.shape, f.dtype) for f in fulls], {m: m for m in range(n)},
                  [sems, sems, sems, sems], first, last)


def _start_all(copies):
    def first(ins, outs, sems):
        for cp in copies(ins, outs, sems):
            cp.start()

    def last(ins, outs, sems):
        cps = copies(ins, outs, sems)
        for cp in cps:
            cp.wait_recv()
        for cp in cps:
            cp.wait_send()

    return first, last


def _mm_while_gathering(a, full, shard_shape, where, *, tm, tn, name):
    S, K = a.shape
    R, Cs = shard_shape
    N = full.shape[1]
    tps, nj, ni = Cs // tn, N // tn, S // tm
    assert R == K and Cs % tn == 0 and S % tm == 0

    def block(j, chip):
        return jnp.bitwise_xor(chip, j // tps) * tps + j % tps

    def body(s_ref, a_ref, w_in, o_ref, w_ref, buf, fsem, send1, recv1, send2, recv2):
        j, i = pl.program_id(0), pl.program_id(1)
        x, y, c = _place()
        chip = 2 * x + y

        def ici(k):
            px, py = _flip(x, y, k)
            mine = _region(w_ref, shard_shape, 1, chip, c)
            return _remote(mine, mine, send1.at[k - 1], recv1.at[k - 1], (px, py, c))

        def handed(k, half):
            px, py = _flip(x, y, k)
            land = _region(w_ref, shard_shape, 1, 2 * px + py, half)
            return land, _remote(land, land, send2.at[k - 1], recv2.at[k - 1], (x, y, 1 - c))

        def tile(jj):
            col = pl.multiple_of(block(jj, chip) * tn, LANES)
            return pltpu.make_async_copy(w_ref.at[:, pl.ds(col, tn)], buf.at[jj % 2], fsem.at[jj % 2])

        @pl.when((j == 0) & (i == 0))
        def _():
            for k in (1, 2, 3):
                ici(k).start()
            tile(j).start()

        @pl.when(i == 0)
        def _():
            tile(j).wait()

        @pl.when((i == ni - 1) & (j + 1 < nj))
        def _():
            for k in (1, 2, 3):
                @pl.when(j + 1 == k * tps)
                def _():
                    land, forward = handed(k, c)
                    _remote(land, land, send1.at[k - 1], recv1.at[k - 1], (x, y, c)).wait_recv()
                    forward.start()
                    handed(k, 1 - c)[1].wait_recv()

            tile(j + 1).start()

        o_ref[...] = _dot(a_ref[...], buf[j % 2], NN)

        @pl.when((j == nj - 1) & (i == ni - 1))
        def _():
            for k in (1, 2, 3):
                ici(k).wait_send()
                handed(k, c)[1].wait_send()

    sems = pltpu.SemaphoreType.DMA((3,))
    spec = pltpu.PrefetchScalarGridSpec(
        num_scalar_prefetch=1,
        grid=(nj, ni),
        in_specs=[pl.BlockSpec((tm, K), lambda j, i, s: (i, 0)), _hbm()],
        out_specs=[pl.BlockSpec((tm, tn), lambda j, i, s: (i, block(j, s[0]))), _hbm()],
        scratch_shapes=[pltpu.VMEM((2, K, tn), full.dtype), pltpu.SemaphoreType.DMA((2,)), sems, sems, sems, sems],
    )
    return pl.pallas_call(
        body,
        name=name,
        grid_spec=spec,
        out_shape=[jax.ShapeDtypeStruct((S, N), f32), jax.ShapeDtypeStruct(full.shape, full.dtype)],
        input_output_aliases={2: 1},
        compiler_params=pltpu.CompilerParams(dimension_semantics=("arbitrary", "arbitrary")),
    )(where, a, full)


def _chips_rider(ts, rels=(1, 2, 3), landing=None):
    n = len(ts)

    def copies(ins, outs, sems):
        x, y, c = _place()
        cps = []
        for m in range(n):
            for k in rels:
                px, py = _flip(x, y, k)
                cps.append(_remote(ins[m].at[2 * px + py], outs[m].at[k - 1], sems[0].at[3 * m + k - 1], sems[1].at[3 * m + k - 1], (px, py, c)))
        return cps

    sems = pltpu.SemaphoreType.DMA((3 * n,))
    return _Rider(list(ts) + list(landing or []), [jax.ShapeDtypeStruct((3,) + t.shape[1:], t.dtype) for t in ts],
                  {n + m: m for m in range(n)} if landing else {}, [sems, sems], *_start_all(copies))


def _sibling_rider(dws, shard_shapes, axes):
    n = len(dws)

    def copies(ins, outs, sems):
        x, y, c = _place()
        return [_remote(_region(ins[m], shard_shapes[m], axes[m], j, 1 - c), outs[m].at[j],
                        sems[0].at[N_CHIPS * m + j], sems[1].at[N_CHIPS * m + j], (x, y, 1 - c))
                for m in range(n) for j in range(N_CHIPS)]

    sems = pltpu.SemaphoreType.DMA((N_CHIPS * n,))
    return _Rider(list(dws), [jax.ShapeDtypeStruct((N_CHIPS, s[0] // 2, s[1]), d.dtype) for s, d in zip(shard_shapes, dws)], {},
                  [sems, sems], *_start_all(copies))


def _join_rider(gs):
    n = len(gs)

    def copies(ins, outs, sems):
        x, y, c = _place()
        cps = []
        for m in range(n):
            hr = gs[m].shape[0] // 2
            cps.append(_remote(ins[m].at[pl.ds(c * hr, hr), :], outs[m].at[pl.ds(c * hr, hr), :], sems[0].at[m], sems[1].at[m], (x, y, 1 - c)))
        return cps

    sems = pltpu.SemaphoreType.DMA((n,))
    return _Rider(list(gs), [jax.ShapeDtypeStruct(g.shape, g.dtype) for g in gs], {m: m for m in range(n)}, [sems, sems], *_start_all(copies))


def _both(r1, r2):
    i1, o1, s1 = len(r1.ins), len(r1.out_shapes), len(r1.sems)
    aliases = dict(r1.aliases)
    aliases.update({i1 + i: o1 + o for i, o in r2.aliases.items()})

    def first(ins, outs, sems):
        r1.first(ins[:i1], outs[:o1], sems[:s1])
        r2.first(ins[i1:], outs[o1:], sems[s1:])

    def last(ins, outs, sems):
        r1.last(ins[:i1], outs[:o1], sems[:s1])
        r2.last(ins[i1:], outs[o1:], sems[s1:])

    return _Rider(r1.ins + r2.ins, r1.out_shapes + r2.out_shapes, aliases, r1.sems + r2.sems, first, last)


def _pair_sum(dw, got, shard_shape, axis, where, *, name):
    R, C = shard_shape
    hr = R // 2
    tr = _div(hr, 256, 2 * SUBLANES)
    nr = hr // tr
    dmap = (lambda j, r, s: ((2 * j + s[1]) * nr + r, 0)) if axis == 0 else (lambda j, r, s: (s[1] * nr + r, j))
    slot = pl.BlockSpec((1, tr, C), lambda j, r, s: (j, r, 0))

    def body(s_ref, d_ref, g_ref, o_ref):
        o_ref[0] = (d_ref[...].astype(f32) + g_ref[0].astype(f32)).astype(bf16)

    return pl.pallas_call(
        body,
        name=name,
        grid_spec=pltpu.PrefetchScalarGridSpec(
            num_scalar_prefetch=1,
            grid=(N_CHIPS, nr),
            in_specs=[pl.BlockSpec((tr, C), dmap), slot],
            out_specs=slot,
        ),
        out_shape=jax.ShapeDtypeStruct((N_CHIPS, hr, C), bf16),
        compiler_params=pltpu.CompilerParams(dimension_semantics=("parallel", "parallel")),
    )(where, dw, got)


def _chip_sum(pair, others, where, *, name):
    _, hr, C = pair.shape
    tr = _div(hr, 256, 2 * SUBLANES)
    nr = hr // tr

    def body(s_ref, p_ref, o_ref, g_ref):
        g_ref[...] = p_ref[0].astype(f32) + o_ref[0].astype(f32) + o_ref[1].astype(f32) + o_ref[2].astype(f32)

    return pl.pallas_call(
        body,
        name=name,
        grid_spec=pltpu.PrefetchScalarGridSpec(
            num_scalar_prefetch=1,
            grid=(nr,),
            in_specs=[pl.BlockSpec((1, tr, C), lambda r, s: (s[0], r, 0)), pl.BlockSpec((3, tr, C), lambda r, s: (0, r, 0))],
            out_specs=pl.BlockSpec((tr, C), lambda r, s: (s[1] * nr + r, 0)),
        ),
        out_shape=jax.ShapeDtypeStruct((2 * hr, C), f32),
        compiler_params=pltpu.CompilerParams(dimension_semantics=("parallel",)),
    )(where, pair, others)


def _allreduce_small(v):
    rows = v.shape[0]

    def body(v_ref, o_ref, gath, send, recv):
        x, y, c = _place()
        me = 4 * x + 2 * y + c
        gath[pl.ds(me, 1)] = v_ref[...][None]
        cps = []
        for k in range(1, N_DEV):
            peer = (1 - x if k & 4 else x, 1 - y if k & 2 else y, 1 - c if k & 1 else c)
            cp = _remote(v_ref, gath.at[me], send.at[k - 1], recv.at[k - 1], peer)
            cp.start()
            cps.append(cp)
        for cp in cps:
            cp.wait_recv()
        for cp in cps:
            cp.wait_send()
        acc = gath[0]
        for d in range(1, N_DEV):
            acc = acc + gath[d]
        o_ref[...] = acc

    return pl.pallas_call(
        body,
        name="small_allreduce",
        in_specs=[_vmem()],
        out_specs=_vmem(),
        out_shape=jax.ShapeDtypeStruct(v.shape, f32),
        scratch_shapes=[pltpu.VMEM((N_DEV, rows, LANES), f32), pltpu.SemaphoreType.DMA((N_DEV - 1,)), pltpu.SemaphoreType.DMA((N_DEV - 1,))],
    )(v)


def _adamw_math(w, g, m, v):
    m = ADAM_B1 * m + (1.0 - ADAM_B1) * g
    v = ADAM_B2 * v + (1.0 - ADAM_B2) * (g * g)
    m_hat = m / (1.0 - ADAM_B1 ** ADAM_STEP)
    v_hat = v / (1.0 - ADAM_B2 ** ADAM_STEP)
    delta = -ADAM_LR * (m_hat / (jnp.sqrt(v_hat) + ADAM_EPS) + ADAM_WD * w)
    return delta, m, v


def _adamw(w, g, m, v, *, name):
    R, C = w.shape
    tr = _div(R, 128, SUBLANES)

    def body(w_ref, g_ref, m_ref, v_ref, d_ref, mo_ref, vo_ref):
        d, mn, vn = _adamw_math(w_ref[...], g_ref[...], m_ref[...], v_ref[...])
        d_ref[...] = d
        mo_ref[...] = mn
        vo_ref[...] = vn

    spec = pl.BlockSpec((tr, C), lambda r: (r, 0))
    return pl.pallas_call(
        body,
        name=name,
        grid=(R // tr,),
        in_specs=[spec] * 4,
        out_specs=[spec] * 3,
        out_shape=[jax.ShapeDtypeStruct((R, C), f32)] * 3,
        compiler_params=pltpu.CompilerParams(dimension_semantics=("parallel",)),
    )(w, g, m, v)


def _pack(arrs):
    flat = jnp.concatenate([a.reshape(-1).astype(f32) for a in arrs])
    tile = SUBLANES * LANES
    total = -(-flat.shape[0] // tile) * tile
    return jnp.pad(flat, (0, total - flat.shape[0])).reshape(total // LANES, LANES)


def _unpack(buf, shapes):
    flat = buf.reshape(-1)
    out, off = [], 0
    for s in shapes:
        size = int(np.prod(s))
        out.append(flat[off:off + size].reshape(s))
        off += size
    return out


def kernel(x, p, norm_mix, w_in, lb_logits, hg_norm, rel_bias, w_out, norm_ffn, w_up, conv_w, conv_b, w_down, norm_ple, w_ple_gate, w_ple_proj, final_norm, loss_target, m_norm_mix, m_w_in, m_lb_logits, m_hg_norm, m_rel_bias, m_w_out, m_norm_ffn, m_w_up, m_conv_w, m_conv_b, m_w_down, m_norm_ple, m_w_ple_gate, m_w_ple_proj, m_final_norm, v_norm_mix, v_w_in, v_lb_logits, v_hg_norm, v_rel_bias, v_w_out, v_norm_ffn, v_w_up, v_conv_w, v_conv_b, v_w_down, v_norm_ple, v_w_ple_gate, v_w_ple_proj, v_final_norm):
    S, D = x.shape[1], x.shape[2]
    xv, pv, tgt = x[0], p[0, 0], loss_target[0]
    W = (w_in.shape[2] * N_CHIPS) // 7
    H = W // HEAD_DIM
    F = w_down.shape[1] * N_CHIPS
    tc = _div(F // 2, 1408, LANES)
    nj = F // tc
    jx, jy = lax.axis_index("x"), lax.axis_index("y")
    chip = 2 * jx + jy

    big = [w_in[0], w_out[0], w_up[0], w_down[0], w_ple_gate[0], w_ple_proj[0]]
    big_axes = [1, 0, 1, 0, 0, 1]
    where = jnp.stack([chip, lax.axis_index("c")]).astype(jnp.int32)
    cw_pad = jnp.pad(conv_w[0], ((0, SUBLANES - conv_w.shape[1]), (0, 0)))
    cw_mine = lax.dynamic_update_slice(jnp.zeros((N_CHIPS * SUBLANES, cw_pad.shape[1]), f32), cw_pad, (chip * SUBLANES, 0))
    placed = [_cast_place(b, ax, where, name=f"cast_place_{i}") for i, (b, ax) in enumerate(zip(big, big_axes))]
    shard_shapes = [b.shape for b in big]

    every = (1, 2, 3)

    def gather(items):
        return _gather_rider([b for b, _, _ in items],
                             [cw_pad.shape if i is None else shard_shapes[i] for _, i, _ in items],
                             [0 if i is None else big_axes[i] for _, i, _ in items],
                             [i is not None for _, i, _ in items], [r for _, _, r in items])

    tm = _div(S, 1024, LANES)

    a1 = _rms_fwd(xv, norm_mix, name="rms_mix")
    cs_in = shard_shapes[0][1]
    tn_in = _div(cs_in, 1024, LANES)
    proj, W_in = _mm_while_gathering(a1, placed[0], shard_shapes[0], where, tm=tm, tn=tn_in, name="mm_in")
    y_hg, o_hg, st, W_up_part, W_out_part = _hgrn_fwd(proj, lb_logits, hg_norm, W,
                                                      rider=gather([(placed[2], 2, (1, 2)), (placed[1], 1, (1, 2))]))

    qt, pad, kb, ne = _att_dims(S)
    nrel = rel_bias.shape[2]
    nrel_pad = -(-nrel // LANES) * LANES
    onehot = _bias_onehot(qt, ne, nrel_pad)
    rb_pad = jnp.pad(rel_bias[0], ((0, 0), (0, nrel_pad - nrel)))[:, None, :]
    ext = _bias_ext(rb_pad, onehot)
    kp, vp = _kv_padded(proj, W, pad)
    y_att, W_up, W_out = _attn_fwd(proj, kp, vp, ext, W, rider=gather([(W_up_part, 2, (3,)), (W_out_part, 1, (3,))]))
    ycat = jnp.concatenate([y_hg, y_att], axis=1)

    tn_d = _div(D, 512, LANES)
    h1 = _mm(ycat, W_out, dims="nn", tm=tm, tn=tn_d, tk=D, out_dtype=f32, res=xv, name="mm_out")
    a2 = _rms_fwd(h1, norm_ffn, name="rms_ffn")
    perm = lambda b: (b % 2) * nj + b // 2
    u, W_down, W_pg, W_pp, cw_all = _mm(
        a2, W_up, dims="nn", tm=tm, tn=tc, tk=D, out_dtype=f32, name="mm_up", b_idx=lambda i, j, k: (k, perm(j)),
        rider=gather([(placed[3], 3, every), (placed[4], 4, every), (placed[5], 5, every), (cw_mine, None, every)]))
    ncw = conv_w.shape[1]
    cw_full = jnp.transpose(cw_all.reshape(N_CHIPS, SUBLANES, -1)[:, :ncw], (1, 0, 2)).reshape(ncw, -1)
    cb = conv_b
    mact = _convact_fwd(u, cw_full, cb, tc)
    tn_w = _div(D, 1024, LANES)
    h2 = _mm(mact, W_down, dims="nn", tm=_div(S, 512, LANES), tn=tn_w, tk=F, out_dtype=f32, res=h1, name="mm_down", cols_outer=True)
    a3 = _rms_fwd(h2, norm_ple, name="rms_ple")
    zg = _mm(a3, W_pg, dims="nn", tm=tm, tn=tn_d, tk=D, out_dtype=f32, name="mm_ple_gate")
    p16 = pv.astype(bf16)
    pp = _mm(p16, W_pp, dims="nn", tm=tm, tn=tn_d, tk=pv.shape[1], out_dtype=f32, name="mm_ple_proj")
    dh3, dzg, dpp, loss_part, d_fn = _tail(h2, zg, pp, tgt, final_norm.reshape(1, D))

    pair = {}

    def siblings(idx, grads):
        return _sibling_rider(grads, [shard_shapes[i] for i in idx], [big_axes[i] for i in idx])

    def pair_up(idx, grads, got):
        for i, d, g in zip(idx, grads, got):
            pair[i] = _pair_sum(d, g, shard_shapes[i], big_axes[i], where, name=f"grad_pair_sum_{i}")

    def reduced_half(i, landed):
        return _chip_sum(pair[i], landed, where, name=f"grad_chip_sum_{i}")

    tk_s = S
    dW_pp = _mm(p16, dpp, dims="tn", tm=pv.shape[1], tn=tn_w, tk=tk_s, out_dtype=bf16, name="mm_d_ple_proj")
    dW_pg = _mm(a3, dzg, dims="tn", tm=tn_w, tn=tn_w, tk=tk_s, out_dtype=bf16, name="mm_d_ple_gate")
    pair_up([4, 5], [dW_pg, dW_pp], _run_rider(siblings([4, 5], [dW_pg, dW_pp]), name="grads_to_sibling_4_5"))
    da3 = _mm(dzg, W_pg, dims="nt", tm=tm, tn=_div(D, 1024, LANES), tk=D, out_dtype=f32, name="mm_da3")
    dh2, dh2b, d_nple = _rms_bwd(h2, norm_ple, da3, dh3, name="rms_ple_bwd")
    dm, land4, land5 = _mm(dh2b, W_down, dims="nt", tm=tm, tn=_div(F, 512, LANES), tk=D, out_dtype=f32, name="mm_dm",
                           rider=_chips_rider([pair[4], pair[5]]))
    dW_down = _mm(mact, dh2b, dims="tn", tm=_div(F, 1408, LANES), tn=tn_d, tk=tk_s, out_dtype=bf16, name="mm_d_down")
    du, dcw_g, dcw_v, dcb_g, dcb_v, got3 = _convact_bwd(u, dm, cw_full, cb, tc, rider=siblings([3], [dW_down]))
    pair_up([3], [dW_down], [got3])
    dW_up, land3 = _mm(a2, du, dims="tn", tm=_div(D, 1024, LANES), tn=tc, tk=tk_s, out_dtype=bf16, name="mm_d_up",
                       o_idx=lambda i, j, k: (i, perm(j)), rider=_chips_rider([pair[3]]))
    da2, got2 = _mm(du, W_up, dims="nt", tm=tm, tn=D, tk=tc, out_dtype=f32, name="mm_da2",
                    b_idx=lambda i, j, k: (j, perm(k)), rider=siblings([2], [dW_up]))
    pair_up([2], [dW_up], [got2])
    dh1, dh1b, d_nffn = _rms_bwd(h1, norm_ffn, da2, dh2, name="rms_ffn_bwd")
    dycat = _mm(dh1b, W_out, dims="nt", tm=tm, tn=_div(D, 1024, LANES), tk=D, out_dtype=f32, name="mm_dycat")
    dW_out = _mm(ycat, dh1b, dims="tn", tm=tn_w, tn=tn_w, tk=tk_s, out_dtype=bf16, name="mm_d_out")
    pair_up([1], [dW_out], _run_rider(siblings([1], [dW_out]), name="grads_to_sibling_1"))
    dq_att, dk_att, dv_att, dbias, land2 = _attn_bwd(proj, kp, vp, ext, dycat, W, rider=_chips_rider([pair[2]], rels=(1, 2)))
    d_hg4, d_lbl, d_hgn, land2 = _hgrn_bwd(proj, o_hg, st, dycat, lb_logits, hg_norm, W,
                                           rider=_chips_rider([pair[2]], rels=(3,), landing=[land2]))
    d_rb = _bias_bwd(dbias, onehot)[:, 0, :nrel]
    dproj = jnp.concatenate([d_hg4, dq_att, dk_att, dv_att], axis=1)
    dW_in, land1 = _mm(a1, dproj, dims="tn", tm=tn_w, tn=_div(7 * W, 1024, LANES), tk=tk_s, out_dtype=bf16, name="mm_d_in",
                       rider=_chips_rider([pair[1]]))
    pair_up([0], [dW_in], _run_rider(siblings([0], [dW_in]), name="grads_to_sibling_0"))
    halves = [reduced_half(i, land) for i, land in ((1, land1), (2, land2), (3, land3), (4, land4), (5, land5))]
    da1, land0, *g_rest = _mm(dproj, W_in, dims="nt", tm=tm, tn=D, tk=_div(7 * W, 1792, LANES), out_dtype=f32, name="mm_da1",
                              rider=_both(_chips_rider([pair[0]]), _join_rider(halves)))
    grad_x, _, d_nmix = _rms_bwd(xv, norm_mix, da1, dh1, name="rms_mix_bwd")
    g_big = list(_run_rider(_join_rider([reduced_half(0, land0)]), name="grads_join_w_in")) + list(g_rest)

    d_cw = jnp.concatenate([dcw_g, dcw_v], axis=1)
    d_cb = jnp.concatenate([dcb_g, dcb_v], axis=1)
    small_parts = [loss_part[:, :1], d_nmix, d_lbl, d_hgn, d_rb, d_nffn, d_cw, d_cb, d_nple, d_fn]
    small_shapes = [(1, 1), (1, D), lb_logits.shape, hg_norm.shape, (H, nrel), (1, D), (ncw, 2 * F), (1, 2 * F), (1, D), (1, D)]
    red = _unpack(_allreduce_small(_pack(small_parts)), small_shapes)
    loss = red[0].reshape(())
    g_nmix, g_lbl, g_hgn, g_rb, g_nffn, g_cw_all, g_cb, g_nple, g_fn = red[1:]
    csh = conv_w.shape[2]
    g_cw = lax.dynamic_slice(g_cw_all, (0, chip * csh), (ncw, csh))
    small_g = [g_nmix, g_lbl, g_hgn, g_rb[None], g_nffn, g_cw[None], g_cb, g_nple, g_fn.reshape(D)]
    small_w = [norm_mix, lb_logits, hg_norm, rel_bias, norm_ffn, conv_w, conv_b, norm_ple, final_norm]
    small_m = [m_norm_mix, m_lb_logits, m_hg_norm, m_rel_bias, m_norm_ffn, m_conv_w, m_conv_b, m_norm_ple, m_final_norm]
    small_v = [v_norm_mix, v_lb_logits, v_hg_norm, v_rel_bias, v_norm_ffn, v_conv_w, v_conv_b, v_norm_ple, v_final_norm]
    shapes_s = [w.shape for w in small_w]
    sd, sm, sv = _adamw(_pack(small_w), _pack(small_g), _pack(small_m), _pack(small_v), name="adamw_small")
    small_g = [g.reshape(s) for g, s in zip(small_g, shapes_s)]
    small_d, small_nm, small_nv = _unpack(sd, shapes_s), _unpack(sm, shapes_s), _unpack(sv, shapes_s)

    big_m = [m_w_in[0], m_w_out[0], m_w_up[0], m_w_down[0], m_w_ple_gate[0], m_w_ple_proj[0]]
    big_v = [v_w_in[0], v_w_out[0], v_w_up[0], v_w_down[0], v_w_ple_gate[0], v_w_ple_proj[0]]
    big_d, big_nm, big_nv = [], [], []
    for i in range(6):
        d_, m_, v_ = _adamw(big[i], g_big[i], big_m[i], big_v[i], name=f"adamw_{i}")
        big_d.append(d_[None])
        big_nm.append(m_[None])
        big_nv.append(v_[None])
    g_big = [g[None] for g in g_big]

    def order(sm_list, bg_list):
        s, b = sm_list, bg_list
        return [s[0], b[0], s[1], s[2], s[3], b[1], s[4], b[2], s[5], s[6], b[3], s[7], b[4], b[5], s[8]]

    return (loss, grad_x[None], *order(small_g, g_big), *order(small_d, big_d), *order(small_nm, big_nm), *order(small_nv, big_nv))
```

```python
import functools
from typing import Callable, NamedTuple

import jax
import jax.numpy as jnp
import numpy as np
from jax import lax
from jax.experimental import pallas as pl
from jax.experimental.pallas import tpu as pltpu

f32 = jnp.float32
bf16 = jnp.bfloat16

CHUNK = 64
HEAD_DIM = 128
LEFT_CHUNKS = 8
REL_CLIP = 128
EPS = 1e-6
HG_SUB = 16
HG_HEADS_PER_STEP = 8
ATT_Q_ROWS = 256
ATT_HEADS_PER_STEP = 2
ADAM_LR, ADAM_B1, ADAM_B2, ADAM_EPS, ADAM_WD, ADAM_STEP = 0.001, 0.9, 0.999, 1e-08, 0.01, 10
LANES = 128
SUBLANES = 8
N_CHIPS = 4
N_DEV = 8
MESH = pl.DeviceIdType.MESH
NEG = float(np.finfo(np.float32).min)

NN = (((1,), (0,)), ((), ()))
NT = (((1,), (1,)), ((), ()))
TN = (((0,), (0,)), ((), ()))
HIGHEST = lax.Precision.HIGHEST


def _dot(a, b, dims, precision=None):
    return lax.dot_general(a, b, dims, preferred_element_type=f32, precision=precision)


def _sigmoid(v):
    return 1.0 / (1.0 + jnp.exp(-v))


def _div(n, pref, mult):
    best = None
    d = mult
    while d <= min(n, pref):
        if n % d == 0:
            best = d
        d += mult
    return best if best is not None else n


def _hbm():
    return pl.BlockSpec(memory_space=pltpu.HBM)


def _vmem():
    return pl.BlockSpec(memory_space=pltpu.VMEM)


def _mm(a, b, *, dims, tm, tn, tk, out_dtype, name, res=None, a_idx=None, b_idx=None, o_idx=None, out_shape=None, rider=None,
        cols_outer=False):
    if dims == "nn":
        (M, K), (_, N) = a.shape, b.shape
        a_blk, b_blk, dn = (tm, tk), (tk, tn), NN
        a_def, b_def = (lambda i, j, k: (i, k)), (lambda i, j, k: (k, j))
    elif dims == "nt":
        (M, K), (N, _) = a.shape, b.shape
        a_blk, b_blk, dn = (tm, tk), (tn, tk), NT
        a_def, b_def = (lambda i, j, k: (i, k)), (lambda i, j, k: (j, k))
    else:
        (K, M), (_, N) = a.shape, b.shape
        a_blk, b_blk, dn = (tk, tm), (tk, tn), TN
        a_def, b_def = (lambda i, j, k: (k, i)), (lambda i, j, k: (k, j))
    assert M % tm == 0 and N % tn == 0 and K % tk == 0, (name, M, N, K, tm, tn, tk)
    nk = K // tk
    has_res = res is not None

    def body(*refs):
        a_ref, b_ref = refs[0], refs[1]
        res_ref = refs[2] if has_res else None
        o_ref = refs[3] if has_res else refs[2]
        part = _dot(a_ref[...].astype(bf16), b_ref[...].astype(bf16), dn)

        def finish(acc):
            if has_res:
                acc = acc + res_ref[...]
            o_ref[...] = acc.astype(out_dtype)

        if nk == 1:
            finish(part)
        else:
            acc_ref = refs[-1]
            k = pl.program_id(2)

            @pl.when(k == 0)
            def _():
                acc_ref[...] = part

            @pl.when(k > 0)
            def _():
                acc_ref[...] += part

            @pl.when(k == nk - 1)
            def _():
                finish(acc_ref[...])

    def ordered(custom, default):
        f = custom or default
        return (lambda j, i, k: f(i, j, k)) if cols_outer else f

    o_map = ordered(o_idx, lambda i, j, k: (i, j))
    in_specs = [pl.BlockSpec(a_blk, ordered(a_idx, a_def)), pl.BlockSpec(b_blk, ordered(b_idx, b_def))]
    args = [a, b]
    if has_res:
        in_specs.append(pl.BlockSpec((tm, tn), o_map))
        args.append(res)
    outs = _grid_call(
        body,
        rider,
        name=name,
        grid=(N // tn, M // tm, nk) if cols_outer else (M // tm, N // tn, nk),
        in_specs=in_specs,
        out_specs=[pl.BlockSpec((tm, tn), o_map)],
        out_shape=[jax.ShapeDtypeStruct(out_shape or (M, N), out_dtype)],
        scratch_shapes=[pltpu.VMEM((tm, tn), f32)] if nk > 1 else [],
        args=args,
    )
    return outs[0] if rider is None else outs


def _rms_fwd(xv, g, *, name):
    S, D = xv.shape
    ts = _div(S, 512, SUBLANES)

    def body(x_ref, g_ref, o_ref):
        v = x_ref[...]
        r = lax.rsqrt(jnp.mean(v * v, axis=-1, keepdims=True) + EPS)
        o_ref[...] = (v * r * g_ref[...]).astype(bf16)

    return pl.pallas_call(
        body,
        name=name,
        grid=(S // ts,),
        in_specs=[pl.BlockSpec((ts, D), lambda i: (i, 0)), pl.BlockSpec((1, D), lambda i: (0, 0))],
        out_specs=pl.BlockSpec((ts, D), lambda i: (i, 0)),
        out_shape=jax.ShapeDtypeStruct((S, D), bf16),
        compiler_params=pltpu.CompilerParams(dimension_semantics=("parallel",)),
    )(xv, g)


def _rms_bwd(xv, g, dy, dres, *, name):
    S, D = xv.shape
    ts = _div(S, 256, SUBLANES)

    def body(x_ref, g_ref, dy_ref, dres_ref, dx_ref, dxb_ref, dg_ref):
        i = pl.program_id(0)
        v = x_ref[...]
        r = lax.rsqrt(jnp.mean(v * v, axis=-1, keepdims=True) + EPS)
        vn = v * r
        d = dy_ref[...]
        part = jnp.sum(d * vn, axis=0, keepdims=True)

        @pl.when(i == 0)
        def _():
            dg_ref[...] = part

        @pl.when(i > 0)
        def _():
            dg_ref[...] += part

        t = d * g_ref[...]
        dx = dres_ref[...] + r * (t - vn * jnp.mean(t * vn, axis=-1, keepdims=True))
        dx_ref[...] = dx
        dxb_ref[...] = dx.astype(bf16)

    row = pl.BlockSpec((ts, D), lambda i: (i, 0))
    vec = pl.BlockSpec((1, D), lambda i: (0, 0))
    return pl.pallas_call(
        body,
        name=name,
        grid=(S // ts,),
        in_specs=[row, vec, row, row],
        out_specs=[row, row, vec],
        out_shape=[jax.ShapeDtypeStruct((S, D), f32), jax.ShapeDtypeStruct((S, D), bf16), jax.ShapeDtypeStruct((1, D), f32)],
        compiler_params=pltpu.CompilerParams(dimension_semantics=("arbitrary",)),
    )(xv, g, dy, dres)


def _tail(h2, zg, pp, tgt, fn):
    S, D = h2.shape
    ts = _div(S, 256, SUBLANES)

    def body(h_ref, z_ref, p_ref, t_ref, fn_ref, dh_ref, dz_ref, dp_ref, loss_ref, dfn_ref):
        i = pl.program_id(0)
        sg = _sigmoid(z_ref[...])
        ppv = p_ref[...]
        h3 = h_ref[...] + sg * ppv
        r = lax.rsqrt(jnp.mean(h3 * h3, axis=-1, keepdims=True) + EPS)
        hn = h3 * r
        fnv = fn_ref[...]
        e = hn * fnv - t_ref[...]
        lpart = 0.5 * jnp.sum(jnp.mean(e * e, axis=-1, keepdims=True), axis=0, keepdims=True)
        dy = e * (1.0 / D)
        dpart = jnp.sum(dy * hn, axis=0, keepdims=True)

        @pl.when(i == 0)
        def _():
            loss_ref[...] = jnp.broadcast_to(lpart, loss_ref.shape)
            dfn_ref[...] = dpart

        @pl.when(i > 0)
        def _():
            loss_ref[...] += jnp.broadcast_to(lpart, loss_ref.shape)
            dfn_ref[...] += dpart

        t = dy * fnv
        dh3 = r * (t - hn * jnp.mean(t * hn, axis=-1, keepdims=True))
        dh_ref[...] = dh3
        dz_ref[...] = (dh3 * ppv * sg * (1.0 - sg)).astype(bf16)
        dp_ref[...] = (dh3 * sg).astype(bf16)

    row = pl.BlockSpec((ts, D), lambda i: (i, 0))
    vec = pl.BlockSpec((1, D), lambda i: (0, 0))
    one = pl.BlockSpec((1, LANES), lambda i: (0, 0))
    return pl.pallas_call(
        body,
        name="ple_final_loss",
        grid=(S // ts,),
        in_specs=[row, row, row, row, vec],
        out_specs=[row, row, row, one, vec],
        out_shape=[
            jax.ShapeDtypeStruct((S, D), f32),
            jax.ShapeDtypeStruct((S, D), bf16),
            jax.ShapeDtypeStruct((S, D), bf16),
            jax.ShapeDtypeStruct((1, LANES), f32),
            jax.ShapeDtypeStruct((1, D), f32),
        ],
        compiler_params=pltpu.CompilerParams(dimension_semantics=("arbitrary",)),
    )(h2, zg, pp, tgt, fn)


def _shift_down(v, halo, k):
    r = pltpu.roll(v, k, 0)
    hr = pltpu.roll(halo, k, 0)
    row = lax.broadcasted_iota(jnp.int32, hr.shape, 0)
    top = jnp.where(row < k, hr, r[0:SUBLANES])
    return top if v.shape[0] == SUBLANES else jnp.concatenate([top, r[SUBLANES:]], axis=0)


def _shift_up(v, halo, k):
    n = v.shape[0]
    r = pltpu.roll(v, n - k, 0)
    hr = pltpu.roll(halo, SUBLANES - k, 0)
    row = lax.broadcasted_iota(jnp.int32, hr.shape, 0)
    bot = jnp.where(row >= SUBLANES - k, hr, r[n - SUBLANES:])
    return jnp.concatenate([r[: n - SUBLANES], bot], axis=0)


def _conv_specs(S, F, tc, ts):
    nj = F // tc
    rows8 = ts // SUBLANES
    main = pl.BlockSpec((ts, 2 * tc), lambda j, i: (i, j))
    prev = pl.BlockSpec((SUBLANES, 2 * tc), lambda j, i: (jnp.maximum(i * rows8 - 1, 0), j))
    nxt = pl.BlockSpec((SUBLANES, 2 * tc), lambda j, i: (jnp.minimum((i + 1) * rows8, S // SUBLANES - 1), j))
    wg = pl.BlockSpec((3, tc), lambda j, i: (0, j))
    wv = pl.BlockSpec((3, tc), lambda j, i: (0, nj + j))
    bg = pl.BlockSpec((1, tc), lambda j, i: (0, j))
    bv = pl.BlockSpec((1, tc), lambda j, i: (0, nj + j))
    half = pl.BlockSpec((ts, tc), lambda j, i: (i, j))
    return nj, main, prev, nxt, wg, wv, bg, bv, half


def _conv_pre(u_ref, h_ref, wg_ref, wv_ref, bg_ref, bv_ref, tc):
    i = pl.program_id(1)
    u = u_ref[...]
    halo = jnp.where(i == 0, 0.0, h_ref[...])
    u1 = _shift_down(u, halo, 1)
    u2 = _shift_down(u, halo, 2)
    w = jnp.concatenate([wg_ref[...], wv_ref[...]], axis=1)
    b = jnp.concatenate([bg_ref[...], bv_ref[...]], axis=1)
    uc = b + w[0:1] * u2 + w[1:2] * u1 + w[2:3] * u
    return u, u1, u2, uc[:, :tc], uc[:, tc:]


def _convact_fwd(u, cw, cb, tc, rider=None):
    S, F2 = u.shape
    F = F2 // 2
    ts = _div(S, 512, SUBLANES)
    nj, main, prev, nxt, wg, wv, bg, bv, half = _conv_specs(S, F, tc, ts)

    def body(u_ref, h_ref, wg_ref, wv_ref, bg_ref, bv_ref, m_ref):
        _, _, _, g, v = _conv_pre(u_ref, h_ref, wg_ref, wv_ref, bg_ref, bv_ref, tc)
        m_ref[...] = (g * _sigmoid(g) * v).astype(bf16)

    outs = _grid_call(
        body,
        rider,
        name="convact_fwd",
        grid=(nj, S // ts),
        in_specs=[main, prev, wg, wv, bg, bv],
        out_specs=[half],
        out_shape=[jax.ShapeDtypeStruct((S, F), bf16)],
        scratch_shapes=[],
        args=(u, u, cw, cw, cb, cb),
    )
    return outs[0] if rider is None else outs


def _convact_bwd(u, dm, cw, cb, tc, rider=None):
    S, F2 = u.shape
    F = F2 // 2
    ts = _div(S, 512, SUBLANES)
    nj, main, prev, nxt, wg, wv, bg, bv, half = _conv_specs(S, F, tc, ts)
    ni = S // ts
    half_nxt = pl.BlockSpec((SUBLANES, tc), lambda j, i: (jnp.minimum((i + 1) * (ts // SUBLANES), S // SUBLANES - 1), j))

    def act_bwd(g, v, d):
        sg = _sigmoid(g)
        return jnp.concatenate([d * v * sg * (1.0 + g * (1.0 - sg)), d * g * sg], axis=1)

    def body(u_ref, h_ref, n_ref, dm_ref, dmn_ref, wg_ref, wv_ref, bg_ref, bv_ref, du_ref, dwg_ref, dwv_ref, dbg_ref, dbv_ref):
        i = pl.program_id(1)
        u0, u1, u2, g, v = _conv_pre(u_ref, h_ref, wg_ref, wv_ref, bg_ref, bv_ref, tc)
        duc = act_bwd(g, v, dm_ref[...])
        w = jnp.concatenate([wg_ref[...], wv_ref[...]], axis=1)
        b = jnp.concatenate([bg_ref[...], bv_ref[...]], axis=1)
        un = n_ref[...]
        tail = u0[ts - SUBLANES:]
        ucn = b + w[0:1] * _shift_down(un, tail, 2) + w[1:2] * _shift_down(un, tail, 1) + w[2:3] * un
        ducn = jnp.where(i == ni - 1, 0.0, act_bwd(ucn[:, :tc], ucn[:, tc:], dmn_ref[...]))
        du_ref[...] = (w[2:3] * duc + w[1:2] * _shift_up(duc, ducn, 1) + w[0:1] * _shift_up(duc, ducn, 2)).astype(bf16)
        db = jnp.sum(duc, axis=0, keepdims=True)
        dw = jnp.concatenate(
            [jnp.sum(duc * u2, axis=0, keepdims=True), jnp.sum(duc * u1, axis=0, keepdims=True), jnp.sum(duc * u0, axis=0, keepdims=True)],
            axis=0,
        )

        @pl.when(i == 0)
        def _():
            dwg_ref[...] = dw[:, :tc]
            dwv_ref[...] = dw[:, tc:]
            dbg_ref[...] = db[:, :tc]
            dbv_ref[...] = db[:, tc:]

        @pl.when(i > 0)
        def _():
            dwg_ref[...] += dw[:, :tc]
            dwv_ref[...] += dw[:, tc:]
            dbg_ref[...] += db[:, :tc]
            dbv_ref[...] += db[:, tc:]

    w_out = pl.BlockSpec((3, tc), lambda j, i: (0, j))
    b_out = pl.BlockSpec((1, tc), lambda j, i: (0, j))
    return _grid_call(
        body,
        rider,
        name="convact_bwd",
        grid=(nj, ni),
        in_specs=[main, prev, nxt, half, half_nxt, wg, wv, bg, bv],
        out_specs=[main, w_out, w_out, b_out, b_out],
        out_shape=[
            jax.ShapeDtypeStruct((S, F2), bf16),
            jax.ShapeDtypeStruct((3, F), f32),
            jax.ShapeDtypeStruct((3, F), f32),
            jax.ShapeDtypeStruct((1, F), f32),
            jax.ShapeDtypeStruct((1, F), f32),
        ],
        scratch_shapes=[],
        args=(u, u, u, dm, dm, cw, cw, cb, cb),
    )


def _hg_gates(qp, fp, lbl):
    lb = _sigmoid(lbl[0:1] - lbl[1:2])
    sig = _sigmoid(fp)
    sigm = _sigmoid(-fp)
    f = lb + (1.0 - lb) * sig
    k = (1.0 - lb) * sigm
    sq = _sigmoid(qp)
    qf = qp * sq
    row = lax.broadcasted_iota(jnp.int32, (CHUNK, CHUNK), 0)
    col = lax.broadcasted_iota(jnp.int32, (CHUNK, CHUNK), 1)
    b = _dot((row >= col).astype(f32), jnp.log(f), NN, HIGHEST)
    return lb, sig, sigm, f, k, sq, qf, b


def _hg_block(qf, k, b, I):
    r0, n = I * HG_SUB, (I + 1) * HG_SUB
    base = b[r0 - 1:r0] if I > 0 else jnp.zeros_like(b[0:1])
    eq = jnp.exp(b[r0:n] - base)
    ek = jnp.exp(base - b[0:n])
    qt16 = (qf[r0:n] * eq).astype(bf16)
    kt16 = (k[0:n] * ek).astype(bf16)
    row = lax.broadcasted_iota(jnp.int32, (HG_SUB, n), 0) + r0
    col = lax.broadcasted_iota(jnp.int32, (HG_SUB, n), 1)
    return r0, n, eq, ek, qt16, kt16, col <= row


def _per_head(v, hp, fn):
    return jnp.concatenate([fn(v[:, p * HEAD_DIM:(p + 1) * HEAD_DIM]) for p in range(hp)], axis=1)


def _head_mean(v, hp):
    return _per_head(v, hp, lambda t: jnp.broadcast_to(jnp.mean(t, axis=-1, keepdims=True), t.shape))


def _pad_rows(v, rows):
    return v if v.shape[0] == rows else jnp.concatenate([v, jnp.zeros((rows - v.shape[0], v.shape[1]), v.dtype)], axis=0)


def _hgrn_specs(S, W, hp, reverse):
    nc = S // CHUNK
    ngrp = W // (hp * HEAD_DIM)
    cw = hp * HEAD_DIM
    cidx = (lambda c: nc - 1 - c) if reverse else (lambda c: c)

    def proj(off):
        return pl.BlockSpec((CHUNK, cw), lambda h, c: (cidx(c), off * ngrp + h))

    act = pl.BlockSpec((CHUNK, cw), lambda h, c: (cidx(c), h))
    lbl = pl.BlockSpec((2, cw), lambda h, c: (0, h))
    gn = pl.BlockSpec((1, HEAD_DIM), lambda h, c: (0, 0))
    st = pl.BlockSpec((hp, 1, HEAD_DIM, HEAD_DIM), lambda h, c: (h, cidx(c), 0, 0))
    return nc, ngrp, proj, act, lbl, gn, st


def _grid_call(body, rider, *, name, grid, in_specs, out_specs, out_shape, scratch_shapes, args):
    params = pltpu.CompilerParams(dimension_semantics=("arbitrary",) * len(grid))
    if rider is None:
        return pl.pallas_call(body, name=name, grid=grid, in_specs=in_specs, out_specs=out_specs, out_shape=out_shape,
                              scratch_shapes=scratch_shapes, compiler_params=params)(*args)
    n_in, n_out, n_scr = len(in_specs), len(out_specs), len(scratch_shapes)
    ri, ro = len(rider.ins), len(rider.out_shapes)

    def wrapped(*refs):
        a, b = n_in, n_in + ri
        c, d = b + n_out, b + n_out + ro
        e = d + n_scr
        ids = [pl.program_id(t) for t in range(len(grid))]
        is_first = functools.reduce(jnp.logical_and, [i == 0 for i in ids])
        is_last = functools.reduce(jnp.logical_and, [i == g - 1 for i, g in zip(ids, grid)])

        @pl.when(is_first)
        def _():
            rider.first(refs[a:b], refs[c:d], refs[e:])

        body(*refs[:a], *refs[b:c], *refs[d:e])

        @pl.when(is_last)
        def _():
            rider.last(refs[a:b], refs[c:d], refs[e:])

    return pl.pallas_call(
        wrapped,
        name=name,
        grid=grid,
        in_specs=list(in_specs) + [_hbm()] * ri,
        out_specs=list(out_specs) + [_hbm()] * ro,
        out_shape=list(out_shape) + list(rider.out_shapes),
        input_output_aliases={n_in + i: n_out + o for i, o in rider.aliases.items()},
        scratch_shapes=list(scratch_shapes) + list(rider.sems),
        compiler_params=params,
    )(*args, *rider.ins)


def _hgrn_fwd(proj, lb_logits, hg_norm, W, rider=None):
    S = proj.shape[0]
    hp = min(HG_HEADS_PER_STEP, W // HEAD_DIM)
    nc, ngrp, pspec, act, lbl_spec, gn_spec, st_spec = _hgrn_specs(S, W, hp, False)
    nb = CHUNK // HG_SUB

    def body(q_ref, f_ref, i_ref, g_ref, lbl_ref, gn_ref, y_ref, o_ref, st_ref, state):
        c = pl.program_id(1)

        @pl.when(c == 0)
        def _():
            state[...] = jnp.zeros_like(state)

        _, _, _, _, k, _, qf, b = _hg_gates(q_ref[...], f_ref[...], lbl_ref[...])
        v16 = i_ref[...].astype(bf16)
        a16 = (qf * jnp.exp(b)).astype(bf16)
        bl = b[CHUNK - 1:CHUNK]
        kd16 = (k * jnp.exp(bl - b)).astype(bf16)
        ebl = jnp.exp(bl)
        blocks = [_hg_block(qf, k, b, I) for I in range(nb)]
        heads = [slice(p * HEAD_DIM, (p + 1) * HEAD_DIM) for p in range(hp)]
        st_prev = state[...]
        st_ref[:, 0] = st_prev
        inter = [_dot(a16[:, sl], st_prev[p].astype(bf16), NT) for p, sl in enumerate(heads)]
        scores = [[_dot(qt16[:, sl], kt16[:, sl], NT) for sl in heads] for _, _, _, _, qt16, kt16, _ in blocks]
        intra = [[_dot(jnp.where(blk[6], sc, 0.0).astype(bf16), v16[0:blk[1], sl], NN) for sc, sl in zip(scs, heads)]
                 for blk, scs in zip(blocks, scores)]
        st_new = [st_prev[p] * ebl[:, sl] + _dot(v16[:, sl], kd16[:, sl], TN) for p, sl in enumerate(heads)]
        state[...] = jnp.stack(st_new)
        o = jnp.concatenate([inter[p] + jnp.concatenate([rows[p] for rows in intra], axis=0) for p in range(hp)], axis=1)
        o_ref[...] = o
        r = lax.rsqrt(_head_mean(o * o, hp) + EPS)
        gp = g_ref[...]
        y_ref[...] = (o * r * jnp.tile(gn_ref[...], (1, hp)) * (gp * _sigmoid(gp))).astype(bf16)

    H = W // HEAD_DIM
    return _grid_call(
        body,
        rider,
        name="hgrn_fwd",
        grid=(ngrp, nc),
        in_specs=[pspec(0), pspec(1), pspec(2), pspec(3), lbl_spec, gn_spec],
        out_specs=[act, act, st_spec],
        out_shape=[
            jax.ShapeDtypeStruct((S, W), bf16),
            jax.ShapeDtypeStruct((S, W), f32),
            jax.ShapeDtypeStruct((H, nc, HEAD_DIM, HEAD_DIM), f32),
        ],
        scratch_shapes=[pltpu.VMEM((hp, HEAD_DIM, HEAD_DIM), f32)],
        args=(proj, proj, proj, proj, lb_logits, hg_norm),
    )


def _hgrn_bwd(proj, o_hg, st, dycat, lb_logits, hg_norm, W, rider=None):
    S = proj.shape[0]
    hp = min(HG_HEADS_PER_STEP, W // HEAD_DIM)
    assert hp * HEAD_DIM == W, "the four gradients share one output block: all heads in one grid step"
    nc, ngrp, pspec, act, lbl_spec, gn_spec, st_spec = _hgrn_specs(S, W, hp, True)
    nb = CHUNK // HG_SUB

    def body(q_ref, f_ref, i_ref, g_ref, o_ref, st_ref, dy_ref, lbl_ref, gn_ref,
             d4_ref, dlbl_ref, dgn_ref, dstate):
        cw_ = hp * HEAD_DIM
        dq_ref, df_ref, di_ref, dg_ref = (d4_ref.at[:, pl.ds(t * cw_, cw_)] for t in range(4))
        h = pl.program_id(0)
        c = pl.program_id(1)

        @pl.when(c == 0)
        def _():
            dstate[...] = jnp.zeros_like(dstate)
            dlbl_ref[...] = jnp.zeros_like(dlbl_ref)

        @pl.when((c == 0) & (h == 0))
        def _():
            dgn_ref[...] = jnp.zeros_like(dgn_ref)

        row = lax.broadcasted_iota(jnp.int32, (CHUNK, CHUNK), 0)
        col = lax.broadcasted_iota(jnp.int32, (CHUNK, CHUNK), 1)
        upper = (col >= row).astype(f32)
        last_row = lax.broadcasted_iota(jnp.int32, (CHUNK, hp * HEAD_DIM), 0) == CHUNK - 1

        qp, fp = q_ref[...], f_ref[...]
        lb, sig, sigm, f, k, sq, qf, b = _hg_gates(qp, fp, lbl_ref[...])
        v16 = i_ref[...].astype(bf16)
        gnv = jnp.tile(gn_ref[...], (1, hp))
        gp = g_ref[...]
        sgg = _sigmoid(gp)
        gate = gp * sgg
        o = o_ref[...]
        r = lax.rsqrt(_head_mean(o * o, hp) + EPS)
        on = o * r
        dy = dy_ref[...]
        dg_ref[...] = (dy * on * gnv * (sgg * (1.0 + gp * (1.0 - sgg)))).astype(bf16)
        dgn_wide = jnp.sum(dy * on * gate, axis=0, keepdims=True)
        dgn_ref[...] += functools.reduce(jnp.add, [dgn_wide[:, p * HEAD_DIM:(p + 1) * HEAD_DIM] for p in range(hp)])
        don = dy * gnv * gate
        do16 = (r * (don - on * _head_mean(don * on, hp))).astype(bf16)
        eb = jnp.exp(b)
        A = qf * eb
        a16 = A.astype(bf16)
        bl = b[CHUNK - 1:CHUNK]
        ebl = jnp.exp(bl)
        ekd = jnp.exp(bl - b)
        kd = k * ekd
        kd16 = kd.astype(bf16)
        blocks = [_hg_block(qf, k, b, I) for I in range(nb)]

        heads = [slice(p * HEAD_DIM, (p + 1) * HEAD_DIM) for p in range(hp)]
        st_prev = st_ref[:, 0]
        dst_new = dstate[...]
        st16, dst16 = st_prev.astype(bf16), dst_new.astype(bf16)
        dA_h = [_dot(do16[:, sl], st16[p], NN) for p, sl in enumerate(heads)]
        dkd_h = [_dot(v16[:, sl], dst16[p], NN) for p, sl in enumerate(heads)]
        dv_h = [_dot(kd16[:, sl], dst16[p], NT) for p, sl in enumerate(heads)]
        dstate[...] = jnp.stack([dst_new[p] * ebl[:, sl] + _dot(do16[:, sl], a16[:, sl], TN) for p, sl in enumerate(heads)])
        dbl_h = [jnp.sum(dst_new[p] * st_prev[p], axis=0, keepdims=True) for p in range(hp)]
        sc_h = [[jnp.where(mask, _dot(qt16[:, sl], kt16[:, sl], NT), 0.0).astype(bf16) for sl in heads]
                for _, _, _, _, qt16, kt16, mask in blocks]
        dsc_h = [[jnp.where(mask, _dot(do16[r0:n, sl], v16[0:n, sl], NT), 0.0).astype(bf16) for sl in heads]
                 for r0, n, _, _, _, _, mask in blocks]
        dvi_h = [[_dot(sc, do16[blk[0]:blk[1], sl], TN) for sc, sl in zip(scs, heads)] for blk, scs in zip(blocks, sc_h)]
        dqt_h = [[_dot(dsc, blk[5][:, sl], NN) for dsc, sl in zip(dscs, heads)] for blk, dscs in zip(blocks, dsc_h)]
        dkt_h = [[_dot(dsc, blk[4][:, sl], TN) for dsc, sl in zip(dscs, heads)] for blk, dscs in zip(blocks, dsc_h)]
        dv_h = [functools.reduce(jnp.add, [dv_h[p]] + [_pad_rows(rows[p], CHUNK) for rows in dvi_h]) for p in range(hp)]

        dA, dkd = jnp.concatenate(dA_h, axis=1), jnp.concatenate(dkd_h, axis=1)
        dbl = jnp.concatenate(dbl_h, axis=1) * ebl + jnp.sum(dkd * kd, axis=0, keepdims=True)
        db = dA * A - dkd * kd + jnp.where(last_row, dbl, 0.0)
        dk = dkd * ekd
        dq_rows, db_rows = [], []
        for I, (r0, n, eq, ek, qt16, kt16, _) in enumerate(blocks):
            dqt, dkt = jnp.concatenate(dqt_h[I], axis=1), jnp.concatenate(dkt_h[I], axis=1)
            dq_rows.append(dqt * eq)
            dk = dk + _pad_rows(dkt * ek, CHUNK)
            db_rows.append(dqt * qt16.astype(f32))
            db = db - _pad_rows(dkt * kt16.astype(f32), CHUNK)
        dqf = dA * eb + jnp.concatenate(dq_rows, axis=0)
        db = db + jnp.concatenate(db_rows, axis=0)
        dlogf = _dot(upper, db, NN, HIGHEST)
        dfg = dlogf / f
        df_ref[...] = ((1.0 - lb) * sig * sigm * (dfg - dk)).astype(bf16)
        dlb = jnp.sum(sigm * (dfg - dk), axis=0, keepdims=True)
        dl0 = dlb * lb * (1.0 - lb)
        dlbl_ref[...] += jnp.concatenate([dl0, -dl0], axis=0)
        dq_ref[...] = (dqf * (sq * (1.0 + qp * (1.0 - sq)))).astype(bf16)
        di_ref[...] = jnp.concatenate(dv_h, axis=1).astype(bf16)

    cw = hp * HEAD_DIM
    dy_spec = pl.BlockSpec((CHUNK, cw), lambda h, c: (nc - 1 - c, h))
    return _grid_call(
        body,
        rider,
        name="hgrn_bwd",
        grid=(ngrp, nc),
        in_specs=[pspec(0), pspec(1), pspec(2), pspec(3), act, st_spec, dy_spec, lbl_spec, gn_spec],
        out_specs=[pl.BlockSpec((CHUNK, 4 * W), lambda h, c: (nc - 1 - c, 0)), lbl_spec, gn_spec],
        out_shape=[jax.ShapeDtypeStruct((S, 4 * W), bf16), jax.ShapeDtypeStruct((2, W), f32), jax.ShapeDtypeStruct((1, HEAD_DIM), f32)],
        scratch_shapes=[pltpu.VMEM((hp, HEAD_DIM, HEAD_DIM), f32)],
        args=(proj, proj, proj, proj, o_hg, st, dycat, lb_logits, hg_norm),
    )


def _att_dims(S):
    qt = ATT_Q_ROWS if S % ATT_Q_ROWS == 0 else CHUNK
    pad = LEFT_CHUNKS * CHUNK
    kb = pad + qt
    return qt, pad, kb, kb + qt


def _kv_padded(proj, W, pad):
    S = proj.shape[0]
    ts = _div(pad, 512, 2 * SUBLANES)
    assert S % ts == 0 and pad % ts == 0
    npad = pad // ts

    def body(k_ref, v_ref, ko_ref, vo_ref):
        i = pl.program_id(0)

        @pl.when(i < npad)
        def _():
            ko_ref[...] = jnp.zeros_like(ko_ref)
            vo_ref[...] = jnp.zeros_like(vo_ref)

        @pl.when(i >= npad)
        def _():
            ko_ref[...] = k_ref[...].astype(bf16)
            vo_ref[...] = v_ref[...].astype(bf16)

    out = pl.BlockSpec((ts, W), lambda i: (i, 0))
    return pl.pallas_call(
        body,
        name="kv_padded",
        grid=((S + pad) // ts,),
        in_specs=[pl.BlockSpec((ts, W), lambda i: (jnp.maximum(i - npad, 0), 5)), pl.BlockSpec((ts, W), lambda i: (jnp.maximum(i - npad, 0), 6))],
        out_specs=[out, out],
        out_shape=[jax.ShapeDtypeStruct((S + pad, W), bf16)] * 2,
        compiler_params=pltpu.CompilerParams(dimension_semantics=("parallel",)),
    )(proj, proj)


def _att_probs(qk, bias, start, qt, pad, kb):
    s = qk * (HEAD_DIM ** -0.5) + bias
    row = lax.broadcasted_iota(jnp.int32, (qt, kb), 0)
    col = lax.broadcasted_iota(jnp.int32, (qt, kb), 1)
    lo = jnp.bitwise_and(row, -CHUNK)
    ok = (col >= lo) & (col < lo + pad + CHUNK) & (col + start >= pad)
    s = jnp.where(ok, s, NEG)
    e = jnp.exp(s - jnp.max(s, axis=-1, keepdims=True))
    return e / jnp.sum(e, axis=-1, keepdims=True)


def _att_bias(ext_row, qt, kb, ne):
    e = jnp.broadcast_to(ext_row, (qt, ne))
    return pltpu.roll(e, ne - qt + 1, 1, stride=1, stride_axis=0)[:, :kb]


def _att_specs(S, H, ah, qt, pad, ne):
    cw = ah * HEAD_DIM
    q_spec = pl.BlockSpec((qt, cw), lambda h, g: (g, 4 * H // ah + h))
    kv_spec = pl.BlockSpec((S + pad, cw), lambda h, g: (0, h))
    ext_spec = pl.BlockSpec((ah, 1, ne), lambda h, g: (h, 0, 0))
    row_spec = pl.BlockSpec((qt, cw), lambda h, g: (g, h))
    return cw, q_spec, kv_spec, ext_spec, row_spec


def _attn_fwd(proj, kp, vp, ext, W, rider=None):
    S = proj.shape[0]
    H = W // HEAD_DIM
    ah = min(ATT_HEADS_PER_STEP, H)
    qt, pad, kb, ne = _att_dims(S)
    cw, q_spec, kv_spec, ext_spec, row_spec = _att_specs(S, H, ah, qt, pad, ne)
    heads = [slice(a * HEAD_DIM, (a + 1) * HEAD_DIM) for a in range(ah)]

    def body(q_ref, k_ref, v_ref, ext_ref, o_ref, bias):
        g = pl.program_id(1)

        @pl.when(g == 0)
        def _():
            for a in range(ah):
                bias[a] = _att_bias(ext_ref[a], qt, kb, ne)

        start = pl.multiple_of(g * qt, qt)
        q16 = q_ref[...].astype(bf16)
        kb16 = k_ref[pl.ds(start, kb), :]
        vb16 = v_ref[pl.ds(start, kb), :]
        qk = [_dot(q16[:, sl], kb16[:, sl], NT) for sl in heads]
        pn = [_att_probs(qk[a], bias[a], start, qt, pad, kb).astype(bf16) for a in range(ah)]
        o_ref[...] = jnp.concatenate([_dot(pn[a], vb16[:, sl], NN) for a, sl in enumerate(heads)], axis=1).astype(bf16)

    outs = _grid_call(
        body,
        rider,
        name="attn_fwd",
        grid=(H // ah, S // qt),
        in_specs=[q_spec, kv_spec, kv_spec, ext_spec],
        out_specs=[row_spec],
        out_shape=[jax.ShapeDtypeStruct((S, W), bf16)],
        scratch_shapes=[pltpu.VMEM((ah, qt, kb), f32)],
        args=(proj, kp, vp, ext),
    )
    return outs[0] if rider is None else outs


def _attn_bwd(proj, kp, vp, ext, dycat, W, rider=None):
    S = proj.shape[0]
    H = W // HEAD_DIM
    ah = min(ATT_HEADS_PER_STEP, H)
    qt, pad, kb, ne = _att_dims(S)
    cw, q_spec, kv_spec, ext_spec, row_spec = _att_specs(S, H, ah, qt, pad, ne)
    heads = [slice(a * HEAD_DIM, (a + 1) * HEAD_DIM) for a in range(ah)]
    scale = HEAD_DIM ** -0.5
    ng = S // qt

    def body(q_ref, k_ref, v_ref, ext_ref, do_ref, dq_ref, dko_ref, dvo_ref, db_ref, bias, dk_ref, dv_ref):
        g = pl.program_id(1)

        @pl.when(g == 0)
        def _():
            for a in range(ah):
                bias[a] = _att_bias(ext_ref[a], qt, kb, ne)
            dk_ref[...] = jnp.zeros_like(dk_ref)
            dv_ref[...] = jnp.zeros_like(dv_ref)
            db_ref[...] = jnp.zeros_like(db_ref)

        start = pl.multiple_of(g * qt, qt)
        q16 = q_ref[...].astype(bf16)
        kb16 = k_ref[pl.ds(start, kb), :]
        vb16 = v_ref[pl.ds(start, kb), :]
        do16 = do_ref[...].astype(bf16)
        qk = [_dot(q16[:, sl], kb16[:, sl], NT) for sl in heads]
        dpn = [_dot(do16[:, sl], vb16[:, sl], NT) for sl in heads]
        pn = [_att_probs(qk[a], bias[a], start, qt, pad, kb) for a in range(ah)]
        ds = [pn[a] * (dpn[a] - jnp.sum(dpn[a] * pn[a], axis=-1, keepdims=True)) for a in range(ah)]
        ds16 = [d.astype(bf16) for d in ds]
        dv_ref[pl.ds(start, kb), :] += jnp.concatenate([_dot(pn[a].astype(bf16), do16[:, sl], TN) for a, sl in enumerate(heads)], axis=1)
        dq_ref[...] = (jnp.concatenate([_dot(ds16[a], kb16[:, sl], NN) for a, sl in enumerate(heads)], axis=1) * scale).astype(bf16)
        dk_ref[pl.ds(start, kb), :] += jnp.concatenate([_dot(ds16[a], q16[:, sl], TN) for a, sl in enumerate(heads)], axis=1) * scale
        for a in range(ah):
            db_ref[a] += ds[a]

        @pl.when(g == ng - 1)
        def _():
            dko_ref[...] = dk_ref[pl.ds(pad, S), :].astype(bf16)
            dvo_ref[...] = dv_ref[pl.ds(pad, S), :].astype(bf16)

    kv_out = pl.BlockSpec((S, cw), lambda h, g: (0, h))
    return _grid_call(
        body,
        rider,
        name="attn_bwd",
        grid=(H // ah, S // qt),
        in_specs=[q_spec, kv_spec, kv_spec, ext_spec,
                  pl.BlockSpec((qt, cw), lambda h, g: (g, H // ah + h))],
        out_specs=[row_spec, kv_out, kv_out, pl.BlockSpec((ah, qt, kb), lambda h, g: (h, 0, 0))],
        out_shape=[
            jax.ShapeDtypeStruct((S, W), bf16),
            jax.ShapeDtypeStruct((S, W), bf16),
            jax.ShapeDtypeStruct((S, W), bf16),
            jax.ShapeDtypeStruct((H, qt, kb), f32),
        ],
        scratch_shapes=[pltpu.VMEM((ah, qt, kb), f32), pltpu.VMEM((S + pad, cw), f32), pltpu.VMEM((S + pad, cw), f32)],
        args=(proj, kp, vp, ext, dycat),
    )


def _bias_onehot(qt, ne, nrel_pad):
    m = lax.broadcasted_iota(jnp.int32, (nrel_pad, ne), 1)
    r = lax.broadcasted_iota(jnp.int32, (nrel_pad, ne), 0)
    rel = LEFT_CHUNKS * CHUNK + qt - 1 - m
    hot = (r == jnp.clip(rel, -REL_CLIP, REL_CLIP) + REL_CLIP) & (m < ne - 1)
    return hot.astype(f32)


def _bias_ext(rel_bias_pad, onehot):
    H, _, nr = rel_bias_pad.shape
    ne = onehot.shape[1]

    def body(rb_ref, oh_ref, o_ref):
        o_ref[0] = _dot(rb_ref[0], oh_ref[...], NN, HIGHEST)

    return pl.pallas_call(
        body,
        name="bias_ext",
        grid=(H,),
        in_specs=[pl.BlockSpec((1, 1, nr), lambda h: (h, 0, 0)), pl.BlockSpec((nr, ne), lambda h: (0, 0))],
        out_specs=pl.BlockSpec((1, 1, ne), lambda h: (h, 0, 0)),
        out_shape=jax.ShapeDtypeStruct((H, 1, ne), f32),
        compiler_params=pltpu.CompilerParams(dimension_semantics=("parallel",)),
    )(rel_bias_pad, onehot)


def _bias_bwd(dbias, onehot):
    H, qt, kb = dbias.shape
    nr, ne = onehot.shape

    def body(d_ref, oh_ref, o_ref):
        row = lax.broadcasted_iota(jnp.int32, (qt, qt), 0)
        col = lax.broadcasted_iota(jnp.int32, (qt, qt), 1)
        flip = (row + col == qt - 1).astype(f32)
        x = _dot(flip, d_ref[0], NN, HIGHEST)
        x = jnp.concatenate([x, jnp.zeros((qt, ne - kb), f32)], axis=1)
        de = jnp.sum(pltpu.roll(x, 0, 1, stride=1, stride_axis=0), axis=0, keepdims=True)
        o_ref[0] = _dot(de, oh_ref[...], NT, HIGHEST)

    return pl.pallas_call(
        body,
        name="bias_bwd",
        grid=(H,),
        in_specs=[pl.BlockSpec((1, qt, kb), lambda h: (h, 0, 0)), pl.BlockSpec((nr, ne), lambda h: (0, 0))],
        out_specs=pl.BlockSpec((1, 1, nr), lambda h: (h, 0, 0)),
        out_shape=jax.ShapeDtypeStruct((H, 1, nr), f32),
        compiler_params=pltpu.CompilerParams(dimension_semantics=("parallel",)),
    )(dbias, onehot)


def _place():
    x, y, c = lax.axis_index("x"), lax.axis_index("y"), lax.axis_index("c")
    return x, y, c


def _flip(x, y, k):
    return (1 - x if k & 2 else x), (1 - y if k & 1 else y)


def _region(ref, shard_shape, axis, j, half):
    R, C = shard_shape
    if axis == 0:
        if half is None:
            return ref.at[pl.ds(j * R, R), :]
        return ref.at[pl.ds(j * R + half * (R // 2), R // 2), :]
    if half is None:
        return ref.at[:, pl.ds(j * C, C)]
    return ref.at[pl.ds(half * (R // 2), R // 2), pl.ds(j * C, C)]


def _remote(src, dst, send_sem, recv_sem, dev):
    return pltpu.make_async_remote_copy(src_ref=src, dst_ref=dst, send_sem=send_sem, recv_sem=recv_sem,
                                        device_id=dev, device_id_type=MESH)


def _cast_place(shard, axis, where, *, name):
    R, C = shard.shape
    tr = _div(R, 256, 2 * SUBLANES)
    nr = R // tr
    full = (N_CHIPS * R, C) if axis == 0 else (R, N_CHIPS * C)
    omap = (lambda r, s: (s[0] * nr + r, 0)) if axis == 0 else (lambda r, s: (r, s[0]))

    def body(s_ref, x_ref, o_ref):
        o_ref[...] = x_ref[...].astype(bf16)

    return pl.pallas_call(
        body,
        name=name,
        grid_spec=pltpu.PrefetchScalarGridSpec(
            num_scalar_prefetch=1,
            grid=(nr,),
            in_specs=[pl.BlockSpec((tr, C), lambda r, s: (r, 0))],
            out_specs=pl.BlockSpec((tr, C), omap),
        ),
        out_shape=jax.ShapeDtypeStruct(full, bf16),
        compiler_params=pltpu.CompilerParams(dimension_semantics=("parallel",)),
    )(where, shard)


class _Rider(NamedTuple):
    ins: list
    out_shapes: list
    aliases: dict
    sems: list
    first: Callable
    last: Callable


def _run_rider(rider, *, name):
    ni, no = len(rider.ins), len(rider.out_shapes)

    def body(*refs):
        ins, outs, sems = refs[:ni], refs[ni:ni + no], refs[ni + no:]
        rider.first(ins, outs, sems)
        rider.last(ins, outs, sems)

    return pl.pallas_call(
        body,
        name=name,
        in_specs=[_hbm()] * ni,
        out_specs=[_hbm()] * no,
        out_shape=rider.out_shapes,
        input_output_aliases=rider.aliases,
        scratch_shapes=rider.sems,
    )(*rider.ins)


def _start_all(copies):
    def first(ins, outs, sems):
        for cp in copies(ins, outs, sems):
            cp.start()

    def last(ins, outs, sems):
        cps = copies(ins, outs, sems)
        for cp in cps:
            cp.wait_recv()
        for cp in cps:
            cp.wait_send()

    return first, last


def _mm_while_gathering(a, full, shard_shape, where, *, tm, tn, name):
    S, K = a.shape
    R, Cs = shard_shape
    N = full.shape[1]
    tps, nj, ni = Cs // tn, N // tn, S // tm
    assert R == K and Cs % tn == 0 and S % tm == 0

    def block(j, chip):
        return jnp.bitwise_xor(chip, j // tps) * tps + j % tps

    def body(s_ref, a_ref, w_in, o_ref, w_ref, buf, fsem, send1, recv1, send2, recv2):
        j, i = pl.program_id(0), pl.program_id(1)
        x, y, c = _place()
        chip = 2 * x + y

        def ici(k):
            px, py = _flip(x, y, k)
            mine = _region(w_ref, shard_shape, 1, chip, c)
            return _remote(mine, mine, send1.at[k - 1], recv1.at[k - 1], (px, py, c))

        def handed(k, half):
            px, py = _flip(x, y, k)
            land = _region(w_ref, shard_shape, 1, 2 * px + py, half)
            return land, _remote(land, land, send2.at[k - 1], recv2.at[k - 1], (x, y, 1 - c))

        def tile(jj):
            col = pl.multiple_of(block(jj, chip) * tn, LANES)
            return pltpu.make_async_copy(w_ref.at[:, pl.ds(col, tn)], buf.at[jj % 2], fsem.at[jj % 2])

        @pl.when((j == 0) & (i == 0))
        def _():
            for k in (1, 2, 3):
                ici(k).start()
            tile(j).start()

        @pl.when(i == 0)
        def _():
            tile(j).wait()

        @pl.when((i == ni - 1) & (j + 1 < nj))
        def _():
            for k in (1, 2, 3):
                @pl.when(j + 1 == k * tps)
                def _():
                    land, forward = handed(k, c)
                    _remote(land, land, send1.at[k - 1], recv1.at[k - 1], (x, y, c)).wait_recv()
                    forward.start()
                    handed(k, 1 - c)[1].wait_recv()

            tile(j + 1).start()

        o_ref[...] = _dot(a_ref[...], buf[j % 2], NN)

        @pl.when((j == nj - 1) & (i == ni - 1))
        def _():
            for k in (1, 2, 3):
                ici(k).wait_send()
                handed(k, c)[1].wait_send()

    sems = pltpu.SemaphoreType.DMA((3,))
    spec = pltpu.PrefetchScalarGridSpec(
        num_scalar_prefetch=1,
        grid=(nj, ni),
        in_specs=[pl.BlockSpec((tm, K), lambda j, i, s: (i, 0)), _hbm()],
        out_specs=[pl.BlockSpec((tm, tn), lambda j, i, s: (i, block(j, s[0]))), _hbm()],
        scratch_shapes=[pltpu.VMEM((2, K, tn), full.dtype), pltpu.SemaphoreType.DMA((2,)), sems, sems, sems, sems],
    )
    return pl.pallas_call(
        body,
        name=name,
        grid_spec=spec,
        out_shape=[jax.ShapeDtypeStruct((S, N), f32), jax.ShapeDtypeStruct(full.shape, full.dtype)],
        input_output_aliases={2: 1},
        compiler_params=pltpu.CompilerParams(dimension_semantics=("arbitrary", "arbitrary")),
    )(where, a, full)


class _GatherJob(NamedTuple):
    buf: int
    shard_shape: tuple
    axis: int
    split: bool
    rels: tuple
    hand_over: bool


def _gather_rider(bufs, jobs):
    def copies(ins, outs, sems):
        x, y, c = _place()
        cps = []
        for n, job in enumerate(jobs):
            half = c if job.split else None
            for k in job.rels:
                px, py = _flip(x, y, k)
                chip, dev = (2 * px + py, (x, y, 1 - c)) if job.hand_over else (2 * x + y, (px, py, c))
                cps.append(_remote(_region(ins[job.buf], job.shard_shape, job.axis, chip, half),
                                   _region(outs[job.buf], job.shard_shape, job.axis, chip, half),
                                   sems[0].at[3 * n + k - 1], sems[1].at[3 * n + k - 1], dev))
        return cps

    assert all(job.split or not job.hand_over for job in jobs)
    sems = pltpu.SemaphoreType.DMA((3 * len(jobs),))
    return _Rider(list(bufs), [jax.ShapeDtypeStruct(b.shape, b.dtype) for b in bufs], {m: m for m in range(len(bufs))},
                  [sems, sems], *_start_all(copies))


def _chips_rider(ts, rels=(1, 2, 3), landing=None):
    n = len(ts)

    def copies(ins, outs, sems):
        x, y, c = _place()
        cps = []
        for m in range(n):
            for k in rels:
                px, py = _flip(x, y, k)
                cps.append(_remote(ins[m].at[2 * px + py], outs[m].at[k - 1], sems[0].at[3 * m + k - 1], sems[1].at[3 * m + k - 1], (px, py, c)))
        return cps

    sems = pltpu.SemaphoreType.DMA((3 * n,))
    return _Rider(list(ts) + list(landing or []), [jax.ShapeDtypeStruct((3,) + t.shape[1:], t.dtype) for t in ts],
                  {n + m: m for m in range(n)} if landing else {}, [sems, sems], *_start_all(copies))


def _sibling_rider(dws, shard_shapes, axes):
    n = len(dws)

    def copies(ins, outs, sems):
        x, y, c = _place()
        return [_remote(_region(ins[m], shard_shapes[m], axes[m], j, 1 - c), outs[m].at[j],
                        sems[0].at[N_CHIPS * m + j], sems[1].at[N_CHIPS * m + j], (x, y, 1 - c))
                for m in range(n) for j in range(N_CHIPS)]

    sems = pltpu.SemaphoreType.DMA((N_CHIPS * n,))
    return _Rider(list(dws), [jax.ShapeDtypeStruct((N_CHIPS, s[0] // 2, s[1]), d.dtype) for s, d in zip(shard_shapes, dws)], {},
                  [sems, sems], *_start_all(copies))


def _join_rider(gs):
    n = len(gs)

    def copies(ins, outs, sems):
        x, y, c = _place()
        cps = []
        for m in range(n):
            hr = gs[m].shape[0] // 2
            cps.append(_remote(ins[m].at[pl.ds(c * hr, hr), :], outs[m].at[pl.ds(c * hr, hr), :], sems[0].at[m], sems[1].at[m], (x, y, 1 - c)))
        return cps

    sems = pltpu.SemaphoreType.DMA((n,))
    return _Rider(list(gs), [jax.ShapeDtypeStruct(g.shape, g.dtype) for g in gs], {m: m for m in range(n)}, [sems, sems], *_start_all(copies))


def _both(r1, r2):
    i1, o1, s1 = len(r1.ins), len(r1.out_shapes), len(r1.sems)
    aliases = dict(r1.aliases)
    aliases.update({i1 + i: o1 + o for i, o in r2.aliases.items()})

    def first(ins, outs, sems):
        r1.first(ins[:i1], outs[:o1], sems[:s1])
        r2.first(ins[i1:], outs[o1:], sems[s1:])

    def last(ins, outs, sems):
        r1.last(ins[:i1], outs[:o1], sems[:s1])
        r2.last(ins[i1:], outs[o1:], sems[s1:])

    return _Rider(r1.ins + r2.ins, r1.out_shapes + r2.out_shapes, aliases, r1.sems + r2.sems, first, last)


def _pair_sum(dw, got, shard_shape, axis, where, *, name):
    R, C = shard_shape
    hr = R // 2
    tr = _div(hr, 256, 2 * SUBLANES)
    nr = hr // tr
    dmap = (lambda j, r, s: ((2 * j + s[1]) * nr + r, 0)) if axis == 0 else (lambda j, r, s: (s[1] * nr + r, j))
    slot = pl.BlockSpec((1, tr, C), lambda j, r, s: (j, r, 0))

    def body(s_ref, d_ref, g_ref, o_ref):
        o_ref[0] = (d_ref[...].astype(f32) + g_ref[0].astype(f32)).astype(bf16)

    return pl.pallas_call(
        body,
        name=name,
        grid_spec=pltpu.PrefetchScalarGridSpec(
            num_scalar_prefetch=1,
            grid=(N_CHIPS, nr),
            in_specs=[pl.BlockSpec((tr, C), dmap), slot],
            out_specs=slot,
        ),
        out_shape=jax.ShapeDtypeStruct((N_CHIPS, hr, C), bf16),
        compiler_params=pltpu.CompilerParams(dimension_semantics=("parallel", "parallel")),
    )(where, dw, got)


def _chip_sum(pair, others, where, *, name):
    _, hr, C = pair.shape
    tr = _div(hr, 256, 2 * SUBLANES)
    nr = hr // tr

    def body(s_ref, p_ref, o_ref, g_ref):
        g_ref[...] = p_ref[0].astype(f32) + o_ref[0].astype(f32) + o_ref[1].astype(f32) + o_ref[2].astype(f32)

    return pl.pallas_call(
        body,
        name=name,
        grid_spec=pltpu.PrefetchScalarGridSpec(
            num_scalar_prefetch=1,
            grid=(nr,),
            in_specs=[pl.BlockSpec((1, tr, C), lambda r, s: (s[0], r, 0)), pl.BlockSpec((3, tr, C), lambda r, s: (0, r, 0))],
            out_specs=pl.BlockSpec((tr, C), lambda r, s: (s[1] * nr + r, 0)),
        ),
        out_shape=jax.ShapeDtypeStruct((2 * hr, C), f32),
        compiler_params=pltpu.CompilerParams(dimension_semantics=("parallel",)),
    )(where, pair, others)


def _allreduce_small(v):
    rows = v.shape[0]

    def body(v_ref, o_ref, gath, send, recv):
        x, y, c = _place()
        me = 4 * x + 2 * y + c
        gath[pl.ds(me, 1)] = v_ref[...][None]
        cps = []
        for k in range(1, N_DEV):
            peer = (1 - x if k & 4 else x, 1 - y if k & 2 else y, 1 - c if k & 1 else c)
            cp = _remote(v_ref, gath.at[me], send.at[k - 1], recv.at[k - 1], peer)
            cp.start()
            cps.append(cp)
        for cp in cps:
            cp.wait_recv()
        for cp in cps:
            cp.wait_send()
        acc = gath[0]
        for d in range(1, N_DEV):
            acc = acc + gath[d]
        o_ref[...] = acc

    return pl.pallas_call(
        body,
        name="small_allreduce",
        in_specs=[_vmem()],
        out_specs=_vmem(),
        out_shape=jax.ShapeDtypeStruct(v.shape, f32),
        scratch_shapes=[pltpu.VMEM((N_DEV, rows, LANES), f32), pltpu.SemaphoreType.DMA((N_DEV - 1,)), pltpu.SemaphoreType.DMA((N_DEV - 1,))],
    )(v)


def _adamw_math(w, g, m, v):
    m = ADAM_B1 * m + (1.0 - ADAM_B1) * g
    v = ADAM_B2 * v + (1.0 - ADAM_B2) * (g * g)
    m_hat = m / (1.0 - ADAM_B1 ** ADAM_STEP)
    v_hat = v / (1.0 - ADAM_B2 ** ADAM_STEP)
    delta = -ADAM_LR * (m_hat / (jnp.sqrt(v_hat) + ADAM_EPS) + ADAM_WD * w)
    return delta, m, v


def _adamw(w, g, m, v, *, name):
    R, C = w.shape
    tr = _div(R, 128, SUBLANES)

    def body(w_ref, g_ref, m_ref, v_ref, d_ref, mo_ref, vo_ref):
        d, mn, vn = _adamw_math(w_ref[...], g_ref[...], m_ref[...], v_ref[...])
        d_ref[...] = d
        mo_ref[...] = mn
        vo_ref[...] = vn

    spec = pl.BlockSpec((tr, C), lambda r: (r, 0))
    return pl.pallas_call(
        body,
        name=name,
        grid=(R // tr,),
        in_specs=[spec] * 4,
        out_specs=[spec] * 3,
        out_shape=[jax.ShapeDtypeStruct((R, C), f32)] * 3,
        compiler_params=pltpu.CompilerParams(dimension_semantics=("parallel",)),
    )(w, g, m, v)


def _pack(arrs):
    flat = jnp.concatenate([a.reshape(-1).astype(f32) for a in arrs])
    tile = SUBLANES * LANES
    total = -(-flat.shape[0] // tile) * tile
    return jnp.pad(flat, (0, total - flat.shape[0])).reshape(total // LANES, LANES)


def _unpack(buf, shapes):
    flat = buf.reshape(-1)
    out, off = [], 0
    for s in shapes:
        size = int(np.prod(s))
        out.append(flat[off:off + size].reshape(s))
        off += size
    return out


def kernel(x, p, norm_mix, w_in, lb_logits, hg_norm, rel_bias, w_out, norm_ffn, w_up, conv_w, conv_b, w_down, norm_ple, w_ple_gate, w_ple_proj, final_norm, loss_target, m_norm_mix, m_w_in, m_lb_logits, m_hg_norm, m_rel_bias, m_w_out, m_norm_ffn, m_w_up, m_conv_w, m_conv_b, m_w_down, m_norm_ple, m_w_ple_gate, m_w_ple_proj, m_final_norm, v_norm_mix, v_w_in, v_lb_logits, v_hg_norm, v_rel_bias, v_w_out, v_norm_ffn, v_w_up, v_conv_w, v_conv_b, v_w_down, v_norm_ple, v_w_ple_gate, v_w_ple_proj, v_final_norm):
    S, D = x.shape[1], x.shape[2]
    xv, pv, tgt = x[0], p[0, 0], loss_target[0]
    W = (w_in.shape[2] * N_CHIPS) // 7
    H = W // HEAD_DIM
    F = w_down.shape[1] * N_CHIPS
    tc = _div(F // 2, 1408, LANES)
    nj = F // tc
    jx, jy = lax.axis_index("x"), lax.axis_index("y")
    chip = 2 * jx + jy

    big = [w_in[0], w_out[0], w_up[0], w_down[0], w_ple_gate[0], w_ple_proj[0]]
    big_axes = [1, 0, 1, 0, 0, 1]
    where = jnp.stack([chip, lax.axis_index("c")]).astype(jnp.int32)
    cw_pad = jnp.pad(conv_w[0], ((0, SUBLANES - conv_w.shape[1]), (0, 0)))
    cw_mine = lax.dynamic_update_slice(jnp.zeros((N_CHIPS * SUBLANES, cw_pad.shape[1]), f32), cw_pad, (chip * SUBLANES, 0))
    placed = [_cast_place(b, ax, where, name=f"cast_place_{i}") for i, (b, ax) in enumerate(zip(big, big_axes))]
    shard_shapes = [b.shape for b in big]

    every = (1, 2, 3)

    def job(buf, i, rels, hand_over=False):
        if i is None:
            return _GatherJob(buf, cw_pad.shape, 0, False, rels, hand_over)
        return _GatherJob(buf, shard_shapes[i], big_axes[i], True, rels, hand_over)

    tm = _div(S, 1024, LANES)

    a1 = _rms_fwd(xv, norm_mix, name="rms_mix")
    cs_in = shard_shapes[0][1]
    tn_in = _div(cs_in, 1024, LANES)
    proj, W_in = _mm_while_gathering(a1, placed[0], shard_shapes[0], where, tm=tm, tn=tn_in, name="mm_in")
    y_hg, o_hg, st, W_out_i, W_up_i = _hgrn_fwd(
        proj, lb_logits, hg_norm, W, rider=_gather_rider([placed[1], placed[2]], [job(0, 1, every), job(1, 2, (1, 2))]))

    qt, pad, kb, ne = _att_dims(S)
    nrel = rel_bias.shape[2]
    nrel_pad = -(-nrel // LANES) * LANES
    onehot = _bias_onehot(qt, ne, nrel_pad)
    rb_pad = jnp.pad(rel_bias[0], ((0, 0), (0, nrel_pad - nrel)))[:, None, :]
    ext = _bias_ext(rb_pad, onehot)
    kp, vp = _kv_padded(proj, W, pad)
    y_att, W_out, W_up_i = _attn_fwd(
        proj, kp, vp, ext, W,
        rider=_gather_rider([W_out_i, W_up_i], [job(0, 1, every, True), job(1, 2, (1, 2), True), job(1, 2, (3,))]))
    ycat = jnp.concatenate([y_hg, y_att], axis=1)

    tn_d = _div(D, 512, LANES)
    h1, W_up = _mm(ycat, W_out, dims="nn", tm=tm, tn=tn_d, tk=D, out_dtype=f32, res=xv, name="mm_out",
                   rider=_gather_rider([W_up_i], [job(0, 2, (3,), True)]))
    a2 = _rms_fwd(h1, norm_ffn, name="rms_ffn")
    perm = lambda b: (b % 2) * nj + b // 2
    u, W_down_i, W_pg_i, W_pp_i, cw_all = _mm(
        a2, W_up, dims="nn", tm=tm, tn=tc, tk=D, out_dtype=f32, name="mm_up", b_idx=lambda i, j, k: (k, perm(j)),
        rider=_gather_rider([placed[3], placed[4], placed[5], cw_mine], [job(0, 3, every), job(1, 4, every), job(2, 5, every), job(3, None, every)]))
    ncw = conv_w.shape[1]
    cw_full = jnp.transpose(cw_all.reshape(N_CHIPS, SUBLANES, -1)[:, :ncw], (1, 0, 2)).reshape(ncw, -1)
    cb = conv_b
    mact, W_down, W_pg, W_pp = _convact_fwd(
        u, cw_full, cb, tc, rider=_gather_rider([W_down_i, W_pg_i, W_pp_i], [job(0, 3, every, True), job(1, 4, every, True), job(2, 5, every, True)]))
    tn_w = _div(D, 1024, LANES)
    h2 = _mm(mact, W_down, dims="nn", tm=_div(S, 512, LANES), tn=tn_w, tk=F, out_dtype=f32, res=h1, name="mm_down", cols_outer=True)
    a3 = _rms_fwd(h2, norm_ple, name="rms_ple")
    zg = _mm(a3, W_pg, dims="nn", tm=tm, tn=tn_d, tk=D, out_dtype=f32, name="mm_ple_gate")
    p16 = pv.astype(bf16)
    pp = _mm(p16, W_pp, dims="nn", tm=tm, tn=tn_d, tk=pv.shape[1], out_dtype=f32, name="mm_ple_proj")
    dh3, dzg, dpp, loss_part, d_fn = _tail(h2, zg, pp, tgt, final_norm.reshape(1, D))

    pair = {}

    def siblings(idx, grads):
        return _sibling_rider(grads, [shard_shapes[i] for i in idx], [big_axes[i] for i in idx])

    def pair_up(idx, grads, got):
        for i, d, g in zip(idx, grads, got):
            pair[i] = _pair_sum(d, g, shard_shapes[i], big_axes[i], where, name=f"grad_pair_sum_{i}")

    def reduced_half(i, landed):
        return _chip_sum(pair[i], landed, where, name=f"grad_chip_sum_{i}")

    tk_s = S
    dW_pp = _mm(p16, dpp, dims="tn", tm=pv.shape[1], tn=tn_w, tk=tk_s, out_dtype=bf16, name="mm_d_ple_proj")
    dW_pg = _mm(a3, dzg, dims="tn", tm=tn_w, tn=tn_w, tk=tk_s, out_dtype=bf16, name="mm_d_ple_gate")
    pair_up([4, 5], [dW_pg, dW_pp], _run_rider(siblings([4, 5], [dW_pg, dW_pp]), name="grads_to_sibling_4_5"))
    da3 = _mm(dzg, W_pg, dims="nt", tm=tm, tn=_div(D, 1024, LANES), tk=D, out_dtype=f32, name="mm_da3")
    dh2, dh2b, d_nple = _rms_bwd(h2, norm_ple, da3, dh3, name="rms_ple_bwd")
    dm, land4, land5 = _mm(dh2b, W_down, dims="nt", tm=tm, tn=_div(F, 512, LANES), tk=D, out_dtype=f32, name="mm_dm",
                           rider=_chips_rider([pair[4], pair[5]]))
    dW_down = _mm(mact, dh2b, dims="tn", tm=_div(F, 1408, LANES), tn=tn_d, tk=tk_s, out_dtype=bf16, name="mm_d_down")
    du, dcw_g, dcw_v, dcb_g, dcb_v, got3 = _convact_bwd(u, dm, cw_full, cb, tc, rider=siblings([3], [dW_down]))
    pair_up([3], [dW_down], [got3])
    dW_up, land3 = _mm(a2, du, dims="tn", tm=_div(D, 1024, LANES), tn=tc, tk=tk_s, out_dtype=bf16, name="mm_d_up",
                       o_idx=lambda i, j, k: (i, perm(j)), rider=_chips_rider([pair[3]]))
    da2, got2 = _mm(du, W_up, dims="nt", tm=tm, tn=D, tk=tc, out_dtype=f32, name="mm_da2",
                    b_idx=lambda i, j, k: (j, perm(k)), rider=siblings([2], [dW_up]))
    pair_up([2], [dW_up], [got2])
    dh1, dh1b, d_nffn = _rms_bwd(h1, norm_ffn, da2, dh2, name="rms_ffn_bwd")
    dycat = _mm(dh1b, W_out, dims="nt", tm=tm, tn=_div(D, 1024, LANES), tk=D, out_dtype=f32, name="mm_dycat")
    dW_out = _mm(ycat, dh1b, dims="tn", tm=tn_w, tn=tn_w, tk=tk_s, out_dtype=bf16, name="mm_d_out")
    pair_up([1], [dW_out], _run_rider(siblings([1], [dW_out]), name="grads_to_sibling_1"))
    dq_att, dk_att, dv_att, dbias, land2 = _attn_bwd(proj, kp, vp, ext, dycat, W, rider=_chips_rider([pair[2]], rels=(1, 2)))
    d_hg4, d_lbl, d_hgn, land2 = _hgrn_bwd(proj, o_hg, st, dycat, lb_logits, hg_norm, W,
                                           rider=_chips_rider([pair[2]], rels=(3,), landing=[land2]))
    d_rb = _bias_bwd(dbias, onehot)[:, 0, :nrel]
    dproj = jnp.concatenate([d_hg4, dq_att, dk_att, dv_att], axis=1)
    dW_in, land1 = _mm(a1, dproj, dims="tn", tm=tn_w, tn=_div(7 * W, 1024, LANES), tk=tk_s, out_dtype=bf16, name="mm_d_in",
                       rider=_chips_rider([pair[1]]))
    pair_up([0], [dW_in], _run_rider(siblings([0], [dW_in]), name="grads_to_sibling_0"))
    halves = [reduced_half(i, land) for i, land in ((1, land1), (2, land2), (3, land3), (4, land4), (5, land5))]
    da1, land0, *g_rest = _mm(dproj, W_in, dims="nt", tm=tm, tn=D, tk=_div(7 * W, 1792, LANES), out_dtype=f32, name="mm_da1",
                              rider=_both(_chips_rider([pair[0]]), _join_rider(halves)))
    grad_x, _, d_nmix = _rms_bwd(xv, norm_mix, da1, dh1, name="rms_mix_bwd")
    g_big = list(_run_rider(_join_rider([reduced_half(0, land0)]), name="grads_join_w_in")) + list(g_rest)

    d_cw = jnp.concatenate([dcw_g, dcw_v], axis=1)
    d_cb = jnp.concatenate([dcb_g, dcb_v], axis=1)
    small_parts = [loss_part[:, :1], d_nmix, d_lbl, d_hgn, d_rb, d_nffn, d_cw, d_cb, d_nple, d_fn]
    small_shapes = [(1, 1), (1, D), lb_logits.shape, hg_norm.shape, (H, nrel), (1, D), (ncw, 2 * F), (1, 2 * F), (1, D), (1, D)]
    red = _unpack(_allreduce_small(_pack(small_parts)), small_shapes)
    loss = red[0].reshape(())
    g_nmix, g_lbl, g_hgn, g_rb, g_nffn, g_cw_all, g_cb, g_nple, g_fn = red[1:]
    csh = conv_w.shape[2]
    g_cw = lax.dynamic_slice(g_cw_all, (0, chip * csh), (ncw, csh))
    small_g = [g_nmix, g_lbl, g_hgn, g_rb[None], g_nffn, g_cw[None], g_cb, g_nple, g_fn.reshape(D)]
    small_w = [norm_mix, lb_logits, hg_norm, rel_bias, norm_ffn, conv_w, conv_b, norm_ple, final_norm]
    small_m = [m_norm_mix, m_lb_logits, m_hg_norm, m_rel_bias, m_norm_ffn, m_conv_w, m_conv_b, m_norm_ple, m_final_norm]
    small_v = [v_norm_mix, v_lb_logits, v_hg_norm, v_rel_bias, v_norm_ffn, v_conv_w, v_conv_b, v_norm_ple, v_final_norm]
    shapes_s = [w.shape for w in small_w]
    sd, sm, sv = _adamw(_pack(small_w), _pack(small_g), _pack(small_m), _pack(small_v), name="adamw_small")
    small_g = [g.reshape(s) for g, s in zip(small_g, shapes_s)]
    small_d, small_nm, small_nv = _unpack(sd, shapes_s), _unpack(sm, shapes_s), _unpack(sv, shapes_s)

    big_m = [m_w_in[0], m_w_out[0], m_w_up[0], m_w_down[0], m_w_ple_gate[0], m_w_ple_proj[0]]
    big_v = [v_w_in[0], v_w_out[0], v_w_up[0], v_w_down[0], v_w_ple_gate[0], v_w_ple_proj[0]]
    big_d, big_nm, big_nv = [], [], []
    for i in range(6):
        d_, m_, v_ = _adamw(big[i], g_big[i], big_m[i], big_v[i], name=f"adamw_{i}")
        big_d.append(d_[None])
        big_nm.append(m_[None])
        big_nv.append(v_[None])
    g_big = [g[None] for g in g_big]

    def order(sm_list, bg_list):
        s, b = sm_list, bg_list
        return [s[0], b[0], s[1], s[2], s[3], b[1], s[4], b[2], s[5], s[6], b[3], s[7], b[4], b[5], s[8]]

    return (loss, grad_x[None], *order(small_g, g_big), *order(small_d, big_d), *order(small_nm, big_nm), *order(small_nv, big_nv))
```

```python
import functools
from typing import Callable, NamedTuple

import jax
import jax.numpy as jnp
import numpy as np
from jax import lax
from jax.experimental import pallas as pl
from jax.experimental.pallas import tpu as pltpu

f32 = jnp.float32
bf16 = jnp.bfloat16

CHUNK = 64
HEAD_DIM = 128
LEFT_CHUNKS = 8
REL_CLIP = 128
EPS = 1e-6
HG_SUB = 16
HG_HEADS_PER_STEP = 8
ATT_Q_ROWS = 256
ATT_HEADS_PER_STEP = 2
ADAM_LR, ADAM_B1, ADAM_B2, ADAM_EPS, ADAM_WD, ADAM_STEP = 0.001, 0.9, 0.999, 1e-08, 0.01, 10
LANES = 128
SUBLANES = 8
N_CHIPS = 4
N_DEV = 8
MESH = pl.DeviceIdType.MESH
NEG = float(np.finfo(np.float32).min)

NN = (((1,), (0,)), ((), ()))
NT = (((1,), (1,)), ((), ()))
TN = (((0,), (0,)), ((), ()))
HIGHEST = lax.Precision.HIGHEST


def _dot(a, b, dims, precision=None):
    return lax.dot_general(a, b, dims, preferred_element_type=f32, precision=precision)


def _sigmoid(v):
    return 1.0 / (1.0 + jnp.exp(-v))


def _div(n, pref, mult):
    best = None
    d = mult
    while d <= min(n, pref):
        if n % d == 0:
            best = d
        d += mult
    return best if best is not None else n


def _hbm():
    return pl.BlockSpec(memory_space=pltpu.HBM)


def _vmem():
    return pl.BlockSpec(memory_space=pltpu.VMEM)


def _mm(a, b, *, dims, tm, tn, tk, out_dtype, name, res=None, a_idx=None, b_idx=None, o_idx=None, out_shape=None, rider=None,
        cols_outer=False):
    if dims == "nn":
        (M, K), (_, N) = a.shape, b.shape
        a_blk, b_blk, dn = (tm, tk), (tk, tn), NN
        a_def, b_def = (lambda i, j, k: (i, k)), (lambda i, j, k: (k, j))
    elif dims == "nt":
        (M, K), (N, _) = a.shape, b.shape
        a_blk, b_blk, dn = (tm, tk), (tn, tk), NT
        a_def, b_def = (lambda i, j, k: (i, k)), (lambda i, j, k: (j, k))
    else:
        (K, M), (_, N) = a.shape, b.shape
        a_blk, b_blk, dn = (tk, tm), (tk, tn), TN
        a_def, b_def = (lambda i, j, k: (k, i)), (lambda i, j, k: (k, j))
    assert M % tm == 0 and N % tn == 0 and K % tk == 0, (name, M, N, K, tm, tn, tk)
    nk = K // tk
    has_res = res is not None

    def body(*refs):
        a_ref, b_ref = refs[0], refs[1]
        res_ref = refs[2] if has_res else None
        o_ref = refs[3] if has_res else refs[2]
        part = _dot(a_ref[...].astype(bf16), b_ref[...].astype(bf16), dn)

        def finish(acc):
            if has_res:
                acc = acc + res_ref[...]
            o_ref[...] = acc.astype(out_dtype)

        if nk == 1:
            finish(part)
        else:
            acc_ref = refs[-1]
            k = pl.program_id(2)

            @pl.when(k == 0)
            def _():
                acc_ref[...] = part

            @pl.when(k > 0)
            def _():
                acc_ref[...] += part

            @pl.when(k == nk - 1)
            def _():
                finish(acc_ref[...])

    def ordered(custom, default):
        f = custom or default
        return (lambda j, i, k: f(i, j, k)) if cols_outer else f

    o_map = ordered(o_idx, lambda i, j, k: (i, j))
    in_specs = [pl.BlockSpec(a_blk, ordered(a_idx, a_def)), pl.BlockSpec(b_blk, ordered(b_idx, b_def))]
    args = [a, b]
    if has_res:
        in_specs.append(pl.BlockSpec((tm, tn), o_map))
        args.append(res)
    outs = _grid_call(
        body,
        rider,
        name=name,
        grid=(N // tn, M // tm, nk) if cols_outer else (M // tm, N // tn, nk),
        in_specs=in_specs,
        out_specs=[pl.BlockSpec((tm, tn), o_map)],
        out_shape=[jax.ShapeDtypeStruct(out_shape or (M, N), out_dtype)],
        scratch_shapes=[pltpu.VMEM((tm, tn), f32)] if nk > 1 else [],
        args=args,
    )
    return outs[0] if rider is None else outs


def _rms_fwd(xv, g, *, name):
    S, D = xv.shape
    ts = _div(S, 512, SUBLANES)

    def body(x_ref, g_ref, o_ref):
        v = x_ref[...]
        r = lax.rsqrt(jnp.mean(v * v, axis=-1, keepdims=True) + EPS)
        o_ref[...] = (v * r * g_ref[...]).astype(bf16)

    return pl.pallas_call(
        body,
        name=name,
        grid=(S // ts,),
        in_specs=[pl.BlockSpec((ts, D), lambda i: (i, 0)), pl.BlockSpec((1, D), lambda i: (0, 0))],
        out_specs=pl.BlockSpec((ts, D), lambda i: (i, 0)),
        out_shape=jax.ShapeDtypeStruct((S, D), bf16),
        compiler_params=pltpu.CompilerParams(dimension_semantics=("parallel",)),
    )(xv, g)


def _rms_bwd(xv, g, dy, dres, *, name, rider=None):
    S, D = xv.shape
    ts = _div(S, 256, SUBLANES)

    def body(x_ref, g_ref, dy_ref, dres_ref, dx_ref, dxb_ref, dg_ref):
        i = pl.program_id(0)
        v = x_ref[...]
        r = lax.rsqrt(jnp.mean(v * v, axis=-1, keepdims=True) + EPS)
        vn = v * r
        d = dy_ref[...]
        part = jnp.sum(d * vn, axis=0, keepdims=True)

        @pl.when(i == 0)
        def _():
            dg_ref[...] = part

        @pl.when(i > 0)
        def _():
            dg_ref[...] += part

        t = d * g_ref[...]
        dx = dres_ref[...] + r * (t - vn * jnp.mean(t * vn, axis=-1, keepdims=True))
        dx_ref[...] = dx
        dxb_ref[...] = dx.astype(bf16)

    row = pl.BlockSpec((ts, D), lambda i: (i, 0))
    vec = pl.BlockSpec((1, D), lambda i: (0, 0))
    return _grid_call(
        body,
        rider,
        name=name,
        grid=(S // ts,),
        in_specs=[row, vec, row, row],
        out_specs=[row, row, vec],
        out_shape=[jax.ShapeDtypeStruct((S, D), f32), jax.ShapeDtypeStruct((S, D), bf16), jax.ShapeDtypeStruct((1, D), f32)],
        scratch_shapes=[],
        args=(xv, g, dy, dres),
    )


def _tail(h2, zg, pp, tgt, fn):
    S, D = h2.shape
    ts = _div(S, 256, SUBLANES)

    def body(h_ref, z_ref, p_ref, t_ref, fn_ref, dh_ref, dz_ref, dp_ref, loss_ref, dfn_ref):
        i = pl.program_id(0)
        sg = _sigmoid(z_ref[...])
        ppv = p_ref[...]
        h3 = h_ref[...] + sg * ppv
        r = lax.rsqrt(jnp.mean(h3 * h3, axis=-1, keepdims=True) + EPS)
        hn = h3 * r
        fnv = fn_ref[...]
        e = hn * fnv - t_ref[...]
        lpart = 0.5 * jnp.sum(jnp.mean(e * e, axis=-1, keepdims=True), axis=0, keepdims=True)
        dy = e * (1.0 / D)
        dpart = jnp.sum(dy * hn, axis=0, keepdims=True)

        @pl.when(i == 0)
        def _():
            loss_ref[...] = jnp.broadcast_to(lpart, loss_ref.shape)
            dfn_ref[...] = dpart

        @pl.when(i > 0)
        def _():
            loss_ref[...] += jnp.broadcast_to(lpart, loss_ref.shape)
            dfn_ref[...] += dpart

        t = dy * fnv
        dh3 = r * (t - hn * jnp.mean(t * hn, axis=-1, keepdims=True))
        dh_ref[...] = dh3
        dz_ref[...] = (dh3 * ppv * sg * (1.0 - sg)).astype(bf16)
        dp_ref[...] = (dh3 * sg).astype(bf16)

    row = pl.BlockSpec((ts, D), lambda i: (i, 0))
    vec = pl.BlockSpec((1, D), lambda i: (0, 0))
    one = pl.BlockSpec((1, LANES), lambda i: (0, 0))
    return pl.pallas_call(
        body,
        name="ple_final_loss",
        grid=(S // ts,),
        in_specs=[row, row, row, row, vec],
        out_specs=[row, row, row, one, vec],
        out_shape=[
            jax.ShapeDtypeStruct((S, D), f32),
            jax.ShapeDtypeStruct((S, D), bf16),
            jax.ShapeDtypeStruct((S, D), bf16),
            jax.ShapeDtypeStruct((1, LANES), f32),
            jax.ShapeDtypeStruct((1, D), f32),
        ],
        compiler_params=pltpu.CompilerParams(dimension_semantics=("arbitrary",)),
    )(h2, zg, pp, tgt, fn)


def _shift_down(v, halo, k):
    r = pltpu.roll(v, k, 0)
    hr = pltpu.roll(halo, k, 0)
    row = lax.broadcasted_iota(jnp.int32, hr.shape, 0)
    top = jnp.where(row < k, hr, r[0:SUBLANES])
    return top if v.shape[0] == SUBLANES else jnp.concatenate([top, r[SUBLANES:]], axis=0)


def _shift_up(v, halo, k):
    n = v.shape[0]
    r = pltpu.roll(v, n - k, 0)
    hr = pltpu.roll(halo, SUBLANES - k, 0)
    row = lax.broadcasted_iota(jnp.int32, hr.shape, 0)
    bot = jnp.where(row >= SUBLANES - k, hr, r[n - SUBLANES:])
    return jnp.concatenate([r[: n - SUBLANES], bot], axis=0)


def _conv_specs(S, F, tc, ts):
    nj = F // tc
    rows8 = ts // SUBLANES
    main = pl.BlockSpec((ts, 2 * tc), lambda j, i: (i, j))
    prev = pl.BlockSpec((SUBLANES, 2 * tc), lambda j, i: (jnp.maximum(i * rows8 - 1, 0), j))
    nxt = pl.BlockSpec((SUBLANES, 2 * tc), lambda j, i: (jnp.minimum((i + 1) * rows8, S // SUBLANES - 1), j))
    wg = pl.BlockSpec((3, tc), lambda j, i: (0, j))
    wv = pl.BlockSpec((3, tc), lambda j, i: (0, nj + j))
    bg = pl.BlockSpec((1, tc), lambda j, i: (0, j))
    bv = pl.BlockSpec((1, tc), lambda j, i: (0, nj + j))
    half = pl.BlockSpec((ts, tc), lambda j, i: (i, j))
    return nj, main, prev, nxt, wg, wv, bg, bv, half


def _conv_pre(u_ref, h_ref, wg_ref, wv_ref, bg_ref, bv_ref, tc):
    i = pl.program_id(1)
    u = u_ref[...]
    halo = jnp.where(i == 0, 0.0, h_ref[...])
    u1 = _shift_down(u, halo, 1)
    u2 = _shift_down(u, halo, 2)
    w = jnp.concatenate([wg_ref[...], wv_ref[...]], axis=1)
    b = jnp.concatenate([bg_ref[...], bv_ref[...]], axis=1)
    uc = b + w[0:1] * u2 + w[1:2] * u1 + w[2:3] * u
    return u, u1, u2, uc[:, :tc], uc[:, tc:]


def _convact_fwd(u, cw, cb, tc, rider=None):
    S, F2 = u.shape
    F = F2 // 2
    ts = _div(S, 512, SUBLANES)
    nj, main, prev, nxt, wg, wv, bg, bv, half = _conv_specs(S, F, tc, ts)

    def body(u_ref, h_ref, wg_ref, wv_ref, bg_ref, bv_ref, m_ref):
        _, _, _, g, v = _conv_pre(u_ref, h_ref, wg_ref, wv_ref, bg_ref, bv_ref, tc)
        m_ref[...] = (g * _sigmoid(g) * v).astype(bf16)

    outs = _grid_call(
        body,
        rider,
        name="convact_fwd",
        grid=(nj, S // ts),
        in_specs=[main, prev, wg, wv, bg, bv],
        out_specs=[half],
        out_shape=[jax.ShapeDtypeStruct((S, F), bf16)],
        scratch_shapes=[],
        args=(u, u, cw, cw, cb, cb),
    )
    return outs[0] if rider is None else outs


def _convact_bwd(u, dm, cw, cb, tc, rider=None):
    S, F2 = u.shape
    F = F2 // 2
    ts = _div(S, 512, SUBLANES)
    nj, main, prev, nxt, wg, wv, bg, bv, half = _conv_specs(S, F, tc, ts)
    ni = S // ts
    half_nxt = pl.BlockSpec((SUBLANES, tc), lambda j, i: (jnp.minimum((i + 1) * (ts // SUBLANES), S // SUBLANES - 1), j))

    def act_bwd(g, v, d):
        sg = _sigmoid(g)
        return jnp.concatenate([d * v * sg * (1.0 + g * (1.0 - sg)), d * g * sg], axis=1)

    def body(u_ref, h_ref, n_ref, dm_ref, dmn_ref, wg_ref, wv_ref, bg_ref, bv_ref, du_ref, dwg_ref, dwv_ref, dbg_ref, dbv_ref):
        i = pl.program_id(1)
        u0, u1, u2, g, v = _conv_pre(u_ref, h_ref, wg_ref, wv_ref, bg_ref, bv_ref, tc)
        duc = act_bwd(g, v, dm_ref[...])
        w = jnp.concatenate([wg_ref[...], wv_ref[...]], axis=1)
        b = jnp.concatenate([bg_ref[...], bv_ref[...]], axis=1)
        un = n_ref[...]
        tail = u0[ts - SUBLANES:]
        ucn = b + w[0:1] * _shift_down(un, tail, 2) + w[1:2] * _shift_down(un, tail, 1) + w[2:3] * un
        ducn = jnp.where(i == ni - 1, 0.0, act_bwd(ucn[:, :tc], ucn[:, tc:], dmn_ref[...]))
        du_ref[...] = (w[2:3] * duc + w[1:2] * _shift_up(duc, ducn, 1) + w[0:1] * _shift_up(duc, ducn, 2)).astype(bf16)
        db = jnp.sum(duc, axis=0, keepdims=True)
        dw = jnp.concatenate(
            [jnp.sum(duc * u2, axis=0, keepdims=True), jnp.sum(duc * u1, axis=0, keepdims=True), jnp.sum(duc * u0, axis=0, keepdims=True)],
            axis=0,
        )

        @pl.when(i == 0)
        def _():
            dwg_ref[...] = dw[:, :tc]
            dwv_ref[...] = dw[:, tc:]
            dbg_ref[...] = db[:, :tc]
            dbv_ref[...] = db[:, tc:]

        @pl.when(i > 0)
        def _():
            dwg_ref[...] += dw[:, :tc]
            dwv_ref[...] += dw[:, tc:]
            dbg_ref[...] += db[:, :tc]
            dbv_ref[...] += db[:, tc:]

    w_out = pl.BlockSpec((3, tc), lambda j, i: (0, j))
    b_out = pl.BlockSpec((1, tc), lambda j, i: (0, j))
    return _grid_call(
        body,
        rider,
        name="convact_bwd",
        grid=(nj, ni),
        in_specs=[main, prev, nxt, half, half_nxt, wg, wv, bg, bv],
        out_specs=[main, w_out, w_out, b_out, b_out],
        out_shape=[
            jax.ShapeDtypeStruct((S, F2), bf16),
            jax.ShapeDtypeStruct((3, F), f32),
            jax.ShapeDtypeStruct((3, F), f32),
            jax.ShapeDtypeStruct((1, F), f32),
            jax.ShapeDtypeStruct((1, F), f32),
        ],
        scratch_shapes=[],
        args=(u, u, u, dm, dm, cw, cw, cb, cb),
    )


def _hg_gates(qp, fp, lbl):
    lb = _sigmoid(lbl[0:1] - lbl[1:2])
    sig = _sigmoid(fp)
    sigm = _sigmoid(-fp)
    f = lb + (1.0 - lb) * sig
    k = (1.0 - lb) * sigm
    sq = _sigmoid(qp)
    qf = qp * sq
    row = lax.broadcasted_iota(jnp.int32, (CHUNK, CHUNK), 0)
    col = lax.broadcasted_iota(jnp.int32, (CHUNK, CHUNK), 1)
    b = _dot((row >= col).astype(f32), jnp.log(f), NN, HIGHEST)
    return lb, sig, sigm, f, k, sq, qf, b


def _hg_block(qf, k, b, I):
    r0, n = I * HG_SUB, (I + 1) * HG_SUB
    base = b[r0 - 1:r0] if I > 0 else jnp.zeros_like(b[0:1])
    eq = jnp.exp(b[r0:n] - base)
    ek = jnp.exp(base - b[0:n])
    qt16 = (qf[r0:n] * eq).astype(bf16)
    kt16 = (k[0:n] * ek).astype(bf16)
    row = lax.broadcasted_iota(jnp.int32, (HG_SUB, n), 0) + r0
    col = lax.broadcasted_iota(jnp.int32, (HG_SUB, n), 1)
    return r0, n, eq, ek, qt16, kt16, col <= row


def _per_head(v, hp, fn):
    return jnp.concatenate([fn(v[:, p * HEAD_DIM:(p + 1) * HEAD_DIM]) for p in range(hp)], axis=1)


def _head_mean(v, hp):
    return _per_head(v, hp, lambda t: jnp.broadcast_to(jnp.mean(t, axis=-1, keepdims=True), t.shape))


def _pad_rows(v, rows):
    return v if v.shape[0] == rows else jnp.concatenate([v, jnp.zeros((rows - v.shape[0], v.shape[1]), v.dtype)], axis=0)


def _hgrn_specs(S, W, hp, reverse):
    nc = S // CHUNK
    ngrp = W // (hp * HEAD_DIM)
    cw = hp * HEAD_DIM
    cidx = (lambda c: nc - 1 - c) if reverse else (lambda c: c)

    def proj(off):
        return pl.BlockSpec((CHUNK, cw), lambda h, c: (cidx(c), off * ngrp + h))

    act = pl.BlockSpec((CHUNK, cw), lambda h, c: (cidx(c), h))
    lbl = pl.BlockSpec((2, cw), lambda h, c: (0, h))
    gn = pl.BlockSpec((1, HEAD_DIM), lambda h, c: (0, 0))
    st = pl.BlockSpec((hp, 1, HEAD_DIM, HEAD_DIM), lambda h, c: (h, cidx(c), 0, 0))
    return nc, ngrp, proj, act, lbl, gn, st


def _grid_call(body, rider, *, name, grid, in_specs, out_specs, out_shape, scratch_shapes, args):
    params = pltpu.CompilerParams(dimension_semantics=("arbitrary",) * len(grid))
    if rider is None:
        return pl.pallas_call(body, name=name, grid=grid, in_specs=in_specs, out_specs=out_specs, out_shape=out_shape,
                              scratch_shapes=scratch_shapes, compiler_params=params)(*args)
    n_in, n_out, n_scr = len(in_specs), len(out_specs), len(scratch_shapes)
    ri, ro = len(rider.ins), len(rider.out_shapes)

    def wrapped(*refs):
        a, b = n_in, n_in + ri
        c, d = b + n_out, b + n_out + ro
        e = d + n_scr
        ids = [pl.program_id(t) for t in range(len(grid))]
        is_first = functools.reduce(jnp.logical_and, [i == 0 for i in ids])
        is_last = functools.reduce(jnp.logical_and, [i == g - 1 for i, g in zip(ids, grid)])

        @pl.when(is_first)
        def _():
            rider.first(refs[a:b], refs[c:d], refs[e:])

        body(*refs[:a], *refs[b:c], *refs[d:e])

        @pl.when(is_last)
        def _():
            rider.last(refs[a:b], refs[c:d], refs[e:])

    return pl.pallas_call(
        wrapped,
        name=name,
        grid=grid,
        in_specs=list(in_specs) + [_hbm()] * ri,
        out_specs=list(out_specs) + [_hbm()] * ro,
        out_shape=list(out_shape) + list(rider.out_shapes),
        input_output_aliases={n_in + i: n_out + o for i, o in rider.aliases.items()},
        scratch_shapes=list(scratch_shapes) + list(rider.sems),
        compiler_params=params,
    )(*args, *rider.ins)


def _hgrn_fwd(proj, lb_logits, hg_norm, W, rider=None):
    S = proj.shape[0]
    hp = min(HG_HEADS_PER_STEP, W // HEAD_DIM)
    nc, ngrp, pspec, act, lbl_spec, gn_spec, st_spec = _hgrn_specs(S, W, hp, False)
    nb = CHUNK // HG_SUB

    def body(q_ref, f_ref, i_ref, g_ref, lbl_ref, gn_ref, y_ref, o_ref, st_ref, state):
        c = pl.program_id(1)

        @pl.when(c == 0)
        def _():
            state[...] = jnp.zeros_like(state)

        _, _, _, _, k, _, qf, b = _hg_gates(q_ref[...], f_ref[...], lbl_ref[...])
        v16 = i_ref[...].astype(bf16)
        a16 = (qf * jnp.exp(b)).astype(bf16)
        bl = b[CHUNK - 1:CHUNK]
        kd16 = (k * jnp.exp(bl - b)).astype(bf16)
        ebl = jnp.exp(bl)
        blocks = [_hg_block(qf, k, b, I) for I in range(nb)]
        heads = [slice(p * HEAD_DIM, (p + 1) * HEAD_DIM) for p in range(hp)]
        st_prev = state[...]
        st_ref[:, 0] = st_prev
        inter = [_dot(a16[:, sl], st_prev[p].astype(bf16), NT) for p, sl in enumerate(heads)]
        scores = [[_dot(qt16[:, sl], kt16[:, sl], NT) for sl in heads] for _, _, _, _, qt16, kt16, _ in blocks]
        intra = [[_dot(jnp.where(blk[6], sc, 0.0).astype(bf16), v16[0:blk[1], sl], NN) for sc, sl in zip(scs, heads)]
                 for blk, scs in zip(blocks, scores)]
        st_new = [st_prev[p] * ebl[:, sl] + _dot(v16[:, sl], kd16[:, sl], TN) for p, sl in enumerate(heads)]
        state[...] = jnp.stack(st_new)
        o = jnp.concatenate([inter[p] + jnp.concatenate([rows[p] for rows in intra], axis=0) for p in range(hp)], axis=1)
        o_ref[...] = o
        r = lax.rsqrt(_head_mean(o * o, hp) + EPS)
        gp = g_ref[...]
        y_ref[...] = (o * r * jnp.tile(gn_ref[...], (1, hp)) * (gp * _sigmoid(gp))).astype(bf16)

    H = W // HEAD_DIM
    return _grid_call(
        body,
        rider,
        name="hgrn_fwd",
        grid=(ngrp, nc),
        in_specs=[pspec(0), pspec(1), pspec(2), pspec(3), lbl_spec, gn_spec],
        out_specs=[act, act, st_spec],
        out_shape=[
            jax.ShapeDtypeStruct((S, W), bf16),
            jax.ShapeDtypeStruct((S, W), f32),
            jax.ShapeDtypeStruct((H, nc, HEAD_DIM, HEAD_DIM), f32),
        ],
        scratch_shapes=[pltpu.VMEM((hp, HEAD_DIM, HEAD_DIM), f32)],
        args=(proj, proj, proj, proj, lb_logits, hg_norm),
    )


def _hgrn_bwd(proj, o_hg, st, dycat, lb_logits, hg_norm, W, rider=None):
    S = proj.shape[0]
    hp = min(HG_HEADS_PER_STEP, W // HEAD_DIM)
    assert hp * HEAD_DIM == W, "the four gradients share one output block: all heads in one grid step"
    nc, ngrp, pspec, act, lbl_spec, gn_spec, st_spec = _hgrn_specs(S, W, hp, True)
    nb = CHUNK // HG_SUB

    def body(q_ref, f_ref, i_ref, g_ref, o_ref, st_ref, dy_ref, lbl_ref, gn_ref,
             d4_ref, dlbl_ref, dgn_ref, dstate):
        cw_ = hp * HEAD_DIM
        dq_ref, df_ref, di_ref, dg_ref = (d4_ref.at[:, pl.ds(t * cw_, cw_)] for t in range(4))
        h = pl.program_id(0)
        c = pl.program_id(1)

        @pl.when(c == 0)
        def _():
            dstate[...] = jnp.zeros_like(dstate)
            dlbl_ref[...] = jnp.zeros_like(dlbl_ref)

        @pl.when((c == 0) & (h == 0))
        def _():
            dgn_ref[...] = jnp.zeros_like(dgn_ref)

        row = lax.broadcasted_iota(jnp.int32, (CHUNK, CHUNK), 0)
        col = lax.broadcasted_iota(jnp.int32, (CHUNK, CHUNK), 1)
        upper = (col >= row).astype(f32)
        last_row = lax.broadcasted_iota(jnp.int32, (CHUNK, hp * HEAD_DIM), 0) == CHUNK - 1

        qp, fp = q_ref[...], f_ref[...]
        lb, sig, sigm, f, k, sq, qf, b = _hg_gates(qp, fp, lbl_ref[...])
        v16 = i_ref[...].astype(bf16)
        gnv = jnp.tile(gn_ref[...], (1, hp))
        gp = g_ref[...]
        sgg = _sigmoid(gp)
        gate = gp * sgg
        o = o_ref[...]
        r = lax.rsqrt(_head_mean(o * o, hp) + EPS)
        on = o * r
        dy = dy_ref[...]
        dg_ref[...] = (dy * on * gnv * (sgg * (1.0 + gp * (1.0 - sgg)))).astype(bf16)
        dgn_wide = jnp.sum(dy * on * gate, axis=0, keepdims=True)
        dgn_ref[...] += functools.reduce(jnp.add, [dgn_wide[:, p * HEAD_DIM:(p + 1) * HEAD_DIM] for p in range(hp)])
        don = dy * gnv * gate
        do16 = (r * (don - on * _head_mean(don * on, hp))).astype(bf16)
        eb = jnp.exp(b)
        A = qf * eb
        a16 = A.astype(bf16)
        bl = b[CHUNK - 1:CHUNK]
        ebl = jnp.exp(bl)
        ekd = jnp.exp(bl - b)
        kd = k * ekd
        kd16 = kd.astype(bf16)
        blocks = [_hg_block(qf, k, b, I) for I in range(nb)]

        heads = [slice(p * HEAD_DIM, (p + 1) * HEAD_DIM) for p in range(hp)]
        st_prev = st_ref[:, 0]
        dst_new = dstate[...]
        st16, dst16 = st_prev.astype(bf16), dst_new.astype(bf16)
        dA_h = [_dot(do16[:, sl], st16[p], NN) for p, sl in enumerate(heads)]
        dkd_h = [_dot(v16[:, sl], dst16[p], NN) for p, sl in enumerate(heads)]
        dv_h = [_dot(kd16[:, sl], dst16[p], NT) for p, sl in enumerate(heads)]
        dstate[...] = jnp.stack([dst_new[p] * ebl[:, sl] + _dot(do16[:, sl], a16[:, sl], TN) for p, sl in enumerate(heads)])
        dbl_h = [jnp.sum(dst_new[p] * st_prev[p], axis=0, keepdims=True) for p in range(hp)]
        sc_h = [[jnp.where(mask, _dot(qt16[:, sl], kt16[:, sl], NT), 0.0).astype(bf16) for sl in heads]
                for _, _, _, _, qt16, kt16, mask in blocks]
        dsc_h = [[jnp.where(mask, _dot(do16[r0:n, sl], v16[0:n, sl], NT), 0.0).astype(bf16) for sl in heads]
                 for r0, n, _, _, _, _, mask in blocks]
        dvi_h = [[_dot(sc, do16[blk[0]:blk[1], sl], TN) for sc, sl in zip(scs, heads)] for blk, scs in zip(blocks, sc_h)]
        dqt_h = [[_dot(dsc, blk[5][:, sl], NN) for dsc, sl in zip(dscs, heads)] for blk, dscs in zip(blocks, dsc_h)]
        dkt_h = [[_dot(dsc, blk[4][:, sl], TN) for dsc, sl in zip(dscs, heads)] for blk, dscs in zip(blocks, dsc_h)]
        dv_h = [functools.reduce(jnp.add, [dv_h[p]] + [_pad_rows(rows[p], CHUNK) for rows in dvi_h]) for p in range(hp)]

        dA, dkd = jnp.concatenate(dA_h, axis=1), jnp.concatenate(dkd_h, axis=1)
        dbl = jnp.concatenate(dbl_h, axis=1) * ebl + jnp.sum(dkd * kd, axis=0, keepdims=True)
        db = dA * A - dkd * kd + jnp.where(last_row, dbl, 0.0)
        dk = dkd * ekd
        dq_rows, db_rows = [], []
        for I, (r0, n, eq, ek, qt16, kt16, _) in enumerate(blocks):
            dqt, dkt = jnp.concatenate(dqt_h[I], axis=1), jnp.concatenate(dkt_h[I], axis=1)
            dq_rows.append(dqt * eq)
            dk = dk + _pad_rows(dkt * ek, CHUNK)
            db_rows.append(dqt * qt16.astype(f32))
            db = db - _pad_rows(dkt * kt16.astype(f32), CHUNK)
        dqf = dA * eb + jnp.concatenate(dq_rows, axis=0)
        db = db + jnp.concatenate(db_rows, axis=0)
        dlogf = _dot(upper, db, NN, HIGHEST)
        dfg = dlogf / f
        df_ref[...] = ((1.0 - lb) * sig * sigm * (dfg - dk)).astype(bf16)
        dlb = jnp.sum(sigm * (dfg - dk), axis=0, keepdims=True)
        dl0 = dlb * lb * (1.0 - lb)
        dlbl_ref[...] += jnp.concatenate([dl0, -dl0], axis=0)
        dq_ref[...] = (dqf * (sq * (1.0 + qp * (1.0 - sq)))).astype(bf16)
        di_ref[...] = jnp.concatenate(dv_h, axis=1).astype(bf16)

    cw = hp * HEAD_DIM
    dy_spec = pl.BlockSpec((CHUNK, cw), lambda h, c: (nc - 1 - c, h))
    return _grid_call(
        body,
        rider,
        name="hgrn_bwd",
        grid=(ngrp, nc),
        in_specs=[pspec(0), pspec(1), pspec(2), pspec(3), act, st_spec, dy_spec, lbl_spec, gn_spec],
        out_specs=[pl.BlockSpec((CHUNK, 4 * W), lambda h, c: (nc - 1 - c, 0)), lbl_spec, gn_spec],
        out_shape=[jax.ShapeDtypeStruct((S, 4 * W), bf16), jax.ShapeDtypeStruct((2, W), f32), jax.ShapeDtypeStruct((1, HEAD_DIM), f32)],
        scratch_shapes=[pltpu.VMEM((hp, HEAD_DIM, HEAD_DIM), f32)],
        args=(proj, proj, proj, proj, o_hg, st, dycat, lb_logits, hg_norm),
    )


def _att_dims(S):
    qt = ATT_Q_ROWS if S % ATT_Q_ROWS == 0 else CHUNK
    pad = LEFT_CHUNKS * CHUNK
    kb = pad + qt
    return qt, pad, kb, kb + qt


def _kv_padded(proj, W, pad):
    S = proj.shape[0]
    ts = _div(pad, 512, 2 * SUBLANES)
    assert S % ts == 0 and pad % ts == 0
    npad = pad // ts

    def body(k_ref, v_ref, ko_ref, vo_ref):
        i = pl.program_id(0)

        @pl.when(i < npad)
        def _():
            ko_ref[...] = jnp.zeros_like(ko_ref)
            vo_ref[...] = jnp.zeros_like(vo_ref)

        @pl.when(i >= npad)
        def _():
            ko_ref[...] = k_ref[...].astype(bf16)
            vo_ref[...] = v_ref[...].astype(bf16)

    out = pl.BlockSpec((ts, W), lambda i: (i, 0))
    return pl.pallas_call(
        body,
        name="kv_padded",
        grid=((S + pad) // ts,),
        in_specs=[pl.BlockSpec((ts, W), lambda i: (jnp.maximum(i - npad, 0), 5)), pl.BlockSpec((ts, W), lambda i: (jnp.maximum(i - npad, 0), 6))],
        out_specs=[out, out],
        out_shape=[jax.ShapeDtypeStruct((S + pad, W), bf16)] * 2,
        compiler_params=pltpu.CompilerParams(dimension_semantics=("parallel",)),
    )(proj, proj)


def _att_probs(qk, bias, start, qt, pad, kb):
    s = qk * (HEAD_DIM ** -0.5) + bias
    row = lax.broadcasted_iota(jnp.int32, (qt, kb), 0)
    col = lax.broadcasted_iota(jnp.int32, (qt, kb), 1)
    lo = jnp.bitwise_and(row, -CHUNK)
    ok = (col >= lo) & (col < lo + pad + CHUNK) & (col + start >= pad)
    s = jnp.where(ok, s, NEG)
    e = jnp.exp(s - jnp.max(s, axis=-1, keepdims=True))
    return e / jnp.sum(e, axis=-1, keepdims=True)


def _att_bias(ext_row, qt, kb, ne):
    e = jnp.broadcast_to(ext_row, (qt, ne))
    return pltpu.roll(e, ne - qt + 1, 1, stride=1, stride_axis=0)[:, :kb]


def _att_specs(S, H, ah, qt, pad, ne):
    cw = ah * HEAD_DIM
    q_spec = pl.BlockSpec((qt, cw), lambda h, g: (g, 4 * H // ah + h))
    kv_spec = pl.BlockSpec((S + pad, cw), lambda h, g: (0, h))
    ext_spec = pl.BlockSpec((ah, 1, ne), lambda h, g: (h, 0, 0))
    row_spec = pl.BlockSpec((qt, cw), lambda h, g: (g, h))
    return cw, q_spec, kv_spec, ext_spec, row_spec


def _attn_fwd(proj, kp, vp, ext, W, rider=None):
    S = proj.shape[0]
    H = W // HEAD_DIM
    ah = min(ATT_HEADS_PER_STEP, H)
    qt, pad, kb, ne = _att_dims(S)
    cw, q_spec, kv_spec, ext_spec, row_spec = _att_specs(S, H, ah, qt, pad, ne)
    heads = [slice(a * HEAD_DIM, (a + 1) * HEAD_DIM) for a in range(ah)]

    def body(q_ref, k_ref, v_ref, ext_ref, o_ref, bias):
        g = pl.program_id(1)

        @pl.when(g == 0)
        def _():
            for a in range(ah):
                bias[a] = _att_bias(ext_ref[a], qt, kb, ne)

        start = pl.multiple_of(g * qt, qt)
        q16 = q_ref[...].astype(bf16)
        kb16 = k_ref[pl.ds(start, kb), :]
        vb16 = v_ref[pl.ds(start, kb), :]
        qk = [_dot(q16[:, sl], kb16[:, sl], NT) for sl in heads]
        pn = [_att_probs(qk[a], bias[a], start, qt, pad, kb).astype(bf16) for a in range(ah)]
        o_ref[...] = jnp.concatenate([_dot(pn[a], vb16[:, sl], NN) for a, sl in enumerate(heads)], axis=1).astype(bf16)

    outs = _grid_call(
        body,
        rider,
        name="attn_fwd",
        grid=(H // ah, S // qt),
        in_specs=[q_spec, kv_spec, kv_spec, ext_spec],
        out_specs=[row_spec],
        out_shape=[jax.ShapeDtypeStruct((S, W), bf16)],
        scratch_shapes=[pltpu.VMEM((ah, qt, kb), f32)],
        args=(proj, kp, vp, ext),
    )
    return outs[0] if rider is None else outs


def _attn_bwd(proj, kp, vp, ext, dycat, W, rider=None):
    S = proj.shape[0]
    H = W // HEAD_DIM
    ah = min(ATT_HEADS_PER_STEP, H)
    qt, pad, kb, ne = _att_dims(S)
    cw, q_spec, kv_spec, ext_spec, row_spec = _att_specs(S, H, ah, qt, pad, ne)
    heads = [slice(a * HEAD_DIM, (a + 1) * HEAD_DIM) for a in range(ah)]
    scale = HEAD_DIM ** -0.5
    ng = S // qt

    def body(q_ref, k_ref, v_ref, ext_ref, do_ref, dq_ref, dko_ref, dvo_ref, db_ref, bias, dk_ref, dv_ref):
        g = pl.program_id(1)

        @pl.when(g == 0)
        def _():
            for a in range(ah):
                bias[a] = _att_bias(ext_ref[a], qt, kb, ne)
            dk_ref[...] = jnp.zeros_like(dk_ref)
            dv_ref[...] = jnp.zeros_like(dv_ref)
            db_ref[...] = jnp.zeros_like(db_ref)

        start = pl.multiple_of(g * qt, qt)
        q16 = q_ref[...].astype(bf16)
        kb16 = k_ref[pl.ds(start, kb), :]
        vb16 = v_ref[pl.ds(start, kb), :]
        do16 = do_ref[...].astype(bf16)
        qk = [_dot(q16[:, sl], kb16[:, sl], NT) for sl in heads]
        dpn = [_dot(do16[:, sl], vb16[:, sl], NT) for sl in heads]
        pn = [_att_probs(qk[a], bias[a], start, qt, pad, kb) for a in range(ah)]
        ds = [pn[a] * (dpn[a] - jnp.sum(dpn[a] * pn[a], axis=-1, keepdims=True)) for a in range(ah)]
        ds16 = [d.astype(bf16) for d in ds]
        dv_ref[pl.ds(start, kb), :] += jnp.concatenate([_dot(pn[a].astype(bf16), do16[:, sl], TN) for a, sl in enumerate(heads)], axis=1)
        dq_ref[...] = (jnp.concatenate([_dot(ds16[a], kb16[:, sl], NN) for a, sl in enumerate(heads)], axis=1) * scale).astype(bf16)
        dk_ref[pl.ds(start, kb), :] += jnp.concatenate([_dot(ds16[a], q16[:, sl], TN) for a, sl in enumerate(heads)], axis=1) * scale
        for a in range(ah):
            db_ref[a] += ds[a]

        @pl.when(g == ng - 1)
        def _():
            dko_ref[...] = dk_ref[pl.ds(pad, S), :].astype(bf16)
            dvo_ref[...] = dv_ref[pl.ds(pad, S), :].astype(bf16)

    kv_out = pl.BlockSpec((S, cw), lambda h, g: (0, h))
    return _grid_call(
        body,
        rider,
        name="attn_bwd",
        grid=(H // ah, S // qt),
        in_specs=[q_spec, kv_spec, kv_spec, ext_spec,
                  pl.BlockSpec((qt, cw), lambda h, g: (g, H // ah + h))],
        out_specs=[row_spec, kv_out, kv_out, pl.BlockSpec((ah, qt, kb), lambda h, g: (h, 0, 0))],
        out_shape=[
            jax.ShapeDtypeStruct((S, W), bf16),
            jax.ShapeDtypeStruct((S, W), bf16),
            jax.ShapeDtypeStruct((S, W), bf16),
            jax.ShapeDtypeStruct((H, qt, kb), f32),
        ],
        scratch_shapes=[pltpu.VMEM((ah, qt, kb), f32), pltpu.VMEM((S + pad, cw), f32), pltpu.VMEM((S + pad, cw), f32)],
        args=(proj, kp, vp, ext, dycat),
    )


def _bias_onehot(qt, ne, nrel_pad):
    m = lax.broadcasted_iota(jnp.int32, (nrel_pad, ne), 1)
    r = lax.broadcasted_iota(jnp.int32, (nrel_pad, ne), 0)
    rel = LEFT_CHUNKS * CHUNK + qt - 1 - m
    hot = (r == jnp.clip(rel, -REL_CLIP, REL_CLIP) + REL_CLIP) & (m < ne - 1)
    return hot.astype(f32)


def _bias_ext(rel_bias_pad, onehot):
    H, _, nr = rel_bias_pad.shape
    ne = onehot.shape[1]

    def body(rb_ref, oh_ref, o_ref):
        o_ref[0] = _dot(rb_ref[0], oh_ref[...], NN, HIGHEST)

    return pl.pallas_call(
        body,
        name="bias_ext",
        grid=(H,),
        in_specs=[pl.BlockSpec((1, 1, nr), lambda h: (h, 0, 0)), pl.BlockSpec((nr, ne), lambda h: (0, 0))],
        out_specs=pl.BlockSpec((1, 1, ne), lambda h: (h, 0, 0)),
        out_shape=jax.ShapeDtypeStruct((H, 1, ne), f32),
        compiler_params=pltpu.CompilerParams(dimension_semantics=("parallel",)),
    )(rel_bias_pad, onehot)


def _bias_bwd(dbias, onehot):
    H, qt, kb = dbias.shape
    nr, ne = onehot.shape

    def body(d_ref, oh_ref, o_ref):
        row = lax.broadcasted_iota(jnp.int32, (qt, qt), 0)
        col = lax.broadcasted_iota(jnp.int32, (qt, qt), 1)
        flip = (row + col == qt - 1).astype(f32)
        x = _dot(flip, d_ref[0], NN, HIGHEST)
        x = jnp.concatenate([x, jnp.zeros((qt, ne - kb), f32)], axis=1)
        de = jnp.sum(pltpu.roll(x, 0, 1, stride=1, stride_axis=0), axis=0, keepdims=True)
        o_ref[0] = _dot(de, oh_ref[...], NT, HIGHEST)

    return pl.pallas_call(
        body,
        name="bias_bwd",
        grid=(H,),
        in_specs=[pl.BlockSpec((1, qt, kb), lambda h: (h, 0, 0)), pl.BlockSpec((nr, ne), lambda h: (0, 0))],
        out_specs=pl.BlockSpec((1, 1, nr), lambda h: (h, 0, 0)),
        out_shape=jax.ShapeDtypeStruct((H, 1, nr), f32),
        compiler_params=pltpu.CompilerParams(dimension_semantics=("parallel",)),
    )(dbias, onehot)


def _place():
    x, y, c = lax.axis_index("x"), lax.axis_index("y"), lax.axis_index("c")
    return x, y, c


def _flip(x, y, k):
    return (1 - x if k & 2 else x), (1 - y if k & 1 else y)


def _region(ref, shard_shape, axis, j, half):
    R, C = shard_shape
    if axis == 0:
        if half is None:
            return ref.at[pl.ds(j * R, R), :]
        return ref.at[pl.ds(j * R + half * (R // 2), R // 2), :]
    if half is None:
        return ref.at[:, pl.ds(j * C, C)]
    return ref.at[pl.ds(half * (R // 2), R // 2), pl.ds(j * C, C)]


def _remote(src, dst, send_sem, recv_sem, dev):
    return pltpu.make_async_remote_copy(src_ref=src, dst_ref=dst, send_sem=send_sem, recv_sem=recv_sem,
                                        device_id=dev, device_id_type=MESH)


def _cast_place(shard, axis, where, *, name):
    R, C = shard.shape
    tr = _div(R, 256, 2 * SUBLANES)
    nr = R // tr
    full = (N_CHIPS * R, C) if axis == 0 else (R, N_CHIPS * C)
    omap = (lambda r, s: (s[0] * nr + r, 0)) if axis == 0 else (lambda r, s: (r, s[0]))

    def body(s_ref, x_ref, o_ref):
        o_ref[...] = x_ref[...].astype(bf16)

    return pl.pallas_call(
        body,
        name=name,
        grid_spec=pltpu.PrefetchScalarGridSpec(
            num_scalar_prefetch=1,
            grid=(nr,),
            in_specs=[pl.BlockSpec((tr, C), lambda r, s: (r, 0))],
            out_specs=pl.BlockSpec((tr, C), omap),
        ),
        out_shape=jax.ShapeDtypeStruct(full, bf16),
        compiler_params=pltpu.CompilerParams(dimension_semantics=("parallel",)),
    )(where, shard)


class _Rider(NamedTuple):
    ins: list
    out_shapes: list
    aliases: dict
    sems: list
    first: Callable
    last: Callable


def _run_rider(rider, *, name):
    ni, no = len(rider.ins), len(rider.out_shapes)

    def body(*refs):
        ins, outs, sems = refs[:ni], refs[ni:ni + no], refs[ni + no:]
        rider.first(ins, outs, sems)
        rider.last(ins, outs, sems)

    return pl.pallas_call(
        body,
        name=name,
        in_specs=[_hbm()] * ni,
        out_specs=[_hbm()] * no,
        out_shape=rider.out_shapes,
        input_output_aliases=rider.aliases,
        scratch_shapes=rider.sems,
    )(*rider.ins)


def _start_all(copies):
    def first(ins, outs, sems):
        for cp in copies(ins, outs, sems):
            cp.start()

    def last(ins, outs, sems):
        cps = copies(ins, outs, sems)
        for cp in cps:
            cp.wait_recv()
        for cp in cps:
            cp.wait_send()

    return first, last


def _mm_while_gathering(a, full, shard_shape, where, *, tm, tn, name):
    S, K = a.shape
    R, Cs = shard_shape
    N = full.shape[1]
    tps, nj, ni = Cs // tn, N // tn, S // tm
    assert R == K and Cs % tn == 0 and S % tm == 0

    def block(j, chip):
        return jnp.bitwise_xor(chip, j // tps) * tps + j % tps

    def body(s_ref, a_ref, w_in, o_ref, w_ref, buf, fsem, send1, recv1, send2, recv2):
        j, i = pl.program_id(0), pl.program_id(1)
        x, y, c = _place()
        chip = 2 * x + y

        def ici(k):
            px, py = _flip(x, y, k)
            mine = _region(w_ref, shard_shape, 1, chip, c)
            return _remote(mine, mine, send1.at[k - 1], recv1.at[k - 1], (px, py, c))

        def handed(k, half):
            px, py = _flip(x, y, k)
            land = _region(w_ref, shard_shape, 1, 2 * px + py, half)
            return land, _remote(land, land, send2.at[k - 1], recv2.at[k - 1], (x, y, 1 - c))

        def tile(jj):
            col = pl.multiple_of(block(jj, chip) * tn, LANES)
            return pltpu.make_async_copy(w_ref.at[:, pl.ds(col, tn)], buf.at[jj % 2], fsem.at[jj % 2])

        @pl.when((j == 0) & (i == 0))
        def _():
            for k in (1, 2, 3):
                ici(k).start()
            tile(j).start()

        @pl.when(i == 0)
        def _():
            tile(j).wait()

        @pl.when((i == ni - 1) & (j + 1 < nj))
        def _():
            for k in (1, 2, 3):
                @pl.when(j + 1 == k * tps)
                def _():
                    land, forward = handed(k, c)
                    _remote(land, land, send1.at[k - 1], recv1.at[k - 1], (x, y, c)).wait_recv()
                    forward.start()
                    handed(k, 1 - c)[1].wait_recv()

            tile(j + 1).start()

        o_ref[...] = _dot(a_ref[...], buf[j % 2], NN)

        @pl.when((j == nj - 1) & (i == ni - 1))
        def _():
            for k in (1, 2, 3):
                ici(k).wait_send()
                handed(k, c)[1].wait_send()

    sems = pltpu.SemaphoreType.DMA((3,))
    spec = pltpu.PrefetchScalarGridSpec(
        num_scalar_prefetch=1,
        grid=(nj, ni),
        in_specs=[pl.BlockSpec((tm, K), lambda j, i, s: (i, 0)), _hbm()],
        out_specs=[pl.BlockSpec((tm, tn), lambda j, i, s: (i, block(j, s[0]))), _hbm()],
        scratch_shapes=[pltpu.VMEM((2, K, tn), full.dtype), pltpu.SemaphoreType.DMA((2,)), sems, sems, sems, sems],
    )
    return pl.pallas_call(
        body,
        name=name,
        grid_spec=spec,
        out_shape=[jax.ShapeDtypeStruct((S, N), f32), jax.ShapeDtypeStruct(full.shape, full.dtype)],
        input_output_aliases={2: 1},
        compiler_params=pltpu.CompilerParams(dimension_semantics=("arbitrary", "arbitrary")),
    )(where, a, full)


class _GatherJob(NamedTuple):
    buf: int
    shard_shape: tuple
    axis: int
    split: bool
    rels: tuple
    hand_over: bool


def _gather_rider(bufs, jobs):
    def copies(ins, outs, sems):
        x, y, c = _place()
        cps = []
        for n, job in enumerate(jobs):
            half = c if job.split else None
            for k in job.rels:
                px, py = _flip(x, y, k)
                chip, dev = (2 * px + py, (x, y, 1 - c)) if job.hand_over else (2 * x + y, (px, py, c))
                cps.append(_remote(_region(ins[job.buf], job.shard_shape, job.axis, chip, half),
                                   _region(outs[job.buf], job.shard_shape, job.axis, chip, half),
                                   sems[0].at[3 * n + k - 1], sems[1].at[3 * n + k - 1], dev))
        return cps

    assert all(job.split or not job.hand_over for job in jobs)
    sems = pltpu.SemaphoreType.DMA((3 * len(jobs),))
    return _Rider(list(bufs), [jax.ShapeDtypeStruct(b.shape, b.dtype) for b in bufs], {m: m for m in range(len(bufs))},
                  [sems, sems], *_start_all(copies))


def _chips_rider(ts, rels=(1, 2, 3), landing=None):
    n = len(ts)

    def copies(ins, outs, sems):
        x, y, c = _place()
        cps = []
        for m in range(n):
            for k in rels:
                px, py = _flip(x, y, k)
                cps.append(_remote(ins[m].at[2 * px + py], outs[m].at[k - 1], sems[0].at[3 * m + k - 1], sems[1].at[3 * m + k - 1], (px, py, c)))
        return cps

    sems = pltpu.SemaphoreType.DMA((3 * n,))
    return _Rider(list(ts) + list(landing or []), [jax.ShapeDtypeStruct((3,) + t.shape[1:], t.dtype) for t in ts],
                  {n + m: m for m in range(n)} if landing else {}, [sems, sems], *_start_all(copies))


def _sibling_rider(dws, shard_shapes, axes):
    n = len(dws)

    def copies(ins, outs, sems):
        x, y, c = _place()
        return [_remote(_region(ins[m], shard_shapes[m], axes[m], j, 1 - c), outs[m].at[j],
                        sems[0].at[N_CHIPS * m + j], sems[1].at[N_CHIPS * m + j], (x, y, 1 - c))
                for m in range(n) for j in range(N_CHIPS)]

    sems = pltpu.SemaphoreType.DMA((N_CHIPS * n,))
    return _Rider(list(dws), [jax.ShapeDtypeStruct((N_CHIPS, s[0] // 2, s[1]), d.dtype) for s, d in zip(shard_shapes, dws)], {},
                  [sems, sems], *_start_all(copies))


def _join_rider(gs):
    n = len(gs)

    def copies(ins, outs, sems):
        x, y, c = _place()
        cps = []
        for m in range(n):
            hr = gs[m].shape[0] // 2
            cps.append(_remote(ins[m].at[pl.ds(c * hr, hr), :], outs[m].at[pl.ds(c * hr, hr), :], sems[0].at[m], sems[1].at[m], (x, y, 1 - c)))
        return cps

    sems = pltpu.SemaphoreType.DMA((n,))
    return _Rider(list(gs), [jax.ShapeDtypeStruct(g.shape, g.dtype) for g in gs], {m: m for m in range(n)}, [sems, sems], *_start_all(copies))


def _both(r1, r2):
    i1, o1, s1 = len(r1.ins), len(r1.out_shapes), len(r1.sems)
    aliases = dict(r1.aliases)
    aliases.update({i1 + i: o1 + o for i, o in r2.aliases.items()})

    def first(ins, outs, sems):
        r1.first(ins[:i1], outs[:o1], sems[:s1])
        r2.first(ins[i1:], outs[o1:], sems[s1:])

    def last(ins, outs, sems):
        r1.last(ins[:i1], outs[:o1], sems[:s1])
        r2.last(ins[i1:], outs[o1:], sems[s1:])

    return _Rider(r1.ins + r2.ins, r1.out_shapes + r2.out_shapes, aliases, r1.sems + r2.sems, first, last)


def _pair_sum(dw, got, shard_shape, axis, where, *, name):
    R, C = shard_shape
    hr = R // 2
    tr = _div(hr, 256, 2 * SUBLANES)
    nr = hr // tr
    dmap = (lambda j, r, s: ((2 * j + s[1]) * nr + r, 0)) if axis == 0 else (lambda j, r, s: (s[1] * nr + r, j))
    slot = pl.BlockSpec((1, tr, C), lambda j, r, s: (j, r, 0))

    def body(s_ref, d_ref, g_ref, o_ref):
        o_ref[0] = (d_ref[...].astype(f32) + g_ref[0].astype(f32)).astype(bf16)

    return pl.pallas_call(
        body,
        name=name,
        grid_spec=pltpu.PrefetchScalarGridSpec(
            num_scalar_prefetch=1,
            grid=(N_CHIPS, nr),
            in_specs=[pl.BlockSpec((tr, C), dmap), slot],
            out_specs=slot,
        ),
        out_shape=jax.ShapeDtypeStruct((N_CHIPS, hr, C), bf16),
        compiler_params=pltpu.CompilerParams(dimension_semantics=("parallel", "parallel")),
    )(where, dw, got)


def _chip_sum(pair, others, where, *, name):
    _, hr, C = pair.shape
    tr = _div(hr, 256, 2 * SUBLANES)
    nr = hr // tr

    def body(s_ref, p_ref, o_ref, g_ref):
        g_ref[...] = p_ref[0].astype(f32) + o_ref[0].astype(f32) + o_ref[1].astype(f32) + o_ref[2].astype(f32)

    return pl.pallas_call(
        body,
        name=name,
        grid_spec=pltpu.PrefetchScalarGridSpec(
            num_scalar_prefetch=1,
            grid=(nr,),
            in_specs=[pl.BlockSpec((1, tr, C), lambda r, s: (s[0], r, 0)), pl.BlockSpec((3, tr, C), lambda r, s: (0, r, 0))],
            out_specs=pl.BlockSpec((tr, C), lambda r, s: (s[1] * nr + r, 0)),
        ),
        out_shape=jax.ShapeDtypeStruct((2 * hr, C), f32),
        compiler_params=pltpu.CompilerParams(dimension_semantics=("parallel",)),
    )(where, pair, others)


def _allreduce_small(v):
    rows = v.shape[0]

    def body(v_ref, o_ref, gath, send, recv):
        x, y, c = _place()
        me = 4 * x + 2 * y + c
        gath[pl.ds(me, 1)] = v_ref[...][None]
        cps = []
        for k in range(1, N_DEV):
            peer = (1 - x if k & 4 else x, 1 - y if k & 2 else y, 1 - c if k & 1 else c)
            cp = _remote(v_ref, gath.at[me], send.at[k - 1], recv.at[k - 1], peer)
            cp.start()
            cps.append(cp)
        for cp in cps:
            cp.wait_recv()
        for cp in cps:
            cp.wait_send()
        acc = gath[0]
        for d in range(1, N_DEV):
            acc = acc + gath[d]
        o_ref[...] = acc

    return pl.pallas_call(
        body,
        name="small_allreduce",
        in_specs=[_vmem()],
        out_specs=_vmem(),
        out_shape=jax.ShapeDtypeStruct(v.shape, f32),
        scratch_shapes=[pltpu.VMEM((N_DEV, rows, LANES), f32), pltpu.SemaphoreType.DMA((N_DEV - 1,)), pltpu.SemaphoreType.DMA((N_DEV - 1,))],
    )(v)


def _adamw_math(w, g, m, v):
    m = ADAM_B1 * m + (1.0 - ADAM_B1) * g
    v = ADAM_B2 * v + (1.0 - ADAM_B2) * (g * g)
    m_hat = m / (1.0 - ADAM_B1 ** ADAM_STEP)
    v_hat = v / (1.0 - ADAM_B2 ** ADAM_STEP)
    delta = -ADAM_LR * (m_hat / (jnp.sqrt(v_hat) + ADAM_EPS) + ADAM_WD * w)
    return delta, m, v


def _adamw(w, g, m, v, *, name):
    R, C = w.shape
    tr = _div(R, 128, SUBLANES)

    def body(w_ref, g_ref, m_ref, v_ref, go_ref, d_ref, mo_ref, vo_ref):
        gv = g_ref[...]
        d, mn, vn = _adamw_math(w_ref[...], gv, m_ref[...], v_ref[...])
        go_ref[...] = gv
        d_ref[...] = d
        mo_ref[...] = mn
        vo_ref[...] = vn

    spec = pl.BlockSpec((tr, C), lambda r: (r, 0))
    return pl.pallas_call(
        body,
        name=name,
        grid=(R // tr,),
        in_specs=[spec] * 4,
        out_specs=[spec] * 4,
        out_shape=[jax.ShapeDtypeStruct((R, C), f32)] * 4,
        compiler_params=pltpu.CompilerParams(dimension_semantics=("parallel",)),
    )(w, g, m, v)


def _pack(arrs):
    flat = jnp.concatenate([a.reshape(-1).astype(f32) for a in arrs])
    tile = SUBLANES * LANES
    total = -(-flat.shape[0] // tile) * tile
    return jnp.pad(flat, (0, total - flat.shape[0])).reshape(total // LANES, LANES)


def _unpack(buf, shapes):
    flat = buf.reshape(-1)
    out, off = [], 0
    for s in shapes:
        size = int(np.prod(s))
        out.append(flat[off:off + size].reshape(s))
        off += size
    return out


def kernel(x, p, norm_mix, w_in, lb_logits, hg_norm, rel_bias, w_out, norm_ffn, w_up, conv_w, conv_b, w_down, norm_ple, w_ple_gate, w_ple_proj, final_norm, loss_target, m_norm_mix, m_w_in, m_lb_logits, m_hg_norm, m_rel_bias, m_w_out, m_norm_ffn, m_w_up, m_conv_w, m_conv_b, m_w_down, m_norm_ple, m_w_ple_gate, m_w_ple_proj, m_final_norm, v_norm_mix, v_w_in, v_lb_logits, v_hg_norm, v_rel_bias, v_w_out, v_norm_ffn, v_w_up, v_conv_w, v_conv_b, v_w_down, v_norm_ple, v_w_ple_gate, v_w_ple_proj, v_final_norm):
    S, D = x.shape[1], x.shape[2]
    xv, pv, tgt = x[0], p[0, 0], loss_target[0]
    W = (w_in.shape[2] * N_CHIPS) // 7
    H = W // HEAD_DIM
    F = w_down.shape[1] * N_CHIPS
    tc = _div(F // 2, 1408, LANES)
    nj = F // tc
    jx, jy = lax.axis_index("x"), lax.axis_index("y")
    chip = 2 * jx + jy

    big = [w_in[0], w_out[0], w_up[0], w_down[0], w_ple_gate[0], w_ple_proj[0]]
    big_axes = [1, 0, 1, 0, 0, 1]
    where = jnp.stack([chip, lax.axis_index("c")]).astype(jnp.int32)
    cw_pad = jnp.pad(conv_w[0], ((0, SUBLANES - conv_w.shape[1]), (0, 0)))
    cw_mine = lax.dynamic_update_slice(jnp.zeros((N_CHIPS * SUBLANES, cw_pad.shape[1]), f32), cw_pad, (chip * SUBLANES, 0))
    placed = [_cast_place(b, ax, where, name=f"cast_place_{i}") for i, (b, ax) in enumerate(zip(big, big_axes))]
    shard_shapes = [b.shape for b in big]

    every = (1, 2, 3)

    def job(buf, i, rels, hand_over=False):
        if i is None:
            return _GatherJob(buf, cw_pad.shape, 0, False, rels, hand_over)
        return _GatherJob(buf, shard_shapes[i], big_axes[i], True, rels, hand_over)

    tm = _div(S, 1024, LANES)

    a1 = _rms_fwd(xv, norm_mix, name="rms_mix")
    cs_in = shard_shapes[0][1]
    tn_in = _div(cs_in, 1024, LANES)
    proj, W_in = _mm_while_gathering(a1, placed[0], shard_shapes[0], where, tm=tm, tn=tn_in, name="mm_in")
    y_hg, o_hg, st, W_out_i, W_up_i = _hgrn_fwd(
        proj, lb_logits, hg_norm, W, rider=_gather_rider([placed[1], placed[2]], [job(0, 1, every), job(1, 2, (1, 2))]))

    qt, pad, kb, ne = _att_dims(S)
    nrel = rel_bias.shape[2]
    nrel_pad = -(-nrel // LANES) * LANES
    onehot = _bias_onehot(qt, ne, nrel_pad)
    rb_pad = jnp.pad(rel_bias[0], ((0, 0), (0, nrel_pad - nrel)))[:, None, :]
    ext = _bias_ext(rb_pad, onehot)
    kp, vp = _kv_padded(proj, W, pad)
    y_att, W_out, W_up_i = _attn_fwd(
        proj, kp, vp, ext, W,
        rider=_gather_rider([W_out_i, W_up_i], [job(0, 1, every, True), job(1, 2, (1, 2), True), job(1, 2, (3,))]))
    ycat = jnp.concatenate([y_hg, y_att], axis=1)

    tn_d = _div(D, 512, LANES)
    h1, W_up = _mm(ycat, W_out, dims="nn", tm=tm, tn=tn_d, tk=D, out_dtype=f32, res=xv, name="mm_out",
                   rider=_gather_rider([W_up_i], [job(0, 2, (3,), True)]))
    a2 = _rms_fwd(h1, norm_ffn, name="rms_ffn")
    perm = lambda b: (b % 2) * nj + b // 2
    u, W_down_i, W_pg_i, W_pp_i, cw_all = _mm(
        a2, W_up, dims="nn", tm=tm, tn=tc, tk=D, out_dtype=f32, name="mm_up", b_idx=lambda i, j, k: (k, perm(j)),
        rider=_gather_rider([placed[3], placed[4], placed[5], cw_mine], [job(0, 3, every), job(1, 4, every), job(2, 5, every), job(3, None, every)]))
    ncw = conv_w.shape[1]
    cw_full = jnp.transpose(cw_all.reshape(N_CHIPS, SUBLANES, -1)[:, :ncw], (1, 0, 2)).reshape(ncw, -1)
    cb = conv_b
    mact, W_down, W_pg, W_pp = _convact_fwd(
        u, cw_full, cb, tc, rider=_gather_rider([W_down_i, W_pg_i, W_pp_i], [job(0, 3, every, True), job(1, 4, every, True), job(2, 5, every, True)]))
    tn_w = _div(D, 1024, LANES)
    h2 = _mm(mact, W_down, dims="nn", tm=_div(S, 512, LANES), tn=tn_w, tk=F, out_dtype=f32, res=h1, name="mm_down", cols_outer=True)
    a3 = _rms_fwd(h2, norm_ple, name="rms_ple")
    zg = _mm(a3, W_pg, dims="nn", tm=tm, tn=tn_d, tk=D, out_dtype=f32, name="mm_ple_gate")
    p16 = pv.astype(bf16)
    pp = _mm(p16, W_pp, dims="nn", tm=tm, tn=tn_d, tk=pv.shape[1], out_dtype=f32, name="mm_ple_proj")
    dh3, dzg, dpp, loss_part, d_fn = _tail(h2, zg, pp, tgt, final_norm.reshape(1, D))

    pair = {}

    def siblings(idx, grads):
        return _sibling_rider(grads, [shard_shapes[i] for i in idx], [big_axes[i] for i in idx])

    def pair_up(idx, grads, got):
        for i, d, g in zip(idx, grads, got):
            pair[i] = _pair_sum(d, g, shard_shapes[i], big_axes[i], where, name=f"grad_pair_sum_{i}")

    def reduced_half(i, landed):
        return _chip_sum(pair[i], landed, where, name=f"grad_chip_sum_{i}")

    tk_s = S
    dW_pp = _mm(p16, dpp, dims="tn", tm=pv.shape[1], tn=tn_w, tk=tk_s, out_dtype=bf16, name="mm_d_ple_proj")
    dW_pg = _mm(a3, dzg, dims="tn", tm=tn_w, tn=tn_w, tk=tk_s, out_dtype=bf16, name="mm_d_ple_gate")
    da3, *got = _mm(dzg, W_pg, dims="nt", tm=tm, tn=_div(D, 1024, LANES), tk=D, out_dtype=f32, name="mm_da3",
                    rider=siblings([4, 5], [dW_pg, dW_pp]))
    pair_up([4, 5], [dW_pg, dW_pp], got)
    dh2, dh2b, d_nple = _rms_bwd(h2, norm_ple, da3, dh3, name="rms_ple_bwd")
    dm, land4, land5 = _mm(dh2b, W_down, dims="nt", tm=tm, tn=_div(F, 512, LANES), tk=D, out_dtype=f32, name="mm_dm",
                           rider=_chips_rider([pair[4], pair[5]]))
    dW_down = _mm(mact, dh2b, dims="tn", tm=_div(F, 1408, LANES), tn=tn_d, tk=tk_s, out_dtype=bf16, name="mm_d_down")
    du, dcw_g, dcw_v, dcb_g, dcb_v, got3 = _convact_bwd(u, dm, cw_full, cb, tc, rider=siblings([3], [dW_down]))
    pair_up([3], [dW_down], [got3])
    dW_up, land3 = _mm(a2, du, dims="tn", tm=_div(D, 1024, LANES), tn=tc, tk=tk_s, out_dtype=bf16, name="mm_d_up",
                       o_idx=lambda i, j, k: (i, perm(j)), rider=_chips_rider([pair[3]]))
    da2, got2 = _mm(du, W_up, dims="nt", tm=tm, tn=D, tk=tc, out_dtype=f32, name="mm_da2",
                    b_idx=lambda i, j, k: (j, perm(k)), rider=siblings([2], [dW_up]))
    pair_up([2], [dW_up], [got2])
    dh1, dh1b, d_nffn = _rms_bwd(h1, norm_ffn, da2, dh2, name="rms_ffn_bwd")
    dycat = _mm(dh1b, W_out, dims="nt", tm=tm, tn=_div(D, 1024, LANES), tk=D, out_dtype=f32, name="mm_dycat")
    dW_out = _mm(ycat, dh1b, dims="tn", tm=tn_w, tn=tn_w, tk=tk_s, out_dtype=bf16, name="mm_d_out")
    dq_att, dk_att, dv_att, dbias, land2, got1 = _attn_bwd(
        proj, kp, vp, ext, dycat, W, rider=_both(_chips_rider([pair[2]], rels=(1, 2)), siblings([1], [dW_out])))
    pair_up([1], [dW_out], [got1])
    d_hg4, d_lbl, d_hgn, land2 = _hgrn_bwd(proj, o_hg, st, dycat, lb_logits, hg_norm, W,
                                           rider=_chips_rider([pair[2]], rels=(3,), landing=[land2]))
    d_rb = _bias_bwd(dbias, onehot)[:, 0, :nrel]
    dproj = jnp.concatenate([d_hg4, dq_att, dk_att, dv_att], axis=1)
    dW_in, land1 = _mm(a1, dproj, dims="tn", tm=tn_w, tn=_div(7 * W, 1024, LANES), tk=tk_s, out_dtype=bf16, name="mm_d_in",
                       rider=_chips_rider([pair[1]]))
    pair_up([0], [dW_in], _run_rider(siblings([0], [dW_in]), name="grads_to_sibling_0"))
    halves = [reduced_half(i, land) for i, land in ((1, land1), (2, land2), (3, land3), (4, land4), (5, land5))]
    da1, land0, *g_rest = _mm(dproj, W_in, dims="nt", tm=tm, tn=D, tk=_div(7 * W, 1792, LANES), out_dtype=f32, name="mm_da1",
                              rider=_both(_chips_rider([pair[0]]), _join_rider(halves)))
    grad_x, _, d_nmix = _rms_bwd(xv, norm_mix, da1, dh1, name="rms_mix_bwd")
    g_big = list(_run_rider(_join_rider([reduced_half(0, land0)]), name="grads_join_w_in")) + list(g_rest)

    d_cw = jnp.concatenate([dcw_g, dcw_v], axis=1)
    d_cb = jnp.concatenate([dcb_g, dcb_v], axis=1)
    small_parts = [loss_part[:, :1], d_nmix, d_lbl, d_hgn, d_rb, d_nffn, d_cw, d_cb, d_nple, d_fn]
    small_shapes = [(1, 1), (1, D), lb_logits.shape, hg_norm.shape, (H, nrel), (1, D), (ncw, 2 * F), (1, 2 * F), (1, D), (1, D)]
    red = _unpack(_allreduce_small(_pack(small_parts)), small_shapes)
    loss = red[0].reshape(())
    g_nmix, g_lbl, g_hgn, g_rb, g_nffn, g_cw_all, g_cb, g_nple, g_fn = red[1:]
    csh = conv_w.shape[2]
    g_cw = lax.dynamic_slice(g_cw_all, (0, chip * csh), (ncw, csh))
    small_g = [g_nmix, g_lbl, g_hgn, g_rb[None], g_nffn, g_cw[None], g_cb, g_nple, g_fn.reshape(D)]
    small_w = [norm_mix, lb_logits, hg_norm, rel_bias, norm_ffn, conv_w, conv_b, norm_ple, final_norm]
    small_m = [m_norm_mix, m_lb_logits, m_hg_norm, m_rel_bias, m_norm_ffn, m_conv_w, m_conv_b, m_norm_ple, m_final_norm]
    small_v = [v_norm_mix, v_lb_logits, v_hg_norm, v_rel_bias, v_norm_ffn, v_conv_w, v_conv_b, v_norm_ple, v_final_norm]
    shapes_s = [w.shape for w in small_w]
    _, sd, sm, sv = _adamw(_pack(small_w), _pack(small_g), _pack(small_m), _pack(small_v), name="adamw_small")
    small_g = [g.reshape(s) for g, s in zip(small_g, shapes_s)]
    small_d, small_nm, small_nv = _unpack(sd, shapes_s), _unpack(sm, shapes_s), _unpack(sv, shapes_s)

    big_m = [m_w_in[0], m_w_out[0], m_w_up[0], m_w_down[0], m_w_ple_gate[0], m_w_ple_proj[0]]
    big_v = [v_w_in[0], v_w_out[0], v_w_up[0], v_w_down[0], v_w_ple_gate[0], v_w_ple_proj[0]]
    big_g, big_d, big_nm, big_nv = [], [], [], []
    for i in range(6):
        g_, d_, m_, v_ = _adamw(big[i], g_big[i], big_m[i], big_v[i], name=f"adamw_{i}")
        big_g.append(g_[None])
        big_d.append(d_[None])
        big_nm.append(m_[None])
        big_nv.append(v_[None])
    g_big = big_g

    def order(sm_list, bg_list):
        s, b = sm_list, bg_list
        return [s[0], b[0], s[1], s[2], s[3], b[1], s[4], b[2], s[5], s[6], b[3], s[7], b[4], b[5], s[8]]

    return (loss, grad_x[None], *order(small_g, g_big), *order(small_d, big_d), *order(small_nm, big_nm), *order(small_nv, big_nv))
```

```python
import functools
from typing import Callable, NamedTuple

import jax
import jax.numpy as jnp
import numpy as np
from jax import lax
from jax.experimental import pallas as pl
from jax.experimental.pallas import tpu as pltpu

f32 = jnp.float32
bf16 = jnp.bfloat16

CHUNK = 64
HEAD_DIM = 128
LEFT_CHUNKS = 8
REL_CLIP = 128
EPS = 1e-6
HG_SUB = 16
HG_HEADS_PER_STEP = 8
ATT_Q_ROWS = 256
ATT_HEADS_PER_STEP = 2
ADAM_LR, ADAM_B1, ADAM_B2, ADAM_EPS, ADAM_WD, ADAM_STEP = 0.001, 0.9, 0.999, 1e-08, 0.01, 10
LANES = 128
SUBLANES = 8
N_CHIPS = 4
N_DEV = 8
MESH = pl.DeviceIdType.MESH
NEG = float(np.finfo(np.float32).min)

NN = (((1,), (0,)), ((), ()))
NT = (((1,), (1,)), ((), ()))
TN = (((0,), (0,)), ((), ()))
HIGHEST = lax.Precision.HIGHEST


def _dot(a, b, dims, precision=None):
    return lax.dot_general(a, b, dims, preferred_element_type=f32, precision=precision)


def _sigmoid(v):
    return 1.0 / (1.0 + jnp.exp(-v))


def _div(n, pref, mult):
    best = None
    d = mult
    while d <= min(n, pref):
        if n % d == 0:
            best = d
        d += mult
    return best if best is not None else n


def _hbm():
    return pl.BlockSpec(memory_space=pltpu.HBM)


def _vmem():
    return pl.BlockSpec(memory_space=pltpu.VMEM)


def _mm(a, b, *, dims, tm, tn, tk, out_dtype, name, res=None, a_idx=None, b_idx=None, o_idx=None, out_shape=None, rider=None,
        cols_outer=False):
    a_list = list(a) if isinstance(a, (list, tuple)) else [a]
    b_list = list(b) if isinstance(b, (list, tuple)) else [b]
    na, nb = len(a_list), len(b_list)
    if dims == "nn":
        assert na == 1 and nb == 1
        (M, K), (_, N) = a.shape, b.shape
        a_blk, b_blk, dn = (tm, tk), (tk, tn), NN
        a_def, b_def = (lambda i, j, k: (i, k)), (lambda i, j, k: (k, j))
    elif dims == "nt":
        assert nb == 1
        M, K, N = a_list[0].shape[0], sum(p.shape[1] for p in a_list), b.shape[0]
        a_blk, b_blk, dn = (tm, tk), (tn, tk), NT
        a_def, b_def = (lambda i, j, k: (i, k)), (lambda i, j, k: (j, k))
    else:
        assert na == 1
        (K, M), N = a.shape, sum(p.shape[1] for p in b_list)
        a_blk, b_blk, dn = (tk, tm), (tk, tn), TN
        a_def, b_def = (lambda i, j, k: (k, i)), (lambda i, j, k: (k, j))
    assert M % tm == 0 and N % tn == 0 and K % tk == 0, (name, M, N, K, tm, tn, tk)
    nk = K // tk
    has_res = res is not None
    counts = [p.shape[1] // tk for p in a_list] if na > 1 else [p.shape[1] // tn for p in b_list]
    starts = [sum(counts[:p]) for p in range(len(counts))]

    def body(*refs):
        a_refs, b_refs, rest = refs[:na], refs[na:na + nb], refs[na + nb:]
        res_ref = rest[0] if has_res else None
        o_ref = rest[1] if has_res else rest[0]
        k = pl.program_id(2)
        j = pl.program_id(0) if cols_outer else pl.program_id(1)

        def finish(acc):
            if has_res:
                acc = acc + res_ref[...]
            o_ref[...] = acc.astype(out_dtype)

        def consume(part):
            if nk == 1:
                finish(part)
                return
            acc_ref = refs[-1]

            @pl.when(k == 0)
            def _():
                acc_ref[...] = part

            @pl.when(k > 0)
            def _():
                acc_ref[...] += part

            @pl.when(k == nk - 1)
            def _():
                finish(acc_ref[...])

        if len(counts) == 1:
            consume(_dot(a_refs[0][...].astype(bf16), b_refs[0][...].astype(bf16), dn))
        else:
            at = k if na > 1 else j
            for p, (s, n) in enumerate(zip(starts, counts)):
                a_ref, b_ref = a_refs[p if na > 1 else 0], b_refs[p if nb > 1 else 0]
                pl.when((at >= s) & (at < s + n))(
                    functools.partial(lambda ar, br: consume(_dot(ar[...].astype(bf16), br[...].astype(bf16), dn)), a_ref, b_ref))

    def ordered(custom, default):
        f = custom or default
        return (lambda j, i, k: f(i, j, k)) if cols_outer else f

    def piece_map(default, p):
        s, n = starts[p], counts[p]
        if na > 1:
            return lambda i, j, k: default(i, j, jnp.clip(k - s, 0, n - 1))
        return lambda i, j, k: default(i, jnp.clip(j - s, 0, n - 1), k)

    o_map = ordered(o_idx, lambda i, j, k: (i, j))
    a_specs = [pl.BlockSpec(a_blk, ordered(a_idx, piece_map(a_def, p) if na > 1 else a_def)) for p in range(na)]
    b_specs = [pl.BlockSpec(b_blk, ordered(b_idx, piece_map(b_def, p) if nb > 1 else b_def)) for p in range(nb)]
    in_specs = a_specs + b_specs
    args = a_list + b_list
    if has_res:
        in_specs.append(pl.BlockSpec((tm, tn), o_map))
        args.append(res)
    outs = _grid_call(
        body,
        rider,
        name=name,
        grid=(N // tn, M // tm, nk) if cols_outer else (M // tm, N // tn, nk),
        in_specs=in_specs,
        out_specs=[pl.BlockSpec((tm, tn), o_map)],
        out_shape=[jax.ShapeDtypeStruct(out_shape or (M, N), out_dtype)],
        scratch_shapes=[pltpu.VMEM((tm, tn), f32)] if nk > 1 else [],
        args=args,
    )
    return outs[0] if rider is None else outs


def _rms_fwd(xv, g, *, name):
    S, D = xv.shape
    ts = _div(S, 512, SUBLANES)

    def body(x_ref, g_ref, o_ref):
        v = x_ref[...]
        r = lax.rsqrt(jnp.mean(v * v, axis=-1, keepdims=True) + EPS)
        o_ref[...] = (v * r * g_ref[...]).astype(bf16)

    return pl.pallas_call(
        body,
        name=name,
        grid=(S // ts,),
        in_specs=[pl.BlockSpec((ts, D), lambda i: (i, 0)), pl.BlockSpec((1, D), lambda i: (0, 0))],
        out_specs=pl.BlockSpec((ts, D), lambda i: (i, 0)),
        out_shape=jax.ShapeDtypeStruct((S, D), bf16),
        compiler_params=pltpu.CompilerParams(dimension_semantics=("parallel",)),
    )(xv, g)


def _rms_bwd(xv, g, dy, dres, *, name, rider=None):
    S, D = xv.shape
    ts = _div(S, 256, SUBLANES)

    def body(x_ref, g_ref, dy_ref, dres_ref, dx_ref, dxb_ref, dg_ref):
        i = pl.program_id(0)
        v = x_ref[...]
        r = lax.rsqrt(jnp.mean(v * v, axis=-1, keepdims=True) + EPS)
        vn = v * r
        d = dy_ref[...]
        part = jnp.sum(d * vn, axis=0, keepdims=True)

        @pl.when(i == 0)
        def _():
            dg_ref[...] = part

        @pl.when(i > 0)
        def _():
            dg_ref[...] += part

        t = d * g_ref[...]
        dx = dres_ref[...] + r * (t - vn * jnp.mean(t * vn, axis=-1, keepdims=True))
        dx_ref[...] = dx
        dxb_ref[...] = dx.astype(bf16)

    row = pl.BlockSpec((ts, D), lambda i: (i, 0))
    vec = pl.BlockSpec((1, D), lambda i: (0, 0))
    return _grid_call(
        body,
        rider,
        name=name,
        grid=(S // ts,),
        in_specs=[row, vec, row, row],
        out_specs=[row, row, vec],
        out_shape=[jax.ShapeDtypeStruct((S, D), f32), jax.ShapeDtypeStruct((S, D), bf16), jax.ShapeDtypeStruct((1, D), f32)],
        scratch_shapes=[],
        args=(xv, g, dy, dres),
    )


def _tail(h2, zg, pp, tgt, fn):
    S, D = h2.shape
    ts = _div(S, 256, SUBLANES)

    def body(h_ref, z_ref, p_ref, t_ref, fn_ref, dh_ref, dz_ref, dp_ref, loss_ref, dfn_ref):
        i = pl.program_id(0)
        sg = _sigmoid(z_ref[...])
        ppv = p_ref[...]
        h3 = h_ref[...] + sg * ppv
        r = lax.rsqrt(jnp.mean(h3 * h3, axis=-1, keepdims=True) + EPS)
        hn = h3 * r
        fnv = fn_ref[...]
        e = hn * fnv - t_ref[...]
        lpart = 0.5 * jnp.sum(jnp.mean(e * e, axis=-1, keepdims=True), axis=0, keepdims=True)
        dy = e * (1.0 / D)
        dpart = jnp.sum(dy * hn, axis=0, keepdims=True)

        @pl.when(i == 0)
        def _():
            loss_ref[...] = jnp.broadcast_to(lpart, loss_ref.shape)
            dfn_ref[...] = dpart

        @pl.when(i > 0)
        def _():
            loss_ref[...] += jnp.broadcast_to(lpart, loss_ref.shape)
            dfn_ref[...] += dpart

        t = dy * fnv
        dh3 = r * (t - hn * jnp.mean(t * hn, axis=-1, keepdims=True))
        dh_ref[...] = dh3
        dz_ref[...] = (dh3 * ppv * sg * (1.0 - sg)).astype(bf16)
        dp_ref[...] = (dh3 * sg).astype(bf16)

    row = pl.BlockSpec((ts, D), lambda i: (i, 0))
    vec = pl.BlockSpec((1, D), lambda i: (0, 0))
    one = pl.BlockSpec((1, LANES), lambda i: (0, 0))
    return pl.pallas_call(
        body,
        name="ple_final_loss",
        grid=(S // ts,),
        in_specs=[row, row, row, row, vec],
        out_specs=[row, row, row, one, vec],
        out_shape=[
            jax.ShapeDtypeStruct((S, D), f32),
            jax.ShapeDtypeStruct((S, D), bf16),
            jax.ShapeDtypeStruct((S, D), bf16),
            jax.ShapeDtypeStruct((1, LANES), f32),
            jax.ShapeDtypeStruct((1, D), f32),
        ],
        compiler_params=pltpu.CompilerParams(dimension_semantics=("arbitrary",)),
    )(h2, zg, pp, tgt, fn)


def _shift_down(v, halo, k):
    r = pltpu.roll(v, k, 0)
    hr = pltpu.roll(halo, k, 0)
    row = lax.broadcasted_iota(jnp.int32, hr.shape, 0)
    top = jnp.where(row < k, hr, r[0:SUBLANES])
    return top if v.shape[0] == SUBLANES else jnp.concatenate([top, r[SUBLANES:]], axis=0)


def _shift_up(v, halo, k):
    n = v.shape[0]
    r = pltpu.roll(v, n - k, 0)
    hr = pltpu.roll(halo, SUBLANES - k, 0)
    row = lax.broadcasted_iota(jnp.int32, hr.shape, 0)
    bot = jnp.where(row >= SUBLANES - k, hr, r[n - SUBLANES:])
    return jnp.concatenate([r[: n - SUBLANES], bot], axis=0)


def _conv_specs(S, F, tc, ts):
    nj = F // tc
    rows8 = ts // SUBLANES
    main = pl.BlockSpec((ts, 2 * tc), lambda j, i: (i, j))
    prev = pl.BlockSpec((SUBLANES, 2 * tc), lambda j, i: (jnp.maximum(i * rows8 - 1, 0), j))
    nxt = pl.BlockSpec((SUBLANES, 2 * tc), lambda j, i: (jnp.minimum((i + 1) * rows8, S // SUBLANES - 1), j))
    wg = pl.BlockSpec((3, tc), lambda j, i: (0, j))
    wv = pl.BlockSpec((3, tc), lambda j, i: (0, nj + j))
    bg = pl.BlockSpec((1, tc), lambda j, i: (0, j))
    bv = pl.BlockSpec((1, tc), lambda j, i: (0, nj + j))
    half = pl.BlockSpec((ts, tc), lambda j, i: (i, j))
    return nj, main, prev, nxt, wg, wv, bg, bv, half


def _conv_pre(u_ref, h_ref, wg_ref, wv_ref, bg_ref, bv_ref, tc):
    i = pl.program_id(1)
    u = u_ref[...]
    halo = jnp.where(i == 0, 0.0, h_ref[...])
    u1 = _shift_down(u, halo, 1)
    u2 = _shift_down(u, halo, 2)
    w = jnp.concatenate([wg_ref[...], wv_ref[...]], axis=1)
    b = jnp.concatenate([bg_ref[...], bv_ref[...]], axis=1)
    uc = b + w[0:1] * u2 + w[1:2] * u1 + w[2:3] * u
    return u, u1, u2, uc[:, :tc], uc[:, tc:]


def _convact_fwd(u, cw, cb, tc, rider=None):
    S, F2 = u.shape
    F = F2 // 2
    ts = _div(S, 512, SUBLANES)
    nj, main, prev, nxt, wg, wv, bg, bv, half = _conv_specs(S, F, tc, ts)

    def body(u_ref, h_ref, wg_ref, wv_ref, bg_ref, bv_ref, m_ref):
        _, _, _, g, v = _conv_pre(u_ref, h_ref, wg_ref, wv_ref, bg_ref, bv_ref, tc)
        m_ref[...] = (g * _sigmoid(g) * v).astype(bf16)

    outs = _grid_call(
        body,
        rider,
        name="convact_fwd",
        grid=(nj, S // ts),
        in_specs=[main, prev, wg, wv, bg, bv],
        out_specs=[half],
        out_shape=[jax.ShapeDtypeStruct((S, F), bf16)],
        scratch_shapes=[],
        args=(u, u, cw, cw, cb, cb),
    )
    return outs[0] if rider is None else outs


def _convact_bwd(u, dm, cw, cb, tc, rider=None):
    S, F2 = u.shape
    F = F2 // 2
    ts = _div(S, 512, SUBLANES)
    nj, main, prev, nxt, wg, wv, bg, bv, half = _conv_specs(S, F, tc, ts)
    ni = S // ts
    half_nxt = pl.BlockSpec((SUBLANES, tc), lambda j, i: (jnp.minimum((i + 1) * (ts // SUBLANES), S // SUBLANES - 1), j))

    def act_bwd(g, v, d):
        sg = _sigmoid(g)
        return jnp.concatenate([d * v * sg * (1.0 + g * (1.0 - sg)), d * g * sg], axis=1)

    def body(u_ref, h_ref, n_ref, dm_ref, dmn_ref, wg_ref, wv_ref, bg_ref, bv_ref, du_ref, dwg_ref, dwv_ref, dbg_ref, dbv_ref):
        i = pl.program_id(1)
        u0, u1, u2, g, v = _conv_pre(u_ref, h_ref, wg_ref, wv_ref, bg_ref, bv_ref, tc)
        duc = act_bwd(g, v, dm_ref[...])
        w = jnp.concatenate([wg_ref[...], wv_ref[...]], axis=1)
        b = jnp.concatenate([bg_ref[...], bv_ref[...]], axis=1)
        un = n_ref[...]
        tail = u0[ts - SUBLANES:]
        ucn = b + w[0:1] * _shift_down(un, tail, 2) + w[1:2] * _shift_down(un, tail, 1) + w[2:3] * un
        ducn = jnp.where(i == ni - 1, 0.0, act_bwd(ucn[:, :tc], ucn[:, tc:], dmn_ref[...]))
        du_ref[...] = (w[2:3] * duc + w[1:2] * _shift_up(duc, ducn, 1) + w[0:1] * _shift_up(duc, ducn, 2)).astype(bf16)
        db = jnp.sum(duc, axis=0, keepdims=True)
        dw = jnp.concatenate(
            [jnp.sum(duc * u2, axis=0, keepdims=True), jnp.sum(duc * u1, axis=0, keepdims=True), jnp.sum(duc * u0, axis=0, keepdims=True)],
            axis=0,
        )

        @pl.when(i == 0)
        def _():
            dwg_ref[...] = dw[:, :tc]
            dwv_ref[...] = dw[:, tc:]
            dbg_ref[...] = db[:, :tc]
            dbv_ref[...] = db[:, tc:]

        @pl.when(i > 0)
        def _():
            dwg_ref[...] += dw[:, :tc]
            dwv_ref[...] += dw[:, tc:]
            dbg_ref[...] += db[:, :tc]
            dbv_ref[...] += db[:, tc:]

    w_out = pl.BlockSpec((3, tc), lambda j, i: (0, j))
    b_out = pl.BlockSpec((1, tc), lambda j, i: (0, j))
    return _grid_call(
        body,
        rider,
        name="convact_bwd",
        grid=(nj, ni),
        in_specs=[main, prev, nxt, half, half_nxt, wg, wv, bg, bv],
        out_specs=[main, w_out, w_out, b_out, b_out],
        out_shape=[
            jax.ShapeDtypeStruct((S, F2), bf16),
            jax.ShapeDtypeStruct((3, F), f32),
            jax.ShapeDtypeStruct((3, F), f32),
            jax.ShapeDtypeStruct((1, F), f32),
            jax.ShapeDtypeStruct((1, F), f32),
        ],
        scratch_shapes=[],
        args=(u, u, u, dm, dm, cw, cw, cb, cb),
    )


def _hg_gates(qp, fp, lbl):
    lb = _sigmoid(lbl[0:1] - lbl[1:2])
    sig = _sigmoid(fp)
    sigm = _sigmoid(-fp)
    f = lb + (1.0 - lb) * sig
    k = (1.0 - lb) * sigm
    sq = _sigmoid(qp)
    qf = qp * sq
    row = lax.broadcasted_iota(jnp.int32, (CHUNK, CHUNK), 0)
    col = lax.broadcasted_iota(jnp.int32, (CHUNK, CHUNK), 1)
    b = _dot((row >= col).astype(f32), jnp.log(f), NN, HIGHEST)
    return lb, sig, sigm, f, k, sq, qf, b


def _hg_block(qf, k, b, I):
    r0, n = I * HG_SUB, (I + 1) * HG_SUB
    base = b[r0 - 1:r0] if I > 0 else jnp.zeros_like(b[0:1])
    eq = jnp.exp(b[r0:n] - base)
    ek = jnp.exp(base - b[0:n])
    qt16 = (qf[r0:n] * eq).astype(bf16)
    kt16 = (k[0:n] * ek).astype(bf16)
    row = lax.broadcasted_iota(jnp.int32, (HG_SUB, n), 0) + r0
    col = lax.broadcasted_iota(jnp.int32, (HG_SUB, n), 1)
    return r0, n, eq, ek, qt16, kt16, col <= row


def _per_head(v, hp, fn):
    return jnp.concatenate([fn(v[:, p * HEAD_DIM:(p + 1) * HEAD_DIM]) for p in range(hp)], axis=1)


def _head_mean(v, hp):
    return _per_head(v, hp, lambda t: jnp.broadcast_to(jnp.mean(t, axis=-1, keepdims=True), t.shape))


def _pad_rows(v, rows):
    return v if v.shape[0] == rows else jnp.concatenate([v, jnp.zeros((rows - v.shape[0], v.shape[1]), v.dtype)], axis=0)


def _hgrn_specs(S, W, hp, reverse):
    nc = S // CHUNK
    ngrp = W // (hp * HEAD_DIM)
    cw = hp * HEAD_DIM
    cidx = (lambda c: nc - 1 - c) if reverse else (lambda c: c)

    def proj(off):
        return pl.BlockSpec((CHUNK, cw), lambda h, c: (cidx(c), off * ngrp + h))

    act = pl.BlockSpec((CHUNK, cw), lambda h, c: (cidx(c), h))
    lbl = pl.BlockSpec((2, cw), lambda h, c: (0, h))
    gn = pl.BlockSpec((1, HEAD_DIM), lambda h, c: (0, 0))
    st = pl.BlockSpec((hp, 1, HEAD_DIM, HEAD_DIM), lambda h, c: (h, cidx(c), 0, 0))
    return nc, ngrp, proj, act, lbl, gn, st


def _grid_call(body, rider, *, name, grid, in_specs, out_specs, out_shape, scratch_shapes, args):
    params = pltpu.CompilerParams(dimension_semantics=("arbitrary",) * len(grid))
    if rider is None:
        return pl.pallas_call(body, name=name, grid=grid, in_specs=in_specs, out_specs=out_specs, out_shape=out_shape,
                              scratch_shapes=scratch_shapes, compiler_params=params)(*args)
    n_in, n_out, n_scr = len(in_specs), len(out_specs), len(scratch_shapes)
    ri, ro = len(rider.ins), len(rider.out_shapes)

    def wrapped(*refs):
        a, b = n_in, n_in + ri
        c, d = b + n_out, b + n_out + ro
        e = d + n_scr
        ids = [pl.program_id(t) for t in range(len(grid))]
        is_first = functools.reduce(jnp.logical_and, [i == 0 for i in ids])
        is_last = functools.reduce(jnp.logical_and, [i == g - 1 for i, g in zip(ids, grid)])

        @pl.when(is_first)
        def _():
            rider.first(refs[a:b], refs[c:d], refs[e:])

        body(*refs[:a], *refs[b:c], *refs[d:e])

        @pl.when(is_last)
        def _():
            rider.last(refs[a:b], refs[c:d], refs[e:])

    return pl.pallas_call(
        wrapped,
        name=name,
        grid=grid,
        in_specs=list(in_specs) + [_hbm()] * ri,
        out_specs=list(out_specs) + [_hbm()] * ro,
        out_shape=list(out_shape) + list(rider.out_shapes),
        input_output_aliases={n_in + i: n_out + o for i, o in rider.aliases.items()},
        scratch_shapes=list(scratch_shapes) + list(rider.sems),
        compiler_params=params,
    )(*args, *rider.ins)


def _hgrn_fwd(proj, lb_logits, hg_norm, W, rider=None):
    S = proj.shape[0]
    hp = min(HG_HEADS_PER_STEP, W // HEAD_DIM)
    nc, ngrp, pspec, act, lbl_spec, gn_spec, st_spec = _hgrn_specs(S, W, hp, False)
    nb = CHUNK // HG_SUB

    def body(q_ref, f_ref, i_ref, g_ref, lbl_ref, gn_ref, y_ref, o_ref, st_ref, state):
        c = pl.program_id(1)

        @pl.when(c == 0)
        def _():
            state[...] = jnp.zeros_like(state)

        _, _, _, _, k, _, qf, b = _hg_gates(q_ref[...], f_ref[...], lbl_ref[...])
        v16 = i_ref[...].astype(bf16)
        a16 = (qf * jnp.exp(b)).astype(bf16)
        bl = b[CHUNK - 1:CHUNK]
        kd16 = (k * jnp.exp(bl - b)).astype(bf16)
        ebl = jnp.exp(bl)
        blocks = [_hg_block(qf, k, b, I) for I in range(nb)]
        heads = [slice(p * HEAD_DIM, (p + 1) * HEAD_DIM) for p in range(hp)]
        st_prev = state[...]
        st_ref[:, 0] = st_prev
        inter = [_dot(a16[:, sl], st_prev[p].astype(bf16), NT) for p, sl in enumerate(heads)]
        scores = [[_dot(qt16[:, sl], kt16[:, sl], NT) for sl in heads] for _, _, _, _, qt16, kt16, _ in blocks]
        intra = [[_dot(jnp.where(blk[6], sc, 0.0).astype(bf16), v16[0:blk[1], sl], NN) for sc, sl in zip(scs, heads)]
                 for blk, scs in zip(blocks, scores)]
        st_new = [st_prev[p] * ebl[:, sl] + _dot(v16[:, sl], kd16[:, sl], TN) for p, sl in enumerate(heads)]
        state[...] = jnp.stack(st_new)
        o = jnp.concatenate([inter[p] + jnp.concatenate([rows[p] for rows in intra], axis=0) for p in range(hp)], axis=1)
        o_ref[...] = o
        r = lax.rsqrt(_head_mean(o * o, hp) + EPS)
        gp = g_ref[...]
        y_ref[...] = (o * r * jnp.tile(gn_ref[...], (1, hp)) * (gp * _sigmoid(gp))).astype(bf16)

    H = W // HEAD_DIM
    return _grid_call(
        body,
        rider,
        name="hgrn_fwd",
        grid=(ngrp, nc),
        in_specs=[pspec(0), pspec(1), pspec(2), pspec(3), lbl_spec, gn_spec],
        out_specs=[act, act, st_spec],
        out_shape=[
            jax.ShapeDtypeStruct((S, W), bf16),
            jax.ShapeDtypeStruct((S, W), f32),
            jax.ShapeDtypeStruct((H, nc, HEAD_DIM, HEAD_DIM), f32),
        ],
        scratch_shapes=[pltpu.VMEM((hp, HEAD_DIM, HEAD_DIM), f32)],
        args=(proj, proj, proj, proj, lb_logits, hg_norm),
    )


def _hgrn_bwd(proj, o_hg, st, dycat, lb_logits, hg_norm, W, rider=None):
    S = proj.shape[0]
    hp = min(HG_HEADS_PER_STEP, W // HEAD_DIM)
    assert hp * HEAD_DIM == W, "the four gradients share one output block: all heads in one grid step"
    nc, ngrp, pspec, act, lbl_spec, gn_spec, st_spec = _hgrn_specs(S, W, hp, True)
    nb = CHUNK // HG_SUB

    def body(q_ref, f_ref, i_ref, g_ref, o_ref, st_ref, dy_ref, lbl_ref, gn_ref,
             d4_ref, dlbl_ref, dgn_ref, dstate):
        cw_ = hp * HEAD_DIM
        dq_ref, df_ref, di_ref, dg_ref = (d4_ref.at[:, pl.ds(t * cw_, cw_)] for t in range(4))
        h = pl.program_id(0)
        c = pl.program_id(1)

        @pl.when(c == 0)
        def _():
            dstate[...] = jnp.zeros_like(dstate)
            dlbl_ref[...] = jnp.zeros_like(dlbl_ref)

        @pl.when((c == 0) & (h == 0))
        def _():
            dgn_ref[...] = jnp.zeros_like(dgn_ref)

        row = lax.broadcasted_iota(jnp.int32, (CHUNK, CHUNK), 0)
        col = lax.broadcasted_iota(jnp.int32, (CHUNK, CHUNK), 1)
        upper = (col >= row).astype(f32)
        last_row = lax.broadcasted_iota(jnp.int32, (CHUNK, hp * HEAD_DIM), 0) == CHUNK - 1

        qp, fp = q_ref[...], f_ref[...]
        lb, sig, sigm, f, k, sq, qf, b = _hg_gates(qp, fp, lbl_ref[...])
        v16 = i_ref[...].astype(bf16)
        gnv = jnp.tile(gn_ref[...], (1, hp))
        gp = g_ref[...]
        sgg = _sigmoid(gp)
        gate = gp * sgg
        o = o_ref[...]
        r = lax.rsqrt(_head_mean(o * o, hp) + EPS)
        on = o * r
        dy = dy_ref[...]
        dg_ref[...] = (dy * on * gnv * (sgg * (1.0 + gp * (1.0 - sgg)))).astype(bf16)
        dgn_wide = jnp.sum(dy * on * gate, axis=0, keepdims=True)
        dgn_ref[...] += functools.reduce(jnp.add, [dgn_wide[:, p * HEAD_DIM:(p + 1) * HEAD_DIM] for p in range(hp)])
        don = dy * gnv * gate
        do16 = (r * (don - on * _head_mean(don * on, hp))).astype(bf16)
        eb = jnp.exp(b)
        A = qf * eb
        a16 = A.astype(bf16)
        bl = b[CHUNK - 1:CHUNK]
        ebl = jnp.exp(bl)
        ekd = jnp.exp(bl - b)
        kd = k * ekd
        kd16 = kd.astype(bf16)
        blocks = [_hg_block(qf, k, b, I) for I in range(nb)]

        heads = [slice(p * HEAD_DIM, (p + 1) * HEAD_DIM) for p in range(hp)]
        st_prev = st_ref[:, 0]
        dst_new = dstate[...]
        st16, dst16 = st_prev.astype(bf16), dst_new.astype(bf16)
        dA_h = [_dot(do16[:, sl], st16[p], NN) for p, sl in enumerate(heads)]
        dkd_h = [_dot(v16[:, sl], dst16[p], NN) for p, sl in enumerate(heads)]
        dv_h = [_dot(kd16[:, sl], dst16[p], NT) for p, sl in enumerate(heads)]
        dstate[...] = jnp.stack([dst_new[p] * ebl[:, sl] + _dot(do16[:, sl], a16[:, sl], TN) for p, sl in enumerate(heads)])
        dbl_h = [jnp.sum(dst_new[p] * st_prev[p], axis=0, keepdims=True) for p in range(hp)]
        sc_h = [[jnp.where(mask, _dot(qt16[:, sl], kt16[:, sl], NT), 0.0).astype(bf16) for sl in heads]
                for _, _, _, _, qt16, kt16, mask in blocks]
        dsc_h = [[jnp.where(mask, _dot(do16[r0:n, sl], v16[0:n, sl], NT), 0.0).astype(bf16) for sl in heads]
                 for r0, n, _, _, _, _, mask in blocks]
        dvi_h = [[_dot(sc, do16[blk[0]:blk[1], sl], TN) for sc, sl in zip(scs, heads)] for blk, scs in zip(blocks, sc_h)]
        dqt_h = [[_dot(dsc, blk[5][:, sl], NN) for dsc, sl in zip(dscs, heads)] for blk, dscs in zip(blocks, dsc_h)]
        dkt_h = [[_dot(dsc, blk[4][:, sl], TN) for dsc, sl in zip(dscs, heads)] for blk, dscs in zip(blocks, dsc_h)]
        dv_h = [functools.reduce(jnp.add, [dv_h[p]] + [_pad_rows(rows[p], CHUNK) for rows in dvi_h]) for p in range(hp)]

        dA, dkd = jnp.concatenate(dA_h, axis=1), jnp.concatenate(dkd_h, axis=1)
        dbl = jnp.concatenate(dbl_h, axis=1) * ebl + jnp.sum(dkd * kd, axis=0, keepdims=True)
        db = dA * A - dkd * kd + jnp.where(last_row, dbl, 0.0)
        dk = dkd * ekd
        dq_rows, db_rows = [], []
        for I, (r0, n, eq, ek, qt16, kt16, _) in enumerate(blocks):
            dqt, dkt = jnp.concatenate(dqt_h[I], axis=1), jnp.concatenate(dkt_h[I], axis=1)
            dq_rows.append(dqt * eq)
            dk = dk + _pad_rows(dkt * ek, CHUNK)
            db_rows.append(dqt * qt16.astype(f32))
            db = db - _pad_rows(dkt * kt16.astype(f32), CHUNK)
        dqf = dA * eb + jnp.concatenate(dq_rows, axis=0)
        db = db + jnp.concatenate(db_rows, axis=0)
        dlogf = _dot(upper, db, NN, HIGHEST)
        dfg = dlogf / f
        df_ref[...] = ((1.0 - lb) * sig * sigm * (dfg - dk)).astype(bf16)
        dlb = jnp.sum(sigm * (dfg - dk), axis=0, keepdims=True)
        dl0 = dlb * lb * (1.0 - lb)
        dlbl_ref[...] += jnp.concatenate([dl0, -dl0], axis=0)
        dq_ref[...] = (dqf * (sq * (1.0 + qp * (1.0 - sq)))).astype(bf16)
        di_ref[...] = jnp.concatenate(dv_h, axis=1).astype(bf16)

    cw = hp * HEAD_DIM
    dy_spec = pl.BlockSpec((CHUNK, cw), lambda h, c: (nc - 1 - c, h))
    return _grid_call(
        body,
        rider,
        name="hgrn_bwd",
        grid=(ngrp, nc),
        in_specs=[pspec(0), pspec(1), pspec(2), pspec(3), act, st_spec, dy_spec, lbl_spec, gn_spec],
        out_specs=[pl.BlockSpec((CHUNK, 4 * W), lambda h, c: (nc - 1 - c, 0)), lbl_spec, gn_spec],
        out_shape=[jax.ShapeDtypeStruct((S, 4 * W), bf16), jax.ShapeDtypeStruct((2, W), f32), jax.ShapeDtypeStruct((1, HEAD_DIM), f32)],
        scratch_shapes=[pltpu.VMEM((hp, HEAD_DIM, HEAD_DIM), f32)],
        args=(proj, proj, proj, proj, o_hg, st, dycat, lb_logits, hg_norm),
    )


def _att_dims(S):
    qt = ATT_Q_ROWS if S % ATT_Q_ROWS == 0 else CHUNK
    pad = LEFT_CHUNKS * CHUNK
    kb = pad + qt
    return qt, pad, kb, kb + qt


def _kv_padded(proj, W, pad):
    S = proj.shape[0]
    ts = _div(pad, 512, 2 * SUBLANES)
    assert S % ts == 0 and pad % ts == 0
    npad = pad // ts

    def body(k_ref, v_ref, ko_ref, vo_ref):
        i = pl.program_id(0)

        @pl.when(i < npad)
        def _():
            ko_ref[...] = jnp.zeros_like(ko_ref)
            vo_ref[...] = jnp.zeros_like(vo_ref)

        @pl.when(i >= npad)
        def _():
            ko_ref[...] = k_ref[...].astype(bf16)
            vo_ref[...] = v_ref[...].astype(bf16)

    out = pl.BlockSpec((ts, W), lambda i: (i, 0))
    return pl.pallas_call(
        body,
        name="kv_padded",
        grid=((S + pad) // ts,),
        in_specs=[pl.BlockSpec((ts, W), lambda i: (jnp.maximum(i - npad, 0), 5)), pl.BlockSpec((ts, W), lambda i: (jnp.maximum(i - npad, 0), 6))],
        out_specs=[out, out],
        out_shape=[jax.ShapeDtypeStruct((S + pad, W), bf16)] * 2,
        compiler_params=pltpu.CompilerParams(dimension_semantics=("parallel",)),
    )(proj, proj)


def _att_probs(qk, bias, start, qt, pad, kb):
    s = qk * (HEAD_DIM ** -0.5) + bias
    row = lax.broadcasted_iota(jnp.int32, (qt, kb), 0)
    col = lax.broadcasted_iota(jnp.int32, (qt, kb), 1)
    lo = jnp.bitwise_and(row, -CHUNK)
    ok = (col >= lo) & (col < lo + pad + CHUNK) & (col + start >= pad)
    s = jnp.where(ok, s, NEG)
    e = jnp.exp(s - jnp.max(s, axis=-1, keepdims=True))
    return e / jnp.sum(e, axis=-1, keepdims=True)


def _att_bias(ext_row, qt, kb, ne):
    e = jnp.broadcast_to(ext_row, (qt, ne))
    return pltpu.roll(e, ne - qt + 1, 1, stride=1, stride_axis=0)[:, :kb]


def _att_specs(S, H, ah, qt, pad, ne):
    cw = ah * HEAD_DIM
    q_spec = pl.BlockSpec((qt, cw), lambda h, g: (g, 4 * H // ah + h))
    kv_spec = pl.BlockSpec((S + pad, cw), lambda h, g: (0, h))
    ext_spec = pl.BlockSpec((ah, 1, ne), lambda h, g: (h, 0, 0))
    row_spec = pl.BlockSpec((qt, cw), lambda h, g: (g, h))
    return cw, q_spec, kv_spec, ext_spec, row_spec


def _attn_fwd(proj, kp, vp, ext, W, rider=None):
    S = proj.shape[0]
    H = W // HEAD_DIM
    ah = min(ATT_HEADS_PER_STEP, H)
    qt, pad, kb, ne = _att_dims(S)
    cw, q_spec, kv_spec, ext_spec, row_spec = _att_specs(S, H, ah, qt, pad, ne)
    heads = [slice(a * HEAD_DIM, (a + 1) * HEAD_DIM) for a in range(ah)]

    def body(q_ref, k_ref, v_ref, ext_ref, o_ref, bias):
        g = pl.program_id(1)

        @pl.when(g == 0)
        def _():
            for a in range(ah):
                bias[a] = _att_bias(ext_ref[a], qt, kb, ne)

        start = pl.multiple_of(g * qt, qt)
        q16 = q_ref[...].astype(bf16)
        kb16 = k_ref[pl.ds(start, kb), :]
        vb16 = v_ref[pl.ds(start, kb), :]
        qk = [_dot(q16[:, sl], kb16[:, sl], NT) for sl in heads]
        pn = [_att_probs(qk[a], bias[a], start, qt, pad, kb).astype(bf16) for a in range(ah)]
        o_ref[...] = jnp.concatenate([_dot(pn[a], vb16[:, sl], NN) for a, sl in enumerate(heads)], axis=1).astype(bf16)

    outs = _grid_call(
        body,
        rider,
        name="attn_fwd",
        grid=(H // ah, S // qt),
        in_specs=[q_spec, kv_spec, kv_spec, ext_spec],
        out_specs=[row_spec],
        out_shape=[jax.ShapeDtypeStruct((S, W), bf16)],
        scratch_shapes=[pltpu.VMEM((ah, qt, kb), f32)],
        args=(proj, kp, vp, ext),
    )
    return outs[0] if rider is None else outs


def _attn_bwd(proj, kp, vp, ext, dycat, W, rider=None):
    S = proj.shape[0]
    H = W // HEAD_DIM
    ah = min(ATT_HEADS_PER_STEP, H)
    qt, pad, kb, ne = _att_dims(S)
    cw, q_spec, kv_spec, ext_spec, row_spec = _att_specs(S, H, ah, qt, pad, ne)
    heads = [slice(a * HEAD_DIM, (a + 1) * HEAD_DIM) for a in range(ah)]
    scale = HEAD_DIM ** -0.5
    ng = S // qt

    def body(q_ref, k_ref, v_ref, ext_ref, do_ref, dq_ref, dko_ref, dvo_ref, db_ref, bias, dk_ref, dv_ref):
        g = pl.program_id(1)

        @pl.when(g == 0)
        def _():
            for a in range(ah):
                bias[a] = _att_bias(ext_ref[a], qt, kb, ne)
            dk_ref[...] = jnp.zeros_like(dk_ref)
            dv_ref[...] = jnp.zeros_like(dv_ref)
            db_ref[...] = jnp.zeros_like(db_ref)

        start = pl.multiple_of(g * qt, qt)
        q16 = q_ref[...].astype(bf16)
        kb16 = k_ref[pl.ds(start, kb), :]
        vb16 = v_ref[pl.ds(start, kb), :]
        do16 = do_ref[...].astype(bf16)
        qk = [_dot(q16[:, sl], kb16[:, sl], NT) for sl in heads]
        dpn = [_dot(do16[:, sl], vb16[:, sl], NT) for sl in heads]
        pn = [_att_probs(qk[a], bias[a], start, qt, pad, kb) for a in range(ah)]
        ds = [pn[a] * (dpn[a] - jnp.sum(dpn[a] * pn[a], axis=-1, keepdims=True)) for a in range(ah)]
        ds16 = [d.astype(bf16) for d in ds]
        dv_ref[pl.ds(start, kb), :] += jnp.concatenate([_dot(pn[a].astype(bf16), do16[:, sl], TN) for a, sl in enumerate(heads)], axis=1)
        dq_ref[...] = (jnp.concatenate([_dot(ds16[a], kb16[:, sl], NN) for a, sl in enumerate(heads)], axis=1) * scale).astype(bf16)
        dk_ref[pl.ds(start, kb), :] += jnp.concatenate([_dot(ds16[a], q16[:, sl], TN) for a, sl in enumerate(heads)], axis=1) * scale
        for a in range(ah):
            db_ref[a] += ds[a]

        @pl.when(g == ng - 1)
        def _():
            dko_ref[...] = dk_ref[pl.ds(pad, S), :].astype(bf16)
            dvo_ref[...] = dv_ref[pl.ds(pad, S), :].astype(bf16)

    kv_out = pl.BlockSpec((S, cw), lambda h, g: (0, h))
    return _grid_call(
        body,
        rider,
        name="attn_bwd",
        grid=(H // ah, S // qt),
        in_specs=[q_spec, kv_spec, kv_spec, ext_spec,
                  pl.BlockSpec((qt, cw), lambda h, g: (g, H // ah + h))],
        out_specs=[row_spec, kv_out, kv_out, pl.BlockSpec((ah, qt, kb), lambda h, g: (h, 0, 0))],
        out_shape=[
            jax.ShapeDtypeStruct((S, W), bf16),
            jax.ShapeDtypeStruct((S, W), bf16),
            jax.ShapeDtypeStruct((S, W), bf16),
            jax.ShapeDtypeStruct((H, qt, kb), f32),
        ],
        scratch_shapes=[pltpu.VMEM((ah, qt, kb), f32), pltpu.VMEM((S + pad, cw), f32), pltpu.VMEM((S + pad, cw), f32)],
        args=(proj, kp, vp, ext, dycat),
    )


def _bias_onehot(qt, ne, nrel_pad):
    m = lax.broadcasted_iota(jnp.int32, (nrel_pad, ne), 1)
    r = lax.broadcasted_iota(jnp.int32, (nrel_pad, ne), 0)
    rel = LEFT_CHUNKS * CHUNK + qt - 1 - m
    hot = (r == jnp.clip(rel, -REL_CLIP, REL_CLIP) + REL_CLIP) & (m < ne - 1)
    return hot.astype(f32)


def _bias_ext(rel_bias_pad, onehot):
    H, _, nr = rel_bias_pad.shape
    ne = onehot.shape[1]

    def body(rb_ref, oh_ref, o_ref):
        o_ref[0] = _dot(rb_ref[0], oh_ref[...], NN, HIGHEST)

    return pl.pallas_call(
        body,
        name="bias_ext",
        grid=(H,),
        in_specs=[pl.BlockSpec((1, 1, nr), lambda h: (h, 0, 0)), pl.BlockSpec((nr, ne), lambda h: (0, 0))],
        out_specs=pl.BlockSpec((1, 1, ne), lambda h: (h, 0, 0)),
        out_shape=jax.ShapeDtypeStruct((H, 1, ne), f32),
        compiler_params=pltpu.CompilerParams(dimension_semantics=("parallel",)),
    )(rel_bias_pad, onehot)


def _bias_bwd(dbias, onehot):
    H, qt, kb = dbias.shape
    nr, ne = onehot.shape

    def body(d_ref, oh_ref, o_ref):
        row = lax.broadcasted_iota(jnp.int32, (qt, qt), 0)
        col = lax.broadcasted_iota(jnp.int32, (qt, qt), 1)
        flip = (row + col == qt - 1).astype(f32)
        x = _dot(flip, d_ref[0], NN, HIGHEST)
        x = jnp.concatenate([x, jnp.zeros((qt, ne - kb), f32)], axis=1)
        de = jnp.sum(pltpu.roll(x, 0, 1, stride=1, stride_axis=0), axis=0, keepdims=True)
        o_ref[0] = _dot(de, oh_ref[...], NT, HIGHEST)

    return pl.pallas_call(
        body,
        name="bias_bwd",
        grid=(H,),
        in_specs=[pl.BlockSpec((1, qt, kb), lambda h: (h, 0, 0)), pl.BlockSpec((nr, ne), lambda h: (0, 0))],
        out_specs=pl.BlockSpec((1, 1, nr), lambda h: (h, 0, 0)),
        out_shape=jax.ShapeDtypeStruct((H, 1, nr), f32),
        compiler_params=pltpu.CompilerParams(dimension_semantics=("parallel",)),
    )(dbias, onehot)


def _place():
    x, y, c = lax.axis_index("x"), lax.axis_index("y"), lax.axis_index("c")
    return x, y, c


def _flip(x, y, k):
    return (1 - x if k & 2 else x), (1 - y if k & 1 else y)


def _region(ref, shard_shape, axis, j, half):
    R, C = shard_shape
    if axis == 0:
        if half is None:
            return ref.at[pl.ds(j * R, R), :]
        return ref.at[pl.ds(j * R + half * (R // 2), R // 2), :]
    if half is None:
        return ref.at[:, pl.ds(j * C, C)]
    return ref.at[pl.ds(half * (R // 2), R // 2), pl.ds(j * C, C)]


def _remote(src, dst, send_sem, recv_sem, dev):
    return pltpu.make_async_remote_copy(src_ref=src, dst_ref=dst, send_sem=send_sem, recv_sem=recv_sem,
                                        device_id=dev, device_id_type=MESH)


def _cast_place(shard, axis, where, *, name):
    R, C = shard.shape
    tr = _div(R, 256, 2 * SUBLANES)
    nr = R // tr
    full = (N_CHIPS * R, C) if axis == 0 else (R, N_CHIPS * C)
    omap = (lambda r, s: (s[0] * nr + r, 0)) if axis == 0 else (lambda r, s: (r, s[0]))

    def body(s_ref, x_ref, o_ref):
        o_ref[...] = x_ref[...].astype(bf16)

    return pl.pallas_call(
        body,
        name=name,
        grid_spec=pltpu.PrefetchScalarGridSpec(
            num_scalar_prefetch=1,
            grid=(nr,),
            in_specs=[pl.BlockSpec((tr, C), lambda r, s: (r, 0))],
            out_specs=pl.BlockSpec((tr, C), omap),
        ),
        out_shape=jax.ShapeDtypeStruct(full, bf16),
        compiler_params=pltpu.CompilerParams(dimension_semantics=("parallel",)),
    )(where, shard)


class _Rider(NamedTuple):
    ins: list
    out_shapes: list
    aliases: dict
    sems: list
    first: Callable
    last: Callable


def _run_rider(rider, *, name):
    ni, no = len(rider.ins), len(rider.out_shapes)

    def body(*refs):
        ins, outs, sems = refs[:ni], refs[ni:ni + no], refs[ni + no:]
        rider.first(ins, outs, sems)
        rider.last(ins, outs, sems)

    return pl.pallas_call(
        body,
        name=name,
        in_specs=[_hbm()] * ni,
        out_specs=[_hbm()] * no,
        out_shape=rider.out_shapes,
        input_output_aliases=rider.aliases,
        scratch_shapes=rider.sems,
    )(*rider.ins)


def _start_all(copies):
    def first(ins, outs, sems):
        for cp in copies(ins, outs, sems):
            cp.start()

    def last(ins, outs, sems):
        cps = copies(ins, outs, sems)
        for cp in cps:
            cp.wait_recv()
        for cp in cps:
            cp.wait_send()

    return first, last


def _mm_while_gathering(a, full, shard_shape, where, *, tm, tn, name):
    S, K = a.shape
    R, Cs = shard_shape
    N = full.shape[1]
    tps, nj, ni = Cs // tn, N // tn, S // tm
    assert R == K and Cs % tn == 0 and S % tm == 0

    def block(j, chip):
        return jnp.bitwise_xor(chip, j // tps) * tps + j % tps

    def body(s_ref, a_ref, w_in, o_ref, w_ref, buf, fsem, send1, recv1, send2, recv2):
        j, i = pl.program_id(0), pl.program_id(1)
        x, y, c = _place()
        chip = 2 * x + y

        def ici(k):
            px, py = _flip(x, y, k)
            mine = _region(w_ref, shard_shape, 1, chip, c)
            return _remote(mine, mine, send1.at[k - 1], recv1.at[k - 1], (px, py, c))

        def handed(k, half):
            px, py = _flip(x, y, k)
            land = _region(w_ref, shard_shape, 1, 2 * px + py, half)
            return land, _remote(land, land, send2.at[k - 1], recv2.at[k - 1], (x, y, 1 - c))

        def tile(jj):
            col = pl.multiple_of(block(jj, chip) * tn, LANES)
            return pltpu.make_async_copy(w_ref.at[:, pl.ds(col, tn)], buf.at[jj % 2], fsem.at[jj % 2])

        @pl.when((j == 0) & (i == 0))
        def _():
            for k in (1, 2, 3):
                ici(k).start()
            tile(j).start()

        @pl.when(i == 0)
        def _():
            tile(j).wait()

        @pl.when((i == ni - 1) & (j + 1 < nj))
        def _():
            for k in (1, 2, 3):
                @pl.when(j + 1 == k * tps)
                def _():
                    land, forward = handed(k, c)
                    _remote(land, land, send1.at[k - 1], recv1.at[k - 1], (x, y, c)).wait_recv()
                    forward.start()
                    handed(k, 1 - c)[1].wait_recv()

            tile(j + 1).start()

        o_ref[...] = _dot(a_ref[...], buf[j % 2], NN)

        @pl.when((j == nj - 1) & (i == ni - 1))
        def _():
            for k in (1, 2, 3):
                ici(k).wait_send()
                handed(k, c)[1].wait_send()

    sems = pltpu.SemaphoreType.DMA((3,))
    spec = pltpu.PrefetchScalarGridSpec(
        num_scalar_prefetch=1,
        grid=(nj, ni),
        in_specs=[pl.BlockSpec((tm, K), lambda j, i, s: (i, 0)), _hbm()],
        out_specs=[pl.BlockSpec((tm, tn), lambda j, i, s: (i, block(j, s[0]))), _hbm()],
        scratch_shapes=[pltpu.VMEM((2, K, tn), full.dtype), pltpu.SemaphoreType.DMA((2,)), sems, sems, sems, sems],
    )
    return pl.pallas_call(
        body,
        name=name,
        grid_spec=spec,
        out_shape=[jax.ShapeDtypeStruct((S, N), f32), jax.ShapeDtypeStruct(full.shape, full.dtype)],
        input_output_aliases={2: 1},
        compiler_params=pltpu.CompilerParams(dimension_semantics=("arbitrary", "arbitrary")),
    )(where, a, full)


class _GatherJob(NamedTuple):
    buf: int
    shard_shape: tuple
    axis: int
    split: bool
    rels: tuple
    hand_over: bool


def _gather_rider(bufs, jobs):
    def copies(ins, outs, sems):
        x, y, c = _place()
        cps = []
        for n, job in enumerate(jobs):
            half = c if job.split else None
            for k in job.rels:
                px, py = _flip(x, y, k)
                chip, dev = (2 * px + py, (x, y, 1 - c)) if job.hand_over else (2 * x + y, (px, py, c))
                cps.append(_remote(_region(ins[job.buf], job.shard_shape, job.axis, chip, half),
                                   _region(outs[job.buf], job.shard_shape, job.axis, chip, half),
                                   sems[0].at[3 * n + k - 1], sems[1].at[3 * n + k - 1], dev))
        return cps

    assert all(job.split or not job.hand_over for job in jobs)
    sems = pltpu.SemaphoreType.DMA((3 * len(jobs),))
    return _Rider(list(bufs), [jax.ShapeDtypeStruct(b.shape, b.dtype) for b in bufs], {m: m for m in range(len(bufs))},
                  [sems, sems], *_start_all(copies))


def _chips_rider(ts, rels=(1, 2, 3), landing=None):
    n = len(ts)

    def copies(ins, outs, sems):
        x, y, c = _place()
        cps = []
        for m in range(n):
            for k in rels:
                px, py = _flip(x, y, k)
                cps.append(_remote(ins[m].at[2 * px + py], outs[m].at[k - 1], sems[0].at[3 * m + k - 1], sems[1].at[3 * m + k - 1], (px, py, c)))
        return cps

    sems = pltpu.SemaphoreType.DMA((3 * n,))
    return _Rider(list(ts) + list(landing or []), [jax.ShapeDtypeStruct((3,) + t.shape[1:], t.dtype) for t in ts],
                  {n + m: m for m in range(n)} if landing else {}, [sems, sems], *_start_all(copies))


def _sibling_rider(dws, shard_shapes, axes):
    n = len(dws)

    def copies(ins, outs, sems):
        x, y, c = _place()
        return [_remote(_region(ins[m], shard_shapes[m], axes[m], j, 1 - c), outs[m].at[j],
                        sems[0].at[N_CHIPS * m + j], sems[1].at[N_CHIPS * m + j], (x, y, 1 - c))
                for m in range(n) for j in range(N_CHIPS)]

    sems = pltpu.SemaphoreType.DMA((N_CHIPS * n,))
    return _Rider(list(dws), [jax.ShapeDtypeStruct((N_CHIPS, s[0] // 2, s[1]), d.dtype) for s, d in zip(shard_shapes, dws)], {},
                  [sems, sems], *_start_all(copies))


def _join_rider(gs):
    n = len(gs)

    def copies(ins, outs, sems):
        x, y, c = _place()
        cps = []
        for m in range(n):
            hr = gs[m].shape[0] // 2
            cps.append(_remote(ins[m].at[pl.ds(c * hr, hr), :], outs[m].at[pl.ds(c * hr, hr), :], sems[0].at[m], sems[1].at[m], (x, y, 1 - c)))
        return cps

    sems = pltpu.SemaphoreType.DMA((n,))
    return _Rider(list(gs), [jax.ShapeDtypeStruct(g.shape, g.dtype) for g in gs], {m: m for m in range(n)}, [sems, sems], *_start_all(copies))


def _both(r1, r2):
    i1, o1, s1 = len(r1.ins), len(r1.out_shapes), len(r1.sems)
    aliases = dict(r1.aliases)
    aliases.update({i1 + i: o1 + o for i, o in r2.aliases.items()})

    def first(ins, outs, sems):
        r1.first(ins[:i1], outs[:o1], sems[:s1])
        r2.first(ins[i1:], outs[o1:], sems[s1:])

    def last(ins, outs, sems):
        r1.last(ins[:i1], outs[:o1], sems[:s1])
        r2.last(ins[i1:], outs[o1:], sems[s1:])

    return _Rider(r1.ins + r2.ins, r1.out_shapes + r2.out_shapes, aliases, r1.sems + r2.sems, first, last)


def _pair_sum(dw, got, shard_shape, axis, where, *, name):
    R, C = shard_shape
    hr = R // 2
    tr = _div(hr, 256, 2 * SUBLANES)
    nr = hr // tr
    dmap = (lambda j, r, s: ((2 * j + s[1]) * nr + r, 0)) if axis == 0 else (lambda j, r, s: (s[1] * nr + r, j))
    slot = pl.BlockSpec((1, tr, C), lambda j, r, s: (j, r, 0))

    def body(s_ref, d_ref, g_ref, o_ref):
        o_ref[0] = (d_ref[...].astype(f32) + g_ref[0].astype(f32)).astype(bf16)

    return pl.pallas_call(
        body,
        name=name,
        grid_spec=pltpu.PrefetchScalarGridSpec(
            num_scalar_prefetch=1,
            grid=(N_CHIPS, nr),
            in_specs=[pl.BlockSpec((tr, C), dmap), slot],
            out_specs=slot,
        ),
        out_shape=jax.ShapeDtypeStruct((N_CHIPS, hr, C), bf16),
        compiler_params=pltpu.CompilerParams(dimension_semantics=("parallel", "parallel")),
    )(where, dw, got)


def _chip_sum(pair, others, where, *, name):
    _, hr, C = pair.shape
    tr = _div(hr, 256, 2 * SUBLANES)
    nr = hr // tr

    def body(s_ref, p_ref, o_ref, g_ref):
        g_ref[...] = p_ref[0].astype(f32) + o_ref[0].astype(f32) + o_ref[1].astype(f32) + o_ref[2].astype(f32)

    return pl.pallas_call(
        body,
        name=name,
        grid_spec=pltpu.PrefetchScalarGridSpec(
            num_scalar_prefetch=1,
            grid=(nr,),
            in_specs=[pl.BlockSpec((1, tr, C), lambda r, s: (s[0], r, 0)), pl.BlockSpec((3, tr, C), lambda r, s: (0, r, 0))],
            out_specs=pl.BlockSpec((tr, C), lambda r, s: (s[1] * nr + r, 0)),
        ),
        out_shape=jax.ShapeDtypeStruct((2 * hr, C), f32),
        compiler_params=pltpu.CompilerParams(dimension_semantics=("parallel",)),
    )(where, pair, others)


def _allreduce_small(v):
    rows = v.shape[0]

    def body(v_ref, o_ref, gath, send, recv):
        x, y, c = _place()
        me = 4 * x + 2 * y + c
        gath[pl.ds(me, 1)] = v_ref[...][None]
        cps = []
        for k in range(1, N_DEV):
            peer = (1 - x if k & 4 else x, 1 - y if k & 2 else y, 1 - c if k & 1 else c)
            cp = _remote(v_ref, gath.at[me], send.at[k - 1], recv.at[k - 1], peer)
            cp.start()
            cps.append(cp)
        for cp in cps:
            cp.wait_recv()
        for cp in cps:
            cp.wait_send()
        acc = gath[0]
        for d in range(1, N_DEV):
            acc = acc + gath[d]
        o_ref[...] = acc

    return pl.pallas_call(
        body,
        name="small_allreduce",
        in_specs=[_vmem()],
        out_specs=_vmem(),
        out_shape=jax.ShapeDtypeStruct(v.shape, f32),
        scratch_shapes=[pltpu.VMEM((N_DEV, rows, LANES), f32), pltpu.SemaphoreType.DMA((N_DEV - 1,)), pltpu.SemaphoreType.DMA((N_DEV - 1,))],
    )(v)


def _adamw_math(w, g, m, v):
    m = ADAM_B1 * m + (1.0 - ADAM_B1) * g
    v = ADAM_B2 * v + (1.0 - ADAM_B2) * (g * g)
    m_hat = m / (1.0 - ADAM_B1 ** ADAM_STEP)
    v_hat = v / (1.0 - ADAM_B2 ** ADAM_STEP)
    delta = -ADAM_LR * (m_hat / (jnp.sqrt(v_hat) + ADAM_EPS) + ADAM_WD * w)
    return delta, m, v


def _adamw(w, g, m, v, *, name):
    R, C = w.shape
    tr = _div(R, 128, SUBLANES)

    def body(w_ref, g_ref, m_ref, v_ref, go_ref, d_ref, mo_ref, vo_ref):
        gv = g_ref[...]
        d, mn, vn = _adamw_math(w_ref[...], gv, m_ref[...], v_ref[...])
        go_ref[...] = gv
        d_ref[...] = d
        mo_ref[...] = mn
        vo_ref[...] = vn

    spec = pl.BlockSpec((tr, C), lambda r: (r, 0))
    return pl.pallas_call(
        body,
        name=name,
        grid=(R // tr,),
        in_specs=[spec] * 4,
        out_specs=[spec] * 4,
        out_shape=[jax.ShapeDtypeStruct((R, C), f32)] * 4,
        compiler_params=pltpu.CompilerParams(dimension_semantics=("parallel",)),
    )(w, g, m, v)


def _pack(arrs):
    flat = jnp.concatenate([a.reshape(-1).astype(f32) for a in arrs])
    tile = SUBLANES * LANES
    total = -(-flat.shape[0] // tile) * tile
    return jnp.pad(flat, (0, total - flat.shape[0])).reshape(total // LANES, LANES)


def _unpack(buf, shapes):
    flat = buf.reshape(-1)
    out, off = [], 0
    for s in shapes:
        size = int(np.prod(s))
        out.append(flat[off:off + size].reshape(s))
        off += size
    return out


def kernel(x, p, norm_mix, w_in, lb_logits, hg_norm, rel_bias, w_out, norm_ffn, w_up, conv_w, conv_b, w_down, norm_ple, w_ple_gate, w_ple_proj, final_norm, loss_target, m_norm_mix, m_w_in, m_lb_logits, m_hg_norm, m_rel_bias, m_w_out, m_norm_ffn, m_w_up, m_conv_w, m_conv_b, m_w_down, m_norm_ple, m_w_ple_gate, m_w_ple_proj, m_final_norm, v_norm_mix, v_w_in, v_lb_logits, v_hg_norm, v_rel_bias, v_w_out, v_norm_ffn, v_w_up, v_conv_w, v_conv_b, v_w_down, v_norm_ple, v_w_ple_gate, v_w_ple_proj, v_final_norm):
    S, D = x.shape[1], x.shape[2]
    xv, pv, tgt = x[0], p[0, 0], loss_target[0]
    W = (w_in.shape[2] * N_CHIPS) // 7
    H = W // HEAD_DIM
    F = w_down.shape[1] * N_CHIPS
    tc = _div(F // 2, 1408, LANES)
    nj = F // tc
    jx, jy = lax.axis_index("x"), lax.axis_index("y")
    chip = 2 * jx + jy

    big = [w_in[0], w_out[0], w_up[0], w_down[0], w_ple_gate[0], w_ple_proj[0]]
    big_axes = [1, 0, 1, 0, 0, 1]
    where = jnp.stack([chip, lax.axis_index("c")]).astype(jnp.int32)
    cw_pad = jnp.pad(conv_w[0], ((0, SUBLANES - conv_w.shape[1]), (0, 0)))
    cw_mine = lax.dynamic_update_slice(jnp.zeros((N_CHIPS * SUBLANES, cw_pad.shape[1]), f32), cw_pad, (chip * SUBLANES, 0))
    placed = [_cast_place(b, ax, where, name=f"cast_place_{i}") for i, (b, ax) in enumerate(zip(big, big_axes))]
    shard_shapes = [b.shape for b in big]

    every = (1, 2, 3)

    def job(buf, i, rels, hand_over=False):
        if i is None:
            return _GatherJob(buf, cw_pad.shape, 0, False, rels, hand_over)
        return _GatherJob(buf, shard_shapes[i], big_axes[i], True, rels, hand_over)

    tm = _div(S, 1024, LANES)

    a1 = _rms_fwd(xv, norm_mix, name="rms_mix")
    cs_in = shard_shapes[0][1]
    tn_in = _div(cs_in, 1024, LANES)
    proj, W_in = _mm_while_gathering(a1, placed[0], shard_shapes[0], where, tm=tm, tn=tn_in, name="mm_in")
    y_hg, o_hg, st, W_out_i, W_up_i = _hgrn_fwd(
        proj, lb_logits, hg_norm, W, rider=_gather_rider([placed[1], placed[2]], [job(0, 1, every), job(1, 2, (1, 2))]))

    qt, pad, kb, ne = _att_dims(S)
    nrel = rel_bias.shape[2]
    nrel_pad = -(-nrel // LANES) * LANES
    onehot = _bias_onehot(qt, ne, nrel_pad)
    rb_pad = jnp.pad(rel_bias[0], ((0, 0), (0, nrel_pad - nrel)))[:, None, :]
    ext = _bias_ext(rb_pad, onehot)
    kp, vp = _kv_padded(proj, W, pad)
    y_att, W_out, W_up_i = _attn_fwd(
        proj, kp, vp, ext, W,
        rider=_gather_rider([W_out_i, W_up_i], [job(0, 1, every, True), job(1, 2, (1, 2), True), job(1, 2, (3,))]))
    ycat = jnp.concatenate([y_hg, y_att], axis=1)

    tn_d = _div(D, 512, LANES)
    h1, W_up = _mm(ycat, W_out, dims="nn", tm=tm, tn=tn_d, tk=D, out_dtype=f32, res=xv, name="mm_out",
                   rider=_gather_rider([W_up_i], [job(0, 2, (3,), True)]))
    a2 = _rms_fwd(h1, norm_ffn, name="rms_ffn")
    perm = lambda b: (b % 2) * nj + b // 2
    u, W_down_i, W_pg_i, W_pp_i, cw_all = _mm(
        a2, W_up, dims="nn", tm=tm, tn=tc, tk=D, out_dtype=f32, name="mm_up", b_idx=lambda i, j, k: (k, perm(j)),
        rider=_gather_rider([placed[3], placed[4], placed[5], cw_mine], [job(0, 3, every), job(1, 4, every), job(2, 5, every), job(3, None, every)]))
    ncw = conv_w.shape[1]
    cw_full = jnp.transpose(cw_all.reshape(N_CHIPS, SUBLANES, -1)[:, :ncw], (1, 0, 2)).reshape(ncw, -1)
    cb = conv_b
    mact, W_down, W_pg, W_pp = _convact_fwd(
        u, cw_full, cb, tc, rider=_gather_rider([W_down_i, W_pg_i, W_pp_i], [job(0, 3, every, True), job(1, 4, every, True), job(2, 5, every, True)]))
    tn_w = _div(D, 1024, LANES)
    h2 = _mm(mact, W_down, dims="nn", tm=_div(S, 512, LANES), tn=tn_w, tk=F, out_dtype=f32, res=h1, name="mm_down", cols_outer=True)
    a3 = _rms_fwd(h2, norm_ple, name="rms_ple")
    zg = _mm(a3, W_pg, dims="nn", tm=tm, tn=tn_d, tk=D, out_dtype=f32, name="mm_ple_gate")
    p16 = pv.astype(bf16)
    pp = _mm(p16, W_pp, dims="nn", tm=tm, tn=tn_d, tk=pv.shape[1], out_dtype=f32, name="mm_ple_proj")
    dh3, dzg, dpp, loss_part, d_fn = _tail(h2, zg, pp, tgt, final_norm.reshape(1, D))

    pair = {}

    def siblings(idx, grads):
        return _sibling_rider(grads, [shard_shapes[i] for i in idx], [big_axes[i] for i in idx])

    def pair_up(idx, grads, got):
        for i, d, g in zip(idx, grads, got):
            pair[i] = _pair_sum(d, g, shard_shapes[i], big_axes[i], where, name=f"grad_pair_sum_{i}")

    def reduced_half(i, landed):
        return _chip_sum(pair[i], landed, where, name=f"grad_chip_sum_{i}")

    tk_s = S
    dW_pp = _mm(p16, dpp, dims="tn", tm=pv.shape[1], tn=tn_w, tk=tk_s, out_dtype=bf16, name="mm_d_ple_proj")
    dW_pg = _mm(a3, dzg, dims="tn", tm=tn_w, tn=tn_w, tk=tk_s, out_dtype=bf16, name="mm_d_ple_gate")
    da3, *got = _mm(dzg, W_pg, dims="nt", tm=tm, tn=_div(D, 1024, LANES), tk=D, out_dtype=f32, name="mm_da3",
                    rider=siblings([4, 5], [dW_pg, dW_pp]))
    pair_up([4, 5], [dW_pg, dW_pp], got)
    dh2, dh2b, d_nple = _rms_bwd(h2, norm_ple, da3, dh3, name="rms_ple_bwd")
    dm, land4, land5 = _mm(dh2b, W_down, dims="nt", tm=tm, tn=_div(F, 512, LANES), tk=D, out_dtype=f32, name="mm_dm",
                           rider=_chips_rider([pair[4], pair[5]]))
    dW_down = _mm(mact, dh2b, dims="tn", tm=_div(F, 1408, LANES), tn=tn_d, tk=tk_s, out_dtype=bf16, name="mm_d_down")
    du, dcw_g, dcw_v, dcb_g, dcb_v, got3 = _convact_bwd(u, dm, cw_full, cb, tc, rider=siblings([3], [dW_down]))
    pair_up([3], [dW_down], [got3])
    dW_up, land3 = _mm(a2, du, dims="tn", tm=_div(D, 1024, LANES), tn=tc, tk=tk_s, out_dtype=bf16, name="mm_d_up",
                       o_idx=lambda i, j, k: (i, perm(j)), rider=_chips_rider([pair[3]]))
    da2, got2 = _mm(du, W_up, dims="nt", tm=tm, tn=D, tk=tc, out_dtype=f32, name="mm_da2",
                    b_idx=lambda i, j, k: (j, perm(k)), rider=siblings([2], [dW_up]))
    pair_up([2], [dW_up], [got2])
    dh1, dh1b, d_nffn = _rms_bwd(h1, norm_ffn, da2, dh2, name="rms_ffn_bwd")
    dycat = _mm(dh1b, W_out, dims="nt", tm=tm, tn=_div(D, 1024, LANES), tk=D, out_dtype=f32, name="mm_dycat")
    dW_out = _mm(ycat, dh1b, dims="tn", tm=tn_w, tn=tn_w, tk=tk_s, out_dtype=bf16, name="mm_d_out")
    dq_att, dk_att, dv_att, dbias, land2, got1 = _attn_bwd(
        proj, kp, vp, ext, dycat, W, rider=_both(_chips_rider([pair[2]], rels=(1, 2)), siblings([1], [dW_out])))
    pair_up([1], [dW_out], [got1])
    d_hg4, d_lbl, d_hgn, land2 = _hgrn_bwd(proj, o_hg, st, dycat, lb_logits, hg_norm, W,
                                           rider=_chips_rider([pair[2]], rels=(3,), landing=[land2]))
    d_rb = _bias_bwd(dbias, onehot)[:, 0, :nrel]
    dproj = [d_hg4, dq_att, dk_att, dv_att]
    tw = _div(W, 1024, LANES)
    dW_in, land1 = _mm(a1, dproj, dims="tn", tm=tn_w, tn=_div(W, 512, LANES), tk=tk_s, out_dtype=bf16, name="mm_d_in",
                       rider=_chips_rider([pair[1]]))
    pair_up([0], [dW_in], _run_rider(siblings([0], [dW_in]), name="grads_to_sibling_0"))
    halves = [reduced_half(i, land) for i, land in ((1, land1), (2, land2), (3, land3), (4, land4), (5, land5))]
    da1, land0, *g_rest = _mm(dproj, W_in, dims="nt", tm=tm, tn=D, tk=tw, out_dtype=f32, name="mm_da1",
                              rider=_both(_chips_rider([pair[0]]), _join_rider(halves)))
    grad_x, _, d_nmix = _rms_bwd(xv, norm_mix, da1, dh1, name="rms_mix_bwd")
    g_big = list(_run_rider(_join_rider([reduced_half(0, land0)]), name="grads_join_w_in")) + list(g_rest)

    d_cw = jnp.concatenate([dcw_g, dcw_v], axis=1)
    d_cb = jnp.concatenate([dcb_g, dcb_v], axis=1)
    small_parts = [loss_part[:, :1], d_nmix, d_lbl, d_hgn, d_rb, d_nffn, d_cw, d_cb, d_nple, d_fn]
    small_shapes = [(1, 1), (1, D), lb_logits.shape, hg_norm.shape, (H, nrel), (1, D), (ncw, 2 * F), (1, 2 * F), (1, D), (1, D)]
    red = _unpack(_allreduce_small(_pack(small_parts)), small_shapes)
    loss = red[0].reshape(())
    g_nmix, g_lbl, g_hgn, g_rb, g_nffn, g_cw_all, g_cb, g_nple, g_fn = red[1:]
    csh = conv_w.shape[2]
    g_cw = lax.dynamic_slice(g_cw_all, (0, chip * csh), (ncw, csh))
    small_g = [g_nmix, g_lbl, g_hgn, g_rb[None], g_nffn, g_cw[None], g_cb, g_nple, g_fn.reshape(D)]
    small_w = [norm_mix, lb_logits, hg_norm, rel_bias, norm_ffn, conv_w, conv_b, norm_ple, final_norm]
    small_m = [m_norm_mix, m_lb_logits, m_hg_norm, m_rel_bias, m_norm_ffn, m_conv_w, m_conv_b, m_norm_ple, m_final_norm]
    small_v = [v_norm_mix, v_lb_logits, v_hg_norm, v_rel_bias, v_norm_ffn, v_conv_w, v_conv_b, v_norm_ple, v_final_norm]
    shapes_s = [w.shape for w in small_w]
    _, sd, sm, sv = _adamw(_pack(small_w), _pack(small_g), _pack(small_m), _pack(small_v), name="adamw_small")
    small_g = [g.reshape(s) for g, s in zip(small_g, shapes_s)]
    small_d, small_nm, small_nv = _unpack(sd, shapes_s), _unpack(sm, shapes_s), _unpack(sv, shapes_s)

    big_m = [m_w_in[0], m_w_out[0], m_w_up[0], m_w_down[0], m_w_ple_gate[0], m_w_ple_proj[0]]
    big_v = [v_w_in[0], v_w_out[0], v_w_up[0], v_w_down[0], v_w_ple_gate[0], v_w_ple_proj[0]]
    big_g, big_d, big_nm, big_nv = [], [], [], []
    for i in range(6):
        g_, d_, m_, v_ = _adamw(big[i], g_big[i], big_m[i], big_v[i], name=f"adamw_{i}")
        big_g.append(g_[None])
        big_d.append(d_[None])
        big_nm.append(m_[None])
        big_nv.append(v_[None])
    g_big = big_g

    def order(sm_list, bg_list):
        s, b = sm_list, bg_list
        return [s[0], b[0], s[1], s[2], s[3], b[1], s[4], b[2], s[5], s[6], b[3], s[7], b[4], b[5], s[8]]

    return (loss, grad_x[None], *order(small_g, g_big), *order(small_d, big_d), *order(small_nm, big_nm), *order(small_nv, big_nv))
```

```python
import functools
from typing import Callable, NamedTuple

import jax
import jax.numpy as jnp
import numpy as np
from jax import lax
from jax.experimental import pallas as pl
from jax.experimental.pallas import tpu as pltpu

f32 = jnp.float32
bf16 = jnp.bfloat16

CHUNK = 64
HEAD_DIM = 128
LEFT_CHUNKS = 8
REL_CLIP = 128
EPS = 1e-6
HG_SUB = 16
HG_HEADS_PER_STEP = 8
ATT_Q_ROWS = 256
ATT_HEADS_PER_STEP = 2
ADAM_LR, ADAM_B1, ADAM_B2, ADAM_EPS, ADAM_WD, ADAM_STEP = 0.001, 0.9, 0.999, 1e-08, 0.01, 10
LANES = 128
SUBLANES = 8
N_CHIPS = 4
N_DEV = 8
MESH = pl.DeviceIdType.MESH
NEG = float(np.finfo(np.float32).min)

NN = (((1,), (0,)), ((), ()))
NT = (((1,), (1,)), ((), ()))
TN = (((0,), (0,)), ((), ()))
HIGHEST = lax.Precision.HIGHEST


def _dot(a, b, dims, precision=None):
    return lax.dot_general(a, b, dims, preferred_element_type=f32, precision=precision)


def _sigmoid(v):
    return 1.0 / (1.0 + jnp.exp(-v))


def _div(n, pref, mult):
    best = None
    d = mult
    while d <= min(n, pref):
        if n % d == 0:
            best = d
        d += mult
    return best if best is not None else n


def _hbm():
    return pl.BlockSpec(memory_space=pltpu.HBM)


def _vmem():
    return pl.BlockSpec(memory_space=pltpu.VMEM)


def _mm(a, b, *, dims, tm, tn, tk, out_dtype, name, res=None, a_idx=None, b_idx=None, o_idx=None, out_shape=None, rider=None,
        cols_outer=False):
    a_list = list(a) if isinstance(a, (list, tuple)) else [a]
    b_list = list(b) if isinstance(b, (list, tuple)) else [b]
    na, nb = len(a_list), len(b_list)
    if dims == "nn":
        assert na == 1 and nb == 1
        (M, K), (_, N) = a.shape, b.shape
        a_blk, b_blk, dn = (tm, tk), (tk, tn), NN
        a_def, b_def = (lambda i, j, k: (i, k)), (lambda i, j, k: (k, j))
    elif dims == "nt":
        assert nb == 1
        M, K, N = a_list[0].shape[0], sum(p.shape[1] for p in a_list), b.shape[0]
        a_blk, b_blk, dn = (tm, tk), (tn, tk), NT
        a_def, b_def = (lambda i, j, k: (i, k)), (lambda i, j, k: (j, k))
    else:
        assert na == 1
        (K, M), N = a.shape, sum(p.shape[1] for p in b_list)
        a_blk, b_blk, dn = (tk, tm), (tk, tn), TN
        a_def, b_def = (lambda i, j, k: (k, i)), (lambda i, j, k: (k, j))
    assert M % tm == 0 and N % tn == 0 and K % tk == 0, (name, M, N, K, tm, tn, tk)
    nk = K // tk
    has_res = res is not None
    counts = [p.shape[1] // tk for p in a_list] if na > 1 else [p.shape[1] // tn for p in b_list]
    starts = [sum(counts[:p]) for p in range(len(counts))]

    def body(*refs):
        a_refs, b_refs, rest = refs[:na], refs[na:na + nb], refs[na + nb:]
        res_ref = rest[0] if has_res else None
        o_ref = rest[1] if has_res else rest[0]
        k = pl.program_id(2)
        j = pl.program_id(0) if cols_outer else pl.program_id(1)

        def finish(acc):
            if has_res:
                acc = acc + res_ref[...]
            o_ref[...] = acc.astype(out_dtype)

        def consume(part):
            if nk == 1:
                finish(part)
                return
            acc_ref = refs[-1]

            @pl.when(k == 0)
            def _():
                acc_ref[...] = part

            @pl.when(k > 0)
            def _():
                acc_ref[...] += part

            @pl.when(k == nk - 1)
            def _():
                finish(acc_ref[...])

        if len(counts) == 1:
            consume(_dot(a_refs[0][...].astype(bf16), b_refs[0][...].astype(bf16), dn))
        else:
            at = k if na > 1 else j
            for p, (s, n) in enumerate(zip(starts, counts)):
                a_ref, b_ref = a_refs[p if na > 1 else 0], b_refs[p if nb > 1 else 0]
                pl.when((at >= s) & (at < s + n))(
                    functools.partial(lambda ar, br: consume(_dot(ar[...].astype(bf16), br[...].astype(bf16), dn)), a_ref, b_ref))

    def ordered(custom, default):
        f = custom or default
        return (lambda j, i, k: f(i, j, k)) if cols_outer else f

    def piece_map(default, p):
        s, n = starts[p], counts[p]
        if na > 1:
            return lambda i, j, k: default(i, j, jnp.clip(k - s, 0, n - 1))
        return lambda i, j, k: default(i, jnp.clip(j - s, 0, n - 1), k)

    o_map = ordered(o_idx, lambda i, j, k: (i, j))
    a_specs = [pl.BlockSpec(a_blk, ordered(a_idx, piece_map(a_def, p) if na > 1 else a_def)) for p in range(na)]
    b_specs = [pl.BlockSpec(b_blk, ordered(b_idx, piece_map(b_def, p) if nb > 1 else b_def)) for p in range(nb)]
    in_specs = a_specs + b_specs
    args = a_list + b_list
    if has_res:
        in_specs.append(pl.BlockSpec((tm, tn), o_map))
        args.append(res)
    outs = _grid_call(
        body,
        rider,
        name=name,
        grid=(N // tn, M // tm, nk) if cols_outer else (M // tm, N // tn, nk),
        in_specs=in_specs,
        out_specs=[pl.BlockSpec((tm, tn), o_map)],
        out_shape=[jax.ShapeDtypeStruct(out_shape or (M, N), out_dtype)],
        scratch_shapes=[pltpu.VMEM((tm, tn), f32)] if nk > 1 else [],
        args=args,
    )
    return outs[0] if rider is None else outs


def _rms_fwd(xv, g, *, name):
    S, D = xv.shape
    ts = _div(S, 512, SUBLANES)

    def body(x_ref, g_ref, o_ref):
        v = x_ref[...]
        r = lax.rsqrt(jnp.mean(v * v, axis=-1, keepdims=True) + EPS)
        o_ref[...] = (v * r * g_ref[...]).astype(bf16)

    return pl.pallas_call(
        body,
        name=name,
        grid=(S // ts,),
        in_specs=[pl.BlockSpec((ts, D), lambda i: (i, 0)), pl.BlockSpec((1, D), lambda i: (0, 0))],
        out_specs=pl.BlockSpec((ts, D), lambda i: (i, 0)),
        out_shape=jax.ShapeDtypeStruct((S, D), bf16),
        compiler_params=pltpu.CompilerParams(dimension_semantics=("parallel",)),
    )(xv, g)


def _rms_bwd(xv, g, dy, dres, *, name, rider=None):
    S, D = xv.shape
    ts = _div(S, 256, SUBLANES)

    def body(x_ref, g_ref, dy_ref, dres_ref, dx_ref, dxb_ref, dg_ref):
        i = pl.program_id(0)
        v = x_ref[...]
        r = lax.rsqrt(jnp.mean(v * v, axis=-1, keepdims=True) + EPS)
        vn = v * r
        d = dy_ref[...]
        part = jnp.sum(d * vn, axis=0, keepdims=True)

        @pl.when(i == 0)
        def _():
            dg_ref[...] = part

        @pl.when(i > 0)
        def _():
            dg_ref[...] += part

        t = d * g_ref[...]
        dx = dres_ref[...] + r * (t - vn * jnp.mean(t * vn, axis=-1, keepdims=True))
        dx_ref[...] = dx
        dxb_ref[...] = dx.astype(bf16)

    row = pl.BlockSpec((ts, D), lambda i: (i, 0))
    vec = pl.BlockSpec((1, D), lambda i: (0, 0))
    return _grid_call(
        body,
        rider,
        name=name,
        grid=(S // ts,),
        in_specs=[row, vec, row, row],
        out_specs=[row, row, vec],
        out_shape=[jax.ShapeDtypeStruct((S, D), f32), jax.ShapeDtypeStruct((S, D), bf16), jax.ShapeDtypeStruct((1, D), f32)],
        scratch_shapes=[],
        args=(xv, g, dy, dres),
    )


def _tail(h2, zg, pp, tgt, fn):
    S, D = h2.shape
    ts = _div(S, 256, SUBLANES)

    def body(h_ref, z_ref, p_ref, t_ref, fn_ref, dh_ref, dz_ref, dp_ref, loss_ref, dfn_ref):
        i = pl.program_id(0)
        sg = _sigmoid(z_ref[...])
        ppv = p_ref[...]
        h3 = h_ref[...] + sg * ppv
        r = lax.rsqrt(jnp.mean(h3 * h3, axis=-1, keepdims=True) + EPS)
        hn = h3 * r
        fnv = fn_ref[...]
        e = hn * fnv - t_ref[...]
        lpart = 0.5 * jnp.sum(jnp.mean(e * e, axis=-1, keepdims=True), axis=0, keepdims=True)
        dy = e * (1.0 / D)
        dpart = jnp.sum(dy * hn, axis=0, keepdims=True)

        @pl.when(i == 0)
        def _():
            loss_ref[...] = jnp.broadcast_to(lpart, loss_ref.shape)
            dfn_ref[...] = dpart

        @pl.when(i > 0)
        def _():
            loss_ref[...] += jnp.broadcast_to(lpart, loss_ref.shape)
            dfn_ref[...] += dpart

        t = dy * fnv
        dh3 = r * (t - hn * jnp.mean(t * hn, axis=-1, keepdims=True))
        dh_ref[...] = dh3
        dz_ref[...] = (dh3 * ppv * sg * (1.0 - sg)).astype(bf16)
        dp_ref[...] = (dh3 * sg).astype(bf16)

    row = pl.BlockSpec((ts, D), lambda i: (i, 0))
    vec = pl.BlockSpec((1, D), lambda i: (0, 0))
    one = pl.BlockSpec((1, LANES), lambda i: (0, 0))
    return pl.pallas_call(
        body,
        name="ple_final_loss",
        grid=(S // ts,),
        in_specs=[row, row, row, row, vec],
        out_specs=[row, row, row, one, vec],
        out_shape=[
            jax.ShapeDtypeStruct((S, D), f32),
            jax.ShapeDtypeStruct((S, D), bf16),
            jax.ShapeDtypeStruct((S, D), bf16),
            jax.ShapeDtypeStruct((1, LANES), f32),
            jax.ShapeDtypeStruct((1, D), f32),
        ],
        compiler_params=pltpu.CompilerParams(dimension_semantics=("arbitrary",)),
    )(h2, zg, pp, tgt, fn)


def _shift_down(v, halo, k):
    r = pltpu.roll(v, k, 0)
    hr = pltpu.roll(halo, k, 0)
    row = lax.broadcasted_iota(jnp.int32, hr.shape, 0)
    top = jnp.where(row < k, hr, r[0:SUBLANES])
    return top if v.shape[0] == SUBLANES else jnp.concatenate([top, r[SUBLANES:]], axis=0)


def _shift_up(v, halo, k):
    n = v.shape[0]
    r = pltpu.roll(v, n - k, 0)
    hr = pltpu.roll(halo, SUBLANES - k, 0)
    row = lax.broadcasted_iota(jnp.int32, hr.shape, 0)
    bot = jnp.where(row >= SUBLANES - k, hr, r[n - SUBLANES:])
    return jnp.concatenate([r[: n - SUBLANES], bot], axis=0)


def _conv_specs(S, F, tc, ts):
    nj = F // tc
    rows8 = ts // SUBLANES
    main = pl.BlockSpec((ts, 2 * tc), lambda j, i: (i, j))
    prev = pl.BlockSpec((SUBLANES, 2 * tc), lambda j, i: (jnp.maximum(i * rows8 - 1, 0), j))
    nxt = pl.BlockSpec((SUBLANES, 2 * tc), lambda j, i: (jnp.minimum((i + 1) * rows8, S // SUBLANES - 1), j))
    wg = pl.BlockSpec((3, tc), lambda j, i: (0, j))
    wv = pl.BlockSpec((3, tc), lambda j, i: (0, nj + j))
    bg = pl.BlockSpec((1, tc), lambda j, i: (0, j))
    bv = pl.BlockSpec((1, tc), lambda j, i: (0, nj + j))
    half = pl.BlockSpec((ts, tc), lambda j, i: (i, j))
    return nj, main, prev, nxt, wg, wv, bg, bv, half


def _conv_pre(u_ref, h_ref, wg_ref, wv_ref, bg_ref, bv_ref, tc):
    i = pl.program_id(1)
    u = u_ref[...]
    halo = jnp.where(i == 0, 0.0, h_ref[...])
    u1 = _shift_down(u, halo, 1)
    u2 = _shift_down(u, halo, 2)
    w = jnp.concatenate([wg_ref[...], wv_ref[...]], axis=1)
    b = jnp.concatenate([bg_ref[...], bv_ref[...]], axis=1)
    uc = b + w[0:1] * u2 + w[1:2] * u1 + w[2:3] * u
    return u, u1, u2, uc[:, :tc], uc[:, tc:]


def _convact_fwd(u, cw, cb, tc, rider=None):
    S, F2 = u.shape
    F = F2 // 2
    ts = _div(S, 512, SUBLANES)
    nj, main, prev, nxt, wg, wv, bg, bv, half = _conv_specs(S, F, tc, ts)

    def body(u_ref, h_ref, wg_ref, wv_ref, bg_ref, bv_ref, m_ref):
        _, _, _, g, v = _conv_pre(u_ref, h_ref, wg_ref, wv_ref, bg_ref, bv_ref, tc)
        m_ref[...] = (g * _sigmoid(g) * v).astype(bf16)

    outs = _grid_call(
        body,
        rider,
        name="convact_fwd",
        grid=(nj, S // ts),
        in_specs=[main, prev, wg, wv, bg, bv],
        out_specs=[half],
        out_shape=[jax.ShapeDtypeStruct((S, F), bf16)],
        scratch_shapes=[],
        args=(u, u, cw, cw, cb, cb),
    )
    return outs[0] if rider is None else outs


def _convact_bwd(u, dm, cw, cb, tc, rider=None):
    S, F2 = u.shape
    F = F2 // 2
    ts = _div(S, 512, SUBLANES)
    nj, main, prev, nxt, wg, wv, bg, bv, half = _conv_specs(S, F, tc, ts)
    ni = S // ts
    half_nxt = pl.BlockSpec((SUBLANES, tc), lambda j, i: (jnp.minimum((i + 1) * (ts // SUBLANES), S // SUBLANES - 1), j))

    def act_bwd(g, v, d):
        sg = _sigmoid(g)
        return jnp.concatenate([d * v * sg * (1.0 + g * (1.0 - sg)), d * g * sg], axis=1)

    def body(u_ref, h_ref, n_ref, dm_ref, dmn_ref, wg_ref, wv_ref, bg_ref, bv_ref, du_ref, dwg_ref, dwv_ref, dbg_ref, dbv_ref):
        i = pl.program_id(1)
        u0, u1, u2, g, v = _conv_pre(u_ref, h_ref, wg_ref, wv_ref, bg_ref, bv_ref, tc)
        duc = act_bwd(g, v, dm_ref[...])
        w = jnp.concatenate([wg_ref[...], wv_ref[...]], axis=1)
        b = jnp.concatenate([bg_ref[...], bv_ref[...]], axis=1)
        un = n_ref[...]
        tail = u0[ts - SUBLANES:]
        ucn = b + w[0:1] * _shift_down(un, tail, 2) + w[1:2] * _shift_down(un, tail, 1) + w[2:3] * un
        ducn = jnp.where(i == ni - 1, 0.0, act_bwd(ucn[:, :tc], ucn[:, tc:], dmn_ref[...]))
        du_ref[...] = (w[2:3] * duc + w[1:2] * _shift_up(duc, ducn, 1) + w[0:1] * _shift_up(duc, ducn, 2)).astype(bf16)
        db = jnp.sum(duc, axis=0, keepdims=True)
        dw = jnp.concatenate(
            [jnp.sum(duc * u2, axis=0, keepdims=True), jnp.sum(duc * u1, axis=0, keepdims=True), jnp.sum(duc * u0, axis=0, keepdims=True)],
            axis=0,
        )

        @pl.when(i == 0)
        def _():
            dwg_ref[...] = dw[:, :tc]
            dwv_ref[...] = dw[:, tc:]
            dbg_ref[...] = db[:, :tc]
            dbv_ref[...] = db[:, tc:]

        @pl.when(i > 0)
        def _():
            dwg_ref[...] += dw[:, :tc]
            dwv_ref[...] += dw[:, tc:]
            dbg_ref[...] += db[:, :tc]
            dbv_ref[...] += db[:, tc:]

    w_out = pl.BlockSpec((3, tc), lambda j, i: (0, j))
    b_out = pl.BlockSpec((1, tc), lambda j, i: (0, j))
    return _grid_call(
        body,
        rider,
        name="convact_bwd",
        grid=(nj, ni),
        in_specs=[main, prev, nxt, half, half_nxt, wg, wv, bg, bv],
        out_specs=[main, w_out, w_out, b_out, b_out],
        out_shape=[
            jax.ShapeDtypeStruct((S, F2), bf16),
            jax.ShapeDtypeStruct((3, F), f32),
            jax.ShapeDtypeStruct((3, F), f32),
            jax.ShapeDtypeStruct((1, F), f32),
            jax.ShapeDtypeStruct((1, F), f32),
        ],
        scratch_shapes=[],
        args=(u, u, u, dm, dm, cw, cw, cb, cb),
    )


def _hg_gates(qp, fp, lbl):
    lb = _sigmoid(lbl[0:1] - lbl[1:2])
    sig = _sigmoid(fp)
    sigm = _sigmoid(-fp)
    f = lb + (1.0 - lb) * sig
    k = (1.0 - lb) * sigm
    sq = _sigmoid(qp)
    qf = qp * sq
    row = lax.broadcasted_iota(jnp.int32, (CHUNK, CHUNK), 0)
    col = lax.broadcasted_iota(jnp.int32, (CHUNK, CHUNK), 1)
    b = _dot((row >= col).astype(f32), jnp.log(f), NN, HIGHEST)
    return lb, sig, sigm, f, k, sq, qf, b


def _hg_block(qf, k, b, I):
    r0, n = I * HG_SUB, (I + 1) * HG_SUB
    base = b[r0 - 1:r0] if I > 0 else jnp.zeros_like(b[0:1])
    eq = jnp.exp(b[r0:n] - base)
    ek = jnp.exp(base - b[0:n])
    qt16 = (qf[r0:n] * eq).astype(bf16)
    kt16 = (k[0:n] * ek).astype(bf16)
    row = lax.broadcasted_iota(jnp.int32, (HG_SUB, n), 0) + r0
    col = lax.broadcasted_iota(jnp.int32, (HG_SUB, n), 1)
    return r0, n, eq, ek, qt16, kt16, col <= row


def _per_head(v, hp, fn):
    return jnp.concatenate([fn(v[:, p * HEAD_DIM:(p + 1) * HEAD_DIM]) for p in range(hp)], axis=1)


def _head_mean(v, hp):
    return _per_head(v, hp, lambda t: jnp.broadcast_to(jnp.mean(t, axis=-1, keepdims=True), t.shape))


def _pad_rows(v, rows):
    return v if v.shape[0] == rows else jnp.concatenate([v, jnp.zeros((rows - v.shape[0], v.shape[1]), v.dtype)], axis=0)


def _hgrn_specs(S, W, hp, reverse):
    nc = S // CHUNK
    ngrp = W // (hp * HEAD_DIM)
    cw = hp * HEAD_DIM
    cidx = (lambda c: nc - 1 - c) if reverse else (lambda c: c)

    def proj(off):
        return pl.BlockSpec((CHUNK, cw), lambda h, c: (cidx(c), off * ngrp + h))

    act = pl.BlockSpec((CHUNK, cw), lambda h, c: (cidx(c), h))
    lbl = pl.BlockSpec((2, cw), lambda h, c: (0, h))
    gn = pl.BlockSpec((1, HEAD_DIM), lambda h, c: (0, 0))
    st = pl.BlockSpec((hp, 1, HEAD_DIM, HEAD_DIM), lambda h, c: (h, cidx(c), 0, 0))
    return nc, ngrp, proj, act, lbl, gn, st


def _grid_call(body, rider, *, name, grid, in_specs, out_specs, out_shape, scratch_shapes, args):
    params = pltpu.CompilerParams(dimension_semantics=("arbitrary",) * len(grid))
    if rider is None:
        return pl.pallas_call(body, name=name, grid=grid, in_specs=in_specs, out_specs=out_specs, out_shape=out_shape,
                              scratch_shapes=scratch_shapes, compiler_params=params)(*args)
    n_in, n_out, n_scr = len(in_specs), len(out_specs), len(scratch_shapes)
    ri, ro = len(rider.ins), len(rider.out_shapes)

    def wrapped(*refs):
        a, b = n_in, n_in + ri
        c, d = b + n_out, b + n_out + ro
        e = d + n_scr
        ids = [pl.program_id(t) for t in range(len(grid))]
        is_first = functools.reduce(jnp.logical_and, [i == 0 for i in ids])
        is_last = functools.reduce(jnp.logical_and, [i == g - 1 for i, g in zip(ids, grid)])

        @pl.when(is_first)
        def _():
            rider.first(refs[a:b], refs[c:d], refs[e:])

        body(*refs[:a], *refs[b:c], *refs[d:e])

        @pl.when(is_last)
        def _():
            rider.last(refs[a:b], refs[c:d], refs[e:])

    return pl.pallas_call(
        wrapped,
        name=name,
        grid=grid,
        in_specs=list(in_specs) + [_hbm()] * ri,
        out_specs=list(out_specs) + [_hbm()] * ro,
        out_shape=list(out_shape) + list(rider.out_shapes),
        input_output_aliases={n_in + i: n_out + o for i, o in rider.aliases.items()},
        scratch_shapes=list(scratch_shapes) + list(rider.sems),
        compiler_params=params,
    )(*args, *rider.ins)


def _hgrn_fwd(proj, lb_logits, hg_norm, W, rider=None):
    S = proj.shape[0]
    hp = min(HG_HEADS_PER_STEP, W // HEAD_DIM)
    nc, ngrp, pspec, act, lbl_spec, gn_spec, st_spec = _hgrn_specs(S, W, hp, False)
    nb = CHUNK // HG_SUB

    def body(q_ref, f_ref, i_ref, g_ref, lbl_ref, gn_ref, y_ref, o_ref, st_ref, state):
        c = pl.program_id(1)

        @pl.when(c == 0)
        def _():
            state[...] = jnp.zeros_like(state)

        _, _, _, _, k, _, qf, b = _hg_gates(q_ref[...], f_ref[...], lbl_ref[...])
        v16 = i_ref[...].astype(bf16)
        a16 = (qf * jnp.exp(b)).astype(bf16)
        bl = b[CHUNK - 1:CHUNK]
        kd16 = (k * jnp.exp(bl - b)).astype(bf16)
        ebl = jnp.exp(bl)
        blocks = [_hg_block(qf, k, b, I) for I in range(nb)]
        heads = [slice(p * HEAD_DIM, (p + 1) * HEAD_DIM) for p in range(hp)]
        st_prev = state[...]
        st_ref[:, 0] = st_prev
        inter = [_dot(a16[:, sl], st_prev[p].astype(bf16), NT) for p, sl in enumerate(heads)]
        scores = [[_dot(qt16[:, sl], kt16[:, sl], NT) for sl in heads] for _, _, _, _, qt16, kt16, _ in blocks]
        intra = [[_dot(jnp.where(blk[6], sc, 0.0).astype(bf16), v16[0:blk[1], sl], NN) for sc, sl in zip(scs, heads)]
                 for blk, scs in zip(blocks, scores)]
        st_new = [st_prev[p] * ebl[:, sl] + _dot(v16[:, sl], kd16[:, sl], TN) for p, sl in enumerate(heads)]
        state[...] = jnp.stack(st_new)
        o = jnp.concatenate([inter[p] + jnp.concatenate([rows[p] for rows in intra], axis=0) for p in range(hp)], axis=1)
        o_ref[...] = o
        r = lax.rsqrt(_head_mean(o * o, hp) + EPS)
        gp = g_ref[...]
        y_ref[...] = (o * r * jnp.tile(gn_ref[...], (1, hp)) * (gp * _sigmoid(gp))).astype(bf16)

    H = W // HEAD_DIM
    return _grid_call(
        body,
        rider,
        name="hgrn_fwd",
        grid=(ngrp, nc),
        in_specs=[pspec(0), pspec(1), pspec(2), pspec(3), lbl_spec, gn_spec],
        out_specs=[act, act, st_spec],
        out_shape=[
            jax.ShapeDtypeStruct((S, W), bf16),
            jax.ShapeDtypeStruct((S, W), f32),
            jax.ShapeDtypeStruct((H, nc, HEAD_DIM, HEAD_DIM), f32),
        ],
        scratch_shapes=[pltpu.VMEM((hp, HEAD_DIM, HEAD_DIM), f32)],
        args=(proj, proj, proj, proj, lb_logits, hg_norm),
    )


def _hgrn_bwd(proj, o_hg, st, dycat, lb_logits, hg_norm, W, rider=None):
    S = proj.shape[0]
    hp = min(HG_HEADS_PER_STEP, W // HEAD_DIM)
    assert hp * HEAD_DIM == W, "the four gradients share one output block: all heads in one grid step"
    nc, ngrp, pspec, act, lbl_spec, gn_spec, st_spec = _hgrn_specs(S, W, hp, True)
    nb = CHUNK // HG_SUB

    def body(q_ref, f_ref, i_ref, g_ref, o_ref, st_ref, dy_ref, lbl_ref, gn_ref,
             d4_ref, dlbl_ref, dgn_ref, dstate):
        cw_ = hp * HEAD_DIM
        dq_ref, df_ref, di_ref, dg_ref = (d4_ref.at[:, pl.ds(t * cw_, cw_)] for t in range(4))
        h = pl.program_id(0)
        c = pl.program_id(1)

        @pl.when(c == 0)
        def _():
            dstate[...] = jnp.zeros_like(dstate)
            dlbl_ref[...] = jnp.zeros_like(dlbl_ref)

        @pl.when((c == 0) & (h == 0))
        def _():
            dgn_ref[...] = jnp.zeros_like(dgn_ref)

        row = lax.broadcasted_iota(jnp.int32, (CHUNK, CHUNK), 0)
        col = lax.broadcasted_iota(jnp.int32, (CHUNK, CHUNK), 1)
        upper = (col >= row).astype(f32)
        last_row = lax.broadcasted_iota(jnp.int32, (CHUNK, hp * HEAD_DIM), 0) == CHUNK - 1

        qp, fp = q_ref[...], f_ref[...]
        lb, sig, sigm, f, k, sq, qf, b = _hg_gates(qp, fp, lbl_ref[...])
        v16 = i_ref[...].astype(bf16)
        gnv = jnp.tile(gn_ref[...], (1, hp))
        gp = g_ref[...]
        sgg = _sigmoid(gp)
        gate = gp * sgg
        o = o_ref[...]
        r = lax.rsqrt(_head_mean(o * o, hp) + EPS)
        on = o * r
        dy = dy_ref[...]
        dg_ref[...] = (dy * on * gnv * (sgg * (1.0 + gp * (1.0 - sgg)))).astype(bf16)
        dgn_wide = jnp.sum(dy * on * gate, axis=0, keepdims=True)
        dgn_ref[...] += functools.reduce(jnp.add, [dgn_wide[:, p * HEAD_DIM:(p + 1) * HEAD_DIM] for p in range(hp)])
        don = dy * gnv * gate
        do16 = (r * (don - on * _head_mean(don * on, hp))).astype(bf16)
        eb = jnp.exp(b)
        A = qf * eb
        a16 = A.astype(bf16)
        bl = b[CHUNK - 1:CHUNK]
        ebl = jnp.exp(bl)
        ekd = jnp.exp(bl - b)
        kd = k * ekd
        kd16 = kd.astype(bf16)
        blocks = [_hg_block(qf, k, b, I) for I in range(nb)]

        heads = [slice(p * HEAD_DIM, (p + 1) * HEAD_DIM) for p in range(hp)]
        st_prev = st_ref[:, 0]
        dst_new = dstate[...]
        st16, dst16 = st_prev.astype(bf16), dst_new.astype(bf16)
        dA_h = [_dot(do16[:, sl], st16[p], NN) for p, sl in enumerate(heads)]
        dkd_h = [_dot(v16[:, sl], dst16[p], NN) for p, sl in enumerate(heads)]
        dv_h = [_dot(kd16[:, sl], dst16[p], NT) for p, sl in enumerate(heads)]
        dstate[...] = jnp.stack([dst_new[p] * ebl[:, sl] + _dot(do16[:, sl], a16[:, sl], TN) for p, sl in enumerate(heads)])
        dbl_h = [jnp.sum(dst_new[p] * st_prev[p], axis=0, keepdims=True) for p in range(hp)]
        sc_h = [[jnp.where(mask, _dot(qt16[:, sl], kt16[:, sl], NT), 0.0).astype(bf16) for sl in heads]
                for _, _, _, _, qt16, kt16, mask in blocks]
        dsc_h = [[jnp.where(mask, _dot(do16[r0:n, sl], v16[0:n, sl], NT), 0.0).astype(bf16) for sl in heads]
                 for r0, n, _, _, _, _, mask in blocks]
        dvi_h = [[_dot(sc, do16[blk[0]:blk[1], sl], TN) for sc, sl in zip(scs, heads)] for blk, scs in zip(blocks, sc_h)]
        dqt_h = [[_dot(dsc, blk[5][:, sl], NN) for dsc, sl in zip(dscs, heads)] for blk, dscs in zip(blocks, dsc_h)]
        dkt_h = [[_dot(dsc, blk[4][:, sl], TN) for dsc, sl in zip(dscs, heads)] for blk, dscs in zip(blocks, dsc_h)]
        dv_h = [functools.reduce(jnp.add, [dv_h[p]] + [_pad_rows(rows[p], CHUNK) for rows in dvi_h]) for p in range(hp)]

        dA, dkd = jnp.concatenate(dA_h, axis=1), jnp.concatenate(dkd_h, axis=1)
        dbl = jnp.concatenate(dbl_h, axis=1) * ebl + jnp.sum(dkd * kd, axis=0, keepdims=True)
        db = dA * A - dkd * kd + jnp.where(last_row, dbl, 0.0)
        dk = dkd * ekd
        dq_rows, db_rows = [], []
        for I, (r0, n, eq, ek, qt16, kt16, _) in enumerate(blocks):
            dqt, dkt = jnp.concatenate(dqt_h[I], axis=1), jnp.concatenate(dkt_h[I], axis=1)
            dq_rows.append(dqt * eq)
            dk = dk + _pad_rows(dkt * ek, CHUNK)
            db_rows.append(dqt * qt16.astype(f32))
            db = db - _pad_rows(dkt * kt16.astype(f32), CHUNK)
        dqf = dA * eb + jnp.concatenate(dq_rows, axis=0)
        db = db + jnp.concatenate(db_rows, axis=0)
        dlogf = _dot(upper, db, NN, HIGHEST)
        dfg = dlogf / f
        df_ref[...] = ((1.0 - lb) * sig * sigm * (dfg - dk)).astype(bf16)
        dlb = jnp.sum(sigm * (dfg - dk), axis=0, keepdims=True)
        dl0 = dlb * lb * (1.0 - lb)
        dlbl_ref[...] += jnp.concatenate([dl0, -dl0], axis=0)
        dq_ref[...] = (dqf * (sq * (1.0 + qp * (1.0 - sq)))).astype(bf16)
        di_ref[...] = jnp.concatenate(dv_h, axis=1).astype(bf16)

    cw = hp * HEAD_DIM
    dy_spec = pl.BlockSpec((CHUNK, cw), lambda h, c: (nc - 1 - c, h))
    return _grid_call(
        body,
        rider,
        name="hgrn_bwd",
        grid=(ngrp, nc),
        in_specs=[pspec(0), pspec(1), pspec(2), pspec(3), act, st_spec, dy_spec, lbl_spec, gn_spec],
        out_specs=[pl.BlockSpec((CHUNK, 4 * W), lambda h, c: (nc - 1 - c, 0)), lbl_spec, gn_spec],
        out_shape=[jax.ShapeDtypeStruct((S, 4 * W), bf16), jax.ShapeDtypeStruct((2, W), f32), jax.ShapeDtypeStruct((1, HEAD_DIM), f32)],
        scratch_shapes=[pltpu.VMEM((hp, HEAD_DIM, HEAD_DIM), f32)],
        args=(proj, proj, proj, proj, o_hg, st, dycat, lb_logits, hg_norm),
    )


def _att_dims(S):
    qt = ATT_Q_ROWS if S % ATT_Q_ROWS == 0 else CHUNK
    pad = LEFT_CHUNKS * CHUNK
    kb = pad + qt
    return qt, pad, kb, kb + qt


def _kv_padded(proj, W, pad):
    S = proj.shape[0]
    ts = _div(pad, 512, 2 * SUBLANES)
    assert S % ts == 0 and pad % ts == 0
    npad = pad // ts

    def body(k_ref, v_ref, ko_ref, vo_ref):
        i = pl.program_id(0)

        @pl.when(i < npad)
        def _():
            ko_ref[...] = jnp.zeros_like(ko_ref)
            vo_ref[...] = jnp.zeros_like(vo_ref)

        @pl.when(i >= npad)
        def _():
            ko_ref[...] = k_ref[...].astype(bf16)
            vo_ref[...] = v_ref[...].astype(bf16)

    out = pl.BlockSpec((ts, W), lambda i: (i, 0))
    return pl.pallas_call(
        body,
        name="kv_padded",
        grid=((S + pad) // ts,),
        in_specs=[pl.BlockSpec((ts, W), lambda i: (jnp.maximum(i - npad, 0), 5)), pl.BlockSpec((ts, W), lambda i: (jnp.maximum(i - npad, 0), 6))],
        out_specs=[out, out],
        out_shape=[jax.ShapeDtypeStruct((S + pad, W), bf16)] * 2,
        compiler_params=pltpu.CompilerParams(dimension_semantics=("parallel",)),
    )(proj, proj)


def _att_probs(qk, bias, start, qt, pad, kb):
    s = qk * (HEAD_DIM ** -0.5) + bias
    row = lax.broadcasted_iota(jnp.int32, (qt, kb), 0)
    col = lax.broadcasted_iota(jnp.int32, (qt, kb), 1)
    lo = jnp.bitwise_and(row, -CHUNK)
    ok = (col >= lo) & (col < lo + pad + CHUNK) & (col + start >= pad)
    s = jnp.where(ok, s, NEG)
    e = jnp.exp(s - jnp.max(s, axis=-1, keepdims=True))
    return e / jnp.sum(e, axis=-1, keepdims=True)


def _att_bias(ext_row, qt, kb, ne):
    e = jnp.broadcast_to(ext_row, (qt, ne))
    return pltpu.roll(e, ne - qt + 1, 1, stride=1, stride_axis=0)[:, :kb]


def _att_specs(S, H, ah, qt, pad, ne):
    cw = ah * HEAD_DIM
    q_spec = pl.BlockSpec((qt, cw), lambda h, g: (g, 4 * H // ah + h))
    kv_spec = pl.BlockSpec((S + pad, cw), lambda h, g: (0, h))
    ext_spec = pl.BlockSpec((ah, 1, ne), lambda h, g: (h, 0, 0))
    row_spec = pl.BlockSpec((qt, cw), lambda h, g: (g, h))
    return cw, q_spec, kv_spec, ext_spec, row_spec


def _attn_fwd(proj, kp, vp, ext, W, rider=None):
    S = proj.shape[0]
    H = W // HEAD_DIM
    ah = min(ATT_HEADS_PER_STEP, H)
    qt, pad, kb, ne = _att_dims(S)
    cw, q_spec, kv_spec, ext_spec, row_spec = _att_specs(S, H, ah, qt, pad, ne)
    heads = [slice(a * HEAD_DIM, (a + 1) * HEAD_DIM) for a in range(ah)]

    def body(q_ref, k_ref, v_ref, ext_ref, o_ref, bias):
        g = pl.program_id(1)

        @pl.when(g == 0)
        def _():
            for a in range(ah):
                bias[a] = _att_bias(ext_ref[a], qt, kb, ne)

        start = pl.multiple_of(g * qt, qt)
        q16 = q_ref[...].astype(bf16)
        kb16 = k_ref[pl.ds(start, kb), :]
        vb16 = v_ref[pl.ds(start, kb), :]
        qk = [_dot(q16[:, sl], kb16[:, sl], NT) for sl in heads]
        pn = [_att_probs(qk[a], bias[a], start, qt, pad, kb).astype(bf16) for a in range(ah)]
        o_ref[...] = jnp.concatenate([_dot(pn[a], vb16[:, sl], NN) for a, sl in enumerate(heads)], axis=1).astype(bf16)

    outs = _grid_call(
        body,
        rider,
        name="attn_fwd",
        grid=(H // ah, S // qt),
        in_specs=[q_spec, kv_spec, kv_spec, ext_spec],
        out_specs=[row_spec],
        out_shape=[jax.ShapeDtypeStruct((S, W), bf16)],
        scratch_shapes=[pltpu.VMEM((ah, qt, kb), f32)],
        args=(proj, kp, vp, ext),
    )
    return outs[0] if rider is None else outs


def _attn_bwd(proj, kp, vp, ext, dycat, W, rider=None):
    S = proj.shape[0]
    H = W // HEAD_DIM
    ah = min(ATT_HEADS_PER_STEP, H)
    qt, pad, kb, ne = _att_dims(S)
    cw, q_spec, kv_spec, ext_spec, row_spec = _att_specs(S, H, ah, qt, pad, ne)
    heads = [slice(a * HEAD_DIM, (a + 1) * HEAD_DIM) for a in range(ah)]
    scale = HEAD_DIM ** -0.5
    ng = S // qt

    def body(q_ref, k_ref, v_ref, ext_ref, do_ref, dq_ref, dko_ref, dvo_ref, db_ref, bias, dk_ref, dv_ref):
        g = pl.program_id(1)

        @pl.when(g == 0)
        def _():
            for a in range(ah):
                bias[a] = _att_bias(ext_ref[a], qt, kb, ne)
            dk_ref[...] = jnp.zeros_like(dk_ref)
            dv_ref[...] = jnp.zeros_like(dv_ref)
            db_ref[...] = jnp.zeros_like(db_ref)

        start = pl.multiple_of(g * qt, qt)
        q16 = q_ref[...].astype(bf16)
        kb16 = k_ref[pl.ds(start, kb), :]
        vb16 = v_ref[pl.ds(start, kb), :]
        do16 = do_ref[...].astype(bf16)
        qk = [_dot(q16[:, sl], kb16[:, sl], NT) for sl in heads]
        dpn = [_dot(do16[:, sl], vb16[:, sl], NT) for sl in heads]
        pn = [_att_probs(qk[a], bias[a], start, qt, pad, kb) for a in range(ah)]
        ds = [pn[a] * (dpn[a] - jnp.sum(dpn[a] * pn[a], axis=-1, keepdims=True)) for a in range(ah)]
        ds16 = [d.astype(bf16) for d in ds]
        dv_ref[pl.ds(start, kb), :] += jnp.concatenate([_dot(pn[a].astype(bf16), do16[:, sl], TN) for a, sl in enumerate(heads)], axis=1)
        dq_ref[...] = (jnp.concatenate([_dot(ds16[a], kb16[:, sl], NN) for a, sl in enumerate(heads)], axis=1) * scale).astype(bf16)
        dk_ref[pl.ds(start, kb), :] += jnp.concatenate([_dot(ds16[a], q16[:, sl], TN) for a, sl in enumerate(heads)], axis=1) * scale
        for a in range(ah):
            db_ref[a] += ds[a]

        @pl.when(g == ng - 1)
        def _():
            dko_ref[...] = dk_ref[pl.ds(pad, S), :].astype(bf16)
            dvo_ref[...] = dv_ref[pl.ds(pad, S), :].astype(bf16)

    kv_out = pl.BlockSpec((S, cw), lambda h, g: (0, h))
    return _grid_call(
        body,
        rider,
        name="attn_bwd",
        grid=(H // ah, S // qt),
        in_specs=[q_spec, kv_spec, kv_spec, ext_spec,
                  pl.BlockSpec((qt, cw), lambda h, g: (g, H // ah + h))],
        out_specs=[row_spec, kv_out, kv_out, pl.BlockSpec((ah, qt, kb), lambda h, g: (h, 0, 0))],
        out_shape=[
            jax.ShapeDtypeStruct((S, W), bf16),
            jax.ShapeDtypeStruct((S, W), bf16),
            jax.ShapeDtypeStruct((S, W), bf16),
            jax.ShapeDtypeStruct((H, qt, kb), f32),
        ],
        scratch_shapes=[pltpu.VMEM((ah, qt, kb), f32), pltpu.VMEM((S + pad, cw), f32), pltpu.VMEM((S + pad, cw), f32)],
        args=(proj, kp, vp, ext, dycat),
    )


def _bias_onehot(qt, ne, nrel_pad):
    m = lax.broadcasted_iota(jnp.int32, (nrel_pad, ne), 1)
    r = lax.broadcasted_iota(jnp.int32, (nrel_pad, ne), 0)
    rel = LEFT_CHUNKS * CHUNK + qt - 1 - m
    hot = (r == jnp.clip(rel, -REL_CLIP, REL_CLIP) + REL_CLIP) & (m < ne - 1)
    return hot.astype(f32)


def _bias_ext(rel_bias_pad, onehot):
    H, _, nr = rel_bias_pad.shape
    ne = onehot.shape[1]

    def body(rb_ref, oh_ref, o_ref):
        o_ref[0] = _dot(rb_ref[0], oh_ref[...], NN, HIGHEST)

    return pl.pallas_call(
        body,
        name="bias_ext",
        grid=(H,),
        in_specs=[pl.BlockSpec((1, 1, nr), lambda h: (h, 0, 0)), pl.BlockSpec((nr, ne), lambda h: (0, 0))],
        out_specs=pl.BlockSpec((1, 1, ne), lambda h: (h, 0, 0)),
        out_shape=jax.ShapeDtypeStruct((H, 1, ne), f32),
        compiler_params=pltpu.CompilerParams(dimension_semantics=("parallel",)),
    )(rel_bias_pad, onehot)


def _bias_bwd(dbias, onehot):
    H, qt, kb = dbias.shape
    nr, ne = onehot.shape

    def body(d_ref, oh_ref, o_ref):
        row = lax.broadcasted_iota(jnp.int32, (qt, qt), 0)
        col = lax.broadcasted_iota(jnp.int32, (qt, qt), 1)
        flip = (row + col == qt - 1).astype(f32)
        x = _dot(flip, d_ref[0], NN, HIGHEST)
        x = jnp.concatenate([x, jnp.zeros((qt, ne - kb), f32)], axis=1)
        de = jnp.sum(pltpu.roll(x, 0, 1, stride=1, stride_axis=0), axis=0, keepdims=True)
        o_ref[0] = _dot(de, oh_ref[...], NT, HIGHEST)

    return pl.pallas_call(
        body,
        name="bias_bwd",
        grid=(H,),
        in_specs=[pl.BlockSpec((1, qt, kb), lambda h: (h, 0, 0)), pl.BlockSpec((nr, ne), lambda h: (0, 0))],
        out_specs=pl.BlockSpec((1, 1, nr), lambda h: (h, 0, 0)),
        out_shape=jax.ShapeDtypeStruct((H, 1, nr), f32),
        compiler_params=pltpu.CompilerParams(dimension_semantics=("parallel",)),
    )(dbias, onehot)


def _place():
    x, y, c = lax.axis_index("x"), lax.axis_index("y"), lax.axis_index("c")
    return x, y, c


def _flip(x, y, k):
    return (1 - x if k & 2 else x), (1 - y if k & 1 else y)


def _region(ref, shard_shape, axis, j, half):
    R, C = shard_shape
    if axis == 0:
        if half is None:
            return ref.at[pl.ds(j * R, R), :]
        return ref.at[pl.ds(j * R + half * (R // 2), R // 2), :]
    if half is None:
        return ref.at[:, pl.ds(j * C, C)]
    return ref.at[pl.ds(half * (R // 2), R // 2), pl.ds(j * C, C)]


def _remote(src, dst, send_sem, recv_sem, dev):
    return pltpu.make_async_remote_copy(src_ref=src, dst_ref=dst, send_sem=send_sem, recv_sem=recv_sem,
                                        device_id=dev, device_id_type=MESH)


def _cast_place(shard, axis, where, *, name):
    R, C = shard.shape
    tr = _div(R, 256, 2 * SUBLANES)
    nr = R // tr
    full = (N_CHIPS * R, C) if axis == 0 else (R, N_CHIPS * C)
    omap = (lambda r, s: (s[0] * nr + r, 0)) if axis == 0 else (lambda r, s: (r, s[0]))

    def body(s_ref, x_ref, o_ref):
        o_ref[...] = x_ref[...].astype(bf16)

    return pl.pallas_call(
        body,
        name=name,
        grid_spec=pltpu.PrefetchScalarGridSpec(
            num_scalar_prefetch=1,
            grid=(nr,),
            in_specs=[pl.BlockSpec((tr, C), lambda r, s: (r, 0))],
            out_specs=pl.BlockSpec((tr, C), omap),
        ),
        out_shape=jax.ShapeDtypeStruct(full, bf16),
        compiler_params=pltpu.CompilerParams(dimension_semantics=("parallel",)),
    )(where, shard)


class _Rider(NamedTuple):
    ins: list
    out_shapes: list
    aliases: dict
    sems: list
    first: Callable
    last: Callable


def _run_rider(rider, *, name):
    ni, no = len(rider.ins), len(rider.out_shapes)

    def body(*refs):
        ins, outs, sems = refs[:ni], refs[ni:ni + no], refs[ni + no:]
        rider.first(ins, outs, sems)
        rider.last(ins, outs, sems)

    return pl.pallas_call(
        body,
        name=name,
        in_specs=[_hbm()] * ni,
        out_specs=[_hbm()] * no,
        out_shape=rider.out_shapes,
        input_output_aliases=rider.aliases,
        scratch_shapes=rider.sems,
    )(*rider.ins)


def _start_all(copies):
    def first(ins, outs, sems):
        for cp in copies(ins, outs, sems):
            cp.start()

    def last(ins, outs, sems):
        cps = copies(ins, outs, sems)
        for cp in cps:
            cp.wait_recv()
        for cp in cps:
            cp.wait_send()

    return first, last


def _mm_while_gathering(a, full, shard_shape, where, *, tm, tn, name):
    S, K = a.shape
    R, Cs = shard_shape
    N = full.shape[1]
    tps, nj, ni = Cs // tn, N // tn, S // tm
    assert R == K and Cs % tn == 0 and S % tm == 0

    def block(j, chip):
        return jnp.bitwise_xor(chip, j // tps) * tps + j % tps

    def body(s_ref, a_ref, w_in, o_ref, w_ref, buf, fsem, send1, recv1, send2, recv2):
        j, i = pl.program_id(0), pl.program_id(1)
        x, y, c = _place()
        chip = 2 * x + y

        def ici(k):
            px, py = _flip(x, y, k)
            mine = _region(w_ref, shard_shape, 1, chip, c)
            return _remote(mine, mine, send1.at[k - 1], recv1.at[k - 1], (px, py, c))

        def handed(k, half):
            px, py = _flip(x, y, k)
            land = _region(w_ref, shard_shape, 1, 2 * px + py, half)
            return land, _remote(land, land, send2.at[k - 1], recv2.at[k - 1], (x, y, 1 - c))

        def tile(jj):
            col = pl.multiple_of(block(jj, chip) * tn, LANES)
            return pltpu.make_async_copy(w_ref.at[:, pl.ds(col, tn)], buf.at[jj % 2], fsem.at[jj % 2])

        @pl.when((j == 0) & (i == 0))
        def _():
            for k in (1, 2, 3):
                ici(k).start()
            tile(j).start()

        @pl.when(i == 0)
        def _():
            tile(j).wait()

        @pl.when((i == ni - 1) & (j + 1 < nj))
        def _():
            for k in (1, 2, 3):
                @pl.when(j + 1 == k * tps)
                def _():
                    land, forward = handed(k, c)
                    _remote(land, land, send1.at[k - 1], recv1.at[k - 1], (x, y, c)).wait_recv()
                    forward.start()
                    handed(k, 1 - c)[1].wait_recv()

            tile(j + 1).start()

        o_ref[...] = _dot(a_ref[...], buf[j % 2], NN)

        @pl.when((j == nj - 1) & (i == ni - 1))
        def _():
            for k in (1, 2, 3):
                ici(k).wait_send()
                handed(k, c)[1].wait_send()

    sems = pltpu.SemaphoreType.DMA((3,))
    spec = pltpu.PrefetchScalarGridSpec(
        num_scalar_prefetch=1,
        grid=(nj, ni),
        in_specs=[pl.BlockSpec((tm, K), lambda j, i, s: (i, 0)), _hbm()],
        out_specs=[pl.BlockSpec((tm, tn), lambda j, i, s: (i, block(j, s[0]))), _hbm()],
        scratch_shapes=[pltpu.VMEM((2, K, tn), full.dtype), pltpu.SemaphoreType.DMA((2,)), sems, sems, sems, sems],
    )
    return pl.pallas_call(
        body,
        name=name,
        grid_spec=spec,
        out_shape=[jax.ShapeDtypeStruct((S, N), f32), jax.ShapeDtypeStruct(full.shape, full.dtype)],
        input_output_aliases={2: 1},
        compiler_params=pltpu.CompilerParams(dimension_semantics=("arbitrary", "arbitrary")),
    )(where, a, full)


class _GatherJob(NamedTuple):
    buf: int
    shard_shape: tuple
    axis: int
    split: bool
    rels: tuple
    hand_over: bool


def _gather_rider(bufs, jobs):
    def copies(ins, outs, sems):
        x, y, c = _place()
        cps = []
        for n, job in enumerate(jobs):
            half = c if job.split else None
            for k in job.rels:
                px, py = _flip(x, y, k)
                chip, dev = (2 * px + py, (x, y, 1 - c)) if job.hand_over else (2 * x + y, (px, py, c))
                cps.append(_remote(_region(ins[job.buf], job.shard_shape, job.axis, chip, half),
                                   _region(outs[job.buf], job.shard_shape, job.axis, chip, half),
                                   sems[0].at[3 * n + k - 1], sems[1].at[3 * n + k - 1], dev))
        return cps

    assert all(job.split or not job.hand_over for job in jobs)
    sems = pltpu.SemaphoreType.DMA((3 * len(jobs),))
    return _Rider(list(bufs), [jax.ShapeDtypeStruct(b.shape, b.dtype) for b in bufs], {m: m for m in range(len(bufs))},
                  [sems, sems], *_start_all(copies))


def _chips_rider(ts, rels=(1, 2, 3), landing=None):
    n = len(ts)

    def copies(ins, outs, sems):
        x, y, c = _place()
        cps = []
        for m in range(n):
            for k in rels:
                px, py = _flip(x, y, k)
                cps.append(_remote(ins[m].at[2 * px + py], outs[m].at[k - 1], sems[0].at[3 * m + k - 1], sems[1].at[3 * m + k - 1], (px, py, c)))
        return cps

    sems = pltpu.SemaphoreType.DMA((3 * n,))
    return _Rider(list(ts) + list(landing or []), [jax.ShapeDtypeStruct((3,) + t.shape[1:], t.dtype) for t in ts],
                  {n + m: m for m in range(n)} if landing else {}, [sems, sems], *_start_all(copies))


def _sibling_rider(dws, shard_shapes, axes):
    n = len(dws)

    def copies(ins, outs, sems):
        x, y, c = _place()
        return [_remote(_region(ins[m], shard_shapes[m], axes[m], j, 1 - c), outs[m].at[j],
                        sems[0].at[N_CHIPS * m + j], sems[1].at[N_CHIPS * m + j], (x, y, 1 - c))
                for m in range(n) for j in range(N_CHIPS)]

    sems = pltpu.SemaphoreType.DMA((N_CHIPS * n,))
    return _Rider(list(dws), [jax.ShapeDtypeStruct((N_CHIPS, s[0] // 2, s[1]), d.dtype) for s, d in zip(shard_shapes, dws)], {},
                  [sems, sems], *_start_all(copies))


def _join_rider(gs):
    n = len(gs)

    def copies(ins, outs, sems):
        x, y, c = _place()
        cps = []
        for m in range(n):
            hr = gs[m].shape[0] // 2
            cps.append(_remote(ins[m].at[pl.ds(c * hr, hr), :], outs[m].at[pl.ds(c * hr, hr), :], sems[0].at[m], sems[1].at[m], (x, y, 1 - c)))
        return cps

    sems = pltpu.SemaphoreType.DMA((n,))
    return _Rider(list(gs), [jax.ShapeDtypeStruct(g.shape, g.dtype) for g in gs], {m: m for m in range(n)}, [sems, sems], *_start_all(copies))


def _both(r1, r2):
    i1, o1, s1 = len(r1.ins), len(r1.out_shapes), len(r1.sems)
    aliases = dict(r1.aliases)
    aliases.update({i1 + i: o1 + o for i, o in r2.aliases.items()})

    def first(ins, outs, sems):
        r1.first(ins[:i1], outs[:o1], sems[:s1])
        r2.first(ins[i1:], outs[o1:], sems[s1:])

    def last(ins, outs, sems):
        r1.last(ins[:i1], outs[:o1], sems[:s1])
        r2.last(ins[i1:], outs[o1:], sems[s1:])

    return _Rider(r1.ins + r2.ins, r1.out_shapes + r2.out_shapes, aliases, r1.sems + r2.sems, first, last)


def _pair_sum(dw, got, shard_shape, axis, where, *, name):
    R, C = shard_shape
    hr = R // 2
    tr = _div(hr, 256, 2 * SUBLANES)
    nr = hr // tr
    dmap = (lambda j, r, s: ((2 * j + s[1]) * nr + r, 0)) if axis == 0 else (lambda j, r, s: (s[1] * nr + r, j))
    slot = pl.BlockSpec((1, tr, C), lambda j, r, s: (j, r, 0))

    def body(s_ref, d_ref, g_ref, o_ref):
        o_ref[0] = (d_ref[...].astype(f32) + g_ref[0].astype(f32)).astype(bf16)

    return pl.pallas_call(
        body,
        name=name,
        grid_spec=pltpu.PrefetchScalarGridSpec(
            num_scalar_prefetch=1,
            grid=(N_CHIPS, nr),
            in_specs=[pl.BlockSpec((tr, C), dmap), slot],
            out_specs=slot,
        ),
        out_shape=jax.ShapeDtypeStruct((N_CHIPS, hr, C), bf16),
        compiler_params=pltpu.CompilerParams(dimension_semantics=("parallel", "parallel")),
    )(where, dw, got)


def _chip_sum(pair, others, where, *, name):
    _, hr, C = pair.shape
    tr = _div(hr, 256, 2 * SUBLANES)
    nr = hr // tr

    def body(s_ref, p_ref, o_ref, g_ref):
        g_ref[...] = p_ref[0].astype(f32) + o_ref[0].astype(f32) + o_ref[1].astype(f32) + o_ref[2].astype(f32)

    return pl.pallas_call(
        body,
        name=name,
        grid_spec=pltpu.PrefetchScalarGridSpec(
            num_scalar_prefetch=1,
            grid=(nr,),
            in_specs=[pl.BlockSpec((1, tr, C), lambda r, s: (s[0], r, 0)), pl.BlockSpec((3, tr, C), lambda r, s: (0, r, 0))],
            out_specs=pl.BlockSpec((tr, C), lambda r, s: (s[1] * nr + r, 0)),
        ),
        out_shape=jax.ShapeDtypeStruct((2 * hr, C), f32),
        compiler_params=pltpu.CompilerParams(dimension_semantics=("parallel",)),
    )(where, pair, others)


def _allreduce_small(v):
    rows = v.shape[0]

    def body(v_ref, o_ref, gath, send, recv):
        x, y, c = _place()
        me = 4 * x + 2 * y + c
        gath[pl.ds(me, 1)] = v_ref[...][None]
        cps = []
        for k in range(1, N_DEV):
            peer = (1 - x if k & 4 else x, 1 - y if k & 2 else y, 1 - c if k & 1 else c)
            cp = _remote(v_ref, gath.at[me], send.at[k - 1], recv.at[k - 1], peer)
            cp.start()
            cps.append(cp)
        for cp in cps:
            cp.wait_recv()
        for cp in cps:
            cp.wait_send()
        acc = gath[0]
        for d in range(1, N_DEV):
            acc = acc + gath[d]
        o_ref[...] = acc

    return pl.pallas_call(
        body,
        name="small_allreduce",
        in_specs=[_vmem()],
        out_specs=_vmem(),
        out_shape=jax.ShapeDtypeStruct(v.shape, f32),
        scratch_shapes=[pltpu.VMEM((N_DEV, rows, LANES), f32), pltpu.SemaphoreType.DMA((N_DEV - 1,)), pltpu.SemaphoreType.DMA((N_DEV - 1,))],
    )(v)


def _adamw_math(w, g, m, v):
    m = ADAM_B1 * m + (1.0 - ADAM_B1) * g
    v = ADAM_B2 * v + (1.0 - ADAM_B2) * (g * g)
    m_hat = m / (1.0 - ADAM_B1 ** ADAM_STEP)
    v_hat = v / (1.0 - ADAM_B2 ** ADAM_STEP)
    delta = -ADAM_LR * (m_hat / (jnp.sqrt(v_hat) + ADAM_EPS) + ADAM_WD * w)
    return delta, m, v


def _adamw(w, g, m, v, *, name):
    R, C = w.shape
    tr = _div(R, 128, SUBLANES)

    def body(w_ref, g_ref, m_ref, v_ref, go_ref, d_ref, mo_ref, vo_ref):
        gv = g_ref[...]
        d, mn, vn = _adamw_math(w_ref[...], gv, m_ref[...], v_ref[...])
        go_ref[...] = gv
        d_ref[...] = d
        mo_ref[...] = mn
        vo_ref[...] = vn

    spec = pl.BlockSpec((tr, C), lambda r: (r, 0))
    return pl.pallas_call(
        body,
        name=name,
        grid=(R // tr,),
        in_specs=[spec] * 4,
        out_specs=[spec] * 4,
        out_shape=[jax.ShapeDtypeStruct((R, C), f32)] * 4,
        compiler_params=pltpu.CompilerParams(dimension_semantics=("parallel",)),
    )(w, g, m, v)


def _pack(arrs):
    flat = jnp.concatenate([a.reshape(-1).astype(f32) for a in arrs])
    tile = SUBLANES * LANES
    total = -(-flat.shape[0] // tile) * tile
    return jnp.pad(flat, (0, total - flat.shape[0])).reshape(total // LANES, LANES)


def _unpack(buf, shapes):
    flat = buf.reshape(-1)
    out, off = [], 0
    for s in shapes:
        size = int(np.prod(s))
        out.append(flat[off:off + size].reshape(s))
        off += size
    return out


def kernel(x, p, norm_mix, w_in, lb_logits, hg_norm, rel_bias, w_out, norm_ffn, w_up, conv_w, conv_b, w_down, norm_ple, w_ple_gate, w_ple_proj, final_norm, loss_target, m_norm_mix, m_w_in, m_lb_logits, m_hg_norm, m_rel_bias, m_w_out, m_norm_ffn, m_w_up, m_conv_w, m_conv_b, m_w_down, m_norm_ple, m_w_ple_gate, m_w_ple_proj, m_final_norm, v_norm_mix, v_w_in, v_lb_logits, v_hg_norm, v_rel_bias, v_w_out, v_norm_ffn, v_w_up, v_conv_w, v_conv_b, v_w_down, v_norm_ple, v_w_ple_gate, v_w_ple_proj, v_final_norm):
    S, D = x.shape[1], x.shape[2]
    xv, pv, tgt = x[0], p[0, 0], loss_target[0]
    W = (w_in.shape[2] * N_CHIPS) // 7
    H = W // HEAD_DIM
    F = w_down.shape[1] * N_CHIPS
    tc = _div(F // 2, 1408, LANES)
    nj = F // tc
    jx, jy = lax.axis_index("x"), lax.axis_index("y")
    chip = 2 * jx + jy

    big = [w_in[0], w_out[0], w_up[0], w_down[0], w_ple_gate[0], w_ple_proj[0]]
    big_axes = [1, 0, 1, 0, 0, 1]
    where = jnp.stack([chip, lax.axis_index("c")]).astype(jnp.int32)
    cw_pad = jnp.pad(conv_w[0], ((0, SUBLANES - conv_w.shape[1]), (0, 0)))
    cw_mine = lax.dynamic_update_slice(jnp.zeros((N_CHIPS * SUBLANES, cw_pad.shape[1]), f32), cw_pad, (chip * SUBLANES, 0))
    placed = [_cast_place(b, ax, where, name=f"cast_place_{i}") for i, (b, ax) in enumerate(zip(big, big_axes))]
    shard_shapes = [b.shape for b in big]

    every = (1, 2, 3)

    def job(buf, i, rels, hand_over=False):
        if i is None:
            return _GatherJob(buf, cw_pad.shape, 0, False, rels, hand_over)
        return _GatherJob(buf, shard_shapes[i], big_axes[i], True, rels, hand_over)

    tm = _div(S, 1024, LANES)

    a1 = _rms_fwd(xv, norm_mix, name="rms_mix")
    cs_in = shard_shapes[0][1]
    tn_in = _div(cs_in, 1024, LANES)
    proj, W_in = _mm_while_gathering(a1, placed[0], shard_shapes[0], where, tm=tm, tn=tn_in, name="mm_in")
    qt, pad, kb, ne = _att_dims(S)
    nrel = rel_bias.shape[2]
    nrel_pad = -(-nrel // LANES) * LANES
    onehot = _bias_onehot(qt, ne, nrel_pad)
    rb_pad = jnp.pad(rel_bias[0], ((0, 0), (0, nrel_pad - nrel)))[:, None, :]
    ext = _bias_ext(rb_pad, onehot)
    kp, vp = _kv_padded(proj, W, pad)
    y_att, W_out_i, W_up_i = _attn_fwd(
        proj, kp, vp, ext, W, rider=_gather_rider([placed[1], placed[2]], [job(0, 1, every), job(1, 2, (1, 2))]))
    y_hg, o_hg, st, W_out, W_up_i = _hgrn_fwd(
        proj, lb_logits, hg_norm, W,
        rider=_gather_rider([W_out_i, W_up_i], [job(0, 1, every, True), job(1, 2, (1, 2), True), job(1, 2, (3,))]))
    ycat = jnp.concatenate([y_hg, y_att], axis=1)

    tn_d = _div(D, 512, LANES)
    h1, W_up = _mm(ycat, W_out, dims="nn", tm=tm, tn=tn_d, tk=D, out_dtype=f32, res=xv, name="mm_out",
                   rider=_gather_rider([W_up_i], [job(0, 2, (3,), True)]))
    a2 = _rms_fwd(h1, norm_ffn, name="rms_ffn")
    perm = lambda b: (b % 2) * nj + b // 2
    u, W_down_i, W_pg_i, W_pp_i, cw_all = _mm(
        a2, W_up, dims="nn", tm=tm, tn=tc, tk=D, out_dtype=f32, name="mm_up", b_idx=lambda i, j, k: (k, perm(j)),
        rider=_gather_rider([placed[3], placed[4], placed[5], cw_mine], [job(0, 3, every), job(1, 4, every), job(2, 5, every), job(3, None, every)]))
    ncw = conv_w.shape[1]
    cw_full = jnp.transpose(cw_all.reshape(N_CHIPS, SUBLANES, -1)[:, :ncw], (1, 0, 2)).reshape(ncw, -1)
    cb = conv_b
    mact, W_down, W_pg, W_pp = _convact_fwd(
        u, cw_full, cb, tc, rider=_gather_rider([W_down_i, W_pg_i, W_pp_i], [job(0, 3, every, True), job(1, 4, every, True), job(2, 5, every, True)]))
    tn_w = _div(D, 1024, LANES)
    h2 = _mm(mact, W_down, dims="nn", tm=_div(S, 512, LANES), tn=tn_w, tk=F, out_dtype=f32, res=h1, name="mm_down", cols_outer=True)
    a3 = _rms_fwd(h2, norm_ple, name="rms_ple")
    zg = _mm(a3, W_pg, dims="nn", tm=tm, tn=tn_d, tk=D, out_dtype=f32, name="mm_ple_gate")
    p16 = pv.astype(bf16)
    pp = _mm(p16, W_pp, dims="nn", tm=tm, tn=tn_d, tk=pv.shape[1], out_dtype=f32, name="mm_ple_proj")
    dh3, dzg, dpp, loss_part, d_fn = _tail(h2, zg, pp, tgt, final_norm.reshape(1, D))

    pair = {}

    def siblings(idx, grads):
        return _sibling_rider(grads, [shard_shapes[i] for i in idx], [big_axes[i] for i in idx])

    def pair_up(idx, grads, got):
        for i, d, g in zip(idx, grads, got):
            pair[i] = _pair_sum(d, g, shard_shapes[i], big_axes[i], where, name=f"grad_pair_sum_{i}")

    def reduced_half(i, landed):
        return _chip_sum(pair[i], landed, where, name=f"grad_chip_sum_{i}")

    tk_s = S
    dW_pp = _mm(p16, dpp, dims="tn", tm=pv.shape[1], tn=tn_w, tk=tk_s, out_dtype=bf16, name="mm_d_ple_proj")
    dW_pg = _mm(a3, dzg, dims="tn", tm=tn_w, tn=tn_w, tk=tk_s, out_dtype=bf16, name="mm_d_ple_gate")
    da3, *got = _mm(dzg, W_pg, dims="nt", tm=tm, tn=_div(D, 1024, LANES), tk=D, out_dtype=f32, name="mm_da3",
                    rider=siblings([4, 5], [dW_pg, dW_pp]))
    pair_up([4, 5], [dW_pg, dW_pp], got)
    dh2, dh2b, d_nple = _rms_bwd(h2, norm_ple, da3, dh3, name="rms_ple_bwd")
    dm, land4, land5 = _mm(dh2b, W_down, dims="nt", tm=tm, tn=_div(F, 512, LANES), tk=D, out_dtype=f32, name="mm_dm",
                           rider=_chips_rider([pair[4], pair[5]]))
    dW_down = _mm(mact, dh2b, dims="tn", tm=_div(F, 1408, LANES), tn=tn_d, tk=tk_s, out_dtype=bf16, name="mm_d_down")
    du, dcw_g, dcw_v, dcb_g, dcb_v, got3 = _convact_bwd(u, dm, cw_full, cb, tc, rider=siblings([3], [dW_down]))
    pair_up([3], [dW_down], [got3])
    dW_up, land3 = _mm(a2, du, dims="tn", tm=_div(D, 1024, LANES), tn=tc, tk=tk_s, out_dtype=bf16, name="mm_d_up",
                       o_idx=lambda i, j, k: (i, perm(j)), rider=_chips_rider([pair[3]]))
    da2, got2 = _mm(du, W_up, dims="nt", tm=tm, tn=D, tk=tc, out_dtype=f32, name="mm_da2",
                    b_idx=lambda i, j, k: (j, perm(k)), rider=siblings([2], [dW_up]))
    pair_up([2], [dW_up], [got2])
    dh1, dh1b, d_nffn = _rms_bwd(h1, norm_ffn, da2, dh2, name="rms_ffn_bwd")
    dycat = _mm(dh1b, W_out, dims="nt", tm=tm, tn=_div(D, 1024, LANES), tk=D, out_dtype=f32, name="mm_dycat")
    dW_out = _mm(ycat, dh1b, dims="tn", tm=tn_w, tn=tn_w, tk=tk_s, out_dtype=bf16, name="mm_d_out")
    dq_att, dk_att, dv_att, dbias, land2, got1 = _attn_bwd(
        proj, kp, vp, ext, dycat, W, rider=_both(_chips_rider([pair[2]], rels=(1, 2)), siblings([1], [dW_out])))
    pair_up([1], [dW_out], [got1])
    d_hg4, d_lbl, d_hgn, land2 = _hgrn_bwd(proj, o_hg, st, dycat, lb_logits, hg_norm, W,
                                           rider=_chips_rider([pair[2]], rels=(3,), landing=[land2]))
    d_rb = _bias_bwd(dbias, onehot)[:, 0, :nrel]
    dproj = [d_hg4, dq_att, dk_att, dv_att]
    tw = _div(W, 1024, LANES)
    dW_in, land1 = _mm(a1, dproj, dims="tn", tm=tn_w, tn=_div(W, 512, LANES), tk=tk_s, out_dtype=bf16, name="mm_d_in",
                       rider=_chips_rider([pair[1]]))
    pair_up([0], [dW_in], _run_rider(siblings([0], [dW_in]), name="grads_to_sibling_0"))
    halves = [reduced_half(i, land) for i, land in ((1, land1), (2, land2), (3, land3), (4, land4), (5, land5))]
    da1, land0, *g_rest = _mm(dproj, W_in, dims="nt", tm=tm, tn=D, tk=tw, out_dtype=f32, name="mm_da1",
                              rider=_both(_chips_rider([pair[0]]), _join_rider(halves)))
    grad_x, _, d_nmix = _rms_bwd(xv, norm_mix, da1, dh1, name="rms_mix_bwd")
    g_big = list(_run_rider(_join_rider([reduced_half(0, land0)]), name="grads_join_w_in")) + list(g_rest)

    d_cw = jnp.concatenate([dcw_g, dcw_v], axis=1)
    d_cb = jnp.concatenate([dcb_g, dcb_v], axis=1)
    small_parts = [loss_part[:, :1], d_nmix, d_lbl, d_hgn, d_rb, d_nffn, d_cw, d_cb, d_nple, d_fn]
    small_shapes = [(1, 1), (1, D), lb_logits.shape, hg_norm.shape, (H, nrel), (1, D), (ncw, 2 * F), (1, 2 * F), (1, D), (1, D)]
    red = _unpack(_allreduce_small(_pack(small_parts)), small_shapes)
    loss = red[0].reshape(())
    g_nmix, g_lbl, g_hgn, g_rb, g_nffn, g_cw_all, g_cb, g_nple, g_fn = red[1:]
    csh = conv_w.shape[2]
    g_cw = lax.dynamic_slice(g_cw_all, (0, chip * csh), (ncw, csh))
    small_g = [g_nmix, g_lbl, g_hgn, g_rb[None], g_nffn, g_cw[None], g_cb, g_nple, g_fn.reshape(D)]
    small_w = [norm_mix, lb_logits, hg_norm, rel_bias, norm_ffn, conv_w, conv_b, norm_ple, final_norm]
    small_m = [m_norm_mix, m_lb_logits, m_hg_norm, m_rel_bias, m_norm_ffn, m_conv_w, m_conv_b, m_norm_ple, m_final_norm]
    small_v = [v_norm_mix, v_lb_logits, v_hg_norm, v_rel_bias, v_norm_ffn, v_conv_w, v_conv_b, v_norm_ple, v_final_norm]
    shapes_s = [w.shape for w in small_w]
    _, sd, sm, sv = _adamw(_pack(small_w), _pack(small_g), _pack(small_m), _pack(small_v), name="adamw_small")
    small_g = [g.reshape(s) for g, s in zip(small_g, shapes_s)]
    small_d, small_nm, small_nv = _unpack(sd, shapes_s), _unpack(sm, shapes_s), _unpack(sv, shapes_s)

    big_m = [m_w_in[0], m_w_out[0], m_w_up[0], m_w_down[0], m_w_ple_gate[0], m_w_ple_proj[0]]
    big_v = [v_w_in[0], v_w_out[0], v_w_up[0], v_w_down[0], v_w_ple_gate[0], v_w_ple_proj[0]]
    big_g, big_d, big_nm, big_nv = [], [], [], []
    for i in range(6):
        g_, d_, m_, v_ = _adamw(big[i], g_big[i], big_m[i], big_v[i], name=f"adamw_{i}")
        big_g.append(g_[None])
        big_d.append(d_[None])
        big_nm.append(m_[None])
        big_nv.append(v_[None])
    g_big = big_g

    def order(sm_list, bg_list):
        s, b = sm_list, bg_list
        return [s[0], b[0], s[1], s[2], s[3], b[1], s[4], b[2], s[5], s[6], b[3], s[7], b[4], b[5], s[8]]

    return (loss, grad_x[None], *order(small_g, g_big), *order(small_d, big_d), *order(small_nm, big_nm), *order(small_nv, big_nv))
```

```python
import functools
from typing import Callable, NamedTuple

import jax
import jax.numpy as jnp
import numpy as np
from jax import lax
from jax.experimental import pallas as pl
from jax.experimental.pallas import tpu as pltpu

f32 = jnp.float32
bf16 = jnp.bfloat16

CHUNK = 64
HEAD_DIM = 128
LEFT_CHUNKS = 8
REL_CLIP = 128
EPS = 1e-6
HG_SUB = 16
HG_HEADS_PER_STEP = 8
ATT_Q_ROWS = 256
ATT_HEADS_PER_STEP = 2
ADAM_LR, ADAM_B1, ADAM_B2, ADAM_EPS, ADAM_WD, ADAM_STEP = 0.001, 0.9, 0.999, 1e-08, 0.01, 10
LANES = 128
SUBLANES = 8
N_CHIPS = 4
N_DEV = 8
RDMA_PIECES = 4
MESH = pl.DeviceIdType.MESH
NEG = float(np.finfo(np.float32).min)

NN = (((1,), (0,)), ((), ()))
NT = (((1,), (1,)), ((), ()))
TN = (((0,), (0,)), ((), ()))
HIGHEST = lax.Precision.HIGHEST


def _dot(a, b, dims, precision=None):
    return lax.dot_general(a, b, dims, preferred_element_type=f32, precision=precision)


def _sigmoid(v):
    return 1.0 / (1.0 + jnp.exp(-v))


def _div(n, pref, mult):
    best = None
    d = mult
    while d <= min(n, pref):
        if n % d == 0:
            best = d
        d += mult
    return best if best is not None else n


def _hbm():
    return pl.BlockSpec(memory_space=pltpu.HBM)


def _vmem():
    return pl.BlockSpec(memory_space=pltpu.VMEM)


def _mm(a, b, *, dims, tm, tn, tk, out_dtype, name, res=None, a_idx=None, b_idx=None, o_idx=None, out_shape=None, rider=None,
        cols_outer=False):
    a_list = list(a) if isinstance(a, (list, tuple)) else [a]
    b_list = list(b) if isinstance(b, (list, tuple)) else [b]
    na, nb = len(a_list), len(b_list)
    if dims == "nn":
        assert na == 1 and nb == 1
        (M, K), (_, N) = a.shape, b.shape
        a_blk, b_blk, dn = (tm, tk), (tk, tn), NN
        a_def, b_def = (lambda i, j, k: (i, k)), (lambda i, j, k: (k, j))
    elif dims == "nt":
        assert nb == 1
        M, K, N = a_list[0].shape[0], sum(p.shape[1] for p in a_list), b.shape[0]
        a_blk, b_blk, dn = (tm, tk), (tn, tk), NT
        a_def, b_def = (lambda i, j, k: (i, k)), (lambda i, j, k: (j, k))
    else:
        assert na == 1
        (K, M), N = a.shape, sum(p.shape[1] for p in b_list)
        a_blk, b_blk, dn = (tk, tm), (tk, tn), TN
        a_def, b_def = (lambda i, j, k: (k, i)), (lambda i, j, k: (k, j))
    assert M % tm == 0 and N % tn == 0 and K % tk == 0, (name, M, N, K, tm, tn, tk)
    nk = K // tk
    has_res = res is not None
    counts = [p.shape[1] // tk for p in a_list] if na > 1 else [p.shape[1] // tn for p in b_list]
    starts = [sum(counts[:p]) for p in range(len(counts))]

    def body(*refs):
        a_refs, b_refs, rest = refs[:na], refs[na:na + nb], refs[na + nb:]
        res_ref = rest[0] if has_res else None
        o_ref = rest[1] if has_res else rest[0]
        k = pl.program_id(2)
        j = pl.program_id(0) if cols_outer else pl.program_id(1)

        def finish(acc):
            if has_res:
                acc = acc + res_ref[...]
            o_ref[...] = acc.astype(out_dtype)

        def consume(part):
            if nk == 1:
                finish(part)
                return
            acc_ref = refs[-1]

            @pl.when(k == 0)
            def _():
                acc_ref[...] = part

            @pl.when(k > 0)
            def _():
                acc_ref[...] += part

            @pl.when(k == nk - 1)
            def _():
                finish(acc_ref[...])

        if len(counts) == 1:
            consume(_dot(a_refs[0][...].astype(bf16), b_refs[0][...].astype(bf16), dn))
        else:
            at = k if na > 1 else j
            for p, (s, n) in enumerate(zip(starts, counts)):
                a_ref, b_ref = a_refs[p if na > 1 else 0], b_refs[p if nb > 1 else 0]
                pl.when((at >= s) & (at < s + n))(
                    functools.partial(lambda ar, br: consume(_dot(ar[...].astype(bf16), br[...].astype(bf16), dn)), a_ref, b_ref))

    def ordered(custom, default):
        f = custom or default
        return (lambda j, i, k: f(i, j, k)) if cols_outer else f

    def piece_map(default, p):
        s, n = starts[p], counts[p]
        if na > 1:
            return lambda i, j, k: default(i, j, jnp.clip(k - s, 0, n - 1))
        return lambda i, j, k: default(i, jnp.clip(j - s, 0, n - 1), k)

    o_map = ordered(o_idx, lambda i, j, k: (i, j))
    a_specs = [pl.BlockSpec(a_blk, ordered(a_idx, piece_map(a_def, p) if na > 1 else a_def)) for p in range(na)]
    b_specs = [pl.BlockSpec(b_blk, ordered(b_idx, piece_map(b_def, p) if nb > 1 else b_def)) for p in range(nb)]
    in_specs = a_specs + b_specs
    args = a_list + b_list
    if has_res:
        in_specs.append(pl.BlockSpec((tm, tn), o_map))
        args.append(res)
    outs = _grid_call(
        body,
        rider,
        name=name,
        grid=(N // tn, M // tm, nk) if cols_outer else (M // tm, N // tn, nk),
        in_specs=in_specs,
        out_specs=[pl.BlockSpec((tm, tn), o_map)],
        out_shape=[jax.ShapeDtypeStruct(out_shape or (M, N), out_dtype)],
        scratch_shapes=[pltpu.VMEM((tm, tn), f32)] if nk > 1 else [],
        args=args,
    )
    return outs[0] if rider is None else outs


def _rms_fwd(xv, g, *, name):
    S, D = xv.shape
    ts = _div(S, 512, SUBLANES)

    def body(x_ref, g_ref, o_ref):
        v = x_ref[...]
        r = lax.rsqrt(jnp.mean(v * v, axis=-1, keepdims=True) + EPS)
        o_ref[...] = (v * r * g_ref[...]).astype(bf16)

    return pl.pallas_call(
        body,
        name=name,
        grid=(S // ts,),
        in_specs=[pl.BlockSpec((ts, D), lambda i: (i, 0)), pl.BlockSpec((1, D), lambda i: (0, 0))],
        out_specs=pl.BlockSpec((ts, D), lambda i: (i, 0)),
        out_shape=jax.ShapeDtypeStruct((S, D), bf16),
        compiler_params=pltpu.CompilerParams(dimension_semantics=("parallel",)),
    )(xv, g)


def _rms_bwd(xv, g, dy, dres, *, name, rider=None):
    S, D = xv.shape
    ts = _div(S, 256, SUBLANES)

    def body(x_ref, g_ref, dy_ref, dres_ref, dx_ref, dxb_ref, dg_ref):
        i = pl.program_id(0)
        v = x_ref[...]
        r = lax.rsqrt(jnp.mean(v * v, axis=-1, keepdims=True) + EPS)
        vn = v * r
        d = dy_ref[...]
        part = jnp.sum(d * vn, axis=0, keepdims=True)

        @pl.when(i == 0)
        def _():
            dg_ref[...] = part

        @pl.when(i > 0)
        def _():
            dg_ref[...] += part

        t = d * g_ref[...]
        dx = dres_ref[...] + r * (t - vn * jnp.mean(t * vn, axis=-1, keepdims=True))
        dx_ref[...] = dx
        dxb_ref[...] = dx.astype(bf16)

    row = pl.BlockSpec((ts, D), lambda i: (i, 0))
    vec = pl.BlockSpec((1, D), lambda i: (0, 0))
    return _grid_call(
        body,
        rider,
        name=name,
        grid=(S // ts,),
        in_specs=[row, vec, row, row],
        out_specs=[row, row, vec],
        out_shape=[jax.ShapeDtypeStruct((S, D), f32), jax.ShapeDtypeStruct((S, D), bf16), jax.ShapeDtypeStruct((1, D), f32)],
        scratch_shapes=[],
        args=(xv, g, dy, dres),
    )


def _tail(h2, zg, pp, tgt, fn):
    S, D = h2.shape
    ts = _div(S, 256, SUBLANES)

    def body(h_ref, z_ref, p_ref, t_ref, fn_ref, dh_ref, dz_ref, dp_ref, loss_ref, dfn_ref):
        i = pl.program_id(0)
        sg = _sigmoid(z_ref[...])
        ppv = p_ref[...]
        h3 = h_ref[...] + sg * ppv
        r = lax.rsqrt(jnp.mean(h3 * h3, axis=-1, keepdims=True) + EPS)
        hn = h3 * r
        fnv = fn_ref[...]
        e = hn * fnv - t_ref[...]
        lpart = 0.5 * jnp.sum(jnp.mean(e * e, axis=-1, keepdims=True), axis=0, keepdims=True)
        dy = e * (1.0 / D)
        dpart = jnp.sum(dy * hn, axis=0, keepdims=True)

        @pl.when(i == 0)
        def _():
            loss_ref[...] = jnp.broadcast_to(lpart, loss_ref.shape)
            dfn_ref[...] = dpart

        @pl.when(i > 0)
        def _():
            loss_ref[...] += jnp.broadcast_to(lpart, loss_ref.shape)
            dfn_ref[...] += dpart

        t = dy * fnv
        dh3 = r * (t - hn * jnp.mean(t * hn, axis=-1, keepdims=True))
        dh_ref[...] = dh3
        dz_ref[...] = (dh3 * ppv * sg * (1.0 - sg)).astype(bf16)
        dp_ref[...] = (dh3 * sg).astype(bf16)

    row = pl.BlockSpec((ts, D), lambda i: (i, 0))
    vec = pl.BlockSpec((1, D), lambda i: (0, 0))
    one = pl.BlockSpec((1, LANES), lambda i: (0, 0))
    return pl.pallas_call(
        body,
        name="ple_final_loss",
        grid=(S // ts,),
        in_specs=[row, row, row, row, vec],
        out_specs=[row, row, row, one, vec],
        out_shape=[
            jax.ShapeDtypeStruct((S, D), f32),
            jax.ShapeDtypeStruct((S, D), bf16),
            jax.ShapeDtypeStruct((S, D), bf16),
            jax.ShapeDtypeStruct((1, LANES), f32),
            jax.ShapeDtypeStruct((1, D), f32),
        ],
        compiler_params=pltpu.CompilerParams(dimension_semantics=("arbitrary",)),
    )(h2, zg, pp, tgt, fn)


def _shift_down(v, halo, k):
    r = pltpu.roll(v, k, 0)
    hr = pltpu.roll(halo, k, 0)
    row = lax.broadcasted_iota(jnp.int32, hr.shape, 0)
    top = jnp.where(row < k, hr, r[0:SUBLANES])
    return top if v.shape[0] == SUBLANES else jnp.concatenate([top, r[SUBLANES:]], axis=0)


def _shift_up(v, halo, k):
    n = v.shape[0]
    r = pltpu.roll(v, n - k, 0)
    hr = pltpu.roll(halo, SUBLANES - k, 0)
    row = lax.broadcasted_iota(jnp.int32, hr.shape, 0)
    bot = jnp.where(row >= SUBLANES - k, hr, r[n - SUBLANES:])
    return jnp.concatenate([r[: n - SUBLANES], bot], axis=0)


def _conv_specs(S, F, tc, ts):
    nj = F // tc
    rows8 = ts // SUBLANES
    main = pl.BlockSpec((ts, 2 * tc), lambda j, i: (i, j))
    prev = pl.BlockSpec((SUBLANES, 2 * tc), lambda j, i: (jnp.maximum(i * rows8 - 1, 0), j))
    nxt = pl.BlockSpec((SUBLANES, 2 * tc), lambda j, i: (jnp.minimum((i + 1) * rows8, S // SUBLANES - 1), j))
    wg = pl.BlockSpec((3, tc), lambda j, i: (0, j))
    wv = pl.BlockSpec((3, tc), lambda j, i: (0, nj + j))
    bg = pl.BlockSpec((1, tc), lambda j, i: (0, j))
    bv = pl.BlockSpec((1, tc), lambda j, i: (0, nj + j))
    half = pl.BlockSpec((ts, tc), lambda j, i: (i, j))
    return nj, main, prev, nxt, wg, wv, bg, bv, half


def _conv_pre(u_ref, h_ref, wg_ref, wv_ref, bg_ref, bv_ref, tc):
    i = pl.program_id(1)
    u = u_ref[...]
    halo = jnp.where(i == 0, 0.0, h_ref[...])
    u1 = _shift_down(u, halo, 1)
    u2 = _shift_down(u, halo, 2)
    w = jnp.concatenate([wg_ref[...], wv_ref[...]], axis=1)
    b = jnp.concatenate([bg_ref[...], bv_ref[...]], axis=1)
    uc = b + w[0:1] * u2 + w[1:2] * u1 + w[2:3] * u
    return u, u1, u2, uc[:, :tc], uc[:, tc:]


def _convact_fwd(u, cw, cb, tc, rider=None):
    S, F2 = u.shape
    F = F2 // 2
    ts = _div(S, 512, SUBLANES)
    nj, main, prev, nxt, wg, wv, bg, bv, half = _conv_specs(S, F, tc, ts)

    def body(u_ref, h_ref, wg_ref, wv_ref, bg_ref, bv_ref, m_ref):
        _, _, _, g, v = _conv_pre(u_ref, h_ref, wg_ref, wv_ref, bg_ref, bv_ref, tc)
        m_ref[...] = (g * _sigmoid(g) * v).astype(bf16)

    outs = _grid_call(
        body,
        rider,
        name="convact_fwd",
        grid=(nj, S // ts),
        in_specs=[main, prev, wg, wv, bg, bv],
        out_specs=[half],
        out_shape=[jax.ShapeDtypeStruct((S, F), bf16)],
        scratch_shapes=[],
        args=(u, u, cw, cw, cb, cb),
    )
    return outs[0] if rider is None else outs


def _convact_bwd(u, dm, cw, cb, tc, rider=None):
    S, F2 = u.shape
    F = F2 // 2
    ts = _div(S, 512, SUBLANES)
    nj, main, prev, nxt, wg, wv, bg, bv, half = _conv_specs(S, F, tc, ts)
    ni = S // ts
    half_nxt = pl.BlockSpec((SUBLANES, tc), lambda j, i: (jnp.minimum((i + 1) * (ts // SUBLANES), S // SUBLANES - 1), j))

    def act_bwd(g, v, d):
        sg = _sigmoid(g)
        return jnp.concatenate([d * v * sg * (1.0 + g * (1.0 - sg)), d * g * sg], axis=1)

    def body(u_ref, h_ref, n_ref, dm_ref, dmn_ref, wg_ref, wv_ref, bg_ref, bv_ref, du_ref, dwg_ref, dwv_ref, dbg_ref, dbv_ref):
        i = pl.program_id(1)
        u0, u1, u2, g, v = _conv_pre(u_ref, h_ref, wg_ref, wv_ref, bg_ref, bv_ref, tc)
        duc = act_bwd(g, v, dm_ref[...])
        w = jnp.concatenate([wg_ref[...], wv_ref[...]], axis=1)
        b = jnp.concatenate([bg_ref[...], bv_ref[...]], axis=1)
        un = n_ref[...]
        tail = u0[ts - SUBLANES:]
        ucn = b + w[0:1] * _shift_down(un, tail, 2) + w[1:2] * _shift_down(un, tail, 1) + w[2:3] * un
        ducn = jnp.where(i == ni - 1, 0.0, act_bwd(ucn[:, :tc], ucn[:, tc:], dmn_ref[...]))
        du_ref[...] = (w[2:3] * duc + w[1:2] * _shift_up(duc, ducn, 1) + w[0:1] * _shift_up(duc, ducn, 2)).astype(bf16)
        db = jnp.sum(duc, axis=0, keepdims=True)
        dw = jnp.concatenate(
            [jnp.sum(duc * u2, axis=0, keepdims=True), jnp.sum(duc * u1, axis=0, keepdims=True), jnp.sum(duc * u0, axis=0, keepdims=True)],
            axis=0,
        )

        @pl.when(i == 0)
        def _():
            dwg_ref[...] = dw[:, :tc]
            dwv_ref[...] = dw[:, tc:]
            dbg_ref[...] = db[:, :tc]
            dbv_ref[...] = db[:, tc:]

        @pl.when(i > 0)
        def _():
            dwg_ref[...] += dw[:, :tc]
            dwv_ref[...] += dw[:, tc:]
            dbg_ref[...] += db[:, :tc]
            dbv_ref[...] += db[:, tc:]

    w_out = pl.BlockSpec((3, tc), lambda j, i: (0, j))
    b_out = pl.BlockSpec((1, tc), lambda j, i: (0, j))
    return _grid_call(
        body,
        rider,
        name="convact_bwd",
        grid=(nj, ni),
        in_specs=[main, prev, nxt, half, half_nxt, wg, wv, bg, bv],
        out_specs=[main, w_out, w_out, b_out, b_out],
        out_shape=[
            jax.ShapeDtypeStruct((S, F2), bf16),
            jax.ShapeDtypeStruct((3, F), f32),
            jax.ShapeDtypeStruct((3, F), f32),
            jax.ShapeDtypeStruct((1, F), f32),
            jax.ShapeDtypeStruct((1, F), f32),
        ],
        scratch_shapes=[],
        args=(u, u, u, dm, dm, cw, cw, cb, cb),
    )


def _hg_gates(qp, fp, lbl):
    lb = _sigmoid(lbl[0:1] - lbl[1:2])
    sig = _sigmoid(fp)
    sigm = _sigmoid(-fp)
    f = lb + (1.0 - lb) * sig
    k = (1.0 - lb) * sigm
    sq = _sigmoid(qp)
    qf = qp * sq
    row = lax.broadcasted_iota(jnp.int32, (CHUNK, CHUNK), 0)
    col = lax.broadcasted_iota(jnp.int32, (CHUNK, CHUNK), 1)
    b = _dot((row >= col).astype(f32), jnp.log(f), NN, HIGHEST)
    return lb, sig, sigm, f, k, sq, qf, b


def _hg_block(qf, k, b, I):
    r0, n = I * HG_SUB, (I + 1) * HG_SUB
    base = b[r0 - 1:r0] if I > 0 else jnp.zeros_like(b[0:1])
    eq = jnp.exp(b[r0:n] - base)
    ek = jnp.exp(base - b[0:n])
    qt16 = (qf[r0:n] * eq).astype(bf16)
    kt16 = (k[0:n] * ek).astype(bf16)
    row = lax.broadcasted_iota(jnp.int32, (HG_SUB, n), 0) + r0
    col = lax.broadcasted_iota(jnp.int32, (HG_SUB, n), 1)
    return r0, n, eq, ek, qt16, kt16, col <= row


def _per_head(v, hp, fn):
    return jnp.concatenate([fn(v[:, p * HEAD_DIM:(p + 1) * HEAD_DIM]) for p in range(hp)], axis=1)


def _head_mean(v, hp):
    return _per_head(v, hp, lambda t: jnp.broadcast_to(jnp.mean(t, axis=-1, keepdims=True), t.shape))


def _pad_rows(v, rows):
    return v if v.shape[0] == rows else jnp.concatenate([v, jnp.zeros((rows - v.shape[0], v.shape[1]), v.dtype)], axis=0)


def _hgrn_specs(S, W, hp, reverse):
    nc = S // CHUNK
    ngrp = W // (hp * HEAD_DIM)
    cw = hp * HEAD_DIM
    cidx = (lambda c: nc - 1 - c) if reverse else (lambda c: c)

    def proj(off):
        return pl.BlockSpec((CHUNK, cw), lambda h, c: (cidx(c), off * ngrp + h))

    act = pl.BlockSpec((CHUNK, cw), lambda h, c: (cidx(c), h))
    lbl = pl.BlockSpec((2, cw), lambda h, c: (0, h))
    gn = pl.BlockSpec((1, HEAD_DIM), lambda h, c: (0, 0))
    st = pl.BlockSpec((hp, 1, HEAD_DIM, HEAD_DIM), lambda h, c: (h, cidx(c), 0, 0))
    return nc, ngrp, proj, act, lbl, gn, st


def _grid_call(body, rider, *, name, grid, in_specs, out_specs, out_shape, scratch_shapes, args):
    params = pltpu.CompilerParams(dimension_semantics=("arbitrary",) * len(grid))
    if rider is None:
        return pl.pallas_call(body, name=name, grid=grid, in_specs=in_specs, out_specs=out_specs, out_shape=out_shape,
                              scratch_shapes=scratch_shapes, compiler_params=params)(*args)
    n_in, n_out, n_scr = len(in_specs), len(out_specs), len(scratch_shapes)
    ri, ro = len(rider.ins), len(rider.out_shapes)

    def wrapped(*refs):
        a, b = n_in, n_in + ri
        c, d = b + n_out, b + n_out + ro
        e = d + n_scr
        ids = [pl.program_id(t) for t in range(len(grid))]
        is_first = functools.reduce(jnp.logical_and, [i == 0 for i in ids])
        is_last = functools.reduce(jnp.logical_and, [i == g - 1 for i, g in zip(ids, grid)])

        @pl.when(is_first)
        def _():
            rider.first(refs[a:b], refs[c:d], refs[e:])

        body(*refs[:a], *refs[b:c], *refs[d:e])

        @pl.when(is_last)
        def _():
            rider.last(refs[a:b], refs[c:d], refs[e:])

    return pl.pallas_call(
        wrapped,
        name=name,
        grid=grid,
        in_specs=list(in_specs) + [_hbm()] * ri,
        out_specs=list(out_specs) + [_hbm()] * ro,
        out_shape=list(out_shape) + list(rider.out_shapes),
        input_output_aliases={n_in + i: n_out + o for i, o in rider.aliases.items()},
        scratch_shapes=list(scratch_shapes) + list(rider.sems),
        compiler_params=params,
    )(*args, *rider.ins)


def _hgrn_fwd(proj, lb_logits, hg_norm, W, rider=None):
    S = proj.shape[0]
    hp = min(HG_HEADS_PER_STEP, W // HEAD_DIM)
    nc, ngrp, pspec, act, lbl_spec, gn_spec, st_spec = _hgrn_specs(S, W, hp, False)
    nb = CHUNK // HG_SUB

    def body(q_ref, f_ref, i_ref, g_ref, lbl_ref, gn_ref, y_ref, o_ref, st_ref, state):
        c = pl.program_id(1)

        @pl.when(c == 0)
        def _():
            state[...] = jnp.zeros_like(state)

        _, _, _, _, k, _, qf, b = _hg_gates(q_ref[...], f_ref[...], lbl_ref[...])
        v16 = i_ref[...].astype(bf16)
        a16 = (qf * jnp.exp(b)).astype(bf16)
        bl = b[CHUNK - 1:CHUNK]
        kd16 = (k * jnp.exp(bl - b)).astype(bf16)
        ebl = jnp.exp(bl)
        blocks = [_hg_block(qf, k, b, I) for I in range(nb)]
        heads = [slice(p * HEAD_DIM, (p + 1) * HEAD_DIM) for p in range(hp)]
        st_prev = state[...]
        st_ref[:, 0] = st_prev
        inter = [_dot(a16[:, sl], st_prev[p].astype(bf16), NT) for p, sl in enumerate(heads)]
        scores = [[_dot(qt16[:, sl], kt16[:, sl], NT) for sl in heads] for _, _, _, _, qt16, kt16, _ in blocks]
        intra = [[_dot(jnp.where(blk[6], sc, 0.0).astype(bf16), v16[0:blk[1], sl], NN) for sc, sl in zip(scs, heads)]
                 for blk, scs in zip(blocks, scores)]
        st_new = [st_prev[p] * ebl[:, sl] + _dot(v16[:, sl], kd16[:, sl], TN) for p, sl in enumerate(heads)]
        state[...] = jnp.stack(st_new)
        o = jnp.concatenate([inter[p] + jnp.concatenate([rows[p] for rows in intra], axis=0) for p in range(hp)], axis=1)
        o_ref[...] = o
        r = lax.rsqrt(_head_mean(o * o, hp) + EPS)
        gp = g_ref[...]
        y_ref[...] = (o * r * jnp.tile(gn_ref[...], (1, hp)) * (gp * _sigmoid(gp))).astype(bf16)

    H = W // HEAD_DIM
    return _grid_call(
        body,
        rider,
        name="hgrn_fwd",
        grid=(ngrp, nc),
        in_specs=[pspec(0), pspec(1), pspec(2), pspec(3), lbl_spec, gn_spec],
        out_specs=[act, act, st_spec],
        out_shape=[
            jax.ShapeDtypeStruct((S, W), bf16),
            jax.ShapeDtypeStruct((S, W), f32),
            jax.ShapeDtypeStruct((H, nc, HEAD_DIM, HEAD_DIM), f32),
        ],
        scratch_shapes=[pltpu.VMEM((hp, HEAD_DIM, HEAD_DIM), f32)],
        args=(proj, proj, proj, proj, lb_logits, hg_norm),
    )


def _hgrn_bwd(proj, o_hg, st, dycat, lb_logits, hg_norm, W, rider=None):
    S = proj.shape[0]
    hp = min(HG_HEADS_PER_STEP, W // HEAD_DIM)
    assert hp * HEAD_DIM == W, "the four gradients share one output block: all heads in one grid step"
    nc, ngrp, pspec, act, lbl_spec, gn_spec, st_spec = _hgrn_specs(S, W, hp, True)
    nb = CHUNK // HG_SUB

    def body(q_ref, f_ref, i_ref, g_ref, o_ref, st_ref, dy_ref, lbl_ref, gn_ref,
             d4_ref, dlbl_ref, dgn_ref, dstate):
        cw_ = hp * HEAD_DIM
        dq_ref, df_ref, di_ref, dg_ref = (d4_ref.at[:, pl.ds(t * cw_, cw_)] for t in range(4))
        h = pl.program_id(0)
        c = pl.program_id(1)

        @pl.when(c == 0)
        def _():
            dstate[...] = jnp.zeros_like(dstate)
            dlbl_ref[...] = jnp.zeros_like(dlbl_ref)

        @pl.when((c == 0) & (h == 0))
        def _():
            dgn_ref[...] = jnp.zeros_like(dgn_ref)

        row = lax.broadcasted_iota(jnp.int32, (CHUNK, CHUNK), 0)
        col = lax.broadcasted_iota(jnp.int32, (CHUNK, CHUNK), 1)
        upper = (col >= row).astype(f32)
        last_row = lax.broadcasted_iota(jnp.int32, (CHUNK, hp * HEAD_DIM), 0) == CHUNK - 1

        qp, fp = q_ref[...], f_ref[...]
        lb, sig, sigm, f, k, sq, qf, b = _hg_gates(qp, fp, lbl_ref[...])
        v16 = i_ref[...].astype(bf16)
        gnv = jnp.tile(gn_ref[...], (1, hp))
        gp = g_ref[...]
        sgg = _sigmoid(gp)
        gate = gp * sgg
        o = o_ref[...]
        r = lax.rsqrt(_head_mean(o * o, hp) + EPS)
        on = o * r
        dy = dy_ref[...]
        dg_ref[...] = (dy * on * gnv * (sgg * (1.0 + gp * (1.0 - sgg)))).astype(bf16)
        dgn_wide = jnp.sum(dy * on * gate, axis=0, keepdims=True)
        dgn_ref[...] += functools.reduce(jnp.add, [dgn_wide[:, p * HEAD_DIM:(p + 1) * HEAD_DIM] for p in range(hp)])
        don = dy * gnv * gate
        do16 = (r * (don - on * _head_mean(don * on, hp))).astype(bf16)
        eb = jnp.exp(b)
        A = qf * eb
        a16 = A.astype(bf16)
        bl = b[CHUNK - 1:CHUNK]
        ebl = jnp.exp(bl)
        ekd = jnp.exp(bl - b)
        kd = k * ekd
        kd16 = kd.astype(bf16)
        blocks = [_hg_block(qf, k, b, I) for I in range(nb)]

        heads = [slice(p * HEAD_DIM, (p + 1) * HEAD_DIM) for p in range(hp)]
        st_prev = st_ref[:, 0]
        dst_new = dstate[...]
        st16, dst16 = st_prev.astype(bf16), dst_new.astype(bf16)
        dA_h = [_dot(do16[:, sl], st16[p], NN) for p, sl in enumerate(heads)]
        dkd_h = [_dot(v16[:, sl], dst16[p], NN) for p, sl in enumerate(heads)]
        dv_h = [_dot(kd16[:, sl], dst16[p], NT) for p, sl in enumerate(heads)]
        dstate[...] = jnp.stack([dst_new[p] * ebl[:, sl] + _dot(do16[:, sl], a16[:, sl], TN) for p, sl in enumerate(heads)])
        dbl_h = [jnp.sum(dst_new[p] * st_prev[p], axis=0, keepdims=True) for p in range(hp)]
        sc_h = [[jnp.where(mask, _dot(qt16[:, sl], kt16[:, sl], NT), 0.0).astype(bf16) for sl in heads]
                for _, _, _, _, qt16, kt16, mask in blocks]
        dsc_h = [[jnp.where(mask, _dot(do16[r0:n, sl], v16[0:n, sl], NT), 0.0).astype(bf16) for sl in heads]
                 for r0, n, _, _, _, _, mask in blocks]
        dvi_h = [[_dot(sc, do16[blk[0]:blk[1], sl], TN) for sc, sl in zip(scs, heads)] for blk, scs in zip(blocks, sc_h)]
        dqt_h = [[_dot(dsc, blk[5][:, sl], NN) for dsc, sl in zip(dscs, heads)] for blk, dscs in zip(blocks, dsc_h)]
        dkt_h = [[_dot(dsc, blk[4][:, sl], TN) for dsc, sl in zip(dscs, heads)] for blk, dscs in zip(blocks, dsc_h)]
        dv_h = [functools.reduce(jnp.add, [dv_h[p]] + [_pad_rows(rows[p], CHUNK) for rows in dvi_h]) for p in range(hp)]

        dA, dkd = jnp.concatenate(dA_h, axis=1), jnp.concatenate(dkd_h, axis=1)
        dbl = jnp.concatenate(dbl_h, axis=1) * ebl + jnp.sum(dkd * kd, axis=0, keepdims=True)
        db = dA * A - dkd * kd + jnp.where(last_row, dbl, 0.0)
        dk = dkd * ekd
        dq_rows, db_rows = [], []
        for I, (r0, n, eq, ek, qt16, kt16, _) in enumerate(blocks):
            dqt, dkt = jnp.concatenate(dqt_h[I], axis=1), jnp.concatenate(dkt_h[I], axis=1)
            dq_rows.append(dqt * eq)
            dk = dk + _pad_rows(dkt * ek, CHUNK)
            db_rows.append(dqt * qt16.astype(f32))
            db = db - _pad_rows(dkt * kt16.astype(f32), CHUNK)
        dqf = dA * eb + jnp.concatenate(dq_rows, axis=0)
        db = db + jnp.concatenate(db_rows, axis=0)
        dlogf = _dot(upper, db, NN, HIGHEST)
        dfg = dlogf / f
        df_ref[...] = ((1.0 - lb) * sig * sigm * (dfg - dk)).astype(bf16)
        dlb = jnp.sum(sigm * (dfg - dk), axis=0, keepdims=True)
        dl0 = dlb * lb * (1.0 - lb)
        dlbl_ref[...] += jnp.concatenate([dl0, -dl0], axis=0)
        dq_ref[...] = (dqf * (sq * (1.0 + qp * (1.0 - sq)))).astype(bf16)
        di_ref[...] = jnp.concatenate(dv_h, axis=1).astype(bf16)

    cw = hp * HEAD_DIM
    dy_spec = pl.BlockSpec((CHUNK, cw), lambda h, c: (nc - 1 - c, h))
    return _grid_call(
        body,
        rider,
        name="hgrn_bwd",
        grid=(ngrp, nc),
        in_specs=[pspec(0), pspec(1), pspec(2), pspec(3), act, st_spec, dy_spec, lbl_spec, gn_spec],
        out_specs=[pl.BlockSpec((CHUNK, 4 * W), lambda h, c: (nc - 1 - c, 0)), lbl_spec, gn_spec],
        out_shape=[jax.ShapeDtypeStruct((S, 4 * W), bf16), jax.ShapeDtypeStruct((2, W), f32), jax.ShapeDtypeStruct((1, HEAD_DIM), f32)],
        scratch_shapes=[pltpu.VMEM((hp, HEAD_DIM, HEAD_DIM), f32)],
        args=(proj, proj, proj, proj, o_hg, st, dycat, lb_logits, hg_norm),
    )


def _att_dims(S):
    qt = ATT_Q_ROWS if S % ATT_Q_ROWS == 0 else CHUNK
    pad = LEFT_CHUNKS * CHUNK
    kb = pad + qt
    return qt, pad, kb, kb + qt


def _kv_padded(proj, W, pad):
    S = proj.shape[0]
    ts = _div(pad, 512, 2 * SUBLANES)
    assert S % ts == 0 and pad % ts == 0
    npad = pad // ts

    def body(k_ref, v_ref, ko_ref, vo_ref):
        i = pl.program_id(0)

        @pl.when(i < npad)
        def _():
            ko_ref[...] = jnp.zeros_like(ko_ref)
            vo_ref[...] = jnp.zeros_like(vo_ref)

        @pl.when(i >= npad)
        def _():
            ko_ref[...] = k_ref[...].astype(bf16)
            vo_ref[...] = v_ref[...].astype(bf16)

    out = pl.BlockSpec((ts, W), lambda i: (i, 0))
    return pl.pallas_call(
        body,
        name="kv_padded",
        grid=((S + pad) // ts,),
        in_specs=[pl.BlockSpec((ts, W), lambda i: (jnp.maximum(i - npad, 0), 5)), pl.BlockSpec((ts, W), lambda i: (jnp.maximum(i - npad, 0), 6))],
        out_specs=[out, out],
        out_shape=[jax.ShapeDtypeStruct((S + pad, W), bf16)] * 2,
        compiler_params=pltpu.CompilerParams(dimension_semantics=("parallel",)),
    )(proj, proj)


def _att_probs(qk, bias, start, qt, pad, kb):
    s = qk * (HEAD_DIM ** -0.5) + bias
    row = lax.broadcasted_iota(jnp.int32, (qt, kb), 0)
    col = lax.broadcasted_iota(jnp.int32, (qt, kb), 1)
    lo = jnp.bitwise_and(row, -CHUNK)
    ok = (col >= lo) & (col < lo + pad + CHUNK) & (col + start >= pad)
    s = jnp.where(ok, s, NEG)
    e = jnp.exp(s - jnp.max(s, axis=-1, keepdims=True))
    return e / jnp.sum(e, axis=-1, keepdims=True)


def _att_bias(ext_row, qt, kb, ne):
    e = jnp.broadcast_to(ext_row, (qt, ne))
    return pltpu.roll(e, ne - qt + 1, 1, stride=1, stride_axis=0)[:, :kb]


def _att_specs(S, H, ah, qt, pad, ne):
    cw = ah * HEAD_DIM
    q_spec = pl.BlockSpec((qt, cw), lambda h, g: (g, 4 * H // ah + h))
    kv_spec = pl.BlockSpec((S + pad, cw), lambda h, g: (0, h))
    ext_spec = pl.BlockSpec((ah, 1, ne), lambda h, g: (h, 0, 0))
    row_spec = pl.BlockSpec((qt, cw), lambda h, g: (g, h))
    return cw, q_spec, kv_spec, ext_spec, row_spec


def _attn_fwd(proj, kp, vp, ext, W, rider=None):
    S = proj.shape[0]
    H = W // HEAD_DIM
    ah = min(ATT_HEADS_PER_STEP, H)
    qt, pad, kb, ne = _att_dims(S)
    cw, q_spec, kv_spec, ext_spec, row_spec = _att_specs(S, H, ah, qt, pad, ne)
    heads = [slice(a * HEAD_DIM, (a + 1) * HEAD_DIM) for a in range(ah)]

    def body(q_ref, k_ref, v_ref, ext_ref, o_ref, bias):
        g = pl.program_id(1)

        @pl.when(g == 0)
        def _():
            for a in range(ah):
                bias[a] = _att_bias(ext_ref[a], qt, kb, ne)

        start = pl.multiple_of(g * qt, qt)
        q16 = q_ref[...].astype(bf16)
        kb16 = k_ref[pl.ds(start, kb), :]
        vb16 = v_ref[pl.ds(start, kb), :]
        qk = [_dot(q16[:, sl], kb16[:, sl], NT) for sl in heads]
        pn = [_att_probs(qk[a], bias[a], start, qt, pad, kb).astype(bf16) for a in range(ah)]
        o_ref[...] = jnp.concatenate([_dot(pn[a], vb16[:, sl], NN) for a, sl in enumerate(heads)], axis=1).astype(bf16)

    outs = _grid_call(
        body,
        rider,
        name="attn_fwd",
        grid=(H // ah, S // qt),
        in_specs=[q_spec, kv_spec, kv_spec, ext_spec],
        out_specs=[row_spec],
        out_shape=[jax.ShapeDtypeStruct((S, W), bf16)],
        scratch_shapes=[pltpu.VMEM((ah, qt, kb), f32)],
        args=(proj, kp, vp, ext),
    )
    return outs[0] if rider is None else outs


def _attn_bwd(proj, kp, vp, ext, dycat, W, rider=None):
    S = proj.shape[0]
    H = W // HEAD_DIM
    ah = min(ATT_HEADS_PER_STEP, H)
    qt, pad, kb, ne = _att_dims(S)
    cw, q_spec, kv_spec, ext_spec, row_spec = _att_specs(S, H, ah, qt, pad, ne)
    heads = [slice(a * HEAD_DIM, (a + 1) * HEAD_DIM) for a in range(ah)]
    scale = HEAD_DIM ** -0.5
    ng = S // qt

    def body(q_ref, k_ref, v_ref, ext_ref, do_ref, dq_ref, dko_ref, dvo_ref, db_ref, bias, dk_ref, dv_ref):
        g = pl.program_id(1)

        @pl.when(g == 0)
        def _():
            for a in range(ah):
                bias[a] = _att_bias(ext_ref[a], qt, kb, ne)
            dk_ref[...] = jnp.zeros_like(dk_ref)
            dv_ref[...] = jnp.zeros_like(dv_ref)
            db_ref[...] = jnp.zeros_like(db_ref)

        start = pl.multiple_of(g * qt, qt)
        q16 = q_ref[...].astype(bf16)
        kb16 = k_ref[pl.ds(start, kb), :]
        vb16 = v_ref[pl.ds(start, kb), :]
        do16 = do_ref[...].astype(bf16)
        qk = [_dot(q16[:, sl], kb16[:, sl], NT) for sl in heads]
        dpn = [_dot(do16[:, sl], vb16[:, sl], NT) for sl in heads]
        pn = [_att_probs(qk[a], bias[a], start, qt, pad, kb) for a in range(ah)]
        ds = [pn[a] * (dpn[a] - jnp.sum(dpn[a] * pn[a], axis=-1, keepdims=True)) for a in range(ah)]
        ds16 = [d.astype(bf16) for d in ds]
        dv_ref[pl.ds(start, kb), :] += jnp.concatenate([_dot(pn[a].astype(bf16), do16[:, sl], TN) for a, sl in enumerate(heads)], axis=1)
        dq_ref[...] = (jnp.concatenate([_dot(ds16[a], kb16[:, sl], NN) for a, sl in enumerate(heads)], axis=1) * scale).astype(bf16)
        dk_ref[pl.ds(start, kb), :] += jnp.concatenate([_dot(ds16[a], q16[:, sl], TN) for a, sl in enumerate(heads)], axis=1) * scale
        for a in range(ah):
            db_ref[a] += ds[a]

        @pl.when(g == ng - 1)
        def _():
            dko_ref[...] = dk_ref[pl.ds(pad, S), :].astype(bf16)
            dvo_ref[...] = dv_ref[pl.ds(pad, S), :].astype(bf16)

    kv_out = pl.BlockSpec((S, cw), lambda h, g: (0, h))
    return _grid_call(
        body,
        rider,
        name="attn_bwd",
        grid=(H // ah, S // qt),
        in_specs=[q_spec, kv_spec, kv_spec, ext_spec,
                  pl.BlockSpec((qt, cw), lambda h, g: (g, H // ah + h))],
        out_specs=[row_spec, kv_out, kv_out, pl.BlockSpec((ah, qt, kb), lambda h, g: (h, 0, 0))],
        out_shape=[
            jax.ShapeDtypeStruct((S, W), bf16),
            jax.ShapeDtypeStruct((S, W), bf16),
            jax.ShapeDtypeStruct((S, W), bf16),
            jax.ShapeDtypeStruct((H, qt, kb), f32),
        ],
        scratch_shapes=[pltpu.VMEM((ah, qt, kb), f32), pltpu.VMEM((S + pad, cw), f32), pltpu.VMEM((S + pad, cw), f32)],
        args=(proj, kp, vp, ext, dycat),
    )


def _bias_onehot(qt, ne, nrel_pad):
    m = lax.broadcasted_iota(jnp.int32, (nrel_pad, ne), 1)
    r = lax.broadcasted_iota(jnp.int32, (nrel_pad, ne), 0)
    rel = LEFT_CHUNKS * CHUNK + qt - 1 - m
    hot = (r == jnp.clip(rel, -REL_CLIP, REL_CLIP) + REL_CLIP) & (m < ne - 1)
    return hot.astype(f32)


def _bias_ext(rel_bias_pad, onehot):
    H, _, nr = rel_bias_pad.shape
    ne = onehot.shape[1]

    def body(rb_ref, oh_ref, o_ref):
        o_ref[0] = _dot(rb_ref[0], oh_ref[...], NN, HIGHEST)

    return pl.pallas_call(
        body,
        name="bias_ext",
        grid=(H,),
        in_specs=[pl.BlockSpec((1, 1, nr), lambda h: (h, 0, 0)), pl.BlockSpec((nr, ne), lambda h: (0, 0))],
        out_specs=pl.BlockSpec((1, 1, ne), lambda h: (h, 0, 0)),
        out_shape=jax.ShapeDtypeStruct((H, 1, ne), f32),
        compiler_params=pltpu.CompilerParams(dimension_semantics=("parallel",)),
    )(rel_bias_pad, onehot)


def _bias_bwd(dbias, onehot):
    H, qt, kb = dbias.shape
    nr, ne = onehot.shape

    def body(d_ref, oh_ref, o_ref):
        row = lax.broadcasted_iota(jnp.int32, (qt, qt), 0)
        col = lax.broadcasted_iota(jnp.int32, (qt, qt), 1)
        flip = (row + col == qt - 1).astype(f32)
        x = _dot(flip, d_ref[0], NN, HIGHEST)
        x = jnp.concatenate([x, jnp.zeros((qt, ne - kb), f32)], axis=1)
        de = jnp.sum(pltpu.roll(x, 0, 1, stride=1, stride_axis=0), axis=0, keepdims=True)
        o_ref[0] = _dot(de, oh_ref[...], NT, HIGHEST)

    return pl.pallas_call(
        body,
        name="bias_bwd",
        grid=(H,),
        in_specs=[pl.BlockSpec((1, qt, kb), lambda h: (h, 0, 0)), pl.BlockSpec((nr, ne), lambda h: (0, 0))],
        out_specs=pl.BlockSpec((1, 1, nr), lambda h: (h, 0, 0)),
        out_shape=jax.ShapeDtypeStruct((H, 1, nr), f32),
        compiler_params=pltpu.CompilerParams(dimension_semantics=("parallel",)),
    )(dbias, onehot)


def _place():
    x, y, c = lax.axis_index("x"), lax.axis_index("y"), lax.axis_index("c")
    return x, y, c


def _flip(x, y, k):
    return (1 - x if k & 2 else x), (1 - y if k & 1 else y)


def _region(ref, shard_shape, axis, j, half):
    R, C = shard_shape
    if axis == 0:
        if half is None:
            return ref.at[pl.ds(j * R, R), :]
        return ref.at[pl.ds(j * R + half * (R // 2), R // 2), :]
    if half is None:
        return ref.at[:, pl.ds(j * C, C)]
    return ref.at[pl.ds(half * (R // 2), R // 2), pl.ds(j * C, C)]


class _Remote:
    def __init__(self, src, dst, send_sem, recv_sem, dev):
        self.src, self.dst, self.sems, self.dev = src, dst, (send_sem, recv_sem), dev

    def _copy(self, src, dst):
        return pltpu.make_async_remote_copy(src_ref=src, dst_ref=dst, send_sem=self.sems[0], recv_sem=self.sems[1],
                                            device_id=self.dev, device_id_type=MESH)

    def start(self):
        rows = self.src.shape[0]
        n = RDMA_PIECES if rows % (RDMA_PIECES * 2 * SUBLANES) == 0 else 1
        for p in range(n):
            part = pl.ds(p * (rows // n), rows // n)
            self._copy(self.src.at[part], self.dst.at[part]).start()

    def wait_recv(self):
        self._copy(self.src, self.dst).wait_recv()

    def wait_send(self):
        self._copy(self.src, self.dst).wait_send()


def _remote(src, dst, send_sem, recv_sem, dev):
    return _Remote(src, dst, send_sem, recv_sem, dev)


def _cast_place(shard, axis, where, *, name):
    R, C = shard.shape
    tr = _div(R, 256, 2 * SUBLANES)
    nr = R // tr
    full = (N_CHIPS * R, C) if axis == 0 else (R, N_CHIPS * C)
    omap = (lambda r, s: (s[0] * nr + r, 0)) if axis == 0 else (lambda r, s: (r, s[0]))

    def body(s_ref, x_ref, o_ref):
        o_ref[...] = x_ref[...].astype(bf16)

    return pl.pallas_call(
        body,
        name=name,
        grid_spec=pltpu.PrefetchScalarGridSpec(
            num_scalar_prefetch=1,
            grid=(nr,),
            in_specs=[pl.BlockSpec((tr, C), lambda r, s: (r, 0))],
            out_specs=pl.BlockSpec((tr, C), omap),
        ),
        out_shape=jax.ShapeDtypeStruct(full, bf16),
        compiler_params=pltpu.CompilerParams(dimension_semantics=("parallel",)),
    )(where, shard)


class _Rider(NamedTuple):
    ins: list
    out_shapes: list
    aliases: dict
    sems: list
    first: Callable
    last: Callable


def _run_rider(rider, *, name):
    ni, no = len(rider.ins), len(rider.out_shapes)

    def body(*refs):
        ins, outs, sems = refs[:ni], refs[ni:ni + no], refs[ni + no:]
        rider.first(ins, outs, sems)
        rider.last(ins, outs, sems)

    return pl.pallas_call(
        body,
        name=name,
        in_specs=[_hbm()] * ni,
        out_specs=[_hbm()] * no,
        out_shape=rider.out_shapes,
        input_output_aliases=rider.aliases,
        scratch_shapes=rider.sems,
    )(*rider.ins)


def _start_all(copies):
    def first(ins, outs, sems):
        for cp in copies(ins, outs, sems):
            cp.start()

    def last(ins, outs, sems):
        cps = copies(ins, outs, sems)
        for cp in cps:
            cp.wait_recv()
        for cp in cps:
            cp.wait_send()

    return first, last


def _mm_while_gathering(a, full, shard_shape, where, *, tm, tn, name):
    S, K = a.shape
    R, Cs = shard_shape
    N = full.shape[1]
    tps, nj, ni = Cs // tn, N // tn, S // tm
    assert R == K and Cs % tn == 0 and S % tm == 0

    def block(j, chip):
        return jnp.bitwise_xor(chip, j // tps) * tps + j % tps

    def body(s_ref, a_ref, w_in, o_ref, w_ref, buf, fsem, send1, recv1, send2, recv2):
        j, i = pl.program_id(0), pl.program_id(1)
        x, y, c = _place()
        chip = 2 * x + y

        def ici(k):
            px, py = _flip(x, y, k)
            mine = _region(w_ref, shard_shape, 1, chip, c)
            return _remote(mine, mine, send1.at[k - 1], recv1.at[k - 1], (px, py, c))

        def handed(k, half):
            px, py = _flip(x, y, k)
            land = _region(w_ref, shard_shape, 1, 2 * px + py, half)
            return land, _remote(land, land, send2.at[k - 1], recv2.at[k - 1], (x, y, 1 - c))

        def tile(jj):
            col = pl.multiple_of(block(jj, chip) * tn, LANES)
            return pltpu.make_async_copy(w_ref.at[:, pl.ds(col, tn)], buf.at[jj % 2], fsem.at[jj % 2])

        @pl.when((j == 0) & (i == 0))
        def _():
            for k in (1, 2, 3):
                ici(k).start()
            tile(j).start()

        @pl.when(i == 0)
        def _():
            tile(j).wait()

        @pl.when((i == ni - 1) & (j + 1 < nj))
        def _():
            for k in (1, 2, 3):
                @pl.when(j + 1 == k * tps)
                def _():
                    land, forward = handed(k, c)
                    _remote(land, land, send1.at[k - 1], recv1.at[k - 1], (x, y, c)).wait_recv()
                    forward.start()
                    handed(k, 1 - c)[1].wait_recv()

            tile(j + 1).start()

        o_ref[...] = _dot(a_ref[...], buf[j % 2], NN)

        @pl.when((j == nj - 1) & (i == ni - 1))
        def _():
            for k in (1, 2, 3):
                ici(k).wait_send()
                handed(k, c)[1].wait_send()

    sems = pltpu.SemaphoreType.DMA((3,))
    spec = pltpu.PrefetchScalarGridSpec(
        num_scalar_prefetch=1,
        grid=(nj, ni),
        in_specs=[pl.BlockSpec((tm, K), lambda j, i, s: (i, 0)), _hbm()],
        out_specs=[pl.BlockSpec((tm, tn), lambda j, i, s: (i, block(j, s[0]))), _hbm()],
        scratch_shapes=[pltpu.VMEM((2, K, tn), full.dtype), pltpu.SemaphoreType.DMA((2,)), sems, sems, sems, sems],
    )
    return pl.pallas_call(
        body,
        name=name,
        grid_spec=spec,
        out_shape=[jax.ShapeDtypeStruct((S, N), f32), jax.ShapeDtypeStruct(full.shape, full.dtype)],
        input_output_aliases={2: 1},
        compiler_params=pltpu.CompilerParams(dimension_semantics=("arbitrary", "arbitrary")),
    )(where, a, full)


class _GatherJob(NamedTuple):
    buf: int
    shard_shape: tuple
    axis: int
    split: bool
    rels: tuple
    hand_over: bool


def _gather_rider(bufs, jobs):
    def copies(ins, outs, sems):
        x, y, c = _place()
        cps = []
        for n, job in enumerate(jobs):
            half = c if job.split else None
            for k in job.rels:
                px, py = _flip(x, y, k)
                chip, dev = (2 * px + py, (x, y, 1 - c)) if job.hand_over else (2 * x + y, (px, py, c))
                cps.append(_remote(_region(ins[job.buf], job.shard_shape, job.axis, chip, half),
                                   _region(outs[job.buf], job.shard_shape, job.axis, chip, half),
                                   sems[0].at[3 * n + k - 1], sems[1].at[3 * n + k - 1], dev))
        return cps

    assert all(job.split or not job.hand_over for job in jobs)
    sems = pltpu.SemaphoreType.DMA((3 * len(jobs),))
    return _Rider(list(bufs), [jax.ShapeDtypeStruct(b.shape, b.dtype) for b in bufs], {m: m for m in range(len(bufs))},
                  [sems, sems], *_start_all(copies))


def _chips_rider(ts, rels=(1, 2, 3), landing=None):
    n = len(ts)

    def copies(ins, outs, sems):
        x, y, c = _place()
        cps = []
        for m in range(n):
            for k in rels:
                px, py = _flip(x, y, k)
                cps.append(_remote(ins[m].at[2 * px + py], outs[m].at[k - 1], sems[0].at[3 * m + k - 1], sems[1].at[3 * m + k - 1], (px, py, c)))
        return cps

    sems = pltpu.SemaphoreType.DMA((3 * n,))
    return _Rider(list(ts) + list(landing or []), [jax.ShapeDtypeStruct((3,) + t.shape[1:], t.dtype) for t in ts],
                  {n + m: m for m in range(n)} if landing else {}, [sems, sems], *_start_all(copies))


def _sibling_rider(dws, shard_shapes, axes):
    n = len(dws)

    def copies(ins, outs, sems):
        x, y, c = _place()
        return [_remote(_region(ins[m], shard_shapes[m], axes[m], j, 1 - c), outs[m].at[j],
                        sems[0].at[N_CHIPS * m + j], sems[1].at[N_CHIPS * m + j], (x, y, 1 - c))
                for m in range(n) for j in range(N_CHIPS)]

    sems = pltpu.SemaphoreType.DMA((N_CHIPS * n,))
    return _Rider(list(dws), [jax.ShapeDtypeStruct((N_CHIPS, s[0] // 2, s[1]), d.dtype) for s, d in zip(shard_shapes, dws)], {},
                  [sems, sems], *_start_all(copies))


def _join_rider(gs):
    n = len(gs)

    def copies(ins, outs, sems):
        x, y, c = _place()
        cps = []
        for m in range(n):
            hr = gs[m].shape[0] // 2
            cps.append(_remote(ins[m].at[pl.ds(c * hr, hr), :], outs[m].at[pl.ds(c * hr, hr), :], sems[0].at[m], sems[1].at[m], (x, y, 1 - c)))
        return cps

    sems = pltpu.SemaphoreType.DMA((n,))
    return _Rider(list(gs), [jax.ShapeDtypeStruct(g.shape, g.dtype) for g in gs], {m: m for m in range(n)}, [sems, sems], *_start_all(copies))


def _both(r1, r2):
    i1, o1, s1 = len(r1.ins), len(r1.out_shapes), len(r1.sems)
    aliases = dict(r1.aliases)
    aliases.update({i1 + i: o1 + o for i, o in r2.aliases.items()})

    def first(ins, outs, sems):
        r1.first(ins[:i1], outs[:o1], sems[:s1])
        r2.first(ins[i1:], outs[o1:], sems[s1:])

    def last(ins, outs, sems):
        r1.last(ins[:i1], outs[:o1], sems[:s1])
        r2.last(ins[i1:], outs[o1:], sems[s1:])

    return _Rider(r1.ins + r2.ins, r1.out_shapes + r2.out_shapes, aliases, r1.sems + r2.sems, first, last)


def _pair_sum(dw, got, shard_shape, axis, where, *, name):
    R, C = shard_shape
    hr = R // 2
    tr = _div(hr, 256, 2 * SUBLANES)
    nr = hr // tr
    dmap = (lambda j, r, s: ((2 * j + s[1]) * nr + r, 0)) if axis == 0 else (lambda j, r, s: (s[1] * nr + r, j))
    slot = pl.BlockSpec((1, tr, C), lambda j, r, s: (j, r, 0))

    def body(s_ref, d_ref, g_ref, o_ref):
        o_ref[0] = (d_ref[...].astype(f32) + g_ref[0].astype(f32)).astype(bf16)

    return pl.pallas_call(
        body,
        name=name,
        grid_spec=pltpu.PrefetchScalarGridSpec(
            num_scalar_prefetch=1,
            grid=(N_CHIPS, nr),
            in_specs=[pl.BlockSpec((tr, C), dmap), slot],
            out_specs=slot,
        ),
        out_shape=jax.ShapeDtypeStruct((N_CHIPS, hr, C), bf16),
        compiler_params=pltpu.CompilerParams(dimension_semantics=("parallel", "parallel")),
    )(where, dw, got)


def _chip_sum(pair, others, where, *, name):
    _, hr, C = pair.shape
    tr = _div(hr, 256, 2 * SUBLANES)
    nr = hr // tr

    def body(s_ref, p_ref, o_ref, g_ref):
        g_ref[...] = p_ref[0].astype(f32) + o_ref[0].astype(f32) + o_ref[1].astype(f32) + o_ref[2].astype(f32)

    return pl.pallas_call(
        body,
        name=name,
        grid_spec=pltpu.PrefetchScalarGridSpec(
            num_scalar_prefetch=1,
            grid=(nr,),
            in_specs=[pl.BlockSpec((1, tr, C), lambda r, s: (s[0], r, 0)), pl.BlockSpec((3, tr, C), lambda r, s: (0, r, 0))],
            out_specs=pl.BlockSpec((tr, C), lambda r, s: (s[1] * nr + r, 0)),
        ),
        out_shape=jax.ShapeDtypeStruct((2 * hr, C), f32),
        compiler_params=pltpu.CompilerParams(dimension_semantics=("parallel",)),
    )(where, pair, others)


def _allreduce_small(v):
    rows = v.shape[0]

    def body(v_ref, o_ref, gath, send, recv):
        x, y, c = _place()
        me = 4 * x + 2 * y + c
        gath[pl.ds(me, 1)] = v_ref[...][None]
        cps = []
        for k in range(1, N_DEV):
            peer = (1 - x if k & 4 else x, 1 - y if k & 2 else y, 1 - c if k & 1 else c)
            cp = _remote(v_ref, gath.at[me], send.at[k - 1], recv.at[k - 1], peer)
            cp.start()
            cps.append(cp)
        for cp in cps:
            cp.wait_recv()
        for cp in cps:
            cp.wait_send()
        acc = gath[0]
        for d in range(1, N_DEV):
            acc = acc + gath[d]
        o_ref[...] = acc

    return pl.pallas_call(
        body,
        name="small_allreduce",
        in_specs=[_vmem()],
        out_specs=_vmem(),
        out_shape=jax.ShapeDtypeStruct(v.shape, f32),
        scratch_shapes=[pltpu.VMEM((N_DEV, rows, LANES), f32), pltpu.SemaphoreType.DMA((N_DEV - 1,)), pltpu.SemaphoreType.DMA((N_DEV - 1,))],
    )(v)


def _adamw_math(w, g, m, v):
    m = ADAM_B1 * m + (1.0 - ADAM_B1) * g
    v = ADAM_B2 * v + (1.0 - ADAM_B2) * (g * g)
    m_hat = m / (1.0 - ADAM_B1 ** ADAM_STEP)
    v_hat = v / (1.0 - ADAM_B2 ** ADAM_STEP)
    delta = -ADAM_LR * (m_hat / (jnp.sqrt(v_hat) + ADAM_EPS) + ADAM_WD * w)
    return delta, m, v


def _adamw(w, g, m, v, *, name):
    R, C = w.shape
    tr = _div(R, 128, SUBLANES)

    def body(w_ref, g_ref, m_ref, v_ref, go_ref, d_ref, mo_ref, vo_ref):
        gv = g_ref[...]
        d, mn, vn = _adamw_math(w_ref[...], gv, m_ref[...], v_ref[...])
        go_ref[...] = gv
        d_ref[...] = d
        mo_ref[...] = mn
        vo_ref[...] = vn

    spec = pl.BlockSpec((tr, C), lambda r: (r, 0))
    return pl.pallas_call(
        body,
        name=name,
        grid=(R // tr,),
        in_specs=[spec] * 4,
        out_specs=[spec] * 4,
        out_shape=[jax.ShapeDtypeStruct((R, C), f32)] * 4,
        compiler_params=pltpu.CompilerParams(dimension_semantics=("parallel",)),
    )(w, g, m, v)


def _pack(arrs):
    flat = jnp.concatenate([a.reshape(-1).astype(f32) for a in arrs])
    tile = SUBLANES * LANES
    total = -(-flat.shape[0] // tile) * tile
    return jnp.pad(flat, (0, total - flat.shape[0])).reshape(total // LANES, LANES)


def _unpack(buf, shapes):
    flat = buf.reshape(-1)
    out, off = [], 0
    for s in shapes:
        size = int(np.prod(s))
        out.append(flat[off:off + size].reshape(s))
        off += size
    return out


def kernel(x, p, norm_mix, w_in, lb_logits, hg_norm, rel_bias, w_out, norm_ffn, w_up, conv_w, conv_b, w_down, norm_ple, w_ple_gate, w_ple_proj, final_norm, loss_target, m_norm_mix, m_w_in, m_lb_logits, m_hg_norm, m_rel_bias, m_w_out, m_norm_ffn, m_w_up, m_conv_w, m_conv_b, m_w_down, m_norm_ple, m_w_ple_gate, m_w_ple_proj, m_final_norm, v_norm_mix, v_w_in, v_lb_logits, v_hg_norm, v_rel_bias, v_w_out, v_norm_ffn, v_w_up, v_conv_w, v_conv_b, v_w_down, v_norm_ple, v_w_ple_gate, v_w_ple_proj, v_final_norm):
    S, D = x.shape[1], x.shape[2]
    xv, pv, tgt = x[0], p[0, 0], loss_target[0]
    W = (w_in.shape[2] * N_CHIPS) // 7
    H = W // HEAD_DIM
    F = w_down.shape[1] * N_CHIPS
    tc = _div(F // 2, 1408, LANES)
    nj = F // tc
    jx, jy = lax.axis_index("x"), lax.axis_index("y")
    chip = 2 * jx + jy

    big = [w_in[0], w_out[0], w_up[0], w_down[0], w_ple_gate[0], w_ple_proj[0]]
    big_axes = [1, 0, 1, 0, 0, 1]
    where = jnp.stack([chip, lax.axis_index("c")]).astype(jnp.int32)
    cw_pad = jnp.pad(conv_w[0], ((0, SUBLANES - conv_w.shape[1]), (0, 0)))
    cw_mine = lax.dynamic_update_slice(jnp.zeros((N_CHIPS * SUBLANES, cw_pad.shape[1]), f32), cw_pad, (chip * SUBLANES, 0))
    placed = [_cast_place(b, ax, where, name=f"cast_place_{i}") for i, (b, ax) in enumerate(zip(big, big_axes))]
    shard_shapes = [b.shape for b in big]

    every = (1, 2, 3)

    def job(buf, i, rels, hand_over=False):
        if i is None:
            return _GatherJob(buf, cw_pad.shape, 0, False, rels, hand_over)
        return _GatherJob(buf, shard_shapes[i], big_axes[i], True, rels, hand_over)

    tm = _div(S, 1024, LANES)

    a1 = _rms_fwd(xv, norm_mix, name="rms_mix")
    cs_in = shard_shapes[0][1]
    tn_in = _div(cs_in, 1024, LANES)
    proj, W_in = _mm_while_gathering(a1, placed[0], shard_shapes[0], where, tm=tm, tn=tn_in, name="mm_in")
    qt, pad, kb, ne = _att_dims(S)
    nrel = rel_bias.shape[2]
    nrel_pad = -(-nrel // LANES) * LANES
    onehot = _bias_onehot(qt, ne, nrel_pad)
    rb_pad = jnp.pad(rel_bias[0], ((0, 0), (0, nrel_pad - nrel)))[:, None, :]
    ext = _bias_ext(rb_pad, onehot)
    kp, vp = _kv_padded(proj, W, pad)
    y_att, W_out_i, W_up_i = _attn_fwd(
        proj, kp, vp, ext, W, rider=_gather_rider([placed[1], placed[2]], [job(0, 1, every), job(1, 2, (1, 2))]))
    y_hg, o_hg, st, W_out, W_up_i = _hgrn_fwd(
        proj, lb_logits, hg_norm, W,
        rider=_gather_rider([W_out_i, W_up_i], [job(0, 1, every, True), job(1, 2, (1, 2), True), job(1, 2, (3,))]))
    ycat = jnp.concatenate([y_hg, y_att], axis=1)

    tn_d = _div(D, 512, LANES)
    h1, W_up = _mm(ycat, W_out, dims="nn", tm=tm, tn=tn_d, tk=D, out_dtype=f32, res=xv, name="mm_out",
                   rider=_gather_rider([W_up_i], [job(0, 2, (3,), True)]))
    a2 = _rms_fwd(h1, norm_ffn, name="rms_ffn")
    perm = lambda b: (b % 2) * nj + b // 2
    u, W_down_i, W_pg_i, W_pp_i, cw_all = _mm(
        a2, W_up, dims="nn", tm=tm, tn=tc, tk=D, out_dtype=f32, name="mm_up", b_idx=lambda i, j, k: (k, perm(j)),
        rider=_gather_rider([placed[3], placed[4], placed[5], cw_mine], [job(0, 3, every), job(1, 4, every), job(2, 5, every), job(3, None, every)]))
    ncw = conv_w.shape[1]
    cw_full = jnp.transpose(cw_all.reshape(N_CHIPS, SUBLANES, -1)[:, :ncw], (1, 0, 2)).reshape(ncw, -1)
    cb = conv_b
    mact, W_down, W_pg, W_pp = _convact_fwd(
        u, cw_full, cb, tc, rider=_gather_rider([W_down_i, W_pg_i, W_pp_i], [job(0, 3, every, True), job(1, 4, every, True), job(2, 5, every, True)]))
    tn_w = _div(D, 1024, LANES)
    h2 = _mm(mact, W_down, dims="nn", tm=_div(S, 512, LANES), tn=tn_w, tk=F, out_dtype=f32, res=h1, name="mm_down", cols_outer=True)
    a3 = _rms_fwd(h2, norm_ple, name="rms_ple")
    zg = _mm(a3, W_pg, dims="nn", tm=tm, tn=tn_d, tk=D, out_dtype=f32, name="mm_ple_gate")
    p16 = pv.astype(bf16)
    pp = _mm(p16, W_pp, dims="nn", tm=tm, tn=tn_d, tk=pv.shape[1], out_dtype=f32, name="mm_ple_proj")
    dh3, dzg, dpp, loss_part, d_fn = _tail(h2, zg, pp, tgt, final_norm.reshape(1, D))

    pair = {}

    def siblings(idx, grads):
        return _sibling_rider(grads, [shard_shapes[i] for i in idx], [big_axes[i] for i in idx])

    def pair_up(idx, grads, got):
        for i, d, g in zip(idx, grads, got):
            pair[i] = _pair_sum(d, g, shard_shapes[i], big_axes[i], where, name=f"grad_pair_sum_{i}")

    def reduced_half(i, landed):
        return _chip_sum(pair[i], landed, where, name=f"grad_chip_sum_{i}")

    tk_s = S
    dW_pp = _mm(p16, dpp, dims="tn", tm=pv.shape[1], tn=tn_w, tk=tk_s, out_dtype=bf16, name="mm_d_ple_proj")
    dW_pg = _mm(a3, dzg, dims="tn", tm=tn_w, tn=tn_w, tk=tk_s, out_dtype=bf16, name="mm_d_ple_gate")
    da3, *got = _mm(dzg, W_pg, dims="nt", tm=tm, tn=_div(D, 1024, LANES), tk=D, out_dtype=f32, name="mm_da3",
                    rider=siblings([4, 5], [dW_pg, dW_pp]))
    pair_up([4, 5], [dW_pg, dW_pp], got)
    dh2, dh2b, d_nple = _rms_bwd(h2, norm_ple, da3, dh3, name="rms_ple_bwd")
    dm, land4, land5 = _mm(dh2b, W_down, dims="nt", tm=tm, tn=_div(F, 512, LANES), tk=D, out_dtype=f32, name="mm_dm",
                           rider=_chips_rider([pair[4], pair[5]]))
    dW_down = _mm(mact, dh2b, dims="tn", tm=_div(F, 1408, LANES), tn=tn_d, tk=tk_s, out_dtype=bf16, name="mm_d_down")
    du, dcw_g, dcw_v, dcb_g, dcb_v, got3 = _convact_bwd(u, dm, cw_full, cb, tc, rider=siblings([3], [dW_down]))
    pair_up([3], [dW_down], [got3])
    dW_up, land3 = _mm(a2, du, dims="tn", tm=_div(D, 1024, LANES), tn=tc, tk=tk_s, out_dtype=bf16, name="mm_d_up",
                       o_idx=lambda i, j, k: (i, perm(j)), rider=_chips_rider([pair[3]]))
    da2, got2 = _mm(du, W_up, dims="nt", tm=tm, tn=D, tk=tc, out_dtype=f32, name="mm_da2",
                    b_idx=lambda i, j, k: (j, perm(k)), rider=siblings([2], [dW_up]))
    pair_up([2], [dW_up], [got2])
    dh1, dh1b, d_nffn = _rms_bwd(h1, norm_ffn, da2, dh2, name="rms_ffn_bwd")
    dycat = _mm(dh1b, W_out, dims="nt", tm=tm, tn=_div(D, 1024, LANES), tk=D, out_dtype=f32, name="mm_dycat")
    dW_out = _mm(ycat, dh1b, dims="tn", tm=tn_w, tn=tn_w, tk=tk_s, out_dtype=bf16, name="mm_d_out")
    dq_att, dk_att, dv_att, dbias, land2, got1 = _attn_bwd(
        proj, kp, vp, ext, dycat, W, rider=_both(_chips_rider([pair[2]], rels=(1, 2)), siblings([1], [dW_out])))
    pair_up([1], [dW_out], [got1])
    d_hg4, d_lbl, d_hgn, land2 = _hgrn_bwd(proj, o_hg, st, dycat, lb_logits, hg_norm, W,
                                           rider=_chips_rider([pair[2]], rels=(3,), landing=[land2]))
    d_rb = _bias_bwd(dbias, onehot)[:, 0, :nrel]
    dproj = [d_hg4, dq_att, dk_att, dv_att]
    tw = _div(W, 1024, LANES)
    dW_in, land1 = _mm(a1, dproj, dims="tn", tm=tn_w, tn=_div(W, 512, LANES), tk=tk_s, out_dtype=bf16, name="mm_d_in",
                       rider=_chips_rider([pair[1]]))
    pair_up([0], [dW_in], _run_rider(siblings([0], [dW_in]), name="grads_to_sibling_0"))
    halves = [reduced_half(i, land) for i, land in ((1, land1), (2, land2), (3, land3), (4, land4), (5, land5))]
    da1, land0, *g_rest = _mm(dproj, W_in, dims="nt", tm=tm, tn=D, tk=tw, out_dtype=f32, name="mm_da1",
                              rider=_both(_chips_rider([pair[0]]), _join_rider(halves)))
    grad_x, _, d_nmix = _rms_bwd(xv, norm_mix, da1, dh1, name="rms_mix_bwd")
    g_big = list(_run_rider(_join_rider([reduced_half(0, land0)]), name="grads_join_w_in")) + list(g_rest)

    d_cw = jnp.concatenate([dcw_g, dcw_v], axis=1)
    d_cb = jnp.concatenate([dcb_g, dcb_v], axis=1)
    small_parts = [loss_part[:, :1], d_nmix, d_lbl, d_hgn, d_rb, d_nffn, d_cw, d_cb, d_nple, d_fn]
    small_shapes = [(1, 1), (1, D), lb_logits.shape, hg_norm.shape, (H, nrel), (1, D), (ncw, 2 * F), (1, 2 * F), (1, D), (1, D)]
    red = _unpack(_allreduce_small(_pack(small_parts)), small_shapes)
    loss = red[0].reshape(())
    g_nmix, g_lbl, g_hgn, g_rb, g_nffn, g_cw_all, g_cb, g_nple, g_fn = red[1:]
    csh = conv_w.shape[2]
    g_cw = lax.dynamic_slice(g_cw_all, (0, chip * csh), (ncw, csh))
    small_g = [g_nmix, g_lbl, g_hgn, g_rb[None], g_nffn, g_cw[None], g_cb, g_nple, g_fn.reshape(D)]
    small_w = [norm_mix, lb_logits, hg_norm, rel_bias, norm_ffn, conv_w, conv_b, norm_ple, final_norm]
    small_m = [m_norm_mix, m_lb_logits, m_hg_norm, m_rel_bias, m_norm_ffn, m_conv_w, m_conv_b, m_norm_ple, m_final_norm]
    small_v = [v_norm_mix, v_lb_logits, v_hg_norm, v_rel_bias, v_norm_ffn, v_conv_w, v_conv_b, v_norm_ple, v_final_norm]
    shapes_s = [w.shape for w in small_w]
    _, sd, sm, sv = _adamw(_pack(small_w), _pack(small_g), _pack(small_m), _pack(small_v), name="adamw_small")
    small_g = [g.reshape(s) for g, s in zip(small_g, shapes_s)]
    small_d, small_nm, small_nv = _unpack(sd, shapes_s), _unpack(sm, shapes_s), _unpack(sv, shapes_s)

    big_m = [m_w_in[0], m_w_out[0], m_w_up[0], m_w_down[0], m_w_ple_gate[0], m_w_ple_proj[0]]
    big_v = [v_w_in[0], v_w_out[0], v_w_up[0], v_w_down[0], v_w_ple_gate[0], v_w_ple_proj[0]]
    big_g, big_d, big_nm, big_nv = [], [], [], []
    for i in range(6):
        g_, d_, m_, v_ = _adamw(big[i], g_big[i], big_m[i], big_v[i], name=f"adamw_{i}")
        big_g.append(g_[None])
        big_d.append(d_[None])
        big_nm.append(m_[None])
        big_nv.append(v_[None])
    g_big = big_g

    def order(sm_list, bg_list):
        s, b = sm_list, bg_list
        return [s[0], b[0], s[1], s[2], s[3], b[1], s[4], b[2], s[5], s[6], b[3], s[7], b[4], b[5], s[8]]

    return (loss, grad_x[None], *order(small_g, g_big), *order(small_d, big_d), *order(small_nm, big_nm), *order(small_nv, big_nv))
```

```python
import functools
from typing import Callable, NamedTuple

import jax
import jax.numpy as jnp
import numpy as np
from jax import lax
from jax.experimental import pallas as pl
from jax.experimental.pallas import tpu as pltpu

f32 = jnp.float32
bf16 = jnp.bfloat16

CHUNK = 64
HEAD_DIM = 128
LEFT_CHUNKS = 8
REL_CLIP = 128
EPS = 1e-6
HG_SUB = 16
HG_HEADS_PER_STEP = 8
ATT_Q_ROWS = 256
ATT_HEADS_PER_STEP = 2
ADAM_LR, ADAM_B1, ADAM_B2, ADAM_EPS, ADAM_WD, ADAM_STEP = 0.001, 0.9, 0.999, 1e-08, 0.01, 10
LANES = 128
SUBLANES = 8
N_CHIPS = 4
N_DEV = 8
RDMA_PIECES = 4
MESH = pl.DeviceIdType.MESH
NEG = float(np.finfo(np.float32).min)

NN = (((1,), (0,)), ((), ()))
NT = (((1,), (1,)), ((), ()))
TN = (((0,), (0,)), ((), ()))
HIGHEST = lax.Precision.HIGHEST


def _dot(a, b, dims, precision=None):
    return lax.dot_general(a, b, dims, preferred_element_type=f32, precision=precision)


def _sigmoid(v):
    return 1.0 / (1.0 + jnp.exp(-v))


def _div(n, pref, mult):
    best = None
    d = mult
    while d <= min(n, pref):
        if n % d == 0:
            best = d
        d += mult
    return best if best is not None else n


def _hbm():
    return pl.BlockSpec(memory_space=pltpu.HBM)


def _vmem():
    return pl.BlockSpec(memory_space=pltpu.VMEM)


def _mm(a, b, *, dims, tm, tn, tk, out_dtype, name, res=None, a_idx=None, b_idx=None, o_idx=None, out_shape=None, rider=None,
        cols_outer=False):
    a_list = list(a) if isinstance(a, (list, tuple)) else [a]
    b_list = list(b) if isinstance(b, (list, tuple)) else [b]
    na, nb = len(a_list), len(b_list)
    if dims == "nn":
        assert na == 1 and nb == 1
        (M, K), (_, N) = a.shape, b.shape
        a_blk, b_blk, dn = (tm, tk), (tk, tn), NN
        a_def, b_def = (lambda i, j, k: (i, k)), (lambda i, j, k: (k, j))
    elif dims == "nt":
        assert nb == 1
        M, K, N = a_list[0].shape[0], sum(p.shape[1] for p in a_list), b.shape[0]
        a_blk, b_blk, dn = (tm, tk), (tn, tk), NT
        a_def, b_def = (lambda i, j, k: (i, k)), (lambda i, j, k: (j, k))
    else:
        assert na == 1
        (K, M), N = a.shape, sum(p.shape[1] for p in b_list)
        a_blk, b_blk, dn = (tk, tm), (tk, tn), TN
        a_def, b_def = (lambda i, j, k: (k, i)), (lambda i, j, k: (k, j))
    assert M % tm == 0 and N % tn == 0 and K % tk == 0, (name, M, N, K, tm, tn, tk)
    nk = K // tk
    has_res = res is not None
    counts = [p.shape[1] // tk for p in a_list] if na > 1 else [p.shape[1] // tn for p in b_list]
    starts = [sum(counts[:p]) for p in range(len(counts))]

    def body(*refs):
        a_refs, b_refs, rest = refs[:na], refs[na:na + nb], refs[na + nb:]
        res_ref = rest[0] if has_res else None
        o_ref = rest[1] if has_res else rest[0]
        k = pl.program_id(2)
        j = pl.program_id(0) if cols_outer else pl.program_id(1)

        def finish(acc):
            if has_res:
                acc = acc + res_ref[...]
            o_ref[...] = acc.astype(out_dtype)

        def consume(part):
            if nk == 1:
                finish(part)
                return
            acc_ref = refs[-1]

            @pl.when(k == 0)
            def _():
                acc_ref[...] = part

            @pl.when(k > 0)
            def _():
                acc_ref[...] += part

            @pl.when(k == nk - 1)
            def _():
                finish(acc_ref[...])

        if len(counts) == 1:
            consume(_dot(a_refs[0][...].astype(bf16), b_refs[0][...].astype(bf16), dn))
        else:
            at = k if na > 1 else j
            for p, (s, n) in enumerate(zip(starts, counts)):
                a_ref, b_ref = a_refs[p if na > 1 else 0], b_refs[p if nb > 1 else 0]
                pl.when((at >= s) & (at < s + n))(
                    functools.partial(lambda ar, br: consume(_dot(ar[...].astype(bf16), br[...].astype(bf16), dn)), a_ref, b_ref))

    def ordered(custom, default):
        f = custom or default
        return (lambda j, i, k: f(i, j, k)) if cols_outer else f

    def piece_map(default, p):
        s, n = starts[p], counts[p]
        if na > 1:
            return lambda i, j, k: default(i, j, jnp.clip(k - s, 0, n - 1))
        return lambda i, j, k: default(i, jnp.clip(j - s, 0, n - 1), k)

    o_map = ordered(o_idx, lambda i, j, k: (i, j))
    a_specs = [pl.BlockSpec(a_blk, ordered(a_idx, piece_map(a_def, p) if na > 1 else a_def)) for p in range(na)]
    b_specs = [pl.BlockSpec(b_blk, ordered(b_idx, piece_map(b_def, p) if nb > 1 else b_def)) for p in range(nb)]
    in_specs = a_specs + b_specs
    args = a_list + b_list
    if has_res:
        in_specs.append(pl.BlockSpec((tm, tn), o_map))
        args.append(res)
    outs = _grid_call(
        body,
        rider,
        name=name,
        grid=(N // tn, M // tm, nk) if cols_outer else (M // tm, N // tn, nk),
        in_specs=in_specs,
        out_specs=[pl.BlockSpec((tm, tn), o_map)],
        out_shape=[jax.ShapeDtypeStruct(out_shape or (M, N), out_dtype)],
        scratch_shapes=[pltpu.VMEM((tm, tn), f32)] if nk > 1 else [],
        args=args,
    )
    return outs[0] if rider is None else outs


def _rms_fwd(xv, g, *, name):
    S, D = xv.shape
    ts = _div(S, 512, SUBLANES)

    def body(x_ref, g_ref, o_ref):
        v = x_ref[...]
        r = lax.rsqrt(jnp.mean(v * v, axis=-1, keepdims=True) + EPS)
        o_ref[...] = (v * r * g_ref[...]).astype(bf16)

    return pl.pallas_call(
        body,
        name=name,
        grid=(S // ts,),
        in_specs=[pl.BlockSpec((ts, D), lambda i: (i, 0)), pl.BlockSpec((1, D), lambda i: (0, 0))],
        out_specs=pl.BlockSpec((ts, D), lambda i: (i, 0)),
        out_shape=jax.ShapeDtypeStruct((S, D), bf16),
        compiler_params=pltpu.CompilerParams(dimension_semantics=("parallel",)),
    )(xv, g)


def _rms_bwd(xv, g, dy, dres, *, name, rider=None):
    S, D = xv.shape
    ts = _div(S, 256, SUBLANES)

    def body(x_ref, g_ref, dy_ref, dres_ref, dx_ref, dxb_ref, dg_ref):
        i = pl.program_id(0)
        v = x_ref[...]
        r = lax.rsqrt(jnp.mean(v * v, axis=-1, keepdims=True) + EPS)
        vn = v * r
        d = dy_ref[...]
        part = jnp.sum(d * vn, axis=0, keepdims=True)

        @pl.when(i == 0)
        def _():
            dg_ref[...] = part

        @pl.when(i > 0)
        def _():
            dg_ref[...] += part

        t = d * g_ref[...]
        dx = dres_ref[...] + r * (t - vn * jnp.mean(t * vn, axis=-1, keepdims=True))
        dx_ref[...] = dx
        dxb_ref[...] = dx.astype(bf16)

    row = pl.BlockSpec((ts, D), lambda i: (i, 0))
    vec = pl.BlockSpec((1, D), lambda i: (0, 0))
    return _grid_call(
        body,
        rider,
        name=name,
        grid=(S // ts,),
        in_specs=[row, vec, row, row],
        out_specs=[row, row, vec],
        out_shape=[jax.ShapeDtypeStruct((S, D), f32), jax.ShapeDtypeStruct((S, D), bf16), jax.ShapeDtypeStruct((1, D), f32)],
        scratch_shapes=[],
        args=(xv, g, dy, dres),
    )


def _tail(h2, zg, pp, tgt, fn):
    S, D = h2.shape
    ts = _div(S, 256, SUBLANES)

    def body(h_ref, z_ref, p_ref, t_ref, fn_ref, dh_ref, dz_ref, dp_ref, loss_ref, dfn_ref):
        i = pl.program_id(0)
        sg = _sigmoid(z_ref[...])
        ppv = p_ref[...]
        h3 = h_ref[...] + sg * ppv
        r = lax.rsqrt(jnp.mean(h3 * h3, axis=-1, keepdims=True) + EPS)
        hn = h3 * r
        fnv = fn_ref[...]
        e = hn * fnv - t_ref[...]
        lpart = 0.5 * jnp.sum(jnp.mean(e * e, axis=-1, keepdims=True), axis=0, keepdims=True)
        dy = e * (1.0 / D)
        dpart = jnp.sum(dy * hn, axis=0, keepdims=True)

        @pl.when(i == 0)
        def _():
            loss_ref[...] = jnp.broadcast_to(lpart, loss_ref.shape)
            dfn_ref[...] = dpart

        @pl.when(i > 0)
        def _():
            loss_ref[...] += jnp.broadcast_to(lpart, loss_ref.shape)
            dfn_ref[...] += dpart

        t = dy * fnv
        dh3 = r * (t - hn * jnp.mean(t * hn, axis=-1, keepdims=True))
        dh_ref[...] = dh3
        dz_ref[...] = (dh3 * ppv * sg * (1.0 - sg)).astype(bf16)
        dp_ref[...] = (dh3 * sg).astype(bf16)

    row = pl.BlockSpec((ts, D), lambda i: (i, 0))
    vec = pl.BlockSpec((1, D), lambda i: (0, 0))
    one = pl.BlockSpec((1, LANES), lambda i: (0, 0))
    return pl.pallas_call(
        body,
        name="ple_final_loss",
        grid=(S // ts,),
        in_specs=[row, row, row, row, vec],
        out_specs=[row, row, row, one, vec],
        out_shape=[
            jax.ShapeDtypeStruct((S, D), f32),
            jax.ShapeDtypeStruct((S, D), bf16),
            jax.ShapeDtypeStruct((S, D), bf16),
            jax.ShapeDtypeStruct((1, LANES), f32),
            jax.ShapeDtypeStruct((1, D), f32),
        ],
        compiler_params=pltpu.CompilerParams(dimension_semantics=("arbitrary",)),
    )(h2, zg, pp, tgt, fn)


def _shift_down(v, halo, k):
    r = pltpu.roll(v, k, 0)
    hr = pltpu.roll(halo, k, 0)
    row = lax.broadcasted_iota(jnp.int32, hr.shape, 0)
    top = jnp.where(row < k, hr, r[0:SUBLANES])
    return top if v.shape[0] == SUBLANES else jnp.concatenate([top, r[SUBLANES:]], axis=0)


def _shift_up(v, halo, k):
    n = v.shape[0]
    r = pltpu.roll(v, n - k, 0)
    hr = pltpu.roll(halo, SUBLANES - k, 0)
    row = lax.broadcasted_iota(jnp.int32, hr.shape, 0)
    bot = jnp.where(row >= SUBLANES - k, hr, r[n - SUBLANES:])
    return jnp.concatenate([r[: n - SUBLANES], bot], axis=0)


def _conv_specs(S, F, tc, ts):
    nj = F // tc
    rows8 = ts // SUBLANES
    main = pl.BlockSpec((ts, 2 * tc), lambda j, i: (i, j))
    prev = pl.BlockSpec((SUBLANES, 2 * tc), lambda j, i: (jnp.maximum(i * rows8 - 1, 0), j))
    nxt = pl.BlockSpec((SUBLANES, 2 * tc), lambda j, i: (jnp.minimum((i + 1) * rows8, S // SUBLANES - 1), j))
    wg = pl.BlockSpec((3, tc), lambda j, i: (0, j))
    wv = pl.BlockSpec((3, tc), lambda j, i: (0, nj + j))
    bg = pl.BlockSpec((1, tc), lambda j, i: (0, j))
    bv = pl.BlockSpec((1, tc), lambda j, i: (0, nj + j))
    half = pl.BlockSpec((ts, tc), lambda j, i: (i, j))
    return nj, main, prev, nxt, wg, wv, bg, bv, half


def _conv_pre(u_ref, h_ref, wg_ref, wv_ref, bg_ref, bv_ref, tc):
    i = pl.program_id(1)
    u = u_ref[...]
    halo = jnp.where(i == 0, 0.0, h_ref[...])
    u1 = _shift_down(u, halo, 1)
    u2 = _shift_down(u, halo, 2)
    w = jnp.concatenate([wg_ref[...], wv_ref[...]], axis=1)
    b = jnp.concatenate([bg_ref[...], bv_ref[...]], axis=1)
    uc = b + w[0:1] * u2 + w[1:2] * u1 + w[2:3] * u
    return u, u1, u2, uc[:, :tc], uc[:, tc:]


def _convact_fwd(u, cw, cb, tc, rider=None):
    S, F2 = u.shape
    F = F2 // 2
    ts = _div(S, 512, SUBLANES)
    nj, main, prev, nxt, wg, wv, bg, bv, half = _conv_specs(S, F, tc, ts)

    def body(u_ref, h_ref, wg_ref, wv_ref, bg_ref, bv_ref, m_ref):
        _, _, _, g, v = _conv_pre(u_ref, h_ref, wg_ref, wv_ref, bg_ref, bv_ref, tc)
        m_ref[...] = (g * _sigmoid(g) * v).astype(bf16)

    outs = _grid_call(
        body,
        rider,
        name="convact_fwd",
        grid=(nj, S // ts),
        in_specs=[main, prev, wg, wv, bg, bv],
        out_specs=[half],
        out_shape=[jax.ShapeDtypeStruct((S, F), bf16)],
        scratch_shapes=[],
        args=(u, u, cw, cw, cb, cb),
    )
    return outs[0] if rider is None else outs


def _convact_bwd(u, dm, cw, cb, tc, rider=None):
    S, F2 = u.shape
    F = F2 // 2
    ts = _div(S, 512, SUBLANES)
    nj, main, prev, nxt, wg, wv, bg, bv, half = _conv_specs(S, F, tc, ts)
    ni = S // ts
    half_nxt = pl.BlockSpec((SUBLANES, tc), lambda j, i: (jnp.minimum((i + 1) * (ts // SUBLANES), S // SUBLANES - 1), j))

    def act_bwd(g, v, d):
        sg = _sigmoid(g)
        return jnp.concatenate([d * v * sg * (1.0 + g * (1.0 - sg)), d * g * sg], axis=1)

    def body(u_ref, h_ref, n_ref, dm_ref, dmn_ref, wg_ref, wv_ref, bg_ref, bv_ref, du_ref, dwg_ref, dwv_ref, dbg_ref, dbv_ref):
        i = pl.program_id(1)
        u0, u1, u2, g, v = _conv_pre(u_ref, h_ref, wg_ref, wv_ref, bg_ref, bv_ref, tc)
        duc = act_bwd(g, v, dm_ref[...])
        w = jnp.concatenate([wg_ref[...], wv_ref[...]], axis=1)
        b = jnp.concatenate([bg_ref[...], bv_ref[...]], axis=1)
        un = n_ref[...]
        tail = u0[ts - SUBLANES:]
        ucn = b + w[0:1] * _shift_down(un, tail, 2) + w[1:2] * _shift_down(un, tail, 1) + w[2:3] * un
        ducn = jnp.where(i == ni - 1, 0.0, act_bwd(ucn[:, :tc], ucn[:, tc:], dmn_ref[...]))
        du_ref[...] = (w[2:3] * duc + w[1:2] * _shift_up(duc, ducn, 1) + w[0:1] * _shift_up(duc, ducn, 2)).astype(bf16)
        db = jnp.sum(duc, axis=0, keepdims=True)
        dw = jnp.concatenate(
            [jnp.sum(duc * u2, axis=0, keepdims=True), jnp.sum(duc * u1, axis=0, keepdims=True), jnp.sum(duc * u0, axis=0, keepdims=True)],
            axis=0,
        )

        @pl.when(i == 0)
        def _():
            dwg_ref[...] = dw[:, :tc]
            dwv_ref[...] = dw[:, tc:]
            dbg_ref[...] = db[:, :tc]
            dbv_ref[...] = db[:, tc:]

        @pl.when(i > 0)
        def _():
            dwg_ref[...] += dw[:, :tc]
            dwv_ref[...] += dw[:, tc:]
            dbg_ref[...] += db[:, :tc]
            dbv_ref[...] += db[:, tc:]

    w_out = pl.BlockSpec((3, tc), lambda j, i: (0, j))
    b_out = pl.BlockSpec((1, tc), lambda j, i: (0, j))
    return _grid_call(
        body,
        rider,
        name="convact_bwd",
        grid=(nj, ni),
        in_specs=[main, prev, nxt, half, half_nxt, wg, wv, bg, bv],
        out_specs=[main, w_out, w_out, b_out, b_out],
        out_shape=[
            jax.ShapeDtypeStruct((S, F2), bf16),
            jax.ShapeDtypeStruct((3, F), f32),
            jax.ShapeDtypeStruct((3, F), f32),
            jax.ShapeDtypeStruct((1, F), f32),
            jax.ShapeDtypeStruct((1, F), f32),
        ],
        scratch_shapes=[],
        args=(u, u, u, dm, dm, cw, cw, cb, cb),
    )


def _hg_gates(qp, fp, lbl):
    lb = _sigmoid(lbl[0:1] - lbl[1:2])
    sig = _sigmoid(fp)
    sigm = _sigmoid(-fp)
    f = lb + (1.0 - lb) * sig
    k = (1.0 - lb) * sigm
    sq = _sigmoid(qp)
    qf = qp * sq
    row = lax.broadcasted_iota(jnp.int32, (CHUNK, CHUNK), 0)
    col = lax.broadcasted_iota(jnp.int32, (CHUNK, CHUNK), 1)
    b = _dot((row >= col).astype(f32), jnp.log(f), NN, HIGHEST)
    return lb, sig, sigm, f, k, sq, qf, b


def _hg_block(qf, k, b, I):
    r0, n = I * HG_SUB, (I + 1) * HG_SUB
    base = b[r0 - 1:r0] if I > 0 else jnp.zeros_like(b[0:1])
    eq = jnp.exp(b[r0:n] - base)
    ek = jnp.exp(base - b[0:n])
    qt16 = (qf[r0:n] * eq).astype(bf16)
    kt16 = (k[0:n] * ek).astype(bf16)
    row = lax.broadcasted_iota(jnp.int32, (HG_SUB, n), 0) + r0
    col = lax.broadcasted_iota(jnp.int32, (HG_SUB, n), 1)
    return r0, n, eq, ek, qt16, kt16, col <= row


def _per_head(v, hp, fn):
    return jnp.concatenate([fn(v[:, p * HEAD_DIM:(p + 1) * HEAD_DIM]) for p in range(hp)], axis=1)


def _head_mean(v, hp):
    return _per_head(v, hp, lambda t: jnp.broadcast_to(jnp.mean(t, axis=-1, keepdims=True), t.shape))


def _pad_rows(v, rows):
    return v if v.shape[0] == rows else jnp.concatenate([v, jnp.zeros((rows - v.shape[0], v.shape[1]), v.dtype)], axis=0)


def _hgrn_specs(S, W, hp, reverse):
    nc = S // CHUNK
    ngrp = W // (hp * HEAD_DIM)
    cw = hp * HEAD_DIM
    cidx = (lambda c: nc - 1 - c) if reverse else (lambda c: c)

    def proj(off):
        return pl.BlockSpec((CHUNK, cw), lambda h, c: (cidx(c), off * ngrp + h))

    act = pl.BlockSpec((CHUNK, cw), lambda h, c: (cidx(c), h))
    lbl = pl.BlockSpec((2, cw), lambda h, c: (0, h))
    gn = pl.BlockSpec((1, HEAD_DIM), lambda h, c: (0, 0))
    st = pl.BlockSpec((hp, 1, HEAD_DIM, HEAD_DIM), lambda h, c: (h, cidx(c), 0, 0))
    return nc, ngrp, proj, act, lbl, gn, st


def _grid_call(body, rider, *, name, grid, in_specs, out_specs, out_shape, scratch_shapes, args):
    params = pltpu.CompilerParams(dimension_semantics=("arbitrary",) * len(grid))
    if rider is None:
        return pl.pallas_call(body, name=name, grid=grid, in_specs=in_specs, out_specs=out_specs, out_shape=out_shape,
                              scratch_shapes=scratch_shapes, compiler_params=params)(*args)
    n_in, n_out, n_scr = len(in_specs), len(out_specs), len(scratch_shapes)
    ri, ro = len(rider.ins), len(rider.out_shapes)

    def wrapped(*refs):
        a, b = n_in, n_in + ri
        c, d = b + n_out, b + n_out + ro
        e = d + n_scr
        ids = [pl.program_id(t) for t in range(len(grid))]
        is_first = functools.reduce(jnp.logical_and, [i == 0 for i in ids])
        is_last = functools.reduce(jnp.logical_and, [i == g - 1 for i, g in zip(ids, grid)])

        @pl.when(is_first)
        def _():
            rider.first(refs[a:b], refs[c:d], refs[e:])

        body(*refs[:a], *refs[b:c], *refs[d:e])

        @pl.when(is_last)
        def _():
            rider.last(refs[a:b], refs[c:d], refs[e:])

    return pl.pallas_call(
        wrapped,
        name=name,
        grid=grid,
        in_specs=list(in_specs) + [_hbm()] * ri,
        out_specs=list(out_specs) + [_hbm()] * ro,
        out_shape=list(out_shape) + list(rider.out_shapes),
        input_output_aliases={n_in + i: n_out + o for i, o in rider.aliases.items()},
        scratch_shapes=list(scratch_shapes) + list(rider.sems),
        compiler_params=params,
    )(*args, *rider.ins)


def _hgrn_fwd(proj, lb_logits, hg_norm, W, rider=None):
    S = proj.shape[0]
    hp = min(HG_HEADS_PER_STEP, W // HEAD_DIM)
    nc, ngrp, pspec, act, lbl_spec, gn_spec, st_spec = _hgrn_specs(S, W, hp, False)
    nb = CHUNK // HG_SUB

    def body(q_ref, f_ref, i_ref, g_ref, lbl_ref, gn_ref, y_ref, o_ref, st_ref, state):
        c = pl.program_id(1)

        @pl.when(c == 0)
        def _():
            state[...] = jnp.zeros_like(state)

        _, _, _, _, k, _, qf, b = _hg_gates(q_ref[...], f_ref[...], lbl_ref[...])
        v16 = i_ref[...].astype(bf16)
        a16 = (qf * jnp.exp(b)).astype(bf16)
        bl = b[CHUNK - 1:CHUNK]
        kd16 = (k * jnp.exp(bl - b)).astype(bf16)
        ebl = jnp.exp(bl)
        blocks = [_hg_block(qf, k, b, I) for I in range(nb)]
        heads = [slice(p * HEAD_DIM, (p + 1) * HEAD_DIM) for p in range(hp)]
        st_prev = state[...]
        st_ref[:, 0] = st_prev
        inter = [_dot(a16[:, sl], st_prev[p].astype(bf16), NT) for p, sl in enumerate(heads)]
        scores = [[_dot(qt16[:, sl], kt16[:, sl], NT) for sl in heads] for _, _, _, _, qt16, kt16, _ in blocks]
        intra = [[_dot(jnp.where(blk[6], sc, 0.0).astype(bf16), v16[0:blk[1], sl], NN) for sc, sl in zip(scs, heads)]
                 for blk, scs in zip(blocks, scores)]
        st_new = [st_prev[p] * ebl[:, sl] + _dot(v16[:, sl], kd16[:, sl], TN) for p, sl in enumerate(heads)]
        state[...] = jnp.stack(st_new)
        o = jnp.concatenate([inter[p] + jnp.concatenate([rows[p] for rows in intra], axis=0) for p in range(hp)], axis=1)
        o_ref[...] = o
        r = lax.rsqrt(_head_mean(o * o, hp) + EPS)
        gp = g_ref[...]
        y_ref[...] = (o * r * jnp.tile(gn_ref[...], (1, hp)) * (gp * _sigmoid(gp))).astype(bf16)

    H = W // HEAD_DIM
    return _grid_call(
        body,
        rider,
        name="hgrn_fwd",
        grid=(ngrp, nc),
        in_specs=[pspec(0), pspec(1), pspec(2), pspec(3), lbl_spec, gn_spec],
        out_specs=[act, act, st_spec],
        out_shape=[
            jax.ShapeDtypeStruct((S, W), bf16),
            jax.ShapeDtypeStruct((S, W), f32),
            jax.ShapeDtypeStruct((H, nc, HEAD_DIM, HEAD_DIM), f32),
        ],
        scratch_shapes=[pltpu.VMEM((hp, HEAD_DIM, HEAD_DIM), f32)],
        args=(proj, proj, proj, proj, lb_logits, hg_norm),
    )


def _hgrn_bwd(proj, o_hg, st, dycat, lb_logits, hg_norm, W, rider=None):
    S = proj.shape[0]
    hp = min(HG_HEADS_PER_STEP, W // HEAD_DIM)
    assert hp * HEAD_DIM == W, "the four gradients share one output block: all heads in one grid step"
    nc, ngrp, pspec, act, lbl_spec, gn_spec, st_spec = _hgrn_specs(S, W, hp, True)
    nb = CHUNK // HG_SUB

    def body(q_ref, f_ref, i_ref, g_ref, o_ref, st_ref, dy_ref, lbl_ref, gn_ref,
             d4_ref, dlbl_ref, dgn_ref, dstate):
        cw_ = hp * HEAD_DIM
        dq_ref, df_ref, di_ref, dg_ref = (d4_ref.at[:, pl.ds(t * cw_, cw_)] for t in range(4))
        h = pl.program_id(0)
        c = pl.program_id(1)

        @pl.when(c == 0)
        def _():
            dstate[...] = jnp.zeros_like(dstate)
            dlbl_ref[...] = jnp.zeros_like(dlbl_ref)

        @pl.when((c == 0) & (h == 0))
        def _():
            dgn_ref[...] = jnp.zeros_like(dgn_ref)

        row = lax.broadcasted_iota(jnp.int32, (CHUNK, CHUNK), 0)
        col = lax.broadcasted_iota(jnp.int32, (CHUNK, CHUNK), 1)
        upper = (col >= row).astype(f32)
        last_row = lax.broadcasted_iota(jnp.int32, (CHUNK, hp * HEAD_DIM), 0) == CHUNK - 1

        qp, fp = q_ref[...], f_ref[...]
        lb, sig, sigm, f, k, sq, qf, b = _hg_gates(qp, fp, lbl_ref[...])
        v16 = i_ref[...].astype(bf16)
        gnv = jnp.tile(gn_ref[...], (1, hp))
        gp = g_ref[...]
        sgg = _sigmoid(gp)
        gate = gp * sgg
        o = o_ref[...]
        r = lax.rsqrt(_head_mean(o * o, hp) + EPS)
        on = o * r
        dy = dy_ref[...]
        dg_ref[...] = (dy * on * gnv * (sgg * (1.0 + gp * (1.0 - sgg)))).astype(bf16)
        dgn_wide = jnp.sum(dy * on * gate, axis=0, keepdims=True)
        dgn_ref[...] += functools.reduce(jnp.add, [dgn_wide[:, p * HEAD_DIM:(p + 1) * HEAD_DIM] for p in range(hp)])
        don = dy * gnv * gate
        do16 = (r * (don - on * _head_mean(don * on, hp))).astype(bf16)
        eb = jnp.exp(b)
        A = qf * eb
        a16 = A.astype(bf16)
        bl = b[CHUNK - 1:CHUNK]
        ebl = jnp.exp(bl)
        ekd = jnp.exp(bl - b)
        kd = k * ekd
        kd16 = kd.astype(bf16)
        blocks = [_hg_block(qf, k, b, I) for I in range(nb)]

        heads = [slice(p * HEAD_DIM, (p + 1) * HEAD_DIM) for p in range(hp)]
        st_prev = st_ref[:, 0]
        dst_new = dstate[...]
        st16, dst16 = st_prev.astype(bf16), dst_new.astype(bf16)
        dA_h = [_dot(do16[:, sl], st16[p], NN) for p, sl in enumerate(heads)]
        dkd_h = [_dot(v16[:, sl], dst16[p], NN) for p, sl in enumerate(heads)]
        dv_h = [_dot(kd16[:, sl], dst16[p], NT) for p, sl in enumerate(heads)]
        dstate[...] = jnp.stack([dst_new[p] * ebl[:, sl] + _dot(do16[:, sl], a16[:, sl], TN) for p, sl in enumerate(heads)])
        dbl_h = [jnp.sum(dst_new[p] * st_prev[p], axis=0, keepdims=True) for p in range(hp)]
        sc_h = [[jnp.where(mask, _dot(qt16[:, sl], kt16[:, sl], NT), 0.0).astype(bf16) for sl in heads]
                for _, _, _, _, qt16, kt16, mask in blocks]
        dsc_h = [[jnp.where(mask, _dot(do16[r0:n, sl], v16[0:n, sl], NT), 0.0).astype(bf16) for sl in heads]
                 for r0, n, _, _, _, _, mask in blocks]
        dvi_h = [[_dot(sc, do16[blk[0]:blk[1], sl], TN) for sc, sl in zip(scs, heads)] for blk, scs in zip(blocks, sc_h)]
        dqt_h = [[_dot(dsc, blk[5][:, sl], NN) for dsc, sl in zip(dscs, heads)] for blk, dscs in zip(blocks, dsc_h)]
        dkt_h = [[_dot(dsc, blk[4][:, sl], TN) for dsc, sl in zip(dscs, heads)] for blk, dscs in zip(blocks, dsc_h)]
        dv_h = [functools.reduce(jnp.add, [dv_h[p]] + [_pad_rows(rows[p], CHUNK) for rows in dvi_h]) for p in range(hp)]

        dA, dkd = jnp.concatenate(dA_h, axis=1), jnp.concatenate(dkd_h, axis=1)
        dbl = jnp.concatenate(dbl_h, axis=1) * ebl + jnp.sum(dkd * kd, axis=0, keepdims=True)
        db = dA * A - dkd * kd + jnp.where(last_row, dbl, 0.0)
        dk = dkd * ekd
        dq_rows, db_rows = [], []
        for I, (r0, n, eq, ek, qt16, kt16, _) in enumerate(blocks):
            dqt, dkt = jnp.concatenate(dqt_h[I], axis=1), jnp.concatenate(dkt_h[I], axis=1)
            dq_rows.append(dqt * eq)
            dk = dk + _pad_rows(dkt * ek, CHUNK)
            db_rows.append(dqt * qt16.astype(f32))
            db = db - _pad_rows(dkt * kt16.astype(f32), CHUNK)
        dqf = dA * eb + jnp.concatenate(dq_rows, axis=0)
        db = db + jnp.concatenate(db_rows, axis=0)
        dlogf = _dot(upper, db, NN, HIGHEST)
        dfg = dlogf / f
        df_ref[...] = ((1.0 - lb) * sig * sigm * (dfg - dk)).astype(bf16)
        dlb = jnp.sum(sigm * (dfg - dk), axis=0, keepdims=True)
        dl0 = dlb * lb * (1.0 - lb)
        dlbl_ref[...] += jnp.concatenate([dl0, -dl0], axis=0)
        dq_ref[...] = (dqf * (sq * (1.0 + qp * (1.0 - sq)))).astype(bf16)
        di_ref[...] = jnp.concatenate(dv_h, axis=1).astype(bf16)

    cw = hp * HEAD_DIM
    dy_spec = pl.BlockSpec((CHUNK, cw), lambda h, c: (nc - 1 - c, h))
    return _grid_call(
        body,
        rider,
        name="hgrn_bwd",
        grid=(ngrp, nc),
        in_specs=[pspec(0), pspec(1), pspec(2), pspec(3), act, st_spec, dy_spec, lbl_spec, gn_spec],
        out_specs=[pl.BlockSpec((CHUNK, 4 * W), lambda h, c: (nc - 1 - c, 0)), lbl_spec, gn_spec],
        out_shape=[jax.ShapeDtypeStruct((S, 4 * W), bf16), jax.ShapeDtypeStruct((2, W), f32), jax.ShapeDtypeStruct((1, HEAD_DIM), f32)],
        scratch_shapes=[pltpu.VMEM((hp, HEAD_DIM, HEAD_DIM), f32)],
        args=(proj, proj, proj, proj, o_hg, st, dycat, lb_logits, hg_norm),
    )


def _att_dims(S):
    qt = ATT_Q_ROWS if S % ATT_Q_ROWS == 0 else CHUNK
    pad = LEFT_CHUNKS * CHUNK
    kb = pad + qt
    return qt, pad, kb, kb + qt


def _kv_padded(proj, W, pad):
    S = proj.shape[0]
    ts = _div(pad, 512, 2 * SUBLANES)
    assert S % ts == 0 and pad % ts == 0
    npad = pad // ts

    def body(k_ref, v_ref, ko_ref, vo_ref):
        i = pl.program_id(0)

        @pl.when(i < npad)
        def _():
            ko_ref[...] = jnp.zeros_like(ko_ref)
            vo_ref[...] = jnp.zeros_like(vo_ref)

        @pl.when(i >= npad)
        def _():
            ko_ref[...] = k_ref[...].astype(bf16)
            vo_ref[...] = v_ref[...].astype(bf16)

    out = pl.BlockSpec((ts, W), lambda i: (i, 0))
    return pl.pallas_call(
        body,
        name="kv_padded",
        grid=((S + pad) // ts,),
        in_specs=[pl.BlockSpec((ts, W), lambda i: (jnp.maximum(i - npad, 0), 5)), pl.BlockSpec((ts, W), lambda i: (jnp.maximum(i - npad, 0), 6))],
        out_specs=[out, out],
        out_shape=[jax.ShapeDtypeStruct((S + pad, W), bf16)] * 2,
        compiler_params=pltpu.CompilerParams(dimension_semantics=("parallel",)),
    )(proj, proj)


def _att_probs(qk, bias, start, qt, pad, kb):
    s = qk * (HEAD_DIM ** -0.5) + bias
    row = lax.broadcasted_iota(jnp.int32, (qt, kb), 0)
    col = lax.broadcasted_iota(jnp.int32, (qt, kb), 1)
    lo = jnp.bitwise_and(row, -CHUNK)
    ok = (col >= lo) & (col < lo + pad + CHUNK) & (col + start >= pad)
    s = jnp.where(ok, s, NEG)
    e = jnp.exp(s - jnp.max(s, axis=-1, keepdims=True))
    return e / jnp.sum(e, axis=-1, keepdims=True)


def _att_bias(ext_row, qt, kb, ne):
    e = jnp.broadcast_to(ext_row, (qt, ne))
    return pltpu.roll(e, ne - qt + 1, 1, stride=1, stride_axis=0)[:, :kb]


def _att_specs(S, H, ah, qt, pad, ne):
    cw = ah * HEAD_DIM
    q_spec = pl.BlockSpec((qt, cw), lambda h, g: (g, 4 * H // ah + h))
    kv_spec = pl.BlockSpec((S + pad, cw), lambda h, g: (0, h))
    ext_spec = pl.BlockSpec((ah, 1, ne), lambda h, g: (h, 0, 0))
    row_spec = pl.BlockSpec((qt, cw), lambda h, g: (g, h))
    return cw, q_spec, kv_spec, ext_spec, row_spec


def _attn_fwd(proj, kp, vp, ext, W, rider=None):
    S = proj.shape[0]
    H = W // HEAD_DIM
    ah = min(ATT_HEADS_PER_STEP, H)
    qt, pad, kb, ne = _att_dims(S)
    cw, q_spec, kv_spec, ext_spec, row_spec = _att_specs(S, H, ah, qt, pad, ne)
    heads = [slice(a * HEAD_DIM, (a + 1) * HEAD_DIM) for a in range(ah)]

    def body(q_ref, k_ref, v_ref, ext_ref, o_ref, bias):
        g = pl.program_id(1)

        @pl.when(g == 0)
        def _():
            for a in range(ah):
                bias[a] = _att_bias(ext_ref[a], qt, kb, ne)

        start = pl.multiple_of(g * qt, qt)
        q16 = q_ref[...].astype(bf16)
        kb16 = k_ref[pl.ds(start, kb), :]
        vb16 = v_ref[pl.ds(start, kb), :]
        qk = [_dot(q16[:, sl], kb16[:, sl], NT) for sl in heads]
        pn = [_att_probs(qk[a], bias[a], start, qt, pad, kb).astype(bf16) for a in range(ah)]
        o_ref[...] = jnp.concatenate([_dot(pn[a], vb16[:, sl], NN) for a, sl in enumerate(heads)], axis=1).astype(bf16)

    outs = _grid_call(
        body,
        rider,
        name="attn_fwd",
        grid=(H // ah, S // qt),
        in_specs=[q_spec, kv_spec, kv_spec, ext_spec],
        out_specs=[row_spec],
        out_shape=[jax.ShapeDtypeStruct((S, W), bf16)],
        scratch_shapes=[pltpu.VMEM((ah, qt, kb), f32)],
        args=(proj, kp, vp, ext),
    )
    return outs[0] if rider is None else outs


def _attn_bwd(proj, kp, vp, ext, dycat, W, rider=None):
    S = proj.shape[0]
    H = W // HEAD_DIM
    ah = min(ATT_HEADS_PER_STEP, H)
    qt, pad, kb, ne = _att_dims(S)
    cw, q_spec, kv_spec, ext_spec, row_spec = _att_specs(S, H, ah, qt, pad, ne)
    heads = [slice(a * HEAD_DIM, (a + 1) * HEAD_DIM) for a in range(ah)]
    scale = HEAD_DIM ** -0.5
    ng = S // qt

    def body(q_ref, k_ref, v_ref, ext_ref, do_ref, dq_ref, dko_ref, dvo_ref, db_ref, bias, dk_ref, dv_ref):
        g = pl.program_id(1)

        @pl.when(g == 0)
        def _():
            for a in range(ah):
                bias[a] = _att_bias(ext_ref[a], qt, kb, ne)
            dk_ref[...] = jnp.zeros_like(dk_ref)
            dv_ref[...] = jnp.zeros_like(dv_ref)
            db_ref[...] = jnp.zeros_like(db_ref)

        start = pl.multiple_of(g * qt, qt)
        q16 = q_ref[...].astype(bf16)
        kb16 = k_ref[pl.ds(start, kb), :]
        vb16 = v_ref[pl.ds(start, kb), :]
        do16 = do_ref[...].astype(bf16)
        qk = [_dot(q16[:, sl], kb16[:, sl], NT) for sl in heads]
        dpn = [_dot(do16[:, sl], vb16[:, sl], NT) for sl in heads]
        pn = [_att_probs(qk[a], bias[a], start, qt, pad, kb) for a in range(ah)]
        ds = [pn[a] * (dpn[a] - jnp.sum(dpn[a] * pn[a], axis=-1, keepdims=True)) for a in range(ah)]
        ds16 = [d.astype(bf16) for d in ds]
        dv_ref[pl.ds(start, kb), :] += jnp.concatenate([_dot(pn[a].astype(bf16), do16[:, sl], TN) for a, sl in enumerate(heads)], axis=1)
        dq_ref[...] = (jnp.concatenate([_dot(ds16[a], kb16[:, sl], NN) for a, sl in enumerate(heads)], axis=1) * scale).astype(bf16)
        dk_ref[pl.ds(start, kb), :] += jnp.concatenate([_dot(ds16[a], q16[:, sl], TN) for a, sl in enumerate(heads)], axis=1) * scale
        for a in range(ah):
            db_ref[a] += ds[a]

        @pl.when(g == ng - 1)
        def _():
            dko_ref[...] = dk_ref[pl.ds(pad, S), :].astype(bf16)
            dvo_ref[...] = dv_ref[pl.ds(pad, S), :].astype(bf16)

    kv_out = pl.BlockSpec((S, cw), lambda h, g: (0, h))
    return _grid_call(
        body,
        rider,
        name="attn_bwd",
        grid=(H // ah, S // qt),
        in_specs=[q_spec, kv_spec, kv_spec, ext_spec,
                  pl.BlockSpec((qt, cw), lambda h, g: (g, H // ah + h))],
        out_specs=[row_spec, kv_out, kv_out, pl.BlockSpec((ah, qt, kb), lambda h, g: (h, 0, 0))],
        out_shape=[
            jax.ShapeDtypeStruct((S, W), bf16),
            jax.ShapeDtypeStruct((S, W), bf16),
            jax.ShapeDtypeStruct((S, W), bf16),
            jax.ShapeDtypeStruct((H, qt, kb), f32),
        ],
        scratch_shapes=[pltpu.VMEM((ah, qt, kb), f32), pltpu.VMEM((S + pad, cw), f32), pltpu.VMEM((S + pad, cw), f32)],
        args=(proj, kp, vp, ext, dycat),
    )


def _bias_onehot(qt, ne, nrel_pad):
    m = lax.broadcasted_iota(jnp.int32, (nrel_pad, ne), 1)
    r = lax.broadcasted_iota(jnp.int32, (nrel_pad, ne), 0)
    rel = LEFT_CHUNKS * CHUNK + qt - 1 - m
    hot = (r == jnp.clip(rel, -REL_CLIP, REL_CLIP) + REL_CLIP) & (m < ne - 1)
    return hot.astype(f32)


def _bias_ext(rel_bias_pad, onehot):
    H, _, nr = rel_bias_pad.shape
    ne = onehot.shape[1]

    def body(rb_ref, oh_ref, o_ref):
        o_ref[0] = _dot(rb_ref[0], oh_ref[...], NN, HIGHEST)

    return pl.pallas_call(
        body,
        name="bias_ext",
        grid=(H,),
        in_specs=[pl.BlockSpec((1, 1, nr), lambda h: (h, 0, 0)), pl.BlockSpec((nr, ne), lambda h: (0, 0))],
        out_specs=pl.BlockSpec((1, 1, ne), lambda h: (h, 0, 0)),
        out_shape=jax.ShapeDtypeStruct((H, 1, ne), f32),
        compiler_params=pltpu.CompilerParams(dimension_semantics=("parallel",)),
    )(rel_bias_pad, onehot)


def _bias_bwd(dbias, onehot):
    H, qt, kb = dbias.shape
    nr, ne = onehot.shape

    def body(d_ref, oh_ref, o_ref):
        row = lax.broadcasted_iota(jnp.int32, (qt, qt), 0)
        col = lax.broadcasted_iota(jnp.int32, (qt, qt), 1)
        flip = (row + col == qt - 1).astype(f32)
        x = _dot(flip, d_ref[0], NN, HIGHEST)
        x = jnp.concatenate([x, jnp.zeros((qt, ne - kb), f32)], axis=1)
        de = jnp.sum(pltpu.roll(x, 0, 1, stride=1, stride_axis=0), axis=0, keepdims=True)
        o_ref[0] = _dot(de, oh_ref[...], NT, HIGHEST)

    return pl.pallas_call(
        body,
        name="bias_bwd",
        grid=(H,),
        in_specs=[pl.BlockSpec((1, qt, kb), lambda h: (h, 0, 0)), pl.BlockSpec((nr, ne), lambda h: (0, 0))],
        out_specs=pl.BlockSpec((1, 1, nr), lambda h: (h, 0, 0)),
        out_shape=jax.ShapeDtypeStruct((H, 1, nr), f32),
        compiler_params=pltpu.CompilerParams(dimension_semantics=("parallel",)),
    )(dbias, onehot)


def _place():
    x, y, c = lax.axis_index("x"), lax.axis_index("y"), lax.axis_index("c")
    return x, y, c


def _flip(x, y, k):
    return (1 - x if k & 2 else x), (1 - y if k & 1 else y)


def _region(ref, shard_shape, axis, j, half):
    R, C = shard_shape
    if axis == 0:
        if half is None:
            return ref.at[pl.ds(j * R, R), :]
        return ref.at[pl.ds(j * R + half * (R // 2), R // 2), :]
    if half is None:
        return ref.at[:, pl.ds(j * C, C)]
    return ref.at[pl.ds(half * (R // 2), R // 2), pl.ds(j * C, C)]


class _Remote:
    def __init__(self, src, dst, send_sem, recv_sem, dev):
        self.src, self.dst, self.sems, self.dev = src, dst, (send_sem, recv_sem), dev

    def _copy(self, src, dst):
        return pltpu.make_async_remote_copy(src_ref=src, dst_ref=dst, send_sem=self.sems[0], recv_sem=self.sems[1],
                                            device_id=self.dev, device_id_type=MESH)

    def start(self):
        rows = self.src.shape[0]
        n = RDMA_PIECES if rows % (RDMA_PIECES * 2 * SUBLANES) == 0 else 1
        for p in range(n):
            part = pl.ds(p * (rows // n), rows // n)
            self._copy(self.src.at[part], self.dst.at[part]).start()

    def wait_recv(self):
        self._copy(self.src, self.dst).wait_recv()

    def wait_send(self):
        self._copy(self.src, self.dst).wait_send()


def _remote(src, dst, send_sem, recv_sem, dev):
    return _Remote(src, dst, send_sem, recv_sem, dev)


def _cast_place(shard, axis, where, *, name):
    R, C = shard.shape
    tr = _div(R, 256, 2 * SUBLANES)
    nr = R // tr
    full = (N_CHIPS * R, C) if axis == 0 else (R, N_CHIPS * C)
    omap = (lambda r, s: (s[0] * nr + r, 0)) if axis == 0 else (lambda r, s: (r, s[0]))

    def body(s_ref, x_ref, o_ref):
        o_ref[...] = x_ref[...].astype(bf16)

    return pl.pallas_call(
        body,
        name=name,
        grid_spec=pltpu.PrefetchScalarGridSpec(
            num_scalar_prefetch=1,
            grid=(nr,),
            in_specs=[pl.BlockSpec((tr, C), lambda r, s: (r, 0))],
            out_specs=pl.BlockSpec((tr, C), omap),
        ),
        out_shape=jax.ShapeDtypeStruct(full, bf16),
        compiler_params=pltpu.CompilerParams(dimension_semantics=("parallel",)),
    )(where, shard)


class _Rider(NamedTuple):
    ins: list
    out_shapes: list
    aliases: dict
    sems: list
    first: Callable
    last: Callable


def _run_rider(rider, *, name):
    ni, no = len(rider.ins), len(rider.out_shapes)

    def body(*refs):
        ins, outs, sems = refs[:ni], refs[ni:ni + no], refs[ni + no:]
        rider.first(ins, outs, sems)
        rider.last(ins, outs, sems)

    return pl.pallas_call(
        body,
        name=name,
        in_specs=[_hbm()] * ni,
        out_specs=[_hbm()] * no,
        out_shape=rider.out_shapes,
        input_output_aliases=rider.aliases,
        scratch_shapes=rider.sems,
    )(*rider.ins)


def _start_all(copies):
    def first(ins, outs, sems):
        for cp in copies(ins, outs, sems):
            cp.start()

    def last(ins, outs, sems):
        cps = copies(ins, outs, sems)
        for cp in cps:
            cp.wait_recv()
        for cp in cps:
            cp.wait_send()

    return first, last


def _relay_copy(buf, shard_shape, axis, send_sem, recv_sem):
    x, y, c = _place()
    first = c == 0
    sx, sy = jnp.where(first, 1 - x, x), jnp.where(first, y, 1 - y)
    dx, dy = jnp.where(first, x, 1 - x), jnp.where(first, 1 - y, y)
    part = _region(buf, shard_shape, axis, 2 * sx + sy, c)
    return _remote(part, part, send_sem, recv_sem, (dx, dy, c))


def _mm_while_gathering(a, full, shard_shape, where, *, tm, tn, name):
    S, K = a.shape
    R, Cs = shard_shape
    N = full.shape[1]
    tps, nj, ni = Cs // tn, N // tn, S // tm
    assert R == K and Cs % tn == 0 and S % tm == 0

    def block(j, chip):
        return jnp.bitwise_xor(chip, j // tps) * tps + j % tps

    def body(s_ref, a_ref, w_in, o_ref, w_ref, buf, fsem, send1, recv1, send2, recv2):
        j, i = pl.program_id(0), pl.program_id(1)
        x, y, c = _place()
        chip = 2 * x + y

        def ici(k):
            px, py = _flip(x, y, k)
            mine = _region(w_ref, shard_shape, 1, chip, c)
            return _remote(mine, mine, send1.at[k - 1], recv1.at[k - 1], (px, py, c))

        def handed(k, half):
            px, py = _flip(x, y, k)
            land = _region(w_ref, shard_shape, 1, 2 * px + py, half)
            return land, _remote(land, land, send2.at[k - 1], recv2.at[k - 1], (x, y, 1 - c))

        def tile(jj):
            col = pl.multiple_of(block(jj, chip) * tn, LANES)
            return pltpu.make_async_copy(w_ref.at[:, pl.ds(col, tn)], buf.at[jj % 2], fsem.at[jj % 2])

        def relay():
            return _relay_copy(w_ref, shard_shape, 1, send1.at[2], recv1.at[2])

        def landed(k):
            land, forward = handed(k, c)
            _remote(land, land, send1.at[k - 1], recv1.at[k - 1], (x, y, c)).wait_recv()
            forward.start()

        @pl.when((j == 0) & (i == 0))
        def _():
            for k in (1, 2):
                ici(k).start()
            tile(j).start()

        @pl.when(i == 0)
        def _():
            tile(j).wait()

        @pl.when((i == ni - 1) & (j + 1 < nj))
        def _():
            @pl.when(j + 1 == tps)
            def _():
                landed(1)
                landed(2)
                relay().start()
                handed(1, 1 - c)[1].wait_recv()

            @pl.when(j + 1 == 2 * tps)
            def _():
                handed(2, 1 - c)[1].wait_recv()

            @pl.when(j + 1 == 3 * tps)
            def _():
                landed(3)
                handed(3, 1 - c)[1].wait_recv()

            tile(j + 1).start()

        o_ref[...] = _dot(a_ref[...], buf[j % 2], NN)

        @pl.when((j == nj - 1) & (i == ni - 1))
        def _():
            ici(1).wait_send()
            ici(2).wait_send()
            relay().wait_send()
            for k in (1, 2, 3):
                handed(k, c)[1].wait_send()

    sems = pltpu.SemaphoreType.DMA((3,))
    spec = pltpu.PrefetchScalarGridSpec(
        num_scalar_prefetch=1,
        grid=(nj, ni),
        in_specs=[pl.BlockSpec((tm, K), lambda j, i, s: (i, 0)), _hbm()],
        out_specs=[pl.BlockSpec((tm, tn), lambda j, i, s: (i, block(j, s[0]))), _hbm()],
        scratch_shapes=[pltpu.VMEM((2, K, tn), full.dtype), pltpu.SemaphoreType.DMA((2,)), sems, sems, sems, sems],
    )
    return pl.pallas_call(
        body,
        name=name,
        grid_spec=spec,
        out_shape=[jax.ShapeDtypeStruct((S, N), f32), jax.ShapeDtypeStruct(full.shape, full.dtype)],
        input_output_aliases={2: 1},
        compiler_params=pltpu.CompilerParams(dimension_semantics=("arbitrary", "arbitrary")),
    )(where, a, full)


class _GatherJob(NamedTuple):
    buf: int
    shard_shape: tuple
    axis: int
    split: bool
    rels: tuple
    hand_over: bool
    relay: bool = False


def _gather_rider(bufs, jobs):
    def copies(ins, outs, sems):
        x, y, c = _place()
        cps = []
        for n, job in enumerate(jobs):
            if job.relay:
                cps.append(_relay_copy(outs[job.buf], job.shard_shape, job.axis, sems[0].at[3 * n + 2], sems[1].at[3 * n + 2]))
                continue
            half = c if job.split else None
            for k in job.rels:
                px, py = _flip(x, y, k)
                chip, dev = (2 * px + py, (x, y, 1 - c)) if job.hand_over else (2 * x + y, (px, py, c))
                cps.append(_remote(_region(ins[job.buf], job.shard_shape, job.axis, chip, half),
                                   _region(outs[job.buf], job.shard_shape, job.axis, chip, half),
                                   sems[0].at[3 * n + k - 1], sems[1].at[3 * n + k - 1], dev))
        return cps

    assert all(job.split or not job.hand_over for job in jobs)
    sems = pltpu.SemaphoreType.DMA((3 * len(jobs),))
    return _Rider(list(bufs), [jax.ShapeDtypeStruct(b.shape, b.dtype) for b in bufs], {m: m for m in range(len(bufs))},
                  [sems, sems], *_start_all(copies))


def _chips_rider(ts, rels=(1, 2, 3), landing=None):
    n = len(ts)

    def copies(ins, outs, sems):
        x, y, c = _place()
        cps = []
        for m in range(n):
            for k in rels:
                px, py = _flip(x, y, k)
                cps.append(_remote(ins[m].at[2 * px + py], outs[m].at[k - 1], sems[0].at[3 * m + k - 1], sems[1].at[3 * m + k - 1], (px, py, c)))
        return cps

    sems = pltpu.SemaphoreType.DMA((3 * n,))
    return _Rider(list(ts) + list(landing or []), [jax.ShapeDtypeStruct((3,) + t.shape[1:], t.dtype) for t in ts],
                  {n + m: m for m in range(n)} if landing else {}, [sems, sems], *_start_all(copies))


def _sibling_rider(dws, shard_shapes, axes):
    n = len(dws)

    def copies(ins, outs, sems):
        x, y, c = _place()
        return [_remote(_region(ins[m], shard_shapes[m], axes[m], j, 1 - c), outs[m].at[j],
                        sems[0].at[N_CHIPS * m + j], sems[1].at[N_CHIPS * m + j], (x, y, 1 - c))
                for m in range(n) for j in range(N_CHIPS)]

    sems = pltpu.SemaphoreType.DMA((N_CHIPS * n,))
    return _Rider(list(dws), [jax.ShapeDtypeStruct((N_CHIPS, s[0] // 2, s[1]), d.dtype) for s, d in zip(shard_shapes, dws)], {},
                  [sems, sems], *_start_all(copies))


def _join_rider(gs):
    n = len(gs)

    def copies(ins, outs, sems):
        x, y, c = _place()
        cps = []
        for m in range(n):
            hr = gs[m].shape[0] // 2
            cps.append(_remote(ins[m].at[pl.ds(c * hr, hr), :], outs[m].at[pl.ds(c * hr, hr), :], sems[0].at[m], sems[1].at[m], (x, y, 1 - c)))
        return cps

    sems = pltpu.SemaphoreType.DMA((n,))
    return _Rider(list(gs), [jax.ShapeDtypeStruct(g.shape, g.dtype) for g in gs], {m: m for m in range(n)}, [sems, sems], *_start_all(copies))


def _both(r1, r2):
    i1, o1, s1 = len(r1.ins), len(r1.out_shapes), len(r1.sems)
    aliases = dict(r1.aliases)
    aliases.update({i1 + i: o1 + o for i, o in r2.aliases.items()})

    def first(ins, outs, sems):
        r1.first(ins[:i1], outs[:o1], sems[:s1])
        r2.first(ins[i1:], outs[o1:], sems[s1:])

    def last(ins, outs, sems):
        r1.last(ins[:i1], outs[:o1], sems[:s1])
        r2.last(ins[i1:], outs[o1:], sems[s1:])

    return _Rider(r1.ins + r2.ins, r1.out_shapes + r2.out_shapes, aliases, r1.sems + r2.sems, first, last)


def _pair_sum(dw, got, shard_shape, axis, where, *, name):
    R, C = shard_shape
    hr = R // 2
    tr = _div(hr, 256, 2 * SUBLANES)
    nr = hr // tr
    dmap = (lambda j, r, s: ((2 * j + s[1]) * nr + r, 0)) if axis == 0 else (lambda j, r, s: (s[1] * nr + r, j))
    slot = pl.BlockSpec((1, tr, C), lambda j, r, s: (j, r, 0))

    def body(s_ref, d_ref, g_ref, o_ref):
        o_ref[0] = (d_ref[...].astype(f32) + g_ref[0].astype(f32)).astype(bf16)

    return pl.pallas_call(
        body,
        name=name,
        grid_spec=pltpu.PrefetchScalarGridSpec(
            num_scalar_prefetch=1,
            grid=(N_CHIPS, nr),
            in_specs=[pl.BlockSpec((tr, C), dmap), slot],
            out_specs=slot,
        ),
        out_shape=jax.ShapeDtypeStruct((N_CHIPS, hr, C), bf16),
        compiler_params=pltpu.CompilerParams(dimension_semantics=("parallel", "parallel")),
    )(where, dw, got)


def _chip_sum(pair, others, where, *, name):
    _, hr, C = pair.shape
    tr = _div(hr, 256, 2 * SUBLANES)
    nr = hr // tr

    def body(s_ref, p_ref, o_ref, g_ref):
        g_ref[...] = p_ref[0].astype(f32) + o_ref[0].astype(f32) + o_ref[1].astype(f32) + o_ref[2].astype(f32)

    return pl.pallas_call(
        body,
        name=name,
        grid_spec=pltpu.PrefetchScalarGridSpec(
            num_scalar_prefetch=1,
            grid=(nr,),
            in_specs=[pl.BlockSpec((1, tr, C), lambda r, s: (s[0], r, 0)), pl.BlockSpec((3, tr, C), lambda r, s: (0, r, 0))],
            out_specs=pl.BlockSpec((tr, C), lambda r, s: (s[1] * nr + r, 0)),
        ),
        out_shape=jax.ShapeDtypeStruct((2 * hr, C), f32),
        compiler_params=pltpu.CompilerParams(dimension_semantics=("parallel",)),
    )(where, pair, others)


def _allreduce_small(v):
    rows = v.shape[0]

    def body(v_ref, o_ref, gath, send, recv):
        x, y, c = _place()
        me = 4 * x + 2 * y + c
        gath[pl.ds(me, 1)] = v_ref[...][None]
        cps = []
        for k in range(1, N_DEV):
            peer = (1 - x if k & 4 else x, 1 - y if k & 2 else y, 1 - c if k & 1 else c)
            cp = _remote(v_ref, gath.at[me], send.at[k - 1], recv.at[k - 1], peer)
            cp.start()
            cps.append(cp)
        for cp in cps:
            cp.wait_recv()
        for cp in cps:
            cp.wait_send()
        acc = gath[0]
        for d in range(1, N_DEV):
            acc = acc + gath[d]
        o_ref[...] = acc

    return pl.pallas_call(
        body,
        name="small_allreduce",
        in_specs=[_vmem()],
        out_specs=_vmem(),
        out_shape=jax.ShapeDtypeStruct(v.shape, f32),
        scratch_shapes=[pltpu.VMEM((N_DEV, rows, LANES), f32), pltpu.SemaphoreType.DMA((N_DEV - 1,)), pltpu.SemaphoreType.DMA((N_DEV - 1,))],
    )(v)


def _adamw_math(w, g, m, v):
    m = ADAM_B1 * m + (1.0 - ADAM_B1) * g
    v = ADAM_B2 * v + (1.0 - ADAM_B2) * (g * g)
    m_hat = m / (1.0 - ADAM_B1 ** ADAM_STEP)
    v_hat = v / (1.0 - ADAM_B2 ** ADAM_STEP)
    delta = -ADAM_LR * (m_hat / (jnp.sqrt(v_hat) + ADAM_EPS) + ADAM_WD * w)
    return delta, m, v


def _adamw(w, g, m, v, *, name):
    R, C = w.shape
    tr = _div(R, 128, SUBLANES)

    def body(w_ref, g_ref, m_ref, v_ref, go_ref, d_ref, mo_ref, vo_ref):
        gv = g_ref[...]
        d, mn, vn = _adamw_math(w_ref[...], gv, m_ref[...], v_ref[...])
        go_ref[...] = gv
        d_ref[...] = d
        mo_ref[...] = mn
        vo_ref[...] = vn

    spec = pl.BlockSpec((tr, C), lambda r: (r, 0))
    return pl.pallas_call(
        body,
        name=name,
        grid=(R // tr,),
        in_specs=[spec] * 4,
        out_specs=[spec] * 4,
        out_shape=[jax.ShapeDtypeStruct((R, C), f32)] * 4,
        compiler_params=pltpu.CompilerParams(dimension_semantics=("parallel",)),
    )(w, g, m, v)


def _pack(arrs):
    flat = jnp.concatenate([a.reshape(-1).astype(f32) for a in arrs])
    tile = SUBLANES * LANES
    total = -(-flat.shape[0] // tile) * tile
    return jnp.pad(flat, (0, total - flat.shape[0])).reshape(total // LANES, LANES)


def _unpack(buf, shapes):
    flat = buf.reshape(-1)
    out, off = [], 0
    for s in shapes:
        size = int(np.prod(s))
        out.append(flat[off:off + size].reshape(s))
        off += size
    return out


def kernel(x, p, norm_mix, w_in, lb_logits, hg_norm, rel_bias, w_out, norm_ffn, w_up, conv_w, conv_b, w_down, norm_ple, w_ple_gate, w_ple_proj, final_norm, loss_target, m_norm_mix, m_w_in, m_lb_logits, m_hg_norm, m_rel_bias, m_w_out, m_norm_ffn, m_w_up, m_conv_w, m_conv_b, m_w_down, m_norm_ple, m_w_ple_gate, m_w_ple_proj, m_final_norm, v_norm_mix, v_w_in, v_lb_logits, v_hg_norm, v_rel_bias, v_w_out, v_norm_ffn, v_w_up, v_conv_w, v_conv_b, v_w_down, v_norm_ple, v_w_ple_gate, v_w_ple_proj, v_final_norm):
    S, D = x.shape[1], x.shape[2]
    xv, pv, tgt = x[0], p[0, 0], loss_target[0]
    W = (w_in.shape[2] * N_CHIPS) // 7
    H = W // HEAD_DIM
    F = w_down.shape[1] * N_CHIPS
    tc = _div(F // 2, 1408, LANES)
    nj = F // tc
    jx, jy = lax.axis_index("x"), lax.axis_index("y")
    chip = 2 * jx + jy

    big = [w_in[0], w_out[0], w_up[0], w_down[0], w_ple_gate[0], w_ple_proj[0]]
    big_axes = [1, 0, 1, 0, 0, 1]
    where = jnp.stack([chip, lax.axis_index("c")]).astype(jnp.int32)
    cw_pad = jnp.pad(conv_w[0], ((0, SUBLANES - conv_w.shape[1]), (0, 0)))
    cw_mine = lax.dynamic_update_slice(jnp.zeros((N_CHIPS * SUBLANES, cw_pad.shape[1]), f32), cw_pad, (chip * SUBLANES, 0))
    placed = [_cast_place(b, ax, where, name=f"cast_place_{i}") for i, (b, ax) in enumerate(zip(big, big_axes))]
    shard_shapes = [b.shape for b in big]

    every = (1, 2, 3)

    def job(buf, i, rels, hand_over=False, relay=False):
        if i is None:
            return _GatherJob(buf, cw_pad.shape, 0, False, rels, hand_over)
        return _GatherJob(buf, shard_shapes[i], big_axes[i], True, rels, hand_over, relay)

    tm = _div(S, 1024, LANES)

    a1 = _rms_fwd(xv, norm_mix, name="rms_mix")
    cs_in = shard_shapes[0][1]
    tn_in = _div(cs_in, 1024, LANES)
    proj, W_in = _mm_while_gathering(a1, placed[0], shard_shapes[0], where, tm=tm, tn=tn_in, name="mm_in")
    qt, pad, kb, ne = _att_dims(S)
    nrel = rel_bias.shape[2]
    nrel_pad = -(-nrel // LANES) * LANES
    onehot = _bias_onehot(qt, ne, nrel_pad)
    rb_pad = jnp.pad(rel_bias[0], ((0, 0), (0, nrel_pad - nrel)))[:, None, :]
    ext = _bias_ext(rb_pad, onehot)
    kp, vp = _kv_padded(proj, W, pad)
    y_att, W_out_i, W_up_i = _attn_fwd(
        proj, kp, vp, ext, W, rider=_gather_rider([placed[1], placed[2]], [job(0, 1, every), job(1, 2, (1, 2))]))
    y_hg, o_hg, st, W_out, W_up_i = _hgrn_fwd(
        proj, lb_logits, hg_norm, W,
        rider=_gather_rider([W_out_i, W_up_i], [job(0, 1, every, True), job(1, 2, (1, 2), True), job(1, 2, (3,), relay=True)]))
    ycat = jnp.concatenate([y_hg, y_att], axis=1)

    tn_d = _div(D, 512, LANES)
    h1, W_up = _mm(ycat, W_out, dims="nn", tm=tm, tn=tn_d, tk=D, out_dtype=f32, res=xv, name="mm_out",
                   rider=_gather_rider([W_up_i], [job(0, 2, (3,), True)]))
    a2 = _rms_fwd(h1, norm_ffn, name="rms_ffn")
    perm = lambda b: (b % 2) * nj + b // 2
    u, W_down_i, W_pg_i, W_pp_i, cw_all = _mm(
        a2, W_up, dims="nn", tm=tm, tn=tc, tk=D, out_dtype=f32, name="mm_up", b_idx=lambda i, j, k: (k, perm(j)),
        rider=_gather_rider([placed[3], placed[4], placed[5], cw_mine], [job(0, 3, every), job(1, 4, every), job(2, 5, every), job(3, None, every)]))
    ncw = conv_w.shape[1]
    cw_full = jnp.transpose(cw_all.reshape(N_CHIPS, SUBLANES, -1)[:, :ncw], (1, 0, 2)).reshape(ncw, -1)
    cb = conv_b
    mact, W_down, W_pg, W_pp = _convact_fwd(
        u, cw_full, cb, tc, rider=_gather_rider([W_down_i, W_pg_i, W_pp_i], [job(0, 3, every, True), job(1, 4, every, True), job(2, 5, every, True)]))
    tn_w = _div(D, 1024, LANES)
    h2 = _mm(mact, W_down, dims="nn", tm=_div(S, 512, LANES), tn=tn_w, tk=F, out_dtype=f32, res=h1, name="mm_down", cols_outer=True)
    a3 = _rms_fwd(h2, norm_ple, name="rms_ple")
    zg = _mm(a3, W_pg, dims="nn", tm=tm, tn=tn_d, tk=D, out_dtype=f32, name="mm_ple_gate")
    p16 = pv.astype(bf16)
    pp = _mm(p16, W_pp, dims="nn", tm=tm, tn=tn_d, tk=pv.shape[1], out_dtype=f32, name="mm_ple_proj")
    dh3, dzg, dpp, loss_part, d_fn = _tail(h2, zg, pp, tgt, final_norm.reshape(1, D))

    pair = {}

    def siblings(idx, grads):
        return _sibling_rider(grads, [shard_shapes[i] for i in idx], [big_axes[i] for i in idx])

    def pair_up(idx, grads, got):
        for i, d, g in zip(idx, grads, got):
            pair[i] = _pair_sum(d, g, shard_shapes[i], big_axes[i], where, name=f"grad_pair_sum_{i}")

    def reduced_half(i, landed):
        return _chip_sum(pair[i], landed, where, name=f"grad_chip_sum_{i}")

    tk_s = S
    dW_pp = _mm(p16, dpp, dims="tn", tm=pv.shape[1], tn=tn_w, tk=tk_s, out_dtype=bf16, name="mm_d_ple_proj")
    dW_pg = _mm(a3, dzg, dims="tn", tm=tn_w, tn=tn_w, tk=tk_s, out_dtype=bf16, name="mm_d_ple_gate")
    da3, *got = _mm(dzg, W_pg, dims="nt", tm=tm, tn=_div(D, 1024, LANES), tk=D, out_dtype=f32, name="mm_da3",
                    rider=siblings([4, 5], [dW_pg, dW_pp]))
    pair_up([4, 5], [dW_pg, dW_pp], got)
    dh2, dh2b, d_nple = _rms_bwd(h2, norm_ple, da3, dh3, name="rms_ple_bwd")
    dm, land4, land5 = _mm(dh2b, W_down, dims="nt", tm=tm, tn=_div(F, 512, LANES), tk=D, out_dtype=f32, name="mm_dm",
                           rider=_chips_rider([pair[4], pair[5]]))
    dW_down = _mm(mact, dh2b, dims="tn", tm=_div(F, 1408, LANES), tn=tn_d, tk=tk_s, out_dtype=bf16, name="mm_d_down")
    du, dcw_g, dcw_v, dcb_g, dcb_v, got3 = _convact_bwd(u, dm, cw_full, cb, tc, rider=siblings([3], [dW_down]))
    pair_up([3], [dW_down], [got3])
    dW_up, land3 = _mm(a2, du, dims="tn", tm=_div(D, 1024, LANES), tn=tc, tk=tk_s, out_dtype=bf16, name="mm_d_up",
                       o_idx=lambda i, j, k: (i, perm(j)), rider=_chips_rider([pair[3]]))
    da2, got2 = _mm(du, W_up, dims="nt", tm=tm, tn=D, tk=tc, out_dtype=f32, name="mm_da2",
                    b_idx=lambda i, j, k: (j, perm(k)), rider=siblings([2], [dW_up]))
    pair_up([2], [dW_up], [got2])
    dh1, dh1b, d_nffn = _rms_bwd(h1, norm_ffn, da2, dh2, name="rms_ffn_bwd")
    dycat = _mm(dh1b, W_out, dims="nt", tm=tm, tn=_div(D, 1024, LANES), tk=D, out_dtype=f32, name="mm_dycat")
    dW_out = _mm(ycat, dh1b, dims="tn", tm=tn_w, tn=tn_w, tk=tk_s, out_dtype=bf16, name="mm_d_out")
    dq_att, dk_att, dv_att, dbias, land2, got1 = _attn_bwd(
        proj, kp, vp, ext, dycat, W, rider=_both(_chips_rider([pair[2]], rels=(1, 2)), siblings([1], [dW_out])))
    pair_up([1], [dW_out], [got1])
    d_hg4, d_lbl, d_hgn, land2 = _hgrn_bwd(proj, o_hg, st, dycat, lb_logits, hg_norm, W,
                                           rider=_chips_rider([pair[2]], rels=(3,), landing=[land2]))
    d_rb = _bias_bwd(dbias, onehot)[:, 0, :nrel]
    dproj = [d_hg4, dq_att, dk_att, dv_att]
    tw = _div(W, 1024, LANES)
    dW_in, land1 = _mm(a1, dproj, dims="tn", tm=tn_w, tn=_div(W, 512, LANES), tk=tk_s, out_dtype=bf16, name="mm_d_in",
                       rider=_chips_rider([pair[1]]))
    pair_up([0], [dW_in], _run_rider(siblings([0], [dW_in]), name="grads_to_sibling_0"))
    halves = [reduced_half(i, land) for i, land in ((1, land1), (2, land2), (3, land3), (4, land4), (5, land5))]
    da1, land0, *g_rest = _mm(dproj, W_in, dims="nt", tm=tm, tn=D, tk=tw, out_dtype=f32, name="mm_da1",
                              rider=_both(_chips_rider([pair[0]]), _join_rider(halves)))
    grad_x, _, d_nmix = _rms_bwd(xv, norm_mix, da1, dh1, name="rms_mix_bwd")
    g_big = list(_run_rider(_join_rider([reduced_half(0, land0)]), name="grads_join_w_in")) + list(g_rest)

    d_cw = jnp.concatenate([dcw_g, dcw_v], axis=1)
    d_cb = jnp.concatenate([dcb_g, dcb_v], axis=1)
    small_parts = [loss_part[:, :1], d_nmix, d_lbl, d_hgn, d_rb, d_nffn, d_cw, d_cb, d_nple, d_fn]
    small_shapes = [(1, 1), (1, D), lb_logits.shape, hg_norm.shape, (H, nrel), (1, D), (ncw, 2 * F), (1, 2 * F), (1, D), (1, D)]
    red = _unpack(_allreduce_small(_pack(small_parts)), small_shapes)
    loss = red[0].reshape(())
    g_nmix, g_lbl, g_hgn, g_rb, g_nffn, g_cw_all, g_cb, g_nple, g_fn = red[1:]
    csh = conv_w.shape[2]
    g_cw = lax.dynamic_slice(g_cw_all, (0, chip * csh), (ncw, csh))
    small_g = [g_nmix, g_lbl, g_hgn, g_rb[None], g_nffn, g_cw[None], g_cb, g_nple, g_fn.reshape(D)]
    small_w = [norm_mix, lb_logits, hg_norm, rel_bias, norm_ffn, conv_w, conv_b, norm_ple, final_norm]
    small_m = [m_norm_mix, m_lb_logits, m_hg_norm, m_rel_bias, m_norm_ffn, m_conv_w, m_conv_b, m_norm_ple, m_final_norm]
    small_v = [v_norm_mix, v_lb_logits, v_hg_norm, v_rel_bias, v_norm_ffn, v_conv_w, v_conv_b, v_norm_ple, v_final_norm]
    shapes_s = [w.shape for w in small_w]
    _, sd, sm, sv = _adamw(_pack(small_w), _pack(small_g), _pack(small_m), _pack(small_v), name="adamw_small")
    small_g = [g.reshape(s) for g, s in zip(small_g, shapes_s)]
    small_d, small_nm, small_nv = _unpack(sd, shapes_s), _unpack(sm, shapes_s), _unpack(sv, shapes_s)

    big_m = [m_w_in[0], m_w_out[0], m_w_up[0], m_w_down[0], m_w_ple_gate[0], m_w_ple_proj[0]]
    big_v = [v_w_in[0], v_w_out[0], v_w_up[0], v_w_down[0], v_w_ple_gate[0], v_w_ple_proj[0]]
    big_g, big_d, big_nm, big_nv = [], [], [], []
    for i in range(6):
        g_, d_, m_, v_ = _adamw(big[i], g_big[i], big_m[i], big_v[i], name=f"adamw_{i}")
        big_g.append(g_[None])
        big_d.append(d_[None])
        big_nm.append(m_[None])
        big_nv.append(v_[None])
    g_big = big_g

    def order(sm_list, bg_list):
        s, b = sm_list, bg_list
        return [s[0], b[0], s[1], s[2], s[3], b[1], s[4], b[2], s[5], s[6], b[3], s[7], b[4], b[5], s[8]]

    return (loss, grad_x[None], *order(small_g, g_big), *order(small_d, big_d), *order(small_nm, big_nm), *order(small_nv, big_nv))
```

```python
import functools
from typing import Callable, NamedTuple

import jax
import jax.numpy as jnp
import numpy as np
from jax import lax
from jax.experimental import pallas as pl
from jax.experimental.pallas import tpu as pltpu

f32 = jnp.float32
bf16 = jnp.bfloat16

CHUNK = 64
HEAD_DIM = 128
LEFT_CHUNKS = 8
REL_CLIP = 128
EPS = 1e-6
HG_SUB = 16
HG_HEADS_PER_STEP = 8
ATT_Q_ROWS = 256
ATT_HEADS_PER_STEP = 2
ADAM_LR, ADAM_B1, ADAM_B2, ADAM_EPS, ADAM_WD, ADAM_STEP = 0.001, 0.9, 0.999, 1e-08, 0.01, 10
LANES = 128
SUBLANES = 8
N_CHIPS = 4
N_DEV = 8
RDMA_PIECES = 4
MESH = pl.DeviceIdType.MESH
NEG = float(np.finfo(np.float32).min)

NN = (((1,), (0,)), ((), ()))
NT = (((1,), (1,)), ((), ()))
TN = (((0,), (0,)), ((), ()))
HIGHEST = lax.Precision.HIGHEST


def _dot(a, b, dims, precision=None):
    return lax.dot_general(a, b, dims, preferred_element_type=f32, precision=precision)


def _sigmoid(v):
    return 1.0 / (1.0 + jnp.exp(-v))


def _div(n, pref, mult):
    best = None
    d = mult
    while d <= min(n, pref):
        if n % d == 0:
            best = d
        d += mult
    return best if best is not None else n


def _hbm():
    return pl.BlockSpec(memory_space=pltpu.HBM)


def _vmem():
    return pl.BlockSpec(memory_space=pltpu.VMEM)


def _mm(a, b, *, dims, tm, tn, tk, out_dtype, name, res=None, a_idx=None, b_idx=None, o_idx=None, out_shape=None, rider=None,
        cols_outer=False):
    a_list = list(a) if isinstance(a, (list, tuple)) else [a]
    b_list = list(b) if isinstance(b, (list, tuple)) else [b]
    na, nb = len(a_list), len(b_list)
    assert na == 1 or nb == 1
    a_cols, b_cols = sum(p.shape[1] for p in a_list), sum(p.shape[1] for p in b_list)
    if dims == "nn":
        assert nb == 1
        M, K, N = a_list[0].shape[0], a_cols, b_cols
        a_blk, b_blk, dn = (tm, tk), (tk, tn), NN
        a_def, b_def = (lambda i, j, k: (i, k)), (lambda i, j, k: (k, j))
    elif dims == "nt":
        assert nb == 1
        M, K, N = a_list[0].shape[0], a_cols, b_list[0].shape[0]
        a_blk, b_blk, dn = (tm, tk), (tn, tk), NT
        a_def, b_def = (lambda i, j, k: (i, k)), (lambda i, j, k: (j, k))
    else:
        K, M, N = a_list[0].shape[0], a_cols, b_cols
        a_blk, b_blk, dn = (tk, tm), (tk, tn), TN
        a_def, b_def = (lambda i, j, k: (k, i)), (lambda i, j, k: (k, j))
    assert M % tm == 0 and N % tn == 0 and K % tk == 0, (name, M, N, K, tm, tn, tk)
    nk = K // tk
    has_res = res is not None
    axis, tile = ((0, tm) if dims == "tn" else (2, tk)) if na > 1 else (1, tn)
    counts = [p.shape[1] // tile for p in (a_list if na > 1 else b_list)]
    starts = [sum(counts[:p]) for p in range(len(counts))]

    def body(*refs):
        a_refs, b_refs, rest = refs[:na], refs[na:na + nb], refs[na + nb:]
        res_ref = rest[0] if has_res else None
        o_ref = rest[1] if has_res else rest[0]
        k = pl.program_id(2)
        i, j = (pl.program_id(1), pl.program_id(0)) if cols_outer else (pl.program_id(0), pl.program_id(1))

        def finish(acc):
            if has_res:
                acc = acc + res_ref[...]
            o_ref[...] = acc.astype(out_dtype)

        def consume(part):
            if nk == 1:
                finish(part)
                return
            acc_ref = refs[-1]

            @pl.when(k == 0)
            def _():
                acc_ref[...] = part

            @pl.when(k > 0)
            def _():
                acc_ref[...] += part

            @pl.when(k == nk - 1)
            def _():
                finish(acc_ref[...])

        if len(counts) == 1:
            consume(_dot(a_refs[0][...].astype(bf16), b_refs[0][...].astype(bf16), dn))
        else:
            at = (i, j, k)[axis]
            for p, (s, n) in enumerate(zip(starts, counts)):
                a_ref, b_ref = a_refs[p if na > 1 else 0], b_refs[p if nb > 1 else 0]
                pl.when((at >= s) & (at < s + n))(
                    functools.partial(lambda ar, br: consume(_dot(ar[...].astype(bf16), br[...].astype(bf16), dn)), a_ref, b_ref))

    def ordered(custom, default):
        f = custom or default
        return (lambda j, i, k: f(i, j, k)) if cols_outer else f

    def piece_map(default, p):
        s, n = starts[p], counts[p]

        def held(*ids):
            ids = list(ids)
            ids[axis] = jnp.clip(ids[axis] - s, 0, n - 1)
            return default(*ids)

        return held

    o_map = ordered(o_idx, lambda i, j, k: (i, j))
    a_specs = [pl.BlockSpec(a_blk, ordered(a_idx, piece_map(a_def, p) if na > 1 else a_def)) for p in range(na)]
    b_specs = [pl.BlockSpec(b_blk, ordered(b_idx, piece_map(b_def, p) if nb > 1 else b_def)) for p in range(nb)]
    in_specs = a_specs + b_specs
    args = a_list + b_list
    if has_res:
        in_specs.append(pl.BlockSpec((tm, tn), o_map))
        args.append(res)
    outs = _grid_call(
        body,
        rider,
        name=name,
        grid=(N // tn, M // tm, nk) if cols_outer else (M // tm, N // tn, nk),
        in_specs=in_specs,
        out_specs=[pl.BlockSpec((tm, tn), o_map)],
        out_shape=[jax.ShapeDtypeStruct(out_shape or (M, N), out_dtype)],
        scratch_shapes=[pltpu.VMEM((tm, tn), f32)] if nk > 1 else [],
        args=args,
    )
    return outs[0] if rider is None else outs


def _rms_fwd(xv, g, *, name):
    S, D = xv.shape
    ts = _div(S, 512, SUBLANES)

    def body(x_ref, g_ref, o_ref):
        v = x_ref[...]
        r = lax.rsqrt(jnp.mean(v * v, axis=-1, keepdims=True) + EPS)
        o_ref[...] = (v * r * g_ref[...]).astype(bf16)

    return pl.pallas_call(
        body,
        name=name,
        grid=(S // ts,),
        in_specs=[pl.BlockSpec((ts, D), lambda i: (i, 0)), pl.BlockSpec((1, D), lambda i: (0, 0))],
        out_specs=pl.BlockSpec((ts, D), lambda i: (i, 0)),
        out_shape=jax.ShapeDtypeStruct((S, D), bf16),
        compiler_params=pltpu.CompilerParams(dimension_semantics=("parallel",)),
    )(xv, g)


def _rms_bwd(xv, g, dy, dres, *, name, rider=None):
    S, D = xv.shape
    ts = _div(S, 256, SUBLANES)

    def body(x_ref, g_ref, dy_ref, dres_ref, dx_ref, dxb_ref, dg_ref):
        i = pl.program_id(0)
        v = x_ref[...]
        r = lax.rsqrt(jnp.mean(v * v, axis=-1, keepdims=True) + EPS)
        vn = v * r
        d = dy_ref[...]
        part = jnp.sum(d * vn, axis=0, keepdims=True)

        @pl.when(i == 0)
        def _():
            dg_ref[...] = part

        @pl.when(i > 0)
        def _():
            dg_ref[...] += part

        t = d * g_ref[...]
        dx = dres_ref[...] + r * (t - vn * jnp.mean(t * vn, axis=-1, keepdims=True))
        dx_ref[...] = dx
        dxb_ref[...] = dx.astype(bf16)

    row = pl.BlockSpec((ts, D), lambda i: (i, 0))
    vec = pl.BlockSpec((1, D), lambda i: (0, 0))
    return _grid_call(
        body,
        rider,
        name=name,
        grid=(S // ts,),
        in_specs=[row, vec, row, row],
        out_specs=[row, row, vec],
        out_shape=[jax.ShapeDtypeStruct((S, D), f32), jax.ShapeDtypeStruct((S, D), bf16), jax.ShapeDtypeStruct((1, D), f32)],
        scratch_shapes=[],
        args=(xv, g, dy, dres),
    )


def _tail(h2, zg, pp, tgt, fn):
    S, D = h2.shape
    ts = _div(S, 256, SUBLANES)

    def body(h_ref, z_ref, p_ref, t_ref, fn_ref, dh_ref, dz_ref, dp_ref, loss_ref, dfn_ref):
        i = pl.program_id(0)
        sg = _sigmoid(z_ref[...])
        ppv = p_ref[...]
        h3 = h_ref[...] + sg * ppv
        r = lax.rsqrt(jnp.mean(h3 * h3, axis=-1, keepdims=True) + EPS)
        hn = h3 * r
        fnv = fn_ref[...]
        e = hn * fnv - t_ref[...]
        lpart = 0.5 * jnp.sum(jnp.mean(e * e, axis=-1, keepdims=True), axis=0, keepdims=True)
        dy = e * (1.0 / D)
        dpart = jnp.sum(dy * hn, axis=0, keepdims=True)

        @pl.when(i == 0)
        def _():
            loss_ref[...] = jnp.broadcast_to(lpart, loss_ref.shape)
            dfn_ref[...] = dpart

        @pl.when(i > 0)
        def _():
            loss_ref[...] += jnp.broadcast_to(lpart, loss_ref.shape)
            dfn_ref[...] += dpart

        t = dy * fnv
        dh3 = r * (t - hn * jnp.mean(t * hn, axis=-1, keepdims=True))
        dh_ref[...] = dh3
        dz_ref[...] = (dh3 * ppv * sg * (1.0 - sg)).astype(bf16)
        dp_ref[...] = (dh3 * sg).astype(bf16)

    row = pl.BlockSpec((ts, D), lambda i: (i, 0))
    vec = pl.BlockSpec((1, D), lambda i: (0, 0))
    one = pl.BlockSpec((1, LANES), lambda i: (0, 0))
    return pl.pallas_call(
        body,
        name="ple_final_loss",
        grid=(S // ts,),
        in_specs=[row, row, row, row, vec],
        out_specs=[row, row, row, one, vec],
        out_shape=[
            jax.ShapeDtypeStruct((S, D), f32),
            jax.ShapeDtypeStruct((S, D), bf16),
            jax.ShapeDtypeStruct((S, D), bf16),
            jax.ShapeDtypeStruct((1, LANES), f32),
            jax.ShapeDtypeStruct((1, D), f32),
        ],
        compiler_params=pltpu.CompilerParams(dimension_semantics=("arbitrary",)),
    )(h2, zg, pp, tgt, fn)


def _shift_down(v, halo, k):
    r = pltpu.roll(v, k, 0)
    hr = pltpu.roll(halo, k, 0)
    row = lax.broadcasted_iota(jnp.int32, hr.shape, 0)
    top = jnp.where(row < k, hr, r[0:SUBLANES])
    return top if v.shape[0] == SUBLANES else jnp.concatenate([top, r[SUBLANES:]], axis=0)


def _shift_up(v, halo, k):
    n = v.shape[0]
    r = pltpu.roll(v, n - k, 0)
    hr = pltpu.roll(halo, SUBLANES - k, 0)
    row = lax.broadcasted_iota(jnp.int32, hr.shape, 0)
    bot = jnp.where(row >= SUBLANES - k, hr, r[n - SUBLANES:])
    return jnp.concatenate([r[: n - SUBLANES], bot], axis=0)


def _conv_specs(S, F, tc, ts):
    nj = F // tc
    rows8 = ts // SUBLANES
    main = pl.BlockSpec((ts, 2 * tc), lambda j, i: (i, j))
    prev = pl.BlockSpec((SUBLANES, 2 * tc), lambda j, i: (jnp.maximum(i * rows8 - 1, 0), j))
    nxt = pl.BlockSpec((SUBLANES, 2 * tc), lambda j, i: (jnp.minimum((i + 1) * rows8, S // SUBLANES - 1), j))
    wg = pl.BlockSpec((3, tc), lambda j, i: (0, j))
    wv = pl.BlockSpec((3, tc), lambda j, i: (0, nj + j))
    bg = pl.BlockSpec((1, tc), lambda j, i: (0, j))
    bv = pl.BlockSpec((1, tc), lambda j, i: (0, nj + j))
    half = pl.BlockSpec((ts, tc), lambda j, i: (i, j))
    return nj, main, prev, nxt, wg, wv, bg, bv, half


def _conv_pre(u_ref, h_ref, wg_ref, wv_ref, bg_ref, bv_ref, tc):
    i = pl.program_id(1)
    u = u_ref[...]
    halo = jnp.where(i == 0, 0.0, h_ref[...])
    u1 = _shift_down(u, halo, 1)
    u2 = _shift_down(u, halo, 2)
    w = jnp.concatenate([wg_ref[...], wv_ref[...]], axis=1)
    b = jnp.concatenate([bg_ref[...], bv_ref[...]], axis=1)
    uc = b + w[0:1] * u2 + w[1:2] * u1 + w[2:3] * u
    return u, u1, u2, uc[:, :tc], uc[:, tc:]


def _convact_fwd(u, cw, cb, tc, rider=None):
    S, F2 = u.shape
    F = F2 // 2
    ts = _div(S, 512, SUBLANES)
    nj, main, prev, nxt, wg, wv, bg, bv, half = _conv_specs(S, F, tc, ts)

    def body(u_ref, h_ref, wg_ref, wv_ref, bg_ref, bv_ref, m_ref):
        _, _, _, g, v = _conv_pre(u_ref, h_ref, wg_ref, wv_ref, bg_ref, bv_ref, tc)
        m_ref[...] = (g * _sigmoid(g) * v).astype(bf16)

    outs = _grid_call(
        body,
        rider,
        name="convact_fwd",
        grid=(nj, S // ts),
        in_specs=[main, prev, wg, wv, bg, bv],
        out_specs=[half],
        out_shape=[jax.ShapeDtypeStruct((S, F), bf16)],
        scratch_shapes=[],
        args=(u, u, cw, cw, cb, cb),
    )
    return outs[0] if rider is None else outs


def _convact_bwd(u, dm, cw, cb, tc, rider=None):
    S, F2 = u.shape
    F = F2 // 2
    ts = _div(S, 512, SUBLANES)
    nj, main, prev, nxt, wg, wv, bg, bv, half = _conv_specs(S, F, tc, ts)
    ni = S // ts
    half_nxt = pl.BlockSpec((SUBLANES, tc), lambda j, i: (jnp.minimum((i + 1) * (ts // SUBLANES), S // SUBLANES - 1), j))

    def act_bwd(g, v, d):
        sg = _sigmoid(g)
        return jnp.concatenate([d * v * sg * (1.0 + g * (1.0 - sg)), d * g * sg], axis=1)

    def body(u_ref, h_ref, n_ref, dm_ref, dmn_ref, wg_ref, wv_ref, bg_ref, bv_ref, du_ref, dwg_ref, dwv_ref, dbg_ref, dbv_ref):
        i = pl.program_id(1)
        u0, u1, u2, g, v = _conv_pre(u_ref, h_ref, wg_ref, wv_ref, bg_ref, bv_ref, tc)
        duc = act_bwd(g, v, dm_ref[...])
        w = jnp.concatenate([wg_ref[...], wv_ref[...]], axis=1)
        b = jnp.concatenate([bg_ref[...], bv_ref[...]], axis=1)
        un = n_ref[...]
        tail = u0[ts - SUBLANES:]
        ucn = b + w[0:1] * _shift_down(un, tail, 2) + w[1:2] * _shift_down(un, tail, 1) + w[2:3] * un
        ducn = jnp.where(i == ni - 1, 0.0, act_bwd(ucn[:, :tc], ucn[:, tc:], dmn_ref[...]))
        du_ref[...] = (w[2:3] * duc + w[1:2] * _shift_up(duc, ducn, 1) + w[0:1] * _shift_up(duc, ducn, 2)).astype(bf16)
        db = jnp.sum(duc, axis=0, keepdims=True)
        dw = jnp.concatenate(
            [jnp.sum(duc * u2, axis=0, keepdims=True), jnp.sum(duc * u1, axis=0, keepdims=True), jnp.sum(duc * u0, axis=0, keepdims=True)],
            axis=0,
        )

        @pl.when(i == 0)
        def _():
            dwg_ref[...] = dw[:, :tc]
            dwv_ref[...] = dw[:, tc:]
            dbg_ref[...] = db[:, :tc]
            dbv_ref[...] = db[:, tc:]

        @pl.when(i > 0)
        def _():
            dwg_ref[...] += dw[:, :tc]
            dwv_ref[...] += dw[:, tc:]
            dbg_ref[...] += db[:, :tc]
            dbv_ref[...] += db[:, tc:]

    w_out = pl.BlockSpec((3, tc), lambda j, i: (0, j))
    b_out = pl.BlockSpec((1, tc), lambda j, i: (0, j))
    return _grid_call(
        body,
        rider,
        name="convact_bwd",
        grid=(nj, ni),
        in_specs=[main, prev, nxt, half, half_nxt, wg, wv, bg, bv],
        out_specs=[main, w_out, w_out, b_out, b_out],
        out_shape=[
            jax.ShapeDtypeStruct((S, F2), bf16),
            jax.ShapeDtypeStruct((3, F), f32),
            jax.ShapeDtypeStruct((3, F), f32),
            jax.ShapeDtypeStruct((1, F), f32),
            jax.ShapeDtypeStruct((1, F), f32),
        ],
        scratch_shapes=[],
        args=(u, u, u, dm, dm, cw, cw, cb, cb),
    )


def _hg_gates(qp, fp, lbl):
    lb = _sigmoid(lbl[0:1] - lbl[1:2])
    sig = _sigmoid(fp)
    sigm = _sigmoid(-fp)
    f = lb + (1.0 - lb) * sig
    k = (1.0 - lb) * sigm
    sq = _sigmoid(qp)
    qf = qp * sq
    row = lax.broadcasted_iota(jnp.int32, (CHUNK, CHUNK), 0)
    col = lax.broadcasted_iota(jnp.int32, (CHUNK, CHUNK), 1)
    b = _dot((row >= col).astype(f32), jnp.log(f), NN, HIGHEST)
    return lb, sig, sigm, f, k, sq, qf, b


def _hg_block(qf, k, b, I):
    r0, n = I * HG_SUB, (I + 1) * HG_SUB
    base = b[r0 - 1:r0] if I > 0 else jnp.zeros_like(b[0:1])
    eq = jnp.exp(b[r0:n] - base)
    ek = jnp.exp(base - b[0:n])
    qt16 = (qf[r0:n] * eq).astype(bf16)
    kt16 = (k[0:n] * ek).astype(bf16)
    row = lax.broadcasted_iota(jnp.int32, (HG_SUB, n), 0) + r0
    col = lax.broadcasted_iota(jnp.int32, (HG_SUB, n), 1)
    return r0, n, eq, ek, qt16, kt16, col <= row


def _per_head(v, hp, fn):
    return jnp.concatenate([fn(v[:, p * HEAD_DIM:(p + 1) * HEAD_DIM]) for p in range(hp)], axis=1)


def _head_mean(v, hp):
    return _per_head(v, hp, lambda t: jnp.broadcast_to(jnp.mean(t, axis=-1, keepdims=True), t.shape))


def _pad_rows(v, rows):
    return v if v.shape[0] == rows else jnp.concatenate([v, jnp.zeros((rows - v.shape[0], v.shape[1]), v.dtype)], axis=0)


def _hgrn_specs(S, W, hp, reverse):
    nc = S // CHUNK
    ngrp = W // (hp * HEAD_DIM)
    cw = hp * HEAD_DIM
    cidx = (lambda c: nc - 1 - c) if reverse else (lambda c: c)

    def proj(off):
        return pl.BlockSpec((CHUNK, cw), lambda h, c: (cidx(c), off * ngrp + h))

    act = pl.BlockSpec((CHUNK, cw), lambda h, c: (cidx(c), h))
    lbl = pl.BlockSpec((2, cw), lambda h, c: (0, h))
    gn = pl.BlockSpec((1, HEAD_DIM), lambda h, c: (0, 0))
    st = pl.BlockSpec((hp, 1, HEAD_DIM, HEAD_DIM), lambda h, c: (h, cidx(c), 0, 0))
    return nc, ngrp, proj, act, lbl, gn, st


def _grid_call(body, rider, *, name, grid, in_specs, out_specs, out_shape, scratch_shapes, args):
    params = pltpu.CompilerParams(dimension_semantics=("arbitrary",) * len(grid))
    if rider is None:
        return pl.pallas_call(body, name=name, grid=grid, in_specs=in_specs, out_specs=out_specs, out_shape=out_shape,
                              scratch_shapes=scratch_shapes, compiler_params=params)(*args)
    n_in, n_out, n_scr = len(in_specs), len(out_specs), len(scratch_shapes)
    ri, ro = len(rider.ins), len(rider.out_shapes)

    def wrapped(*refs):
        a, b = n_in, n_in + ri
        c, d = b + n_out, b + n_out + ro
        e = d + n_scr
        ids = [pl.program_id(t) for t in range(len(grid))]
        is_first = functools.reduce(jnp.logical_and, [i == 0 for i in ids])
        is_last = functools.reduce(jnp.logical_and, [i == g - 1 for i, g in zip(ids, grid)])

        @pl.when(is_first)
        def _():
            rider.first(refs[a:b], refs[c:d], refs[e:])

        body(*refs[:a], *refs[b:c], *refs[d:e])

        @pl.when(is_last)
        def _():
            rider.last(refs[a:b], refs[c:d], refs[e:])

    return pl.pallas_call(
        wrapped,
        name=name,
        grid=grid,
        in_specs=list(in_specs) + [_hbm()] * ri,
        out_specs=list(out_specs) + [_hbm()] * ro,
        out_shape=list(out_shape) + list(rider.out_shapes),
        input_output_aliases={n_in + i: n_out + o for i, o in rider.aliases.items()},
        scratch_shapes=list(scratch_shapes) + list(rider.sems),
        compiler_params=params,
    )(*args, *rider.ins)


def _hgrn_fwd(proj, lb_logits, hg_norm, W, rider=None):
    S = proj.shape[0]
    hp = min(HG_HEADS_PER_STEP, W // HEAD_DIM)
    nc, ngrp, pspec, act, lbl_spec, gn_spec, st_spec = _hgrn_specs(S, W, hp, False)
    nb = CHUNK // HG_SUB

    def body(q_ref, f_ref, i_ref, g_ref, lbl_ref, gn_ref, y_ref, o_ref, st_ref, state):
        c = pl.program_id(1)

        @pl.when(c == 0)
        def _():
            state[...] = jnp.zeros_like(state)

        _, _, _, _, k, _, qf, b = _hg_gates(q_ref[...], f_ref[...], lbl_ref[...])
        v16 = i_ref[...].astype(bf16)
        a16 = (qf * jnp.exp(b)).astype(bf16)
        bl = b[CHUNK - 1:CHUNK]
        kd16 = (k * jnp.exp(bl - b)).astype(bf16)
        ebl = jnp.exp(bl)
        blocks = [_hg_block(qf, k, b, I) for I in range(nb)]
        heads = [slice(p * HEAD_DIM, (p + 1) * HEAD_DIM) for p in range(hp)]
        st_prev = state[...]
        st_ref[:, 0] = st_prev
        inter = [_dot(a16[:, sl], st_prev[p].astype(bf16), NT) for p, sl in enumerate(heads)]
        scores = [[_dot(qt16[:, sl], kt16[:, sl], NT) for sl in heads] for _, _, _, _, qt16, kt16, _ in blocks]
        intra = [[_dot(jnp.where(blk[6], sc, 0.0).astype(bf16), v16[0:blk[1], sl], NN) for sc, sl in zip(scs, heads)]
                 for blk, scs in zip(blocks, scores)]
        st_new = [st_prev[p] * ebl[:, sl] + _dot(v16[:, sl], kd16[:, sl], TN) for p, sl in enumerate(heads)]
        state[...] = jnp.stack(st_new)
        o = jnp.concatenate([inter[p] + jnp.concatenate([rows[p] for rows in intra], axis=0) for p in range(hp)], axis=1)
        o_ref[...] = o
        r = lax.rsqrt(_head_mean(o * o, hp) + EPS)
        gp = g_ref[...]
        y_ref[...] = (o * r * jnp.tile(gn_ref[...], (1, hp)) * (gp * _sigmoid(gp))).astype(bf16)

    H = W // HEAD_DIM
    return _grid_call(
        body,
        rider,
        name="hgrn_fwd",
        grid=(ngrp, nc),
        in_specs=[pspec(0), pspec(1), pspec(2), pspec(3), lbl_spec, gn_spec],
        out_specs=[act, act, st_spec],
        out_shape=[
            jax.ShapeDtypeStruct((S, W), bf16),
            jax.ShapeDtypeStruct((S, W), f32),
            jax.ShapeDtypeStruct((H, nc, HEAD_DIM, HEAD_DIM), f32),
        ],
        scratch_shapes=[pltpu.VMEM((hp, HEAD_DIM, HEAD_DIM), f32)],
        args=(proj, proj, proj, proj, lb_logits, hg_norm),
    )


def _hgrn_bwd(proj, o_hg, st, dycat, lb_logits, hg_norm, W, rider=None):
    S = proj.shape[0]
    hp = min(HG_HEADS_PER_STEP, W // HEAD_DIM)
    assert hp * HEAD_DIM == W, "the four gradients share one output block: all heads in one grid step"
    nc, ngrp, pspec, act, lbl_spec, gn_spec, st_spec = _hgrn_specs(S, W, hp, True)
    nb = CHUNK // HG_SUB

    def body(q_ref, f_ref, i_ref, g_ref, o_ref, st_ref, dy_ref, lbl_ref, gn_ref,
             d4_ref, dlbl_ref, dgn_ref, dstate):
        cw_ = hp * HEAD_DIM
        dq_ref, df_ref, di_ref, dg_ref = (d4_ref.at[:, pl.ds(t * cw_, cw_)] for t in range(4))
        h = pl.program_id(0)
        c = pl.program_id(1)

        @pl.when(c == 0)
        def _():
            dstate[...] = jnp.zeros_like(dstate)
            dlbl_ref[...] = jnp.zeros_like(dlbl_ref)

        @pl.when((c == 0) & (h == 0))
        def _():
            dgn_ref[...] = jnp.zeros_like(dgn_ref)

        row = lax.broadcasted_iota(jnp.int32, (CHUNK, CHUNK), 0)
        col = lax.broadcasted_iota(jnp.int32, (CHUNK, CHUNK), 1)
        upper = (col >= row).astype(f32)
        last_row = lax.broadcasted_iota(jnp.int32, (CHUNK, hp * HEAD_DIM), 0) == CHUNK - 1

        qp, fp = q_ref[...], f_ref[...]
        lb, sig, sigm, f, k, sq, qf, b = _hg_gates(qp, fp, lbl_ref[...])
        v16 = i_ref[...].astype(bf16)
        gnv = jnp.tile(gn_ref[...], (1, hp))
        gp = g_ref[...]
        sgg = _sigmoid(gp)
        gate = gp * sgg
        o = o_ref[...]
        r = lax.rsqrt(_head_mean(o * o, hp) + EPS)
        on = o * r
        dy = dy_ref[...]
        dg_ref[...] = (dy * on * gnv * (sgg * (1.0 + gp * (1.0 - sgg)))).astype(bf16)
        dgn_wide = jnp.sum(dy * on * gate, axis=0, keepdims=True)
        dgn_ref[...] += functools.reduce(jnp.add, [dgn_wide[:, p * HEAD_DIM:(p + 1) * HEAD_DIM] for p in range(hp)])
        don = dy * gnv * gate
        do16 = (r * (don - on * _head_mean(don * on, hp))).astype(bf16)
        eb = jnp.exp(b)
        A = qf * eb
        a16 = A.astype(bf16)
        bl = b[CHUNK - 1:CHUNK]
        ebl = jnp.exp(bl)
        ekd = jnp.exp(bl - b)
        kd = k * ekd
        kd16 = kd.astype(bf16)
        blocks = [_hg_block(qf, k, b, I) for I in range(nb)]

        heads = [slice(p * HEAD_DIM, (p + 1) * HEAD_DIM) for p in range(hp)]
        st_prev = st_ref[:, 0]
        dst_new = dstate[...]
        st16, dst16 = st_prev.astype(bf16), dst_new.astype(bf16)
        dA_h = [_dot(do16[:, sl], st16[p], NN) for p, sl in enumerate(heads)]
        dkd_h = [_dot(v16[:, sl], dst16[p], NN) for p, sl in enumerate(heads)]
        dv_h = [_dot(kd16[:, sl], dst16[p], NT) for p, sl in enumerate(heads)]
        dstate[...] = jnp.stack([dst_new[p] * ebl[:, sl] + _dot(do16[:, sl], a16[:, sl], TN) for p, sl in enumerate(heads)])
        dbl_h = [jnp.sum(dst_new[p] * st_prev[p], axis=0, keepdims=True) for p in range(hp)]
        sc_h = [[jnp.where(mask, _dot(qt16[:, sl], kt16[:, sl], NT), 0.0).astype(bf16) for sl in heads]
                for _, _, _, _, qt16, kt16, mask in blocks]
        dsc_h = [[jnp.where(mask, _dot(do16[r0:n, sl], v16[0:n, sl], NT), 0.0).astype(bf16) for sl in heads]
                 for r0, n, _, _, _, _, mask in blocks]
        dvi_h = [[_dot(sc, do16[blk[0]:blk[1], sl], TN) for sc, sl in zip(scs, heads)] for blk, scs in zip(blocks, sc_h)]
        dqt_h = [[_dot(dsc, blk[5][:, sl], NN) for dsc, sl in zip(dscs, heads)] for blk, dscs in zip(blocks, dsc_h)]
        dkt_h = [[_dot(dsc, blk[4][:, sl], TN) for dsc, sl in zip(dscs, heads)] for blk, dscs in zip(blocks, dsc_h)]
        dv_h = [functools.reduce(jnp.add, [dv_h[p]] + [_pad_rows(rows[p], CHUNK) for rows in dvi_h]) for p in range(hp)]

        dA, dkd = jnp.concatenate(dA_h, axis=1), jnp.concatenate(dkd_h, axis=1)
        dbl = jnp.concatenate(dbl_h, axis=1) * ebl + jnp.sum(dkd * kd, axis=0, keepdims=True)
        db = dA * A - dkd * kd + jnp.where(last_row, dbl, 0.0)
        dk = dkd * ekd
        dq_rows, db_rows = [], []
        for I, (r0, n, eq, ek, qt16, kt16, _) in enumerate(blocks):
            dqt, dkt = jnp.concatenate(dqt_h[I], axis=1), jnp.concatenate(dkt_h[I], axis=1)
            dq_rows.append(dqt * eq)
            dk = dk + _pad_rows(dkt * ek, CHUNK)
            db_rows.append(dqt * qt16.astype(f32))
            db = db - _pad_rows(dkt * kt16.astype(f32), CHUNK)
        dqf = dA * eb + jnp.concatenate(dq_rows, axis=0)
        db = db + jnp.concatenate(db_rows, axis=0)
        dlogf = _dot(upper, db, NN, HIGHEST)
        dfg = dlogf / f
        df_ref[...] = ((1.0 - lb) * sig * sigm * (dfg - dk)).astype(bf16)
        dlb = jnp.sum(sigm * (dfg - dk), axis=0, keepdims=True)
        dl0 = dlb * lb * (1.0 - lb)
        dlbl_ref[...] += jnp.concatenate([dl0, -dl0], axis=0)
        dq_ref[...] = (dqf * (sq * (1.0 + qp * (1.0 - sq)))).astype(bf16)
        di_ref[...] = jnp.concatenate(dv_h, axis=1).astype(bf16)

    cw = hp * HEAD_DIM
    dy_spec = pl.BlockSpec((CHUNK, cw), lambda h, c: (nc - 1 - c, h))
    return _grid_call(
        body,
        rider,
        name="hgrn_bwd",
        grid=(ngrp, nc),
        in_specs=[pspec(0), pspec(1), pspec(2), pspec(3), act, st_spec, dy_spec, lbl_spec, gn_spec],
        out_specs=[pl.BlockSpec((CHUNK, 4 * W), lambda h, c: (nc - 1 - c, 0)), lbl_spec, gn_spec],
        out_shape=[jax.ShapeDtypeStruct((S, 4 * W), bf16), jax.ShapeDtypeStruct((2, W), f32), jax.ShapeDtypeStruct((1, HEAD_DIM), f32)],
        scratch_shapes=[pltpu.VMEM((hp, HEAD_DIM, HEAD_DIM), f32)],
        args=(proj, proj, proj, proj, o_hg, st, dycat, lb_logits, hg_norm),
    )


def _att_dims(S):
    qt = ATT_Q_ROWS if S % ATT_Q_ROWS == 0 else CHUNK
    pad = LEFT_CHUNKS * CHUNK
    kb = pad + qt
    return qt, pad, kb, kb + qt


def _kv_padded(proj, W, pad):
    S = proj.shape[0]
    ts = _div(pad, 512, 2 * SUBLANES)
    assert S % ts == 0 and pad % ts == 0
    npad = pad // ts

    def body(k_ref, v_ref, ko_ref, vo_ref):
        i = pl.program_id(0)

        @pl.when(i < npad)
        def _():
            ko_ref[...] = jnp.zeros_like(ko_ref)
            vo_ref[...] = jnp.zeros_like(vo_ref)

        @pl.when(i >= npad)
        def _():
            ko_ref[...] = k_ref[...].astype(bf16)
            vo_ref[...] = v_ref[...].astype(bf16)

    out = pl.BlockSpec((ts, W), lambda i: (i, 0))
    return pl.pallas_call(
        body,
        name="kv_padded",
        grid=((S + pad) // ts,),
        in_specs=[pl.BlockSpec((ts, W), lambda i: (jnp.maximum(i - npad, 0), 5)), pl.BlockSpec((ts, W), lambda i: (jnp.maximum(i - npad, 0), 6))],
        out_specs=[out, out],
        out_shape=[jax.ShapeDtypeStruct((S + pad, W), bf16)] * 2,
        compiler_params=pltpu.CompilerParams(dimension_semantics=("parallel",)),
    )(proj, proj)


def _att_probs(qk, bias, start, qt, pad, kb):
    s = qk * (HEAD_DIM ** -0.5) + bias
    row = lax.broadcasted_iota(jnp.int32, (qt, kb), 0)
    col = lax.broadcasted_iota(jnp.int32, (qt, kb), 1)
    lo = jnp.bitwise_and(row, -CHUNK)
    ok = (col >= lo) & (col < lo + pad + CHUNK) & (col + start >= pad)
    s = jnp.where(ok, s, NEG)
    e = jnp.exp(s - jnp.max(s, axis=-1, keepdims=True))
    return e / jnp.sum(e, axis=-1, keepdims=True)


def _att_bias(ext_row, qt, kb, ne):
    e = jnp.broadcast_to(ext_row, (qt, ne))
    return pltpu.roll(e, ne - qt + 1, 1, stride=1, stride_axis=0)[:, :kb]


def _att_specs(S, H, ah, qt, pad, ne):
    cw = ah * HEAD_DIM
    q_spec = pl.BlockSpec((qt, cw), lambda h, g: (g, 4 * H // ah + h))
    kv_spec = pl.BlockSpec((S + pad, cw), lambda h, g: (0, h))
    ext_spec = pl.BlockSpec((ah, 1, ne), lambda h, g: (h, 0, 0))
    row_spec = pl.BlockSpec((qt, cw), lambda h, g: (g, h))
    return cw, q_spec, kv_spec, ext_spec, row_spec


def _attn_fwd(proj, kp, vp, ext, W, rider=None):
    S = proj.shape[0]
    H = W // HEAD_DIM
    ah = min(ATT_HEADS_PER_STEP, H)
    qt, pad, kb, ne = _att_dims(S)
    cw, q_spec, kv_spec, ext_spec, row_spec = _att_specs(S, H, ah, qt, pad, ne)
    heads = [slice(a * HEAD_DIM, (a + 1) * HEAD_DIM) for a in range(ah)]

    def body(q_ref, k_ref, v_ref, ext_ref, o_ref, bias):
        g = pl.program_id(1)

        @pl.when(g == 0)
        def _():
            for a in range(ah):
                bias[a] = _att_bias(ext_ref[a], qt, kb, ne)

        start = pl.multiple_of(g * qt, qt)
        q16 = q_ref[...].astype(bf16)
        kb16 = k_ref[pl.ds(start, kb), :]
        vb16 = v_ref[pl.ds(start, kb), :]
        qk = [_dot(q16[:, sl], kb16[:, sl], NT) for sl in heads]
        pn = [_att_probs(qk[a], bias[a], start, qt, pad, kb).astype(bf16) for a in range(ah)]
        o_ref[...] = jnp.concatenate([_dot(pn[a], vb16[:, sl], NN) for a, sl in enumerate(heads)], axis=1).astype(bf16)

    outs = _grid_call(
        body,
        rider,
        name="attn_fwd",
        grid=(H // ah, S // qt),
        in_specs=[q_spec, kv_spec, kv_spec, ext_spec],
        out_specs=[row_spec],
        out_shape=[jax.ShapeDtypeStruct((S, W), bf16)],
        scratch_shapes=[pltpu.VMEM((ah, qt, kb), f32)],
        args=(proj, kp, vp, ext),
    )
    return outs[0] if rider is None else outs


def _attn_bwd(proj, kp, vp, ext, dycat, W, rider=None):
    S = proj.shape[0]
    H = W // HEAD_DIM
    ah = min(ATT_HEADS_PER_STEP, H)
    qt, pad, kb, ne = _att_dims(S)
    cw, q_spec, kv_spec, ext_spec, row_spec = _att_specs(S, H, ah, qt, pad, ne)
    heads = [slice(a * HEAD_DIM, (a + 1) * HEAD_DIM) for a in range(ah)]
    scale = HEAD_DIM ** -0.5
    ng = S // qt

    def body(q_ref, k_ref, v_ref, ext_ref, do_ref, dq_ref, dko_ref, dvo_ref, db_ref, bias, dk_ref, dv_ref):
        g = pl.program_id(1)

        @pl.when(g == 0)
        def _():
            for a in range(ah):
                bias[a] = _att_bias(ext_ref[a], qt, kb, ne)
            dk_ref[...] = jnp.zeros_like(dk_ref)
            dv_ref[...] = jnp.zeros_like(dv_ref)
            db_ref[...] = jnp.zeros_like(db_ref)

        start = pl.multiple_of(g * qt, qt)
        q16 = q_ref[...].astype(bf16)
        kb16 = k_ref[pl.ds(start, kb), :]
        vb16 = v_ref[pl.ds(start, kb), :]
        do16 = do_ref[...].astype(bf16)
        qk = [_dot(q16[:, sl], kb16[:, sl], NT) for sl in heads]
        dpn = [_dot(do16[:, sl], vb16[:, sl], NT) for sl in heads]
        pn = [_att_probs(qk[a], bias[a], start, qt, pad, kb) for a in range(ah)]
        ds = [pn[a] * (dpn[a] - jnp.sum(dpn[a] * pn[a], axis=-1, keepdims=True)) for a in range(ah)]
        ds16 = [d.astype(bf16) for d in ds]
        dv_ref[pl.ds(start, kb), :] += jnp.concatenate([_dot(pn[a].astype(bf16), do16[:, sl], TN) for a, sl in enumerate(heads)], axis=1)
        dq_ref[...] = (jnp.concatenate([_dot(ds16[a], kb16[:, sl], NN) for a, sl in enumerate(heads)], axis=1) * scale).astype(bf16)
        dk_ref[pl.ds(start, kb), :] += jnp.concatenate([_dot(ds16[a], q16[:, sl], TN) for a, sl in enumerate(heads)], axis=1) * scale
        for a in range(ah):
            db_ref[a] += ds[a]

        @pl.when(g == ng - 1)
        def _():
            dko_ref[...] = dk_ref[pl.ds(pad, S), :].astype(bf16)
            dvo_ref[...] = dv_ref[pl.ds(pad, S), :].astype(bf16)

    kv_out = pl.BlockSpec((S, cw), lambda h, g: (0, h))
    return _grid_call(
        body,
        rider,
        name="attn_bwd",
        grid=(H // ah, S // qt),
        in_specs=[q_spec, kv_spec, kv_spec, ext_spec,
                  pl.BlockSpec((qt, cw), lambda h, g: (g, H // ah + h))],
        out_specs=[row_spec, kv_out, kv_out, pl.BlockSpec((ah, qt, kb), lambda h, g: (h, 0, 0))],
        out_shape=[
            jax.ShapeDtypeStruct((S, W), bf16),
            jax.ShapeDtypeStruct((S, W), bf16),
            jax.ShapeDtypeStruct((S, W), bf16),
            jax.ShapeDtypeStruct((H, qt, kb), f32),
        ],
        scratch_shapes=[pltpu.VMEM((ah, qt, kb), f32), pltpu.VMEM((S + pad, cw), f32), pltpu.VMEM((S + pad, cw), f32)],
        args=(proj, kp, vp, ext, dycat),
    )


def _bias_onehot(qt, ne, nrel_pad):
    m = lax.broadcasted_iota(jnp.int32, (nrel_pad, ne), 1)
    r = lax.broadcasted_iota(jnp.int32, (nrel_pad, ne), 0)
    rel = LEFT_CHUNKS * CHUNK + qt - 1 - m
    hot = (r == jnp.clip(rel, -REL_CLIP, REL_CLIP) + REL_CLIP) & (m < ne - 1)
    return hot.astype(f32)


def _bias_ext(rel_bias_pad, onehot):
    H, _, nr = rel_bias_pad.shape
    ne = onehot.shape[1]

    def body(rb_ref, oh_ref, o_ref):
        o_ref[0] = _dot(rb_ref[0], oh_ref[...], NN, HIGHEST)

    return pl.pallas_call(
        body,
        name="bias_ext",
        grid=(H,),
        in_specs=[pl.BlockSpec((1, 1, nr), lambda h: (h, 0, 0)), pl.BlockSpec((nr, ne), lambda h: (0, 0))],
        out_specs=pl.BlockSpec((1, 1, ne), lambda h: (h, 0, 0)),
        out_shape=jax.ShapeDtypeStruct((H, 1, ne), f32),
        compiler_params=pltpu.CompilerParams(dimension_semantics=("parallel",)),
    )(rel_bias_pad, onehot)


def _bias_bwd(dbias, onehot):
    H, qt, kb = dbias.shape
    nr, ne = onehot.shape

    def body(d_ref, oh_ref, o_ref):
        row = lax.broadcasted_iota(jnp.int32, (qt, qt), 0)
        col = lax.broadcasted_iota(jnp.int32, (qt, qt), 1)
        flip = (row + col == qt - 1).astype(f32)
        x = _dot(flip, d_ref[0], NN, HIGHEST)
        x = jnp.concatenate([x, jnp.zeros((qt, ne - kb), f32)], axis=1)
        de = jnp.sum(pltpu.roll(x, 0, 1, stride=1, stride_axis=0), axis=0, keepdims=True)
        o_ref[0] = _dot(de, oh_ref[...], NT, HIGHEST)

    return pl.pallas_call(
        body,
        name="bias_bwd",
        grid=(H,),
        in_specs=[pl.BlockSpec((1, qt, kb), lambda h: (h, 0, 0)), pl.BlockSpec((nr, ne), lambda h: (0, 0))],
        out_specs=pl.BlockSpec((1, 1, nr), lambda h: (h, 0, 0)),
        out_shape=jax.ShapeDtypeStruct((H, 1, nr), f32),
        compiler_params=pltpu.CompilerParams(dimension_semantics=("parallel",)),
    )(dbias, onehot)


def _place():
    x, y, c = lax.axis_index("x"), lax.axis_index("y"), lax.axis_index("c")
    return x, y, c


def _flip(x, y, k):
    return (1 - x if k & 2 else x), (1 - y if k & 1 else y)


def _region(ref, shard_shape, axis, j, half):
    R, C = shard_shape
    if axis == 0:
        if half is None:
            return ref.at[pl.ds(j * R, R), :]
        return ref.at[pl.ds(j * R + half * (R // 2), R // 2), :]
    if half is None:
        return ref.at[:, pl.ds(j * C, C)]
    return ref.at[pl.ds(half * (R // 2), R // 2), pl.ds(j * C, C)]


class _Remote:
    def __init__(self, src, dst, send_sem, recv_sem, dev):
        self.src, self.dst, self.sems, self.dev = src, dst, (send_sem, recv_sem), dev

    def _copy(self, src, dst):
        return pltpu.make_async_remote_copy(src_ref=src, dst_ref=dst, send_sem=self.sems[0], recv_sem=self.sems[1],
                                            device_id=self.dev, device_id_type=MESH)

    def start(self):
        rows = self.src.shape[0]
        n = RDMA_PIECES if rows % (RDMA_PIECES * 2 * SUBLANES) == 0 else 1
        for p in range(n):
            part = pl.ds(p * (rows // n), rows // n)
            self._copy(self.src.at[part], self.dst.at[part]).start()

    def wait_recv(self):
        self._copy(self.src, self.dst).wait_recv()

    def wait_send(self):
        self._copy(self.src, self.dst).wait_send()


def _remote(src, dst, send_sem, recv_sem, dev):
    return _Remote(src, dst, send_sem, recv_sem, dev)


def _cast_place(shard, axis, where, *, name):
    R, C = shard.shape
    tr = _div(R, 256, 2 * SUBLANES)
    nr = R // tr
    full = (N_CHIPS * R, C) if axis == 0 else (R, N_CHIPS * C)
    omap = (lambda r, s: (s[0] * nr + r, 0)) if axis == 0 else (lambda r, s: (r, s[0]))

    def body(s_ref, x_ref, o_ref):
        o_ref[...] = x_ref[...].astype(bf16)

    return pl.pallas_call(
        body,
        name=name,
        grid_spec=pltpu.PrefetchScalarGridSpec(
            num_scalar_prefetch=1,
            grid=(nr,),
            in_specs=[pl.BlockSpec((tr, C), lambda r, s: (r, 0))],
            out_specs=pl.BlockSpec((tr, C), omap),
        ),
        out_shape=jax.ShapeDtypeStruct(full, bf16),
        compiler_params=pltpu.CompilerParams(dimension_semantics=("parallel",)),
    )(where, shard)


class _Rider(NamedTuple):
    ins: list
    out_shapes: list
    aliases: dict
    sems: list
    first: Callable
    last: Callable


def _run_rider(rider, *, name):
    ni, no = len(rider.ins), len(rider.out_shapes)

    def body(*refs):
        ins, outs, sems = refs[:ni], refs[ni:ni + no], refs[ni + no:]
        rider.first(ins, outs, sems)
        rider.last(ins, outs, sems)

    return pl.pallas_call(
        body,
        name=name,
        in_specs=[_hbm()] * ni,
        out_specs=[_hbm()] * no,
        out_shape=rider.out_shapes,
        input_output_aliases=rider.aliases,
        scratch_shapes=rider.sems,
    )(*rider.ins)


def _start_all(copies):
    def first(ins, outs, sems):
        for cp in copies(ins, outs, sems):
            cp.start()

    def last(ins, outs, sems):
        cps = copies(ins, outs, sems)
        for cp in cps:
            cp.wait_recv()
        for cp in cps:
            cp.wait_send()

    return first, last


def _relay_copy(buf, shard_shape, axis, send_sem, recv_sem):
    x, y, c = _place()
    first = c == 0
    sx, sy = jnp.where(first, 1 - x, x), jnp.where(first, y, 1 - y)
    dx, dy = jnp.where(first, x, 1 - x), jnp.where(first, 1 - y, y)
    part = _region(buf, shard_shape, axis, 2 * sx + sy, c)
    return _remote(part, part, send_sem, recv_sem, (dx, dy, c))


def _mm_while_gathering(a, full, shard_shape, where, *, tm, tn, name):
    S, K = a.shape
    R, Cs = shard_shape
    N = full.shape[1]
    tps, nj, ni = Cs // tn, N // tn, S // tm
    assert R == K and Cs % tn == 0 and S % tm == 0

    def block(j, chip):
        return jnp.bitwise_xor(chip, j // tps) * tps + j % tps

    def body(s_ref, a_ref, w_in, o_ref, w_ref, buf, fsem, send1, recv1, send2, recv2):
        j, i = pl.program_id(0), pl.program_id(1)
        x, y, c = _place()
        chip = 2 * x + y

        def ici(k):
            px, py = _flip(x, y, k)
            mine = _region(w_ref, shard_shape, 1, chip, c)
            return _remote(mine, mine, send1.at[k - 1], recv1.at[k - 1], (px, py, c))

        def handed(k, half):
            px, py = _flip(x, y, k)
            land = _region(w_ref, shard_shape, 1, 2 * px + py, half)
            return land, _remote(land, land, send2.at[k - 1], recv2.at[k - 1], (x, y, 1 - c))

        def tile(jj):
            col = pl.multiple_of(block(jj, chip) * tn, LANES)
            return pltpu.make_async_copy(w_ref.at[:, pl.ds(col, tn)], buf.at[jj % 2], fsem.at[jj % 2])

        def relay():
            return _relay_copy(w_ref, shard_shape, 1, send1.at[2], recv1.at[2])

        def landed(k):
            land, forward = handed(k, c)
            _remote(land, land, send1.at[k - 1], recv1.at[k - 1], (x, y, c)).wait_recv()
            forward.start()

        @pl.when((j == 0) & (i == 0))
        def _():
            for k in (1, 2):
                ici(k).start()
            tile(j).start()

        @pl.when(i == 0)
        def _():
            tile(j).wait()

        @pl.when((i == ni - 1) & (j + 1 < nj))
        def _():
            @pl.when(j + 1 == tps)
            def _():
                landed(1)
                landed(2)
                relay().start()
                handed(1, 1 - c)[1].wait_recv()

            @pl.when(j + 1 == 2 * tps)
            def _():
                handed(2, 1 - c)[1].wait_recv()

            @pl.when(j + 1 == 3 * tps)
            def _():
                landed(3)
                handed(3, 1 - c)[1].wait_recv()

            tile(j + 1).start()

        o_ref[...] = _dot(a_ref[...], buf[j % 2], NN)

        @pl.when((j == nj - 1) & (i == ni - 1))
        def _():
            ici(1).wait_send()
            ici(2).wait_send()
            relay().wait_send()
            for k in (1, 2, 3):
                handed(k, c)[1].wait_send()

    sems = pltpu.SemaphoreType.DMA((3,))
    spec = pltpu.PrefetchScalarGridSpec(
        num_scalar_prefetch=1,
        grid=(nj, ni),
        in_specs=[pl.BlockSpec((tm, K), lambda j, i, s: (i, 0)), _hbm()],
        out_specs=[pl.BlockSpec((tm, tn), lambda j, i, s: (i, block(j, s[0]))), _hbm()],
        scratch_shapes=[pltpu.VMEM((2, K, tn), full.dtype), pltpu.SemaphoreType.DMA((2,)), sems, sems, sems, sems],
    )
    return pl.pallas_call(
        body,
        name=name,
        grid_spec=spec,
        out_shape=[jax.ShapeDtypeStruct((S, N), f32), jax.ShapeDtypeStruct(full.shape, full.dtype)],
        input_output_aliases={2: 1},
        compiler_params=pltpu.CompilerParams(dimension_semantics=("arbitrary", "arbitrary")),
    )(where, a, full)


class _GatherJob(NamedTuple):
    buf: int
    shard_shape: tuple
    axis: int
    split: bool
    rels: tuple
    hand_over: bool
    relay: bool = False


def _gather_rider(bufs, jobs):
    def copies(ins, outs, sems):
        x, y, c = _place()
        cps = []
        for n, job in enumerate(jobs):
            if job.relay:
                cps.append(_relay_copy(outs[job.buf], job.shard_shape, job.axis, sems[0].at[3 * n + 2], sems[1].at[3 * n + 2]))
                continue
            half = c if job.split else None
            for k in job.rels:
                px, py = _flip(x, y, k)
                chip, dev = (2 * px + py, (x, y, 1 - c)) if job.hand_over else (2 * x + y, (px, py, c))
                cps.append(_remote(_region(ins[job.buf], job.shard_shape, job.axis, chip, half),
                                   _region(outs[job.buf], job.shard_shape, job.axis, chip, half),
                                   sems[0].at[3 * n + k - 1], sems[1].at[3 * n + k - 1], dev))
        return cps

    assert all(job.split or not job.hand_over for job in jobs)
    sems = pltpu.SemaphoreType.DMA((3 * len(jobs),))
    return _Rider(list(bufs), [jax.ShapeDtypeStruct(b.shape, b.dtype) for b in bufs], {m: m for m in range(len(bufs))},
                  [sems, sems], *_start_all(copies))


def _chips_rider(ts, rels=(1, 2, 3), landing=None):
    n = len(ts)

    def copies(ins, outs, sems):
        x, y, c = _place()
        cps = []
        for m in range(n):
            for k in rels:
                px, py = _flip(x, y, k)
                cps.append(_remote(ins[m].at[2 * px + py], outs[m].at[k - 1], sems[0].at[3 * m + k - 1], sems[1].at[3 * m + k - 1], (px, py, c)))
        return cps

    sems = pltpu.SemaphoreType.DMA((3 * n,))
    return _Rider(list(ts) + list(landing or []), [jax.ShapeDtypeStruct((3,) + t.shape[1:], t.dtype) for t in ts],
                  {n + m: m for m in range(n)} if landing else {}, [sems, sems], *_start_all(copies))


def _sibling_rider(dws, shard_shapes, axes):
    n = len(dws)

    def copies(ins, outs, sems):
        x, y, c = _place()
        return [_remote(_region(ins[m], shard_shapes[m], axes[m], j, 1 - c), outs[m].at[j],
                        sems[0].at[N_CHIPS * m + j], sems[1].at[N_CHIPS * m + j], (x, y, 1 - c))
                for m in range(n) for j in range(N_CHIPS)]

    sems = pltpu.SemaphoreType.DMA((N_CHIPS * n,))
    return _Rider(list(dws), [jax.ShapeDtypeStruct((N_CHIPS, s[0] // 2, s[1]), d.dtype) for s, d in zip(shard_shapes, dws)], {},
                  [sems, sems], *_start_all(copies))


def _join_rider(gs):
    n = len(gs)

    def copies(ins, outs, sems):
        x, y, c = _place()
        cps = []
        for m in range(n):
            hr = gs[m].shape[0] // 2
            cps.append(_remote(ins[m].at[pl.ds(c * hr, hr), :], outs[m].at[pl.ds(c * hr, hr), :], sems[0].at[m], sems[1].at[m], (x, y, 1 - c)))
        return cps

    sems = pltpu.SemaphoreType.DMA((n,))
    return _Rider(list(gs), [jax.ShapeDtypeStruct(g.shape, g.dtype) for g in gs], {m: m for m in range(n)}, [sems, sems], *_start_all(copies))


def _both(r1, r2):
    i1, o1, s1 = len(r1.ins), len(r1.out_shapes), len(r1.sems)
    aliases = dict(r1.aliases)
    aliases.update({i1 + i: o1 + o for i, o in r2.aliases.items()})

    def first(ins, outs, sems):
        r1.first(ins[:i1], outs[:o1], sems[:s1])
        r2.first(ins[i1:], outs[o1:], sems[s1:])

    def last(ins, outs, sems):
        r1.last(ins[:i1], outs[:o1], sems[:s1])
        r2.last(ins[i1:], outs[o1:], sems[s1:])

    return _Rider(r1.ins + r2.ins, r1.out_shapes + r2.out_shapes, aliases, r1.sems + r2.sems, first, last)


def _pair_sum(dw, got, shard_shape, axis, where, *, name):
    R, C = shard_shape
    hr = R // 2
    tr = _div(hr, 256, 2 * SUBLANES)
    nr = hr // tr
    dmap = (lambda j, r, s: ((2 * j + s[1]) * nr + r, 0)) if axis == 0 else (lambda j, r, s: (s[1] * nr + r, j))
    slot = pl.BlockSpec((1, tr, C), lambda j, r, s: (j, r, 0))

    def body(s_ref, d_ref, g_ref, o_ref):
        o_ref[0] = (d_ref[...].astype(f32) + g_ref[0].astype(f32)).astype(bf16)

    return pl.pallas_call(
        body,
        name=name,
        grid_spec=pltpu.PrefetchScalarGridSpec(
            num_scalar_prefetch=1,
            grid=(N_CHIPS, nr),
            in_specs=[pl.BlockSpec((tr, C), dmap), slot],
            out_specs=slot,
        ),
        out_shape=jax.ShapeDtypeStruct((N_CHIPS, hr, C), bf16),
        compiler_params=pltpu.CompilerParams(dimension_semantics=("parallel", "parallel")),
    )(where, dw, got)


def _chip_sum(pair, others, where, *, name):
    _, hr, C = pair.shape
    tr = _div(hr, 256, 2 * SUBLANES)
    nr = hr // tr

    def body(s_ref, p_ref, o_ref, g_ref):
        g_ref[...] = p_ref[0].astype(f32) + o_ref[0].astype(f32) + o_ref[1].astype(f32) + o_ref[2].astype(f32)

    return pl.pallas_call(
        body,
        name=name,
        grid_spec=pltpu.PrefetchScalarGridSpec(
            num_scalar_prefetch=1,
            grid=(nr,),
            in_specs=[pl.BlockSpec((1, tr, C), lambda r, s: (s[0], r, 0)), pl.BlockSpec((3, tr, C), lambda r, s: (0, r, 0))],
            out_specs=pl.BlockSpec((tr, C), lambda r, s: (s[1] * nr + r, 0)),
        ),
        out_shape=jax.ShapeDtypeStruct((2 * hr, C), f32),
        compiler_params=pltpu.CompilerParams(dimension_semantics=("parallel",)),
    )(where, pair, others)


def _allreduce_small(v):
    rows = v.shape[0]

    def body(v_ref, o_ref, gath, send, recv):
        x, y, c = _place()
        me = 4 * x + 2 * y + c
        gath[pl.ds(me, 1)] = v_ref[...][None]
        cps = []
        for k in range(1, N_DEV):
            peer = (1 - x if k & 4 else x, 1 - y if k & 2 else y, 1 - c if k & 1 else c)
            cp = _remote(v_ref, gath.at[me], send.at[k - 1], recv.at[k - 1], peer)
            cp.start()
            cps.append(cp)
        for cp in cps:
            cp.wait_recv()
        for cp in cps:
            cp.wait_send()
        acc = gath[0]
        for d in range(1, N_DEV):
            acc = acc + gath[d]
        o_ref[...] = acc

    return pl.pallas_call(
        body,
        name="small_allreduce",
        in_specs=[_vmem()],
        out_specs=_vmem(),
        out_shape=jax.ShapeDtypeStruct(v.shape, f32),
        scratch_shapes=[pltpu.VMEM((N_DEV, rows, LANES), f32), pltpu.SemaphoreType.DMA((N_DEV - 1,)), pltpu.SemaphoreType.DMA((N_DEV - 1,))],
    )(v)


def _adamw_math(w, g, m, v):
    m = ADAM_B1 * m + (1.0 - ADAM_B1) * g
    v = ADAM_B2 * v + (1.0 - ADAM_B2) * (g * g)
    m_hat = m / (1.0 - ADAM_B1 ** ADAM_STEP)
    v_hat = v / (1.0 - ADAM_B2 ** ADAM_STEP)
    delta = -ADAM_LR * (m_hat / (jnp.sqrt(v_hat) + ADAM_EPS) + ADAM_WD * w)
    return delta, m, v


def _adamw(w, g, m, v, *, name):
    R, C = w.shape
    tr = _div(R, 128, SUBLANES)

    def body(w_ref, g_ref, m_ref, v_ref, go_ref, d_ref, mo_ref, vo_ref):
        gv = g_ref[...]
        d, mn, vn = _adamw_math(w_ref[...], gv, m_ref[...], v_ref[...])
        go_ref[...] = gv
        d_ref[...] = d
        mo_ref[...] = mn
        vo_ref[...] = vn

    spec = pl.BlockSpec((tr, C), lambda r: (r, 0))
    return pl.pallas_call(
        body,
        name=name,
        grid=(R // tr,),
        in_specs=[spec] * 4,
        out_specs=[spec] * 4,
        out_shape=[jax.ShapeDtypeStruct((R, C), f32)] * 4,
        compiler_params=pltpu.CompilerParams(dimension_semantics=("parallel",)),
    )(w, g, m, v)


def _pack(arrs):
    flat = jnp.concatenate([a.reshape(-1).astype(f32) for a in arrs])
    tile = SUBLANES * LANES
    total = -(-flat.shape[0] // tile) * tile
    return jnp.pad(flat, (0, total - flat.shape[0])).reshape(total // LANES, LANES)


def _unpack(buf, shapes):
    flat = buf.reshape(-1)
    out, off = [], 0
    for s in shapes:
        size = int(np.prod(s))
        out.append(flat[off:off + size].reshape(s))
        off += size
    return out


def kernel(x, p, norm_mix, w_in, lb_logits, hg_norm, rel_bias, w_out, norm_ffn, w_up, conv_w, conv_b, w_down, norm_ple, w_ple_gate, w_ple_proj, final_norm, loss_target, m_norm_mix, m_w_in, m_lb_logits, m_hg_norm, m_rel_bias, m_w_out, m_norm_ffn, m_w_up, m_conv_w, m_conv_b, m_w_down, m_norm_ple, m_w_ple_gate, m_w_ple_proj, m_final_norm, v_norm_mix, v_w_in, v_lb_logits, v_hg_norm, v_rel_bias, v_w_out, v_norm_ffn, v_w_up, v_conv_w, v_conv_b, v_w_down, v_norm_ple, v_w_ple_gate, v_w_ple_proj, v_final_norm):
    S, D = x.shape[1], x.shape[2]
    xv, pv, tgt = x[0], p[0, 0], loss_target[0]
    W = (w_in.shape[2] * N_CHIPS) // 7
    H = W // HEAD_DIM
    F = w_down.shape[1] * N_CHIPS
    tc = _div(F // 2, 1408, LANES)
    nj = F // tc
    jx, jy = lax.axis_index("x"), lax.axis_index("y")
    chip = 2 * jx + jy

    big = [w_in[0], w_out[0], w_up[0], w_down[0], w_ple_gate[0], w_ple_proj[0]]
    big_axes = [1, 0, 1, 0, 0, 1]
    where = jnp.stack([chip, lax.axis_index("c")]).astype(jnp.int32)
    cw_pad = jnp.pad(conv_w[0], ((0, SUBLANES - conv_w.shape[1]), (0, 0)))
    cw_mine = lax.dynamic_update_slice(jnp.zeros((N_CHIPS * SUBLANES, cw_pad.shape[1]), f32), cw_pad, (chip * SUBLANES, 0))
    placed = [_cast_place(b, ax, where, name=f"cast_place_{i}") for i, (b, ax) in enumerate(zip(big, big_axes))]
    shard_shapes = [b.shape for b in big]

    every = (1, 2, 3)

    def job(buf, i, rels, hand_over=False, relay=False):
        if i is None:
            return _GatherJob(buf, cw_pad.shape, 0, False, rels, hand_over)
        return _GatherJob(buf, shard_shapes[i], big_axes[i], True, rels, hand_over, relay)

    tm = _div(S, 1024, LANES)

    a1 = _rms_fwd(xv, norm_mix, name="rms_mix")
    cs_in = shard_shapes[0][1]
    tn_in = _div(cs_in, 1024, LANES)
    proj, W_in = _mm_while_gathering(a1, placed[0], shard_shapes[0], where, tm=tm, tn=tn_in, name="mm_in")
    qt, pad, kb, ne = _att_dims(S)
    nrel = rel_bias.shape[2]
    nrel_pad = -(-nrel // LANES) * LANES
    onehot = _bias_onehot(qt, ne, nrel_pad)
    rb_pad = jnp.pad(rel_bias[0], ((0, 0), (0, nrel_pad - nrel)))[:, None, :]
    ext = _bias_ext(rb_pad, onehot)
    kp, vp = _kv_padded(proj, W, pad)
    y_att, W_out_i, W_up_i = _attn_fwd(
        proj, kp, vp, ext, W, rider=_gather_rider([placed[1], placed[2]], [job(0, 1, every), job(1, 2, (1, 2))]))
    y_hg, o_hg, st, W_out, W_up_i = _hgrn_fwd(
        proj, lb_logits, hg_norm, W,
        rider=_gather_rider([W_out_i, W_up_i], [job(0, 1, every, True), job(1, 2, (1, 2), True), job(1, 2, (3,), relay=True)]))
    ycat = [y_hg, y_att]
    tw = _div(W, 1024, LANES)

    tn_d = _div(D, 512, LANES)
    h1, W_up = _mm(ycat, W_out, dims="nn", tm=tm, tn=tn_d, tk=tw, out_dtype=f32, res=xv, name="mm_out",
                   rider=_gather_rider([W_up_i], [job(0, 2, (3,), True)]))
    a2 = _rms_fwd(h1, norm_ffn, name="rms_ffn")
    perm = lambda b: (b % 2) * nj + b // 2
    u, W_down_i, W_pg_i, W_pp_i, cw_all = _mm(
        a2, W_up, dims="nn", tm=tm, tn=tc, tk=D, out_dtype=f32, name="mm_up", b_idx=lambda i, j, k: (k, perm(j)),
        rider=_gather_rider([placed[3], placed[4], placed[5], cw_mine], [job(0, 3, every), job(1, 4, every), job(2, 5, every), job(3, None, every)]))
    ncw = conv_w.shape[1]
    cw_full = jnp.transpose(cw_all.reshape(N_CHIPS, SUBLANES, -1)[:, :ncw], (1, 0, 2)).reshape(ncw, -1)
    cb = conv_b
    mact, W_down, W_pg, W_pp = _convact_fwd(
        u, cw_full, cb, tc, rider=_gather_rider([W_down_i, W_pg_i, W_pp_i], [job(0, 3, every, True), job(1, 4, every, True), job(2, 5, every, True)]))
    tn_w = _div(D, 1024, LANES)
    h2 = _mm(mact, W_down, dims="nn", tm=_div(S, 512, LANES), tn=tn_w, tk=F, out_dtype=f32, res=h1, name="mm_down", cols_outer=True)
    a3 = _rms_fwd(h2, norm_ple, name="rms_ple")
    zg = _mm(a3, W_pg, dims="nn", tm=tm, tn=tn_d, tk=D, out_dtype=f32, name="mm_ple_gate")
    p16 = pv.astype(bf16)
    pp = _mm(p16, W_pp, dims="nn", tm=tm, tn=tn_d, tk=pv.shape[1], out_dtype=f32, name="mm_ple_proj")
    dh3, dzg, dpp, loss_part, d_fn = _tail(h2, zg, pp, tgt, final_norm.reshape(1, D))

    pair = {}

    def siblings(idx, grads):
        return _sibling_rider(grads, [shard_shapes[i] for i in idx], [big_axes[i] for i in idx])

    def pair_up(idx, grads, got):
        for i, d, g in zip(idx, grads, got):
            pair[i] = _pair_sum(d, g, shard_shapes[i], big_axes[i], where, name=f"grad_pair_sum_{i}")

    def reduced_half(i, landed):
        return _chip_sum(pair[i], landed, where, name=f"grad_chip_sum_{i}")

    tk_s = S
    dW_pp = _mm(p16, dpp, dims="tn", tm=pv.shape[1], tn=tn_w, tk=tk_s, out_dtype=bf16, name="mm_d_ple_proj")
    dW_pg = _mm(a3, dzg, dims="tn", tm=tn_w, tn=tn_w, tk=tk_s, out_dtype=bf16, name="mm_d_ple_gate")
    da3, *got = _mm(dzg, W_pg, dims="nt", tm=tm, tn=_div(D, 1024, LANES), tk=D, out_dtype=f32, name="mm_da3",
                    rider=siblings([4, 5], [dW_pg, dW_pp]))
    pair_up([4, 5], [dW_pg, dW_pp], got)
    dh2, dh2b, d_nple = _rms_bwd(h2, norm_ple, da3, dh3, name="rms_ple_bwd")
    dm, land4, land5 = _mm(dh2b, W_down, dims="nt", tm=tm, tn=_div(F, 512, LANES), tk=D, out_dtype=f32, name="mm_dm",
                           rider=_chips_rider([pair[4], pair[5]]))
    dW_down = _mm(mact, dh2b, dims="tn", tm=_div(F, 1408, LANES), tn=tn_d, tk=tk_s, out_dtype=bf16, name="mm_d_down")
    du, dcw_g, dcw_v, dcb_g, dcb_v, got3 = _convact_bwd(u, dm, cw_full, cb, tc, rider=siblings([3], [dW_down]))
    pair_up([3], [dW_down], [got3])
    dW_up, land3 = _mm(a2, du, dims="tn", tm=_div(D, 1024, LANES), tn=tc, tk=tk_s, out_dtype=bf16, name="mm_d_up",
                       o_idx=lambda i, j, k: (i, perm(j)), rider=_chips_rider([pair[3]]))
    da2, got2 = _mm(du, W_up, dims="nt", tm=tm, tn=D, tk=tc, out_dtype=f32, name="mm_da2",
                    b_idx=lambda i, j, k: (j, perm(k)), rider=siblings([2], [dW_up]))
    pair_up([2], [dW_up], [got2])
    dh1, dh1b, d_nffn = _rms_bwd(h1, norm_ffn, da2, dh2, name="rms_ffn_bwd")
    dycat = _mm(dh1b, W_out, dims="nt", tm=tm, tn=_div(D, 1024, LANES), tk=D, out_dtype=f32, name="mm_dycat")
    dW_out = _mm(ycat, dh1b, dims="tn", tm=tw, tn=tn_d, tk=tk_s, out_dtype=bf16, name="mm_d_out")
    dq_att, dk_att, dv_att, dbias, land2, got1 = _attn_bwd(
        proj, kp, vp, ext, dycat, W, rider=_both(_chips_rider([pair[2]], rels=(1, 2)), siblings([1], [dW_out])))
    pair_up([1], [dW_out], [got1])
    d_hg4, d_lbl, d_hgn, land2 = _hgrn_bwd(proj, o_hg, st, dycat, lb_logits, hg_norm, W,
                                           rider=_chips_rider([pair[2]], rels=(3,), landing=[land2]))
    d_rb = _bias_bwd(dbias, onehot)[:, 0, :nrel]
    dproj = [d_hg4, dq_att, dk_att, dv_att]
    tw = _div(W, 1024, LANES)
    dW_in, land1 = _mm(a1, dproj, dims="tn", tm=tn_w, tn=_div(W, 512, LANES), tk=tk_s, out_dtype=bf16, name="mm_d_in",
                       rider=_chips_rider([pair[1]]))
    pair_up([0], [dW_in], _run_rider(siblings([0], [dW_in]), name="grads_to_sibling_0"))
    halves = [reduced_half(i, land) for i, land in ((1, land1), (2, land2), (3, land3), (4, land4), (5, land5))]
    da1, land0, *g_rest = _mm(dproj, W_in, dims="nt", tm=tm, tn=D, tk=tw, out_dtype=f32, name="mm_da1",
                              rider=_both(_chips_rider([pair[0]]), _join_rider(halves)))
    grad_x, _, d_nmix = _rms_bwd(xv, norm_mix, da1, dh1, name="rms_mix_bwd")
    g_big = list(_run_rider(_join_rider([reduced_half(0, land0)]), name="grads_join_w_in")) + list(g_rest)

    d_cw = jnp.concatenate([dcw_g, dcw_v], axis=1)
    d_cb = jnp.concatenate([dcb_g, dcb_v], axis=1)
    small_parts = [loss_part[:, :1], d_nmix, d_lbl, d_hgn, d_rb, d_nffn, d_cw, d_cb, d_nple, d_fn]
    small_shapes = [(1, 1), (1, D), lb_logits.shape, hg_norm.shape, (H, nrel), (1, D), (ncw, 2 * F), (1, 2 * F), (1, D), (1, D)]
    red = _unpack(_allreduce_small(_pack(small_parts)), small_shapes)
    loss = red[0].reshape(())
    g_nmix, g_lbl, g_hgn, g_rb, g_nffn, g_cw_all, g_cb, g_nple, g_fn = red[1:]
    csh = conv_w.shape[2]
    g_cw = lax.dynamic_slice(g_cw_all, (0, chip * csh), (ncw, csh))
    small_g = [g_nmix, g_lbl, g_hgn, g_rb[None], g_nffn, g_cw[None], g_cb, g_nple, g_fn.reshape(D)]
    small_w = [norm_mix, lb_logits, hg_norm, rel_bias, norm_ffn, conv_w, conv_b, norm_ple, final_norm]
    small_m = [m_norm_mix, m_lb_logits, m_hg_norm, m_rel_bias, m_norm_ffn, m_conv_w, m_conv_b, m_norm_ple, m_final_norm]
    small_v = [v_norm_mix, v_lb_logits, v_hg_norm, v_rel_bias, v_norm_ffn, v_conv_w, v_conv_b, v_norm_ple, v_final_norm]
    shapes_s = [w.shape for w in small_w]
    _, sd, sm, sv = _adamw(_pack(small_w), _pack(small_g), _pack(small_m), _pack(small_v), name="adamw_small")
    small_g = [g.reshape(s) for g, s in zip(small_g, shapes_s)]
    small_d, small_nm, small_nv = _unpack(sd, shapes_s), _unpack(sm, shapes_s), _unpack(sv, shapes_s)

    big_m = [m_w_in[0], m_w_out[0], m_w_up[0], m_w_down[0], m_w_ple_gate[0], m_w_ple_proj[0]]
    big_v = [v_w_in[0], v_w_out[0], v_w_up[0], v_w_down[0], v_w_ple_gate[0], v_w_ple_proj[0]]
    big_g, big_d, big_nm, big_nv = [], [], [], []
    for i in range(6):
        g_, d_, m_, v_ = _adamw(big[i], g_big[i], big_m[i], big_v[i], name=f"adamw_{i}")
        big_g.append(g_[None])
        big_d.append(d_[None])
        big_nm.append(m_[None])
        big_nv.append(v_[None])
    g_big = big_g

    def order(sm_list, bg_list):
        s, b = sm_list, bg_list
        return [s[0], b[0], s[1], s[2], s[3], b[1], s[4], b[2], s[5], s[6], b[3], s[7], b[4], b[5], s[8]]

    return (loss, grad_x[None], *order(small_g, g_big), *order(small_d, big_d), *order(small_nm, big_nm), *order(small_nv, big_nv))
```

```python
import functools
from typing import Callable, NamedTuple

import jax
import jax.numpy as jnp
import numpy as np
from jax import lax
from jax.experimental import pallas as pl
from jax.experimental.pallas import tpu as pltpu
from jax.experimental.pallas import tpu_sc as plsc

f32 = jnp.float32
bf16 = jnp.bfloat16

CHUNK = 64
HEAD_DIM = 128
LEFT_CHUNKS = 8
REL_CLIP = 128
EPS = 1e-6
HG_SUB = 16
HG_HEADS_PER_STEP = 8
ATT_Q_ROWS = 256
ATT_HEADS_PER_STEP = 2
ADAM_LR, ADAM_B1, ADAM_B2, ADAM_EPS, ADAM_WD, ADAM_STEP = 0.001, 0.9, 0.999, 1e-08, 0.01, 10
LANES = 128
SUBLANES = 8
N_CHIPS = 4
N_DEV = 8
RDMA_PIECES = 4
SC_TILES = 32
SC_LANES = 16
MESH = pl.DeviceIdType.MESH
NEG = float(np.finfo(np.float32).min)

NN = (((1,), (0,)), ((), ()))
NT = (((1,), (1,)), ((), ()))
TN = (((0,), (0,)), ((), ()))
HIGHEST = lax.Precision.HIGHEST


def _dot(a, b, dims, precision=None):
    return lax.dot_general(a, b, dims, preferred_element_type=f32, precision=precision)


def _sigmoid(v):
    return 1.0 / (1.0 + jnp.exp(-v))


def _div(n, pref, mult):
    best = None
    d = mult
    while d <= min(n, pref):
        if n % d == 0:
            best = d
        d += mult
    return best if best is not None else n


def _hbm():
    return pl.BlockSpec(memory_space=pltpu.HBM)


def _vmem():
    return pl.BlockSpec(memory_space=pltpu.VMEM)


def _mm(a, b, *, dims, tm, tn, tk, out_dtype, name, res=None, a_idx=None, b_idx=None, o_idx=None, out_shape=None, rider=None,
        cols_outer=False):
    a_list = list(a) if isinstance(a, (list, tuple)) else [a]
    b_list = list(b) if isinstance(b, (list, tuple)) else [b]
    na, nb = len(a_list), len(b_list)
    if dims == "nn":
        assert na == 1 and nb == 1
        (M, K), (_, N) = a.shape, b.shape
        a_blk, b_blk, dn = (tm, tk), (tk, tn), NN
        a_def, b_def = (lambda i, j, k: (i, k)), (lambda i, j, k: (k, j))
    elif dims == "nt":
        assert nb == 1
        M, K, N = a_list[0].shape[0], sum(p.shape[1] for p in a_list), b.shape[0]
        a_blk, b_blk, dn = (tm, tk), (tn, tk), NT
        a_def, b_def = (lambda i, j, k: (i, k)), (lambda i, j, k: (j, k))
    else:
        assert na == 1
        (K, M), N = a.shape, sum(p.shape[1] for p in b_list)
        a_blk, b_blk, dn = (tk, tm), (tk, tn), TN
        a_def, b_def = (lambda i, j, k: (k, i)), (lambda i, j, k: (k, j))
    assert M % tm == 0 and N % tn == 0 and K % tk == 0, (name, M, N, K, tm, tn, tk)
    nk = K // tk
    has_res = res is not None
    counts = [p.shape[1] // tk for p in a_list] if na > 1 else [p.shape[1] // tn for p in b_list]
    starts = [sum(counts[:p]) for p in range(len(counts))]

    def body(*refs):
        a_refs, b_refs, rest = refs[:na], refs[na:na + nb], refs[na + nb:]
        res_ref = rest[0] if has_res else None
        o_ref = rest[1] if has_res else rest[0]
        k = pl.program_id(2)
        j = pl.program_id(0) if cols_outer else pl.program_id(1)

        def finish(acc):
            if has_res:
                acc = acc + res_ref[...]
            o_ref[...] = acc.astype(out_dtype)

        def consume(part):
            if nk == 1:
                finish(part)
                return
            acc_ref = refs[-1]

            @pl.when(k == 0)
            def _():
                acc_ref[...] = part

            @pl.when(k > 0)
            def _():
                acc_ref[...] += part

            @pl.when(k == nk - 1)
            def _():
                finish(acc_ref[...])

        if len(counts) == 1:
            consume(_dot(a_refs[0][...].astype(bf16), b_refs[0][...].astype(bf16), dn))
        else:
            at = k if na > 1 else j
            for p, (s, n) in enumerate(zip(starts, counts)):
                a_ref, b_ref = a_refs[p if na > 1 else 0], b_refs[p if nb > 1 else 0]
                pl.when((at >= s) & (at < s + n))(
                    functools.partial(lambda ar, br: consume(_dot(ar[...].astype(bf16), br[...].astype(bf16), dn)), a_ref, b_ref))

    def ordered(custom, default):
        f = custom or default
        return (lambda j, i, k: f(i, j, k)) if cols_outer else f

    def piece_map(default, p):
        s, n = starts[p], counts[p]
        if na > 1:
            return lambda i, j, k: default(i, j, jnp.clip(k - s, 0, n - 1))
        return lambda i, j, k: default(i, jnp.clip(j - s, 0, n - 1), k)

    o_map = ordered(o_idx, lambda i, j, k: (i, j))
    a_specs = [pl.BlockSpec(a_blk, ordered(a_idx, piece_map(a_def, p) if na > 1 else a_def)) for p in range(na)]
    b_specs = [pl.BlockSpec(b_blk, ordered(b_idx, piece_map(b_def, p) if nb > 1 else b_def)) for p in range(nb)]
    in_specs = a_specs + b_specs
    args = a_list + b_list
    if has_res:
        in_specs.append(pl.BlockSpec((tm, tn), o_map))
        args.append(res)
    outs = _grid_call(
        body,
        rider,
        name=name,
        grid=(N // tn, M // tm, nk) if cols_outer else (M // tm, N // tn, nk),
        in_specs=in_specs,
        out_specs=[pl.BlockSpec((tm, tn), o_map)],
        out_shape=[jax.ShapeDtypeStruct(out_shape or (M, N), out_dtype)],
        scratch_shapes=[pltpu.VMEM((tm, tn), f32)] if nk > 1 else [],
        args=args,
    )
    return outs[0] if rider is None else outs


def _rms_fwd(xv, g, *, name):
    S, D = xv.shape
    ts = _div(S, 512, SUBLANES)

    def body(x_ref, g_ref, o_ref):
        v = x_ref[...]
        r = lax.rsqrt(jnp.mean(v * v, axis=-1, keepdims=True) + EPS)
        o_ref[...] = (v * r * g_ref[...]).astype(bf16)

    return pl.pallas_call(
        body,
        name=name,
        grid=(S // ts,),
        in_specs=[pl.BlockSpec((ts, D), lambda i: (i, 0)), pl.BlockSpec((1, D), lambda i: (0, 0))],
        out_specs=pl.BlockSpec((ts, D), lambda i: (i, 0)),
        out_shape=jax.ShapeDtypeStruct((S, D), bf16),
        compiler_params=pltpu.CompilerParams(dimension_semantics=("parallel",)),
    )(xv, g)


def _rms_bwd(xv, g, dy, dres, *, name, rider=None):
    S, D = xv.shape
    ts = _div(S, 256, SUBLANES)

    def body(x_ref, g_ref, dy_ref, dres_ref, dx_ref, dxb_ref, dg_ref):
        i = pl.program_id(0)
        v = x_ref[...]
        r = lax.rsqrt(jnp.mean(v * v, axis=-1, keepdims=True) + EPS)
        vn = v * r
        d = dy_ref[...]
        part = jnp.sum(d * vn, axis=0, keepdims=True)

        @pl.when(i == 0)
        def _():
            dg_ref[...] = part

        @pl.when(i > 0)
        def _():
            dg_ref[...] += part

        t = d * g_ref[...]
        dx = dres_ref[...] + r * (t - vn * jnp.mean(t * vn, axis=-1, keepdims=True))
        dx_ref[...] = dx
        dxb_ref[...] = dx.astype(bf16)

    row = pl.BlockSpec((ts, D), lambda i: (i, 0))
    vec = pl.BlockSpec((1, D), lambda i: (0, 0))
    return _grid_call(
        body,
        rider,
        name=name,
        grid=(S // ts,),
        in_specs=[row, vec, row, row],
        out_specs=[row, row, vec],
        out_shape=[jax.ShapeDtypeStruct((S, D), f32), jax.ShapeDtypeStruct((S, D), bf16), jax.ShapeDtypeStruct((1, D), f32)],
        scratch_shapes=[],
        args=(xv, g, dy, dres),
    )


def _tail(h2, zg, pp, tgt, fn):
    S, D = h2.shape
    ts = _div(S, 256, SUBLANES)

    def body(h_ref, z_ref, p_ref, t_ref, fn_ref, dh_ref, dz_ref, dp_ref, loss_ref, dfn_ref):
        i = pl.program_id(0)
        sg = _sigmoid(z_ref[...])
        ppv = p_ref[...]
        h3 = h_ref[...] + sg * ppv
        r = lax.rsqrt(jnp.mean(h3 * h3, axis=-1, keepdims=True) + EPS)
        hn = h3 * r
        fnv = fn_ref[...]
        e = hn * fnv - t_ref[...]
        lpart = 0.5 * jnp.sum(jnp.mean(e * e, axis=-1, keepdims=True), axis=0, keepdims=True)
        dy = e * (1.0 / D)
        dpart = jnp.sum(dy * hn, axis=0, keepdims=True)

        @pl.when(i == 0)
        def _():
            loss_ref[...] = jnp.broadcast_to(lpart, loss_ref.shape)
            dfn_ref[...] = dpart

        @pl.when(i > 0)
        def _():
            loss_ref[...] += jnp.broadcast_to(lpart, loss_ref.shape)
            dfn_ref[...] += dpart

        t = dy * fnv
        dh3 = r * (t - hn * jnp.mean(t * hn, axis=-1, keepdims=True))
        dh_ref[...] = dh3
        dz_ref[...] = (dh3 * ppv * sg * (1.0 - sg)).astype(bf16)
        dp_ref[...] = (dh3 * sg).astype(bf16)

    row = pl.BlockSpec((ts, D), lambda i: (i, 0))
    vec = pl.BlockSpec((1, D), lambda i: (0, 0))
    one = pl.BlockSpec((1, LANES), lambda i: (0, 0))
    return pl.pallas_call(
        body,
        name="ple_final_loss",
        grid=(S // ts,),
        in_specs=[row, row, row, row, vec],
        out_specs=[row, row, row, one, vec],
        out_shape=[
            jax.ShapeDtypeStruct((S, D), f32),
            jax.ShapeDtypeStruct((S, D), bf16),
            jax.ShapeDtypeStruct((S, D), bf16),
            jax.ShapeDtypeStruct((1, LANES), f32),
            jax.ShapeDtypeStruct((1, D), f32),
        ],
        compiler_params=pltpu.CompilerParams(dimension_semantics=("arbitrary",)),
    )(h2, zg, pp, tgt, fn)


def _shift_down(v, halo, k):
    r = pltpu.roll(v, k, 0)
    hr = pltpu.roll(halo, k, 0)
    row = lax.broadcasted_iota(jnp.int32, hr.shape, 0)
    top = jnp.where(row < k, hr, r[0:SUBLANES])
    return top if v.shape[0] == SUBLANES else jnp.concatenate([top, r[SUBLANES:]], axis=0)


def _shift_up(v, halo, k):
    n = v.shape[0]
    r = pltpu.roll(v, n - k, 0)
    hr = pltpu.roll(halo, SUBLANES - k, 0)
    row = lax.broadcasted_iota(jnp.int32, hr.shape, 0)
    bot = jnp.where(row >= SUBLANES - k, hr, r[n - SUBLANES:])
    return jnp.concatenate([r[: n - SUBLANES], bot], axis=0)


def _conv_specs(S, F, tc, ts):
    nj = F // tc
    rows8 = ts // SUBLANES
    main = pl.BlockSpec((ts, 2 * tc), lambda j, i: (i, j))
    prev = pl.BlockSpec((SUBLANES, 2 * tc), lambda j, i: (jnp.maximum(i * rows8 - 1, 0), j))
    nxt = pl.BlockSpec((SUBLANES, 2 * tc), lambda j, i: (jnp.minimum((i + 1) * rows8, S // SUBLANES - 1), j))
    wg = pl.BlockSpec((3, tc), lambda j, i: (0, j))
    wv = pl.BlockSpec((3, tc), lambda j, i: (0, nj + j))
    bg = pl.BlockSpec((1, tc), lambda j, i: (0, j))
    bv = pl.BlockSpec((1, tc), lambda j, i: (0, nj + j))
    half = pl.BlockSpec((ts, tc), lambda j, i: (i, j))
    return nj, main, prev, nxt, wg, wv, bg, bv, half


def _conv_pre(u_ref, h_ref, wg_ref, wv_ref, bg_ref, bv_ref, tc):
    i = pl.program_id(1)
    u = u_ref[...]
    halo = jnp.where(i == 0, 0.0, h_ref[...])
    u1 = _shift_down(u, halo, 1)
    u2 = _shift_down(u, halo, 2)
    w = jnp.concatenate([wg_ref[...], wv_ref[...]], axis=1)
    b = jnp.concatenate([bg_ref[...], bv_ref[...]], axis=1)
    uc = b + w[0:1] * u2 + w[1:2] * u1 + w[2:3] * u
    return u, u1, u2, uc[:, :tc], uc[:, tc:]


def _convact_fwd(u, cw, cb, tc, rider=None):
    S, F2 = u.shape
    F = F2 // 2
    ts = _div(S, 512, SUBLANES)
    nj, main, prev, nxt, wg, wv, bg, bv, half = _conv_specs(S, F, tc, ts)

    def body(u_ref, h_ref, wg_ref, wv_ref, bg_ref, bv_ref, m_ref):
        _, _, _, g, v = _conv_pre(u_ref, h_ref, wg_ref, wv_ref, bg_ref, bv_ref, tc)
        m_ref[...] = (g * _sigmoid(g) * v).astype(bf16)

    outs = _grid_call(
        body,
        rider,
        name="convact_fwd",
        grid=(nj, S // ts),
        in_specs=[main, prev, wg, wv, bg, bv],
        out_specs=[half],
        out_shape=[jax.ShapeDtypeStruct((S, F), bf16)],
        scratch_shapes=[],
        args=(u, u, cw, cw, cb, cb),
    )
    return outs[0] if rider is None else outs


def _convact_bwd(u, dm, cw, cb, tc, rider=None):
    S, F2 = u.shape
    F = F2 // 2
    ts = _div(S, 512, SUBLANES)
    nj, main, prev, nxt, wg, wv, bg, bv, half = _conv_specs(S, F, tc, ts)
    ni = S // ts
    half_nxt = pl.BlockSpec((SUBLANES, tc), lambda j, i: (jnp.minimum((i + 1) * (ts // SUBLANES), S // SUBLANES - 1), j))

    def act_bwd(g, v, d):
        sg = _sigmoid(g)
        return jnp.concatenate([d * v * sg * (1.0 + g * (1.0 - sg)), d * g * sg], axis=1)

    def body(u_ref, h_ref, n_ref, dm_ref, dmn_ref, wg_ref, wv_ref, bg_ref, bv_ref, du_ref, dwg_ref, dwv_ref, dbg_ref, dbv_ref):
        i = pl.program_id(1)
        u0, u1, u2, g, v = _conv_pre(u_ref, h_ref, wg_ref, wv_ref, bg_ref, bv_ref, tc)
        duc = act_bwd(g, v, dm_ref[...])
        w = jnp.concatenate([wg_ref[...], wv_ref[...]], axis=1)
        b = jnp.concatenate([bg_ref[...], bv_ref[...]], axis=1)
        un = n_ref[...]
        tail = u0[ts - SUBLANES:]
        ucn = b + w[0:1] * _shift_down(un, tail, 2) + w[1:2] * _shift_down(un, tail, 1) + w[2:3] * un
        ducn = jnp.where(i == ni - 1, 0.0, act_bwd(ucn[:, :tc], ucn[:, tc:], dmn_ref[...]))
        du_ref[...] = (w[2:3] * duc + w[1:2] * _shift_up(duc, ducn, 1) + w[0:1] * _shift_up(duc, ducn, 2)).astype(bf16)
        db = jnp.sum(duc, axis=0, keepdims=True)
        dw = jnp.concatenate(
            [jnp.sum(duc * u2, axis=0, keepdims=True), jnp.sum(duc * u1, axis=0, keepdims=True), jnp.sum(duc * u0, axis=0, keepdims=True)],
            axis=0,
        )

        @pl.when(i == 0)
        def _():
            dwg_ref[...] = dw[:, :tc]
            dwv_ref[...] = dw[:, tc:]
            dbg_ref[...] = db[:, :tc]
            dbv_ref[...] = db[:, tc:]

        @pl.when(i > 0)
        def _():
            dwg_ref[...] += dw[:, :tc]
            dwv_ref[...] += dw[:, tc:]
            dbg_ref[...] += db[:, :tc]
            dbv_ref[...] += db[:, tc:]

    w_out = pl.BlockSpec((3, tc), lambda j, i: (0, j))
    b_out = pl.BlockSpec((1, tc), lambda j, i: (0, j))
    return _grid_call(
        body,
        rider,
        name="convact_bwd",
        grid=(nj, ni),
        in_specs=[main, prev, nxt, half, half_nxt, wg, wv, bg, bv],
        out_specs=[main, w_out, w_out, b_out, b_out],
        out_shape=[
            jax.ShapeDtypeStruct((S, F2), bf16),
            jax.ShapeDtypeStruct((3, F), f32),
            jax.ShapeDtypeStruct((3, F), f32),
            jax.ShapeDtypeStruct((1, F), f32),
            jax.ShapeDtypeStruct((1, F), f32),
        ],
        scratch_shapes=[],
        args=(u, u, u, dm, dm, cw, cw, cb, cb),
    )


def _hg_gates(qp, fp, lbl):
    lb = _sigmoid(lbl[0:1] - lbl[1:2])
    sig = _sigmoid(fp)
    sigm = _sigmoid(-fp)
    f = lb + (1.0 - lb) * sig
    k = (1.0 - lb) * sigm
    sq = _sigmoid(qp)
    qf = qp * sq
    row = lax.broadcasted_iota(jnp.int32, (CHUNK, CHUNK), 0)
    col = lax.broadcasted_iota(jnp.int32, (CHUNK, CHUNK), 1)
    b = _dot((row >= col).astype(f32), jnp.log(f), NN, HIGHEST)
    return lb, sig, sigm, f, k, sq, qf, b


def _hg_block(qf, k, b, I):
    r0, n = I * HG_SUB, (I + 1) * HG_SUB
    base = b[r0 - 1:r0] if I > 0 else jnp.zeros_like(b[0:1])
    eq = jnp.exp(b[r0:n] - base)
    ek = jnp.exp(base - b[0:n])
    qt16 = (qf[r0:n] * eq).astype(bf16)
    kt16 = (k[0:n] * ek).astype(bf16)
    row = lax.broadcasted_iota(jnp.int32, (HG_SUB, n), 0) + r0
    col = lax.broadcasted_iota(jnp.int32, (HG_SUB, n), 1)
    return r0, n, eq, ek, qt16, kt16, col <= row


def _per_head(v, hp, fn):
    return jnp.concatenate([fn(v[:, p * HEAD_DIM:(p + 1) * HEAD_DIM]) for p in range(hp)], axis=1)


def _head_mean(v, hp):
    return _per_head(v, hp, lambda t: jnp.broadcast_to(jnp.mean(t, axis=-1, keepdims=True), t.shape))


def _pad_rows(v, rows):
    return v if v.shape[0] == rows else jnp.concatenate([v, jnp.zeros((rows - v.shape[0], v.shape[1]), v.dtype)], axis=0)


def _hgrn_specs(S, W, hp, reverse):
    nc = S // CHUNK
    ngrp = W // (hp * HEAD_DIM)
    cw = hp * HEAD_DIM
    cidx = (lambda c: nc - 1 - c) if reverse else (lambda c: c)

    def proj(off):
        return pl.BlockSpec((CHUNK, cw), lambda h, c: (cidx(c), off * ngrp + h))

    act = pl.BlockSpec((CHUNK, cw), lambda h, c: (cidx(c), h))
    lbl = pl.BlockSpec((2, cw), lambda h, c: (0, h))
    gn = pl.BlockSpec((1, HEAD_DIM), lambda h, c: (0, 0))
    st = pl.BlockSpec((hp, 1, HEAD_DIM, HEAD_DIM), lambda h, c: (h, cidx(c), 0, 0))
    return nc, ngrp, proj, act, lbl, gn, st


def _grid_call(body, rider, *, name, grid, in_specs, out_specs, out_shape, scratch_shapes, args):
    params = pltpu.CompilerParams(dimension_semantics=("arbitrary",) * len(grid))
    if rider is None:
        return pl.pallas_call(body, name=name, grid=grid, in_specs=in_specs, out_specs=out_specs, out_shape=out_shape,
                              scratch_shapes=scratch_shapes, compiler_params=params)(*args)
    n_in, n_out, n_scr = len(in_specs), len(out_specs), len(scratch_shapes)
    ri, ro = len(rider.ins), len(rider.out_shapes)

    def wrapped(*refs):
        a, b = n_in, n_in + ri
        c, d = b + n_out, b + n_out + ro
        e = d + n_scr
        ids = [pl.program_id(t) for t in range(len(grid))]
        is_first = functools.reduce(jnp.logical_and, [i == 0 for i in ids])
        is_last = functools.reduce(jnp.logical_and, [i == g - 1 for i, g in zip(ids, grid)])

        @pl.when(is_first)
        def _():
            rider.first(refs[a:b], refs[c:d], refs[e:])

        body(*refs[:a], *refs[b:c], *refs[d:e])

        @pl.when(is_last)
        def _():
            rider.last(refs[a:b], refs[c:d], refs[e:])

    return pl.pallas_call(
        wrapped,
        name=name,
        grid=grid,
        in_specs=list(in_specs) + [_hbm()] * ri,
        out_specs=list(out_specs) + [_hbm()] * ro,
        out_shape=list(out_shape) + list(rider.out_shapes),
        input_output_aliases={n_in + i: n_out + o for i, o in rider.aliases.items()},
        scratch_shapes=list(scratch_shapes) + list(rider.sems),
        compiler_params=params,
    )(*args, *rider.ins)


def _hgrn_fwd(proj, lb_logits, hg_norm, W, rider=None):
    S = proj.shape[0]
    hp = min(HG_HEADS_PER_STEP, W // HEAD_DIM)
    nc, ngrp, pspec, act, lbl_spec, gn_spec, st_spec = _hgrn_specs(S, W, hp, False)
    nb = CHUNK // HG_SUB

    def body(q_ref, f_ref, i_ref, g_ref, lbl_ref, gn_ref, y_ref, o_ref, st_ref, state):
        c = pl.program_id(1)

        @pl.when(c == 0)
        def _():
            state[...] = jnp.zeros_like(state)

        _, _, _, _, k, _, qf, b = _hg_gates(q_ref[...], f_ref[...], lbl_ref[...])
        v16 = i_ref[...].astype(bf16)
        a16 = (qf * jnp.exp(b)).astype(bf16)
        bl = b[CHUNK - 1:CHUNK]
        kd16 = (k * jnp.exp(bl - b)).astype(bf16)
        ebl = jnp.exp(bl)
        blocks = [_hg_block(qf, k, b, I) for I in range(nb)]
        heads = [slice(p * HEAD_DIM, (p + 1) * HEAD_DIM) for p in range(hp)]
        st_prev = state[...]
        st_ref[:, 0] = st_prev
        inter = [_dot(a16[:, sl], st_prev[p].astype(bf16), NT) for p, sl in enumerate(heads)]
        scores = [[_dot(qt16[:, sl], kt16[:, sl], NT) for sl in heads] for _, _, _, _, qt16, kt16, _ in blocks]
        intra = [[_dot(jnp.where(blk[6], sc, 0.0).astype(bf16), v16[0:blk[1], sl], NN) for sc, sl in zip(scs, heads)]
                 for blk, scs in zip(blocks, scores)]
        st_new = [st_prev[p] * ebl[:, sl] + _dot(v16[:, sl], kd16[:, sl], TN) for p, sl in enumerate(heads)]
        state[...] = jnp.stack(st_new)
        o = jnp.concatenate([inter[p] + jnp.concatenate([rows[p] for rows in intra], axis=0) for p in range(hp)], axis=1)
        o_ref[...] = o
        r = lax.rsqrt(_head_mean(o * o, hp) + EPS)
        gp = g_ref[...]
        y_ref[...] = (o * r * jnp.tile(gn_ref[...], (1, hp)) * (gp * _sigmoid(gp))).astype(bf16)

    H = W // HEAD_DIM
    return _grid_call(
        body,
        rider,
        name="hgrn_fwd",
        grid=(ngrp, nc),
        in_specs=[pspec(0), pspec(1), pspec(2), pspec(3), lbl_spec, gn_spec],
        out_specs=[act, act, st_spec],
        out_shape=[
            jax.ShapeDtypeStruct((S, W), bf16),
            jax.ShapeDtypeStruct((S, W), f32),
            jax.ShapeDtypeStruct((H, nc, HEAD_DIM, HEAD_DIM), f32),
        ],
        scratch_shapes=[pltpu.VMEM((hp, HEAD_DIM, HEAD_DIM), f32)],
        args=(proj, proj, proj, proj, lb_logits, hg_norm),
    )


def _hgrn_bwd(proj, o_hg, st, dycat, lb_logits, hg_norm, W, rider=None):
    S = proj.shape[0]
    hp = min(HG_HEADS_PER_STEP, W // HEAD_DIM)
    assert hp * HEAD_DIM == W, "the four gradients share one output block: all heads in one grid step"
    nc, ngrp, pspec, act, lbl_spec, gn_spec, st_spec = _hgrn_specs(S, W, hp, True)
    nb = CHUNK // HG_SUB

    def body(q_ref, f_ref, i_ref, g_ref, o_ref, st_ref, dy_ref, lbl_ref, gn_ref,
             d4_ref, dlbl_ref, dgn_ref, dstate):
        cw_ = hp * HEAD_DIM
        dq_ref, df_ref, di_ref, dg_ref = (d4_ref.at[:, pl.ds(t * cw_, cw_)] for t in range(4))
        h = pl.program_id(0)
        c = pl.program_id(1)

        @pl.when(c == 0)
        def _():
            dstate[...] = jnp.zeros_like(dstate)
            dlbl_ref[...] = jnp.zeros_like(dlbl_ref)

        @pl.when((c == 0) & (h == 0))
        def _():
            dgn_ref[...] = jnp.zeros_like(dgn_ref)

        row = lax.broadcasted_iota(jnp.int32, (CHUNK, CHUNK), 0)
        col = lax.broadcasted_iota(jnp.int32, (CHUNK, CHUNK), 1)
        upper = (col >= row).astype(f32)
        last_row = lax.broadcasted_iota(jnp.int32, (CHUNK, hp * HEAD_DIM), 0) == CHUNK - 1

        qp, fp = q_ref[...], f_ref[...]
        lb, sig, sigm, f, k, sq, qf, b = _hg_gates(qp, fp, lbl_ref[...])
        v16 = i_ref[...].astype(bf16)
        gnv = jnp.tile(gn_ref[...], (1, hp))
        gp = g_ref[...]
        sgg = _sigmoid(gp)
        gate = gp * sgg
        o = o_ref[...]
        r = lax.rsqrt(_head_mean(o * o, hp) + EPS)
        on = o * r
        dy = dy_ref[...]
        dg_ref[...] = (dy * on * gnv * (sgg * (1.0 + gp * (1.0 - sgg)))).astype(bf16)
        dgn_wide = jnp.sum(dy * on * gate, axis=0, keepdims=True)
        dgn_ref[...] += functools.reduce(jnp.add, [dgn_wide[:, p * HEAD_DIM:(p + 1) * HEAD_DIM] for p in range(hp)])
        don = dy * gnv * gate
        do16 = (r * (don - on * _head_mean(don * on, hp))).astype(bf16)
        eb = jnp.exp(b)
        A = qf * eb
        a16 = A.astype(bf16)
        bl = b[CHUNK - 1:CHUNK]
        ebl = jnp.exp(bl)
        ekd = jnp.exp(bl - b)
        kd = k * ekd
        kd16 = kd.astype(bf16)
        blocks = [_hg_block(qf, k, b, I) for I in range(nb)]

        heads = [slice(p * HEAD_DIM, (p + 1) * HEAD_DIM) for p in range(hp)]
        st_prev = st_ref[:, 0]
        dst_new = dstate[...]
        st16, dst16 = st_prev.astype(bf16), dst_new.astype(bf16)
        dA_h = [_dot(do16[:, sl], st16[p], NN) for p, sl in enumerate(heads)]
        dkd_h = [_dot(v16[:, sl], dst16[p], NN) for p, sl in enumerate(heads)]
        dv_h = [_dot(kd16[:, sl], dst16[p], NT) for p, sl in enumerate(heads)]
        dstate[...] = jnp.stack([dst_new[p] * ebl[:, sl] + _dot(do16[:, sl], a16[:, sl], TN) for p, sl in enumerate(heads)])
        dbl_h = [jnp.sum(dst_new[p] * st_prev[p], axis=0, keepdims=True) for p in range(hp)]
        sc_h = [[jnp.where(mask, _dot(qt16[:, sl], kt16[:, sl], NT), 0.0).astype(bf16) for sl in heads]
                for _, _, _, _, qt16, kt16, mask in blocks]
        dsc_h = [[jnp.where(mask, _dot(do16[r0:n, sl], v16[0:n, sl], NT), 0.0).astype(bf16) for sl in heads]
                 for r0, n, _, _, _, _, mask in blocks]
        dvi_h = [[_dot(sc, do16[blk[0]:blk[1], sl], TN) for sc, sl in zip(scs, heads)] for blk, scs in zip(blocks, sc_h)]
        dqt_h = [[_dot(dsc, blk[5][:, sl], NN) for dsc, sl in zip(dscs, heads)] for blk, dscs in zip(blocks, dsc_h)]
        dkt_h = [[_dot(dsc, blk[4][:, sl], TN) for dsc, sl in zip(dscs, heads)] for blk, dscs in zip(blocks, dsc_h)]
        dv_h = [functools.reduce(jnp.add, [dv_h[p]] + [_pad_rows(rows[p], CHUNK) for rows in dvi_h]) for p in range(hp)]

        dA, dkd = jnp.concatenate(dA_h, axis=1), jnp.concatenate(dkd_h, axis=1)
        dbl = jnp.concatenate(dbl_h, axis=1) * ebl + jnp.sum(dkd * kd, axis=0, keepdims=True)
        db = dA * A - dkd * kd + jnp.where(last_row, dbl, 0.0)
        dk = dkd * ekd
        dq_rows, db_rows = [], []
        for I, (r0, n, eq, ek, qt16, kt16, _) in enumerate(blocks):
            dqt, dkt = jnp.concatenate(dqt_h[I], axis=1), jnp.concatenate(dkt_h[I], axis=1)
            dq_rows.append(dqt * eq)
            dk = dk + _pad_rows(dkt * ek, CHUNK)
            db_rows.append(dqt * qt16.astype(f32))
            db = db - _pad_rows(dkt * kt16.astype(f32), CHUNK)
        dqf = dA * eb + jnp.concatenate(dq_rows, axis=0)
        db = db + jnp.concatenate(db_rows, axis=0)
        dlogf = _dot(upper, db, NN, HIGHEST)
        dfg = dlogf / f
        df_ref[...] = ((1.0 - lb) * sig * sigm * (dfg - dk)).astype(bf16)
        dlb = jnp.sum(sigm * (dfg - dk), axis=0, keepdims=True)
        dl0 = dlb * lb * (1.0 - lb)
        dlbl_ref[...] += jnp.concatenate([dl0, -dl0], axis=0)
        dq_ref[...] = (dqf * (sq * (1.0 + qp * (1.0 - sq)))).astype(bf16)
        di_ref[...] = jnp.concatenate(dv_h, axis=1).astype(bf16)

    cw = hp * HEAD_DIM
    dy_spec = pl.BlockSpec((CHUNK, cw), lambda h, c: (nc - 1 - c, h))
    return _grid_call(
        body,
        rider,
        name="hgrn_bwd",
        grid=(ngrp, nc),
        in_specs=[pspec(0), pspec(1), pspec(2), pspec(3), act, st_spec, dy_spec, lbl_spec, gn_spec],
        out_specs=[pl.BlockSpec((CHUNK, 4 * W), lambda h, c: (nc - 1 - c, 0)), lbl_spec, gn_spec],
        out_shape=[jax.ShapeDtypeStruct((S, 4 * W), bf16), jax.ShapeDtypeStruct((2, W), f32), jax.ShapeDtypeStruct((1, HEAD_DIM), f32)],
        scratch_shapes=[pltpu.VMEM((hp, HEAD_DIM, HEAD_DIM), f32)],
        args=(proj, proj, proj, proj, o_hg, st, dycat, lb_logits, hg_norm),
    )


def _att_dims(S):
    qt = ATT_Q_ROWS if S % ATT_Q_ROWS == 0 else CHUNK
    pad = LEFT_CHUNKS * CHUNK
    kb = pad + qt
    return qt, pad, kb, kb + qt


def _kv_padded(proj, W, pad):
    S = proj.shape[0]
    ts = _div(pad, 512, 2 * SUBLANES)
    assert S % ts == 0 and pad % ts == 0
    npad = pad // ts

    def body(k_ref, v_ref, ko_ref, vo_ref):
        i = pl.program_id(0)

        @pl.when(i < npad)
        def _():
            ko_ref[...] = jnp.zeros_like(ko_ref)
            vo_ref[...] = jnp.zeros_like(vo_ref)

        @pl.when(i >= npad)
        def _():
            ko_ref[...] = k_ref[...].astype(bf16)
            vo_ref[...] = v_ref[...].astype(bf16)

    out = pl.BlockSpec((ts, W), lambda i: (i, 0))
    return pl.pallas_call(
        body,
        name="kv_padded",
        grid=((S + pad) // ts,),
        in_specs=[pl.BlockSpec((ts, W), lambda i: (jnp.maximum(i - npad, 0), 5)), pl.BlockSpec((ts, W), lambda i: (jnp.maximum(i - npad, 0), 6))],
        out_specs=[out, out],
        out_shape=[jax.ShapeDtypeStruct((S + pad, W), bf16)] * 2,
        compiler_params=pltpu.CompilerParams(dimension_semantics=("parallel",)),
    )(proj, proj)


def _att_probs(qk, bias, start, qt, pad, kb):
    s = qk * (HEAD_DIM ** -0.5) + bias
    row = lax.broadcasted_iota(jnp.int32, (qt, kb), 0)
    col = lax.broadcasted_iota(jnp.int32, (qt, kb), 1)
    lo = jnp.bitwise_and(row, -CHUNK)
    ok = (col >= lo) & (col < lo + pad + CHUNK) & (col + start >= pad)
    s = jnp.where(ok, s, NEG)
    e = jnp.exp(s - jnp.max(s, axis=-1, keepdims=True))
    return e / jnp.sum(e, axis=-1, keepdims=True)


def _att_bias(ext_row, qt, kb, ne):
    e = jnp.broadcast_to(ext_row, (qt, ne))
    return pltpu.roll(e, ne - qt + 1, 1, stride=1, stride_axis=0)[:, :kb]


def _att_specs(S, H, ah, qt, pad, ne):
    cw = ah * HEAD_DIM
    q_spec = pl.BlockSpec((qt, cw), lambda h, g: (g, 4 * H // ah + h))
    kv_spec = pl.BlockSpec((S + pad, cw), lambda h, g: (0, h))
    ext_spec = pl.BlockSpec((ah, 1, ne), lambda h, g: (h, 0, 0))
    row_spec = pl.BlockSpec((qt, cw), lambda h, g: (g, h))
    return cw, q_spec, kv_spec, ext_spec, row_spec


def _attn_fwd(proj, kp, vp, ext, W, rider=None):
    S = proj.shape[0]
    H = W // HEAD_DIM
    ah = min(ATT_HEADS_PER_STEP, H)
    qt, pad, kb, ne = _att_dims(S)
    cw, q_spec, kv_spec, ext_spec, row_spec = _att_specs(S, H, ah, qt, pad, ne)
    heads = [slice(a * HEAD_DIM, (a + 1) * HEAD_DIM) for a in range(ah)]

    def body(q_ref, k_ref, v_ref, ext_ref, o_ref, bias):
        g = pl.program_id(1)

        @pl.when(g == 0)
        def _():
            for a in range(ah):
                bias[a] = _att_bias(ext_ref[a], qt, kb, ne)

        start = pl.multiple_of(g * qt, qt)
        q16 = q_ref[...].astype(bf16)
        kb16 = k_ref[pl.ds(start, kb), :]
        vb16 = v_ref[pl.ds(start, kb), :]
        qk = [_dot(q16[:, sl], kb16[:, sl], NT) for sl in heads]
        pn = [_att_probs(qk[a], bias[a], start, qt, pad, kb).astype(bf16) for a in range(ah)]
        o_ref[...] = jnp.concatenate([_dot(pn[a], vb16[:, sl], NN) for a, sl in enumerate(heads)], axis=1).astype(bf16)

    outs = _grid_call(
        body,
        rider,
        name="attn_fwd",
        grid=(H // ah, S // qt),
        in_specs=[q_spec, kv_spec, kv_spec, ext_spec],
        out_specs=[row_spec],
        out_shape=[jax.ShapeDtypeStruct((S, W), bf16)],
        scratch_shapes=[pltpu.VMEM((ah, qt, kb), f32)],
        args=(proj, kp, vp, ext),
    )
    return outs[0] if rider is None else outs


def _attn_bwd(proj, kp, vp, ext, dycat, W, rider=None):
    S = proj.shape[0]
    H = W // HEAD_DIM
    ah = min(ATT_HEADS_PER_STEP, H)
    qt, pad, kb, ne = _att_dims(S)
    cw, q_spec, kv_spec, ext_spec, row_spec = _att_specs(S, H, ah, qt, pad, ne)
    heads = [slice(a * HEAD_DIM, (a + 1) * HEAD_DIM) for a in range(ah)]
    scale = HEAD_DIM ** -0.5
    ng = S // qt

    def body(q_ref, k_ref, v_ref, ext_ref, do_ref, dq_ref, dko_ref, dvo_ref, db_ref, bias, dk_ref, dv_ref):
        g = pl.program_id(1)

        @pl.when(g == 0)
        def _():
            for a in range(ah):
                bias[a] = _att_bias(ext_ref[a], qt, kb, ne)
            dk_ref[...] = jnp.zeros_like(dk_ref)
            dv_ref[...] = jnp.zeros_like(dv_ref)
            db_ref[...] = jnp.zeros_like(db_ref)

        start = pl.multiple_of(g * qt, qt)
        q16 = q_ref[...].astype(bf16)
        kb16 = k_ref[pl.ds(start, kb), :]
        vb16 = v_ref[pl.ds(start, kb), :]
        do16 = do_ref[...].astype(bf16)
        qk = [_dot(q16[:, sl], kb16[:, sl], NT) for sl in heads]
        dpn = [_dot(do16[:, sl], vb16[:, sl], NT) for sl in heads]
        pn = [_att_probs(qk[a], bias[a], start, qt, pad, kb) for a in range(ah)]
        ds = [pn[a] * (dpn[a] - jnp.sum(dpn[a] * pn[a], axis=-1, keepdims=True)) for a in range(ah)]
        ds16 = [d.astype(bf16) for d in ds]
        dv_ref[pl.ds(start, kb), :] += jnp.concatenate([_dot(pn[a].astype(bf16), do16[:, sl], TN) for a, sl in enumerate(heads)], axis=1)
        dq_ref[...] = (jnp.concatenate([_dot(ds16[a], kb16[:, sl], NN) for a, sl in enumerate(heads)], axis=1) * scale).astype(bf16)
        dk_ref[pl.ds(start, kb), :] += jnp.concatenate([_dot(ds16[a], q16[:, sl], TN) for a, sl in enumerate(heads)], axis=1) * scale
        for a in range(ah):
            db_ref[a] += ds[a]

        @pl.when(g == ng - 1)
        def _():
            dko_ref[...] = dk_ref[pl.ds(pad, S), :].astype(bf16)
            dvo_ref[...] = dv_ref[pl.ds(pad, S), :].astype(bf16)

    kv_out = pl.BlockSpec((S, cw), lambda h, g: (0, h))
    return _grid_call(
        body,
        rider,
        name="attn_bwd",
        grid=(H // ah, S // qt),
        in_specs=[q_spec, kv_spec, kv_spec, ext_spec,
                  pl.BlockSpec((qt, cw), lambda h, g: (g, H // ah + h))],
        out_specs=[row_spec, kv_out, kv_out, pl.BlockSpec((ah, qt, kb), lambda h, g: (h, 0, 0))],
        out_shape=[
            jax.ShapeDtypeStruct((S, W), bf16),
            jax.ShapeDtypeStruct((S, W), bf16),
            jax.ShapeDtypeStruct((S, W), bf16),
            jax.ShapeDtypeStruct((H, qt, kb), f32),
        ],
        scratch_shapes=[pltpu.VMEM((ah, qt, kb), f32), pltpu.VMEM((S + pad, cw), f32), pltpu.VMEM((S + pad, cw), f32)],
        args=(proj, kp, vp, ext, dycat),
    )


def _bias_onehot(qt, ne, nrel_pad):
    m = lax.broadcasted_iota(jnp.int32, (nrel_pad, ne), 1)
    r = lax.broadcasted_iota(jnp.int32, (nrel_pad, ne), 0)
    rel = LEFT_CHUNKS * CHUNK + qt - 1 - m
    hot = (r == jnp.clip(rel, -REL_CLIP, REL_CLIP) + REL_CLIP) & (m < ne - 1)
    return hot.astype(f32)


def _bias_ext(rel_bias_pad, onehot):
    H, _, nr = rel_bias_pad.shape
    ne = onehot.shape[1]

    def body(rb_ref, oh_ref, o_ref):
        o_ref[0] = _dot(rb_ref[0], oh_ref[...], NN, HIGHEST)

    return pl.pallas_call(
        body,
        name="bias_ext",
        grid=(H,),
        in_specs=[pl.BlockSpec((1, 1, nr), lambda h: (h, 0, 0)), pl.BlockSpec((nr, ne), lambda h: (0, 0))],
        out_specs=pl.BlockSpec((1, 1, ne), lambda h: (h, 0, 0)),
        out_shape=jax.ShapeDtypeStruct((H, 1, ne), f32),
        compiler_params=pltpu.CompilerParams(dimension_semantics=("parallel",)),
    )(rel_bias_pad, onehot)


def _bias_bwd(dbias, onehot):
    H, qt, kb = dbias.shape
    nr, ne = onehot.shape

    def body(d_ref, oh_ref, o_ref):
        row = lax.broadcasted_iota(jnp.int32, (qt, qt), 0)
        col = lax.broadcasted_iota(jnp.int32, (qt, qt), 1)
        flip = (row + col == qt - 1).astype(f32)
        x = _dot(flip, d_ref[0], NN, HIGHEST)
        x = jnp.concatenate([x, jnp.zeros((qt, ne - kb), f32)], axis=1)
        de = jnp.sum(pltpu.roll(x, 0, 1, stride=1, stride_axis=0), axis=0, keepdims=True)
        o_ref[0] = _dot(de, oh_ref[...], NT, HIGHEST)

    return pl.pallas_call(
        body,
        name="bias_bwd",
        grid=(H,),
        in_specs=[pl.BlockSpec((1, qt, kb), lambda h: (h, 0, 0)), pl.BlockSpec((nr, ne), lambda h: (0, 0))],
        out_specs=pl.BlockSpec((1, 1, nr), lambda h: (h, 0, 0)),
        out_shape=jax.ShapeDtypeStruct((H, 1, nr), f32),
        compiler_params=pltpu.CompilerParams(dimension_semantics=("parallel",)),
    )(dbias, onehot)


def _place():
    x, y, c = lax.axis_index("x"), lax.axis_index("y"), lax.axis_index("c")
    return x, y, c


def _flip(x, y, k):
    return (1 - x if k & 2 else x), (1 - y if k & 1 else y)


def _region(ref, shard_shape, axis, j, half):
    R, C = shard_shape
    if axis == 0:
        if half is None:
            return ref.at[pl.ds(j * R, R), :]
        return ref.at[pl.ds(j * R + half * (R // 2), R // 2), :]
    if half is None:
        return ref.at[:, pl.ds(j * C, C)]
    return ref.at[pl.ds(half * (R // 2), R // 2), pl.ds(j * C, C)]


class _Remote:
    def __init__(self, src, dst, send_sem, recv_sem, dev):
        self.src, self.dst, self.sems, self.dev = src, dst, (send_sem, recv_sem), dev

    def _copy(self, src, dst):
        return pltpu.make_async_remote_copy(src_ref=src, dst_ref=dst, send_sem=self.sems[0], recv_sem=self.sems[1],
                                            device_id=self.dev, device_id_type=MESH)

    def start(self):
        rows = self.src.shape[0]
        n = RDMA_PIECES if rows % (RDMA_PIECES * 2 * SUBLANES) == 0 else 1
        for p in range(n):
            part = pl.ds(p * (rows // n), rows // n)
            self._copy(self.src.at[part], self.dst.at[part]).start()

    def wait_recv(self):
        self._copy(self.src, self.dst).wait_recv()

    def wait_send(self):
        self._copy(self.src, self.dst).wait_send()


def _remote(src, dst, send_sem, recv_sem, dev):
    return _Remote(src, dst, send_sem, recv_sem, dev)


def _cast_place(shard, axis, where, *, name):
    R, C = shard.shape
    tr = _div(R, 256, 2 * SUBLANES)
    nr = R // tr
    full = (N_CHIPS * R, C) if axis == 0 else (R, N_CHIPS * C)
    omap = (lambda r, s: (s[0] * nr + r, 0)) if axis == 0 else (lambda r, s: (r, s[0]))

    def body(s_ref, x_ref, o_ref):
        o_ref[...] = x_ref[...].astype(bf16)

    return pl.pallas_call(
        body,
        name=name,
        grid_spec=pltpu.PrefetchScalarGridSpec(
            num_scalar_prefetch=1,
            grid=(nr,),
            in_specs=[pl.BlockSpec((tr, C), lambda r, s: (r, 0))],
            out_specs=pl.BlockSpec((tr, C), omap),
        ),
        out_shape=jax.ShapeDtypeStruct(full, bf16),
        compiler_params=pltpu.CompilerParams(dimension_semantics=("parallel",)),
    )(where, shard)


class _Rider(NamedTuple):
    ins: list
    out_shapes: list
    aliases: dict
    sems: list
    first: Callable
    last: Callable


def _run_rider(rider, *, name):
    ni, no = len(rider.ins), len(rider.out_shapes)

    def body(*refs):
        ins, outs, sems = refs[:ni], refs[ni:ni + no], refs[ni + no:]
        rider.first(ins, outs, sems)
        rider.last(ins, outs, sems)

    return pl.pallas_call(
        body,
        name=name,
        in_specs=[_hbm()] * ni,
        out_specs=[_hbm()] * no,
        out_shape=rider.out_shapes,
        input_output_aliases=rider.aliases,
        scratch_shapes=rider.sems,
    )(*rider.ins)


def _start_all(copies):
    def first(ins, outs, sems):
        for cp in copies(ins, outs, sems):
            cp.start()

    def last(ins, outs, sems):
        cps = copies(ins, outs, sems)
        for cp in cps:
            cp.wait_recv()
        for cp in cps:
            cp.wait_send()

    return first, last


def _relay_copy(buf, shard_shape, axis, send_sem, recv_sem):
    x, y, c = _place()
    first = c == 0
    sx, sy = jnp.where(first, 1 - x, x), jnp.where(first, y, 1 - y)
    dx, dy = jnp.where(first, x, 1 - x), jnp.where(first, 1 - y, y)
    part = _region(buf, shard_shape, axis, 2 * sx + sy, c)
    return _remote(part, part, send_sem, recv_sem, (dx, dy, c))


def _mm_while_gathering(a, full, shard_shape, where, *, tm, tn, name):
    S, K = a.shape
    R, Cs = shard_shape
    N = full.shape[1]
    tps, nj, ni = Cs // tn, N // tn, S // tm
    assert R == K and Cs % tn == 0 and S % tm == 0

    def block(j, chip):
        return jnp.bitwise_xor(chip, j // tps) * tps + j % tps

    def body(s_ref, a_ref, w_in, o_ref, w_ref, buf, fsem, send1, recv1, send2, recv2):
        j, i = pl.program_id(0), pl.program_id(1)
        x, y, c = _place()
        chip = 2 * x + y

        def ici(k):
            px, py = _flip(x, y, k)
            mine = _region(w_ref, shard_shape, 1, chip, c)
            return _remote(mine, mine, send1.at[k - 1], recv1.at[k - 1], (px, py, c))

        def handed(k, half):
            px, py = _flip(x, y, k)
            land = _region(w_ref, shard_shape, 1, 2 * px + py, half)
            return land, _remote(land, land, send2.at[k - 1], recv2.at[k - 1], (x, y, 1 - c))

        def tile(jj):
            col = pl.multiple_of(block(jj, chip) * tn, LANES)
            return pltpu.make_async_copy(w_ref.at[:, pl.ds(col, tn)], buf.at[jj % 2], fsem.at[jj % 2])

        def relay():
            return _relay_copy(w_ref, shard_shape, 1, send1.at[2], recv1.at[2])

        def landed(k):
            land, forward = handed(k, c)
            _remote(land, land, send1.at[k - 1], recv1.at[k - 1], (x, y, c)).wait_recv()
            forward.start()

        @pl.when((j == 0) & (i == 0))
        def _():
            for k in (1, 2):
                ici(k).start()
            tile(j).start()

        @pl.when(i == 0)
        def _():
            tile(j).wait()

        @pl.when((i == ni - 1) & (j + 1 < nj))
        def _():
            @pl.when(j + 1 == tps)
            def _():
                landed(1)
                landed(2)
                relay().start()
                handed(1, 1 - c)[1].wait_recv()

            @pl.when(j + 1 == 2 * tps)
            def _():
                handed(2, 1 - c)[1].wait_recv()

            @pl.when(j + 1 == 3 * tps)
            def _():
                landed(3)
                handed(3, 1 - c)[1].wait_recv()

            tile(j + 1).start()

        o_ref[...] = _dot(a_ref[...], buf[j % 2], NN)

        @pl.when((j == nj - 1) & (i == ni - 1))
        def _():
            ici(1).wait_send()
            ici(2).wait_send()
            relay().wait_send()
            for k in (1, 2, 3):
                handed(k, c)[1].wait_send()

    sems = pltpu.SemaphoreType.DMA((3,))
    spec = pltpu.PrefetchScalarGridSpec(
        num_scalar_prefetch=1,
        grid=(nj, ni),
        in_specs=[pl.BlockSpec((tm, K), lambda j, i, s: (i, 0)), _hbm()],
        out_specs=[pl.BlockSpec((tm, tn), lambda j, i, s: (i, block(j, s[0]))), _hbm()],
        scratch_shapes=[pltpu.VMEM((2, K, tn), full.dtype), pltpu.SemaphoreType.DMA((2,)), sems, sems, sems, sems],
    )
    return pl.pallas_call(
        body,
        name=name,
        grid_spec=spec,
        out_shape=[jax.ShapeDtypeStruct((S, N), f32), jax.ShapeDtypeStruct(full.shape, full.dtype)],
        input_output_aliases={2: 1},
        compiler_params=pltpu.CompilerParams(dimension_semantics=("arbitrary", "arbitrary")),
    )(where, a, full)


class _GatherJob(NamedTuple):
    buf: int
    shard_shape: tuple
    axis: int
    split: bool
    rels: tuple
    hand_over: bool
    relay: bool = False


def _gather_rider(bufs, jobs):
    def copies(ins, outs, sems):
        x, y, c = _place()
        cps = []
        for n, job in enumerate(jobs):
            if job.relay:
                cps.append(_relay_copy(outs[job.buf], job.shard_shape, job.axis, sems[0].at[3 * n + 2], sems[1].at[3 * n + 2]))
                continue
            half = c if job.split else None
            for k in job.rels:
                px, py = _flip(x, y, k)
                chip, dev = (2 * px + py, (x, y, 1 - c)) if job.hand_over else (2 * x + y, (px, py, c))
                cps.append(_remote(_region(ins[job.buf], job.shard_shape, job.axis, chip, half),
                                   _region(outs[job.buf], job.shard_shape, job.axis, chip, half),
                                   sems[0].at[3 * n + k - 1], sems[1].at[3 * n + k - 1], dev))
        return cps

    assert all(job.split or not job.hand_over for job in jobs)
    sems = pltpu.SemaphoreType.DMA((3 * len(jobs),))
    return _Rider(list(bufs), [jax.ShapeDtypeStruct(b.shape, b.dtype) for b in bufs], {m: m for m in range(len(bufs))},
                  [sems, sems], *_start_all(copies))


def _chips_rider(ts, rels=(1, 2, 3), landing=None):
    n = len(ts)

    def copies(ins, outs, sems):
        x, y, c = _place()
        cps = []
        for m in range(n):
            for k in rels:
                px, py = _flip(x, y, k)
                cps.append(_remote(ins[m].at[2 * px + py], outs[m].at[k - 1], sems[0].at[3 * m + k - 1], sems[1].at[3 * m + k - 1], (px, py, c)))
        return cps

    sems = pltpu.SemaphoreType.DMA((3 * n,))
    return _Rider(list(ts) + list(landing or []), [jax.ShapeDtypeStruct((3,) + t.shape[1:], t.dtype) for t in ts],
                  {n + m: m for m in range(n)} if landing else {}, [sems, sems], *_start_all(copies))


def _sibling_rider(dws, shard_shapes, axes):
    n = len(dws)

    def copies(ins, outs, sems):
        x, y, c = _place()
        return [_remote(_region(ins[m], shard_shapes[m], axes[m], j, 1 - c), outs[m].at[j],
                        sems[0].at[N_CHIPS * m + j], sems[1].at[N_CHIPS * m + j], (x, y, 1 - c))
                for m in range(n) for j in range(N_CHIPS)]

    sems = pltpu.SemaphoreType.DMA((N_CHIPS * n,))
    return _Rider(list(dws), [jax.ShapeDtypeStruct((N_CHIPS, s[0] // 2, s[1]), d.dtype) for s, d in zip(shard_shapes, dws)], {},
                  [sems, sems], *_start_all(copies))


def _join_rider(gs):
    n = len(gs)

    def copies(ins, outs, sems):
        x, y, c = _place()
        cps = []
        for m in range(n):
            hr = gs[m].shape[0] // 2
            cps.append(_remote(ins[m].at[pl.ds(c * hr, hr), :], outs[m].at[pl.ds(c * hr, hr), :], sems[0].at[m], sems[1].at[m], (x, y, 1 - c)))
        return cps

    sems = pltpu.SemaphoreType.DMA((n,))
    return _Rider(list(gs), [jax.ShapeDtypeStruct(g.shape, g.dtype) for g in gs], {m: m for m in range(n)}, [sems, sems], *_start_all(copies))


def _both(r1, r2):
    i1, o1, s1 = len(r1.ins), len(r1.out_shapes), len(r1.sems)
    aliases = dict(r1.aliases)
    aliases.update({i1 + i: o1 + o for i, o in r2.aliases.items()})

    def first(ins, outs, sems):
        r1.first(ins[:i1], outs[:o1], sems[:s1])
        r2.first(ins[i1:], outs[o1:], sems[s1:])

    def last(ins, outs, sems):
        r1.last(ins[:i1], outs[:o1], sems[:s1])
        r2.last(ins[i1:], outs[o1:], sems[s1:])

    return _Rider(r1.ins + r2.ins, r1.out_shapes + r2.out_shapes, aliases, r1.sems + r2.sems, first, last)


def _pair_sum(dw, got, shard_shape, axis, where, *, name):
    R, C = shard_shape
    hr = R // 2
    tr = _div(hr, 256, 2 * SUBLANES)
    nr = hr // tr
    dmap = (lambda j, r, s: ((2 * j + s[1]) * nr + r, 0)) if axis == 0 else (lambda j, r, s: (s[1] * nr + r, j))
    slot = pl.BlockSpec((1, tr, C), lambda j, r, s: (j, r, 0))

    def body(s_ref, d_ref, g_ref, o_ref):
        o_ref[0] = (d_ref[...].astype(f32) + g_ref[0].astype(f32)).astype(bf16)

    return pl.pallas_call(
        body,
        name=name,
        grid_spec=pltpu.PrefetchScalarGridSpec(
            num_scalar_prefetch=1,
            grid=(N_CHIPS, nr),
            in_specs=[pl.BlockSpec((tr, C), dmap), slot],
            out_specs=slot,
        ),
        out_shape=jax.ShapeDtypeStruct((N_CHIPS, hr, C), bf16),
        compiler_params=pltpu.CompilerParams(dimension_semantics=("parallel", "parallel")),
    )(where, dw, got)


def _chip_sum(pair, others, where, *, name):
    _, hr, C = pair.shape
    tr = _div(hr, 256, 2 * SUBLANES)
    nr = hr // tr

    def body(s_ref, p_ref, o_ref, g_ref):
        g_ref[...] = p_ref[0].astype(f32) + o_ref[0].astype(f32) + o_ref[1].astype(f32) + o_ref[2].astype(f32)

    return pl.pallas_call(
        body,
        name=name,
        grid_spec=pltpu.PrefetchScalarGridSpec(
            num_scalar_prefetch=1,
            grid=(nr,),
            in_specs=[pl.BlockSpec((1, tr, C), lambda r, s: (s[0], r, 0)), pl.BlockSpec((3, tr, C), lambda r, s: (0, r, 0))],
            out_specs=pl.BlockSpec((tr, C), lambda r, s: (s[1] * nr + r, 0)),
        ),
        out_shape=jax.ShapeDtypeStruct((2 * hr, C), f32),
        compiler_params=pltpu.CompilerParams(dimension_semantics=("parallel",)),
    )(where, pair, others)


def _allreduce_small(v):
    rows = v.shape[0]

    def body(v_ref, o_ref, gath, send, recv):
        x, y, c = _place()
        me = 4 * x + 2 * y + c
        gath[pl.ds(me, 1)] = v_ref[...][None]
        cps = []
        for k in range(1, N_DEV):
            peer = (1 - x if k & 4 else x, 1 - y if k & 2 else y, 1 - c if k & 1 else c)
            cp = _remote(v_ref, gath.at[me], send.at[k - 1], recv.at[k - 1], peer)
            cp.start()
            cps.append(cp)
        for cp in cps:
            cp.wait_recv()
        for cp in cps:
            cp.wait_send()
        acc = gath[0]
        for d in range(1, N_DEV):
            acc = acc + gath[d]
        o_ref[...] = acc

    return pl.pallas_call(
        body,
        name="small_allreduce",
        in_specs=[_vmem()],
        out_specs=_vmem(),
        out_shape=jax.ShapeDtypeStruct(v.shape, f32),
        scratch_shapes=[pltpu.VMEM((N_DEV, rows, LANES), f32), pltpu.SemaphoreType.DMA((N_DEV - 1,)), pltpu.SemaphoreType.DMA((N_DEV - 1,))],
    )(v)


def _adamw_math(w, g, m, v):
    m = ADAM_B1 * m + (1.0 - ADAM_B1) * g
    v = ADAM_B2 * v + (1.0 - ADAM_B2) * (g * g)
    m_hat = m / (1.0 - ADAM_B1 ** ADAM_STEP)
    v_hat = v / (1.0 - ADAM_B2 ** ADAM_STEP)
    delta = -ADAM_LR * (m_hat / (jnp.sqrt(v_hat) + ADAM_EPS) + ADAM_WD * w)
    return delta, m, v


def _adamw(w, g, m, v, *, name):
    R, C = w.shape
    tr = _div(R, 128, SUBLANES)

    def body(w_ref, g_ref, m_ref, v_ref, go_ref, d_ref, mo_ref, vo_ref):
        gv = g_ref[...]
        d, mn, vn = _adamw_math(w_ref[...], gv, m_ref[...], v_ref[...])
        go_ref[...] = gv
        d_ref[...] = d
        mo_ref[...] = mn
        vo_ref[...] = vn

    spec = pl.BlockSpec((tr, C), lambda r: (r, 0))
    return pl.pallas_call(
        body,
        name=name,
        grid=(R // tr,),
        in_specs=[spec] * 4,
        out_specs=[spec] * 4,
        out_shape=[jax.ShapeDtypeStruct((R, C), f32)] * 4,
        compiler_params=pltpu.CompilerParams(dimension_semantics=("parallel",)),
    )(w, g, m, v)


def _adamw_sparsecore(w, g, m, v, *, name):
    R, C = w.shape
    rows = R // SC_TILES
    cb = _div(C, 1408, LANES)
    assert R % (SC_TILES * SUBLANES) == 0 and C % cb == 0

    def body(w_hbm, g_hbm, m_hbm, v_hbm, go_hbm, d_hbm, mo_hbm, vo_hbm, wb, gb, mb, vb, db):
        tile = lax.axis_index("sc_tile") * 2 + lax.axis_index("sc_core")

        @pl.loop(0, rows, step=SUBLANES)
        def _(r):
            @pl.loop(0, C, step=cb)
            def _(c0):
                slab = (pl.ds(tile * rows + r, SUBLANES), pl.ds(c0, cb))
                pltpu.sync_copy(w_hbm.at[slab], wb)
                pltpu.sync_copy(g_hbm.at[slab], gb)
                pltpu.sync_copy(m_hbm.at[slab], mb)
                pltpu.sync_copy(v_hbm.at[slab], vb)

                @pl.loop(0, SUBLANES)
                def _(rr):
                    @pl.loop(0, cb, step=SC_LANES)
                    def _(cc):
                        s = (pl.ds(rr, 1), pl.ds(cc, SC_LANES))
                        d, mn, vn = _adamw_math(wb.at[s][...], gb.at[s][...], mb.at[s][...], vb.at[s][...])
                        db.at[s][...] = d
                        mb.at[s][...] = mn
                        vb.at[s][...] = vn

                pltpu.sync_copy(gb, go_hbm.at[slab])
                pltpu.sync_copy(db, d_hbm.at[slab])
                pltpu.sync_copy(mb, mo_hbm.at[slab])
                pltpu.sync_copy(vb, vo_hbm.at[slab])

    buf = pltpu.VMEM((SUBLANES, cb), f32)
    return pl.kernel(
        body,
        name=name,
        out_type=[jax.ShapeDtypeStruct((R, C), f32)] * 4,
        mesh=plsc.VectorSubcoreMesh(core_axis_name="sc_core", subcore_axis_name="sc_tile"),
        scratch_types=[buf] * 5,
    )(w, g, m, v)


def _pack(arrs):
    flat = jnp.concatenate([a.reshape(-1).astype(f32) for a in arrs])
    tile = SUBLANES * LANES
    total = -(-flat.shape[0] // tile) * tile
    return jnp.pad(flat, (0, total - flat.shape[0])).reshape(total // LANES, LANES)


def _unpack(buf, shapes):
    flat = buf.reshape(-1)
    out, off = [], 0
    for s in shapes:
        size = int(np.prod(s))
        out.append(flat[off:off + size].reshape(s))
        off += size
    return out


def kernel(x, p, norm_mix, w_in, lb_logits, hg_norm, rel_bias, w_out, norm_ffn, w_up, conv_w, conv_b, w_down, norm_ple, w_ple_gate, w_ple_proj, final_norm, loss_target, m_norm_mix, m_w_in, m_lb_logits, m_hg_norm, m_rel_bias, m_w_out, m_norm_ffn, m_w_up, m_conv_w, m_conv_b, m_w_down, m_norm_ple, m_w_ple_gate, m_w_ple_proj, m_final_norm, v_norm_mix, v_w_in, v_lb_logits, v_hg_norm, v_rel_bias, v_w_out, v_norm_ffn, v_w_up, v_conv_w, v_conv_b, v_w_down, v_norm_ple, v_w_ple_gate, v_w_ple_proj, v_final_norm):
    S, D = x.shape[1], x.shape[2]
    xv, pv, tgt = x[0], p[0, 0], loss_target[0]
    W = (w_in.shape[2] * N_CHIPS) // 7
    H = W // HEAD_DIM
    F = w_down.shape[1] * N_CHIPS
    tc = _div(F // 2, 1408, LANES)
    nj = F // tc
    jx, jy = lax.axis_index("x"), lax.axis_index("y")
    chip = 2 * jx + jy

    big = [w_in[0], w_out[0], w_up[0], w_down[0], w_ple_gate[0], w_ple_proj[0]]
    big_axes = [1, 0, 1, 0, 0, 1]
    where = jnp.stack([chip, lax.axis_index("c")]).astype(jnp.int32)
    cw_pad = jnp.pad(conv_w[0], ((0, SUBLANES - conv_w.shape[1]), (0, 0)))
    cw_mine = lax.dynamic_update_slice(jnp.zeros((N_CHIPS * SUBLANES, cw_pad.shape[1]), f32), cw_pad, (chip * SUBLANES, 0))
    placed = [_cast_place(b, ax, where, name=f"cast_place_{i}") for i, (b, ax) in enumerate(zip(big, big_axes))]
    shard_shapes = [b.shape for b in big]

    every = (1, 2, 3)

    def job(buf, i, rels, hand_over=False, relay=False):
        if i is None:
            return _GatherJob(buf, cw_pad.shape, 0, False, rels, hand_over)
        return _GatherJob(buf, shard_shapes[i], big_axes[i], True, rels, hand_over, relay)

    tm = _div(S, 1024, LANES)

    a1 = _rms_fwd(xv, norm_mix, name="rms_mix")
    cs_in = shard_shapes[0][1]
    tn_in = _div(cs_in, 1024, LANES)
    proj, W_in = _mm_while_gathering(a1, placed[0], shard_shapes[0], where, tm=tm, tn=tn_in, name="mm_in")
    qt, pad, kb, ne = _att_dims(S)
    nrel = rel_bias.shape[2]
    nrel_pad = -(-nrel // LANES) * LANES
    onehot = _bias_onehot(qt, ne, nrel_pad)
    rb_pad = jnp.pad(rel_bias[0], ((0, 0), (0, nrel_pad - nrel)))[:, None, :]
    ext = _bias_ext(rb_pad, onehot)
    kp, vp = _kv_padded(proj, W, pad)
    y_att, W_out_i, W_up_i = _attn_fwd(
        proj, kp, vp, ext, W, rider=_gather_rider([placed[1], placed[2]], [job(0, 1, every), job(1, 2, (1, 2))]))
    y_hg, o_hg, st, W_out, W_up_i = _hgrn_fwd(
        proj, lb_logits, hg_norm, W,
        rider=_gather_rider([W_out_i, W_up_i], [job(0, 1, every, True), job(1, 2, (1, 2), True), job(1, 2, (3,), relay=True)]))
    ycat = jnp.concatenate([y_hg, y_att], axis=1)

    tn_d = _div(D, 512, LANES)
    h1, W_up = _mm(ycat, W_out, dims="nn", tm=tm, tn=tn_d, tk=D, out_dtype=f32, res=xv, name="mm_out",
                   rider=_gather_rider([W_up_i], [job(0, 2, (3,), True)]))
    a2 = _rms_fwd(h1, norm_ffn, name="rms_ffn")
    perm = lambda b: (b % 2) * nj + b // 2
    u, W_down_i, W_pg_i, W_pp_i, cw_all = _mm(
        a2, W_up, dims="nn", tm=tm, tn=tc, tk=D, out_dtype=f32, name="mm_up", b_idx=lambda i, j, k: (k, perm(j)),
        rider=_gather_rider([placed[3], placed[4], placed[5], cw_mine], [job(0, 3, every), job(1, 4, every), job(2, 5, every), job(3, None, every)]))
    ncw = conv_w.shape[1]
    cw_full = jnp.transpose(cw_all.reshape(N_CHIPS, SUBLANES, -1)[:, :ncw], (1, 0, 2)).reshape(ncw, -1)
    cb = conv_b
    mact, W_down, W_pg, W_pp = _convact_fwd(
        u, cw_full, cb, tc, rider=_gather_rider([W_down_i, W_pg_i, W_pp_i], [job(0, 3, every, True), job(1, 4, every, True), job(2, 5, every, True)]))
    tn_w = _div(D, 1024, LANES)
    h2 = _mm(mact, W_down, dims="nn", tm=_div(S, 512, LANES), tn=tn_w, tk=F, out_dtype=f32, res=h1, name="mm_down", cols_outer=True)
    a3 = _rms_fwd(h2, norm_ple, name="rms_ple")
    zg = _mm(a3, W_pg, dims="nn", tm=tm, tn=tn_d, tk=D, out_dtype=f32, name="mm_ple_gate")
    p16 = pv.astype(bf16)
    pp = _mm(p16, W_pp, dims="nn", tm=tm, tn=tn_d, tk=pv.shape[1], out_dtype=f32, name="mm_ple_proj")
    dh3, dzg, dpp, loss_part, d_fn = _tail(h2, zg, pp, tgt, final_norm.reshape(1, D))

    pair = {}

    def siblings(idx, grads):
        return _sibling_rider(grads, [shard_shapes[i] for i in idx], [big_axes[i] for i in idx])

    def pair_up(idx, grads, got):
        for i, d, g in zip(idx, grads, got):
            pair[i] = _pair_sum(d, g, shard_shapes[i], big_axes[i], where, name=f"grad_pair_sum_{i}")

    def reduced_half(i, landed):
        return _chip_sum(pair[i], landed, where, name=f"grad_chip_sum_{i}")

    tk_s = S
    dW_pp = _mm(p16, dpp, dims="tn", tm=pv.shape[1], tn=tn_w, tk=tk_s, out_dtype=bf16, name="mm_d_ple_proj")
    dW_pg = _mm(a3, dzg, dims="tn", tm=tn_w, tn=tn_w, tk=tk_s, out_dtype=bf16, name="mm_d_ple_gate")
    da3, *got = _mm(dzg, W_pg, dims="nt", tm=tm, tn=_div(D, 1024, LANES), tk=D, out_dtype=f32, name="mm_da3",
                    rider=siblings([4, 5], [dW_pg, dW_pp]))
    pair_up([4, 5], [dW_pg, dW_pp], got)
    dh2, dh2b, d_nple = _rms_bwd(h2, norm_ple, da3, dh3, name="rms_ple_bwd")
    dm, land4, land5 = _mm(dh2b, W_down, dims="nt", tm=tm, tn=_div(F, 512, LANES), tk=D, out_dtype=f32, name="mm_dm",
                           rider=_chips_rider([pair[4], pair[5]]))
    dW_down = _mm(mact, dh2b, dims="tn", tm=_div(F, 1408, LANES), tn=tn_d, tk=tk_s, out_dtype=bf16, name="mm_d_down")
    du, dcw_g, dcw_v, dcb_g, dcb_v, got3 = _convact_bwd(u, dm, cw_full, cb, tc, rider=siblings([3], [dW_down]))
    pair_up([3], [dW_down], [got3])
    dW_up, land3 = _mm(a2, du, dims="tn", tm=_div(D, 1024, LANES), tn=tc, tk=tk_s, out_dtype=bf16, name="mm_d_up",
                       o_idx=lambda i, j, k: (i, perm(j)), rider=_chips_rider([pair[3]]))
    da2, got2 = _mm(du, W_up, dims="nt", tm=tm, tn=D, tk=tc, out_dtype=f32, name="mm_da2",
                    b_idx=lambda i, j, k: (j, perm(k)), rider=siblings([2], [dW_up]))
    pair_up([2], [dW_up], [got2])
    dh1, dh1b, d_nffn = _rms_bwd(h1, norm_ffn, da2, dh2, name="rms_ffn_bwd")
    dycat = _mm(dh1b, W_out, dims="nt", tm=tm, tn=_div(D, 1024, LANES), tk=D, out_dtype=f32, name="mm_dycat")
    dW_out = _mm(ycat, dh1b, dims="tn", tm=tn_w, tn=tn_w, tk=tk_s, out_dtype=bf16, name="mm_d_out")
    dq_att, dk_att, dv_att, dbias, land2, got1 = _attn_bwd(
        proj, kp, vp, ext, dycat, W, rider=_both(_chips_rider([pair[2]], rels=(1, 2)), siblings([1], [dW_out])))
    pair_up([1], [dW_out], [got1])
    d_hg4, d_lbl, d_hgn, land2 = _hgrn_bwd(proj, o_hg, st, dycat, lb_logits, hg_norm, W,
                                           rider=_chips_rider([pair[2]], rels=(3,), landing=[land2]))
    d_rb = _bias_bwd(dbias, onehot)[:, 0, :nrel]
    dproj = [d_hg4, dq_att, dk_att, dv_att]
    tw = _div(W, 1024, LANES)
    dW_in, land1 = _mm(a1, dproj, dims="tn", tm=tn_w, tn=_div(W, 512, LANES), tk=tk_s, out_dtype=bf16, name="mm_d_in",
                       rider=_chips_rider([pair[1]]))
    pair_up([0], [dW_in], _run_rider(siblings([0], [dW_in]), name="grads_to_sibling_0"))
    halves = [reduced_half(i, land) for i, land in ((1, land1), (2, land2), (3, land3), (4, land4), (5, land5))]
    da1, land0, *g_rest = _mm(dproj, W_in, dims="nt", tm=tm, tn=D, tk=tw, out_dtype=f32, name="mm_da1",
                              rider=_both(_chips_rider([pair[0]]), _join_rider(halves)))
    grad_x, _, d_nmix = _rms_bwd(xv, norm_mix, da1, dh1, name="rms_mix_bwd")
    g_big = list(_run_rider(_join_rider([reduced_half(0, land0)]), name="grads_join_w_in")) + list(g_rest)

    d_cw = jnp.concatenate([dcw_g, dcw_v], axis=1)
    d_cb = jnp.concatenate([dcb_g, dcb_v], axis=1)
    small_parts = [loss_part[:, :1], d_nmix, d_lbl, d_hgn, d_rb, d_nffn, d_cw, d_cb, d_nple, d_fn]
    small_shapes = [(1, 1), (1, D), lb_logits.shape, hg_norm.shape, (H, nrel), (1, D), (ncw, 2 * F), (1, 2 * F), (1, D), (1, D)]
    red = _unpack(_allreduce_small(_pack(small_parts)), small_shapes)
    loss = red[0].reshape(())
    g_nmix, g_lbl, g_hgn, g_rb, g_nffn, g_cw_all, g_cb, g_nple, g_fn = red[1:]
    csh = conv_w.shape[2]
    g_cw = lax.dynamic_slice(g_cw_all, (0, chip * csh), (ncw, csh))
    small_g = [g_nmix, g_lbl, g_hgn, g_rb[None], g_nffn, g_cw[None], g_cb, g_nple, g_fn.reshape(D)]
    small_w = [norm_mix, lb_logits, hg_norm, rel_bias, norm_ffn, conv_w, conv_b, norm_ple, final_norm]
    small_m = [m_norm_mix, m_lb_logits, m_hg_norm, m_rel_bias, m_norm_ffn, m_conv_w, m_conv_b, m_norm_ple, m_final_norm]
    small_v = [v_norm_mix, v_lb_logits, v_hg_norm, v_rel_bias, v_norm_ffn, v_conv_w, v_conv_b, v_norm_ple, v_final_norm]
    shapes_s = [w.shape for w in small_w]
    _, sd, sm, sv = _adamw(_pack(small_w), _pack(small_g), _pack(small_m), _pack(small_v), name="adamw_small")
    small_g = [g.reshape(s) for g, s in zip(small_g, shapes_s)]
    small_d, small_nm, small_nv = _unpack(sd, shapes_s), _unpack(sm, shapes_s), _unpack(sv, shapes_s)

    big_m = [m_w_in[0], m_w_out[0], m_w_up[0], m_w_down[0], m_w_ple_gate[0], m_w_ple_proj[0]]
    big_v = [v_w_in[0], v_w_out[0], v_w_up[0], v_w_down[0], v_w_ple_gate[0], v_w_ple_proj[0]]
    big_g, big_d, big_nm, big_nv = [], [], [], []
    for i in range(6):
        update = _adamw_sparsecore if i == 2 else _adamw
        g_, d_, m_, v_ = update(big[i], g_big[i], big_m[i], big_v[i], name=f"adamw_{i}")
        big_g.append(g_[None])
        big_d.append(d_[None])
        big_nm.append(m_[None])
        big_nv.append(v_[None])
    g_big = big_g

    def order(sm_list, bg_list):
        s, b = sm_list, bg_list
        return [s[0], b[0], s[1], s[2], s[3], b[1], s[4], b[2], s[5], s[6], b[3], s[7], b[4], b[5], s[8]]

    return (loss, grad_x[None], *order(small_g, g_big), *order(small_d, big_d), *order(small_nm, big_nm), *order(small_nv, big_nv))
```

```python
import functools
from typing import Callable, NamedTuple

import jax
import jax.numpy as jnp
import numpy as np
from jax import lax
from jax.experimental import pallas as pl
from jax.experimental.pallas import tpu as pltpu
from jax.experimental.pallas import tpu_sc as plsc

f32 = jnp.float32
bf16 = jnp.bfloat16

CHUNK = 64
HEAD_DIM = 128
LEFT_CHUNKS = 8
REL_CLIP = 128
EPS = 1e-6
HG_SUB = 16
HG_HEADS_PER_STEP = 8
ATT_Q_ROWS = 256
ATT_HEADS_PER_STEP = 2
ADAM_LR, ADAM_B1, ADAM_B2, ADAM_EPS, ADAM_WD, ADAM_STEP = 0.001, 0.9, 0.999, 1e-08, 0.01, 10
LANES = 128
SUBLANES = 8
N_CHIPS = 4
N_DEV = 8
RDMA_PIECES = 4
SC_TILES = 32
SC_LANES = 16
MESH = pl.DeviceIdType.MESH
NEG = float(np.finfo(np.float32).min)

NN = (((1,), (0,)), ((), ()))
NT = (((1,), (1,)), ((), ()))
TN = (((0,), (0,)), ((), ()))
HIGHEST = lax.Precision.HIGHEST


def _dot(a, b, dims, precision=None):
    return lax.dot_general(a, b, dims, preferred_element_type=f32, precision=precision)


def _sigmoid(v):
    return 1.0 / (1.0 + jnp.exp(-v))


def _div(n, pref, mult):
    best = None
    d = mult
    while d <= min(n, pref):
        if n % d == 0:
            best = d
        d += mult
    return best if best is not None else n


def _hbm():
    return pl.BlockSpec(memory_space=pltpu.HBM)


def _vmem():
    return pl.BlockSpec(memory_space=pltpu.VMEM)


def _mm(a, b, *, dims, tm, tn, tk, out_dtype, name, res=None, a_idx=None, b_idx=None, o_idx=None, out_shape=None, rider=None,
        cols_outer=False):
    a_list = list(a) if isinstance(a, (list, tuple)) else [a]
    b_list = list(b) if isinstance(b, (list, tuple)) else [b]
    na, nb = len(a_list), len(b_list)
    if dims == "nn":
        assert na == 1 and nb == 1
        (M, K), (_, N) = a.shape, b.shape
        a_blk, b_blk, dn = (tm, tk), (tk, tn), NN
        a_def, b_def = (lambda i, j, k: (i, k)), (lambda i, j, k: (k, j))
    elif dims == "nt":
        assert nb == 1
        M, K, N = a_list[0].shape[0], sum(p.shape[1] for p in a_list), b.shape[0]
        a_blk, b_blk, dn = (tm, tk), (tn, tk), NT
        a_def, b_def = (lambda i, j, k: (i, k)), (lambda i, j, k: (j, k))
    else:
        assert na == 1
        (K, M), N = a.shape, sum(p.shape[1] for p in b_list)
        a_blk, b_blk, dn = (tk, tm), (tk, tn), TN
        a_def, b_def = (lambda i, j, k: (k, i)), (lambda i, j, k: (k, j))
    assert M % tm == 0 and N % tn == 0 and K % tk == 0, (name, M, N, K, tm, tn, tk)
    nk = K // tk
    has_res = res is not None
    counts = [p.shape[1] // tk for p in a_list] if na > 1 else [p.shape[1] // tn for p in b_list]
    starts = [sum(counts[:p]) for p in range(len(counts))]

    def body(*refs):
        a_refs, b_refs, rest = refs[:na], refs[na:na + nb], refs[na + nb:]
        res_ref = rest[0] if has_res else None
        o_ref = rest[1] if has_res else rest[0]
        k = pl.program_id(2)
        j = pl.program_id(0) if cols_outer else pl.program_id(1)

        def finish(acc):
            if has_res:
                acc = acc + res_ref[...]
            o_ref[...] = acc.astype(out_dtype)

        def consume(part):
            if nk == 1:
                finish(part)
                return
            acc_ref = refs[-1]

            @pl.when(k == 0)
            def _():
                acc_ref[...] = part

            @pl.when(k > 0)
            def _():
                acc_ref[...] += part

            @pl.when(k == nk - 1)
            def _():
                finish(acc_ref[...])

        if len(counts) == 1:
            consume(_dot(a_refs[0][...].astype(bf16), b_refs[0][...].astype(bf16), dn))
        else:
            at = k if na > 1 else j
            for p, (s, n) in enumerate(zip(starts, counts)):
                a_ref, b_ref = a_refs[p if na > 1 else 0], b_refs[p if nb > 1 else 0]
                pl.when((at >= s) & (at < s + n))(
                    functools.partial(lambda ar, br: consume(_dot(ar[...].astype(bf16), br[...].astype(bf16), dn)), a_ref, b_ref))

    def ordered(custom, default):
        f = custom or default
        return (lambda j, i, k: f(i, j, k)) if cols_outer else f

    def piece_map(default, p):
        s, n = starts[p], counts[p]
        if na > 1:
            return lambda i, j, k: default(i, j, jnp.clip(k - s, 0, n - 1))
        return lambda i, j, k: default(i, jnp.clip(j - s, 0, n - 1), k)

    o_map = ordered(o_idx, lambda i, j, k: (i, j))
    a_specs = [pl.BlockSpec(a_blk, ordered(a_idx, piece_map(a_def, p) if na > 1 else a_def)) for p in range(na)]
    b_specs = [pl.BlockSpec(b_blk, ordered(b_idx, piece_map(b_def, p) if nb > 1 else b_def)) for p in range(nb)]
    in_specs = a_specs + b_specs
    args = a_list + b_list
    if has_res:
        in_specs.append(pl.BlockSpec((tm, tn), o_map))
        args.append(res)
    outs = _grid_call(
        body,
        rider,
        name=name,
        grid=(N // tn, M // tm, nk) if cols_outer else (M // tm, N // tn, nk),
        in_specs=in_specs,
        out_specs=[pl.BlockSpec((tm, tn), o_map)],
        out_shape=[jax.ShapeDtypeStruct(out_shape or (M, N), out_dtype)],
        scratch_shapes=[pltpu.VMEM((tm, tn), f32)] if nk > 1 else [],
        args=args,
    )
    return outs[0] if rider is None else outs


def _rms_fwd(xv, g, *, name):
    S, D = xv.shape
    ts = _div(S, 512, SUBLANES)

    def body(x_ref, g_ref, o_ref):
        v = x_ref[...]
        r = lax.rsqrt(jnp.mean(v * v, axis=-1, keepdims=True) + EPS)
        o_ref[...] = (v * r * g_ref[...]).astype(bf16)

    return pl.pallas_call(
        body,
        name=name,
        grid=(S // ts,),
        in_specs=[pl.BlockSpec((ts, D), lambda i: (i, 0)), pl.BlockSpec((1, D), lambda i: (0, 0))],
        out_specs=pl.BlockSpec((ts, D), lambda i: (i, 0)),
        out_shape=jax.ShapeDtypeStruct((S, D), bf16),
        compiler_params=pltpu.CompilerParams(dimension_semantics=("parallel",)),
    )(xv, g)


def _rms_bwd(xv, g, dy, dres, *, name, rider=None):
    S, D = xv.shape
    ts = _div(S, 256, SUBLANES)

    def body(x_ref, g_ref, dy_ref, dres_ref, dx_ref, dxb_ref, dg_ref):
        i = pl.program_id(0)
        v = x_ref[...]
        r = lax.rsqrt(jnp.mean(v * v, axis=-1, keepdims=True) + EPS)
        vn = v * r
        d = dy_ref[...]
        part = jnp.sum(d * vn, axis=0, keepdims=True)

        @pl.when(i == 0)
        def _():
            dg_ref[...] = part

        @pl.when(i > 0)
        def _():
            dg_ref[...] += part

        t = d * g_ref[...]
        dx = dres_ref[...] + r * (t - vn * jnp.mean(t * vn, axis=-1, keepdims=True))
        dx_ref[...] = dx
        dxb_ref[...] = dx.astype(bf16)

    row = pl.BlockSpec((ts, D), lambda i: (i, 0))
    vec = pl.BlockSpec((1, D), lambda i: (0, 0))
    return _grid_call(
        body,
        rider,
        name=name,
        grid=(S // ts,),
        in_specs=[row, vec, row, row],
        out_specs=[row, row, vec],
        out_shape=[jax.ShapeDtypeStruct((S, D), f32), jax.ShapeDtypeStruct((S, D), bf16), jax.ShapeDtypeStruct((1, D), f32)],
        scratch_shapes=[],
        args=(xv, g, dy, dres),
    )


def _tail(h2, zg, pp, tgt, fn):
    S, D = h2.shape
    ts = _div(S, 256, SUBLANES)

    def body(h_ref, z_ref, p_ref, t_ref, fn_ref, dh_ref, dz_ref, dp_ref, loss_ref, dfn_ref):
        i = pl.program_id(0)
        sg = _sigmoid(z_ref[...])
        ppv = p_ref[...]
        h3 = h_ref[...] + sg * ppv
        r = lax.rsqrt(jnp.mean(h3 * h3, axis=-1, keepdims=True) + EPS)
        hn = h3 * r
        fnv = fn_ref[...]
        e = hn * fnv - t_ref[...]
        lpart = 0.5 * jnp.sum(jnp.mean(e * e, axis=-1, keepdims=True), axis=0, keepdims=True)
        dy = e * (1.0 / D)
        dpart = jnp.sum(dy * hn, axis=0, keepdims=True)

        @pl.when(i == 0)
        def _():
            loss_ref[...] = jnp.broadcast_to(lpart, loss_ref.shape)
            dfn_ref[...] = dpart

        @pl.when(i > 0)
        def _():
            loss_ref[...] += jnp.broadcast_to(lpart, loss_ref.shape)
            dfn_ref[...] += dpart

        t = dy * fnv
        dh3 = r * (t - hn * jnp.mean(t * hn, axis=-1, keepdims=True))
        dh_ref[...] = dh3
        dz_ref[...] = (dh3 * ppv * sg * (1.0 - sg)).astype(bf16)
        dp_ref[...] = (dh3 * sg).astype(bf16)

    row = pl.BlockSpec((ts, D), lambda i: (i, 0))
    vec = pl.BlockSpec((1, D), lambda i: (0, 0))
    one = pl.BlockSpec((1, LANES), lambda i: (0, 0))
    return pl.pallas_call(
        body,
        name="ple_final_loss",
        grid=(S // ts,),
        in_specs=[row, row, row, row, vec],
        out_specs=[row, row, row, one, vec],
        out_shape=[
            jax.ShapeDtypeStruct((S, D), f32),
            jax.ShapeDtypeStruct((S, D), bf16),
            jax.ShapeDtypeStruct((S, D), bf16),
            jax.ShapeDtypeStruct((1, LANES), f32),
            jax.ShapeDtypeStruct((1, D), f32),
        ],
        compiler_params=pltpu.CompilerParams(dimension_semantics=("arbitrary",)),
    )(h2, zg, pp, tgt, fn)


def _shift_down(v, halo, k):
    r = pltpu.roll(v, k, 0)
    hr = pltpu.roll(halo, k, 0)
    row = lax.broadcasted_iota(jnp.int32, hr.shape, 0)
    top = jnp.where(row < k, hr, r[0:SUBLANES])
    return top if v.shape[0] == SUBLANES else jnp.concatenate([top, r[SUBLANES:]], axis=0)


def _shift_up(v, halo, k):
    n = v.shape[0]
    r = pltpu.roll(v, n - k, 0)
    hr = pltpu.roll(halo, SUBLANES - k, 0)
    row = lax.broadcasted_iota(jnp.int32, hr.shape, 0)
    bot = jnp.where(row >= SUBLANES - k, hr, r[n - SUBLANES:])
    return jnp.concatenate([r[: n - SUBLANES], bot], axis=0)


def _conv_specs(S, F, tc, ts):
    nj = F // tc
    rows8 = ts // SUBLANES
    main = pl.BlockSpec((ts, 2 * tc), lambda j, i: (i, j))
    prev = pl.BlockSpec((SUBLANES, 2 * tc), lambda j, i: (jnp.maximum(i * rows8 - 1, 0), j))
    nxt = pl.BlockSpec((SUBLANES, 2 * tc), lambda j, i: (jnp.minimum((i + 1) * rows8, S // SUBLANES - 1), j))
    wg = pl.BlockSpec((3, tc), lambda j, i: (0, j))
    wv = pl.BlockSpec((3, tc), lambda j, i: (0, nj + j))
    bg = pl.BlockSpec((1, tc), lambda j, i: (0, j))
    bv = pl.BlockSpec((1, tc), lambda j, i: (0, nj + j))
    half = pl.BlockSpec((ts, tc), lambda j, i: (i, j))
    return nj, main, prev, nxt, wg, wv, bg, bv, half


def _conv_pre(u_ref, h_ref, wg_ref, wv_ref, bg_ref, bv_ref, tc):
    i = pl.program_id(1)
    u = u_ref[...]
    halo = jnp.where(i == 0, 0.0, h_ref[...])
    u1 = _shift_down(u, halo, 1)
    u2 = _shift_down(u, halo, 2)
    w = jnp.concatenate([wg_ref[...], wv_ref[...]], axis=1)
    b = jnp.concatenate([bg_ref[...], bv_ref[...]], axis=1)
    uc = b + w[0:1] * u2 + w[1:2] * u1 + w[2:3] * u
    return u, u1, u2, uc[:, :tc], uc[:, tc:]


def _convact_fwd(u, cw, cb, tc, rider=None):
    S, F2 = u.shape
    F = F2 // 2
    ts = _div(S, 512, SUBLANES)
    nj, main, prev, nxt, wg, wv, bg, bv, half = _conv_specs(S, F, tc, ts)

    def body(u_ref, h_ref, wg_ref, wv_ref, bg_ref, bv_ref, m_ref):
        _, _, _, g, v = _conv_pre(u_ref, h_ref, wg_ref, wv_ref, bg_ref, bv_ref, tc)
        m_ref[...] = (g * _sigmoid(g) * v).astype(bf16)

    outs = _grid_call(
        body,
        rider,
        name="convact_fwd",
        grid=(nj, S // ts),
        in_specs=[main, prev, wg, wv, bg, bv],
        out_specs=[half],
        out_shape=[jax.ShapeDtypeStruct((S, F), bf16)],
        scratch_shapes=[],
        args=(u, u, cw, cw, cb, cb),
    )
    return outs[0] if rider is None else outs


def _convact_bwd(u, dm, cw, cb, tc, rider=None):
    S, F2 = u.shape
    F = F2 // 2
    ts = _div(S, 512, SUBLANES)
    nj, main, prev, nxt, wg, wv, bg, bv, half = _conv_specs(S, F, tc, ts)
    ni = S // ts
    half_nxt = pl.BlockSpec((SUBLANES, tc), lambda j, i: (jnp.minimum((i + 1) * (ts // SUBLANES), S // SUBLANES - 1), j))

    def act_bwd(g, v, d):
        sg = _sigmoid(g)
        return jnp.concatenate([d * v * sg * (1.0 + g * (1.0 - sg)), d * g * sg], axis=1)

    def body(u_ref, h_ref, n_ref, dm_ref, dmn_ref, wg_ref, wv_ref, bg_ref, bv_ref, du_ref, dwg_ref, dwv_ref, dbg_ref, dbv_ref):
        i = pl.program_id(1)
        u0, u1, u2, g, v = _conv_pre(u_ref, h_ref, wg_ref, wv_ref, bg_ref, bv_ref, tc)
        duc = act_bwd(g, v, dm_ref[...])
        w = jnp.concatenate([wg_ref[...], wv_ref[...]], axis=1)
        b = jnp.concatenate([bg_ref[...], bv_ref[...]], axis=1)
        un = n_ref[...]
        tail = u0[ts - SUBLANES:]
        ucn = b + w[0:1] * _shift_down(un, tail, 2) + w[1:2] * _shift_down(un, tail, 1) + w[2:3] * un
        ducn = jnp.where(i == ni - 1, 0.0, act_bwd(ucn[:, :tc], ucn[:, tc:], dmn_ref[...]))
        du_ref[...] = (w[2:3] * duc + w[1:2] * _shift_up(duc, ducn, 1) + w[0:1] * _shift_up(duc, ducn, 2)).astype(bf16)
        db = jnp.sum(duc, axis=0, keepdims=True)
        dw = jnp.concatenate(
            [jnp.sum(duc * u2, axis=0, keepdims=True), jnp.sum(duc * u1, axis=0, keepdims=True), jnp.sum(duc * u0, axis=0, keepdims=True)],
            axis=0,
        )

        @pl.when(i == 0)
        def _():
            dwg_ref[...] = dw[:, :tc]
            dwv_ref[...] = dw[:, tc:]
            dbg_ref[...] = db[:, :tc]
            dbv_ref[...] = db[:, tc:]

        @pl.when(i > 0)
        def _():
            dwg_ref[...] += dw[:, :tc]
            dwv_ref[...] += dw[:, tc:]
            dbg_ref[...] += db[:, :tc]
            dbv_ref[...] += db[:, tc:]

    w_out = pl.BlockSpec((3, tc), lambda j, i: (0, j))
    b_out = pl.BlockSpec((1, tc), lambda j, i: (0, j))
    return _grid_call(
        body,
        rider,
        name="convact_bwd",
        grid=(nj, ni),
        in_specs=[main, prev, nxt, half, half_nxt, wg, wv, bg, bv],
        out_specs=[main, w_out, w_out, b_out, b_out],
        out_shape=[
            jax.ShapeDtypeStruct((S, F2), bf16),
            jax.ShapeDtypeStruct((3, F), f32),
            jax.ShapeDtypeStruct((3, F), f32),
            jax.ShapeDtypeStruct((1, F), f32),
            jax.ShapeDtypeStruct((1, F), f32),
        ],
        scratch_shapes=[],
        args=(u, u, u, dm, dm, cw, cw, cb, cb),
    )


def _hg_gates(qp, fp, lbl):
    lb = _sigmoid(lbl[0:1] - lbl[1:2])
    sig = _sigmoid(fp)
    sigm = _sigmoid(-fp)
    f = lb + (1.0 - lb) * sig
    k = (1.0 - lb) * sigm
    sq = _sigmoid(qp)
    qf = qp * sq
    row = lax.broadcasted_iota(jnp.int32, (CHUNK, CHUNK), 0)
    col = lax.broadcasted_iota(jnp.int32, (CHUNK, CHUNK), 1)
    b = _dot((row >= col).astype(f32), jnp.log(f), NN, HIGHEST)
    return lb, sig, sigm, f, k, sq, qf, b


def _hg_block(qf, k, b, I):
    r0, n = I * HG_SUB, (I + 1) * HG_SUB
    base = b[r0 - 1:r0] if I > 0 else jnp.zeros_like(b[0:1])
    eq = jnp.exp(b[r0:n] - base)
    ek = jnp.exp(base - b[0:n])
    qt16 = (qf[r0:n] * eq).astype(bf16)
    kt16 = (k[0:n] * ek).astype(bf16)
    row = lax.broadcasted_iota(jnp.int32, (HG_SUB, n), 0) + r0
    col = lax.broadcasted_iota(jnp.int32, (HG_SUB, n), 1)
    return r0, n, eq, ek, qt16, kt16, col <= row


def _per_head(v, hp, fn):
    return jnp.concatenate([fn(v[:, p * HEAD_DIM:(p + 1) * HEAD_DIM]) for p in range(hp)], axis=1)


def _head_mean(v, hp):
    return _per_head(v, hp, lambda t: jnp.broadcast_to(jnp.mean(t, axis=-1, keepdims=True), t.shape))


def _pad_rows(v, rows):
    return v if v.shape[0] == rows else jnp.concatenate([v, jnp.zeros((rows - v.shape[0], v.shape[1]), v.dtype)], axis=0)


def _hgrn_specs(S, W, hp, reverse):
    nc = S // CHUNK
    ngrp = W // (hp * HEAD_DIM)
    cw = hp * HEAD_DIM
    cidx = (lambda c: nc - 1 - c) if reverse else (lambda c: c)

    def proj(off):
        return pl.BlockSpec((CHUNK, cw), lambda h, c: (cidx(c), off * ngrp + h))

    act = pl.BlockSpec((CHUNK, cw), lambda h, c: (cidx(c), h))
    lbl = pl.BlockSpec((2, cw), lambda h, c: (0, h))
    gn = pl.BlockSpec((1, HEAD_DIM), lambda h, c: (0, 0))
    st = pl.BlockSpec((hp, 1, HEAD_DIM, HEAD_DIM), lambda h, c: (h, cidx(c), 0, 0))
    return nc, ngrp, proj, act, lbl, gn, st


def _grid_call(body, rider, *, name, grid, in_specs, out_specs, out_shape, scratch_shapes, args):
    params = pltpu.CompilerParams(dimension_semantics=("arbitrary",) * len(grid))
    if rider is None:
        return pl.pallas_call(body, name=name, grid=grid, in_specs=in_specs, out_specs=out_specs, out_shape=out_shape,
                              scratch_shapes=scratch_shapes, compiler_params=params)(*args)
    n_in, n_out, n_scr = len(in_specs), len(out_specs), len(scratch_shapes)
    ri, ro = len(rider.ins), len(rider.out_shapes)

    def wrapped(*refs):
        a, b = n_in, n_in + ri
        c, d = b + n_out, b + n_out + ro
        e = d + n_scr
        ids = [pl.program_id(t) for t in range(len(grid))]
        is_first = functools.reduce(jnp.logical_and, [i == 0 for i in ids])
        is_last = functools.reduce(jnp.logical_and, [i == g - 1 for i, g in zip(ids, grid)])

        @pl.when(is_first)
        def _():
            rider.first(refs[a:b], refs[c:d], refs[e:])

        body(*refs[:a], *refs[b:c], *refs[d:e])

        @pl.when(is_last)
        def _():
            rider.last(refs[a:b], refs[c:d], refs[e:])

    return pl.pallas_call(
        wrapped,
        name=name,
        grid=grid,
        in_specs=list(in_specs) + [_hbm()] * ri,
        out_specs=list(out_specs) + [_hbm()] * ro,
        out_shape=list(out_shape) + list(rider.out_shapes),
        input_output_aliases={n_in + i: n_out + o for i, o in rider.aliases.items()},
        scratch_shapes=list(scratch_shapes) + list(rider.sems),
        compiler_params=params,
    )(*args, *rider.ins)


def _hgrn_fwd(proj, lb_logits, hg_norm, W, rider=None):
    S = proj.shape[0]
    hp = min(HG_HEADS_PER_STEP, W // HEAD_DIM)
    nc, ngrp, pspec, act, lbl_spec, gn_spec, st_spec = _hgrn_specs(S, W, hp, False)
    nb = CHUNK // HG_SUB

    def body(q_ref, f_ref, i_ref, g_ref, lbl_ref, gn_ref, y_ref, o_ref, st_ref, state):
        c = pl.program_id(1)

        @pl.when(c == 0)
        def _():
            state[...] = jnp.zeros_like(state)

        _, _, _, _, k, _, qf, b = _hg_gates(q_ref[...], f_ref[...], lbl_ref[...])
        v16 = i_ref[...].astype(bf16)
        a16 = (qf * jnp.exp(b)).astype(bf16)
        bl = b[CHUNK - 1:CHUNK]
        kd16 = (k * jnp.exp(bl - b)).astype(bf16)
        ebl = jnp.exp(bl)
        blocks = [_hg_block(qf, k, b, I) for I in range(nb)]
        heads = [slice(p * HEAD_DIM, (p + 1) * HEAD_DIM) for p in range(hp)]
        st_prev = state[...]
        st_ref[:, 0] = st_prev
        inter = [_dot(a16[:, sl], st_prev[p].astype(bf16), NT) for p, sl in enumerate(heads)]
        scores = [[_dot(qt16[:, sl], kt16[:, sl], NT) for sl in heads] for _, _, _, _, qt16, kt16, _ in blocks]
        intra = [[_dot(jnp.where(blk[6], sc, 0.0).astype(bf16), v16[0:blk[1], sl], NN) for sc, sl in zip(scs, heads)]
                 for blk, scs in zip(blocks, scores)]
        st_new = [st_prev[p] * ebl[:, sl] + _dot(v16[:, sl], kd16[:, sl], TN) for p, sl in enumerate(heads)]
        state[...] = jnp.stack(st_new)
        o = jnp.concatenate([inter[p] + jnp.concatenate([rows[p] for rows in intra], axis=0) for p in range(hp)], axis=1)
        o_ref[...] = o
        r = lax.rsqrt(_head_mean(o * o, hp) + EPS)
        gp = g_ref[...]
        y_ref[...] = (o * r * jnp.tile(gn_ref[...], (1, hp)) * (gp * _sigmoid(gp))).astype(bf16)

    H = W // HEAD_DIM
    return _grid_call(
        body,
        rider,
        name="hgrn_fwd",
        grid=(ngrp, nc),
        in_specs=[pspec(0), pspec(1), pspec(2), pspec(3), lbl_spec, gn_spec],
        out_specs=[act, act, st_spec],
        out_shape=[
            jax.ShapeDtypeStruct((S, W), bf16),
            jax.ShapeDtypeStruct((S, W), f32),
            jax.ShapeDtypeStruct((H, nc, HEAD_DIM, HEAD_DIM), f32),
        ],
        scratch_shapes=[pltpu.VMEM((hp, HEAD_DIM, HEAD_DIM), f32)],
        args=(proj, proj, proj, proj, lb_logits, hg_norm),
    )


def _hgrn_bwd(proj, o_hg, st, dycat, lb_logits, hg_norm, W, rider=None):
    S = proj.shape[0]
    hp = min(HG_HEADS_PER_STEP, W // HEAD_DIM)
    assert hp * HEAD_DIM == W, "the four gradients share one output block: all heads in one grid step"
    nc, ngrp, pspec, act, lbl_spec, gn_spec, st_spec = _hgrn_specs(S, W, hp, True)
    nb = CHUNK // HG_SUB

    def body(q_ref, f_ref, i_ref, g_ref, o_ref, st_ref, dy_ref, lbl_ref, gn_ref,
             d4_ref, dlbl_ref, dgn_ref, dstate):
        cw_ = hp * HEAD_DIM
        dq_ref, df_ref, di_ref, dg_ref = (d4_ref.at[:, pl.ds(t * cw_, cw_)] for t in range(4))
        h = pl.program_id(0)
        c = pl.program_id(1)

        @pl.when(c == 0)
        def _():
            dstate[...] = jnp.zeros_like(dstate)
            dlbl_ref[...] = jnp.zeros_like(dlbl_ref)

        @pl.when((c == 0) & (h == 0))
        def _():
            dgn_ref[...] = jnp.zeros_like(dgn_ref)

        row = lax.broadcasted_iota(jnp.int32, (CHUNK, CHUNK), 0)
        col = lax.broadcasted_iota(jnp.int32, (CHUNK, CHUNK), 1)
        upper = (col >= row).astype(f32)
        last_row = lax.broadcasted_iota(jnp.int32, (CHUNK, hp * HEAD_DIM), 0) == CHUNK - 1

        qp, fp = q_ref[...], f_ref[...]
        lb, sig, sigm, f, k, sq, qf, b = _hg_gates(qp, fp, lbl_ref[...])
        v16 = i_ref[...].astype(bf16)
        gnv = jnp.tile(gn_ref[...], (1, hp))
        gp = g_ref[...]
        sgg = _sigmoid(gp)
        gate = gp * sgg
        o = o_ref[...]
        r = lax.rsqrt(_head_mean(o * o, hp) + EPS)
        on = o * r
        dy = dy_ref[...]
        dg_ref[...] = (dy * on * gnv * (sgg * (1.0 + gp * (1.0 - sgg)))).astype(bf16)
        dgn_wide = jnp.sum(dy * on * gate, axis=0, keepdims=True)
        dgn_ref[...] += functools.reduce(jnp.add, [dgn_wide[:, p * HEAD_DIM:(p + 1) * HEAD_DIM] for p in range(hp)])
        don = dy * gnv * gate
        do16 = (r * (don - on * _head_mean(don * on, hp))).astype(bf16)
        eb = jnp.exp(b)
        A = qf * eb
        a16 = A.astype(bf16)
        bl = b[CHUNK - 1:CHUNK]
        ebl = jnp.exp(bl)
        ekd = jnp.exp(bl - b)
        kd = k * ekd
        kd16 = kd.astype(bf16)
        blocks = [_hg_block(qf, k, b, I) for I in range(nb)]

        heads = [slice(p * HEAD_DIM, (p + 1) * HEAD_DIM) for p in range(hp)]
        st_prev = st_ref[:, 0]
        dst_new = dstate[...]
        st16, dst16 = st_prev.astype(bf16), dst_new.astype(bf16)
        dA_h = [_dot(do16[:, sl], st16[p], NN) for p, sl in enumerate(heads)]
        dkd_h = [_dot(v16[:, sl], dst16[p], NN) for p, sl in enumerate(heads)]
        dv_h = [_dot(kd16[:, sl], dst16[p], NT) for p, sl in enumerate(heads)]
        dstate[...] = jnp.stack([dst_new[p] * ebl[:, sl] + _dot(do16[:, sl], a16[:, sl], TN) for p, sl in enumerate(heads)])
        dbl_h = [jnp.sum(dst_new[p] * st_prev[p], axis=0, keepdims=True) for p in range(hp)]
        sc_h = [[jnp.where(mask, _dot(qt16[:, sl], kt16[:, sl], NT), 0.0).astype(bf16) for sl in heads]
                for _, _, _, _, qt16, kt16, mask in blocks]
        dsc_h = [[jnp.where(mask, _dot(do16[r0:n, sl], v16[0:n, sl], NT), 0.0).astype(bf16) for sl in heads]
                 for r0, n, _, _, _, _, mask in blocks]
        dvi_h = [[_dot(sc, do16[blk[0]:blk[1], sl], TN) for sc, sl in zip(scs, heads)] for blk, scs in zip(blocks, sc_h)]
        dqt_h = [[_dot(dsc, blk[5][:, sl], NN) for dsc, sl in zip(dscs, heads)] for blk, dscs in zip(blocks, dsc_h)]
        dkt_h = [[_dot(dsc, blk[4][:, sl], TN) for dsc, sl in zip(dscs, heads)] for blk, dscs in zip(blocks, dsc_h)]
        dv_h = [functools.reduce(jnp.add, [dv_h[p]] + [_pad_rows(rows[p], CHUNK) for rows in dvi_h]) for p in range(hp)]

        dA, dkd = jnp.concatenate(dA_h, axis=1), jnp.concatenate(dkd_h, axis=1)
        dbl = jnp.concatenate(dbl_h, axis=1) * ebl + jnp.sum(dkd * kd, axis=0, keepdims=True)
        db = dA * A - dkd * kd + jnp.where(last_row, dbl, 0.0)
        dk = dkd * ekd
        dq_rows, db_rows = [], []
        for I, (r0, n, eq, ek, qt16, kt16, _) in enumerate(blocks):
            dqt, dkt = jnp.concatenate(dqt_h[I], axis=1), jnp.concatenate(dkt_h[I], axis=1)
            dq_rows.append(dqt * eq)
            dk = dk + _pad_rows(dkt * ek, CHUNK)
            db_rows.append(dqt * qt16.astype(f32))
            db = db - _pad_rows(dkt * kt16.astype(f32), CHUNK)
        dqf = dA * eb + jnp.concatenate(dq_rows, axis=0)
        db = db + jnp.concatenate(db_rows, axis=0)
        dlogf = _dot(upper, db, NN, HIGHEST)
        dfg = dlogf / f
        df_ref[...] = ((1.0 - lb) * sig * sigm * (dfg - dk)).astype(bf16)
        dlb = jnp.sum(sigm * (dfg - dk), axis=0, keepdims=True)
        dl0 = dlb * lb * (1.0 - lb)
        dlbl_ref[...] += jnp.concatenate([dl0, -dl0], axis=0)
        dq_ref[...] = (dqf * (sq * (1.0 + qp * (1.0 - sq)))).astype(bf16)
        di_ref[...] = jnp.concatenate(dv_h, axis=1).astype(bf16)

    cw = hp * HEAD_DIM
    dy_spec = pl.BlockSpec((CHUNK, cw), lambda h, c: (nc - 1 - c, h))
    return _grid_call(
        body,
        rider,
        name="hgrn_bwd",
        grid=(ngrp, nc),
        in_specs=[pspec(0), pspec(1), pspec(2), pspec(3), act, st_spec, dy_spec, lbl_spec, gn_spec],
        out_specs=[pl.BlockSpec((CHUNK, 4 * W), lambda h, c: (nc - 1 - c, 0)), lbl_spec, gn_spec],
        out_shape=[jax.ShapeDtypeStruct((S, 4 * W), bf16), jax.ShapeDtypeStruct((2, W), f32), jax.ShapeDtypeStruct((1, HEAD_DIM), f32)],
        scratch_shapes=[pltpu.VMEM((hp, HEAD_DIM, HEAD_DIM), f32)],
        args=(proj, proj, proj, proj, o_hg, st, dycat, lb_logits, hg_norm),
    )


def _att_dims(S):
    qt = ATT_Q_ROWS if S % ATT_Q_ROWS == 0 else CHUNK
    pad = LEFT_CHUNKS * CHUNK
    kb = pad + qt
    return qt, pad, kb, kb + qt


def _kv_padded(proj, W, pad):
    S = proj.shape[0]
    ts = _div(pad, 512, 2 * SUBLANES)
    assert S % ts == 0 and pad % ts == 0
    npad = pad // ts

    def body(k_ref, v_ref, ko_ref, vo_ref):
        i = pl.program_id(0)

        @pl.when(i < npad)
        def _():
            ko_ref[...] = jnp.zeros_like(ko_ref)
            vo_ref[...] = jnp.zeros_like(vo_ref)

        @pl.when(i >= npad)
        def _():
            ko_ref[...] = k_ref[...].astype(bf16)
            vo_ref[...] = v_ref[...].astype(bf16)

    out = pl.BlockSpec((ts, W), lambda i: (i, 0))
    return pl.pallas_call(
        body,
        name="kv_padded",
        grid=((S + pad) // ts,),
        in_specs=[pl.BlockSpec((ts, W), lambda i: (jnp.maximum(i - npad, 0), 5)), pl.BlockSpec((ts, W), lambda i: (jnp.maximum(i - npad, 0), 6))],
        out_specs=[out, out],
        out_shape=[jax.ShapeDtypeStruct((S + pad, W), bf16)] * 2,
        compiler_params=pltpu.CompilerParams(dimension_semantics=("parallel",)),
    )(proj, proj)


def _att_probs(qk, bias, start, qt, pad, kb):
    s = qk * (HEAD_DIM ** -0.5) + bias
    row = lax.broadcasted_iota(jnp.int32, (qt, kb), 0)
    col = lax.broadcasted_iota(jnp.int32, (qt, kb), 1)
    lo = jnp.bitwise_and(row, -CHUNK)
    ok = (col >= lo) & (col < lo + pad + CHUNK) & (col + start >= pad)
    s = jnp.where(ok, s, NEG)
    e = jnp.exp(s - jnp.max(s, axis=-1, keepdims=True))
    return e / jnp.sum(e, axis=-1, keepdims=True)


def _att_bias(ext_row, qt, kb, ne):
    e = jnp.broadcast_to(ext_row, (qt, ne))
    return pltpu.roll(e, ne - qt + 1, 1, stride=1, stride_axis=0)[:, :kb]


def _att_specs(S, H, ah, qt, pad, ne):
    cw = ah * HEAD_DIM
    q_spec = pl.BlockSpec((qt, cw), lambda h, g: (g, 4 * H // ah + h))
    kv_spec = pl.BlockSpec((S + pad, cw), lambda h, g: (0, h))
    ext_spec = pl.BlockSpec((ah, 1, ne), lambda h, g: (h, 0, 0))
    row_spec = pl.BlockSpec((qt, cw), lambda h, g: (g, h))
    return cw, q_spec, kv_spec, ext_spec, row_spec


def _attn_fwd(proj, kp, vp, ext, W, rider=None):
    S = proj.shape[0]
    H = W // HEAD_DIM
    ah = min(ATT_HEADS_PER_STEP, H)
    qt, pad, kb, ne = _att_dims(S)
    cw, q_spec, kv_spec, ext_spec, row_spec = _att_specs(S, H, ah, qt, pad, ne)
    heads = [slice(a * HEAD_DIM, (a + 1) * HEAD_DIM) for a in range(ah)]

    def body(q_ref, k_ref, v_ref, ext_ref, o_ref, bias):
        g = pl.program_id(1)

        @pl.when(g == 0)
        def _():
            for a in range(ah):
                bias[a] = _att_bias(ext_ref[a], qt, kb, ne)

        start = pl.multiple_of(g * qt, qt)
        q16 = q_ref[...].astype(bf16)
        kb16 = k_ref[pl.ds(start, kb), :]
        vb16 = v_ref[pl.ds(start, kb), :]
        qk = [_dot(q16[:, sl], kb16[:, sl], NT) for sl in heads]
        pn = [_att_probs(qk[a], bias[a], start, qt, pad, kb).astype(bf16) for a in range(ah)]
        o_ref[...] = jnp.concatenate([_dot(pn[a], vb16[:, sl], NN) for a, sl in enumerate(heads)], axis=1).astype(bf16)

    outs = _grid_call(
        body,
        rider,
        name="attn_fwd",
        grid=(H // ah, S // qt),
        in_specs=[q_spec, kv_spec, kv_spec, ext_spec],
        out_specs=[row_spec],
        out_shape=[jax.ShapeDtypeStruct((S, W), bf16)],
        scratch_shapes=[pltpu.VMEM((ah, qt, kb), f32)],
        args=(proj, kp, vp, ext),
    )
    return outs[0] if rider is None else outs


def _attn_bwd(proj, kp, vp, ext, dycat, W, rider=None):
    S = proj.shape[0]
    H = W // HEAD_DIM
    ah = min(ATT_HEADS_PER_STEP, H)
    qt, pad, kb, ne = _att_dims(S)
    cw, q_spec, kv_spec, ext_spec, row_spec = _att_specs(S, H, ah, qt, pad, ne)
    heads = [slice(a * HEAD_DIM, (a + 1) * HEAD_DIM) for a in range(ah)]
    scale = HEAD_DIM ** -0.5
    ng = S // qt

    def body(q_ref, k_ref, v_ref, ext_ref, do_ref, dq_ref, dko_ref, dvo_ref, db_ref, bias, dk_ref, dv_ref):
        g = pl.program_id(1)

        @pl.when(g == 0)
        def _():
            for a in range(ah):
                bias[a] = _att_bias(ext_ref[a], qt, kb, ne)
            dk_ref[...] = jnp.zeros_like(dk_ref)
            dv_ref[...] = jnp.zeros_like(dv_ref)
            db_ref[...] = jnp.zeros_like(db_ref)

        start = pl.multiple_of(g * qt, qt)
        q16 = q_ref[...].astype(bf16)
        kb16 = k_ref[pl.ds(start, kb), :]
        vb16 = v_ref[pl.ds(start, kb), :]
        do16 = do_ref[...].astype(bf16)
        qk = [_dot(q16[:, sl], kb16[:, sl], NT) for sl in heads]
        dpn = [_dot(do16[:, sl], vb16[:, sl], NT) for sl in heads]
        pn = [_att_probs(qk[a], bias[a], start, qt, pad, kb) for a in range(ah)]
        ds = [pn[a] * (dpn[a] - jnp.sum(dpn[a] * pn[a], axis=-1, keepdims=True)) for a in range(ah)]
        ds16 = [d.astype(bf16) for d in ds]
        dv_ref[pl.ds(start, kb), :] += jnp.concatenate([_dot(pn[a].astype(bf16), do16[:, sl], TN) for a, sl in enumerate(heads)], axis=1)
        dq_ref[...] = (jnp.concatenate([_dot(ds16[a], kb16[:, sl], NN) for a, sl in enumerate(heads)], axis=1) * scale).astype(bf16)
        dk_ref[pl.ds(start, kb), :] += jnp.concatenate([_dot(ds16[a], q16[:, sl], TN) for a, sl in enumerate(heads)], axis=1) * scale
        for a in range(ah):
            db_ref[a] += ds[a]

        @pl.when(g == ng - 1)
        def _():
            dko_ref[...] = dk_ref[pl.ds(pad, S), :].astype(bf16)
            dvo_ref[...] = dv_ref[pl.ds(pad, S), :].astype(bf16)

    kv_out = pl.BlockSpec((S, cw), lambda h, g: (0, h))
    return _grid_call(
        body,
        rider,
        name="attn_bwd",
        grid=(H // ah, S // qt),
        in_specs=[q_spec, kv_spec, kv_spec, ext_spec,
                  pl.BlockSpec((qt, cw), lambda h, g: (g, H // ah + h))],
        out_specs=[row_spec, kv_out, kv_out, pl.BlockSpec((ah, qt, kb), lambda h, g: (h, 0, 0))],
        out_shape=[
            jax.ShapeDtypeStruct((S, W), bf16),
            jax.ShapeDtypeStruct((S, W), bf16),
            jax.ShapeDtypeStruct((S, W), bf16),
            jax.ShapeDtypeStruct((H, qt, kb), f32),
        ],
        scratch_shapes=[pltpu.VMEM((ah, qt, kb), f32), pltpu.VMEM((S + pad, cw), f32), pltpu.VMEM((S + pad, cw), f32)],
        args=(proj, kp, vp, ext, dycat),
    )


def _bias_onehot(qt, ne, nrel_pad):
    m = lax.broadcasted_iota(jnp.int32, (nrel_pad, ne), 1)
    r = lax.broadcasted_iota(jnp.int32, (nrel_pad, ne), 0)
    rel = LEFT_CHUNKS * CHUNK + qt - 1 - m
    hot = (r == jnp.clip(rel, -REL_CLIP, REL_CLIP) + REL_CLIP) & (m < ne - 1)
    return hot.astype(f32)


def _bias_ext(rel_bias_pad, onehot):
    H, _, nr = rel_bias_pad.shape
    ne = onehot.shape[1]

    def body(rb_ref, oh_ref, o_ref):
        o_ref[0] = _dot(rb_ref[0], oh_ref[...], NN, HIGHEST)

    return pl.pallas_call(
        body,
        name="bias_ext",
        grid=(H,),
        in_specs=[pl.BlockSpec((1, 1, nr), lambda h: (h, 0, 0)), pl.BlockSpec((nr, ne), lambda h: (0, 0))],
        out_specs=pl.BlockSpec((1, 1, ne), lambda h: (h, 0, 0)),
        out_shape=jax.ShapeDtypeStruct((H, 1, ne), f32),
        compiler_params=pltpu.CompilerParams(dimension_semantics=("parallel",)),
    )(rel_bias_pad, onehot)


def _bias_bwd(dbias, onehot):
    H, qt, kb = dbias.shape
    nr, ne = onehot.shape

    def body(d_ref, oh_ref, o_ref):
        row = lax.broadcasted_iota(jnp.int32, (qt, qt), 0)
        col = lax.broadcasted_iota(jnp.int32, (qt, qt), 1)
        flip = (row + col == qt - 1).astype(f32)
        x = _dot(flip, d_ref[0], NN, HIGHEST)
        x = jnp.concatenate([x, jnp.zeros((qt, ne - kb), f32)], axis=1)
        de = jnp.sum(pltpu.roll(x, 0, 1, stride=1, stride_axis=0), axis=0, keepdims=True)
        o_ref[0] = _dot(de, oh_ref[...], NT, HIGHEST)

    return pl.pallas_call(
        body,
        name="bias_bwd",
        grid=(H,),
        in_specs=[pl.BlockSpec((1, qt, kb), lambda h: (h, 0, 0)), pl.BlockSpec((nr, ne), lambda h: (0, 0))],
        out_specs=pl.BlockSpec((1, 1, nr), lambda h: (h, 0, 0)),
        out_shape=jax.ShapeDtypeStruct((H, 1, nr), f32),
        compiler_params=pltpu.CompilerParams(dimension_semantics=("parallel",)),
    )(dbias, onehot)


def _place():
    x, y, c = lax.axis_index("x"), lax.axis_index("y"), lax.axis_index("c")
    return x, y, c


def _flip(x, y, k):
    return (1 - x if k & 2 else x), (1 - y if k & 1 else y)


def _region(ref, shard_shape, axis, j, half):
    R, C = shard_shape
    if axis == 0:
        if half is None:
            return ref.at[pl.ds(j * R, R), :]
        return ref.at[pl.ds(j * R + half * (R // 2), R // 2), :]
    if half is None:
        return ref.at[:, pl.ds(j * C, C)]
    return ref.at[pl.ds(half * (R // 2), R // 2), pl.ds(j * C, C)]


class _Remote:
    def __init__(self, src, dst, send_sem, recv_sem, dev):
        self.src, self.dst, self.sems, self.dev = src, dst, (send_sem, recv_sem), dev

    def _copy(self, src, dst):
        return pltpu.make_async_remote_copy(src_ref=src, dst_ref=dst, send_sem=self.sems[0], recv_sem=self.sems[1],
                                            device_id=self.dev, device_id_type=MESH)

    def start(self):
        rows = self.src.shape[0]
        n = RDMA_PIECES if rows % (RDMA_PIECES * 2 * SUBLANES) == 0 else 1
        for p in range(n):
            part = pl.ds(p * (rows // n), rows // n)
            self._copy(self.src.at[part], self.dst.at[part]).start()

    def wait_recv(self):
        self._copy(self.src, self.dst).wait_recv()

    def wait_send(self):
        self._copy(self.src, self.dst).wait_send()


def _remote(src, dst, send_sem, recv_sem, dev):
    return _Remote(src, dst, send_sem, recv_sem, dev)


def _cast_place(shard, axis, where, *, name):
    R, C = shard.shape
    tr = _div(R, 256, 2 * SUBLANES)
    nr = R // tr
    full = (N_CHIPS * R, C) if axis == 0 else (R, N_CHIPS * C)
    omap = (lambda r, s: (s[0] * nr + r, 0)) if axis == 0 else (lambda r, s: (r, s[0]))

    def body(s_ref, x_ref, o_ref):
        o_ref[...] = x_ref[...].astype(bf16)

    return pl.pallas_call(
        body,
        name=name,
        grid_spec=pltpu.PrefetchScalarGridSpec(
            num_scalar_prefetch=1,
            grid=(nr,),
            in_specs=[pl.BlockSpec((tr, C), lambda r, s: (r, 0))],
            out_specs=pl.BlockSpec((tr, C), omap),
        ),
        out_shape=jax.ShapeDtypeStruct(full, bf16),
        compiler_params=pltpu.CompilerParams(dimension_semantics=("parallel",)),
    )(where, shard)


class _Rider(NamedTuple):
    ins: list
    out_shapes: list
    aliases: dict
    sems: list
    first: Callable
    last: Callable


def _run_rider(rider, *, name):
    ni, no = len(rider.ins), len(rider.out_shapes)

    def body(*refs):
        ins, outs, sems = refs[:ni], refs[ni:ni + no], refs[ni + no:]
        rider.first(ins, outs, sems)
        rider.last(ins, outs, sems)

    return pl.pallas_call(
        body,
        name=name,
        in_specs=[_hbm()] * ni,
        out_specs=[_hbm()] * no,
        out_shape=rider.out_shapes,
        input_output_aliases=rider.aliases,
        scratch_shapes=rider.sems,
    )(*rider.ins)


def _start_all(copies):
    def first(ins, outs, sems):
        for cp in copies(ins, outs, sems):
            cp.start()

    def last(ins, outs, sems):
        cps = copies(ins, outs, sems)
        for cp in cps:
            cp.wait_recv()
        for cp in cps:
            cp.wait_send()

    return first, last


def _relay_copy(buf, shard_shape, axis, send_sem, recv_sem):
    x, y, c = _place()
    first = c == 0
    sx, sy = jnp.where(first, 1 - x, x), jnp.where(first, y, 1 - y)
    dx, dy = jnp.where(first, x, 1 - x), jnp.where(first, 1 - y, y)
    part = _region(buf, shard_shape, axis, 2 * sx + sy, c)
    return _remote(part, part, send_sem, recv_sem, (dx, dy, c))


def _mm_while_gathering(a, full, shard_shape, where, *, tm, tn, name):
    S, K = a.shape
    R, Cs = shard_shape
    N = full.shape[1]
    tps, nj, ni = Cs // tn, N // tn, S // tm
    assert R == K and Cs % tn == 0 and S % tm == 0

    def block(j, chip):
        return jnp.bitwise_xor(chip, j // tps) * tps + j % tps

    def body(s_ref, a_ref, w_in, o_ref, w_ref, buf, fsem, send1, recv1, send2, recv2):
        j, i = pl.program_id(0), pl.program_id(1)
        x, y, c = _place()
        chip = 2 * x + y

        def ici(k):
            px, py = _flip(x, y, k)
            mine = _region(w_ref, shard_shape, 1, chip, c)
            return _remote(mine, mine, send1.at[k - 1], recv1.at[k - 1], (px, py, c))

        def handed(k, half):
            px, py = _flip(x, y, k)
            land = _region(w_ref, shard_shape, 1, 2 * px + py, half)
            return land, _remote(land, land, send2.at[k - 1], recv2.at[k - 1], (x, y, 1 - c))

        def tile(jj):
            col = pl.multiple_of(block(jj, chip) * tn, LANES)
            return pltpu.make_async_copy(w_ref.at[:, pl.ds(col, tn)], buf.at[jj % 2], fsem.at[jj % 2])

        def relay():
            return _relay_copy(w_ref, shard_shape, 1, send1.at[2], recv1.at[2])

        def landed(k):
            land, forward = handed(k, c)
            _remote(land, land, send1.at[k - 1], recv1.at[k - 1], (x, y, c)).wait_recv()
            forward.start()

        @pl.when((j == 0) & (i == 0))
        def _():
            for k in (1, 2):
                ici(k).start()
            tile(j).start()

        @pl.when(i == 0)
        def _():
            tile(j).wait()

        @pl.when((i == ni - 1) & (j + 1 < nj))
        def _():
            @pl.when(j + 1 == tps)
            def _():
                landed(1)
                landed(2)
                relay().start()
                handed(1, 1 - c)[1].wait_recv()

            @pl.when(j + 1 == 2 * tps)
            def _():
                handed(2, 1 - c)[1].wait_recv()

            @pl.when(j + 1 == 3 * tps)
            def _():
                landed(3)
                handed(3, 1 - c)[1].wait_recv()

            tile(j + 1).start()

        o_ref[...] = _dot(a_ref[...], buf[j % 2], NN)

        @pl.when((j == nj - 1) & (i == ni - 1))
        def _():
            ici(1).wait_send()
            ici(2).wait_send()
            relay().wait_send()
            for k in (1, 2, 3):
                handed(k, c)[1].wait_send()

    sems = pltpu.SemaphoreType.DMA((3,))
    spec = pltpu.PrefetchScalarGridSpec(
        num_scalar_prefetch=1,
        grid=(nj, ni),
        in_specs=[pl.BlockSpec((tm, K), lambda j, i, s: (i, 0)), _hbm()],
        out_specs=[pl.BlockSpec((tm, tn), lambda j, i, s: (i, block(j, s[0]))), _hbm()],
        scratch_shapes=[pltpu.VMEM((2, K, tn), full.dtype), pltpu.SemaphoreType.DMA((2,)), sems, sems, sems, sems],
    )
    return pl.pallas_call(
        body,
        name=name,
        grid_spec=spec,
        out_shape=[jax.ShapeDtypeStruct((S, N), f32), jax.ShapeDtypeStruct(full.shape, full.dtype)],
        input_output_aliases={2: 1},
        compiler_params=pltpu.CompilerParams(dimension_semantics=("arbitrary", "arbitrary")),
    )(where, a, full)


class _GatherJob(NamedTuple):
    buf: int
    shard_shape: tuple
    axis: int
    split: bool
    rels: tuple
    hand_over: bool
    relay: bool = False


def _gather_rider(bufs, jobs):
    def copies(ins, outs, sems):
        x, y, c = _place()
        cps = []
        for n, job in enumerate(jobs):
            if job.relay:
                cps.append(_relay_copy(outs[job.buf], job.shard_shape, job.axis, sems[0].at[3 * n + 2], sems[1].at[3 * n + 2]))
                continue
            half = c if job.split else None
            for k in job.rels:
                px, py = _flip(x, y, k)
                chip, dev = (2 * px + py, (x, y, 1 - c)) if job.hand_over else (2 * x + y, (px, py, c))
                cps.append(_remote(_region(ins[job.buf], job.shard_shape, job.axis, chip, half),
                                   _region(outs[job.buf], job.shard_shape, job.axis, chip, half),
                                   sems[0].at[3 * n + k - 1], sems[1].at[3 * n + k - 1], dev))
        return cps

    assert all(job.split or not job.hand_over for job in jobs)
    sems = pltpu.SemaphoreType.DMA((3 * len(jobs),))
    return _Rider(list(bufs), [jax.ShapeDtypeStruct(b.shape, b.dtype) for b in bufs], {m: m for m in range(len(bufs))},
                  [sems, sems], *_start_all(copies))


def _chips_rider(ts, rels=(1, 2, 3), landing=None):
    n = len(ts)

    def copies(ins, outs, sems):
        x, y, c = _place()
        cps = []
        for m in range(n):
            for k in rels:
                px, py = _flip(x, y, k)
                cps.append(_remote(ins[m].at[2 * px + py], outs[m].at[k - 1], sems[0].at[3 * m + k - 1], sems[1].at[3 * m + k - 1], (px, py, c)))
        return cps

    sems = pltpu.SemaphoreType.DMA((3 * n,))
    return _Rider(list(ts) + list(landing or []), [jax.ShapeDtypeStruct((3,) + t.shape[1:], t.dtype) for t in ts],
                  {n + m: m for m in range(n)} if landing else {}, [sems, sems], *_start_all(copies))


def _sibling_rider(dws, shard_shapes, axes):
    n = len(dws)

    def copies(ins, outs, sems):
        x, y, c = _place()
        return [_remote(_region(ins[m], shard_shapes[m], axes[m], j, 1 - c), outs[m].at[j],
                        sems[0].at[N_CHIPS * m + j], sems[1].at[N_CHIPS * m + j], (x, y, 1 - c))
                for m in range(n) for j in range(N_CHIPS)]

    sems = pltpu.SemaphoreType.DMA((N_CHIPS * n,))
    return _Rider(list(dws), [jax.ShapeDtypeStruct((N_CHIPS, s[0] // 2, s[1]), d.dtype) for s, d in zip(shard_shapes, dws)], {},
                  [sems, sems], *_start_all(copies))


def _join_rider(gs):
    n = len(gs)

    def copies(ins, outs, sems):
        x, y, c = _place()
        cps = []
        for m in range(n):
            hr = gs[m].shape[0] // 2
            cps.append(_remote(ins[m].at[pl.ds(c * hr, hr), :], outs[m].at[pl.ds(c * hr, hr), :], sems[0].at[m], sems[1].at[m], (x, y, 1 - c)))
        return cps

    sems = pltpu.SemaphoreType.DMA((n,))
    return _Rider(list(gs), [jax.ShapeDtypeStruct(g.shape, g.dtype) for g in gs], {m: m for m in range(n)}, [sems, sems], *_start_all(copies))


def _both(r1, r2):
    i1, o1, s1 = len(r1.ins), len(r1.out_shapes), len(r1.sems)
    aliases = dict(r1.aliases)
    aliases.update({i1 + i: o1 + o for i, o in r2.aliases.items()})

    def first(ins, outs, sems):
        r1.first(ins[:i1], outs[:o1], sems[:s1])
        r2.first(ins[i1:], outs[o1:], sems[s1:])

    def last(ins, outs, sems):
        r1.last(ins[:i1], outs[:o1], sems[:s1])
        r2.last(ins[i1:], outs[o1:], sems[s1:])

    return _Rider(r1.ins + r2.ins, r1.out_shapes + r2.out_shapes, aliases, r1.sems + r2.sems, first, last)


def _pair_sum(dw, got, shard_shape, axis, where, *, name):
    R, C = shard_shape
    hr = R // 2
    tr = _div(hr, 256, 2 * SUBLANES)
    nr = hr // tr
    dmap = (lambda j, r, s: ((2 * j + s[1]) * nr + r, 0)) if axis == 0 else (lambda j, r, s: (s[1] * nr + r, j))
    slot = pl.BlockSpec((1, tr, C), lambda j, r, s: (j, r, 0))

    def body(s_ref, d_ref, g_ref, o_ref):
        o_ref[0] = (d_ref[...].astype(f32) + g_ref[0].astype(f32)).astype(bf16)

    return pl.pallas_call(
        body,
        name=name,
        grid_spec=pltpu.PrefetchScalarGridSpec(
            num_scalar_prefetch=1,
            grid=(N_CHIPS, nr),
            in_specs=[pl.BlockSpec((tr, C), dmap), slot],
            out_specs=slot,
        ),
        out_shape=jax.ShapeDtypeStruct((N_CHIPS, hr, C), bf16),
        compiler_params=pltpu.CompilerParams(dimension_semantics=("parallel", "parallel")),
    )(where, dw, got)


def _chip_sum(pair, others, where, *, name):
    _, hr, C = pair.shape
    tr = _div(hr, 256, 2 * SUBLANES)
    nr = hr // tr

    def body(s_ref, p_ref, o_ref, g_ref):
        g_ref[...] = p_ref[0].astype(f32) + o_ref[0].astype(f32) + o_ref[1].astype(f32) + o_ref[2].astype(f32)

    return pl.pallas_call(
        body,
        name=name,
        grid_spec=pltpu.PrefetchScalarGridSpec(
            num_scalar_prefetch=1,
            grid=(nr,),
            in_specs=[pl.BlockSpec((1, tr, C), lambda r, s: (s[0], r, 0)), pl.BlockSpec((3, tr, C), lambda r, s: (0, r, 0))],
            out_specs=pl.BlockSpec((tr, C), lambda r, s: (s[1] * nr + r, 0)),
        ),
        out_shape=jax.ShapeDtypeStruct((2 * hr, C), f32),
        compiler_params=pltpu.CompilerParams(dimension_semantics=("parallel",)),
    )(where, pair, others)


def _allreduce_small(v):
    rows = v.shape[0]

    def body(v_ref, o_ref, gath, send, recv):
        x, y, c = _place()
        me = 4 * x + 2 * y + c
        gath[pl.ds(me, 1)] = v_ref[...][None]
        cps = []
        for k in range(1, N_DEV):
            peer = (1 - x if k & 4 else x, 1 - y if k & 2 else y, 1 - c if k & 1 else c)
            cp = _remote(v_ref, gath.at[me], send.at[k - 1], recv.at[k - 1], peer)
            cp.start()
            cps.append(cp)
        for cp in cps:
            cp.wait_recv()
        for cp in cps:
            cp.wait_send()
        acc = gath[0]
        for d in range(1, N_DEV):
            acc = acc + gath[d]
        o_ref[...] = acc

    return pl.pallas_call(
        body,
        name="small_allreduce",
        in_specs=[_vmem()],
        out_specs=_vmem(),
        out_shape=jax.ShapeDtypeStruct(v.shape, f32),
        scratch_shapes=[pltpu.VMEM((N_DEV, rows, LANES), f32), pltpu.SemaphoreType.DMA((N_DEV - 1,)), pltpu.SemaphoreType.DMA((N_DEV - 1,))],
    )(v)


def _adamw_math(w, g, m, v):
    m = ADAM_B1 * m + (1.0 - ADAM_B1) * g
    v = ADAM_B2 * v + (1.0 - ADAM_B2) * (g * g)
    m_hat = m / (1.0 - ADAM_B1 ** ADAM_STEP)
    v_hat = v / (1.0 - ADAM_B2 ** ADAM_STEP)
    delta = -ADAM_LR * (m_hat / (jnp.sqrt(v_hat) + ADAM_EPS) + ADAM_WD * w)
    return delta, m, v


def _adamw(w, g, m, v, *, name):
    R, C = w.shape
    tr = _div(R, 128, SUBLANES)

    def body(w_ref, g_ref, m_ref, v_ref, go_ref, d_ref, mo_ref, vo_ref):
        gv = g_ref[...]
        d, mn, vn = _adamw_math(w_ref[...], gv, m_ref[...], v_ref[...])
        go_ref[...] = gv
        d_ref[...] = d
        mo_ref[...] = mn
        vo_ref[...] = vn

    spec = pl.BlockSpec((tr, C), lambda r: (r, 0))
    return pl.pallas_call(
        body,
        name=name,
        grid=(R // tr,),
        in_specs=[spec] * 4,
        out_specs=[spec] * 4,
        out_shape=[jax.ShapeDtypeStruct((R, C), f32)] * 4,
        compiler_params=pltpu.CompilerParams(dimension_semantics=("parallel",)),
    )(w, g, m, v)


def _adamw_sparsecore(w, g, m, v, *, name):
    R, C = w.shape
    cb = _div(C, 1408, LANES)
    ncb = C // cb
    per_tile = (R // SUBLANES) * ncb // SC_TILES
    assert R % SUBLANES == 0 and (R // SUBLANES) * ncb % SC_TILES == 0

    def body(w_hbm, g_hbm, m_hbm, v_hbm, go_hbm, d_hbm, mo_hbm, vo_hbm, wb, gb, mb, vb, db):
        tile = lax.axis_index("sc_tile") * 2 + lax.axis_index("sc_core")

        @pl.loop(0, per_tile)
        def _(t):
            n = tile * per_tile + t
            slab = (pl.ds((n // ncb) * SUBLANES, SUBLANES), pl.ds((n % ncb) * cb, cb))
            pltpu.sync_copy(w_hbm.at[slab], wb)
            pltpu.sync_copy(g_hbm.at[slab], gb)
            pltpu.sync_copy(m_hbm.at[slab], mb)
            pltpu.sync_copy(v_hbm.at[slab], vb)

            @pl.loop(0, SUBLANES)
            def _(rr):
                @pl.loop(0, cb, step=SC_LANES)
                def _(cc):
                    s = (pl.ds(rr, 1), pl.ds(cc, SC_LANES))
                    d, mn, vn = _adamw_math(wb.at[s][...], gb.at[s][...], mb.at[s][...], vb.at[s][...])
                    db.at[s][...] = d
                    mb.at[s][...] = mn
                    vb.at[s][...] = vn

            pltpu.sync_copy(gb, go_hbm.at[slab])
            pltpu.sync_copy(db, d_hbm.at[slab])
            pltpu.sync_copy(mb, mo_hbm.at[slab])
            pltpu.sync_copy(vb, vo_hbm.at[slab])

    buf = pltpu.VMEM((SUBLANES, cb), f32)
    return pl.kernel(
        body,
        name=name,
        out_type=[jax.ShapeDtypeStruct((R, C), f32)] * 4,
        mesh=plsc.VectorSubcoreMesh(core_axis_name="sc_core", subcore_axis_name="sc_tile"),
        scratch_types=[buf] * 5,
    )(w, g, m, v)


def _pack(arrs):
    flat = jnp.concatenate([a.reshape(-1).astype(f32) for a in arrs])
    tile = SUBLANES * LANES
    total = -(-flat.shape[0] // tile) * tile
    return jnp.pad(flat, (0, total - flat.shape[0])).reshape(total // LANES, LANES)


def _unpack(buf, shapes):
    flat = buf.reshape(-1)
    out, off = [], 0
    for s in shapes:
        size = int(np.prod(s))
        out.append(flat[off:off + size].reshape(s))
        off += size
    return out


def kernel(x, p, norm_mix, w_in, lb_logits, hg_norm, rel_bias, w_out, norm_ffn, w_up, conv_w, conv_b, w_down, norm_ple, w_ple_gate, w_ple_proj, final_norm, loss_target, m_norm_mix, m_w_in, m_lb_logits, m_hg_norm, m_rel_bias, m_w_out, m_norm_ffn, m_w_up, m_conv_w, m_conv_b, m_w_down, m_norm_ple, m_w_ple_gate, m_w_ple_proj, m_final_norm, v_norm_mix, v_w_in, v_lb_logits, v_hg_norm, v_rel_bias, v_w_out, v_norm_ffn, v_w_up, v_conv_w, v_conv_b, v_w_down, v_norm_ple, v_w_ple_gate, v_w_ple_proj, v_final_norm):
    S, D = x.shape[1], x.shape[2]
    xv, pv, tgt = x[0], p[0, 0], loss_target[0]
    W = (w_in.shape[2] * N_CHIPS) // 7
    H = W // HEAD_DIM
    F = w_down.shape[1] * N_CHIPS
    tc = _div(F // 2, 1408, LANES)
    nj = F // tc
    jx, jy = lax.axis_index("x"), lax.axis_index("y")
    chip = 2 * jx + jy

    big = [w_in[0], w_out[0], w_up[0], w_down[0], w_ple_gate[0], w_ple_proj[0]]
    big_axes = [1, 0, 1, 0, 0, 1]
    where = jnp.stack([chip, lax.axis_index("c")]).astype(jnp.int32)
    cw_pad = jnp.pad(conv_w[0], ((0, SUBLANES - conv_w.shape[1]), (0, 0)))
    cw_mine = lax.dynamic_update_slice(jnp.zeros((N_CHIPS * SUBLANES, cw_pad.shape[1]), f32), cw_pad, (chip * SUBLANES, 0))
    placed = [_cast_place(b, ax, where, name=f"cast_place_{i}") for i, (b, ax) in enumerate(zip(big, big_axes))]
    shard_shapes = [b.shape for b in big]

    every = (1, 2, 3)

    def job(buf, i, rels, hand_over=False, relay=False):
        if i is None:
            return _GatherJob(buf, cw_pad.shape, 0, False, rels, hand_over)
        return _GatherJob(buf, shard_shapes[i], big_axes[i], True, rels, hand_over, relay)

    tm = _div(S, 1024, LANES)

    a1 = _rms_fwd(xv, norm_mix, name="rms_mix")
    cs_in = shard_shapes[0][1]
    tn_in = _div(cs_in, 1024, LANES)
    proj, W_in = _mm_while_gathering(a1, placed[0], shard_shapes[0], where, tm=tm, tn=tn_in, name="mm_in")
    qt, pad, kb, ne = _att_dims(S)
    nrel = rel_bias.shape[2]
    nrel_pad = -(-nrel // LANES) * LANES
    onehot = _bias_onehot(qt, ne, nrel_pad)
    rb_pad = jnp.pad(rel_bias[0], ((0, 0), (0, nrel_pad - nrel)))[:, None, :]
    ext = _bias_ext(rb_pad, onehot)
    kp, vp = _kv_padded(proj, W, pad)
    y_att, W_out_i, W_up_i = _attn_fwd(
        proj, kp, vp, ext, W, rider=_gather_rider([placed[1], placed[2]], [job(0, 1, every), job(1, 2, (1, 2))]))
    y_hg, o_hg, st, W_out, W_up_i = _hgrn_fwd(
        proj, lb_logits, hg_norm, W,
        rider=_gather_rider([W_out_i, W_up_i], [job(0, 1, every, True), job(1, 2, (1, 2), True), job(1, 2, (3,), relay=True)]))
    ycat = jnp.concatenate([y_hg, y_att], axis=1)

    tn_d = _div(D, 512, LANES)
    h1, W_up = _mm(ycat, W_out, dims="nn", tm=tm, tn=tn_d, tk=D, out_dtype=f32, res=xv, name="mm_out",
                   rider=_gather_rider([W_up_i], [job(0, 2, (3,), True)]))
    a2 = _rms_fwd(h1, norm_ffn, name="rms_ffn")
    perm = lambda b: (b % 2) * nj + b // 2
    u, W_down_i, W_pg_i, W_pp_i, cw_all = _mm(
        a2, W_up, dims="nn", tm=tm, tn=tc, tk=D, out_dtype=f32, name="mm_up", b_idx=lambda i, j, k: (k, perm(j)),
        rider=_gather_rider([placed[3], placed[4], placed[5], cw_mine], [job(0, 3, every), job(1, 4, every), job(2, 5, every), job(3, None, every)]))
    ncw = conv_w.shape[1]
    cw_full = jnp.transpose(cw_all.reshape(N_CHIPS, SUBLANES, -1)[:, :ncw], (1, 0, 2)).reshape(ncw, -1)
    cb = conv_b
    mact, W_down, W_pg, W_pp = _convact_fwd(
        u, cw_full, cb, tc, rider=_gather_rider([W_down_i, W_pg_i, W_pp_i], [job(0, 3, every, True), job(1, 4, every, True), job(2, 5, every, True)]))
    tn_w = _div(D, 1024, LANES)
    h2 = _mm(mact, W_down, dims="nn", tm=_div(S, 512, LANES), tn=tn_w, tk=F, out_dtype=f32, res=h1, name="mm_down", cols_outer=True)
    a3 = _rms_fwd(h2, norm_ple, name="rms_ple")
    zg = _mm(a3, W_pg, dims="nn", tm=tm, tn=tn_d, tk=D, out_dtype=f32, name="mm_ple_gate")
    p16 = pv.astype(bf16)
    pp = _mm(p16, W_pp, dims="nn", tm=tm, tn=tn_d, tk=pv.shape[1], out_dtype=f32, name="mm_ple_proj")
    dh3, dzg, dpp, loss_part, d_fn = _tail(h2, zg, pp, tgt, final_norm.reshape(1, D))

    pair = {}

    def siblings(idx, grads):
        return _sibling_rider(grads, [shard_shapes[i] for i in idx], [big_axes[i] for i in idx])

    def pair_up(idx, grads, got):
        for i, d, g in zip(idx, grads, got):
            pair[i] = _pair_sum(d, g, shard_shapes[i], big_axes[i], where, name=f"grad_pair_sum_{i}")

    def reduced_half(i, landed):
        return _chip_sum(pair[i], landed, where, name=f"grad_chip_sum_{i}")

    tk_s = S
    dW_pp = _mm(p16, dpp, dims="tn", tm=pv.shape[1], tn=tn_w, tk=tk_s, out_dtype=bf16, name="mm_d_ple_proj")
    dW_pg = _mm(a3, dzg, dims="tn", tm=tn_w, tn=tn_w, tk=tk_s, out_dtype=bf16, name="mm_d_ple_gate")
    da3, *got = _mm(dzg, W_pg, dims="nt", tm=tm, tn=_div(D, 1024, LANES), tk=D, out_dtype=f32, name="mm_da3",
                    rider=siblings([4, 5], [dW_pg, dW_pp]))
    pair_up([4, 5], [dW_pg, dW_pp], got)
    dh2, dh2b, d_nple = _rms_bwd(h2, norm_ple, da3, dh3, name="rms_ple_bwd")
    dm, land4, land5 = _mm(dh2b, W_down, dims="nt", tm=tm, tn=_div(F, 512, LANES), tk=D, out_dtype=f32, name="mm_dm",
                           rider=_chips_rider([pair[4], pair[5]]))
    dW_down = _mm(mact, dh2b, dims="tn", tm=_div(F, 1408, LANES), tn=tn_d, tk=tk_s, out_dtype=bf16, name="mm_d_down")
    du, dcw_g, dcw_v, dcb_g, dcb_v, got3 = _convact_bwd(u, dm, cw_full, cb, tc, rider=siblings([3], [dW_down]))
    pair_up([3], [dW_down], [got3])
    dW_up, land3 = _mm(a2, du, dims="tn", tm=_div(D, 1024, LANES), tn=tc, tk=tk_s, out_dtype=bf16, name="mm_d_up",
                       o_idx=lambda i, j, k: (i, perm(j)), rider=_chips_rider([pair[3]]))
    da2, got2 = _mm(du, W_up, dims="nt", tm=tm, tn=D, tk=tc, out_dtype=f32, name="mm_da2",
                    b_idx=lambda i, j, k: (j, perm(k)), rider=siblings([2], [dW_up]))
    pair_up([2], [dW_up], [got2])
    dh1, dh1b, d_nffn = _rms_bwd(h1, norm_ffn, da2, dh2, name="rms_ffn_bwd")
    dycat = _mm(dh1b, W_out, dims="nt", tm=tm, tn=_div(D, 1024, LANES), tk=D, out_dtype=f32, name="mm_dycat")
    dW_out = _mm(ycat, dh1b, dims="tn", tm=tn_w, tn=tn_w, tk=tk_s, out_dtype=bf16, name="mm_d_out")
    dq_att, dk_att, dv_att, dbias, land2, got1 = _attn_bwd(
        proj, kp, vp, ext, dycat, W, rider=_both(_chips_rider([pair[2]], rels=(1, 2)), siblings([1], [dW_out])))
    pair_up([1], [dW_out], [got1])
    d_hg4, d_lbl, d_hgn, land2 = _hgrn_bwd(proj, o_hg, st, dycat, lb_logits, hg_norm, W,
                                           rider=_chips_rider([pair[2]], rels=(3,), landing=[land2]))
    d_rb = _bias_bwd(dbias, onehot)[:, 0, :nrel]
    dproj = [d_hg4, dq_att, dk_att, dv_att]
    tw = _div(W, 1024, LANES)
    dW_in, land1, g_up, g_down = _mm(a1, dproj, dims="tn", tm=tn_w, tn=_div(W, 512, LANES), tk=tk_s, out_dtype=bf16, name="mm_d_in",
                                     rider=_both(_chips_rider([pair[1]]), _join_rider([reduced_half(2, land2), reduced_half(3, land3)])))
    pair_up([0], [dW_in], _run_rider(siblings([0], [dW_in]), name="grads_to_sibling_0"))
    halves = [reduced_half(i, land) for i, land in ((1, land1), (4, land4), (5, land5))]
    da1, land0, g_out, g_pg, g_pp = _mm(dproj, W_in, dims="nt", tm=tm, tn=D, tk=tw, out_dtype=f32, name="mm_da1",
                                        rider=_both(_chips_rider([pair[0]]), _join_rider(halves)))
    grad_x, _, d_nmix = _rms_bwd(xv, norm_mix, da1, dh1, name="rms_mix_bwd")
    (g_in,) = _run_rider(_join_rider([reduced_half(0, land0)]), name="grads_join_w_in")
    g_big = [g_in, g_out, g_up, g_down, g_pg, g_pp]

    d_cw = jnp.concatenate([dcw_g, dcw_v], axis=1)
    d_cb = jnp.concatenate([dcb_g, dcb_v], axis=1)
    small_parts = [loss_part[:, :1], d_nmix, d_lbl, d_hgn, d_rb, d_nffn, d_cw, d_cb, d_nple, d_fn]
    small_shapes = [(1, 1), (1, D), lb_logits.shape, hg_norm.shape, (H, nrel), (1, D), (ncw, 2 * F), (1, 2 * F), (1, D), (1, D)]
    red = _unpack(_allreduce_small(_pack(small_parts)), small_shapes)
    loss = red[0].reshape(())
    g_nmix, g_lbl, g_hgn, g_rb, g_nffn, g_cw_all, g_cb, g_nple, g_fn = red[1:]
    csh = conv_w.shape[2]
    g_cw = lax.dynamic_slice(g_cw_all, (0, chip * csh), (ncw, csh))
    small_g = [g_nmix, g_lbl, g_hgn, g_rb[None], g_nffn, g_cw[None], g_cb, g_nple, g_fn.reshape(D)]
    small_w = [norm_mix, lb_logits, hg_norm, rel_bias, norm_ffn, conv_w, conv_b, norm_ple, final_norm]
    small_m = [m_norm_mix, m_lb_logits, m_hg_norm, m_rel_bias, m_norm_ffn, m_conv_w, m_conv_b, m_norm_ple, m_final_norm]
    small_v = [v_norm_mix, v_lb_logits, v_hg_norm, v_rel_bias, v_norm_ffn, v_conv_w, v_conv_b, v_norm_ple, v_final_norm]
    shapes_s = [w.shape for w in small_w]
    _, sd, sm, sv = _adamw(_pack(small_w), _pack(small_g), _pack(small_m), _pack(small_v), name="adamw_small")
    small_g = [g.reshape(s) for g, s in zip(small_g, shapes_s)]
    small_d, small_nm, small_nv = _unpack(sd, shapes_s), _unpack(sm, shapes_s), _unpack(sv, shapes_s)

    big_m = [m_w_in[0], m_w_out[0], m_w_up[0], m_w_down[0], m_w_ple_gate[0], m_w_ple_proj[0]]
    big_v = [v_w_in[0], v_w_out[0], v_w_up[0], v_w_down[0], v_w_ple_gate[0], v_w_ple_proj[0]]
    big_g, big_d, big_nm, big_nv = [], [], [], []
    for i in range(6):
        update = _adamw_sparsecore if i in (2, 3) else _adamw
        g_, d_, m_, v_ = update(big[i], g_big[i], big_m[i], big_v[i], name=f"adamw_{i}")
        big_g.append(g_[None])
        big_d.append(d_[None])
        big_nm.append(m_[None])
        big_nv.append(v_[None])
    g_big = big_g

    def order(sm_list, bg_list):
        s, b = sm_list, bg_list
        return [s[0], b[0], s[1], s[2], s[3], b[1], s[4], b[2], s[5], s[6], b[3], s[7], b[4], b[5], s[8]]

    return (loss, grad_x[None], *order(small_g, g_big), *order(small_d, big_d), *order(small_nm, big_nm), *order(small_nv, big_nv))
```

```python
import functools
from typing import Callable, NamedTuple

import jax
import jax.numpy as jnp
import numpy as np
from jax import lax
from jax.experimental import pallas as pl
from jax.experimental.pallas import tpu as pltpu
from jax.experimental.pallas import tpu_sc as plsc

f32 = jnp.float32
bf16 = jnp.bfloat16

CHUNK = 64
HEAD_DIM = 128
LEFT_CHUNKS = 8
REL_CLIP = 128
EPS = 1e-6
HG_SUB = 16
HG_HEADS_PER_STEP = 8
ATT_Q_ROWS = 256
ATT_HEADS_PER_STEP = 2
ADAM_LR, ADAM_B1, ADAM_B2, ADAM_EPS, ADAM_WD, ADAM_STEP = 0.001, 0.9, 0.999, 1e-08, 0.01, 10
LANES = 128
SUBLANES = 8
N_CHIPS = 4
N_DEV = 8
RDMA_PIECES = 4
SC_TILES = 32
SC_LANES = 16
MESH = pl.DeviceIdType.MESH
NEG = float(np.finfo(np.float32).min)

NN = (((1,), (0,)), ((), ()))
NT = (((1,), (1,)), ((), ()))
TN = (((0,), (0,)), ((), ()))
HIGHEST = lax.Precision.HIGHEST


def _dot(a, b, dims, precision=None):
    return lax.dot_general(a, b, dims, preferred_element_type=f32, precision=precision)


def _sigmoid(v):
    return 1.0 / (1.0 + jnp.exp(-v))


def _div(n, pref, mult):
    best = None
    d = mult
    while d <= min(n, pref):
        if n % d == 0:
            best = d
        d += mult
    return best if best is not None else n


def _hbm():
    return pl.BlockSpec(memory_space=pltpu.HBM)


def _vmem():
    return pl.BlockSpec(memory_space=pltpu.VMEM)


def _mm(a, b, *, dims, tm, tn, tk, out_dtype, name, res=None, a_idx=None, b_idx=None, o_idx=None, out_shape=None, rider=None,
        cols_outer=False):
    a_list = list(a) if isinstance(a, (list, tuple)) else [a]
    b_list = list(b) if isinstance(b, (list, tuple)) else [b]
    na, nb = len(a_list), len(b_list)
    if dims == "nn":
        assert na == 1 and nb == 1
        (M, K), (_, N) = a.shape, b.shape
        a_blk, b_blk, dn = (tm, tk), (tk, tn), NN
        a_def, b_def = (lambda i, j, k: (i, k)), (lambda i, j, k: (k, j))
    elif dims == "nt":
        assert nb == 1
        M, K, N = a_list[0].shape[0], sum(p.shape[1] for p in a_list), b.shape[0]
        a_blk, b_blk, dn = (tm, tk), (tn, tk), NT
        a_def, b_def = (lambda i, j, k: (i, k)), (lambda i, j, k: (j, k))
    else:
        assert na == 1
        (K, M), N = a.shape, sum(p.shape[1] for p in b_list)
        a_blk, b_blk, dn = (tk, tm), (tk, tn), TN
        a_def, b_def = (lambda i, j, k: (k, i)), (lambda i, j, k: (k, j))
    assert M % tm == 0 and N % tn == 0 and K % tk == 0, (name, M, N, K, tm, tn, tk)
    nk = K // tk
    has_res = res is not None
    counts = [p.shape[1] // tk for p in a_list] if na > 1 else [p.shape[1] // tn for p in b_list]
    starts = [sum(counts[:p]) for p in range(len(counts))]

    def body(*refs):
        a_refs, b_refs, rest = refs[:na], refs[na:na + nb], refs[na + nb:]
        res_ref = rest[0] if has_res else None
        o_ref = rest[1] if has_res else rest[0]
        k = pl.program_id(2)
        j = pl.program_id(0) if cols_outer else pl.program_id(1)

        def finish(acc):
            if has_res:
                acc = acc + res_ref[...]
            o_ref[...] = acc.astype(out_dtype)

        def consume(part):
            if nk == 1:
                finish(part)
                return
            acc_ref = refs[-1]

            @pl.when(k == 0)
            def _():
                acc_ref[...] = part

            @pl.when(k > 0)
            def _():
                acc_ref[...] += part

            @pl.when(k == nk - 1)
            def _():
                finish(acc_ref[...])

        if len(counts) == 1:
            consume(_dot(a_refs[0][...].astype(bf16), b_refs[0][...].astype(bf16), dn))
        else:
            at = k if na > 1 else j
            for p, (s, n) in enumerate(zip(starts, counts)):
                a_ref, b_ref = a_refs[p if na > 1 else 0], b_refs[p if nb > 1 else 0]
                pl.when((at >= s) & (at < s + n))(
                    functools.partial(lambda ar, br: consume(_dot(ar[...].astype(bf16), br[...].astype(bf16), dn)), a_ref, b_ref))

    def ordered(custom, default):
        f = custom or default
        return (lambda j, i, k: f(i, j, k)) if cols_outer else f

    def piece_map(default, p):
        s, n = starts[p], counts[p]
        if na > 1:
            return lambda i, j, k: default(i, j, jnp.clip(k - s, 0, n - 1))
        return lambda i, j, k: default(i, jnp.clip(j - s, 0, n - 1), k)

    o_map = ordered(o_idx, lambda i, j, k: (i, j))
    a_specs = [pl.BlockSpec(a_blk, ordered(a_idx, piece_map(a_def, p) if na > 1 else a_def)) for p in range(na)]
    b_specs = [pl.BlockSpec(b_blk, ordered(b_idx, piece_map(b_def, p) if nb > 1 else b_def)) for p in range(nb)]
    in_specs = a_specs + b_specs
    args = a_list + b_list
    if has_res:
        in_specs.append(pl.BlockSpec((tm, tn), o_map))
        args.append(res)
    outs = _grid_call(
        body,
        rider,
        name=name,
        grid=(N // tn, M // tm, nk) if cols_outer else (M // tm, N // tn, nk),
        in_specs=in_specs,
        out_specs=[pl.BlockSpec((tm, tn), o_map)],
        out_shape=[jax.ShapeDtypeStruct(out_shape or (M, N), out_dtype)],
        scratch_shapes=[pltpu.VMEM((tm, tn), f32)] if nk > 1 else [],
        args=args,
    )
    return outs[0] if rider is None else outs


def _rms_fwd(xv, g, *, name):
    S, D = xv.shape
    ts = _div(S, 512, SUBLANES)

    def body(x_ref, g_ref, o_ref):
        v = x_ref[...]
        r = lax.rsqrt(jnp.mean(v * v, axis=-1, keepdims=True) + EPS)
        o_ref[...] = (v * r * g_ref[...]).astype(bf16)

    return pl.pallas_call(
        body,
        name=name,
        grid=(S // ts,),
        in_specs=[pl.BlockSpec((ts, D), lambda i: (i, 0)), pl.BlockSpec((1, D), lambda i: (0, 0))],
        out_specs=pl.BlockSpec((ts, D), lambda i: (i, 0)),
        out_shape=jax.ShapeDtypeStruct((S, D), bf16),
        compiler_params=pltpu.CompilerParams(dimension_semantics=("parallel",)),
    )(xv, g)


def _rms_bwd(xv, g, dy, dres, *, name, rider=None):
    S, D = xv.shape
    ts = _div(S, 256, SUBLANES)

    def body(x_ref, g_ref, dy_ref, dres_ref, dx_ref, dxb_ref, dg_ref):
        i = pl.program_id(0)
        v = x_ref[...]
        r = lax.rsqrt(jnp.mean(v * v, axis=-1, keepdims=True) + EPS)
        vn = v * r
        d = dy_ref[...]
        part = jnp.sum(d * vn, axis=0, keepdims=True)

        @pl.when(i == 0)
        def _():
            dg_ref[...] = part

        @pl.when(i > 0)
        def _():
            dg_ref[...] += part

        t = d * g_ref[...]
        dx = dres_ref[...] + r * (t - vn * jnp.mean(t * vn, axis=-1, keepdims=True))
        dx_ref[...] = dx
        dxb_ref[...] = dx.astype(bf16)

    row = pl.BlockSpec((ts, D), lambda i: (i, 0))
    vec = pl.BlockSpec((1, D), lambda i: (0, 0))
    return _grid_call(
        body,
        rider,
        name=name,
        grid=(S // ts,),
        in_specs=[row, vec, row, row],
        out_specs=[row, row, vec],
        out_shape=[jax.ShapeDtypeStruct((S, D), f32), jax.ShapeDtypeStruct((S, D), bf16), jax.ShapeDtypeStruct((1, D), f32)],
        scratch_shapes=[],
        args=(xv, g, dy, dres),
    )


def _tail(h2, zg, pp, tgt, fn):
    S, D = h2.shape
    ts = _div(S, 256, SUBLANES)

    def body(h_ref, z_ref, p_ref, t_ref, fn_ref, dh_ref, dz_ref, dp_ref, loss_ref, dfn_ref):
        i = pl.program_id(0)
        sg = _sigmoid(z_ref[...])
        ppv = p_ref[...]
        h3 = h_ref[...] + sg * ppv
        r = lax.rsqrt(jnp.mean(h3 * h3, axis=-1, keepdims=True) + EPS)
        hn = h3 * r
        fnv = fn_ref[...]
        e = hn * fnv - t_ref[...]
        lpart = 0.5 * jnp.sum(jnp.mean(e * e, axis=-1, keepdims=True), axis=0, keepdims=True)
        dy = e * (1.0 / D)
        dpart = jnp.sum(dy * hn, axis=0, keepdims=True)

        @pl.when(i == 0)
        def _():
            loss_ref[...] = jnp.broadcast_to(lpart, loss_ref.shape)
            dfn_ref[...] = dpart

        @pl.when(i > 0)
        def _():
            loss_ref[...] += jnp.broadcast_to(lpart, loss_ref.shape)
            dfn_ref[...] += dpart

        t = dy * fnv
        dh3 = r * (t - hn * jnp.mean(t * hn, axis=-1, keepdims=True))
        dh_ref[...] = dh3
        dz_ref[...] = (dh3 * ppv * sg * (1.0 - sg)).astype(bf16)
        dp_ref[...] = (dh3 * sg).astype(bf16)

    row = pl.BlockSpec((ts, D), lambda i: (i, 0))
    vec = pl.BlockSpec((1, D), lambda i: (0, 0))
    one = pl.BlockSpec((1, LANES), lambda i: (0, 0))
    return pl.pallas_call(
        body,
        name="ple_final_loss",
        grid=(S // ts,),
        in_specs=[row, row, row, row, vec],
        out_specs=[row, row, row, one, vec],
        out_shape=[
            jax.ShapeDtypeStruct((S, D), f32),
            jax.ShapeDtypeStruct((S, D), bf16),
            jax.ShapeDtypeStruct((S, D), bf16),
            jax.ShapeDtypeStruct((1, LANES), f32),
            jax.ShapeDtypeStruct((1, D), f32),
        ],
        compiler_params=pltpu.CompilerParams(dimension_semantics=("arbitrary",)),
    )(h2, zg, pp, tgt, fn)


def _shift_down(v, halo, k):
    r = pltpu.roll(v, k, 0)
    hr = pltpu.roll(halo, k, 0)
    row = lax.broadcasted_iota(jnp.int32, hr.shape, 0)
    top = jnp.where(row < k, hr, r[0:SUBLANES])
    return top if v.shape[0] == SUBLANES else jnp.concatenate([top, r[SUBLANES:]], axis=0)


def _shift_up(v, halo, k):
    n = v.shape[0]
    r = pltpu.roll(v, n - k, 0)
    hr = pltpu.roll(halo, SUBLANES - k, 0)
    row = lax.broadcasted_iota(jnp.int32, hr.shape, 0)
    bot = jnp.where(row >= SUBLANES - k, hr, r[n - SUBLANES:])
    return jnp.concatenate([r[: n - SUBLANES], bot], axis=0)


def _conv_specs(S, F, tc, ts):
    nj = F // tc
    rows8 = ts // SUBLANES
    main = pl.BlockSpec((ts, 2 * tc), lambda j, i: (i, j))
    prev = pl.BlockSpec((SUBLANES, 2 * tc), lambda j, i: (jnp.maximum(i * rows8 - 1, 0), j))
    nxt = pl.BlockSpec((SUBLANES, 2 * tc), lambda j, i: (jnp.minimum((i + 1) * rows8, S // SUBLANES - 1), j))
    wg = pl.BlockSpec((3, tc), lambda j, i: (0, j))
    wv = pl.BlockSpec((3, tc), lambda j, i: (0, nj + j))
    bg = pl.BlockSpec((1, tc), lambda j, i: (0, j))
    bv = pl.BlockSpec((1, tc), lambda j, i: (0, nj + j))
    half = pl.BlockSpec((ts, tc), lambda j, i: (i, j))
    return nj, main, prev, nxt, wg, wv, bg, bv, half


def _conv_pre(u_ref, h_ref, wg_ref, wv_ref, bg_ref, bv_ref, tc):
    i = pl.program_id(1)
    u = u_ref[...]
    halo = jnp.where(i == 0, 0.0, h_ref[...])
    u1 = _shift_down(u, halo, 1)
    u2 = _shift_down(u, halo, 2)
    w = jnp.concatenate([wg_ref[...], wv_ref[...]], axis=1)
    b = jnp.concatenate([bg_ref[...], bv_ref[...]], axis=1)
    uc = b + w[0:1] * u2 + w[1:2] * u1 + w[2:3] * u
    return u, u1, u2, uc[:, :tc], uc[:, tc:]


def _convact_fwd(u, cw, cb, tc, rider=None):
    S, F2 = u.shape
    F = F2 // 2
    ts = _div(S, 512, SUBLANES)
    nj, main, prev, nxt, wg, wv, bg, bv, half = _conv_specs(S, F, tc, ts)

    def body(u_ref, h_ref, wg_ref, wv_ref, bg_ref, bv_ref, m_ref):
        _, _, _, g, v = _conv_pre(u_ref, h_ref, wg_ref, wv_ref, bg_ref, bv_ref, tc)
        m_ref[...] = (g * _sigmoid(g) * v).astype(bf16)

    outs = _grid_call(
        body,
        rider,
        name="convact_fwd",
        grid=(nj, S // ts),
        in_specs=[main, prev, wg, wv, bg, bv],
        out_specs=[half],
        out_shape=[jax.ShapeDtypeStruct((S, F), bf16)],
        scratch_shapes=[],
        args=(u, u, cw, cw, cb, cb),
    )
    return outs[0] if rider is None else outs


def _convact_bwd(u, dm, cw, cb, tc, rider=None):
    S, F2 = u.shape
    F = F2 // 2
    ts = _div(S, 512, SUBLANES)
    nj, main, prev, nxt, wg, wv, bg, bv, half = _conv_specs(S, F, tc, ts)
    ni = S // ts
    half_nxt = pl.BlockSpec((SUBLANES, tc), lambda j, i: (jnp.minimum((i + 1) * (ts // SUBLANES), S // SUBLANES - 1), j))

    def act_bwd(g, v, d):
        sg = _sigmoid(g)
        return jnp.concatenate([d * v * sg * (1.0 + g * (1.0 - sg)), d * g * sg], axis=1)

    def body(u_ref, h_ref, n_ref, dm_ref, dmn_ref, wg_ref, wv_ref, bg_ref, bv_ref, du_ref, dwg_ref, dwv_ref, dbg_ref, dbv_ref):
        i = pl.program_id(1)
        u0, u1, u2, g, v = _conv_pre(u_ref, h_ref, wg_ref, wv_ref, bg_ref, bv_ref, tc)
        duc = act_bwd(g, v, dm_ref[...])
        w = jnp.concatenate([wg_ref[...], wv_ref[...]], axis=1)
        b = jnp.concatenate([bg_ref[...], bv_ref[...]], axis=1)
        un = n_ref[...]
        tail = u0[ts - SUBLANES:]
        ucn = b + w[0:1] * _shift_down(un, tail, 2) + w[1:2] * _shift_down(un, tail, 1) + w[2:3] * un
        ducn = jnp.where(i == ni - 1, 0.0, act_bwd(ucn[:, :tc], ucn[:, tc:], dmn_ref[...]))
        du_ref[...] = (w[2:3] * duc + w[1:2] * _shift_up(duc, ducn, 1) + w[0:1] * _shift_up(duc, ducn, 2)).astype(bf16)
        db = jnp.sum(duc, axis=0, keepdims=True)
        dw = jnp.concatenate(
            [jnp.sum(duc * u2, axis=0, keepdims=True), jnp.sum(duc * u1, axis=0, keepdims=True), jnp.sum(duc * u0, axis=0, keepdims=True)],
            axis=0,
        )

        @pl.when(i == 0)
        def _():
            dwg_ref[...] = dw[:, :tc]
            dwv_ref[...] = dw[:, tc:]
            dbg_ref[...] = db[:, :tc]
            dbv_ref[...] = db[:, tc:]

        @pl.when(i > 0)
        def _():
            dwg_ref[...] += dw[:, :tc]
            dwv_ref[...] += dw[:, tc:]
            dbg_ref[...] += db[:, :tc]
            dbv_ref[...] += db[:, tc:]

    w_out = pl.BlockSpec((3, tc), lambda j, i: (0, j))
    b_out = pl.BlockSpec((1, tc), lambda j, i: (0, j))
    return _grid_call(
        body,
        rider,
        name="convact_bwd",
        grid=(nj, ni),
        in_specs=[main, prev, nxt, half, half_nxt, wg, wv, bg, bv],
        out_specs=[main, w_out, w_out, b_out, b_out],
        out_shape=[
            jax.ShapeDtypeStruct((S, F2), bf16),
            jax.ShapeDtypeStruct((3, F), f32),
            jax.ShapeDtypeStruct((3, F), f32),
            jax.ShapeDtypeStruct((1, F), f32),
            jax.ShapeDtypeStruct((1, F), f32),
        ],
        scratch_shapes=[],
        args=(u, u, u, dm, dm, cw, cw, cb, cb),
    )


def _hg_gates(qp, fp, lbl):
    lb = _sigmoid(lbl[0:1] - lbl[1:2])
    sig = _sigmoid(fp)
    sigm = _sigmoid(-fp)
    f = lb + (1.0 - lb) * sig
    k = (1.0 - lb) * sigm
    sq = _sigmoid(qp)
    qf = qp * sq
    row = lax.broadcasted_iota(jnp.int32, (CHUNK, CHUNK), 0)
    col = lax.broadcasted_iota(jnp.int32, (CHUNK, CHUNK), 1)
    b = _dot((row >= col).astype(f32), jnp.log(f), NN, HIGHEST)
    return lb, sig, sigm, f, k, sq, qf, b


def _hg_block(qf, k, b, I):
    r0, n = I * HG_SUB, (I + 1) * HG_SUB
    base = b[r0 - 1:r0] if I > 0 else jnp.zeros_like(b[0:1])
    eq = jnp.exp(b[r0:n] - base)
    ek = jnp.exp(base - b[0:n])
    qt16 = (qf[r0:n] * eq).astype(bf16)
    kt16 = (k[0:n] * ek).astype(bf16)
    row = lax.broadcasted_iota(jnp.int32, (HG_SUB, n), 0) + r0
    col = lax.broadcasted_iota(jnp.int32, (HG_SUB, n), 1)
    return r0, n, eq, ek, qt16, kt16, col <= row


def _per_head(v, hp, fn):
    return jnp.concatenate([fn(v[:, p * HEAD_DIM:(p + 1) * HEAD_DIM]) for p in range(hp)], axis=1)


def _head_mean(v, hp):
    return _per_head(v, hp, lambda t: jnp.broadcast_to(jnp.mean(t, axis=-1, keepdims=True), t.shape))


def _pad_rows(v, rows):
    return v if v.shape[0] == rows else jnp.concatenate([v, jnp.zeros((rows - v.shape[0], v.shape[1]), v.dtype)], axis=0)


def _hgrn_specs(S, W, hp, reverse):
    nc = S // CHUNK
    ngrp = W // (hp * HEAD_DIM)
    cw = hp * HEAD_DIM
    cidx = (lambda c: nc - 1 - c) if reverse else (lambda c: c)

    def proj(off):
        return pl.BlockSpec((CHUNK, cw), lambda h, c: (cidx(c), off * ngrp + h))

    act = pl.BlockSpec((CHUNK, cw), lambda h, c: (cidx(c), h))
    lbl = pl.BlockSpec((2, cw), lambda h, c: (0, h))
    gn = pl.BlockSpec((1, HEAD_DIM), lambda h, c: (0, 0))
    st = pl.BlockSpec((hp, 1, HEAD_DIM, HEAD_DIM), lambda h, c: (h, cidx(c), 0, 0))
    return nc, ngrp, proj, act, lbl, gn, st


def _grid_call(body, rider, *, name, grid, in_specs, out_specs, out_shape, scratch_shapes, args):
    params = pltpu.CompilerParams(dimension_semantics=("arbitrary",) * len(grid))
    if rider is None:
        return pl.pallas_call(body, name=name, grid=grid, in_specs=in_specs, out_specs=out_specs, out_shape=out_shape,
                              scratch_shapes=scratch_shapes, compiler_params=params)(*args)
    n_in, n_out, n_scr = len(in_specs), len(out_specs), len(scratch_shapes)
    ri, ro = len(rider.ins), len(rider.out_shapes)

    def wrapped(*refs):
        a, b = n_in, n_in + ri
        c, d = b + n_out, b + n_out + ro
        e = d + n_scr
        ids = [pl.program_id(t) for t in range(len(grid))]
        is_first = functools.reduce(jnp.logical_and, [i == 0 for i in ids])
        is_last = functools.reduce(jnp.logical_and, [i == g - 1 for i, g in zip(ids, grid)])

        @pl.when(is_first)
        def _():
            rider.first(refs[a:b], refs[c:d], refs[e:])

        body(*refs[:a], *refs[b:c], *refs[d:e])

        @pl.when(is_last)
        def _():
            rider.last(refs[a:b], refs[c:d], refs[e:])

    return pl.pallas_call(
        wrapped,
        name=name,
        grid=grid,
        in_specs=list(in_specs) + [_hbm()] * ri,
        out_specs=list(out_specs) + [_hbm()] * ro,
        out_shape=list(out_shape) + list(rider.out_shapes),
        input_output_aliases={n_in + i: n_out + o for i, o in rider.aliases.items()},
        scratch_shapes=list(scratch_shapes) + list(rider.sems),
        compiler_params=params,
    )(*args, *rider.ins)


def _hgrn_fwd(proj, lb_logits, hg_norm, W, rider=None):
    S = proj.shape[0]
    hp = min(HG_HEADS_PER_STEP, W // HEAD_DIM)
    nc, ngrp, pspec, act, lbl_spec, gn_spec, st_spec = _hgrn_specs(S, W, hp, False)
    nb = CHUNK // HG_SUB

    def body(q_ref, f_ref, i_ref, g_ref, lbl_ref, gn_ref, y_ref, o_ref, st_ref, state):
        c = pl.program_id(1)

        @pl.when(c == 0)
        def _():
            state[...] = jnp.zeros_like(state)

        _, _, _, _, k, _, qf, b = _hg_gates(q_ref[...], f_ref[...], lbl_ref[...])
        v16 = i_ref[...].astype(bf16)
        a16 = (qf * jnp.exp(b)).astype(bf16)
        bl = b[CHUNK - 1:CHUNK]
        kd16 = (k * jnp.exp(bl - b)).astype(bf16)
        ebl = jnp.exp(bl)
        blocks = [_hg_block(qf, k, b, I) for I in range(nb)]
        heads = [slice(p * HEAD_DIM, (p + 1) * HEAD_DIM) for p in range(hp)]
        st_prev = state[...]
        st_ref[:, 0] = st_prev
        inter = [_dot(a16[:, sl], st_prev[p].astype(bf16), NT) for p, sl in enumerate(heads)]
        scores = [[_dot(qt16[:, sl], kt16[:, sl], NT) for sl in heads] for _, _, _, _, qt16, kt16, _ in blocks]
        intra = [[_dot(jnp.where(blk[6], sc, 0.0).astype(bf16), v16[0:blk[1], sl], NN) for sc, sl in zip(scs, heads)]
                 for blk, scs in zip(blocks, scores)]
        st_new = [st_prev[p] * ebl[:, sl] + _dot(v16[:, sl], kd16[:, sl], TN) for p, sl in enumerate(heads)]
        state[...] = jnp.stack(st_new)
        o = jnp.concatenate([inter[p] + jnp.concatenate([rows[p] for rows in intra], axis=0) for p in range(hp)], axis=1)
        o_ref[...] = o
        r = lax.rsqrt(_head_mean(o * o, hp) + EPS)
        gp = g_ref[...]
        y_ref[...] = (o * r * jnp.tile(gn_ref[...], (1, hp)) * (gp * _sigmoid(gp))).astype(bf16)

    H = W // HEAD_DIM
    return _grid_call(
        body,
        rider,
        name="hgrn_fwd",
        grid=(ngrp, nc),
        in_specs=[pspec(0), pspec(1), pspec(2), pspec(3), lbl_spec, gn_spec],
        out_specs=[act, act, st_spec],
        out_shape=[
            jax.ShapeDtypeStruct((S, W), bf16),
            jax.ShapeDtypeStruct((S, W), f32),
            jax.ShapeDtypeStruct((H, nc, HEAD_DIM, HEAD_DIM), f32),
        ],
        scratch_shapes=[pltpu.VMEM((hp, HEAD_DIM, HEAD_DIM), f32)],
        args=(proj, proj, proj, proj, lb_logits, hg_norm),
    )


def _hgrn_bwd(proj, o_hg, st, dycat, lb_logits, hg_norm, W, rider=None):
    S = proj.shape[0]
    hp = min(HG_HEADS_PER_STEP, W // HEAD_DIM)
    assert hp * HEAD_DIM == W, "the four gradients share one output block: all heads in one grid step"
    nc, ngrp, pspec, act, lbl_spec, gn_spec, st_spec = _hgrn_specs(S, W, hp, True)
    nb = CHUNK // HG_SUB

    def body(q_ref, f_ref, i_ref, g_ref, o_ref, st_ref, dy_ref, lbl_ref, gn_ref,
             d4_ref, dlbl_ref, dgn_ref, dstate):
        cw_ = hp * HEAD_DIM
        dq_ref, df_ref, di_ref, dg_ref = (d4_ref.at[:, pl.ds(t * cw_, cw_)] for t in range(4))
        h = pl.program_id(0)
        c = pl.program_id(1)

        @pl.when(c == 0)
        def _():
            dstate[...] = jnp.zeros_like(dstate)
            dlbl_ref[...] = jnp.zeros_like(dlbl_ref)

        @pl.when((c == 0) & (h == 0))
        def _():
            dgn_ref[...] = jnp.zeros_like(dgn_ref)

        row = lax.broadcasted_iota(jnp.int32, (CHUNK, CHUNK), 0)
        col = lax.broadcasted_iota(jnp.int32, (CHUNK, CHUNK), 1)
        upper = (col >= row).astype(f32)
        last_row = lax.broadcasted_iota(jnp.int32, (CHUNK, hp * HEAD_DIM), 0) == CHUNK - 1

        qp, fp = q_ref[...], f_ref[...]
        lb, sig, sigm, f, k, sq, qf, b = _hg_gates(qp, fp, lbl_ref[...])
        v16 = i_ref[...].astype(bf16)
        gnv = jnp.tile(gn_ref[...], (1, hp))
        gp = g_ref[...]
        sgg = _sigmoid(gp)
        gate = gp * sgg
        o = o_ref[...]
        r = lax.rsqrt(_head_mean(o * o, hp) + EPS)
        on = o * r
        dy = dy_ref[...]
        dg_ref[...] = (dy * on * gnv * (sgg * (1.0 + gp * (1.0 - sgg)))).astype(bf16)
        dgn_wide = jnp.sum(dy * on * gate, axis=0, keepdims=True)
        dgn_ref[...] += functools.reduce(jnp.add, [dgn_wide[:, p * HEAD_DIM:(p + 1) * HEAD_DIM] for p in range(hp)])
        don = dy * gnv * gate
        do16 = (r * (don - on * _head_mean(don * on, hp))).astype(bf16)
        eb = jnp.exp(b)
        A = qf * eb
        a16 = A.astype(bf16)
        bl = b[CHUNK - 1:CHUNK]
        ebl = jnp.exp(bl)
        ekd = jnp.exp(bl - b)
        kd = k * ekd
        kd16 = kd.astype(bf16)
        blocks = [_hg_block(qf, k, b, I) for I in range(nb)]

        heads = [slice(p * HEAD_DIM, (p + 1) * HEAD_DIM) for p in range(hp)]
        st_prev = st_ref[:, 0]
        dst_new = dstate[...]
        st16, dst16 = st_prev.astype(bf16), dst_new.astype(bf16)
        dA_h = [_dot(do16[:, sl], st16[p], NN) for p, sl in enumerate(heads)]
        dkd_h = [_dot(v16[:, sl], dst16[p], NN) for p, sl in enumerate(heads)]
        dv_h = [_dot(kd16[:, sl], dst16[p], NT) for p, sl in enumerate(heads)]
        dstate[...] = jnp.stack([dst_new[p] * ebl[:, sl] + _dot(do16[:, sl], a16[:, sl], TN) for p, sl in enumerate(heads)])
        dbl_h = [jnp.sum(dst_new[p] * st_prev[p], axis=0, keepdims=True) for p in range(hp)]
        sc_h = [[jnp.where(mask, _dot(qt16[:, sl], kt16[:, sl], NT), 0.0).astype(bf16) for sl in heads]
                for _, _, _, _, qt16, kt16, mask in blocks]
        dsc_h = [[jnp.where(mask, _dot(do16[r0:n, sl], v16[0:n, sl], NT), 0.0).astype(bf16) for sl in heads]
                 for r0, n, _, _, _, _, mask in blocks]
        dvi_h = [[_dot(sc, do16[blk[0]:blk[1], sl], TN) for sc, sl in zip(scs, heads)] for blk, scs in zip(blocks, sc_h)]
        dqt_h = [[_dot(dsc, blk[5][:, sl], NN) for dsc, sl in zip(dscs, heads)] for blk, dscs in zip(blocks, dsc_h)]
        dkt_h = [[_dot(dsc, blk[4][:, sl], TN) for dsc, sl in zip(dscs, heads)] for blk, dscs in zip(blocks, dsc_h)]
        dv_h = [functools.reduce(jnp.add, [dv_h[p]] + [_pad_rows(rows[p], CHUNK) for rows in dvi_h]) for p in range(hp)]

        dA, dkd = jnp.concatenate(dA_h, axis=1), jnp.concatenate(dkd_h, axis=1)
        dbl = jnp.concatenate(dbl_h, axis=1) * ebl + jnp.sum(dkd * kd, axis=0, keepdims=True)
        db = dA * A - dkd * kd + jnp.where(last_row, dbl, 0.0)
        dk = dkd * ekd
        dq_rows, db_rows = [], []
        for I, (r0, n, eq, ek, qt16, kt16, _) in enumerate(blocks):
            dqt, dkt = jnp.concatenate(dqt_h[I], axis=1), jnp.concatenate(dkt_h[I], axis=1)
            dq_rows.append(dqt * eq)
            dk = dk + _pad_rows(dkt * ek, CHUNK)
            db_rows.append(dqt * qt16.astype(f32))
            db = db - _pad_rows(dkt * kt16.astype(f32), CHUNK)
        dqf = dA * eb + jnp.concatenate(dq_rows, axis=0)
        db = db + jnp.concatenate(db_rows, axis=0)
        dlogf = _dot(upper, db, NN, HIGHEST)
        dfg = dlogf / f
        df_ref[...] = ((1.0 - lb) * sig * sigm * (dfg - dk)).astype(bf16)
        dlb = jnp.sum(sigm * (dfg - dk), axis=0, keepdims=True)
        dl0 = dlb * lb * (1.0 - lb)
        dlbl_ref[...] += jnp.concatenate([dl0, -dl0], axis=0)
        dq_ref[...] = (dqf * (sq * (1.0 + qp * (1.0 - sq)))).astype(bf16)
        di_ref[...] = jnp.concatenate(dv_h, axis=1).astype(bf16)

    cw = hp * HEAD_DIM
    dy_spec = pl.BlockSpec((CHUNK, cw), lambda h, c: (nc - 1 - c, h))
    return _grid_call(
        body,
        rider,
        name="hgrn_bwd",
        grid=(ngrp, nc),
        in_specs=[pspec(0), pspec(1), pspec(2), pspec(3), act, st_spec, dy_spec, lbl_spec, gn_spec],
        out_specs=[pl.BlockSpec((CHUNK, 4 * W), lambda h, c: (nc - 1 - c, 0)), lbl_spec, gn_spec],
        out_shape=[jax.ShapeDtypeStruct((S, 4 * W), bf16), jax.ShapeDtypeStruct((2, W), f32), jax.ShapeDtypeStruct((1, HEAD_DIM), f32)],
        scratch_shapes=[pltpu.VMEM((hp, HEAD_DIM, HEAD_DIM), f32)],
        args=(proj, proj, proj, proj, o_hg, st, dycat, lb_logits, hg_norm),
    )


def _att_dims(S):
    qt = ATT_Q_ROWS if S % ATT_Q_ROWS == 0 else CHUNK
    pad = LEFT_CHUNKS * CHUNK
    kb = pad + qt
    return qt, pad, kb, kb + qt


def _kv_padded(proj, W, pad):
    S = proj.shape[0]
    ts = _div(pad, 512, 2 * SUBLANES)
    assert S % ts == 0 and pad % ts == 0
    npad = pad // ts

    def body(k_ref, v_ref, ko_ref, vo_ref):
        i = pl.program_id(0)

        @pl.when(i < npad)
        def _():
            ko_ref[...] = jnp.zeros_like(ko_ref)
            vo_ref[...] = jnp.zeros_like(vo_ref)

        @pl.when(i >= npad)
        def _():
            ko_ref[...] = k_ref[...].astype(bf16)
            vo_ref[...] = v_ref[...].astype(bf16)

    out = pl.BlockSpec((ts, W), lambda i: (i, 0))
    return pl.pallas_call(
        body,
        name="kv_padded",
        grid=((S + pad) // ts,),
        in_specs=[pl.BlockSpec((ts, W), lambda i: (jnp.maximum(i - npad, 0), 5)), pl.BlockSpec((ts, W), lambda i: (jnp.maximum(i - npad, 0), 6))],
        out_specs=[out, out],
        out_shape=[jax.ShapeDtypeStruct((S + pad, W), bf16)] * 2,
        compiler_params=pltpu.CompilerParams(dimension_semantics=("parallel",)),
    )(proj, proj)


def _att_probs(qk, bias, start, qt, pad, kb):
    s = qk * (HEAD_DIM ** -0.5) + bias
    row = lax.broadcasted_iota(jnp.int32, (qt, kb), 0)
    col = lax.broadcasted_iota(jnp.int32, (qt, kb), 1)
    lo = jnp.bitwise_and(row, -CHUNK)
    ok = (col >= lo) & (col < lo + pad + CHUNK) & (col + start >= pad)
    s = jnp.where(ok, s, NEG)
    e = jnp.exp(s - jnp.max(s, axis=-1, keepdims=True))
    return e / jnp.sum(e, axis=-1, keepdims=True)


def _att_bias(ext_row, qt, kb, ne):
    e = jnp.broadcast_to(ext_row, (qt, ne))
    return pltpu.roll(e, ne - qt + 1, 1, stride=1, stride_axis=0)[:, :kb]


def _att_specs(S, H, ah, qt, pad, ne):
    cw = ah * HEAD_DIM
    q_spec = pl.BlockSpec((qt, cw), lambda h, g: (g, 4 * H // ah + h))
    kv_spec = pl.BlockSpec((S + pad, cw), lambda h, g: (0, h))
    ext_spec = pl.BlockSpec((ah, 1, ne), lambda h, g: (h, 0, 0))
    row_spec = pl.BlockSpec((qt, cw), lambda h, g: (g, h))
    return cw, q_spec, kv_spec, ext_spec, row_spec


def _attn_fwd(proj, kp, vp, ext, W, rider=None):
    S = proj.shape[0]
    H = W // HEAD_DIM
    ah = min(ATT_HEADS_PER_STEP, H)
    qt, pad, kb, ne = _att_dims(S)
    cw, q_spec, kv_spec, ext_spec, row_spec = _att_specs(S, H, ah, qt, pad, ne)
    heads = [slice(a * HEAD_DIM, (a + 1) * HEAD_DIM) for a in range(ah)]

    def body(q_ref, k_ref, v_ref, ext_ref, o_ref, bias):
        g = pl.program_id(1)

        @pl.when(g == 0)
        def _():
            for a in range(ah):
                bias[a] = _att_bias(ext_ref[a], qt, kb, ne)

        start = pl.multiple_of(g * qt, qt)
        q16 = q_ref[...].astype(bf16)
        kb16 = k_ref[pl.ds(start, kb), :]
        vb16 = v_ref[pl.ds(start, kb), :]
        qk = [_dot(q16[:, sl], kb16[:, sl], NT) for sl in heads]
        pn = [_att_probs(qk[a], bias[a], start, qt, pad, kb).astype(bf16) for a in range(ah)]
        o_ref[...] = jnp.concatenate([_dot(pn[a], vb16[:, sl], NN) for a, sl in enumerate(heads)], axis=1).astype(bf16)

    outs = _grid_call(
        body,
        rider,
        name="attn_fwd",
        grid=(H // ah, S // qt),
        in_specs=[q_spec, kv_spec, kv_spec, ext_spec],
        out_specs=[row_spec],
        out_shape=[jax.ShapeDtypeStruct((S, W), bf16)],
        scratch_shapes=[pltpu.VMEM((ah, qt, kb), f32)],
        args=(proj, kp, vp, ext),
    )
    return outs[0] if rider is None else outs


def _attn_bwd(proj, kp, vp, ext, dycat, W, rider=None):
    S = proj.shape[0]
    H = W // HEAD_DIM
    ah = min(ATT_HEADS_PER_STEP, H)
    qt, pad, kb, ne = _att_dims(S)
    cw, q_spec, kv_spec, ext_spec, row_spec = _att_specs(S, H, ah, qt, pad, ne)
    heads = [slice(a * HEAD_DIM, (a + 1) * HEAD_DIM) for a in range(ah)]
    scale = HEAD_DIM ** -0.5
    ng = S // qt

    def body(q_ref, k_ref, v_ref, ext_ref, do_ref, dq_ref, dko_ref, dvo_ref, db_ref, bias, dk_ref, dv_ref):
        g = pl.program_id(1)

        @pl.when(g == 0)
        def _():
            for a in range(ah):
                bias[a] = _att_bias(ext_ref[a], qt, kb, ne)
            dk_ref[...] = jnp.zeros_like(dk_ref)
            dv_ref[...] = jnp.zeros_like(dv_ref)
            db_ref[...] = jnp.zeros_like(db_ref)

        start = pl.multiple_of(g * qt, qt)
        q16 = q_ref[...].astype(bf16)
        kb16 = k_ref[pl.ds(start, kb), :]
        vb16 = v_ref[pl.ds(start, kb), :]
        do16 = do_ref[...].astype(bf16)
        qk = [_dot(q16[:, sl], kb16[:, sl], NT) for sl in heads]
        dpn = [_dot(do16[:, sl], vb16[:, sl], NT) for sl in heads]
        pn = [_att_probs(qk[a], bias[a], start, qt, pad, kb) for a in range(ah)]
        ds = [pn[a] * (dpn[a] - jnp.sum(dpn[a] * pn[a], axis=-1, keepdims=True)) for a in range(ah)]
        ds16 = [d.astype(bf16) for d in ds]
        dv_ref[pl.ds(start, kb), :] += jnp.concatenate([_dot(pn[a].astype(bf16), do16[:, sl], TN) for a, sl in enumerate(heads)], axis=1)
        dq_ref[...] = (jnp.concatenate([_dot(ds16[a], kb16[:, sl], NN) for a, sl in enumerate(heads)], axis=1) * scale).astype(bf16)
        dk_ref[pl.ds(start, kb), :] += jnp.concatenate([_dot(ds16[a], q16[:, sl], TN) for a, sl in enumerate(heads)], axis=1) * scale
        for a in range(ah):
            db_ref[a] += ds[a]

        @pl.when(g == ng - 1)
        def _():
            dko_ref[...] = dk_ref[pl.ds(pad, S), :].astype(bf16)
            dvo_ref[...] = dv_ref[pl.ds(pad, S), :].astype(bf16)

    kv_out = pl.BlockSpec((S, cw), lambda h, g: (0, h))
    return _grid_call(
        body,
        rider,
        name="attn_bwd",
        grid=(H // ah, S // qt),
        in_specs=[q_spec, kv_spec, kv_spec, ext_spec,
                  pl.BlockSpec((qt, cw), lambda h, g: (g, H // ah + h))],
        out_specs=[row_spec, kv_out, kv_out, pl.BlockSpec((ah, qt, kb), lambda h, g: (h, 0, 0))],
        out_shape=[
            jax.ShapeDtypeStruct((S, W), bf16),
            jax.ShapeDtypeStruct((S, W), bf16),
            jax.ShapeDtypeStruct((S, W), bf16),
            jax.ShapeDtypeStruct((H, qt, kb), f32),
        ],
        scratch_shapes=[pltpu.VMEM((ah, qt, kb), f32), pltpu.VMEM((S + pad, cw), f32), pltpu.VMEM((S + pad, cw), f32)],
        args=(proj, kp, vp, ext, dycat),
    )


def _bias_onehot(qt, ne, nrel_pad):
    m = lax.broadcasted_iota(jnp.int32, (nrel_pad, ne), 1)
    r = lax.broadcasted_iota(jnp.int32, (nrel_pad, ne), 0)
    rel = LEFT_CHUNKS * CHUNK + qt - 1 - m
    hot = (r == jnp.clip(rel, -REL_CLIP, REL_CLIP) + REL_CLIP) & (m < ne - 1)
    return hot.astype(f32)


def _bias_ext(rel_bias_pad, onehot):
    H, _, nr = rel_bias_pad.shape
    ne = onehot.shape[1]

    def body(rb_ref, oh_ref, o_ref):
        o_ref[0] = _dot(rb_ref[0], oh_ref[...], NN, HIGHEST)

    return pl.pallas_call(
        body,
        name="bias_ext",
        grid=(H,),
        in_specs=[pl.BlockSpec((1, 1, nr), lambda h: (h, 0, 0)), pl.BlockSpec((nr, ne), lambda h: (0, 0))],
        out_specs=pl.BlockSpec((1, 1, ne), lambda h: (h, 0, 0)),
        out_shape=jax.ShapeDtypeStruct((H, 1, ne), f32),
        compiler_params=pltpu.CompilerParams(dimension_semantics=("parallel",)),
    )(rel_bias_pad, onehot)


def _bias_bwd(dbias, onehot):
    H, qt, kb = dbias.shape
    nr, ne = onehot.shape

    def body(d_ref, oh_ref, o_ref):
        row = lax.broadcasted_iota(jnp.int32, (qt, qt), 0)
        col = lax.broadcasted_iota(jnp.int32, (qt, qt), 1)
        flip = (row + col == qt - 1).astype(f32)
        x = _dot(flip, d_ref[0], NN, HIGHEST)
        x = jnp.concatenate([x, jnp.zeros((qt, ne - kb), f32)], axis=1)
        de = jnp.sum(pltpu.roll(x, 0, 1, stride=1, stride_axis=0), axis=0, keepdims=True)
        o_ref[0] = _dot(de, oh_ref[...], NT, HIGHEST)

    return pl.pallas_call(
        body,
        name="bias_bwd",
        grid=(H,),
        in_specs=[pl.BlockSpec((1, qt, kb), lambda h: (h, 0, 0)), pl.BlockSpec((nr, ne), lambda h: (0, 0))],
        out_specs=pl.BlockSpec((1, 1, nr), lambda h: (h, 0, 0)),
        out_shape=jax.ShapeDtypeStruct((H, 1, nr), f32),
        compiler_params=pltpu.CompilerParams(dimension_semantics=("parallel",)),
    )(dbias, onehot)


def _place():
    x, y, c = lax.axis_index("x"), lax.axis_index("y"), lax.axis_index("c")
    return x, y, c


def _flip(x, y, k):
    return (1 - x if k & 2 else x), (1 - y if k & 1 else y)


def _region(ref, shard_shape, axis, j, half):
    R, C = shard_shape
    if axis == 0:
        if half is None:
            return ref.at[pl.ds(j * R, R), :]
        return ref.at[pl.ds(j * R + half * (R // 2), R // 2), :]
    if half is None:
        return ref.at[:, pl.ds(j * C, C)]
    return ref.at[pl.ds(half * (R // 2), R // 2), pl.ds(j * C, C)]


class _Remote:
    def __init__(self, src, dst, send_sem, recv_sem, dev):
        self.src, self.dst, self.sems, self.dev = src, dst, (send_sem, recv_sem), dev

    def _copy(self, src, dst):
        return pltpu.make_async_remote_copy(src_ref=src, dst_ref=dst, send_sem=self.sems[0], recv_sem=self.sems[1],
                                            device_id=self.dev, device_id_type=MESH)

    def start(self):
        rows = self.src.shape[0]
        n = RDMA_PIECES if rows % (RDMA_PIECES * 2 * SUBLANES) == 0 else 1
        for p in range(n):
            part = pl.ds(p * (rows // n), rows // n)
            self._copy(self.src.at[part], self.dst.at[part]).start()

    def wait_recv(self):
        self._copy(self.src, self.dst).wait_recv()

    def wait_send(self):
        self._copy(self.src, self.dst).wait_send()


def _remote(src, dst, send_sem, recv_sem, dev):
    return _Remote(src, dst, send_sem, recv_sem, dev)


def _cast_place(shard, axis, where, *, name):
    R, C = shard.shape
    tr = _div(R, 256, 2 * SUBLANES)
    nr = R // tr
    full = (N_CHIPS * R, C) if axis == 0 else (R, N_CHIPS * C)
    omap = (lambda r, s: (s[0] * nr + r, 0)) if axis == 0 else (lambda r, s: (r, s[0]))

    def body(s_ref, x_ref, o_ref):
        o_ref[...] = x_ref[...].astype(bf16)

    return pl.pallas_call(
        body,
        name=name,
        grid_spec=pltpu.PrefetchScalarGridSpec(
            num_scalar_prefetch=1,
            grid=(nr,),
            in_specs=[pl.BlockSpec((tr, C), lambda r, s: (r, 0))],
            out_specs=pl.BlockSpec((tr, C), omap),
        ),
        out_shape=jax.ShapeDtypeStruct(full, bf16),
        compiler_params=pltpu.CompilerParams(dimension_semantics=("parallel",)),
    )(where, shard)


class _Rider(NamedTuple):
    ins: list
    out_shapes: list
    aliases: dict
    sems: list
    first: Callable
    last: Callable


def _run_rider(rider, *, name):
    ni, no = len(rider.ins), len(rider.out_shapes)

    def body(*refs):
        ins, outs, sems = refs[:ni], refs[ni:ni + no], refs[ni + no:]
        rider.first(ins, outs, sems)
        rider.last(ins, outs, sems)

    return pl.pallas_call(
        body,
        name=name,
        in_specs=[_hbm()] * ni,
        out_specs=[_hbm()] * no,
        out_shape=rider.out_shapes,
        input_output_aliases=rider.aliases,
        scratch_shapes=rider.sems,
    )(*rider.ins)


def _start_all(copies):
    def first(ins, outs, sems):
        for cp in copies(ins, outs, sems):
            cp.start()

    def last(ins, outs, sems):
        cps = copies(ins, outs, sems)
        for cp in cps:
            cp.wait_recv()
        for cp in cps:
            cp.wait_send()

    return first, last


def _relay_copy(buf, shard_shape, axis, send_sem, recv_sem):
    x, y, c = _place()
    first = c == 0
    sx, sy = jnp.where(first, 1 - x, x), jnp.where(first, y, 1 - y)
    dx, dy = jnp.where(first, x, 1 - x), jnp.where(first, 1 - y, y)
    part = _region(buf, shard_shape, axis, 2 * sx + sy, c)
    return _remote(part, part, send_sem, recv_sem, (dx, dy, c))


def _mm_while_gathering(a, full, shard_shape, where, *, tm, tn, name):
    S, K = a.shape
    R, Cs = shard_shape
    N = full.shape[1]
    tps, nj, ni = Cs // tn, N // tn, S // tm
    assert R == K and Cs % tn == 0 and S % tm == 0

    def block(j, chip):
        return jnp.bitwise_xor(chip, j // tps) * tps + j % tps

    def body(s_ref, a_ref, w_in, o_ref, w_ref, buf, fsem, send1, recv1, send2, recv2):
        j, i = pl.program_id(0), pl.program_id(1)
        x, y, c = _place()
        chip = 2 * x + y

        def ici(k):
            px, py = _flip(x, y, k)
            mine = _region(w_ref, shard_shape, 1, chip, c)
            return _remote(mine, mine, send1.at[k - 1], recv1.at[k - 1], (px, py, c))

        def handed(k, half):
            px, py = _flip(x, y, k)
            land = _region(w_ref, shard_shape, 1, 2 * px + py, half)
            return land, _remote(land, land, send2.at[k - 1], recv2.at[k - 1], (x, y, 1 - c))

        def tile(jj):
            col = pl.multiple_of(block(jj, chip) * tn, LANES)
            return pltpu.make_async_copy(w_ref.at[:, pl.ds(col, tn)], buf.at[jj % 2], fsem.at[jj % 2])

        def relay():
            return _relay_copy(w_ref, shard_shape, 1, send1.at[2], recv1.at[2])

        def landed(k):
            land, forward = handed(k, c)
            _remote(land, land, send1.at[k - 1], recv1.at[k - 1], (x, y, c)).wait_recv()
            forward.start()

        @pl.when((j == 0) & (i == 0))
        def _():
            for k in (1, 2):
                ici(k).start()
            tile(j).start()

        @pl.when(i == 0)
        def _():
            tile(j).wait()

        @pl.when((i == ni - 1) & (j + 1 < nj))
        def _():
            @pl.when(j + 1 == tps)
            def _():
                landed(1)
                landed(2)
                relay().start()
                handed(1, 1 - c)[1].wait_recv()

            @pl.when(j + 1 == 2 * tps)
            def _():
                handed(2, 1 - c)[1].wait_recv()

            @pl.when(j + 1 == 3 * tps)
            def _():
                landed(3)
                handed(3, 1 - c)[1].wait_recv()

            tile(j + 1).start()

        o_ref[...] = _dot(a_ref[...], buf[j % 2], NN)

        @pl.when((j == nj - 1) & (i == ni - 1))
        def _():
            ici(1).wait_send()
            ici(2).wait_send()
            relay().wait_send()
            for k in (1, 2, 3):
                handed(k, c)[1].wait_send()

    sems = pltpu.SemaphoreType.DMA((3,))
    spec = pltpu.PrefetchScalarGridSpec(
        num_scalar_prefetch=1,
        grid=(nj, ni),
        in_specs=[pl.BlockSpec((tm, K), lambda j, i, s: (i, 0)), _hbm()],
        out_specs=[pl.BlockSpec((tm, tn), lambda j, i, s: (i, block(j, s[0]))), _hbm()],
        scratch_shapes=[pltpu.VMEM((2, K, tn), full.dtype), pltpu.SemaphoreType.DMA((2,)), sems, sems, sems, sems],
    )
    return pl.pallas_call(
        body,
        name=name,
        grid_spec=spec,
        out_shape=[jax.ShapeDtypeStruct((S, N), f32), jax.ShapeDtypeStruct(full.shape, full.dtype)],
        input_output_aliases={2: 1},
        compiler_params=pltpu.CompilerParams(dimension_semantics=("arbitrary", "arbitrary")),
    )(where, a, full)


class _GatherJob(NamedTuple):
    buf: int
    shard_shape: tuple
    axis: int
    split: bool
    rels: tuple
    hand_over: bool
    relay: bool = False


def _gather_rider(bufs, jobs):
    def copies(ins, outs, sems):
        x, y, c = _place()
        cps = []
        for n, job in enumerate(jobs):
            if job.relay:
                cps.append(_relay_copy(outs[job.buf], job.shard_shape, job.axis, sems[0].at[3 * n + 2], sems[1].at[3 * n + 2]))
                continue
            half = c if job.split else None
            for k in job.rels:
                px, py = _flip(x, y, k)
                chip, dev = (2 * px + py, (x, y, 1 - c)) if job.hand_over else (2 * x + y, (px, py, c))
                cps.append(_remote(_region(ins[job.buf], job.shard_shape, job.axis, chip, half),
                                   _region(outs[job.buf], job.shard_shape, job.axis, chip, half),
                                   sems[0].at[3 * n + k - 1], sems[1].at[3 * n + k - 1], dev))
        return cps

    assert all(job.split or not job.hand_over for job in jobs)
    sems = pltpu.SemaphoreType.DMA((3 * len(jobs),))
    return _Rider(list(bufs), [jax.ShapeDtypeStruct(b.shape, b.dtype) for b in bufs], {m: m for m in range(len(bufs))},
                  [sems, sems], *_start_all(copies))


def _chips_rider(ts, rels=(1, 2, 3), landing=None):
    n = len(ts)

    def copies(ins, outs, sems):
        x, y, c = _place()
        cps = []
        for m in range(n):
            for k in rels:
                px, py = _flip(x, y, k)
                cps.append(_remote(ins[m].at[2 * px + py], outs[m].at[k - 1], sems[0].at[3 * m + k - 1], sems[1].at[3 * m + k - 1], (px, py, c)))
        return cps

    sems = pltpu.SemaphoreType.DMA((3 * n,))
    return _Rider(list(ts) + list(landing or []), [jax.ShapeDtypeStruct((3,) + t.shape[1:], t.dtype) for t in ts],
                  {n + m: m for m in range(n)} if landing else {}, [sems, sems], *_start_all(copies))


def _sibling_rider(dws, shard_shapes, axes):
    n = len(dws)

    def copies(ins, outs, sems):
        x, y, c = _place()
        return [_remote(_region(ins[m], shard_shapes[m], axes[m], j, 1 - c), outs[m].at[j],
                        sems[0].at[N_CHIPS * m + j], sems[1].at[N_CHIPS * m + j], (x, y, 1 - c))
                for m in range(n) for j in range(N_CHIPS)]

    sems = pltpu.SemaphoreType.DMA((N_CHIPS * n,))
    return _Rider(list(dws), [jax.ShapeDtypeStruct((N_CHIPS, s[0] // 2, s[1]), d.dtype) for s, d in zip(shard_shapes, dws)], {},
                  [sems, sems], *_start_all(copies))


def _join_rider(gs):
    n = len(gs)

    def copies(ins, outs, sems):
        x, y, c = _place()
        cps = []
        for m in range(n):
            hr = gs[m].shape[0] // 2
            cps.append(_remote(ins[m].at[pl.ds(c * hr, hr), :], outs[m].at[pl.ds(c * hr, hr), :], sems[0].at[m], sems[1].at[m], (x, y, 1 - c)))
        return cps

    sems = pltpu.SemaphoreType.DMA((n,))
    return _Rider(list(gs), [jax.ShapeDtypeStruct(g.shape, g.dtype) for g in gs], {m: m for m in range(n)}, [sems, sems], *_start_all(copies))


def _both(r1, r2):
    i1, o1, s1 = len(r1.ins), len(r1.out_shapes), len(r1.sems)
    aliases = dict(r1.aliases)
    aliases.update({i1 + i: o1 + o for i, o in r2.aliases.items()})

    def first(ins, outs, sems):
        r1.first(ins[:i1], outs[:o1], sems[:s1])
        r2.first(ins[i1:], outs[o1:], sems[s1:])

    def last(ins, outs, sems):
        r1.last(ins[:i1], outs[:o1], sems[:s1])
        r2.last(ins[i1:], outs[o1:], sems[s1:])

    return _Rider(r1.ins + r2.ins, r1.out_shapes + r2.out_shapes, aliases, r1.sems + r2.sems, first, last)


def _pair_sum(dw, got, shard_shape, axis, where, *, name):
    R, C = shard_shape
    hr = R // 2
    tr = _div(hr, 256, 2 * SUBLANES)
    nr = hr // tr
    dmap = (lambda j, r, s: ((2 * j + s[1]) * nr + r, 0)) if axis == 0 else (lambda j, r, s: (s[1] * nr + r, j))
    slot = pl.BlockSpec((1, tr, C), lambda j, r, s: (j, r, 0))

    def body(s_ref, d_ref, g_ref, o_ref):
        o_ref[0] = (d_ref[...].astype(f32) + g_ref[0].astype(f32)).astype(bf16)

    return pl.pallas_call(
        body,
        name=name,
        grid_spec=pltpu.PrefetchScalarGridSpec(
            num_scalar_prefetch=1,
            grid=(N_CHIPS, nr),
            in_specs=[pl.BlockSpec((tr, C), dmap), slot],
            out_specs=slot,
        ),
        out_shape=jax.ShapeDtypeStruct((N_CHIPS, hr, C), bf16),
        compiler_params=pltpu.CompilerParams(dimension_semantics=("parallel", "parallel")),
    )(where, dw, got)


def _chip_sum(pair, others, where, *, name):
    _, hr, C = pair.shape
    tr = _div(hr, 256, 2 * SUBLANES)
    nr = hr // tr

    def body(s_ref, p_ref, o_ref, g_ref):
        g_ref[...] = p_ref[0].astype(f32) + o_ref[0].astype(f32) + o_ref[1].astype(f32) + o_ref[2].astype(f32)

    return pl.pallas_call(
        body,
        name=name,
        grid_spec=pltpu.PrefetchScalarGridSpec(
            num_scalar_prefetch=1,
            grid=(nr,),
            in_specs=[pl.BlockSpec((1, tr, C), lambda r, s: (s[0], r, 0)), pl.BlockSpec((3, tr, C), lambda r, s: (0, r, 0))],
            out_specs=pl.BlockSpec((tr, C), lambda r, s: (s[1] * nr + r, 0)),
        ),
        out_shape=jax.ShapeDtypeStruct((2 * hr, C), f32),
        compiler_params=pltpu.CompilerParams(dimension_semantics=("parallel",)),
    )(where, pair, others)


def _allreduce_small(v):
    rows = v.shape[0]

    def body(v_ref, o_ref, gath, send, recv):
        x, y, c = _place()
        me = 4 * x + 2 * y + c
        gath[pl.ds(me, 1)] = v_ref[...][None]
        cps = []
        for k in range(1, N_DEV):
            peer = (1 - x if k & 4 else x, 1 - y if k & 2 else y, 1 - c if k & 1 else c)
            cp = _remote(v_ref, gath.at[me], send.at[k - 1], recv.at[k - 1], peer)
            cp.start()
            cps.append(cp)
        for cp in cps:
            cp.wait_recv()
        for cp in cps:
            cp.wait_send()
        acc = gath[0]
        for d in range(1, N_DEV):
            acc = acc + gath[d]
        o_ref[...] = acc

    return pl.pallas_call(
        body,
        name="small_allreduce",
        in_specs=[_vmem()],
        out_specs=_vmem(),
        out_shape=jax.ShapeDtypeStruct(v.shape, f32),
        scratch_shapes=[pltpu.VMEM((N_DEV, rows, LANES), f32), pltpu.SemaphoreType.DMA((N_DEV - 1,)), pltpu.SemaphoreType.DMA((N_DEV - 1,))],
    )(v)


def _adamw_math(w, g, m, v):
    m = ADAM_B1 * m + (1.0 - ADAM_B1) * g
    v = ADAM_B2 * v + (1.0 - ADAM_B2) * (g * g)
    m_hat = m / (1.0 - ADAM_B1 ** ADAM_STEP)
    v_hat = v / (1.0 - ADAM_B2 ** ADAM_STEP)
    delta = -ADAM_LR * (m_hat / (jnp.sqrt(v_hat) + ADAM_EPS) + ADAM_WD * w)
    return delta, m, v


def _adamw(w, g, m, v, *, name):
    R, C = w.shape
    tr = _div(R, 128, SUBLANES)

    def body(w_ref, g_ref, m_ref, v_ref, go_ref, d_ref, mo_ref, vo_ref):
        gv = g_ref[...]
        d, mn, vn = _adamw_math(w_ref[...], gv, m_ref[...], v_ref[...])
        go_ref[...] = gv
        d_ref[...] = d
        mo_ref[...] = mn
        vo_ref[...] = vn

    spec = pl.BlockSpec((tr, C), lambda r: (r, 0))
    return pl.pallas_call(
        body,
        name=name,
        grid=(R // tr,),
        in_specs=[spec] * 4,
        out_specs=[spec] * 4,
        out_shape=[jax.ShapeDtypeStruct((R, C), f32)] * 4,
        compiler_params=pltpu.CompilerParams(dimension_semantics=("parallel",)),
    )(w, g, m, v)


def _adamw_sparsecore(w, g, m, v, *, name):
    R, C = w.shape
    cb = _div(C, 1408, LANES)
    ncb = C // cb
    per_tile = (R // SUBLANES) * ncb // SC_TILES
    assert R % SUBLANES == 0 and (R // SUBLANES) * ncb % SC_TILES == 0

    def body(w_hbm, g_hbm, m_hbm, v_hbm, go_hbm, d_hbm, mo_hbm, vo_hbm, wb, gb, mb, vb, db):
        tile = lax.axis_index("sc_tile") * 2 + lax.axis_index("sc_core")

        @pl.loop(0, per_tile)
        def _(t):
            n = tile * per_tile + t
            slab = (pl.ds((n // ncb) * SUBLANES, SUBLANES), pl.ds((n % ncb) * cb, cb))
            pltpu.sync_copy(w_hbm.at[slab], wb)
            pltpu.sync_copy(g_hbm.at[slab], gb)
            pltpu.sync_copy(m_hbm.at[slab], mb)
            pltpu.sync_copy(v_hbm.at[slab], vb)

            @pl.loop(0, SUBLANES)
            def _(rr):
                @pl.loop(0, cb, step=SC_LANES)
                def _(cc):
                    s = (pl.ds(rr, 1), pl.ds(cc, SC_LANES))
                    d, mn, vn = _adamw_math(wb.at[s][...], gb.at[s][...], mb.at[s][...], vb.at[s][...])
                    db.at[s][...] = d
                    mb.at[s][...] = mn
                    vb.at[s][...] = vn

            pltpu.sync_copy(gb, go_hbm.at[slab])
            pltpu.sync_copy(db, d_hbm.at[slab])
            pltpu.sync_copy(mb, mo_hbm.at[slab])
            pltpu.sync_copy(vb, vo_hbm.at[slab])

    buf = pltpu.VMEM((SUBLANES, cb), f32)
    return pl.kernel(
        body,
        name=name,
        out_type=[jax.ShapeDtypeStruct((R, C), f32)] * 4,
        mesh=plsc.VectorSubcoreMesh(core_axis_name="sc_core", subcore_axis_name="sc_tile"),
        scratch_types=[buf] * 5,
    )(w, g, m, v)


def _pack(arrs):
    flat = jnp.concatenate([a.reshape(-1).astype(f32) for a in arrs])
    tile = SUBLANES * LANES
    total = -(-flat.shape[0] // tile) * tile
    return jnp.pad(flat, (0, total - flat.shape[0])).reshape(total // LANES, LANES)


def _unpack(buf, shapes):
    flat = buf.reshape(-1)
    out, off = [], 0
    for s in shapes:
        size = int(np.prod(s))
        out.append(flat[off:off + size].reshape(s))
        off += size
    return out


def kernel(x, p, norm_mix, w_in, lb_logits, hg_norm, rel_bias, w_out, norm_ffn, w_up, conv_w, conv_b, w_down, norm_ple, w_ple_gate, w_ple_proj, final_norm, loss_target, m_norm_mix, m_w_in, m_lb_logits, m_hg_norm, m_rel_bias, m_w_out, m_norm_ffn, m_w_up, m_conv_w, m_conv_b, m_w_down, m_norm_ple, m_w_ple_gate, m_w_ple_proj, m_final_norm, v_norm_mix, v_w_in, v_lb_logits, v_hg_norm, v_rel_bias, v_w_out, v_norm_ffn, v_w_up, v_conv_w, v_conv_b, v_w_down, v_norm_ple, v_w_ple_gate, v_w_ple_proj, v_final_norm):
    S, D = x.shape[1], x.shape[2]
    xv, pv, tgt = x[0], p[0, 0], loss_target[0]
    W = (w_in.shape[2] * N_CHIPS) // 7
    H = W // HEAD_DIM
    F = w_down.shape[1] * N_CHIPS
    tc = _div(F // 2, 1408, LANES)
    nj = F // tc
    jx, jy = lax.axis_index("x"), lax.axis_index("y")
    chip = 2 * jx + jy

    big = [w_in[0], w_out[0], w_up[0], w_down[0], w_ple_gate[0], w_ple_proj[0]]
    big_axes = [1, 0, 1, 0, 0, 1]
    where = jnp.stack([chip, lax.axis_index("c")]).astype(jnp.int32)
    cw_pad = jnp.pad(conv_w[0], ((0, SUBLANES - conv_w.shape[1]), (0, 0)))
    cw_mine = lax.dynamic_update_slice(jnp.zeros((N_CHIPS * SUBLANES, cw_pad.shape[1]), f32), cw_pad, (chip * SUBLANES, 0))
    placed = [_cast_place(b, ax, where, name=f"cast_place_{i}") for i, (b, ax) in enumerate(zip(big, big_axes))]
    shard_shapes = [b.shape for b in big]

    every = (1, 2, 3)

    def job(buf, i, rels, hand_over=False, relay=False):
        if i is None:
            return _GatherJob(buf, cw_pad.shape, 0, False, rels, hand_over)
        return _GatherJob(buf, shard_shapes[i], big_axes[i], True, rels, hand_over, relay)

    tm = _div(S, 1024, LANES)

    a1 = _rms_fwd(xv, norm_mix, name="rms_mix")
    cs_in = shard_shapes[0][1]
    tn_in = _div(cs_in, 1024, LANES)
    proj, W_in = _mm_while_gathering(a1, placed[0], shard_shapes[0], where, tm=tm, tn=tn_in, name="mm_in")
    qt, pad, kb, ne = _att_dims(S)
    nrel = rel_bias.shape[2]
    nrel_pad = -(-nrel // LANES) * LANES
    onehot = _bias_onehot(qt, ne, nrel_pad)
    rb_pad = jnp.pad(rel_bias[0], ((0, 0), (0, nrel_pad - nrel)))[:, None, :]
    ext = _bias_ext(rb_pad, onehot)
    kp, vp = _kv_padded(proj, W, pad)
    y_att, W_out_i, W_up_i = _attn_fwd(
        proj, kp, vp, ext, W, rider=_gather_rider([placed[1], placed[2]], [job(0, 1, every), job(1, 2, (1, 2))]))
    y_hg, o_hg, st, W_out, W_up_i = _hgrn_fwd(
        proj, lb_logits, hg_norm, W,
        rider=_gather_rider([W_out_i, W_up_i], [job(0, 1, every, True), job(1, 2, (1, 2), True), job(1, 2, (3,), relay=True)]))
    ycat = jnp.concatenate([y_hg, y_att], axis=1)

    tn_d = _div(D, 512, LANES)
    h1, W_up = _mm(ycat, W_out, dims="nn", tm=tm, tn=tn_d, tk=D, out_dtype=f32, res=xv, name="mm_out",
                   rider=_gather_rider([W_up_i], [job(0, 2, (3,), True)]))
    a2 = _rms_fwd(h1, norm_ffn, name="rms_ffn")
    perm = lambda b: (b % 2) * nj + b // 2
    u, W_down_i, W_pg_i, W_pp_i, cw_all = _mm(
        a2, W_up, dims="nn", tm=tm, tn=tc, tk=D, out_dtype=f32, name="mm_up", b_idx=lambda i, j, k: (k, perm(j)),
        rider=_gather_rider([placed[3], placed[4], placed[5], cw_mine], [job(0, 3, every), job(1, 4, every), job(2, 5, every), job(3, None, every)]))
    ncw = conv_w.shape[1]
    cw_full = jnp.transpose(cw_all.reshape(N_CHIPS, SUBLANES, -1)[:, :ncw], (1, 0, 2)).reshape(ncw, -1)
    cb = conv_b
    mact, W_down, W_pg, W_pp = _convact_fwd(
        u, cw_full, cb, tc, rider=_gather_rider([W_down_i, W_pg_i, W_pp_i], [job(0, 3, every, True), job(1, 4, every, True), job(2, 5, every, True)]))
    tn_w = _div(D, 1024, LANES)
    h2 = _mm(mact, W_down, dims="nn", tm=_div(S, 512, LANES), tn=tn_w, tk=F, out_dtype=f32, res=h1, name="mm_down", cols_outer=True)
    a3 = _rms_fwd(h2, norm_ple, name="rms_ple")
    zg = _mm(a3, W_pg, dims="nn", tm=tm, tn=tn_d, tk=D, out_dtype=f32, name="mm_ple_gate")
    p16 = pv.astype(bf16)
    pp = _mm(p16, W_pp, dims="nn", tm=tm, tn=tn_d, tk=pv.shape[1], out_dtype=f32, name="mm_ple_proj")
    dh3, dzg, dpp, loss_part, d_fn = _tail(h2, zg, pp, tgt, final_norm.reshape(1, D))

    pair = {}

    def siblings(idx, grads):
        return _sibling_rider(grads, [shard_shapes[i] for i in idx], [big_axes[i] for i in idx])

    def pair_up(idx, grads, got):
        for i, d, g in zip(idx, grads, got):
            pair[i] = _pair_sum(d, g, shard_shapes[i], big_axes[i], where, name=f"grad_pair_sum_{i}")

    def reduced_half(i, landed):
        return _chip_sum(pair[i], landed, where, name=f"grad_chip_sum_{i}")

    tk_s = S
    dW_pp = _mm(p16, dpp, dims="tn", tm=pv.shape[1], tn=tn_w, tk=tk_s, out_dtype=bf16, name="mm_d_ple_proj")
    dW_pg = _mm(a3, dzg, dims="tn", tm=tn_w, tn=tn_w, tk=tk_s, out_dtype=bf16, name="mm_d_ple_gate")
    da3, *got = _mm(dzg, W_pg, dims="nt", tm=tm, tn=_div(D, 1024, LANES), tk=D, out_dtype=f32, name="mm_da3",
                    rider=siblings([4, 5], [dW_pg, dW_pp]))
    pair_up([4, 5], [dW_pg, dW_pp], got)
    dh2, dh2b, d_nple = _rms_bwd(h2, norm_ple, da3, dh3, name="rms_ple_bwd")
    dm, land4, land5 = _mm(dh2b, W_down, dims="nt", tm=tm, tn=_div(F, 512, LANES), tk=D, out_dtype=f32, name="mm_dm",
                           rider=_chips_rider([pair[4], pair[5]]))
    dW_down = _mm(mact, dh2b, dims="tn", tm=_div(F, 1408, LANES), tn=tn_d, tk=tk_s, out_dtype=bf16, name="mm_d_down")
    du, dcw_g, dcw_v, dcb_g, dcb_v, got3 = _convact_bwd(u, dm, cw_full, cb, tc, rider=siblings([3], [dW_down]))
    pair_up([3], [dW_down], [got3])
    dW_up, land3 = _mm(a2, du, dims="tn", tm=_div(D, 1024, LANES), tn=tc, tk=tk_s, out_dtype=bf16, name="mm_d_up",
                       o_idx=lambda i, j, k: (i, perm(j)), rider=_chips_rider([pair[3]]))
    da2, got2 = _mm(du, W_up, dims="nt", tm=tm, tn=D, tk=tc, out_dtype=f32, name="mm_da2",
                    b_idx=lambda i, j, k: (j, perm(k)), rider=siblings([2], [dW_up]))
    pair_up([2], [dW_up], [got2])
    dh1, dh1b, d_nffn = _rms_bwd(h1, norm_ffn, da2, dh2, name="rms_ffn_bwd")
    dycat = _mm(dh1b, W_out, dims="nt", tm=tm, tn=_div(D, 1024, LANES), tk=D, out_dtype=f32, name="mm_dycat")
    dW_out = _mm(ycat, dh1b, dims="tn", tm=tn_w, tn=tn_w, tk=tk_s, out_dtype=bf16, name="mm_d_out")
    dq_att, dk_att, dv_att, dbias, land2, got1 = _attn_bwd(
        proj, kp, vp, ext, dycat, W, rider=_both(_chips_rider([pair[2]], rels=(1, 2)), siblings([1], [dW_out])))
    pair_up([1], [dW_out], [got1])
    d_hg4, d_lbl, d_hgn, land2 = _hgrn_bwd(proj, o_hg, st, dycat, lb_logits, hg_norm, W,
                                           rider=_chips_rider([pair[2]], rels=(3,), landing=[land2]))
    d_rb = _bias_bwd(dbias, onehot)[:, 0, :nrel]
    dproj = [d_hg4, dq_att, dk_att, dv_att]
    tw = _div(W, 1024, LANES)
    early = [reduced_half(i, land) for i, land in ((2, land2), (3, land3), (4, land4), (5, land5))]
    dW_in, land1, g_up, g_down, g_pg, g_pp = _mm(a1, dproj, dims="tn", tm=tn_w, tn=_div(W, 512, LANES), tk=tk_s, out_dtype=bf16,
                                                 name="mm_d_in", rider=_both(_chips_rider([pair[1]]), _join_rider(early)))
    pair_up([0], [dW_in], _run_rider(siblings([0], [dW_in]), name="grads_to_sibling_0"))
    da1, land0, g_out = _mm(dproj, W_in, dims="nt", tm=tm, tn=D, tk=tw, out_dtype=f32, name="mm_da1",
                            rider=_both(_chips_rider([pair[0]]), _join_rider([reduced_half(1, land1)])))
    grad_x, _, d_nmix = _rms_bwd(xv, norm_mix, da1, dh1, name="rms_mix_bwd")
    (g_in,) = _run_rider(_join_rider([reduced_half(0, land0)]), name="grads_join_w_in")
    g_big = [g_in, g_out, g_up, g_down, g_pg, g_pp]

    d_cw = jnp.concatenate([dcw_g, dcw_v], axis=1)
    d_cb = jnp.concatenate([dcb_g, dcb_v], axis=1)
    small_parts = [loss_part[:, :1], d_nmix, d_lbl, d_hgn, d_rb, d_nffn, d_cw, d_cb, d_nple, d_fn]
    small_shapes = [(1, 1), (1, D), lb_logits.shape, hg_norm.shape, (H, nrel), (1, D), (ncw, 2 * F), (1, 2 * F), (1, D), (1, D)]
    red = _unpack(_allreduce_small(_pack(small_parts)), small_shapes)
    loss = red[0].reshape(())
    g_nmix, g_lbl, g_hgn, g_rb, g_nffn, g_cw_all, g_cb, g_nple, g_fn = red[1:]
    csh = conv_w.shape[2]
    g_cw = lax.dynamic_slice(g_cw_all, (0, chip * csh), (ncw, csh))
    small_g = [g_nmix, g_lbl, g_hgn, g_rb[None], g_nffn, g_cw[None], g_cb, g_nple, g_fn.reshape(D)]
    small_w = [norm_mix, lb_logits, hg_norm, rel_bias, norm_ffn, conv_w, conv_b, norm_ple, final_norm]
    small_m = [m_norm_mix, m_lb_logits, m_hg_norm, m_rel_bias, m_norm_ffn, m_conv_w, m_conv_b, m_norm_ple, m_final_norm]
    small_v = [v_norm_mix, v_lb_logits, v_hg_norm, v_rel_bias, v_norm_ffn, v_conv_w, v_conv_b, v_norm_ple, v_final_norm]
    shapes_s = [w.shape for w in small_w]
    _, sd, sm, sv = _adamw(_pack(small_w), _pack(small_g), _pack(small_m), _pack(small_v), name="adamw_small")
    small_g = [g.reshape(s) for g, s in zip(small_g, shapes_s)]
    small_d, small_nm, small_nv = _unpack(sd, shapes_s), _unpack(sm, shapes_s), _unpack(sv, shapes_s)

    big_m = [m_w_in[0], m_w_out[0], m_w_up[0], m_w_down[0], m_w_ple_gate[0], m_w_ple_proj[0]]
    big_v = [v_w_in[0], v_w_out[0], v_w_up[0], v_w_down[0], v_w_ple_gate[0], v_w_ple_proj[0]]
    big_g, big_d, big_nm, big_nv = [], [], [], []
    for i in range(6):
        update = _adamw_sparsecore if i in (2, 3, 4, 5) else _adamw
        g_, d_, m_, v_ = update(big[i], g_big[i], big_m[i], big_v[i], name=f"adamw_{i}")
        big_g.append(g_[None])
        big_d.append(d_[None])
        big_nm.append(m_[None])
        big_nv.append(v_[None])
    g_big = big_g

    def order(sm_list, bg_list):
        s, b = sm_list, bg_list
        return [s[0], b[0], s[1], s[2], s[3], b[1], s[4], b[2], s[5], s[6], b[3], s[7], b[4], b[5], s[8]]

    return (loss, grad_x[None], *order(small_g, g_big), *order(small_d, big_d), *order(small_nm, big_nm), *order(small_nv, big_nv))
```
